```python
import math
import jax, jax.numpy as jnp
from jax import lax
import numpy as np

D_MODEL = 1024
BATCH = 16
SEQ = 256
DEPTH = 4
DEC_BATCH = 2
DEC_SEQ = 2048
PAST_LEN = 512

GRID_W = 64
N_MIXERS = 2
N_ATTN_LAYERS = (DEPTH + 1) // 2
N_FOURIER_LAYERS = DEPTH // 2
N_HEADS = 8
HEAD_DIM = 64
V_DIM = 2 * HEAD_DIM
Q_WIDTH = N_HEADS * 2 * HEAD_DIM
K_WIDTH = N_HEADS * 2 * HEAD_DIM
V_WIDTH = N_HEADS * V_DIM
QKV_WIDTH = Q_WIDTH + K_WIDTH + V_WIDTH
ROPE_BASE = 10000.0
Q_BLOCK = 128
N_FOURIER_GROUPS = 8
FOURIER_GROUP_DIM = D_MODEL // N_FOURIER_GROUPS
N_EXPERT_GROUPS = 4
EXPERTS_PER_GROUP = 8
N_EXPERTS = N_EXPERT_GROUPS * EXPERTS_PER_GROUP
D_EXPERT = 256
TOP_K_IN_GROUP = 2
N_MOD = 6
LN_EPS = 1e-5
DEEPNORM_ALPHA = (2.0 * DEPTH) ** 0.25
DEEPNORM_BETA = (8.0 * DEPTH) ** -0.25

kernel_name = "hybrid_diffattn_fnet_hmoe_diffusion_step"


def _layernorm(x):
    xf = x.astype(jnp.float32)
    mu = jnp.mean(xf, axis=-1, keepdims=True)
    var = jnp.mean(jnp.square(xf - mu), axis=-1, keepdims=True)
    return ((xf - mu) * lax.rsqrt(var + LN_EPS)).astype(x.dtype)


def _modulate(x, shift, scale):
    return _layernorm(x) * (1 + scale) + shift


def _post_norm(x, out, gain, bias):
    return _layernorm(DEEPNORM_ALPHA * x + out) * gain + bias


def _ada(cond, w_ada, b_ada):
    m = (jax.nn.silu(cond) @ w_ada + b_ada)[..., None, :]
    return jnp.split(m, N_MOD, axis=-1)


def _rope_2d_tables(n_rows):
    nf = HEAD_DIM // 4
    rows = jnp.repeat(jnp.arange(n_rows, dtype=jnp.float32), GRID_W)
    cols = jnp.tile(jnp.arange(GRID_W, dtype=jnp.float32), n_rows)
    inv = ROPE_BASE ** (-jnp.arange(nf, dtype=jnp.float32) / nf)
    ang = jnp.stack([rows[:, None] * inv, cols[:, None] * inv], axis=1)
    return jnp.cos(ang), jnp.sin(ang)


def _apply_rope_2d(x, cos, sin):
    shp = x.shape
    nf = HEAD_DIM // 4
    xa = x.reshape(shp[:-1] + (2, HEAD_DIM // 2))
    x1, x2 = xa[..., :nf], xa[..., nf:]
    c = cos[None, :, None, None].astype(x.dtype)
    s = sin[None, :, None, None].astype(x.dtype)
    out = jnp.concatenate([x1 * c - x2 * s, x1 * s + x2 * c], axis=-1)
    return out.reshape(shp)


def _diff_lambda(lq1, lk1, lq2, lk2, lam_init):
    e1 = jnp.exp(jnp.sum(lq1.astype(jnp.float32) * lk1.astype(jnp.float32)))
    e2 = jnp.exp(jnp.sum(lq2.astype(jnp.float32) * lk2.astype(jnp.float32)))
    return e1 - e2 + lam_init


def _qkv(h, w_qkv):
    b, s, _ = h.shape
    qkv = h @ w_qkv
    q = qkv[..., :Q_WIDTH].reshape(b, s, N_HEADS, 2, HEAD_DIM)
    k = qkv[..., Q_WIDTH:Q_WIDTH + K_WIDTH].reshape(b, s, N_HEADS, 2, HEAD_DIM)
    v = qkv[..., Q_WIDTH + K_WIDTH:].reshape(b, s, N_HEADS, V_DIM)
    return q, k, v


def _diff_attention(q, k, v, lam):
    b, sq = q.shape[:2]
    nb = sq // Q_BLOCK
    qb = q.reshape(b, nb, Q_BLOCK, N_HEADS, 2, HEAD_DIM).transpose(1, 0, 2, 3, 4, 5)
    scale = HEAD_DIM ** -0.5

    def block(qblk):
        s = jnp.einsum('bqhjd,bkhjd->jbhqk', qblk, k).astype(jnp.float32) * scale
        p = jax.nn.softmax(s, axis=-1)
        pd = (p[0] - lam * p[1]).astype(v.dtype)
        return jnp.einsum('bhqk,bkhv->bqhv', pd, v)

    out = lax.map(block, qb)
    return out.transpose(1, 0, 2, 3, 4).reshape(b, sq, N_HEADS, V_DIM)


def _attn_output(o, gain, lam_init, w_o):
    b, s = o.shape[:2]
    of = o.astype(jnp.float32)
    of = of * lax.rsqrt(jnp.mean(jnp.square(of), axis=-1, keepdims=True) + LN_EPS)
    o = of.astype(o.dtype) * gain * (1.0 - lam_init)
    return o.reshape(b, s, D_MODEL) @ w_o


def _fourier_mix(h, w_out):
    b, s, _ = h.shape
    hg = h.reshape(b, s, N_FOURIER_GROUPS, FOURIER_GROUP_DIM).astype(jnp.float32)
    f = jnp.fft.fft2(hg, axes=(1, 3), norm="ortho").real
    return f.reshape(b, s, D_MODEL).astype(h.dtype) @ w_out


def _hier_moe(h, w_rg, w_re, w_g, w_u, w_d):
    b, s, d = h.shape
    t = h.reshape(b * s, d)
    pg = jax.nn.softmax((t @ w_rg).astype(jnp.float32), axis=-1)
    g_prob, g_idx = lax.top_k(pg, 1)
    le = (t @ w_re).astype(jnp.float32).reshape(-1, N_EXPERT_GROUPS, EXPERTS_PER_GROUP)
    g_onehot = jax.nn.one_hot(g_idx[:, 0], N_EXPERT_GROUPS, dtype=jnp.float32)
    le_sel = jnp.einsum('nge,ng->ne', le, g_onehot)
    pe = jax.nn.softmax(le_sel, axis=-1)
    e_prob, e_idx = lax.top_k(pe, TOP_K_IN_GROUP)
    e_w = e_prob / jnp.sum(e_prob, axis=-1, keepdims=True) * g_prob
    e_global = g_idx * EXPERTS_PER_GROUP + e_idx
    combine = jnp.sum(jax.nn.one_hot(e_global, N_EXPERTS, dtype=jnp.float32) * e_w[..., None], axis=1)
    a = jnp.einsum('nd,edf->nef', t, w_g)
    u = jnp.einsum('nd,edf->nef', t, w_u)
    act = jax.nn.silu(a) * u * combine[..., None].astype(t.dtype)
    y = jnp.einsum('nef,efd->nd', act, w_d)
    return y.reshape(b, s, d)


def setup_inputs(seed: int = 0) -> dict:
    key = jax.random.key(seed)
    ks = jax.random.split(key, 24)
    f32 = jnp.float32
    d = D_MODEL

    def nrm(k, shape, scale):
        return jax.random.normal(k, shape, f32) * scale

    return {
        "x_prompt": nrm(ks[0], (BATCH, SEQ, d), 1.0),
        "x_sample": nrm(ks[1], (DEC_BATCH, DEC_SEQ, d), 1.0),
        "cache_k": nrm(ks[2], (DEC_BATCH, N_ATTN_LAYERS, PAST_LEN, N_HEADS, 2 * HEAD_DIM), 1.0),
        "cache_v": nrm(ks[3], (DEC_BATCH, N_ATTN_LAYERS, PAST_LEN, N_HEADS, V_DIM), 1.0),
        "c": nrm(ks[4], (DEC_BATCH, d), 1.0),
        "c_ctx": nrm(ks[5], (d,), 1.0),
        "w_ada": nrm(ks[6], (DEPTH, d, N_MOD * d), d ** -0.5),
        "b_ada": nrm(ks[7], (DEPTH, N_MOD * d), 0.02),
        "ln_gain": 1.0 + nrm(ks[8], (DEPTH, 2, d), 0.02),
        "ln_bias": nrm(ks[9], (DEPTH, 2, d), 0.02),
        "w_qkv": nrm(ks[10], (N_ATTN_LAYERS, d, QKV_WIDTH), d ** -0.5),
        "w_attn_out": nrm(ks[11], (N_ATTN_LAYERS, V_WIDTH, d), V_WIDTH ** -0.5 * DEEPNORM_BETA),
        "lambda_q1": nrm(ks[12], (N_ATTN_LAYERS, HEAD_DIM), 0.1),
        "lambda_k1": nrm(ks[13], (N_ATTN_LAYERS, HEAD_DIM), 0.1),
        "lambda_q2": nrm(ks[14], (N_ATTN_LAYERS, HEAD_DIM), 0.1),
        "lambda_k2": nrm(ks[15], (N_ATTN_LAYERS, HEAD_DIM), 0.1),
        "subln_gain": 1.0 + nrm(ks[16], (N_ATTN_LAYERS, V_DIM), 0.02),
        "w_fourier_out": nrm(ks[17], (N_FOURIER_LAYERS, d, d), d ** -0.5 * DEEPNORM_BETA),
        "w_router_group": nrm(ks[18], (DEPTH, d, N_EXPERT_GROUPS), d ** -0.5),
        "w_router_expert": nrm(ks[19], (DEPTH, d, N_EXPERTS), d ** -0.5),
        "w_expert_gate": nrm(ks[20], (DEPTH, N_EXPERTS, d, D_EXPERT), d ** -0.5),
        "w_expert_up": nrm(ks[21], (DEPTH, N_EXPERTS, d, D_EXPERT), d ** -0.5),
        "w_expert_down": nrm(ks[22], (DEPTH, N_EXPERTS, D_EXPERT, d), D_EXPERT ** -0.5 * DEEPNORM_BETA),
    }


def reference(x_prompt, x_sample, cache_k, cache_v, c, c_ctx, w_ada, b_ada, ln_gain, ln_bias,
              w_qkv, w_attn_out, lambda_q1, lambda_k1, lambda_q2, lambda_k2, subln_gain,
              w_fourier_out, w_router_group, w_router_expert, w_expert_gate, w_expert_up,
              w_expert_down):
    x = x_prompt
    bp, sp, _ = x.shape
    k_ctx_list, v_ctx_list = [], []
    for i in range(DEPTH):
        sh_a, sc_a, g_a, sh_f, sc_f, g_f = _ada(c_ctx, w_ada[i], b_ada[i])
        h = _modulate(x, sh_a, sc_a)
        if i % N_MIXERS == 0:
            a = i // N_MIXERS
            lam_init = 0.8 - 0.6 * math.exp(-0.3 * i)
            lam = _diff_lambda(lambda_q1[a], lambda_k1[a], lambda_q2[a], lambda_k2[a], lam_init)
            q, k, v = _qkv(h, w_qkv[a])
            k_ctx_list.append(k.reshape(bp, sp, N_HEADS, 2 * HEAD_DIM))
            v_ctx_list.append(v)
            o = _diff_attention(q, k, v, lam)
            out = _attn_output(o, subln_gain[a], lam_init, w_attn_out[a])
        else:
            out = _fourier_mix(h, w_fourier_out[i // N_MIXERS])
        x = _post_norm(x, g_a * out, ln_gain[i, 0], ln_bias[i, 0])
        h = _modulate(x, sh_f, sc_f)
        moe = _hier_moe(h, w_router_group[i], w_router_expert[i], w_expert_gate[i],
                        w_expert_up[i], w_expert_down[i])
        x = _post_norm(x, g_f * moe, ln_gain[i, 1], ln_bias[i, 1])
    y_prompt = x
    new_cache_k = jnp.stack(k_ctx_list, axis=1)
    new_cache_v = jnp.stack(v_ctx_list, axis=1)

    x = x_sample
    bs, n_lat, _ = x.shape
    n_rows = n_lat // GRID_W
    cos, sin = _rope_2d_tables(n_rows)
    n_ctx = cache_k.shape[2]
    for i in range(DEPTH):
        sh_a, sc_a, g_a, sh_f, sc_f, g_f = _ada(c, w_ada[i], b_ada[i])
        h = _modulate(x, sh_a, sc_a)
        if i % N_MIXERS == 0:
            a = i // N_MIXERS
            lam_init = 0.8 - 0.6 * math.exp(-0.3 * i)
            lam = _diff_lambda(lambda_q1[a], lambda_k1[a], lambda_q2[a], lambda_k2[a], lam_init)
            q, k, v = _qkv(h, w_qkv[a])
            q = _apply_rope_2d(q, cos, sin)
            k = _apply_rope_2d(k, cos, sin)
            k_ctx = cache_k[:, a].reshape(bs, n_ctx, N_HEADS, 2, HEAD_DIM)
            k_all = jnp.concatenate([k, k_ctx.astype(k.dtype)], axis=1)
            v_all = jnp.concatenate([v, cache_v[:, a].astype(v.dtype)], axis=1)
            o = _diff_attention(q, k_all, v_all, lam)
            out = _attn_output(o, subln_gain[a], lam_init, w_attn_out[a])
        else:
            out = _fourier_mix(h, w_fourier_out[i // N_MIXERS])
        x = _post_norm(x, g_a * out, ln_gain[i, 0], ln_bias[i, 0])
        h = _modulate(x, sh_f, sc_f)
        moe = _hier_moe(h, w_router_group[i], w_router_expert[i], w_expert_gate[i],
                        w_expert_up[i], w_expert_down[i])
        x = _post_norm(x, g_f * moe, ln_gain[i, 1], ln_bias[i, 1])
    y_sample = x
    return (y_prompt, y_sample, new_cache_k, new_cache_v)
```

```python
import functools
import math

import numpy as np
import jax
import jax.numpy as jnp
from jax import lax
from jax.experimental import pallas as pl
from jax.experimental.pallas import tpu as pltpu

F32 = jnp.float32
BF16 = jnp.bfloat16

D_MODEL = 1024
DEPTH = 4
GRID_W = 64
N_HEADS = 8
HEAD_DIM = 64
V_DIM = 2 * HEAD_DIM
ROPE_BASE = 10000.0
N_FOURIER_GROUPS = 8
FOURIER_GROUP_DIM = D_MODEL // N_FOURIER_GROUPS
N_EXPERT_GROUPS = 4
EXPERTS_PER_GROUP = 8
N_EXPERTS = N_EXPERT_GROUPS * EXPERTS_PER_GROUP
D_EXPERT = 256
N_MOD = 6
LN_EPS = 1e-5
DEEPNORM_ALPHA = (2.0 * DEPTH) ** 0.25

LANES = 128
TOKEN_TILE = 256
Q_TILE = 256
MOE_TILE = 256
GATHER_ROWS_PER_STEP = 512
VMEM_LIMIT = 48 * 1024 * 1024


def _params(semantics):
    return pltpu.CompilerParams(dimension_semantics=semantics, vmem_limit_bytes=VMEM_LIMIT)


def _layernorm(x):
    mu = jnp.mean(x, axis=-1, keepdims=True)
    xc = x - mu
    var = jnp.mean(xc * xc, axis=-1, keepdims=True)
    return xc * lax.rsqrt(var + LN_EPS)


def _silu(a):
    return a / (1.0 + jnp.exp(-a))


def _dot(a, b):
    return jnp.dot(a, b, preferred_element_type=F32)


def _ada_kernel(cond_ref, w_ref, b_ref, o_ref):
    a = _silu(cond_ref[...])
    o_ref[...] = _dot(a.astype(BF16), w_ref[...].astype(BF16)) + b_ref[...]


def _ada_all(cond, w_ada, b_ada):
    depth, d, n = w_ada.shape
    tn = n // 4
    return pl.pallas_call(
        _ada_kernel,
        out_shape=jax.ShapeDtypeStruct((depth, cond.shape[0], n), F32),
        grid=(depth, n // tn),
        in_specs=[
            pl.BlockSpec(cond.shape, lambda l, j: (0, 0)),
            pl.BlockSpec((None, d, tn), lambda l, j: (l, 0, j)),
            pl.BlockSpec((None, 1, tn), lambda l, j: (l, 0, j)),
        ],
        out_specs=pl.BlockSpec((None, cond.shape[0], tn), lambda l, j: (l, 0, j)),
        compiler_params=_params(("arbitrary", "arbitrary")),
        name="ada",
    )(cond, w_ada, b_ada.reshape(depth, 1, n))


def _rope(x, cos, sin_signed, first_half):
    outs = []
    for c in range(x.shape[1] // LANES):
        xc = x[:, c * LANES:(c + 1) * LANES]
        partner = jnp.where(first_half, pltpu.roll(xc, LANES - 16, 1), pltpu.roll(xc, 16, 1))
        outs.append(xc * cos + partner * sin_signed)
    return jnp.concatenate(outs, axis=1)


def _qkv_kernel(x_ref, mod_ref, w_ref, cos_ref, sin_ref, q_ref, k_ref, v_ref, kf_ref, vf_ref, *,
                n_prompt_tiles):
    t = pl.program_id(0)
    h = _layernorm(x_ref[...]) * (1.0 + mod_ref[1:2, :]) + mod_ref[0:1, :]
    acc = _dot(h.astype(BF16), w_ref[...])
    d = x_ref.shape[1]
    q = acc[:, :d] * (HEAD_DIM ** -0.5)
    k = acc[:, d:2 * d]
    v = acc[:, 2 * d:]
    v_ref[...] = v.astype(BF16)

    @pl.when(t < n_prompt_tiles)
    def _():
        q_ref[...] = q.astype(BF16)
        k_ref[...] = k.astype(BF16)
        kf_ref[...] = k
        vf_ref[...] = v

    @pl.when(t >= n_prompt_tiles)
    def _():
        lane = lax.broadcasted_iota(jnp.int32, (x_ref.shape[0], LANES), 1)
        first_half = (lane % 32) < 16
        cos = cos_ref[...]
        sin = sin_ref[...]
        q_ref[...] = _rope(q, cos, sin, first_half).astype(BF16)
        k_ref[...] = _rope(k, cos, sin, first_half).astype(BF16)


def _cond_index(t, n_prompt_tiles, tiles_per_sample):
    return jnp.where(t < n_prompt_tiles, 0, (t - n_prompt_tiles) // tiles_per_sample + 1)


def _qkv(x, mod, w_qkv, cos, sin, n_prompt, dec_seq):
    t_tok, d = x.shape
    tm = TOKEN_TILE
    npt = n_prompt // tm
    tps = dec_seq // tm
    cond = functools.partial(_cond_index, n_prompt_tiles=npt, tiles_per_sample=tps)
    row = lambda t: (t, 0)
    pos = lambda t: (jnp.maximum(t - npt, 0) % tps, 0)
    prm = lambda t: (jnp.minimum(t, npt - 1), 0)
    return pl.pallas_call(
        functools.partial(_qkv_kernel, n_prompt_tiles=npt),
        out_shape=(
            jax.ShapeDtypeStruct((t_tok, d), BF16),
            jax.ShapeDtypeStruct((t_tok, d), BF16),
            jax.ShapeDtypeStruct((t_tok, d), BF16),
            jax.ShapeDtypeStruct((n_prompt, d), F32),
            jax.ShapeDtypeStruct((n_prompt, d), F32),
        ),
        grid=(t_tok // tm,),
        in_specs=[
            pl.BlockSpec((tm, d), row),
            pl.BlockSpec((None, N_MOD, d), lambda t: (cond(t), 0, 0)),
            pl.BlockSpec(w_qkv.shape, lambda t: (0, 0)),
            pl.BlockSpec((tm, LANES), pos),
            pl.BlockSpec((tm, LANES), pos),
        ],
        out_specs=(
            pl.BlockSpec((tm, d), row),
            pl.BlockSpec((tm, d), row),
            pl.BlockSpec((tm, d), row),
            pl.BlockSpec((tm, d), prm),
            pl.BlockSpec((tm, d), prm),
        ),
        compiler_params=_params(("arbitrary",)),
        name="ln_qkv_rope",
    )(x, mod, w_qkv, cos, sin)


def _chan_dft_kernel(x_ref, mod_ref, dcs_ref, xc_ref, xs_ref):
    h = (_layernorm(x_ref[...]) * (1.0 + mod_ref[1:2, :]) + mod_ref[0:1, :]).astype(BF16)
    g = FOURIER_GROUP_DIM
    for i in range(N_FOURIER_GROUPS):
        r = _dot(h[:, i * g:(i + 1) * g], dcs_ref[...])
        xc_ref[:, i * g:(i + 1) * g] = r[:, :g].astype(BF16)
        xs_ref[:, i * g:(i + 1) * g] = r[:, g:].astype(BF16)


def _chan_dft(x, mod, dcs, n_prompt, dec_seq):
    t_tok, d = x.shape
    tm = TOKEN_TILE
    cond = functools.partial(_cond_index, n_prompt_tiles=n_prompt // tm, tiles_per_sample=dec_seq // tm)
    row = lambda t: (t, 0)
    return pl.pallas_call(
        _chan_dft_kernel,
        out_shape=(jax.ShapeDtypeStruct((t_tok, d), BF16), jax.ShapeDtypeStruct((t_tok, d), BF16)),
        grid=(t_tok // tm,),
        in_specs=[
            pl.BlockSpec((tm, d), row),
            pl.BlockSpec((None, N_MOD, d), lambda t: (cond(t), 0, 0)),
            pl.BlockSpec(dcs.shape, lambda t: (0, 0)),
        ],
        out_specs=(pl.BlockSpec((tm, d), row), pl.BlockSpec((tm, d), row)),
        compiler_params=_params(("arbitrary",)),
        name="ln_chan_dft",
    )(x, mod, dcs)


def _seq_dft_kernel(cs_ref, ss_ref, xc_ref, xs_ref, o_ref, *, norm):
    f = _dot(cs_ref[...], xc_ref[...]) - _dot(ss_ref[...], xs_ref[...])
    o_ref[...] = (f * norm).astype(BF16)


def _seq_dft(cs, ss, xc, xs, batch, seq, row_offset):
    d = xc.shape[1]
    tm = min(TOKEN_TILE, seq)
    spt = seq // tm
    off_seq = row_offset // seq
    return pl.pallas_call(
        functools.partial(_seq_dft_kernel, norm=1.0 / math.sqrt(seq * FOURIER_GROUP_DIM)),
        out_shape=jax.ShapeDtypeStruct((batch * seq, d), BF16),
        grid=(batch, spt),
        in_specs=[
            pl.BlockSpec((tm, seq), lambda b, i: (i, 0)),
            pl.BlockSpec((tm, seq), lambda b, i: (i, 0)),
            pl.BlockSpec((seq, d), lambda b, i: (off_seq + b, 0)),
            pl.BlockSpec((seq, d), lambda b, i: (off_seq + b, 0)),
        ],
        out_specs=pl.BlockSpec((tm, d), lambda b, i: (b * spt + i, 0)),
        compiler_params=_params(("arbitrary", "arbitrary")),
        name=f"seq_dft_{seq}",
    )(cs, ss, xc, xs)


def _attn_kernel(lam_ref, gain_ref, q_ref, k_ref, v_ref, *rest, lam_init, heads, has_ctx):
    if has_ctx:
        kc_ref, vc_ref = rest[0], rest[1]
    o_ref = rest[-1]
    lv = lam_ref[...]
    lam = (jnp.exp(jnp.sum(lv[0:1] * lv[1:2], axis=-1, keepdims=True))
           - jnp.exp(jnp.sum(lv[2:3] * lv[3:4], axis=-1, keepdims=True)) + lam_init)
    gain = gain_ref[...]
    tq = q_ref.shape[0]
    lane = lax.broadcasted_iota(jnp.int32, (tq, V_DIM), 1)
    nt = (((1,), (1,)), ((), ()))
    for hd in range(heads):
        cols = slice(hd * V_DIM, (hd + 1) * V_DIM)
        q = q_ref[:, cols]
        q1 = jnp.where(lane < HEAD_DIM, q, jnp.zeros_like(q))
        q2 = jnp.where(lane >= HEAD_DIM, q, jnp.zeros_like(q))
        k = k_ref[:, cols]
        s1 = lax.dot_general(q1, k, nt, preferred_element_type=F32)
        s2 = lax.dot_general(q2, k, nt, preferred_element_type=F32)
        m1 = jnp.max(s1, axis=-1, keepdims=True)
        m2 = jnp.max(s2, axis=-1, keepdims=True)
        if has_ctx:
            kc = kc_ref[:, cols]
            c1 = lax.dot_general(q1, kc, nt, preferred_element_type=F32)
            c2 = lax.dot_general(q2, kc, nt, preferred_element_type=F32)
            m1 = jnp.maximum(m1, jnp.max(c1, axis=-1, keepdims=True))
            m2 = jnp.maximum(m2, jnp.max(c2, axis=-1, keepdims=True))
        p1 = jnp.exp(s1 - m1)
        p2 = jnp.exp(s2 - m2)
        l1 = jnp.sum(p1, axis=-1, keepdims=True)
        l2 = jnp.sum(p2, axis=-1, keepdims=True)
        if has_ctx:
            pc1 = jnp.exp(c1 - m1)
            pc2 = jnp.exp(c2 - m2)
            l1 = l1 + jnp.sum(pc1, axis=-1, keepdims=True)
            l2 = l2 + jnp.sum(pc2, axis=-1, keepdims=True)
        r1 = 1.0 / l1
        r2 = lam / l2
        o = _dot((p1 * r1 - p2 * r2).astype(BF16), v_ref[:, cols])
        if has_ctx:
            o = o + _dot((pc1 * r1 - pc2 * r2).astype(BF16), vc_ref[:, cols])
        o = o * lax.rsqrt(jnp.mean(o * o, axis=-1, keepdims=True) + LN_EPS)
        o_ref[:, cols] = (o * gain * (1.0 - lam_init)).astype(BF16)


def _attn_prompt(lam_vecs, gain, q, k, v, batch, seq, lam_init):
    d = q.shape[1]
    blk = pl.BlockSpec((seq, d), lambda b: (b, 0))
    return pl.pallas_call(
        functools.partial(_attn_kernel, lam_init=lam_init, heads=N_HEADS, has_ctx=False),
        out_shape=jax.ShapeDtypeStruct((batch * seq, d), BF16),
        grid=(batch,),
        in_specs=[
            pl.BlockSpec(lam_vecs.shape, lambda b: (0, 0)),
            pl.BlockSpec(gain.shape, lambda b: (0, 0)),
            blk, blk, blk,
        ],
        out_specs=blk,
        compiler_params=_params(("arbitrary",)),
        name="diff_attn_ctx",
    )(lam_vecs, gain, q, k, v)


def _attn_sample(lam_vecs, gain, q, k, v, kc, vc, batch, seq, n_ctx, row_offset, lam_init):
    d = q.shape[1]
    tq = Q_TILE
    qpt = seq // tq
    off_seq = row_offset // seq
    off_tile = row_offset // tq
    qmap = lambda b, h, i: (off_tile + b * qpt + i, h)
    kmap = lambda b, h, i: (off_seq + b, h)
    cmap = lambda b, h, i: (b, h)
    return pl.pallas_call(
        functools.partial(_attn_kernel, lam_init=lam_init, heads=1, has_ctx=True),
        out_shape=jax.ShapeDtypeStruct((batch * seq, d), BF16),
        grid=(batch, N_HEADS, qpt),
        in_specs=[
            pl.BlockSpec(lam_vecs.shape, lambda b, h, i: (0, 0)),
            pl.BlockSpec(gain.shape, lambda b, h, i: (0, 0)),
            pl.BlockSpec((tq, V_DIM), qmap),
            pl.BlockSpec((seq, V_DIM), kmap),
            pl.BlockSpec((seq, V_DIM), kmap),
            pl.BlockSpec((n_ctx, V_DIM), cmap),
            pl.BlockSpec((n_ctx, V_DIM), cmap),
        ],
        out_specs=pl.BlockSpec((tq, V_DIM), lambda b, h, i: (b * qpt + i, h)),
        compiler_params=_params(("arbitrary", "arbitrary", "arbitrary")),
        name="diff_attn_latent",
    )(lam_vecs, gain, q, k, v, kc, vc)


def _route(lg):
    lane = lax.broadcasted_iota(jnp.int32, lg.shape, 1)
    lane_f = lane.astype(F32)
    neg = jnp.float32(-jnp.inf)
    big = jnp.float32(LANES)
    gl = jnp.where(lane < N_EXPERT_GROUPS, lg, neg)
    gmax = jnp.max(gl, axis=-1, keepdims=True)
    g_prob = 1.0 / jnp.sum(jnp.exp(gl - gmax), axis=-1, keepdims=True)
    g_idx = jnp.min(jnp.where(gl == gmax, lane_f, big), axis=-1, keepdims=True)
    lo = N_EXPERT_GROUPS + EXPERTS_PER_GROUP * g_idx
    el = jnp.where((lane_f >= lo) & (lane_f < lo + EXPERTS_PER_GROUP), lg, neg)
    m1 = jnp.max(el, axis=-1, keepdims=True)
    i1 = jnp.min(jnp.where(el == m1, lane_f, big), axis=-1, keepdims=True)
    el2 = jnp.where(lane_f == i1, neg, el)
    m2 = jnp.max(el2, axis=-1, keepdims=True)
    i2 = jnp.min(jnp.where(el2 == m2, lane_f, big), axis=-1, keepdims=True)
    t = jnp.exp(m2 - m1)
    w1 = g_prob / (1.0 + t)
    w2 = g_prob * t / (1.0 + t)
    out = jnp.where(lane == 0, i1 - N_EXPERT_GROUPS, 0.0)
    out = jnp.where(lane == 1, i2 - N_EXPERT_GROUPS, out)
    out = jnp.where(lane == 2, w1, out)
    out = jnp.where(lane == 3, w2, out)
    return out


def _pack_bf16_pairs(h):
    n = h.shape[1] // 2
    bits = lax.bitcast_convert_type(h.astype(BF16).astype(F32), jnp.uint32)
    return (bits[:, :n] >> 16) | bits[:, n:]


def _unpack_bf16_pairs(p):
    lo = lax.bitcast_convert_type(p << 16, F32)
    hi = lax.bitcast_convert_type(p & jnp.uint32(0xFFFF0000), F32)
    return jnp.concatenate([lo, hi], axis=1).astype(BF16)


def _mix_out_kernel(ap_ref, as_ref, w_ref, x_ref, mod_ref, gb_ref, wr_ref, x1_ref, hp_ref, route_ref, *,
                    n_prompt_tiles):
    a = jnp.where(pl.program_id(0) < n_prompt_tiles, ap_ref[...], as_ref[...])
    out = _dot(a, w_ref[...])
    x1 = _layernorm(DEEPNORM_ALPHA * x_ref[...] + mod_ref[2:3, :] * out) * gb_ref[0:1, :] + gb_ref[1:2, :]
    x1_ref[...] = x1
    h2 = _layernorm(x1) * (1.0 + mod_ref[4:5, :]) + mod_ref[3:4, :]
    hp_ref[...] = _pack_bf16_pairs(h2)
    hi = h2.astype(BF16)
    lo = (h2 - hi.astype(F32)).astype(BF16)
    logits = _dot(hi, wr_ref[0]) + _dot(lo, wr_ref[0]) + _dot(hi, wr_ref[1])
    route_ref[...] = _route(logits)


def _mix_out(a_prompt, a_sample, w, x, mod, gb, wr, n_prompt, dec_seq):
    t_tok, d = x.shape
    tm = TOKEN_TILE
    npt = n_prompt // tm
    cond = functools.partial(_cond_index, n_prompt_tiles=npt, tiles_per_sample=dec_seq // tm)
    row = lambda t: (t, 0)
    return pl.pallas_call(
        functools.partial(_mix_out_kernel, n_prompt_tiles=npt),
        out_shape=(
            jax.ShapeDtypeStruct((t_tok, d), F32),
            jax.ShapeDtypeStruct((t_tok, d // 2), jnp.uint32),
            jax.ShapeDtypeStruct((t_tok, LANES), F32),
        ),
        grid=(t_tok // tm,),
        in_specs=[
            pl.BlockSpec((tm, d), lambda t: (jnp.minimum(t, npt - 1), 0)),
            pl.BlockSpec((tm, d), lambda t: (jnp.maximum(t - npt, 0), 0)),
            pl.BlockSpec(w.shape, lambda t: (0, 0)),
            pl.BlockSpec((tm, d), row),
            pl.BlockSpec((None, N_MOD, d), lambda t: (cond(t), 0, 0)),
            pl.BlockSpec(gb.shape, lambda t: (0, 0)),
            pl.BlockSpec(wr.shape, lambda t: (0, 0, 0)),
        ],
        out_specs=(
            pl.BlockSpec((tm, d), row),
            pl.BlockSpec((tm, d // 2), row),
            pl.BlockSpec((tm, LANES), row),
        ),
        compiler_params=_params(("arbitrary",)),
        name="mix_out_postnorm_router",
    )(a_prompt, a_sample, w, x, mod, gb, wr)


def _gather_kernel(idx_ref, src_ref, out_ref, sem):
    base = pl.program_id(0) * GATHER_ROWS_PER_STEP

    def row_copy(r):
        return pltpu.make_async_copy(src_ref.at[pl.ds(idx_ref[base + r], 1)],
                                     out_ref.at[pl.ds(base + r, 1)], sem)

    @pl.loop(0, GATHER_ROWS_PER_STEP)
    def _(r):
        row_copy(r).start()

    @pl.loop(0, GATHER_ROWS_PER_STEP)
    def _(r):
        row_copy(r).wait()


def _gather_rows(src, idx):
    n = idx.shape[0]
    return pl.pallas_call(
        _gather_kernel,
        out_shape=jax.ShapeDtypeStruct((n, src.shape[1]), src.dtype),
        grid_spec=pltpu.PrefetchScalarGridSpec(
            num_scalar_prefetch=1,
            grid=(n // GATHER_ROWS_PER_STEP,),
            in_specs=[pl.BlockSpec(memory_space=pl.ANY)],
            out_specs=pl.BlockSpec(memory_space=pl.ANY),
            scratch_shapes=[pltpu.SemaphoreType.DMA(())],
        ),
        compiler_params=_params(("arbitrary",)),
        name="row_gather",
    )(idx, src)


def _moe_kernel(ie_ref, it_ref, lo_ref, hi_ref, first_ref, xs_ref, wg_ref, wu_ref, wd_ref, o_ref):
    w = pl.program_id(0)
    x = _unpack_bf16_pairs(xs_ref[...])
    a = _dot(x, wg_ref[...].astype(BF16))
    u = _dot(x, wu_ref[...].astype(BF16))
    act = (_silu(a) * u).astype(BF16)
    y = _dot(act, wd_ref[...].astype(BF16))
    tm = xs_ref.shape[0]
    row = it_ref[w] * tm + lax.broadcasted_iota(jnp.int32, (tm, 1), 0)
    y = jnp.where((row >= lo_ref[w]) & (row < hi_ref[w]), y, 0.0)

    @pl.when(first_ref[w] == 1)
    def _():
        o_ref[...] = y

    @pl.when(first_ref[w] == 0)
    def _():
        o_ref[...] += y


def _moe_experts(plan, xs, w_gate, w_up, w_down):
    ie, it, lo, hi, first = plan
    n_rows, dp = xs.shape
    _, d, f = w_gate.shape
    tm = MOE_TILE
    return pl.pallas_call(
        _moe_kernel,
        out_shape=jax.ShapeDtypeStruct((n_rows, d), F32),
        grid_spec=pltpu.PrefetchScalarGridSpec(
            num_scalar_prefetch=5,
            grid=(ie.shape[0],),
            in_specs=[
                pl.BlockSpec((tm, dp), lambda w, ie, it, lo, hi, fi: (it[w], 0)),
                pl.BlockSpec((None, d, f), lambda w, ie, it, lo, hi, fi: (ie[w], 0, 0)),
                pl.BlockSpec((None, d, f), lambda w, ie, it, lo, hi, fi: (ie[w], 0, 0)),
                pl.BlockSpec((None, f, d), lambda w, ie, it, lo, hi, fi: (ie[w], 0, 0)),
            ],
            out_specs=pl.BlockSpec((tm, d), lambda w, ie, it, lo, hi, fi: (it[w], 0)),
        ),
        compiler_params=_params(("arbitrary",)),
        name="moe_grouped_mlp",
    )(ie, it, lo, hi, first, xs, w_gate, w_up, w_down)


def _moe_plan(route):
    t_tok = route.shape[0]
    n_assign = 2 * t_tok
    tm = MOE_TILE
    n_tiles = n_assign // tm
    n_items = n_tiles + N_EXPERTS - 1
    ef = route[:, :2].astype(jnp.int32).reshape(-1)
    perm = jnp.argsort(ef, stable=True).astype(jnp.int32)
    tok_sorted = perm // 2
    onehot = (ef[:, None] == jnp.arange(N_EXPERTS, dtype=jnp.int32)[None, :]).astype(jnp.int32)
    csum = jnp.cumsum(onehot, axis=0)
    counts = csum[-1]
    ends = jnp.cumsum(counts)
    starts = ends - counts
    rank = jnp.sum(csum * onehot, axis=1) - 1
    pos = starts[ef] + rank
    first_tile = starts // tm
    n_it = jnp.where(counts > 0, (ends - 1) // tm - first_tile + 1, 0)
    it_end = jnp.cumsum(n_it)
    it_start = it_end - n_it
    total = it_end[-1]
    w = jnp.arange(n_items, dtype=jnp.int32)
    ex = jnp.minimum(jnp.searchsorted(it_end, w, side="right").astype(jnp.int32), N_EXPERTS - 1)
    valid = w < total
    ex_last = jnp.minimum(jnp.searchsorted(it_end, total - 1, side="right").astype(jnp.int32), N_EXPERTS - 1)
    ex = jnp.where(valid, ex, ex_last)
    tile = jnp.where(valid, first_tile[ex] + (w - it_start[ex]), n_tiles - 1)
    lo = jnp.where(valid, jnp.maximum(starts[ex], tile * tm), 0)
    hi = jnp.where(valid, jnp.minimum(ends[ex], (tile + 1) * tm), 0)
    first = jnp.concatenate([jnp.ones((1,), jnp.int32), (tile[1:] != tile[:-1]).astype(jnp.int32)])
    gather_back = pos.reshape(t_tok, 2).T.reshape(-1)
    return tok_sorted, gather_back, (ex, tile.astype(jnp.int32), lo.astype(jnp.int32), hi.astype(jnp.int32), first)


def _moe_combine_kernel(x_ref, y0_ref, y1_ref, route_ref, mod_ref, gb_ref, o_ref):
    r = route_ref[...]
    moe = r[:, 2:3] * y0_ref[...] + r[:, 3:4] * y1_ref[...]
    o_ref[...] = (_layernorm(DEEPNORM_ALPHA * x_ref[...] + mod_ref[5:6, :] * moe) * gb_ref[0:1, :]
                  + gb_ref[1:2, :])


def _moe_combine(x1, ys, route, mod, gb, n_prompt, dec_seq):
    t_tok, d = x1.shape
    tm = TOKEN_TILE
    nt = t_tok // tm
    cond = functools.partial(_cond_index, n_prompt_tiles=n_prompt // tm, tiles_per_sample=dec_seq // tm)
    row = lambda t: (t, 0)
    return pl.pallas_call(
        _moe_combine_kernel,
        out_shape=jax.ShapeDtypeStruct((t_tok, d), F32),
        grid=(nt,),
        in_specs=[
            pl.BlockSpec((tm, d), row),
            pl.BlockSpec((tm, d), row),
            pl.BlockSpec((tm, d), lambda t: (t + nt, 0)),
            pl.BlockSpec((tm, LANES), row),
            pl.BlockSpec((None, N_MOD, d), lambda t: (cond(t), 0, 0)),
            pl.BlockSpec(gb.shape, lambda t: (0, 0)),
        ],
        out_specs=pl.BlockSpec((tm, d), row),
        compiler_params=_params(("arbitrary",)),
        name="moe_combine_postnorm",
    )(x1, ys, ys, route, mod, gb)


def _rope_tables(n_lat):
    nf = HEAD_DIM // 4
    s = np.arange(n_lat)
    lane = np.arange(LANES)
    inv = ROPE_BASE ** (-(lane % nf).astype(np.float64) / nf)
    use_col = (lane % HEAD_DIM) >= HEAD_DIM // 2
    p = np.where(use_col[None, :], (s % GRID_W)[:, None], (s // GRID_W)[:, None]).astype(np.float64)
    ang = p * inv[None, :]
    sign = np.where((lane % (2 * nf)) < nf, -1.0, 1.0)
    return jnp.asarray(np.cos(ang), F32), jnp.asarray(np.sin(ang) * sign[None, :], F32)


def _dft_tables(n):
    k = np.arange(n)
    ang = 2.0 * np.pi * ((k[:, None] * k[None, :]) % n).astype(np.float64) / n
    return np.cos(ang), np.sin(ang)


def kernel(x_prompt, x_sample, cache_k, cache_v, c, c_ctx, w_ada, b_ada, ln_gain, ln_bias, w_qkv, w_attn_out,
           lambda_q1, lambda_k1, lambda_q2, lambda_k2, subln_gain, w_fourier_out, w_router_group,
           w_router_expert, w_expert_gate, w_expert_up, w_expert_down):
    bp, sp, d = x_prompt.shape
    bs, n_lat, _ = x_sample.shape
    n_ctx = cache_k.shape[2]
    n_prompt = bp * sp
    t_tok = n_prompt + bs * n_lat
    assert d == D_MODEL and n_prompt % n_lat == 0 and n_lat % TOKEN_TILE == 0 and sp % 16 == 0

    cond = jnp.concatenate([c_ctx[None, :], c, jnp.zeros((8 - 1 - bs, d), F32)], axis=0)
    mods = _ada_all(cond, w_ada, b_ada).reshape(DEPTH, 8, N_MOD, d)

    cos, sin = _rope_tables(n_lat)
    cc, sc = _dft_tables(FOURIER_GROUP_DIM)
    dcs = jnp.asarray(np.concatenate([cc, sc], axis=1), BF16)
    seq_tabs = {s: tuple(jnp.asarray(m, BF16) for m in _dft_tables(s)) for s in (sp, n_lat)}

    x = jnp.concatenate([x_prompt.reshape(n_prompt, d), x_sample.reshape(bs * n_lat, d)], axis=0)
    new_k, new_v = [], []
    for i in range(DEPTH):
        mod = mods[i]
        if i % 2 == 0:
            a = i // 2
            lam_init = 0.8 - 0.6 * math.exp(-0.3 * i)
            q, k, v, kf, vf = _qkv(x, mod, w_qkv[a].astype(BF16), cos, sin, n_prompt, n_lat)
            new_k.append(kf.reshape(bp, sp, N_HEADS, 2 * HEAD_DIM))
            new_v.append(vf.reshape(bp, sp, N_HEADS, V_DIM))
            lam_vecs = jnp.stack([lambda_q1[a], lambda_k1[a], lambda_q2[a], lambda_k2[a]], axis=0)
            gain = subln_gain[a][None, :]
            kc = cache_k[:, a].reshape(bs * n_ctx, d).astype(BF16)
            vc = cache_v[:, a].reshape(bs * n_ctx, d).astype(BF16)
            mixed_p = _attn_prompt(lam_vecs, gain, q, k, v, bp, sp, lam_init)
            mixed_s = _attn_sample(lam_vecs, gain, q, k, v, kc, vc, bs, n_lat, n_ctx, n_prompt, lam_init)
            w_mix = w_attn_out[a]
        else:
            xc, xs = _chan_dft(x, mod, dcs, n_prompt, n_lat)
            mixed_p = _seq_dft(*seq_tabs[sp], xc, xs, bp, sp, 0)
            mixed_s = _seq_dft(*seq_tabs[n_lat], xc, xs, bs, n_lat, n_prompt)
            w_mix = w_fourier_out[i // 2]
        gb0 = jnp.stack([ln_gain[i, 0], ln_bias[i, 0]], axis=0)
        gb1 = jnp.stack([ln_gain[i, 1], ln_bias[i, 1]], axis=0)
        wr = jnp.concatenate([w_router_group[i], w_router_expert[i],
                              jnp.zeros((d, LANES - N_EXPERT_GROUPS - N_EXPERTS), F32)], axis=1)
        wr_hi = wr.astype(BF16)
        wr_lo = (wr - wr_hi.astype(F32)).astype(BF16)
        x1, hp, route = _mix_out(mixed_p, mixed_s, w_mix.astype(BF16), x, mod, gb0,
                                 jnp.stack([wr_hi, wr_lo], axis=0), n_prompt, n_lat)
        tok_sorted, gather_back, plan = _moe_plan(route)
        xs_sorted = _gather_rows(hp, tok_sorted)
        ys = _moe_experts(plan, xs_sorted, w_expert_gate[i], w_expert_up[i], w_expert_down[i])
        yg = _gather_rows(ys, gather_back)
        x = _moe_combine(x1, yg, route, mod, gb1, n_prompt, n_lat)

    y_prompt = x[:n_prompt].reshape(bp, sp, d)
    y_sample = x[n_prompt:].reshape(bs, n_lat, d)
    return (y_prompt, y_sample, jnp.stack(new_k, axis=1), jnp.stack(new_v, axis=1))
```

```python
import functools
import math

import numpy as np
import jax
import jax.numpy as jnp
from jax import lax
from jax.experimental import pallas as pl
from jax.experimental.pallas import tpu as pltpu
from jax.experimental.pallas import tpu_sc as plsc

F32 = jnp.float32
BF16 = jnp.bfloat16

D_MODEL = 1024
DEPTH = 4
GRID_W = 64
N_HEADS = 8
HEAD_DIM = 64
V_DIM = 2 * HEAD_DIM
ROPE_BASE = 10000.0
N_FOURIER_GROUPS = 8
FOURIER_GROUP_DIM = D_MODEL // N_FOURIER_GROUPS
N_EXPERT_GROUPS = 4
EXPERTS_PER_GROUP = 8
N_EXPERTS = N_EXPERT_GROUPS * EXPERTS_PER_GROUP
D_EXPERT = 256
N_MOD = 6
LN_EPS = 1e-5
DEEPNORM_ALPHA = (2.0 * DEPTH) ** 0.25

LANES = 128
TOKEN_TILE = 256
Q_TILE = 256
MOE_TILE = 256
SC_BUFFER_BYTES = 128 * 1024
SC_MAX_INDEX_CHUNK = 128
VMEM_LIMIT = 48 * 1024 * 1024


def _params(semantics):
    return pltpu.CompilerParams(dimension_semantics=semantics, vmem_limit_bytes=VMEM_LIMIT)


def _layernorm(x):
    mu = jnp.mean(x, axis=-1, keepdims=True)
    xc = x - mu
    var = jnp.mean(xc * xc, axis=-1, keepdims=True)
    return xc * lax.rsqrt(var + LN_EPS)


def _silu(a):
    return a / (1.0 + jnp.exp(-a))


def _dot(a, b):
    return jnp.dot(a, b, preferred_element_type=F32)


def _ada_kernel(cond_ref, w_ref, b_ref, o_ref):
    a = _silu(cond_ref[...])
    o_ref[...] = _dot(a.astype(BF16), w_ref[...].astype(BF16)) + b_ref[...]


def _ada_all(cond, w_ada, b_ada):
    depth, d, n = w_ada.shape
    tn = n // 4
    return pl.pallas_call(
        _ada_kernel,
        out_shape=jax.ShapeDtypeStruct((depth, cond.shape[0], n), F32),
        grid=(depth, n // tn),
        in_specs=[
            pl.BlockSpec(cond.shape, lambda l, j: (0, 0)),
            pl.BlockSpec((None, d, tn), lambda l, j: (l, 0, j)),
            pl.BlockSpec((None, 1, tn), lambda l, j: (l, 0, j)),
        ],
        out_specs=pl.BlockSpec((None, cond.shape[0], tn), lambda l, j: (l, 0, j)),
        compiler_params=_params(("arbitrary", "arbitrary")),
        name="ada",
    )(cond, w_ada, b_ada.reshape(depth, 1, n))


def _rope(x, cos, sin_signed, first_half):
    outs = []
    for c in range(x.shape[1] // LANES):
        xc = x[:, c * LANES:(c + 1) * LANES]
        partner = jnp.where(first_half, pltpu.roll(xc, LANES - 16, 1), pltpu.roll(xc, 16, 1))
        outs.append(xc * cos + partner * sin_signed)
    return jnp.concatenate(outs, axis=1)


def _qkv_kernel(x_ref, mod_ref, w_ref, cos_ref, sin_ref, q_ref, k_ref, v_ref, kf_ref, vf_ref, *,
                n_prompt_tiles):
    t = pl.program_id(0)
    h = _layernorm(x_ref[...]) * (1.0 + mod_ref[1:2, :]) + mod_ref[0:1, :]
    acc = _dot(h.astype(BF16), w_ref[...])
    d = x_ref.shape[1]
    q = acc[:, :d] * (HEAD_DIM ** -0.5)
    k = acc[:, d:2 * d]
    v = acc[:, 2 * d:]
    v_ref[...] = v.astype(BF16)

    @pl.when(t < n_prompt_tiles)
    def _():
        q_ref[...] = q.astype(BF16)
        k_ref[...] = k.astype(BF16)
        kf_ref[...] = k
        vf_ref[...] = v

    @pl.when(t >= n_prompt_tiles)
    def _():
        lane = lax.broadcasted_iota(jnp.int32, (x_ref.shape[0], LANES), 1)
        first_half = (lane % 32) < 16
        cos = cos_ref[...]
        sin = sin_ref[...]
        q_ref[...] = _rope(q, cos, sin, first_half).astype(BF16)
        k_ref[...] = _rope(k, cos, sin, first_half).astype(BF16)


def _cond_index(t, n_prompt_tiles, tiles_per_sample):
    return jnp.where(t < n_prompt_tiles, 0, (t - n_prompt_tiles) // tiles_per_sample + 1)


def _qkv(x, mod, w_qkv, cos, sin, n_prompt, dec_seq):
    t_tok, d = x.shape
    tm = TOKEN_TILE
    npt = n_prompt // tm
    tps = dec_seq // tm
    cond = functools.partial(_cond_index, n_prompt_tiles=npt, tiles_per_sample=tps)
    row = lambda t: (t, 0)
    pos = lambda t: (jnp.maximum(t - npt, 0) % tps, 0)
    prm = lambda t: (jnp.minimum(t, npt - 1), 0)
    return pl.pallas_call(
        functools.partial(_qkv_kernel, n_prompt_tiles=npt),
        out_shape=(
            jax.ShapeDtypeStruct((t_tok, d), BF16),
            jax.ShapeDtypeStruct((t_tok, d), BF16),
            jax.ShapeDtypeStruct((t_tok, d), BF16),
            jax.ShapeDtypeStruct((n_prompt, d), F32),
            jax.ShapeDtypeStruct((n_prompt, d), F32),
        ),
        grid=(t_tok // tm,),
        in_specs=[
            pl.BlockSpec((tm, d), row),
            pl.BlockSpec((None, N_MOD, d), lambda t: (cond(t), 0, 0)),
            pl.BlockSpec(w_qkv.shape, lambda t: (0, 0)),
            pl.BlockSpec((tm, LANES), pos),
            pl.BlockSpec((tm, LANES), pos),
        ],
        out_specs=(
            pl.BlockSpec((tm, d), row),
            pl.BlockSpec((tm, d), row),
            pl.BlockSpec((tm, d), row),
            pl.BlockSpec((tm, d), prm),
            pl.BlockSpec((tm, d), prm),
        ),
        compiler_params=_params(("arbitrary",)),
        name="ln_qkv_rope",
    )(x, mod, w_qkv, cos, sin)


def _chan_dft_kernel(x_ref, mod_ref, dcs_ref, xc_ref, xs_ref):
    h = (_layernorm(x_ref[...]) * (1.0 + mod_ref[1:2, :]) + mod_ref[0:1, :]).astype(BF16)
    g = FOURIER_GROUP_DIM
    for i in range(N_FOURIER_GROUPS):
        r = _dot(h[:, i * g:(i + 1) * g], dcs_ref[...])
        xc_ref[:, i * g:(i + 1) * g] = r[:, :g].astype(BF16)
        xs_ref[:, i * g:(i + 1) * g] = r[:, g:].astype(BF16)


def _chan_dft(x, mod, dcs, n_prompt, dec_seq):
    t_tok, d = x.shape
    tm = TOKEN_TILE
    cond = functools.partial(_cond_index, n_prompt_tiles=n_prompt // tm, tiles_per_sample=dec_seq // tm)
    row = lambda t: (t, 0)
    return pl.pallas_call(
        _chan_dft_kernel,
        out_shape=(jax.ShapeDtypeStruct((t_tok, d), BF16), jax.ShapeDtypeStruct((t_tok, d), BF16)),
        grid=(t_tok // tm,),
        in_specs=[
            pl.BlockSpec((tm, d), row),
            pl.BlockSpec((None, N_MOD, d), lambda t: (cond(t), 0, 0)),
            pl.BlockSpec(dcs.shape, lambda t: (0, 0)),
        ],
        out_specs=(pl.BlockSpec((tm, d), row), pl.BlockSpec((tm, d), row)),
        compiler_params=_params(("arbitrary",)),
        name="ln_chan_dft",
    )(x, mod, dcs)


def _seq_dft_kernel(cs_ref, ss_ref, xc_ref, xs_ref, o_ref, *, norm):
    f = _dot(cs_ref[...], xc_ref[...]) - _dot(ss_ref[...], xs_ref[...])
    o_ref[...] = (f * norm).astype(BF16)


def _seq_dft(cs, ss, xc, xs, batch, seq, row_offset):
    d = xc.shape[1]
    tm = min(TOKEN_TILE, seq)
    spt = seq // tm
    off_seq = row_offset // seq
    return pl.pallas_call(
        functools.partial(_seq_dft_kernel, norm=1.0 / math.sqrt(seq * FOURIER_GROUP_DIM)),
        out_shape=jax.ShapeDtypeStruct((batch * seq, d), BF16),
        grid=(batch, spt),
        in_specs=[
            pl.BlockSpec((tm, seq), lambda b, i: (i, 0)),
            pl.BlockSpec((tm, seq), lambda b, i: (i, 0)),
            pl.BlockSpec((seq, d), lambda b, i: (off_seq + b, 0)),
            pl.BlockSpec((seq, d), lambda b, i: (off_seq + b, 0)),
        ],
        out_specs=pl.BlockSpec((tm, d), lambda b, i: (b * spt + i, 0)),
        compiler_params=_params(("arbitrary", "arbitrary")),
        name=f"seq_dft_{seq}",
    )(cs, ss, xc, xs)


def _attn_kernel(lam_ref, gain_ref, q_ref, k_ref, v_ref, *rest, lam_init, heads, has_ctx):
    if has_ctx:
        kc_ref, vc_ref = rest[0], rest[1]
    o_ref = rest[-1]
    lv = lam_ref[...]
    lam = (jnp.exp(jnp.sum(lv[0:1] * lv[1:2], axis=-1, keepdims=True))
           - jnp.exp(jnp.sum(lv[2:3] * lv[3:4], axis=-1, keepdims=True)) + lam_init)
    gain = gain_ref[...]
    tq = q_ref.shape[0]
    lane = lax.broadcasted_iota(jnp.int32, (tq, V_DIM), 1)
    nt = (((1,), (1,)), ((), ()))
    for hd in range(heads):
        cols = slice(hd * V_DIM, (hd + 1) * V_DIM)
        q = q_ref[:, cols]
        q1 = jnp.where(lane < HEAD_DIM, q, jnp.zeros_like(q))
        q2 = jnp.where(lane >= HEAD_DIM, q, jnp.zeros_like(q))
        k = k_ref[:, cols]
        s1 = lax.dot_general(q1, k, nt, preferred_element_type=F32)
        s2 = lax.dot_general(q2, k, nt, preferred_element_type=F32)
        m1 = jnp.max(s1, axis=-1, keepdims=True)
        m2 = jnp.max(s2, axis=-1, keepdims=True)
        if has_ctx:
            kc = kc_ref[:, cols]
            c1 = lax.dot_general(q1, kc, nt, preferred_element_type=F32)
            c2 = lax.dot_general(q2, kc, nt, preferred_element_type=F32)
            m1 = jnp.maximum(m1, jnp.max(c1, axis=-1, keepdims=True))
            m2 = jnp.maximum(m2, jnp.max(c2, axis=-1, keepdims=True))
        p1 = jnp.exp(s1 - m1)
        p2 = jnp.exp(s2 - m2)
        l1 = jnp.sum(p1, axis=-1, keepdims=True)
        l2 = jnp.sum(p2, axis=-1, keepdims=True)
        if has_ctx:
            pc1 = jnp.exp(c1 - m1)
            pc2 = jnp.exp(c2 - m2)
            l1 = l1 + jnp.sum(pc1, axis=-1, keepdims=True)
            l2 = l2 + jnp.sum(pc2, axis=-1, keepdims=True)
        r1 = 1.0 / l1
        r2 = lam / l2
        o = _dot((p1 * r1 - p2 * r2).astype(BF16), v_ref[:, cols])
        if has_ctx:
            o = o + _dot((pc1 * r1 - pc2 * r2).astype(BF16), vc_ref[:, cols])
        o = o * lax.rsqrt(jnp.mean(o * o, axis=-1, keepdims=True) + LN_EPS)
        o_ref[:, cols] = (o * gain * (1.0 - lam_init)).astype(BF16)


def _attn_prompt(lam_vecs, gain, q, k, v, batch, seq, lam_init):
    d = q.shape[1]
    blk = pl.BlockSpec((seq, d), lambda b: (b, 0))
    return pl.pallas_call(
        functools.partial(_attn_kernel, lam_init=lam_init, heads=N_HEADS, has_ctx=False),
        out_shape=jax.ShapeDtypeStruct((batch * seq, d), BF16),
        grid=(batch,),
        in_specs=[
            pl.BlockSpec(lam_vecs.shape, lambda b: (0, 0)),
            pl.BlockSpec(gain.shape, lambda b: (0, 0)),
            blk, blk, blk,
        ],
        out_specs=blk,
        compiler_params=_params(("arbitrary",)),
        name="diff_attn_ctx",
    )(lam_vecs, gain, q, k, v)


def _attn_sample(lam_vecs, gain, q, k, v, kc, vc, batch, seq, n_ctx, row_offset, lam_init):
    d = q.shape[1]
    tq = Q_TILE
    qpt = seq // tq
    off_seq = row_offset // seq
    off_tile = row_offset // tq
    qmap = lambda b, h, i: (off_tile + b * qpt + i, h)
    kmap = lambda b, h, i: (off_seq + b, h)
    cmap = lambda b, h, i: (b, h)
    return pl.pallas_call(
        functools.partial(_attn_kernel, lam_init=lam_init, heads=1, has_ctx=True),
        out_shape=jax.ShapeDtypeStruct((batch * seq, d), BF16),
        grid=(batch, N_HEADS, qpt),
        in_specs=[
            pl.BlockSpec(lam_vecs.shape, lambda b, h, i: (0, 0)),
            pl.BlockSpec(gain.shape, lambda b, h, i: (0, 0)),
            pl.BlockSpec((tq, V_DIM), qmap),
            pl.BlockSpec((seq, V_DIM), kmap),
            pl.BlockSpec((seq, V_DIM), kmap),
            pl.BlockSpec((n_ctx, V_DIM), cmap),
            pl.BlockSpec((n_ctx, V_DIM), cmap),
        ],
        out_specs=pl.BlockSpec((tq, V_DIM), lambda b, h, i: (b * qpt + i, h)),
        compiler_params=_params(("arbitrary", "arbitrary", "arbitrary")),
        name="diff_attn_latent",
    )(lam_vecs, gain, q, k, v, kc, vc)


def _route(lg):
    lane = lax.broadcasted_iota(jnp.int32, lg.shape, 1)
    lane_f = lane.astype(F32)
    neg = jnp.float32(-jnp.inf)
    big = jnp.float32(LANES)
    gl = jnp.where(lane < N_EXPERT_GROUPS, lg, neg)
    gmax = jnp.max(gl, axis=-1, keepdims=True)
    g_prob = 1.0 / jnp.sum(jnp.exp(gl - gmax), axis=-1, keepdims=True)
    g_idx = jnp.min(jnp.where(gl == gmax, lane_f, big), axis=-1, keepdims=True)
    lo = N_EXPERT_GROUPS + EXPERTS_PER_GROUP * g_idx
    el = jnp.where((lane_f >= lo) & (lane_f < lo + EXPERTS_PER_GROUP), lg, neg)
    m1 = jnp.max(el, axis=-1, keepdims=True)
    i1 = jnp.min(jnp.where(el == m1, lane_f, big), axis=-1, keepdims=True)
    el2 = jnp.where(lane_f == i1, neg, el)
    m2 = jnp.max(el2, axis=-1, keepdims=True)
    i2 = jnp.min(jnp.where(el2 == m2, lane_f, big), axis=-1, keepdims=True)
    t = jnp.exp(m2 - m1)
    w1 = g_prob / (1.0 + t)
    w2 = g_prob * t / (1.0 + t)
    out = jnp.where(lane == 0, i1 - N_EXPERT_GROUPS, 0.0)
    out = jnp.where(lane == 1, i2 - N_EXPERT_GROUPS, out)
    out = jnp.where(lane == 2, w1, out)
    out = jnp.where(lane == 3, w2, out)
    return out


def _pack_bf16_pairs(h):
    n = h.shape[1] // 2
    bits = lax.bitcast_convert_type(h.astype(BF16).astype(F32), jnp.uint32)
    return (bits[:, :n] >> 16) | bits[:, n:]


def _unpack_bf16_pairs(p):
    lo = lax.bitcast_convert_type(p << 16, F32)
    hi = lax.bitcast_convert_type(p & jnp.uint32(0xFFFF0000), F32)
    return jnp.concatenate([lo, hi], axis=1).astype(BF16)


def _mix_out_kernel(ap_ref, as_ref, w_ref, x_ref, mod_ref, gb_ref, wr_ref, x1_ref, hp_ref, route_ref, *,
                    n_prompt_tiles):
    a = jnp.where(pl.program_id(0) < n_prompt_tiles, ap_ref[...], as_ref[...])
    out = _dot(a, w_ref[...])
    x1 = _layernorm(DEEPNORM_ALPHA * x_ref[...] + mod_ref[2:3, :] * out) * gb_ref[0:1, :] + gb_ref[1:2, :]
    x1_ref[...] = x1
    h2 = _layernorm(x1) * (1.0 + mod_ref[4:5, :]) + mod_ref[3:4, :]
    hp_ref[...] = _pack_bf16_pairs(h2)
    hi = h2.astype(BF16)
    lo = (h2 - hi.astype(F32)).astype(BF16)
    logits = _dot(hi, wr_ref[0]) + _dot(lo, wr_ref[0]) + _dot(hi, wr_ref[1])
    route_ref[...] = _route(logits)


def _mix_out(a_prompt, a_sample, w, x, mod, gb, wr, n_prompt, dec_seq):
    t_tok, d = x.shape
    tm = TOKEN_TILE
    npt = n_prompt // tm
    cond = functools.partial(_cond_index, n_prompt_tiles=npt, tiles_per_sample=dec_seq // tm)
    row = lambda t: (t, 0)
    return pl.pallas_call(
        functools.partial(_mix_out_kernel, n_prompt_tiles=npt),
        out_shape=(
            jax.ShapeDtypeStruct((t_tok, d), F32),
            jax.ShapeDtypeStruct((t_tok, d // 2), jnp.uint32),
            jax.ShapeDtypeStruct((t_tok, LANES), F32),
        ),
        grid=(t_tok // tm,),
        in_specs=[
            pl.BlockSpec((tm, d), lambda t: (jnp.minimum(t, npt - 1), 0)),
            pl.BlockSpec((tm, d), lambda t: (jnp.maximum(t - npt, 0), 0)),
            pl.BlockSpec(w.shape, lambda t: (0, 0)),
            pl.BlockSpec((tm, d), row),
            pl.BlockSpec((None, N_MOD, d), lambda t: (cond(t), 0, 0)),
            pl.BlockSpec(gb.shape, lambda t: (0, 0)),
            pl.BlockSpec(wr.shape, lambda t: (0, 0, 0)),
        ],
        out_specs=(
            pl.BlockSpec((tm, d), row),
            pl.BlockSpec((tm, d // 2), row),
            pl.BlockSpec((tm, LANES), row),
        ),
        compiler_params=_params(("arbitrary",)),
        name="mix_out_postnorm_router",
    )(a_prompt, a_sample, w, x, mod, gb, wr)


def _sc_workers():
    info = plsc.get_sparse_core_info()
    return info.num_cores, info.num_cores * info.num_subcores


def _sc_pipeline(n_chunks, load, store):
    load(0, 0).start()
    for j in range(n_chunks):
        b = j % 2
        if j + 1 < n_chunks:
            if j >= 1:
                store(j - 1, 1 - b).wait()
            load(j + 1, 1 - b).start()
        load(j, b).wait()
        store(j, b).start()
    if n_chunks >= 2:
        store(n_chunks - 2, n_chunks % 2).wait()
    store(n_chunks - 1, (n_chunks - 1) % 2).wait()


def _sc_row_mover(n, d, dtype, name, body_of):
    n_cores, n_workers = _sc_workers()
    per_worker = n // n_workers
    k = SC_BUFFER_BYTES // (d * 4)
    n_chunks = per_worker // k
    assert n_chunks * k * n_workers == n and k <= SC_MAX_INDEX_CHUNK
    call = pl.kernel(
        body_of(n_cores, per_worker, k, n_chunks),
        out_type=jax.ShapeDtypeStruct((n, d), dtype),
        mesh=plsc.VectorSubcoreMesh(core_axis_name="c", subcore_axis_name="s"),
        scratch_types=[pltpu.VMEM((n_chunks, k), jnp.int32), pltpu.VMEM((2, k, d), dtype),
                       pltpu.SemaphoreType.DMA((2,)), pltpu.SemaphoreType.DMA((2,))],
        name=name,
    )
    return lambda rows, idx: call(rows, idx.reshape(n_workers, n_chunks, k))


def _sc_gather_rows(table, idx):
    def body_of(n_cores, per_worker, k, n_chunks):
        def body(table_hbm, idx_hbm, out_hbm, idx_v, rows_v, gsem, osem):
            wid = lax.axis_index("s") * n_cores + lax.axis_index("c")
            base = wid * per_worker
            pltpu.sync_copy(idx_hbm.at[wid], idx_v)
            _sc_pipeline(
                n_chunks,
                lambda j, b: pltpu.make_async_copy(table_hbm.at[idx_v.at[j]], rows_v.at[b], gsem.at[b]),
                lambda j, b: pltpu.make_async_copy(rows_v.at[b], out_hbm.at[pl.ds(base + j * k, k)],
                                                   osem.at[b]))
        return body
    return _sc_row_mover(idx.shape[0], table.shape[1], table.dtype, "sc_row_gather", body_of)(table, idx)


def _sc_scatter_rows(src, idx):
    n_src = src.shape[0]

    def body_of(n_cores, per_worker, k, n_chunks):
        assert n_src % per_worker == 0

        def body(src_hbm, idx_hbm, out_hbm, idx_v, rows_v, gsem, osem):
            wid = lax.axis_index("s") * n_cores + lax.axis_index("c")
            base = lax.rem(wid * per_worker, n_src)
            pltpu.sync_copy(idx_hbm.at[wid], idx_v)
            _sc_pipeline(
                n_chunks,
                lambda j, b: pltpu.make_async_copy(src_hbm.at[pl.ds(base + j * k, k)], rows_v.at[b],
                                                   gsem.at[b]),
                lambda j, b: pltpu.make_async_copy(rows_v.at[b], out_hbm.at[idx_v.at[j]], osem.at[b]))
        return body
    return _sc_row_mover(idx.shape[0], src.shape[1], src.dtype, "sc_row_scatter", body_of)(src, idx)


def _moe_kernel(ie_ref, it_ref, lo_ref, hi_ref, first_ref, xs_ref, wg_ref, wu_ref, wd_ref, o_ref):
    w = pl.program_id(0)
    x = _unpack_bf16_pairs(xs_ref[...])
    a = _dot(x, wg_ref[...].astype(BF16))
    u = _dot(x, wu_ref[...].astype(BF16))
    act = (_silu(a) * u).astype(BF16)
    y = _dot(act, wd_ref[...].astype(BF16))
    tm = xs_ref.shape[0]
    row = it_ref[w] * tm + lax.broadcasted_iota(jnp.int32, (tm, 1), 0)
    y = jnp.where((row >= lo_ref[w]) & (row < hi_ref[w]), y, 0.0)

    @pl.when(first_ref[w] == 1)
    def _():
        o_ref[...] = y

    @pl.when(first_ref[w] == 0)
    def _():
        o_ref[...] += y


def _moe_experts(plan, xs, w_gate, w_up, w_down):
    ie, it, lo, hi, first = plan
    n_rows, dp = xs.shape
    _, d, f = w_gate.shape
    tm = MOE_TILE
    return pl.pallas_call(
        _moe_kernel,
        out_shape=jax.ShapeDtypeStruct((n_rows, d), F32),
        grid_spec=pltpu.PrefetchScalarGridSpec(
            num_scalar_prefetch=5,
            grid=(ie.shape[0],),
            in_specs=[
                pl.BlockSpec((tm, dp), lambda w, ie, it, lo, hi, fi: (it[w], 0)),
                pl.BlockSpec((None, d, f), lambda w, ie, it, lo, hi, fi: (ie[w], 0, 0)),
                pl.BlockSpec((None, d, f), lambda w, ie, it, lo, hi, fi: (ie[w], 0, 0)),
                pl.BlockSpec((None, f, d), lambda w, ie, it, lo, hi, fi: (ie[w], 0, 0)),
            ],
            out_specs=pl.BlockSpec((tm, d), lambda w, ie, it, lo, hi, fi: (it[w], 0)),
        ),
        compiler_params=_params(("arbitrary",)),
        name="moe_grouped_mlp",
    )(ie, it, lo, hi, first, xs, w_gate, w_up, w_down)


def _moe_plan(route):
    t_tok = route.shape[0]
    n_assign = 2 * t_tok
    tm = MOE_TILE
    n_tiles = n_assign // tm
    n_items = n_tiles + N_EXPERTS - 1
    ef = route[:, :2].astype(jnp.int32).T.reshape(-1)
    onehot = (ef[:, None] == jnp.arange(N_EXPERTS, dtype=jnp.int32)[None, :]).astype(jnp.int32)
    csum = jnp.cumsum(onehot, axis=0)
    counts = csum[-1]
    ends = jnp.cumsum(counts)
    starts = ends - counts
    rank = jnp.sum(csum * onehot, axis=1) - 1
    pos = starts[ef] + rank
    first_tile = starts // tm
    n_it = jnp.where(counts > 0, (ends - 1) // tm - first_tile + 1, 0)
    it_end = jnp.cumsum(n_it)
    it_start = it_end - n_it
    total = it_end[-1]
    w = jnp.arange(n_items, dtype=jnp.int32)
    ex = jnp.minimum(jnp.searchsorted(it_end, w, side="right").astype(jnp.int32), N_EXPERTS - 1)
    valid = w < total
    ex_last = jnp.minimum(jnp.searchsorted(it_end, total - 1, side="right").astype(jnp.int32), N_EXPERTS - 1)
    ex = jnp.where(valid, ex, ex_last)
    tile = jnp.where(valid, first_tile[ex] + (w - it_start[ex]), n_tiles - 1)
    lo = jnp.where(valid, jnp.maximum(starts[ex], tile * tm), 0)
    hi = jnp.where(valid, jnp.minimum(ends[ex], (tile + 1) * tm), 0)
    first = jnp.concatenate([jnp.ones((1,), jnp.int32), (tile[1:] != tile[:-1]).astype(jnp.int32)])
    return pos.astype(jnp.int32), (ex, tile.astype(jnp.int32), lo.astype(jnp.int32), hi.astype(jnp.int32), first)


def _moe_combine_kernel(x_ref, y0_ref, y1_ref, route_ref, mod_ref, gb_ref, o_ref):
    r = route_ref[...]
    moe = r[:, 2:3] * y0_ref[...] + r[:, 3:4] * y1_ref[...]
    o_ref[...] = (_layernorm(DEEPNORM_ALPHA * x_ref[...] + mod_ref[5:6, :] * moe) * gb_ref[0:1, :]
                  + gb_ref[1:2, :])


def _moe_combine(x1, ys, route, mod, gb, n_prompt, dec_seq):
    t_tok, d = x1.shape
    tm = TOKEN_TILE
    nt = t_tok // tm
    cond = functools.partial(_cond_index, n_prompt_tiles=n_prompt // tm, tiles_per_sample=dec_seq // tm)
    row = lambda t: (t, 0)
    return pl.pallas_call(
        _moe_combine_kernel,
        out_shape=jax.ShapeDtypeStruct((t_tok, d), F32),
        grid=(nt,),
        in_specs=[
            pl.BlockSpec((tm, d), row),
            pl.BlockSpec((tm, d), row),
            pl.BlockSpec((tm, d), lambda t: (t + nt, 0)),
            pl.BlockSpec((tm, LANES), row),
            pl.BlockSpec((None, N_MOD, d), lambda t: (cond(t), 0, 0)),
            pl.BlockSpec(gb.shape, lambda t: (0, 0)),
        ],
        out_specs=pl.BlockSpec((tm, d), row),
        compiler_params=_params(("arbitrary",)),
        name="moe_combine_postnorm",
    )(x1, ys, ys, route, mod, gb)


def _rope_tables(n_lat):
    nf = HEAD_DIM // 4
    s = np.arange(n_lat)
    lane = np.arange(LANES)
    inv = ROPE_BASE ** (-(lane % nf).astype(np.float64) / nf)
    use_col = (lane % HEAD_DIM) >= HEAD_DIM // 2
    p = np.where(use_col[None, :], (s % GRID_W)[:, None], (s // GRID_W)[:, None]).astype(np.float64)
    ang = p * inv[None, :]
    sign = np.where((lane % (2 * nf)) < nf, -1.0, 1.0)
    return jnp.asarray(np.cos(ang), F32), jnp.asarray(np.sin(ang) * sign[None, :], F32)


def _dft_tables(n):
    k = np.arange(n)
    ang = 2.0 * np.pi * ((k[:, None] * k[None, :]) % n).astype(np.float64) / n
    return np.cos(ang), np.sin(ang)


def kernel(x_prompt, x_sample, cache_k, cache_v, c, c_ctx, w_ada, b_ada, ln_gain, ln_bias, w_qkv, w_attn_out,
           lambda_q1, lambda_k1, lambda_q2, lambda_k2, subln_gain, w_fourier_out, w_router_group,
           w_router_expert, w_expert_gate, w_expert_up, w_expert_down):
    bp, sp, d = x_prompt.shape
    bs, n_lat, _ = x_sample.shape
    n_ctx = cache_k.shape[2]
    n_prompt = bp * sp
    t_tok = n_prompt + bs * n_lat
    assert d == D_MODEL and n_prompt % n_lat == 0 and n_lat % TOKEN_TILE == 0 and sp % 16 == 0

    cond = jnp.concatenate([c_ctx[None, :], c, jnp.zeros((8 - 1 - bs, d), F32)], axis=0)
    mods = _ada_all(cond, w_ada, b_ada).reshape(DEPTH, 8, N_MOD, d)

    cos, sin = _rope_tables(n_lat)
    cc, sc = _dft_tables(FOURIER_GROUP_DIM)
    dcs = jnp.asarray(np.concatenate([cc, sc], axis=1), BF16)
    seq_tabs = {s: tuple(jnp.asarray(m, BF16) for m in _dft_tables(s)) for s in (sp, n_lat)}

    x = jnp.concatenate([x_prompt.reshape(n_prompt, d), x_sample.reshape(bs * n_lat, d)], axis=0)
    new_k, new_v = [], []
    for i in range(DEPTH):
        mod = mods[i]
        if i % 2 == 0:
            a = i // 2
            lam_init = 0.8 - 0.6 * math.exp(-0.3 * i)
            q, k, v, kf, vf = _qkv(x, mod, w_qkv[a].astype(BF16), cos, sin, n_prompt, n_lat)
            new_k.append(kf.reshape(bp, sp, N_HEADS, 2 * HEAD_DIM))
            new_v.append(vf.reshape(bp, sp, N_HEADS, V_DIM))
            lam_vecs = jnp.stack([lambda_q1[a], lambda_k1[a], lambda_q2[a], lambda_k2[a]], axis=0)
            gain = subln_gain[a][None, :]
            kc = cache_k[:, a].reshape(bs * n_ctx, d).astype(BF16)
            vc = cache_v[:, a].reshape(bs * n_ctx, d).astype(BF16)
            mixed_p = _attn_prompt(lam_vecs, gain, q, k, v, bp, sp, lam_init)
            mixed_s = _attn_sample(lam_vecs, gain, q, k, v, kc, vc, bs, n_lat, n_ctx, n_prompt, lam_init)
            w_mix = w_attn_out[a]
        else:
            xc, xs = _chan_dft(x, mod, dcs, n_prompt, n_lat)
            mixed_p = _seq_dft(*seq_tabs[sp], xc, xs, bp, sp, 0)
            mixed_s = _seq_dft(*seq_tabs[n_lat], xc, xs, bs, n_lat, n_prompt)
            w_mix = w_fourier_out[i // 2]
        gb0 = jnp.stack([ln_gain[i, 0], ln_bias[i, 0]], axis=0)
        gb1 = jnp.stack([ln_gain[i, 1], ln_bias[i, 1]], axis=0)
        wr = jnp.concatenate([w_router_group[i], w_router_expert[i],
                              jnp.zeros((d, LANES - N_EXPERT_GROUPS - N_EXPERTS), F32)], axis=1)
        wr_hi = wr.astype(BF16)
        wr_lo = (wr - wr_hi.astype(F32)).astype(BF16)
        x1, hp, route = _mix_out(mixed_p, mixed_s, w_mix.astype(BF16), x, mod, gb0,
                                 jnp.stack([wr_hi, wr_lo], axis=0), n_prompt, n_lat)
        pos, plan = _moe_plan(route)
        xs_sorted = _sc_scatter_rows(hp, pos)
        ys = _moe_experts(plan, xs_sorted, w_expert_gate[i], w_expert_up[i], w_expert_down[i])
        yg = _sc_gather_rows(ys, pos)
        x = _moe_combine(x1, yg, route, mod, gb1, n_prompt, n_lat)

    y_prompt = x[:n_prompt].reshape(bp, sp, d)
    y_sample = x[n_prompt:].reshape(bs, n_lat, d)
    return (y_prompt, y_sample, jnp.stack(new_k, axis=1), jnp.stack(new_v, axis=1))
```

```python
import functools
import math

import numpy as np
import jax
import jax.numpy as jnp
from jax import lax
from jax.experimental import pallas as pl
from jax.experimental.pallas import tpu as pltpu
from jax.experimental.pallas import tpu_sc as plsc

F32 = jnp.float32
BF16 = jnp.bfloat16

D_MODEL = 1024
DEPTH = 4
GRID_W = 64
N_HEADS = 8
HEAD_DIM = 64
V_DIM = 2 * HEAD_DIM
ROPE_BASE = 10000.0
N_FOURIER_GROUPS = 8
FOURIER_GROUP_DIM = D_MODEL // N_FOURIER_GROUPS
N_EXPERT_GROUPS = 4
EXPERTS_PER_GROUP = 8
N_EXPERTS = N_EXPERT_GROUPS * EXPERTS_PER_GROUP
D_EXPERT = 256
N_MOD = 6
LN_EPS = 1e-5
DEEPNORM_ALPHA = (2.0 * DEPTH) ** 0.25
Q_SCALE = math.log2(math.e) * HEAD_DIM ** -0.5

LANES = 128
TOKEN_TILE = 256
Q_TILE = 256
ATTN_ROW_CHUNK = 128
MOE_TILE = 512
SC_BUFFER_BYTES = 128 * 1024
SC_MAX_INDEX_CHUNK = 128
VMEM_LIMIT = 48 * 1024 * 1024


def _params(semantics):
    return pltpu.CompilerParams(dimension_semantics=semantics, vmem_limit_bytes=VMEM_LIMIT)


def _layernorm(x):
    mu = jnp.mean(x, axis=-1, keepdims=True)
    xc = x - mu
    var = jnp.mean(xc * xc, axis=-1, keepdims=True)
    return xc * lax.rsqrt(var + LN_EPS)


def _silu(a):
    return a / (1.0 + jnp.exp(-a))


def _dot(a, b):
    return jnp.dot(a, b, preferred_element_type=F32)


def _ada_kernel(cond_ref, w_ref, b_ref, o_ref):
    a = _silu(cond_ref[...])
    o_ref[...] = _dot(a.astype(BF16), w_ref[...].astype(BF16)) + b_ref[...]


def _ada_all(cond, w_ada, b_ada):
    depth, d, n = w_ada.shape
    tn = n // 4
    return pl.pallas_call(
        _ada_kernel,
        out_shape=jax.ShapeDtypeStruct((depth, cond.shape[0], n), F32),
        grid=(depth, n // tn),
        in_specs=[
            pl.BlockSpec(cond.shape, lambda l, j: (0, 0)),
            pl.BlockSpec((None, d, tn), lambda l, j: (l, 0, j)),
            pl.BlockSpec((None, 1, tn), lambda l, j: (l, 0, j)),
        ],
        out_specs=pl.BlockSpec((None, cond.shape[0], tn), lambda l, j: (l, 0, j)),
        compiler_params=_params(("arbitrary", "arbitrary")),
        name="ada",
    )(cond, w_ada, b_ada.reshape(depth, 1, n))


def _rope(x, cos, sin_signed, first_half):
    outs = []
    for c in range(x.shape[1] // LANES):
        xc = x[:, c * LANES:(c + 1) * LANES]
        partner = jnp.where(first_half, pltpu.roll(xc, LANES - 16, 1), pltpu.roll(xc, 16, 1))
        outs.append(xc * cos + partner * sin_signed)
    return jnp.concatenate(outs, axis=1)


def _qkv_kernel(x_ref, mod_ref, w_ref, cos_ref, sin_ref, q_ref, k_ref, v_ref, kf_ref, vf_ref, *,
                n_prompt_tiles):
    t = pl.program_id(0)
    h = _layernorm(x_ref[...]) * (1.0 + mod_ref[1:2, :]) + mod_ref[0:1, :]
    acc = _dot(h.astype(BF16), w_ref[...])
    d = x_ref.shape[1]
    q = acc[:, :d] * Q_SCALE
    k = acc[:, d:2 * d]
    v = acc[:, 2 * d:]
    v_ref[...] = v.astype(BF16)

    @pl.when(t < n_prompt_tiles)
    def _():
        q_ref[...] = q.astype(BF16)
        k_ref[...] = k.astype(BF16)
        kf_ref[...] = k
        vf_ref[...] = v

    @pl.when(t >= n_prompt_tiles)
    def _():
        lane = lax.broadcasted_iota(jnp.int32, (x_ref.shape[0], LANES), 1)
        first_half = (lane % 32) < 16
        cos = cos_ref[...]
        sin = sin_ref[...]
        q_ref[...] = _rope(q, cos, sin, first_half).astype(BF16)
        k_ref[...] = _rope(k, cos, sin, first_half).astype(BF16)


def _cond_index(t, n_prompt_tiles, tiles_per_sample):
    return jnp.where(t < n_prompt_tiles, 0, (t - n_prompt_tiles) // tiles_per_sample + 1)


def _qkv(x, mod, w_qkv, cos, sin, n_prompt, dec_seq):
    t_tok, d = x.shape
    tm = TOKEN_TILE
    npt = n_prompt // tm
    tps = dec_seq // tm
    cond = functools.partial(_cond_index, n_prompt_tiles=npt, tiles_per_sample=tps)
    row = lambda t: (t, 0)
    pos = lambda t: (jnp.maximum(t - npt, 0) % tps, 0)
    prm = lambda t: (jnp.minimum(t, npt - 1), 0)
    return pl.pallas_call(
        functools.partial(_qkv_kernel, n_prompt_tiles=npt),
        out_shape=(
            jax.ShapeDtypeStruct((t_tok, d), BF16),
            jax.ShapeDtypeStruct((t_tok, d), BF16),
            jax.ShapeDtypeStruct((t_tok, d), BF16),
            jax.ShapeDtypeStruct((n_prompt, d), F32),
            jax.ShapeDtypeStruct((n_prompt, d), F32),
        ),
        grid=(t_tok // tm,),
        in_specs=[
            pl.BlockSpec((tm, d), row),
            pl.BlockSpec((None, N_MOD, d), lambda t: (cond(t), 0, 0)),
            pl.BlockSpec(w_qkv.shape, lambda t: (0, 0)),
            pl.BlockSpec((tm, LANES), pos),
            pl.BlockSpec((tm, LANES), pos),
        ],
        out_specs=(
            pl.BlockSpec((tm, d), row),
            pl.BlockSpec((tm, d), row),
            pl.BlockSpec((tm, d), row),
            pl.BlockSpec((tm, d), prm),
            pl.BlockSpec((tm, d), prm),
        ),
        compiler_params=_params(("arbitrary",)),
        name="ln_qkv_rope",
    )(x, mod, w_qkv, cos, sin)


def _chan_dft_kernel(x_ref, mod_ref, dcs_ref, xc_ref, xs_ref):
    h = (_layernorm(x_ref[...]) * (1.0 + mod_ref[1:2, :]) + mod_ref[0:1, :]).astype(BF16)
    g = FOURIER_GROUP_DIM
    for i in range(N_FOURIER_GROUPS):
        r = _dot(h[:, i * g:(i + 1) * g], dcs_ref[...])
        xc_ref[:, i * g:(i + 1) * g] = r[:, :g].astype(BF16)
        xs_ref[:, i * g:(i + 1) * g] = r[:, g:].astype(BF16)


def _chan_dft(x, mod, dcs, n_prompt, dec_seq):
    t_tok, d = x.shape
    tm = TOKEN_TILE
    cond = functools.partial(_cond_index, n_prompt_tiles=n_prompt // tm, tiles_per_sample=dec_seq // tm)
    row = lambda t: (t, 0)
    return pl.pallas_call(
        _chan_dft_kernel,
        out_shape=(jax.ShapeDtypeStruct((t_tok, d), BF16), jax.ShapeDtypeStruct((t_tok, d), BF16)),
        grid=(t_tok // tm,),
        in_specs=[
            pl.BlockSpec((tm, d), row),
            pl.BlockSpec((None, N_MOD, d), lambda t: (cond(t), 0, 0)),
            pl.BlockSpec(dcs.shape, lambda t: (0, 0)),
        ],
        out_specs=(pl.BlockSpec((tm, d), row), pl.BlockSpec((tm, d), row)),
        compiler_params=_params(("arbitrary",)),
        name="ln_chan_dft",
    )(x, mod, dcs)


def _seq_dft_kernel(cs_ref, ss_ref, xc_ref, xs_ref, o_ref, *, norm):
    f = _dot(cs_ref[...], xc_ref[...]) - _dot(ss_ref[...], xs_ref[...])
    o_ref[...] = (f * norm).astype(BF16)


def _seq_dft(cs, ss, xc, xs, batch, seq, row_offset):
    d = xc.shape[1]
    tm = min(TOKEN_TILE, seq)
    spt = seq // tm
    off_seq = row_offset // seq
    return pl.pallas_call(
        functools.partial(_seq_dft_kernel, norm=1.0 / math.sqrt(seq * FOURIER_GROUP_DIM)),
        out_shape=jax.ShapeDtypeStruct((batch * seq, d), BF16),
        grid=(batch, spt),
        in_specs=[
            pl.BlockSpec((tm, seq), lambda b, i: (i, 0)),
            pl.BlockSpec((tm, seq), lambda b, i: (i, 0)),
            pl.BlockSpec((seq, d), lambda b, i: (off_seq + b, 0)),
            pl.BlockSpec((seq, d), lambda b, i: (off_seq + b, 0)),
        ],
        out_specs=pl.BlockSpec((tm, d), lambda b, i: (b * spt + i, 0)),
        compiler_params=_params(("arbitrary", "arbitrary")),
        name=f"seq_dft_{seq}",
    )(cs, ss, xc, xs)


def _diff_lambda(lam_ref, lam_init):
    lv = lam_ref[...]
    return (jnp.exp(jnp.sum(lv[0:1] * lv[1:2], axis=-1, keepdims=True))
            - jnp.exp(jnp.sum(lv[2:3] * lv[3:4], axis=-1, keepdims=True)) + lam_init)


def _diff_attn_head(q, k, v_ext, lam, gain, lam_init):
    tq = q.shape[0]
    lane = lax.broadcasted_iota(jnp.int32, q.shape, 1)
    zero = jnp.zeros_like(q)
    qq = jnp.concatenate([jnp.where(lane < HEAD_DIM, q, zero), jnp.where(lane >= HEAD_DIM, q, zero)], axis=0)
    parts = []
    for r in range(0, 2 * tq, ATTN_ROW_CHUNK):
        s = lax.dot_general(qq[r:r + ATTN_ROW_CHUNK], k, (((1,), (1,)), ((), ())), preferred_element_type=F32)
        e = jnp.exp2(s - jnp.max(s, axis=-1, keepdims=True)).astype(BF16)
        parts.append(_dot(e, v_ext))
    oe = jnp.concatenate(parts, axis=0)
    o = oe[:, :V_DIM] / oe[:, V_DIM:]
    o = o[:tq] - lam * o[tq:]
    o = o * lax.rsqrt(jnp.mean(o * o, axis=-1, keepdims=True) + LN_EPS)
    return o * gain * (1.0 - lam_init)


def _attn_prompt_kernel(lam_ref, gain_ref, q_ref, k_ref, v_ref, o_ref, *, lam_init):
    lam = _diff_lambda(lam_ref, lam_init)
    gain = gain_ref[...]
    ones = jnp.ones((k_ref.shape[0], V_DIM), BF16)
    for hd in range(N_HEADS):
        cols = slice(hd * V_DIM, (hd + 1) * V_DIM)
        v_ext = jnp.concatenate([v_ref[:, cols], ones], axis=1)
        o_ref[:, cols] = _diff_attn_head(q_ref[:, cols], k_ref[:, cols], v_ext, lam, gain, lam_init).astype(BF16)


def _attn_latent_kernel(lam_ref, gain_ref, q_ref, k_ref, v_ref, kc_ref, vc_ref, o_ref, kall_ref, vext_ref, *,
                        lam_init):
    n_new = k_ref.shape[0]

    @pl.when(pl.program_id(2) == 0)
    def _():
        kall_ref[:n_new, :] = k_ref[...]
        kall_ref[n_new:, :] = kc_ref[...]
        vext_ref[:n_new, :V_DIM] = v_ref[...]
        vext_ref[n_new:, :V_DIM] = vc_ref[...]
        vext_ref[:, V_DIM:] = jnp.ones((vext_ref.shape[0], V_DIM), BF16)

    o = _diff_attn_head(q_ref[...], kall_ref[...], vext_ref[...], _diff_lambda(lam_ref, lam_init), gain_ref[...],
                        lam_init)
    o_ref[...] = o.astype(BF16)


def _attn_prompt(lam_vecs, gain, q, k, v, batch, seq, lam_init):
    d = q.shape[1]
    blk = pl.BlockSpec((seq, d), lambda b: (b, 0))
    return pl.pallas_call(
        functools.partial(_attn_prompt_kernel, lam_init=lam_init),
        out_shape=jax.ShapeDtypeStruct((batch * seq, d), BF16),
        grid=(batch,),
        in_specs=[
            pl.BlockSpec(lam_vecs.shape, lambda b: (0, 0)),
            pl.BlockSpec(gain.shape, lambda b: (0, 0)),
            blk, blk, blk,
        ],
        out_specs=blk,
        compiler_params=_params(("arbitrary",)),
        name="diff_attn_ctx",
    )(lam_vecs, gain, q, k, v)


def _attn_sample(lam_vecs, gain, q, k, v, kc, vc, batch, seq, n_ctx, row_offset, lam_init):
    d = q.shape[1]
    tq = Q_TILE
    qpt = seq // tq
    off_seq = row_offset // seq
    off_tile = row_offset // tq
    qmap = lambda b, h, i: (off_tile + b * qpt + i, h)
    kmap = lambda b, h, i: (off_seq + b, h)
    cmap = lambda b, h, i: (b, h)
    return pl.pallas_call(
        functools.partial(_attn_latent_kernel, lam_init=lam_init),
        out_shape=jax.ShapeDtypeStruct((batch * seq, d), BF16),
        grid=(batch, N_HEADS, qpt),
        scratch_shapes=[pltpu.VMEM((seq + n_ctx, V_DIM), BF16), pltpu.VMEM((seq + n_ctx, 2 * V_DIM), BF16)],
        in_specs=[
            pl.BlockSpec(lam_vecs.shape, lambda b, h, i: (0, 0)),
            pl.BlockSpec(gain.shape, lambda b, h, i: (0, 0)),
            pl.BlockSpec((tq, V_DIM), qmap),
            pl.BlockSpec((seq, V_DIM), kmap),
            pl.BlockSpec((seq, V_DIM), kmap),
            pl.BlockSpec((n_ctx, V_DIM), cmap),
            pl.BlockSpec((n_ctx, V_DIM), cmap),
        ],
        out_specs=pl.BlockSpec((tq, V_DIM), lambda b, h, i: (b * qpt + i, h)),
        compiler_params=_params(("arbitrary", "arbitrary", "arbitrary")),
        name="diff_attn_latent",
    )(lam_vecs, gain, q, k, v, kc, vc)


def _route(lg):
    lane = lax.broadcasted_iota(jnp.int32, lg.shape, 1)
    lane_f = lane.astype(F32)
    neg = jnp.float32(-jnp.inf)
    big = jnp.float32(LANES)
    gl = jnp.where(lane < N_EXPERT_GROUPS, lg, neg)
    gmax = jnp.max(gl, axis=-1, keepdims=True)
    g_prob = 1.0 / jnp.sum(jnp.exp(gl - gmax), axis=-1, keepdims=True)
    g_idx = jnp.min(jnp.where(gl == gmax, lane_f, big), axis=-1, keepdims=True)
    lo = N_EXPERT_GROUPS + EXPERTS_PER_GROUP * g_idx
    el = jnp.where((lane_f >= lo) & (lane_f < lo + EXPERTS_PER_GROUP), lg, neg)
    m1 = jnp.max(el, axis=-1, keepdims=True)
    i1 = jnp.min(jnp.where(el == m1, lane_f, big), axis=-1, keepdims=True)
    el2 = jnp.where(lane_f == i1, neg, el)
    m2 = jnp.max(el2, axis=-1, keepdims=True)
    i2 = jnp.min(jnp.where(el2 == m2, lane_f, big), axis=-1, keepdims=True)
    t = jnp.exp(m2 - m1)
    w1 = g_prob / (1.0 + t)
    w2 = g_prob * t / (1.0 + t)
    out = jnp.where(lane == 0, i1 - N_EXPERT_GROUPS, 0.0)
    out = jnp.where(lane == 1, i2 - N_EXPERT_GROUPS, out)
    out = jnp.where(lane == 2, w1, out)
    out = jnp.where(lane == 3, w2, out)
    return out


def _pack_bf16_pairs(h):
    n = h.shape[1] // 2
    bits = lax.bitcast_convert_type(h.astype(BF16).astype(F32), jnp.uint32)
    return (bits[:, :n] >> 16) | bits[:, n:]


def _unpack_bf16_pairs(p):
    lo = lax.bitcast_convert_type(p << 16, F32)
    hi = lax.bitcast_convert_type(p & jnp.uint32(0xFFFF0000), F32)
    return jnp.concatenate([lo, hi], axis=1).astype(BF16)


def _mix_out_kernel(ap_ref, as_ref, w_ref, x_ref, mod_ref, gb_ref, wr_ref, x1_ref, hp_ref, route_ref, *,
                    n_prompt_tiles):
    a = jnp.where(pl.program_id(0) < n_prompt_tiles, ap_ref[...], as_ref[...])
    out = _dot(a, w_ref[...])
    x1 = _layernorm(DEEPNORM_ALPHA * x_ref[...] + mod_ref[2:3, :] * out) * gb_ref[0:1, :] + gb_ref[1:2, :]
    x1_ref[...] = x1
    h2 = _layernorm(x1) * (1.0 + mod_ref[4:5, :]) + mod_ref[3:4, :]
    hp_ref[...] = _pack_bf16_pairs(h2)
    hi = h2.astype(BF16)
    lo = (h2 - hi.astype(F32)).astype(BF16)
    logits = _dot(hi, wr_ref[0]) + _dot(lo, wr_ref[0]) + _dot(hi, wr_ref[1])
    route_ref[...] = _route(logits)


def _mix_out(a_prompt, a_sample, w, x, mod, gb, wr, n_prompt, dec_seq):
    t_tok, d = x.shape
    tm = TOKEN_TILE
    npt = n_prompt // tm
    cond = functools.partial(_cond_index, n_prompt_tiles=npt, tiles_per_sample=dec_seq // tm)
    row = lambda t: (t, 0)
    return pl.pallas_call(
        functools.partial(_mix_out_kernel, n_prompt_tiles=npt),
        out_shape=(
            jax.ShapeDtypeStruct((t_tok, d), F32),
            jax.ShapeDtypeStruct((t_tok, d // 2), jnp.uint32),
            jax.ShapeDtypeStruct((t_tok, LANES), F32),
        ),
        grid=(t_tok // tm,),
        in_specs=[
            pl.BlockSpec((tm, d), lambda t: (jnp.minimum(t, npt - 1), 0)),
            pl.BlockSpec((tm, d), lambda t: (jnp.maximum(t - npt, 0), 0)),
            pl.BlockSpec(w.shape, lambda t: (0, 0)),
            pl.BlockSpec((tm, d), row),
            pl.BlockSpec((None, N_MOD, d), lambda t: (cond(t), 0, 0)),
            pl.BlockSpec(gb.shape, lambda t: (0, 0)),
            pl.BlockSpec(wr.shape, lambda t: (0, 0, 0)),
        ],
        out_specs=(
            pl.BlockSpec((tm, d), row),
            pl.BlockSpec((tm, d // 2), row),
            pl.BlockSpec((tm, LANES), row),
        ),
        compiler_params=_params(("arbitrary",)),
        name="mix_out_postnorm_router",
    )(a_prompt, a_sample, w, x, mod, gb, wr)


def _sc_workers():
    info = plsc.get_sparse_core_info()
    return info.num_cores, info.num_cores * info.num_subcores


def _sc_pipeline(n_chunks, load, store):
    load(0, 0).start()
    for j in range(n_chunks):
        b = j % 2
        if j + 1 < n_chunks:
            if j >= 1:
                store(j - 1, 1 - b).wait()
            load(j + 1, 1 - b).start()
        load(j, b).wait()
        store(j, b).start()
    if n_chunks >= 2:
        store(n_chunks - 2, n_chunks % 2).wait()
    store(n_chunks - 1, (n_chunks - 1) % 2).wait()


def _sc_row_mover(n, d, dtype, name, body_of):
    n_cores, n_workers = _sc_workers()
    per_worker = n // n_workers
    k = SC_BUFFER_BYTES // (d * 4)
    n_chunks = per_worker // k
    assert n_chunks * k * n_workers == n and k <= SC_MAX_INDEX_CHUNK
    call = pl.kernel(
        body_of(n_cores, per_worker, k, n_chunks),
        out_type=jax.ShapeDtypeStruct((n, d), dtype),
        mesh=plsc.VectorSubcoreMesh(core_axis_name="c", subcore_axis_name="s"),
        scratch_types=[pltpu.VMEM((n_chunks, k), jnp.int32), pltpu.VMEM((2, k, d), dtype),
                       pltpu.SemaphoreType.DMA((2,)), pltpu.SemaphoreType.DMA((2,))],
        name=name,
    )
    return lambda rows, idx: call(rows, idx.reshape(n_workers, n_chunks, k))


def _sc_gather_rows(table, idx):
    def body_of(n_cores, per_worker, k, n_chunks):
        def body(table_hbm, idx_hbm, out_hbm, idx_v, rows_v, gsem, osem):
            wid = lax.axis_index("s") * n_cores + lax.axis_index("c")
            base = wid * per_worker
            pltpu.sync_copy(idx_hbm.at[wid], idx_v)
            _sc_pipeline(
                n_chunks,
                lambda j, b: pltpu.make_async_copy(table_hbm.at[idx_v.at[j]], rows_v.at[b], gsem.at[b]),
                lambda j, b: pltpu.make_async_copy(rows_v.at[b], out_hbm.at[pl.ds(base + j * k, k)],
                                                   osem.at[b]))
        return body
    return _sc_row_mover(idx.shape[0], table.shape[1], table.dtype, "sc_row_gather", body_of)(table, idx)


def _sc_scatter_rows(src, idx):
    n_src = src.shape[0]

    def body_of(n_cores, per_worker, k, n_chunks):
        assert n_src % per_worker == 0

        def body(src_hbm, idx_hbm, out_hbm, idx_v, rows_v, gsem, osem):
            wid = lax.axis_index("s") * n_cores + lax.axis_index("c")
            base = lax.rem(wid * per_worker, n_src)
            pltpu.sync_copy(idx_hbm.at[wid], idx_v)
            _sc_pipeline(
                n_chunks,
                lambda j, b: pltpu.make_async_copy(src_hbm.at[pl.ds(base + j * k, k)], rows_v.at[b],
                                                   gsem.at[b]),
                lambda j, b: pltpu.make_async_copy(rows_v.at[b], out_hbm.at[idx_v.at[j]], osem.at[b]))
        return body
    return _sc_row_mover(idx.shape[0], src.shape[1], src.dtype, "sc_row_scatter", body_of)(src, idx)


def _moe_kernel(ie_ref, it_ref, lo_ref, hi_ref, first_ref, newexp_ref, xs_ref, wg_ref, wu_ref, wd_ref, o_ref,
                wgu_s, wd_s):
    w = pl.program_id(0)
    f = wg_ref.shape[1]

    @pl.when(newexp_ref[w] == 1)
    def _():
        wgu_s[:, :f] = wg_ref[...].astype(BF16)
        wgu_s[:, f:] = wu_ref[...].astype(BF16)
        wd_s[...] = wd_ref[...].astype(BF16)

    x = _unpack_bf16_pairs(xs_ref[...])
    au = _dot(x, wgu_s[...])
    act = (_silu(au[:, :f]) * au[:, f:]).astype(BF16)
    y = _dot(act, wd_s[...])
    tm = xs_ref.shape[0]
    row = it_ref[w] * tm + lax.broadcasted_iota(jnp.int32, (tm, 1), 0)
    y = jnp.where((row >= lo_ref[w]) & (row < hi_ref[w]), y, 0.0)

    @pl.when(first_ref[w] == 1)
    def _():
        o_ref[...] = y

    @pl.when(first_ref[w] == 0)
    def _():
        o_ref[...] += y


def _moe_experts(plan, xs, w_gate, w_up, w_down, layer):
    ie, it, lo, hi, first, newexp = plan
    n_rows, dp = xs.shape
    _, _, d, f = w_gate.shape
    tm = MOE_TILE
    wmap = lambda w, ie, it, lo, hi, fi, ne: (layer, ie[w], 0, 0)
    rmap = lambda w, ie, it, lo, hi, fi, ne: (it[w], 0)
    return pl.pallas_call(
        _moe_kernel,
        out_shape=jax.ShapeDtypeStruct((n_rows, d), F32),
        grid_spec=pltpu.PrefetchScalarGridSpec(
            num_scalar_prefetch=6,
            grid=(ie.shape[0],),
            in_specs=[
                pl.BlockSpec((tm, dp), rmap),
                pl.BlockSpec((None, None, d, f), wmap),
                pl.BlockSpec((None, None, d, f), wmap),
                pl.BlockSpec((None, None, f, d), wmap),
            ],
            out_specs=pl.BlockSpec((tm, d), rmap),
            scratch_shapes=[pltpu.VMEM((d, 2 * f), BF16), pltpu.VMEM((f, d), BF16)],
        ),
        compiler_params=_params(("arbitrary",)),
        name="moe_grouped_mlp",
    )(ie, it, lo, hi, first, newexp, xs, w_gate, w_up, w_down)


def _moe_plan(route):
    t_tok = route.shape[0]
    n_assign = 2 * t_tok
    tm = MOE_TILE
    n_tiles = n_assign // tm
    n_items = n_tiles + N_EXPERTS - 1
    ef = route[:, :2].astype(jnp.int32).T.reshape(-1)
    onehot = (ef[:, None] == jnp.arange(N_EXPERTS, dtype=jnp.int32)[None, :]).astype(jnp.int32)
    csum = jnp.cumsum(onehot, axis=0)
    counts = csum[-1]
    ends = jnp.cumsum(counts)
    starts = ends - counts
    rank = jnp.sum(csum * onehot, axis=1) - 1
    pos = starts[ef] + rank
    first_tile = starts // tm
    n_it = jnp.where(counts > 0, (ends - 1) // tm - first_tile + 1, 0)
    it_end = jnp.cumsum(n_it)
    it_start = it_end - n_it
    total = it_end[-1]
    w = jnp.arange(n_items, dtype=jnp.int32)
    ex = jnp.minimum(jnp.searchsorted(it_end, w, side="right").astype(jnp.int32), N_EXPERTS - 1)
    valid = w < total
    ex_last = jnp.minimum(jnp.searchsorted(it_end, total - 1, side="right").astype(jnp.int32), N_EXPERTS - 1)
    ex = jnp.where(valid, ex, ex_last)
    tile = jnp.where(valid, first_tile[ex] + (w - it_start[ex]), n_tiles - 1)
    lo = jnp.where(valid, jnp.maximum(starts[ex], tile * tm), 0)
    hi = jnp.where(valid, jnp.minimum(ends[ex], (tile + 1) * tm), 0)
    one = jnp.ones((1,), jnp.int32)
    first = jnp.concatenate([one, (tile[1:] != tile[:-1]).astype(jnp.int32)])
    newexp = jnp.concatenate([one, (ex[1:] != ex[:-1]).astype(jnp.int32)])
    return pos.astype(jnp.int32), (ex, tile.astype(jnp.int32), lo.astype(jnp.int32), hi.astype(jnp.int32), first,
                                   newexp)


def _moe_combine_kernel(x_ref, y0_ref, y1_ref, route_ref, mod_ref, gb_ref, o_ref):
    r = route_ref[...]
    moe = r[:, 2:3] * y0_ref[...] + r[:, 3:4] * y1_ref[...]
    o_ref[...] = (_layernorm(DEEPNORM_ALPHA * x_ref[...] + mod_ref[5:6, :] * moe) * gb_ref[0:1, :]
                  + gb_ref[1:2, :])


def _moe_combine(x1, ys, route, mod, gb, n_prompt, dec_seq):
    t_tok, d = x1.shape
    tm = TOKEN_TILE
    nt = t_tok // tm
    cond = functools.partial(_cond_index, n_prompt_tiles=n_prompt // tm, tiles_per_sample=dec_seq // tm)
    row = lambda t: (t, 0)
    return pl.pallas_call(
        _moe_combine_kernel,
        out_shape=jax.ShapeDtypeStruct((t_tok, d), F32),
        grid=(nt,),
        in_specs=[
            pl.BlockSpec((tm, d), row),
            pl.BlockSpec((tm, d), row),
            pl.BlockSpec((tm, d), lambda t: (t + nt, 0)),
            pl.BlockSpec((tm, LANES), row),
            pl.BlockSpec((None, N_MOD, d), lambda t: (cond(t), 0, 0)),
            pl.BlockSpec(gb.shape, lambda t: (0, 0)),
        ],
        out_specs=pl.BlockSpec((tm, d), row),
        compiler_params=_params(("arbitrary",)),
        name="moe_combine_postnorm",
    )(x1, ys, ys, route, mod, gb)


def _rope_tables(n_lat):
    nf = HEAD_DIM // 4
    s = np.arange(n_lat)
    lane = np.arange(LANES)
    inv = ROPE_BASE ** (-(lane % nf).astype(np.float64) / nf)
    use_col = (lane % HEAD_DIM) >= HEAD_DIM // 2
    p = np.where(use_col[None, :], (s % GRID_W)[:, None], (s // GRID_W)[:, None]).astype(np.float64)
    ang = p * inv[None, :]
    sign = np.where((lane % (2 * nf)) < nf, -1.0, 1.0)
    return jnp.asarray(np.cos(ang), F32), jnp.asarray(np.sin(ang) * sign[None, :], F32)


def _dft_tables(n):
    k = np.arange(n)
    ang = 2.0 * np.pi * ((k[:, None] * k[None, :]) % n).astype(np.float64) / n
    return np.cos(ang), np.sin(ang)


def kernel(x_prompt, x_sample, cache_k, cache_v, c, c_ctx, w_ada, b_ada, ln_gain, ln_bias, w_qkv, w_attn_out,
           lambda_q1, lambda_k1, lambda_q2, lambda_k2, subln_gain, w_fourier_out, w_router_group,
           w_router_expert, w_expert_gate, w_expert_up, w_expert_down):
    bp, sp, d = x_prompt.shape
    bs, n_lat, _ = x_sample.shape
    n_ctx = cache_k.shape[2]
    n_prompt = bp * sp
    t_tok = n_prompt + bs * n_lat
    assert d == D_MODEL and n_prompt % n_lat == 0 and n_lat % TOKEN_TILE == 0 and sp % 16 == 0

    cond = jnp.concatenate([c_ctx[None, :], c, jnp.zeros((8 - 1 - bs, d), F32)], axis=0)
    mods = _ada_all(cond, w_ada, b_ada).reshape(DEPTH, 8, N_MOD, d)

    cos, sin = _rope_tables(n_lat)
    cc, sc = _dft_tables(FOURIER_GROUP_DIM)
    dcs = jnp.asarray(np.concatenate([cc, sc], axis=1), BF16)
    seq_tabs = {s: tuple(jnp.asarray(m, BF16) for m in _dft_tables(s)) for s in (sp, n_lat)}

    x = jnp.concatenate([x_prompt.reshape(n_prompt, d), x_sample.reshape(bs * n_lat, d)], axis=0)
    new_k, new_v = [], []
    for i in range(DEPTH):
        mod = mods[i]
        if i % 2 == 0:
            a = i // 2
            lam_init = 0.8 - 0.6 * math.exp(-0.3 * i)
            q, k, v, kf, vf = _qkv(x, mod, w_qkv[a].astype(BF16), cos, sin, n_prompt, n_lat)
            new_k.append(kf.reshape(bp, sp, N_HEADS, 2 * HEAD_DIM))
            new_v.append(vf.reshape(bp, sp, N_HEADS, V_DIM))
            lam_vecs = jnp.stack([lambda_q1[a], lambda_k1[a], lambda_q2[a], lambda_k2[a]], axis=0)
            gain = subln_gain[a][None, :]
            kc = cache_k[:, a].reshape(bs * n_ctx, d).astype(BF16)
            vc = cache_v[:, a].reshape(bs * n_ctx, d).astype(BF16)
            mixed_p = _attn_prompt(lam_vecs, gain, q, k, v, bp, sp, lam_init)
            mixed_s = _attn_sample(lam_vecs, gain, q, k, v, kc, vc, bs, n_lat, n_ctx, n_prompt, lam_init)
            w_mix = w_attn_out[a]
        else:
            xc, xs = _chan_dft(x, mod, dcs, n_prompt, n_lat)
            mixed_p = _seq_dft(*seq_tabs[sp], xc, xs, bp, sp, 0)
            mixed_s = _seq_dft(*seq_tabs[n_lat], xc, xs, bs, n_lat, n_prompt)
            w_mix = w_fourier_out[i // 2]
        gb0 = jnp.stack([ln_gain[i, 0], ln_bias[i, 0]], axis=0)
        gb1 = jnp.stack([ln_gain[i, 1], ln_bias[i, 1]], axis=0)
        wr = jnp.concatenate([w_router_group[i], w_router_expert[i],
                              jnp.zeros((d, LANES - N_EXPERT_GROUPS - N_EXPERTS), F32)], axis=1)
        wr_hi = wr.astype(BF16)
        wr_lo = (wr - wr_hi.astype(F32)).astype(BF16)
        x1, hp, route = _mix_out(mixed_p, mixed_s, w_mix.astype(BF16), x, mod, gb0,
                                 jnp.stack([wr_hi, wr_lo], axis=0), n_prompt, n_lat)
        pos, plan = _moe_plan(route)
        xs_sorted = _sc_scatter_rows(hp, pos)
        ys = _moe_experts(plan, xs_sorted, w_expert_gate, w_expert_up, w_expert_down, i)
        yg = _sc_gather_rows(ys, pos)
        x = _moe_combine(x1, yg, route, mod, gb1, n_prompt, n_lat)

    y_prompt = x[:n_prompt].reshape(bp, sp, d)
    y_sample = x[n_prompt:].reshape(bs, n_lat, d)
    return (y_prompt, y_sample, jnp.stack(new_k, axis=1), jnp.stack(new_v, axis=1))
```

```python
import functools
import math

import numpy as np
import jax
import jax.numpy as jnp
from jax import lax
from jax.experimental import pallas as pl
from jax.experimental.pallas import tpu as pltpu
from jax.experimental.pallas import tpu_sc as plsc

F32 = jnp.float32
BF16 = jnp.bfloat16
I32 = jnp.int32

D_MODEL = 1024
DEPTH = 4
GRID_W = 64
N_HEADS = 8
HEAD_DIM = 64
V_DIM = 2 * HEAD_DIM
ROPE_BASE = 10000.0
N_FOURIER_GROUPS = 8
FOURIER_GROUP_DIM = D_MODEL // N_FOURIER_GROUPS
N_EXPERT_GROUPS = 4
EXPERTS_PER_GROUP = 8
N_EXPERTS = N_EXPERT_GROUPS * EXPERTS_PER_GROUP
D_EXPERT = 256
N_MOD = 6
LN_EPS = 1e-5
DEEPNORM_ALPHA = (2.0 * DEPTH) ** 0.25
Q_SCALE = math.log2(math.e) * HEAD_DIM ** -0.5

LANES = 128
SUBLANES = 8
TOKEN_TILE = 256
Q_TILE = 256
ATTN_ROW_CHUNK = 128
MOE_TILE = 512
PLAN_TILE = 1024
SC_BUFFER_BYTES = 128 * 1024
SC_MAX_INDEX_CHUNK = 128
VMEM_LIMIT = 48 * 1024 * 1024


def _params(semantics):
    return pltpu.CompilerParams(dimension_semantics=semantics, vmem_limit_bytes=VMEM_LIMIT)


def _layernorm(x):
    mu = jnp.mean(x, axis=-1, keepdims=True)
    xc = x - mu
    var = jnp.mean(xc * xc, axis=-1, keepdims=True)
    return xc * lax.rsqrt(var + LN_EPS)


def _silu(a):
    return a / (1.0 + jnp.exp(-a))


def _dot(a, b):
    return jnp.dot(a, b, preferred_element_type=F32)


class _Tiles:
    def __init__(self, n_prompt, dec_seq, tm):
        self.tm = tm
        self.npt = n_prompt // tm
        self.tps = dec_seq // tm

    def row(self, t):
        return (t, 0)

    def cond(self, t):
        return (jnp.where(t < self.npt, 0, (t - self.npt) // self.tps + 1), 0, 0)

    def latent_pos(self, t):
        return (jnp.maximum(t - self.npt, 0) % self.tps, 0)

    def prompt_part(self, t):
        return (jnp.minimum(t, self.npt - 1), 0)

    def latent_part(self, t):
        return (jnp.maximum(t - self.npt, 0), 0)

    def source_specs(self, arrays, d):
        if len(arrays) == 1:
            return [pl.BlockSpec((self.tm, d), self.row)]
        return [pl.BlockSpec((self.tm, d), self.prompt_part), pl.BlockSpec((self.tm, d), self.latent_part)]


def _read_rows(refs, n_prompt_tiles):
    if len(refs) == 1:
        return refs[0][...]
    return jnp.where(pl.program_id(0) < n_prompt_tiles, refs[0][...], refs[1][...])


def _ada_kernel(cond_ref, w_ref, b_ref, o_ref):
    a = _silu(cond_ref[...])
    o_ref[...] = _dot(a.astype(BF16), w_ref[...].astype(BF16)) + b_ref[...]


def _ada_all(cond, w_ada, b_ada):
    depth, d, n = w_ada.shape
    tn = n // 4
    return pl.pallas_call(
        _ada_kernel,
        out_shape=jax.ShapeDtypeStruct((depth, cond.shape[0], n), F32),
        grid=(depth, n // tn),
        in_specs=[
            pl.BlockSpec(cond.shape, lambda l, j: (0, 0)),
            pl.BlockSpec((None, d, tn), lambda l, j: (l, 0, j)),
            pl.BlockSpec((None, 1, tn), lambda l, j: (l, 0, j)),
        ],
        out_specs=pl.BlockSpec((None, cond.shape[0], tn), lambda l, j: (l, 0, j)),
        compiler_params=_params(("arbitrary", "arbitrary")),
        name="ada",
    )(cond, w_ada, b_ada.reshape(depth, 1, n))


def _rope(x, cos, sin_signed, first_half):
    outs = []
    for c in range(x.shape[1] // LANES):
        xc = x[:, c * LANES:(c + 1) * LANES]
        partner = jnp.where(first_half, pltpu.roll(xc, LANES - 16, 1), pltpu.roll(xc, 16, 1))
        outs.append(xc * cos + partner * sin_signed)
    return jnp.concatenate(outs, axis=1)


def _store_heads(cache_ref, slot, rows):
    for hd in range(N_HEADS):
        cache_ref[slot, :, hd, :] = rows[:, hd * V_DIM:(hd + 1) * V_DIM]


def _qkv_kernel(*refs, n_prompt_tiles, n_x, n_prev):
    x_refs, (mod_ref, w_ref, cos_ref, sin_ref) = refs[:n_x], refs[n_x:n_x + 4]
    prev = refs[n_x + 4:n_x + 4 + 2 * max(n_prev, 0)]
    q_ref, k_ref, v_ref, ko_ref, vo_ref = refs[n_x + 4 + 2 * max(n_prev, 0):]
    t = pl.program_id(0)
    h = _layernorm(_read_rows(x_refs, n_prompt_tiles)) * (1.0 + mod_ref[1:2, :]) + mod_ref[0:1, :]
    acc = _dot(h.astype(BF16), w_ref[...])
    d = w_ref.shape[0]
    q = acc[:, :d] * Q_SCALE
    k = acc[:, d:2 * d]
    v = acc[:, 2 * d:]
    v_ref[...] = v.astype(BF16)

    @pl.when(t < n_prompt_tiles)
    def _():
        q_ref[...] = q.astype(BF16)
        k_ref[...] = k.astype(BF16)
        if n_prev < 0:
            ko_ref[...] = k
            vo_ref[...] = v
        else:
            for a in range(n_prev):
                _store_heads(ko_ref, a, prev[2 * a][...])
                _store_heads(vo_ref, a, prev[2 * a + 1][...])
            _store_heads(ko_ref, n_prev, k)
            _store_heads(vo_ref, n_prev, v)

    @pl.when(t >= n_prompt_tiles)
    def _():
        lane = lax.broadcasted_iota(I32, (q.shape[0], LANES), 1)
        first_half = (lane % 32) < 16
        cos = cos_ref[...]
        sin = sin_ref[...]
        q_ref[...] = _rope(q, cos, sin, first_half).astype(BF16)
        k_ref[...] = _rope(k, cos, sin, first_half).astype(BF16)


def _qkv(xs, mod, w_qkv, cos, sin, prev_kv, finish_cache, n_prompt, dec_seq, t_tok):
    d = w_qkv.shape[0]
    tl = _Tiles(n_prompt, dec_seq, TOKEN_TILE)
    tm = tl.tm
    n_prev = len(prev_kv) if finish_cache else -1
    if finish_cache:
        n_slots = n_prev + 1
        kv_shape = jax.ShapeDtypeStruct((n_prompt // tm, n_slots, tm, N_HEADS, V_DIM), F32)
        kv_spec = pl.BlockSpec((None, n_slots, tm, N_HEADS, V_DIM), lambda t: (tl.prompt_part(t)[0], 0, 0, 0, 0))
    else:
        kv_shape = jax.ShapeDtypeStruct((n_prompt, d), F32)
        kv_spec = pl.BlockSpec((tm, d), tl.prompt_part)
    prev_flat = [a for kv in prev_kv for a in kv] if finish_cache else []
    return pl.pallas_call(
        functools.partial(_qkv_kernel, n_prompt_tiles=tl.npt, n_x=len(xs), n_prev=n_prev),
        out_shape=(
            jax.ShapeDtypeStruct((t_tok, d), BF16),
            jax.ShapeDtypeStruct((t_tok, d), BF16),
            jax.ShapeDtypeStruct((t_tok, d), BF16),
            kv_shape, kv_shape,
        ),
        grid=(t_tok // tm,),
        in_specs=tl.source_specs(xs, d) + [
            pl.BlockSpec((None, N_MOD, d), tl.cond),
            pl.BlockSpec(w_qkv.shape, lambda t: (0, 0)),
            pl.BlockSpec((tm, LANES), tl.latent_pos),
            pl.BlockSpec((tm, LANES), tl.latent_pos),
        ] + [pl.BlockSpec((tm, d), tl.prompt_part) for _ in prev_flat],
        out_specs=(
            pl.BlockSpec((tm, d), tl.row),
            pl.BlockSpec((tm, d), tl.row),
            pl.BlockSpec((tm, d), tl.row),
            kv_spec, kv_spec,
        ),
        compiler_params=_params(("arbitrary",)),
        name="ln_qkv_rope",
    )(*xs, mod, w_qkv, cos, sin, *prev_flat)


def _chan_dft_kernel(x_ref, mod_ref, dcs_ref, xc_ref, xs_ref):
    h = (_layernorm(x_ref[...]) * (1.0 + mod_ref[1:2, :]) + mod_ref[0:1, :]).astype(BF16)
    g = FOURIER_GROUP_DIM
    for i in range(N_FOURIER_GROUPS):
        r = _dot(h[:, i * g:(i + 1) * g], dcs_ref[...])
        xc_ref[:, i * g:(i + 1) * g] = r[:, :g].astype(BF16)
        xs_ref[:, i * g:(i + 1) * g] = r[:, g:].astype(BF16)


def _chan_dft(x, mod, dcs, n_prompt, dec_seq):
    t_tok, d = x.shape
    tl = _Tiles(n_prompt, dec_seq, TOKEN_TILE)
    return pl.pallas_call(
        _chan_dft_kernel,
        out_shape=(jax.ShapeDtypeStruct((t_tok, d), BF16), jax.ShapeDtypeStruct((t_tok, d), BF16)),
        grid=(t_tok // tl.tm,),
        in_specs=[
            pl.BlockSpec((tl.tm, d), tl.row),
            pl.BlockSpec((None, N_MOD, d), tl.cond),
            pl.BlockSpec(dcs.shape, lambda t: (0, 0)),
        ],
        out_specs=(pl.BlockSpec((tl.tm, d), tl.row), pl.BlockSpec((tl.tm, d), tl.row)),
        compiler_params=_params(("arbitrary",)),
        name="ln_chan_dft",
    )(x, mod, dcs)


def _seq_dft_kernel(cs_ref, ss_ref, xc_ref, xs_ref, o_ref, *, norm):
    f = _dot(cs_ref[...], xc_ref[...]) - _dot(ss_ref[...], xs_ref[...])
    o_ref[...] = (f * norm).astype(BF16)


def _seq_dft(cs, ss, xc, xs, batch, seq, row_offset):
    d = xc.shape[1]
    tm = min(TOKEN_TILE, seq)
    spt = seq // tm
    off_seq = row_offset // seq
    return pl.pallas_call(
        functools.partial(_seq_dft_kernel, norm=1.0 / math.sqrt(seq * FOURIER_GROUP_DIM)),
        out_shape=jax.ShapeDtypeStruct((batch * seq, d), BF16),
        grid=(batch, spt),
        in_specs=[
            pl.BlockSpec((tm, seq), lambda b, i: (i, 0)),
            pl.BlockSpec((tm, seq), lambda b, i: (i, 0)),
            pl.BlockSpec((seq, d), lambda b, i: (off_seq + b, 0)),
            pl.BlockSpec((seq, d), lambda b, i: (off_seq + b, 0)),
        ],
        out_specs=pl.BlockSpec((tm, d), lambda b, i: (b * spt + i, 0)),
        compiler_params=_params(("arbitrary", "arbitrary")),
        name=f"seq_dft_{seq}",
    )(cs, ss, xc, xs)


def _diff_lambda(lam_ref, lam_init):
    lv = lam_ref[...]
    return (jnp.exp(jnp.sum(lv[0:1] * lv[1:2], axis=-1, keepdims=True))
            - jnp.exp(jnp.sum(lv[2:3] * lv[3:4], axis=-1, keepdims=True)) + lam_init)


def _diff_attn_head(q, k, v_ext, lam, gain, lam_init):
    tq = q.shape[0]
    lane = lax.broadcasted_iota(I32, q.shape, 1)
    zero = jnp.zeros_like(q)
    qq = jnp.concatenate([jnp.where(lane < HEAD_DIM, q, zero), jnp.where(lane >= HEAD_DIM, q, zero)], axis=0)
    parts = []
    for r in range(0, 2 * tq, ATTN_ROW_CHUNK):
        s = lax.dot_general(qq[r:r + ATTN_ROW_CHUNK], k, (((1,), (1,)), ((), ())), preferred_element_type=F32)
        e = jnp.exp2(s - jnp.max(s, axis=-1, keepdims=True)).astype(BF16)
        parts.append(_dot(e, v_ext))
    oe = jnp.concatenate(parts, axis=0)
    o = oe[:, :V_DIM] / oe[:, V_DIM:]
    o = o[:tq] - lam * o[tq:]
    o = o * lax.rsqrt(jnp.mean(o * o, axis=-1, keepdims=True) + LN_EPS)
    return o * gain * (1.0 - lam_init)


def _attn_prompt_kernel(lam_ref, gain_ref, q_ref, k_ref, v_ref, o_ref, *, lam_init):
    lam = _diff_lambda(lam_ref, lam_init)
    gain = gain_ref[...]
    ones = jnp.ones((k_ref.shape[0], V_DIM), BF16)
    for hd in range(N_HEADS):
        cols = slice(hd * V_DIM, (hd + 1) * V_DIM)
        v_ext = jnp.concatenate([v_ref[:, cols], ones], axis=1)
        o_ref[:, cols] = _diff_attn_head(q_ref[:, cols], k_ref[:, cols], v_ext, lam, gain, lam_init).astype(BF16)


def _attn_latent_kernel(lam_ref, gain_ref, q_ref, k_ref, v_ref, kc_ref, vc_ref, o_ref, kall_ref, vext_ref, *,
                        lam_init):
    n_new = k_ref.shape[0]

    @pl.when(pl.program_id(2) == 0)
    def _():
        kall_ref[:n_new, :] = k_ref[...]
        kall_ref[n_new:, :] = kc_ref[...]
        vext_ref[:n_new, :V_DIM] = v_ref[...]
        vext_ref[n_new:, :V_DIM] = vc_ref[...]
        vext_ref[:, V_DIM:] = jnp.ones((vext_ref.shape[0], V_DIM), BF16)

    o = _diff_attn_head(q_ref[...], kall_ref[...], vext_ref[...], _diff_lambda(lam_ref, lam_init), gain_ref[...],
                        lam_init)
    o_ref[...] = o.astype(BF16)


def _attn_prompt(lam_vecs, gain, q, k, v, batch, seq, lam_init):
    d = q.shape[1]
    blk = pl.BlockSpec((seq, d), lambda b: (b, 0))
    return pl.pallas_call(
        functools.partial(_attn_prompt_kernel, lam_init=lam_init),
        out_shape=jax.ShapeDtypeStruct((batch * seq, d), BF16),
        grid=(batch,),
        in_specs=[
            pl.BlockSpec(lam_vecs.shape, lambda b: (0, 0)),
            pl.BlockSpec(gain.shape, lambda b: (0, 0)),
            blk, blk, blk,
        ],
        out_specs=blk,
        compiler_params=_params(("arbitrary",)),
        name="diff_attn_ctx",
    )(lam_vecs, gain, q, k, v)


def _attn_sample(lam_vecs, gain, q, k, v, kc, vc, batch, seq, n_ctx, row_offset, lam_init):
    d = q.shape[1]
    tq = Q_TILE
    qpt = seq // tq
    off_seq = row_offset // seq
    off_tile = row_offset // tq
    qmap = lambda b, h, i: (off_tile + b * qpt + i, h)
    kmap = lambda b, h, i: (off_seq + b, h)
    cmap = lambda b, h, i: (b, h)
    return pl.pallas_call(
        functools.partial(_attn_latent_kernel, lam_init=lam_init),
        out_shape=jax.ShapeDtypeStruct((batch * seq, d), BF16),
        grid=(batch, N_HEADS, qpt),
        in_specs=[
            pl.BlockSpec(lam_vecs.shape, lambda b, h, i: (0, 0)),
            pl.BlockSpec(gain.shape, lambda b, h, i: (0, 0)),
            pl.BlockSpec((tq, V_DIM), qmap),
            pl.BlockSpec((seq, V_DIM), kmap),
            pl.BlockSpec((seq, V_DIM), kmap),
            pl.BlockSpec((n_ctx, V_DIM), cmap),
            pl.BlockSpec((n_ctx, V_DIM), cmap),
        ],
        out_specs=pl.BlockSpec((tq, V_DIM), lambda b, h, i: (b * qpt + i, h)),
        scratch_shapes=[pltpu.VMEM((seq + n_ctx, V_DIM), BF16), pltpu.VMEM((seq + n_ctx, 2 * V_DIM), BF16)],
        compiler_params=_params(("arbitrary", "arbitrary", "arbitrary")),
        name="diff_attn_latent",
    )(lam_vecs, gain, q, k, v, kc, vc)


def _route(lg):
    lane = lax.broadcasted_iota(I32, lg.shape, 1)
    lane_f = lane.astype(F32)
    neg = jnp.float32(-jnp.inf)
    big = jnp.float32(LANES)
    gl = jnp.where(lane < N_EXPERT_GROUPS, lg, neg)
    gmax = jnp.max(gl, axis=-1, keepdims=True)
    g_prob = 1.0 / jnp.sum(jnp.exp(gl - gmax), axis=-1, keepdims=True)
    g_idx = jnp.min(jnp.where(gl == gmax, lane_f, big), axis=-1, keepdims=True)
    lo = N_EXPERT_GROUPS + EXPERTS_PER_GROUP * g_idx
    el = jnp.where((lane_f >= lo) & (lane_f < lo + EXPERTS_PER_GROUP), lg, neg)
    m1 = jnp.max(el, axis=-1, keepdims=True)
    i1 = jnp.min(jnp.where(el == m1, lane_f, big), axis=-1, keepdims=True)
    el2 = jnp.where(lane_f == i1, neg, el)
    m2 = jnp.max(el2, axis=-1, keepdims=True)
    i2 = jnp.min(jnp.where(el2 == m2, lane_f, big), axis=-1, keepdims=True)
    t = jnp.exp(m2 - m1)
    w1 = g_prob / (1.0 + t)
    w2 = g_prob * t / (1.0 + t)
    out = jnp.where(lane == 0, i1 - N_EXPERT_GROUPS, 0.0)
    out = jnp.where(lane == 1, i2 - N_EXPERT_GROUPS, out)
    out = jnp.where(lane == 2, w1, out)
    out = jnp.where(lane == 3, w2, out)
    return out


def _pack_bf16_pairs(h):
    n = h.shape[1] // 2
    bits = lax.bitcast_convert_type(h.astype(BF16).astype(F32), jnp.uint32)
    return (bits[:, :n] >> 16) | bits[:, n:]


def _unpack_bf16_pairs(p):
    lo = lax.bitcast_convert_type(p << 16, F32)
    hi = lax.bitcast_convert_type(p & jnp.uint32(0xFFFF0000), F32)
    return jnp.concatenate([lo, hi], axis=1).astype(BF16)


def _mix_out_kernel(*refs, n_prompt_tiles, n_x):
    ap_ref, as_ref, w_ref = refs[:3]
    x_refs = refs[3:3 + n_x]
    mod_ref, gb_ref, wr_ref, x1_ref, hp_ref, route_ref = refs[3 + n_x:]
    a = jnp.where(pl.program_id(0) < n_prompt_tiles, ap_ref[...], as_ref[...])
    out = _dot(a, w_ref[...])
    x = _read_rows(x_refs, n_prompt_tiles)
    x1 = _layernorm(DEEPNORM_ALPHA * x + mod_ref[2:3, :] * out) * gb_ref[0:1, :] + gb_ref[1:2, :]
    x1_ref[...] = x1
    h2 = _layernorm(x1) * (1.0 + mod_ref[4:5, :]) + mod_ref[3:4, :]
    hp_ref[...] = _pack_bf16_pairs(h2)
    hi = h2.astype(BF16)
    lo = (h2 - hi.astype(F32)).astype(BF16)
    logits = _dot(hi, wr_ref[0]) + _dot(lo, wr_ref[0]) + _dot(hi, wr_ref[1])
    route_ref[...] = _route(logits)


def _mix_out(a_prompt, a_sample, w, xs, mod, gb, wr, n_prompt, dec_seq, t_tok):
    d = w.shape[1]
    tl = _Tiles(n_prompt, dec_seq, TOKEN_TILE)
    tm = tl.tm
    return pl.pallas_call(
        functools.partial(_mix_out_kernel, n_prompt_tiles=tl.npt, n_x=len(xs)),
        out_shape=(
            jax.ShapeDtypeStruct((t_tok, d), F32),
            jax.ShapeDtypeStruct((t_tok, d // 2), jnp.uint32),
            jax.ShapeDtypeStruct((t_tok, LANES), F32),
        ),
        grid=(t_tok // tm,),
        in_specs=[
            pl.BlockSpec((tm, d), tl.prompt_part),
            pl.BlockSpec((tm, d), tl.latent_part),
            pl.BlockSpec(w.shape, lambda t: (0, 0)),
        ] + tl.source_specs(xs, d) + [
            pl.BlockSpec((None, N_MOD, d), tl.cond),
            pl.BlockSpec(gb.shape, lambda t: (0, 0)),
            pl.BlockSpec(wr.shape, lambda t: (0, 0, 0)),
        ],
        out_specs=(
            pl.BlockSpec((tm, d), tl.row),
            pl.BlockSpec((tm, d // 2), tl.row),
            pl.BlockSpec((tm, LANES), tl.row),
        ),
        compiler_params=_params(("arbitrary",)),
        name="mix_out_postnorm_router",
    )(a_prompt, a_sample, w, *xs, mod, gb, wr)


def _lane_prefix_sum(x, lane):
    sh = 1
    while sh < LANES:
        x = x + jnp.where(lane >= sh, pltpu.roll(x, sh, 1), 0.0)
        sh *= 2
    return x


def _work_items(counts, starts, ends, lane8, n_moe_tiles):
    shift = int(math.log2(MOE_TILE))
    first_tile = (starts.astype(I32) >> shift).astype(F32)
    last_tile = ((ends.astype(I32) - 1) >> shift).astype(F32)
    n_it = jnp.where(counts > 0.0, last_tile - first_tile + 1.0, 0.0)
    it_end = _lane_prefix_sum(n_it, lane8)
    it_start = it_end - n_it
    total = it_end[0:1, LANES - 1:LANES]
    sub = lax.broadcasted_iota(I32, (LANES, LANES), 0)
    rows = lambda x: jnp.broadcast_to(x[0:1, :], (LANES, LANES))
    stacked = jnp.where(sub == 0, rows(starts), jnp.where(sub == 1, rows(ends), jnp.where(
        sub == 2, rows(first_tile), jnp.where(sub == 3, rows(it_start), jnp.where(sub == 4, rows(it_end), 0.0)))))
    cols = stacked.T
    col = lambda j: cols[:, j:j + 1]
    sub_f = sub.astype(F32)
    w = lax.broadcasted_iota(I32, (LANES, LANES), 1).astype(F32)
    ex = jnp.sum(jnp.where((sub < N_EXPERTS) & (col(4) <= w), 1.0, 0.0), axis=0, keepdims=True)
    ex = jnp.minimum(ex, N_EXPERTS - 1.0)
    w1 = w[0:1, :]
    valid = w1 < total
    ex = jnp.where(valid, ex, jnp.max(jnp.where(valid, ex, 0.0), axis=-1, keepdims=True))
    onehot = sub_f == ex
    pick = lambda j: jnp.sum(jnp.where(onehot, col(j), 0.0), axis=0, keepdims=True)
    tile = jnp.where(valid, pick(2) + (w1 - pick(3)), n_moe_tiles - 1.0)
    lo = jnp.where(valid, jnp.maximum(pick(0), tile * MOE_TILE), 0.0)
    hi = jnp.where(valid, jnp.minimum(pick(1), (tile + 1.0) * MOE_TILE), 0.0)
    b8 = lambda x: jnp.broadcast_to(x, (SUBLANES, LANES))
    ex8, tile8 = b8(ex), b8(tile)
    first = jnp.where((lane8 == 0) | (tile8 != pltpu.roll(tile8, 1, 1)), 1.0, 0.0)
    newexp = jnp.where((lane8 == 0) | (ex8 != pltpu.roll(ex8, 1, 1)), 1.0, 0.0)
    sub8 = lax.broadcasted_iota(I32, (SUBLANES, LANES), 0)
    return jnp.where(sub8 == 0, ex8, jnp.where(sub8 == 1, tile8, jnp.where(sub8 == 2, b8(lo), jnp.where(
        sub8 == 3, b8(hi), jnp.where(sub8 == 4, first, jnp.where(sub8 == 5, newexp, 0.0))))))


def _plan_kernel(route_ref, pos_ref, items_ref, tri_ref, carry_ref, tot_ref, *, n_moe_tiles):
    p = pl.program_id(0)
    t = pl.program_id(1)
    tm = route_ref.shape[0]
    r = route_ref[...]
    lane = lax.broadcasted_iota(I32, (tm, LANES), 1)
    lane_f = lane.astype(F32)
    e0 = r[:, 0:1]
    e1 = r[:, 1:2] + N_EXPERTS
    m = jnp.where((lane_f == e0) | (lane_f == e1), 1.0, 0.0)
    colsum = jnp.sum(m, axis=0, keepdims=True)

    @pl.when((p == 0) & (t == 0))
    def _():
        carry_ref[...] = jnp.zeros_like(carry_ref)
        row = lax.broadcasted_iota(I32, (tm, tm), 0)
        col = lax.broadcasted_iota(I32, (tm, tm), 1)
        tri_ref[...] = jnp.where(row > col, 1.0, 0.0).astype(BF16)

    @pl.when(p == 0)
    def _():
        carry_ref[...] += colsum

    @pl.when((p == 0) & (t == pl.num_programs(1) - 1))
    def _():
        tot_ref[...] = carry_ref[...]
        carry_ref[...] = jnp.zeros_like(carry_ref)

    @pl.when(p == 1)
    def _():
        lane8 = lax.broadcasted_iota(I32, (SUBLANES, LANES), 1)
        tot = tot_ref[...]
        is_first = lane8 < N_EXPERTS
        tot0 = jnp.where(is_first, tot, 0.0)
        counts = jnp.where(is_first, tot + pltpu.roll(tot, LANES - N_EXPERTS, 1), 0.0)
        ends = _lane_prefix_sum(counts, lane8)
        starts = ends - counts
        base = jnp.where(is_first, starts, pltpu.roll(starts + tot0, N_EXPERTS, 1))
        before = _dot(tri_ref[...], m.astype(BF16)) + carry_ref[0:1, :]
        carry_ref[...] += colsum
        rows = before + base[0:1, :]
        pos0 = jnp.sum(jnp.where(lane_f == e0, rows, 0.0), axis=-1, keepdims=True)
        pos1 = jnp.sum(jnp.where(lane_f == e1, rows, 0.0), axis=-1, keepdims=True)
        both = jnp.where(lane == 0, pos0, jnp.where(lane == 1, pos1, 0.0))
        pos_ref[...] = both.T[0:SUBLANES, :].astype(I32)

        @pl.when(t == 0)
        def _():
            items_ref[...] = _work_items(counts, starts, ends, lane8, n_moe_tiles).astype(I32)


def _moe_plan(route):
    t_tok = route.shape[0]
    tm = PLAN_TILE
    n_moe_tiles = 2 * t_tok // MOE_TILE
    assert n_moe_tiles + N_EXPERTS - 1 <= LANES and 2 * N_EXPERTS <= LANES and MOE_TILE & (MOE_TILE - 1) == 0
    return pl.pallas_call(
        functools.partial(_plan_kernel, n_moe_tiles=n_moe_tiles),
        out_shape=(jax.ShapeDtypeStruct((SUBLANES, t_tok), I32), jax.ShapeDtypeStruct((SUBLANES, LANES), I32)),
        grid=(2, t_tok // tm),
        in_specs=[pl.BlockSpec((tm, LANES), lambda p, t: (t, 0))],
        out_specs=(pl.BlockSpec((SUBLANES, tm), lambda p, t: (0, t * p)),
                   pl.BlockSpec((SUBLANES, LANES), lambda p, t: (0, 0))),
        scratch_shapes=[pltpu.VMEM((tm, tm), BF16), pltpu.VMEM((SUBLANES, LANES), F32),
                        pltpu.VMEM((SUBLANES, LANES), F32)],
        compiler_params=_params(("arbitrary", "arbitrary")),
        name="moe_positions",
    )(route)


def _sc_workers():
    info = plsc.get_sparse_core_info()
    return info.num_cores, info.num_cores * info.num_subcores


def _sc_pipeline(n_chunks, load, store):
    load(0, 0).start()
    for j in range(n_chunks):
        b = j % 2
        if j + 1 < n_chunks:
            if j >= 1:
                store(j - 1, 1 - b).wait()
            load(j + 1, 1 - b).start()
        load(j, b).wait()
        store(j, b).start()
    if n_chunks >= 2:
        store(n_chunks - 2, n_chunks % 2).wait()
    store(n_chunks - 1, (n_chunks - 1) % 2).wait()


def _sc_row_mover(n, d, dtype, name, body_of):
    n_cores, n_workers = _sc_workers()
    per_worker = n // n_workers
    k = SC_BUFFER_BYTES // (d * 4)
    n_chunks = per_worker // k
    assert n_chunks * k * n_workers == n and k <= SC_MAX_INDEX_CHUNK
    call = pl.kernel(
        body_of(n_cores, per_worker, k, n_chunks),
        out_type=jax.ShapeDtypeStruct((n, d), dtype),
        mesh=plsc.VectorSubcoreMesh(core_axis_name="c", subcore_axis_name="s"),
        scratch_types=[pltpu.VMEM((n_chunks, k), I32), pltpu.VMEM((2, k, d), dtype),
                       pltpu.SemaphoreType.DMA((2,)), pltpu.SemaphoreType.DMA((2,))],
        name=name,
    )
    return lambda rows, idx: call(rows, idx.reshape(n_workers, n_chunks, k))


def _sc_gather_rows(table, idx):
    def body_of(n_cores, per_worker, k, n_chunks):
        def body(table_hbm, idx_hbm, out_hbm, idx_v, rows_v, gsem, osem):
            wid = lax.axis_index("s") * n_cores + lax.axis_index("c")
            base = wid * per_worker
            pltpu.sync_copy(idx_hbm.at[wid], idx_v)
            _sc_pipeline(
                n_chunks,
                lambda j, b: pltpu.make_async_copy(table_hbm.at[idx_v.at[j]], rows_v.at[b], gsem.at[b]),
                lambda j, b: pltpu.make_async_copy(rows_v.at[b], out_hbm.at[pl.ds(base + j * k, k)],
                                                   osem.at[b]))
        return body
    return _sc_row_mover(idx.shape[0], table.shape[1], table.dtype, "sc_row_gather", body_of)(table, idx)


def _sc_scatter_rows(src, idx):
    n_src = src.shape[0]

    def body_of(n_cores, per_worker, k, n_chunks):
        assert n_src % per_worker == 0

        def body(src_hbm, idx_hbm, out_hbm, idx_v, rows_v, gsem, osem):
            wid = lax.axis_index("s") * n_cores + lax.axis_index("c")
            base = lax.rem(wid * per_worker, n_src)
            pltpu.sync_copy(idx_hbm.at[wid], idx_v)
            _sc_pipeline(
                n_chunks,
                lambda j, b: pltpu.make_async_copy(src_hbm.at[pl.ds(base + j * k, k)], rows_v.at[b],
                                                   gsem.at[b]),
                lambda j, b: pltpu.make_async_copy(rows_v.at[b], out_hbm.at[idx_v.at[j]], osem.at[b]))
        return body
    return _sc_row_mover(idx.shape[0], src.shape[1], src.dtype, "sc_row_scatter", body_of)(src, idx)


IT_EXPERT, IT_TILE, IT_LO, IT_HI, IT_FIRST, IT_NEWEXP = range(6)


def _moe_kernel(it_ref, xs_ref, wg_ref, wu_ref, wd_ref, o_ref, wgu_s, wd_s):
    w = pl.program_id(0)
    f = wg_ref.shape[1]

    @pl.when(it_ref[IT_NEWEXP, w] == 1)
    def _():
        wgu_s[:, :f] = wg_ref[...].astype(BF16)
        wgu_s[:, f:] = wu_ref[...].astype(BF16)
        wd_s[...] = wd_ref[...].astype(BF16)

    x = _unpack_bf16_pairs(xs_ref[...])
    au = _dot(x, wgu_s[...])
    act = (_silu(au[:, :f]) * au[:, f:]).astype(BF16)
    y = _dot(act, wd_s[...])
    tm = xs_ref.shape[0]
    row = it_ref[IT_TILE, w] * tm + lax.broadcasted_iota(I32, (tm, 1), 0)
    y = jnp.where((row >= it_ref[IT_LO, w]) & (row < it_ref[IT_HI, w]), y, 0.0)

    @pl.when(it_ref[IT_FIRST, w] == 1)
    def _():
        o_ref[...] = y

    @pl.when(it_ref[IT_FIRST, w] == 0)
    def _():
        o_ref[...] += y


def _moe_experts(items, xs, w_gate, w_up, w_down, layer):
    n_rows, dp = xs.shape
    _, _, d, f = w_gate.shape
    tm = MOE_TILE
    wmap = lambda w, it: (layer, it[IT_EXPERT, w], 0, 0)
    rmap = lambda w, it: (it[IT_TILE, w], 0)
    return pl.pallas_call(
        _moe_kernel,
        out_shape=jax.ShapeDtypeStruct((n_rows, d), F32),
        grid_spec=pltpu.PrefetchScalarGridSpec(
            num_scalar_prefetch=1,
            grid=(n_rows // tm + N_EXPERTS - 1,),
            in_specs=[
                pl.BlockSpec((tm, dp), rmap),
                pl.BlockSpec((None, None, d, f), wmap),
                pl.BlockSpec((None, None, d, f), wmap),
                pl.BlockSpec((None, None, f, d), wmap),
            ],
            out_specs=pl.BlockSpec((tm, d), rmap),
            scratch_shapes=[pltpu.VMEM((d, 2 * f), BF16), pltpu.VMEM((f, d), BF16)],
        ),
        compiler_params=_params(("arbitrary",)),
        name="moe_grouped_mlp",
    )(items, xs, w_gate, w_up, w_down)


def _moe_combine_kernel(x_ref, y0_ref, y1_ref, route_ref, mod_ref, gb_ref, *o_refs, n_prompt_tiles):
    r = route_ref[...]
    moe = r[:, 2:3] * y0_ref[...] + r[:, 3:4] * y1_ref[...]
    x2 = _layernorm(DEEPNORM_ALPHA * x_ref[...] + mod_ref[5:6, :] * moe) * gb_ref[0:1, :] + gb_ref[1:2, :]
    if len(o_refs) == 1:
        o_refs[0][...] = x2
    else:
        @pl.when(pl.program_id(0) < n_prompt_tiles)
        def _():
            o_refs[0][...] = x2

        @pl.when(pl.program_id(0) >= n_prompt_tiles)
        def _():
            o_refs[1][...] = x2


def _moe_combine(x1, ys, route, mod, gb, n_prompt, dec_seq, split):
    t_tok, d = x1.shape
    tl = _Tiles(n_prompt, dec_seq, TOKEN_TILE)
    tm = tl.tm
    nt = t_tok // tm
    if split:
        out_shape = (jax.ShapeDtypeStruct((n_prompt, d), F32), jax.ShapeDtypeStruct((t_tok - n_prompt, d), F32))
        out_specs = (pl.BlockSpec((tm, d), tl.prompt_part), pl.BlockSpec((tm, d), tl.latent_part))
    else:
        out_shape = jax.ShapeDtypeStruct((t_tok, d), F32)
        out_specs = pl.BlockSpec((tm, d), tl.row)
    return pl.pallas_call(
        functools.partial(_moe_combine_kernel, n_prompt_tiles=tl.npt),
        out_shape=out_shape,
        grid=(nt,),
        in_specs=[
            pl.BlockSpec((tm, d), tl.row),
            pl.BlockSpec((tm, d), tl.row),
            pl.BlockSpec((tm, d), lambda t: (t + nt, 0)),
            pl.BlockSpec((tm, LANES), tl.row),
            pl.BlockSpec((None, N_MOD, d), tl.cond),
            pl.BlockSpec(gb.shape, lambda t: (0, 0)),
        ],
        out_specs=out_specs,
        compiler_params=_params(("arbitrary",)),
        name="moe_combine_postnorm",
    )(x1, ys, ys, route, mod, gb)


def _rope_tables(n_lat):
    nf = HEAD_DIM // 4
    s = np.arange(n_lat)
    lane = np.arange(LANES)
    inv = ROPE_BASE ** (-(lane % nf).astype(np.float64) / nf)
    use_col = (lane % HEAD_DIM) >= HEAD_DIM // 2
    p = np.where(use_col[None, :], (s % GRID_W)[:, None], (s // GRID_W)[:, None]).astype(np.float64)
    ang = p * inv[None, :]
    sign = np.where((lane % (2 * nf)) < nf, -1.0, 1.0)
    return jnp.asarray(np.cos(ang), F32), jnp.asarray(np.sin(ang) * sign[None, :], F32)


def _dft_tables(n):
    k = np.arange(n)
    ang = 2.0 * np.pi * ((k[:, None] * k[None, :]) % n).astype(np.float64) / n
    return np.cos(ang), np.sin(ang)


def kernel(x_prompt, x_sample, cache_k, cache_v, c, c_ctx, w_ada, b_ada, ln_gain, ln_bias, w_qkv, w_attn_out,
           lambda_q1, lambda_k1, lambda_q2, lambda_k2, subln_gain, w_fourier_out, w_router_group,
           w_router_expert, w_expert_gate, w_expert_up, w_expert_down):
    bp, sp, d = x_prompt.shape
    bs, n_lat, _ = x_sample.shape
    n_ctx = cache_k.shape[2]
    n_prompt = bp * sp
    t_tok = n_prompt + bs * n_lat
    assert d == D_MODEL and n_prompt % n_lat == 0 and n_lat % TOKEN_TILE == 0 and sp == TOKEN_TILE

    cond = jnp.concatenate([c_ctx[None, :], c, jnp.zeros((SUBLANES - 1 - bs, d), F32)], axis=0)
    mods = _ada_all(cond, w_ada, b_ada).reshape(DEPTH, SUBLANES, N_MOD, d)

    cos, sin = _rope_tables(n_lat)
    cc, sc = _dft_tables(FOURIER_GROUP_DIM)
    dcs = jnp.asarray(np.concatenate([cc, sc], axis=1), BF16)
    seq_tabs = {s: tuple(jnp.asarray(m, BF16) for m in _dft_tables(s)) for s in (sp, n_lat)}
    attn_layers = [i for i in range(DEPTH) if i % 2 == 0]

    xs = (x_prompt.reshape(n_prompt, d), x_sample.reshape(bs * n_lat, d))
    prev_kv = []
    new_k = new_v = None
    for i in range(DEPTH):
        mod = mods[i]
        if i % 2 == 0:
            a = i // 2
            lam_init = 0.8 - 0.6 * math.exp(-0.3 * i)
            last_attn = i == attn_layers[-1]
            q, k, v, kf, vf = _qkv(xs, mod, w_qkv[a].astype(BF16), cos, sin, prev_kv, last_attn, n_prompt, n_lat,
                                   t_tok)
            if last_attn:
                new_k, new_v = kf, vf
            else:
                prev_kv.append((kf, vf))
            lam_vecs = jnp.stack([lambda_q1[a], lambda_k1[a], lambda_q2[a], lambda_k2[a]], axis=0)
            gain = subln_gain[a][None, :]
            kc = cache_k[:, a].reshape(bs * n_ctx, d).astype(BF16)
            vc = cache_v[:, a].reshape(bs * n_ctx, d).astype(BF16)
            mixed_p = _attn_prompt(lam_vecs, gain, q, k, v, bp, sp, lam_init)
            mixed_s = _attn_sample(lam_vecs, gain, q, k, v, kc, vc, bs, n_lat, n_ctx, n_prompt, lam_init)
            w_mix = w_attn_out[a]
        else:
            xc, xsn = _chan_dft(xs[0], mod, dcs, n_prompt, n_lat)
            mixed_p = _seq_dft(*seq_tabs[sp], xc, xsn, bp, sp, 0)
            mixed_s = _seq_dft(*seq_tabs[n_lat], xc, xsn, bs, n_lat, n_prompt)
            w_mix = w_fourier_out[i // 2]
        gb0 = jnp.stack([ln_gain[i, 0], ln_bias[i, 0]], axis=0)
        gb1 = jnp.stack([ln_gain[i, 1], ln_bias[i, 1]], axis=0)
        wr = jnp.concatenate([w_router_group[i], w_router_expert[i],
                              jnp.zeros((d, LANES - N_EXPERT_GROUPS - N_EXPERTS), F32)], axis=1)
        wr_hi = wr.astype(BF16)
        wr_lo = (wr - wr_hi.astype(F32)).astype(BF16)
        x1, hp, route = _mix_out(mixed_p, mixed_s, w_mix.astype(BF16), xs, mod, gb0,
                                 jnp.stack([wr_hi, wr_lo], axis=0), n_prompt, n_lat, t_tok)
        pos8, items = _moe_plan(route)
        pos = pos8[0:2].reshape(-1)
        xs_sorted = _sc_scatter_rows(hp, pos)
        ys = _moe_experts(items, xs_sorted, w_expert_gate, w_expert_up, w_expert_down, i)
        yg = _sc_gather_rows(ys, pos)
        last = i == DEPTH - 1
        out = _moe_combine(x1, yg, route, mod, gb1, n_prompt, n_lat, split=last)
        xs = out if last else (out,)

    y_prompt = xs[0].reshape(bp, sp, d)
    y_sample = xs[1].reshape(bs, n_lat, d)
    return (y_prompt, y_sample, new_k.reshape(bp, len(attn_layers), sp, N_HEADS, 2 * HEAD_DIM),
            new_v.reshape(bp, len(attn_layers), sp, N_HEADS, V_DIM))
```

```python
import functools
import math

import numpy as np
import jax
import jax.numpy as jnp
from jax import lax
from jax.experimental import pallas as pl
from jax.experimental.pallas import tpu as pltpu
from jax.experimental.pallas import tpu_sc as plsc

F32 = jnp.float32
BF16 = jnp.bfloat16
I32 = jnp.int32

D_MODEL = 1024
DEPTH = 4
GRID_W = 64
N_HEADS = 8
HEAD_DIM = 64
V_DIM = 2 * HEAD_DIM
ROPE_BASE = 10000.0
N_FOURIER_GROUPS = 8
FOURIER_GROUP_DIM = D_MODEL // N_FOURIER_GROUPS
N_EXPERT_GROUPS = 4
EXPERTS_PER_GROUP = 8
N_EXPERTS = N_EXPERT_GROUPS * EXPERTS_PER_GROUP
D_EXPERT = 256
N_MOD = 6
LN_EPS = 1e-5
DEEPNORM_ALPHA = (2.0 * DEPTH) ** 0.25
Q_SCALE = math.log2(math.e) * HEAD_DIM ** -0.5

LANES = 128
SUBLANES = 8
TOKEN_TILE = 512
CHAIN_ROWS = 256
QKV_TILE = 256
Q_TILE = 256
ATTN_ROW_CHUNK = 128
MOE_TILE = 512
PLAN_TILE = 1024
SC_BUFFER_BYTES = 128 * 1024
SC_MAX_INDEX_CHUNK = 128
VMEM_LIMIT = 48 * 1024 * 1024


def _params(semantics):
    return pltpu.CompilerParams(dimension_semantics=semantics, vmem_limit_bytes=VMEM_LIMIT)


def _layernorm(x, eps=LN_EPS):
    mu = jnp.mean(x, axis=-1, keepdims=True)
    xc = x - mu
    var = jnp.mean(xc * xc, axis=-1, keepdims=True)
    return xc * lax.rsqrt(var + eps)


def _deepnorm(x, branch_gate, branch):
    return _layernorm(x + (branch_gate * (1.0 / DEEPNORM_ALPHA)) * branch, LN_EPS / DEEPNORM_ALPHA ** 2)


def _silu(a):
    return a / (1.0 + jnp.exp(-a))


def _dot(a, b):
    return jnp.dot(a, b, preferred_element_type=F32)


class _Tiles:
    def __init__(self, n_prompt, dec_seq, tm):
        self.tm = tm
        self.npt = n_prompt // tm
        self.tps = dec_seq // tm

    def row(self, t):
        return (t, 0)

    def cond(self, t):
        return (jnp.where(t < self.npt, 0, (t - self.npt) // self.tps + 1), 0, 0)

    def latent_pos(self, t):
        return (jnp.maximum(t - self.npt, 0) % self.tps, 0)

    def prompt_part(self, t):
        return (jnp.minimum(t, self.npt - 1), 0)

    def latent_part(self, t):
        return (jnp.maximum(t - self.npt, 0), 0)

    def source_specs(self, arrays, d):
        if len(arrays) == 1:
            return [pl.BlockSpec((self.tm, d), self.row)]
        return [pl.BlockSpec((self.tm, d), self.prompt_part), pl.BlockSpec((self.tm, d), self.latent_part)]


def _read_rows(refs, n_prompt_tiles, rows=slice(None)):
    if len(refs) == 1:
        return refs[0][rows, :]
    return jnp.where(pl.program_id(0) < n_prompt_tiles, refs[0][rows, :], refs[1][rows, :])


def _ada_kernel(cond_ref, w_ref, b_ref, o_ref):
    a = _silu(cond_ref[...])
    o_ref[...] = _dot(a.astype(BF16), w_ref[...].astype(BF16)) + b_ref[...]


def _ada_all(cond, w_ada, b_ada):
    depth, d, n = w_ada.shape
    tn = n // 4
    return pl.pallas_call(
        _ada_kernel,
        out_shape=jax.ShapeDtypeStruct((depth, cond.shape[0], n), F32),
        grid=(depth, n // tn),
        in_specs=[
            pl.BlockSpec(cond.shape, lambda l, j: (0, 0)),
            pl.BlockSpec((None, d, tn), lambda l, j: (l, 0, j)),
            pl.BlockSpec((None, 1, tn), lambda l, j: (l, 0, j)),
        ],
        out_specs=pl.BlockSpec((None, cond.shape[0], tn), lambda l, j: (l, 0, j)),
        compiler_params=_params(("arbitrary", "arbitrary")),
        name="ada",
    )(cond, w_ada, b_ada.reshape(depth, 1, n))


def _rope(x, cos, sin_signed, first_half):
    outs = []
    for c in range(x.shape[1] // LANES):
        xc = x[:, c * LANES:(c + 1) * LANES]
        partner = jnp.where(first_half, pltpu.roll(xc, LANES - 16, 1), pltpu.roll(xc, 16, 1))
        outs.append(xc * cos + partner * sin_signed)
    return jnp.concatenate(outs, axis=1)


def _store_heads(cache_ref, slot, rows):
    for hd in range(N_HEADS):
        cache_ref[slot, :, hd, :] = rows[:, hd * V_DIM:(hd + 1) * V_DIM]


def _qkv_kernel(*refs, n_prompt_tiles, n_x, n_prev):
    x_refs, (mod_ref, w_ref, cos_ref, sin_ref) = refs[:n_x], refs[n_x:n_x + 4]
    prev = refs[n_x + 4:n_x + 4 + 2 * max(n_prev, 0)]
    q_ref, k_ref, v_ref, ko_ref, vo_ref = refs[n_x + 4 + 2 * max(n_prev, 0):]
    t = pl.program_id(0)
    h = _layernorm(_read_rows(x_refs, n_prompt_tiles)) * (1.0 + mod_ref[1:2, :]) + mod_ref[0:1, :]
    acc = _dot(h.astype(BF16), w_ref[...])
    d = w_ref.shape[0]
    q = acc[:, :d] * Q_SCALE
    k = acc[:, d:2 * d]
    v = acc[:, 2 * d:]
    v_ref[...] = v.astype(BF16)

    @pl.when(t < n_prompt_tiles)
    def _():
        q_ref[...] = q.astype(BF16)
        k_ref[...] = k.astype(BF16)
        if n_prev < 0:
            ko_ref[...] = k
            vo_ref[...] = v
        else:
            for a in range(n_prev):
                _store_heads(ko_ref, a, prev[2 * a][...])
                _store_heads(vo_ref, a, prev[2 * a + 1][...])
            _store_heads(ko_ref, n_prev, k)
            _store_heads(vo_ref, n_prev, v)

    @pl.when(t >= n_prompt_tiles)
    def _():
        lane = lax.broadcasted_iota(I32, (q.shape[0], LANES), 1)
        first_half = (lane % 32) < 16
        cos = cos_ref[...]
        sin = sin_ref[...]
        q_ref[...] = _rope(q, cos, sin, first_half).astype(BF16)
        k_ref[...] = _rope(k, cos, sin, first_half).astype(BF16)


def _qkv(xs, mod, w_qkv, cos, sin, prev_kv, finish_cache, n_prompt, dec_seq, t_tok):
    d = w_qkv.shape[0]
    tl = _Tiles(n_prompt, dec_seq, QKV_TILE)
    tm = tl.tm
    n_prev = len(prev_kv) if finish_cache else -1
    if finish_cache:
        n_slots = n_prev + 1
        kv_shape = jax.ShapeDtypeStruct((n_prompt // tm, n_slots, tm, N_HEADS, V_DIM), F32)
        kv_spec = pl.BlockSpec((None, n_slots, tm, N_HEADS, V_DIM), lambda t: (tl.prompt_part(t)[0], 0, 0, 0, 0))
    else:
        kv_shape = jax.ShapeDtypeStruct((n_prompt, d), F32)
        kv_spec = pl.BlockSpec((tm, d), tl.prompt_part)
    prev_flat = [a for kv in prev_kv for a in kv] if finish_cache else []
    return pl.pallas_call(
        functools.partial(_qkv_kernel, n_prompt_tiles=tl.npt, n_x=len(xs), n_prev=n_prev),
        out_shape=(
            jax.ShapeDtypeStruct((t_tok, d), BF16),
            jax.ShapeDtypeStruct((t_tok, d), BF16),
            jax.ShapeDtypeStruct((t_tok, d), BF16),
            kv_shape, kv_shape,
        ),
        grid=(t_tok // tm,),
        in_specs=tl.source_specs(xs, d) + [
            pl.BlockSpec((None, N_MOD, d), tl.cond),
            pl.BlockSpec(w_qkv.shape, lambda t: (0, 0)),
            pl.BlockSpec((tm, LANES), tl.latent_pos),
            pl.BlockSpec((tm, LANES), tl.latent_pos),
        ] + [pl.BlockSpec((tm, d), tl.prompt_part) for _ in prev_flat],
        out_specs=(
            pl.BlockSpec((tm, d), tl.row),
            pl.BlockSpec((tm, d), tl.row),
            pl.BlockSpec((tm, d), tl.row),
            kv_spec, kv_spec,
        ),
        compiler_params=_params(("arbitrary",)),
        name="ln_qkv_rope",
    )(*xs, mod, w_qkv, cos, sin, *prev_flat)


def _row_chains(n_rows):
    return [slice(r, r + CHAIN_ROWS) for r in range(0, n_rows, CHAIN_ROWS)]


def _chan_dft_kernel(x_ref, mod_ref, dcs_ref, xc_ref, xs_ref):
    g = FOURIER_GROUP_DIM
    for rows in _row_chains(x_ref.shape[0]):
        h = (_layernorm(x_ref[rows, :]) * (1.0 + mod_ref[1:2, :]) + mod_ref[0:1, :]).astype(BF16)
        for i in range(N_FOURIER_GROUPS):
            r = _dot(h[:, i * g:(i + 1) * g], dcs_ref[...])
            xc_ref[rows, i * g:(i + 1) * g] = r[:, :g].astype(BF16)
            xs_ref[rows, i * g:(i + 1) * g] = r[:, g:].astype(BF16)


def _chan_dft(x, mod, dcs, n_prompt, dec_seq):
    t_tok, d = x.shape
    tl = _Tiles(n_prompt, dec_seq, TOKEN_TILE)
    return pl.pallas_call(
        _chan_dft_kernel,
        out_shape=(jax.ShapeDtypeStruct((t_tok, d), BF16), jax.ShapeDtypeStruct((t_tok, d), BF16)),
        grid=(t_tok // tl.tm,),
        in_specs=[
            pl.BlockSpec((tl.tm, d), tl.row),
            pl.BlockSpec((None, N_MOD, d), tl.cond),
            pl.BlockSpec(dcs.shape, lambda t: (0, 0)),
        ],
        out_specs=(pl.BlockSpec((tl.tm, d), tl.row), pl.BlockSpec((tl.tm, d), tl.row)),
        compiler_params=_params(("arbitrary",)),
        name="ln_chan_dft",
    )(x, mod, dcs)


def _seq_dft_kernel(cs_ref, ss_ref, xc_ref, xs_ref, o_ref, *, norm):
    f = _dot(cs_ref[...], xc_ref[...]) - _dot(ss_ref[...], xs_ref[...])
    o_ref[...] = (f * norm).astype(BF16)


def _seq_dft(cs, ss, xc, xs, batch, seq, row_offset):
    d = xc.shape[1]
    tm = min(QKV_TILE, seq)
    spt = seq // tm
    off_seq = row_offset // seq
    return pl.pallas_call(
        functools.partial(_seq_dft_kernel, norm=1.0 / math.sqrt(seq * FOURIER_GROUP_DIM)),
        out_shape=jax.ShapeDtypeStruct((batch * seq, d), BF16),
        grid=(batch, spt),
        in_specs=[
            pl.BlockSpec((tm, seq), lambda b, i: (i, 0)),
            pl.BlockSpec((tm, seq), lambda b, i: (i, 0)),
            pl.BlockSpec((seq, d), lambda b, i: (off_seq + b, 0)),
            pl.BlockSpec((seq, d), lambda b, i: (off_seq + b, 0)),
        ],
        out_specs=pl.BlockSpec((tm, d), lambda b, i: (b * spt + i, 0)),
        compiler_params=_params(("arbitrary", "arbitrary")),
        name=f"seq_dft_{seq}",
    )(cs, ss, xc, xs)


def _diff_lambda(lam_ref, lam_init):
    lv = lam_ref[...]
    return (jnp.exp(jnp.sum(lv[0:1] * lv[1:2], axis=-1, keepdims=True))
            - jnp.exp(jnp.sum(lv[2:3] * lv[3:4], axis=-1, keepdims=True)) + lam_init)


def _diff_attn_head(q, k, v_ext, lam, gain, lam_init):
    tq = q.shape[0]
    lane = lax.broadcasted_iota(I32, q.shape, 1)
    zero = jnp.zeros_like(q)
    qq = jnp.concatenate([jnp.where(lane < HEAD_DIM, q, zero), jnp.where(lane >= HEAD_DIM, q, zero)], axis=0)
    parts = []
    for r in range(0, 2 * tq, ATTN_ROW_CHUNK):
        s = lax.dot_general(qq[r:r + ATTN_ROW_CHUNK], k, (((1,), (1,)), ((), ())), preferred_element_type=F32)
        e = jnp.exp2(s - jnp.max(s, axis=-1, keepdims=True)).astype(BF16)
        parts.append(_dot(e, v_ext))
    oe = jnp.concatenate(parts, axis=0)
    o = oe[:, :V_DIM] / oe[:, V_DIM:]
    o = o[:tq] - lam * o[tq:]
    o = o * lax.rsqrt(jnp.mean(o * o, axis=-1, keepdims=True) + LN_EPS)
    return o * gain * (1.0 - lam_init)


def _attn_prompt_kernel(lam_ref, gain_ref, q_ref, k_ref, v_ref, o_ref, *, lam_init):
    lam = _diff_lambda(lam_ref, lam_init)
    gain = gain_ref[...]
    ones = jnp.ones((k_ref.shape[0], V_DIM), BF16)
    for hd in range(N_HEADS):
        cols = slice(hd * V_DIM, (hd + 1) * V_DIM)
        v_ext = jnp.concatenate([v_ref[:, cols], ones], axis=1)
        o_ref[:, cols] = _diff_attn_head(q_ref[:, cols], k_ref[:, cols], v_ext, lam, gain, lam_init).astype(BF16)


def _attn_latent_kernel(lam_ref, gain_ref, q_ref, k_ref, v_ref, kc_ref, vc_ref, o_ref, kall_ref, vext_ref, *,
                        lam_init):
    n_new = k_ref.shape[0]

    @pl.when(pl.program_id(2) == 0)
    def _():
        kall_ref[:n_new, :] = k_ref[...]
        kall_ref[n_new:, :] = kc_ref[...]
        vext_ref[:n_new, :V_DIM] = v_ref[...]
        vext_ref[n_new:, :V_DIM] = vc_ref[...]
        vext_ref[:, V_DIM:] = jnp.ones((vext_ref.shape[0], V_DIM), BF16)

    o = _diff_attn_head(q_ref[...], kall_ref[...], vext_ref[...], _diff_lambda(lam_ref, lam_init), gain_ref[...],
                        lam_init)
    o_ref[...] = o.astype(BF16)


def _attn_prompt(lam_vecs, gain, q, k, v, batch, seq, lam_init):
    d = q.shape[1]
    blk = pl.BlockSpec((seq, d), lambda b: (b, 0))
    return pl.pallas_call(
        functools.partial(_attn_prompt_kernel, lam_init=lam_init),
        out_shape=jax.ShapeDtypeStruct((batch * seq, d), BF16),
        grid=(batch,),
        in_specs=[
            pl.BlockSpec(lam_vecs.shape, lambda b: (0, 0)),
            pl.BlockSpec(gain.shape, lambda b: (0, 0)),
            blk, blk, blk,
        ],
        out_specs=blk,
        compiler_params=_params(("arbitrary",)),
        name="diff_attn_ctx",
    )(lam_vecs, gain, q, k, v)


def _attn_sample(lam_vecs, gain, q, k, v, kc, vc, batch, seq, n_ctx, row_offset, lam_init):
    d = q.shape[1]
    tq = Q_TILE
    qpt = seq // tq
    off_seq = row_offset // seq
    off_tile = row_offset // tq
    qmap = lambda b, h, i: (off_tile + b * qpt + i, h)
    kmap = lambda b, h, i: (off_seq + b, h)
    cmap = lambda b, h, i: (b, h)
    return pl.pallas_call(
        functools.partial(_attn_latent_kernel, lam_init=lam_init),
        out_shape=jax.ShapeDtypeStruct((batch * seq, d), BF16),
        grid=(batch, N_HEADS, qpt),
        in_specs=[
            pl.BlockSpec(lam_vecs.shape, lambda b, h, i: (0, 0)),
            pl.BlockSpec(gain.shape, lambda b, h, i: (0, 0)),
            pl.BlockSpec((tq, V_DIM), qmap),
            pl.BlockSpec((seq, V_DIM), kmap),
            pl.BlockSpec((seq, V_DIM), kmap),
            pl.BlockSpec((n_ctx, V_DIM), cmap),
            pl.BlockSpec((n_ctx, V_DIM), cmap),
        ],
        out_specs=pl.BlockSpec((tq, V_DIM), lambda b, h, i: (b * qpt + i, h)),
        scratch_shapes=[pltpu.VMEM((seq + n_ctx, V_DIM), BF16), pltpu.VMEM((seq + n_ctx, 2 * V_DIM), BF16)],
        compiler_params=_params(("arbitrary", "arbitrary", "arbitrary")),
        name="diff_attn_latent",
    )(lam_vecs, gain, q, k, v, kc, vc)


def _route(lg):
    lane = lax.broadcasted_iota(I32, lg.shape, 1)
    lane_f = lane.astype(F32)
    neg = jnp.float32(-jnp.inf)
    big = jnp.float32(LANES)
    gl = jnp.where(lane < N_EXPERT_GROUPS, lg, neg)
    gmax = jnp.max(gl, axis=-1, keepdims=True)
    g_prob = 1.0 / jnp.sum(jnp.exp(gl - gmax), axis=-1, keepdims=True)
    g_idx = jnp.min(jnp.where(gl == gmax, lane_f, big), axis=-1, keepdims=True)
    lo = N_EXPERT_GROUPS + EXPERTS_PER_GROUP * g_idx
    el = jnp.where((lane_f >= lo) & (lane_f < lo + EXPERTS_PER_GROUP), lg, neg)
    m1 = jnp.max(el, axis=-1, keepdims=True)
    i1 = jnp.min(jnp.where(el == m1, lane_f, big), axis=-1, keepdims=True)
    el2 = jnp.where(lane_f == i1, neg, el)
    m2 = jnp.max(el2, axis=-1, keepdims=True)
    i2 = jnp.min(jnp.where(el2 == m2, lane_f, big), axis=-1, keepdims=True)
    t = jnp.exp(m2 - m1)
    w1 = g_prob / (1.0 + t)
    w2 = g_prob * t / (1.0 + t)
    out = jnp.where(lane == 0, i1 - N_EXPERT_GROUPS, 0.0)
    out = jnp.where(lane == 1, i2 - N_EXPERT_GROUPS, out)
    out = jnp.where(lane == 2, w1, out)
    out = jnp.where(lane == 3, w2, out)
    return out


def _pack_bf16_pairs(h):
    n = h.shape[1] // 2
    bits = lax.bitcast_convert_type(h.astype(BF16).astype(F32), jnp.uint32)
    return (bits[:, :n] >> 16) | bits[:, n:]


def _unpack_bf16_pairs(p):
    lo = lax.bitcast_convert_type(p << 16, F32)
    hi = lax.bitcast_convert_type(p & jnp.uint32(0xFFFF0000), F32)
    return jnp.concatenate([lo, hi], axis=1)


def _mix_out_kernel(*refs, n_prompt_tiles, n_x):
    ap_ref, as_ref, w_ref = refs[:3]
    x_refs = refs[3:3 + n_x]
    mod_ref, gb_ref, wr_ref, x1_ref, hp_ref, route_ref = refs[3 + n_x:]
    is_prompt = pl.program_id(0) < n_prompt_tiles
    for rows in _row_chains(x1_ref.shape[0]):
        a = jnp.where(is_prompt, ap_ref[rows, :], as_ref[rows, :])
        out = _dot(a, w_ref[...])
        x = _read_rows(x_refs, n_prompt_tiles, rows)
        x1 = _deepnorm(x, mod_ref[2:3, :], out) * gb_ref[0:1, :] + gb_ref[1:2, :]
        x1_ref[rows, :] = x1
        h2 = _layernorm(x1) * (1.0 + mod_ref[4:5, :]) + mod_ref[3:4, :]
        hp_ref[rows, :] = _pack_bf16_pairs(h2)
        hi = h2.astype(BF16)
        lo = (h2 - hi.astype(F32)).astype(BF16)
        logits = _dot(hi, wr_ref[0]) + _dot(lo, wr_ref[0]) + _dot(hi, wr_ref[1])
        route_ref[rows, :] = _route(logits)


def _mix_out(a_prompt, a_sample, w, xs, mod, gb, wr, n_prompt, dec_seq, t_tok):
    d = w.shape[1]
    tl = _Tiles(n_prompt, dec_seq, TOKEN_TILE)
    tm = tl.tm
    return pl.pallas_call(
        functools.partial(_mix_out_kernel, n_prompt_tiles=tl.npt, n_x=len(xs)),
        out_shape=(
            jax.ShapeDtypeStruct((t_tok, d), F32),
            jax.ShapeDtypeStruct((t_tok, d // 2), jnp.uint32),
            jax.ShapeDtypeStruct((t_tok, LANES), F32),
        ),
        grid=(t_tok // tm,),
        in_specs=[
            pl.BlockSpec((tm, d), tl.prompt_part),
            pl.BlockSpec((tm, d), tl.latent_part),
            pl.BlockSpec(w.shape, lambda t: (0, 0)),
        ] + tl.source_specs(xs, d) + [
            pl.BlockSpec((None, N_MOD, d), tl.cond),
            pl.BlockSpec(gb.shape, lambda t: (0, 0)),
            pl.BlockSpec(wr.shape, lambda t: (0, 0, 0)),
        ],
        out_specs=(
            pl.BlockSpec((tm, d), tl.row),
            pl.BlockSpec((tm, d // 2), tl.row),
            pl.BlockSpec((tm, LANES), tl.row),
        ),
        compiler_params=_params(("arbitrary",)),
        name="mix_out_postnorm_router",
    )(a_prompt, a_sample, w, *xs, mod, gb, wr)


def _lane_prefix_sum(x, lane):
    sh = 1
    while sh < LANES:
        x = x + jnp.where(lane >= sh, pltpu.roll(x, sh, 1), 0.0)
        sh *= 2
    return x


def _work_items(counts, starts, ends, lane8, n_moe_tiles):
    shift = int(math.log2(MOE_TILE))
    first_tile = (starts.astype(I32) >> shift).astype(F32)
    last_tile = ((ends.astype(I32) - 1) >> shift).astype(F32)
    n_it = jnp.where(counts > 0.0, last_tile - first_tile + 1.0, 0.0)
    it_end = _lane_prefix_sum(n_it, lane8)
    it_start = it_end - n_it
    total = it_end[0:1, LANES - 1:LANES]
    sub = lax.broadcasted_iota(I32, (LANES, LANES), 0)
    rows = lambda x: jnp.broadcast_to(x[0:1, :], (LANES, LANES))
    stacked = jnp.where(sub == 0, rows(starts), jnp.where(sub == 1, rows(ends), jnp.where(
        sub == 2, rows(first_tile), jnp.where(sub == 3, rows(it_start), jnp.where(sub == 4, rows(it_end), 0.0)))))
    cols = stacked.T
    col = lambda j: cols[:, j:j + 1]
    sub_f = sub.astype(F32)
    w = lax.broadcasted_iota(I32, (LANES, LANES), 1).astype(F32)
    ex = jnp.sum(jnp.where((sub < N_EXPERTS) & (col(4) <= w), 1.0, 0.0), axis=0, keepdims=True)
    ex = jnp.minimum(ex, N_EXPERTS - 1.0)
    w1 = w[0:1, :]
    valid = w1 < total
    ex = jnp.where(valid, ex, jnp.max(jnp.where(valid, ex, 0.0), axis=-1, keepdims=True))
    onehot = sub_f == ex
    pick = lambda j: jnp.sum(jnp.where(onehot, col(j), 0.0), axis=0, keepdims=True)
    tile = jnp.where(valid, pick(2) + (w1 - pick(3)), n_moe_tiles - 1.0)
    lo = jnp.where(valid, jnp.maximum(pick(0), tile * MOE_TILE), 0.0)
    hi = jnp.where(valid, jnp.minimum(pick(1), (tile + 1.0) * MOE_TILE), 0.0)
    b8 = lambda x: jnp.broadcast_to(x, (SUBLANES, LANES))
    ex8, tile8 = b8(ex), b8(tile)
    first = jnp.where((lane8 == 0) | (tile8 != pltpu.roll(tile8, 1, 1)), 1.0, 0.0)
    newexp = jnp.where((lane8 == 0) | (ex8 != pltpu.roll(ex8, 1, 1)), 1.0, 0.0)
    sub8 = lax.broadcasted_iota(I32, (SUBLANES, LANES), 0)
    return jnp.where(sub8 == 0, ex8, jnp.where(sub8 == 1, tile8, jnp.where(sub8 == 2, b8(lo), jnp.where(
        sub8 == 3, b8(hi), jnp.where(sub8 == 4, first, jnp.where(sub8 == 5, newexp, 0.0))))))


def _plan_kernel(route_ref, pos_ref, items_ref, tri_ref, carry_ref, tot_ref, *, n_moe_tiles):
    p = pl.program_id(0)
    t = pl.program_id(1)
    tm = route_ref.shape[0]
    r = route_ref[...]
    lane = lax.broadcasted_iota(I32, (tm, LANES), 1)
    lane_f = lane.astype(F32)
    e0 = r[:, 0:1]
    e1 = r[:, 1:2] + N_EXPERTS
    m = jnp.where((lane_f == e0) | (lane_f == e1), 1.0, 0.0)
    colsum = jnp.sum(m, axis=0, keepdims=True)

    @pl.when((p == 0) & (t == 0))
    def _():
        carry_ref[...] = jnp.zeros_like(carry_ref)
        row = lax.broadcasted_iota(I32, (tm, tm), 0)
        col = lax.broadcasted_iota(I32, (tm, tm), 1)
        tri_ref[...] = jnp.where(row > col, 1.0, 0.0).astype(BF16)

    @pl.when(p == 0)
    def _():
        carry_ref[...] += colsum

    @pl.when((p == 0) & (t == pl.num_programs(1) - 1))
    def _():
        tot_ref[...] = carry_ref[...]
        carry_ref[...] = jnp.zeros_like(carry_ref)

    @pl.when(p == 1)
    def _():
        lane8 = lax.broadcasted_iota(I32, (SUBLANES, LANES), 1)
        tot = tot_ref[...]
        is_first = lane8 < N_EXPERTS
        tot0 = jnp.where(is_first, tot, 0.0)
        counts = jnp.where(is_first, tot + pltpu.roll(tot, LANES - N_EXPERTS, 1), 0.0)
        ends = _lane_prefix_sum(counts, lane8)
        starts = ends - counts
        base = jnp.where(is_first, starts, pltpu.roll(starts + tot0, N_EXPERTS, 1))
        before = _dot(tri_ref[...], m.astype(BF16)) + carry_ref[0:1, :]
        carry_ref[...] += colsum
        rows = before + base[0:1, :]
        pos0 = jnp.sum(jnp.where(lane_f == e0, rows, 0.0), axis=-1, keepdims=True)
        pos1 = jnp.sum(jnp.where(lane_f == e1, rows, 0.0), axis=-1, keepdims=True)
        both = jnp.where(lane == 0, pos0, jnp.where(lane == 1, pos1, 0.0))
        pos_ref[...] = both.T[0:SUBLANES, :].astype(I32)

        @pl.when(t == 0)
        def _():
            items_ref[...] = _work_items(counts, starts, ends, lane8, n_moe_tiles).astype(I32)


def _moe_plan(route):
    t_tok = route.shape[0]
    tm = PLAN_TILE
    n_moe_tiles = 2 * t_tok // MOE_TILE
    assert n_moe_tiles + N_EXPERTS - 1 <= LANES and 2 * N_EXPERTS <= LANES and MOE_TILE & (MOE_TILE - 1) == 0
    return pl.pallas_call(
        functools.partial(_plan_kernel, n_moe_tiles=n_moe_tiles),
        out_shape=(jax.ShapeDtypeStruct((SUBLANES, t_tok), I32), jax.ShapeDtypeStruct((SUBLANES, LANES), I32)),
        grid=(2, t_tok // tm),
        in_specs=[pl.BlockSpec((tm, LANES), lambda p, t: (t, 0))],
        out_specs=(pl.BlockSpec((SUBLANES, tm), lambda p, t: (0, t * p)),
                   pl.BlockSpec((SUBLANES, LANES), lambda p, t: (0, 0))),
        scratch_shapes=[pltpu.VMEM((tm, tm), BF16), pltpu.VMEM((SUBLANES, LANES), F32),
                        pltpu.VMEM((SUBLANES, LANES), F32)],
        compiler_params=_params(("arbitrary", "arbitrary")),
        name="moe_positions",
    )(route)


def _sc_workers():
    info = plsc.get_sparse_core_info()
    return info.num_cores, info.num_cores * info.num_subcores


def _sc_pipeline(n_chunks, load, store):
    load(0, 0).start()
    for j in range(n_chunks):
        b = j % 2
        if j + 1 < n_chunks:
            if j >= 1:
                store(j - 1, 1 - b).wait()
            load(j + 1, 1 - b).start()
        load(j, b).wait()
        store(j, b).start()
    if n_chunks >= 2:
        store(n_chunks - 2, n_chunks % 2).wait()
    store(n_chunks - 1, (n_chunks - 1) % 2).wait()


def _sc_row_mover(n, d, dtype, name, body_of):
    n_cores, n_workers = _sc_workers()
    per_worker = n // n_workers
    k = SC_BUFFER_BYTES // (d * 4)
    n_chunks = per_worker // k
    assert n_chunks * k * n_workers == n and k <= SC_MAX_INDEX_CHUNK
    call = pl.kernel(
        body_of(n_cores, per_worker, k, n_chunks),
        out_type=jax.ShapeDtypeStruct((n, d), dtype),
        mesh=plsc.VectorSubcoreMesh(core_axis_name="c", subcore_axis_name="s"),
        scratch_types=[pltpu.VMEM((n_chunks, k), I32), pltpu.VMEM((2, k, d), dtype),
                       pltpu.SemaphoreType.DMA((2,)), pltpu.SemaphoreType.DMA((2,))],
        name=name,
    )
    return lambda rows, idx: call(rows, idx.reshape(n_workers, n_chunks, k))


def _sc_gather_rows(table, idx):
    def body_of(n_cores, per_worker, k, n_chunks):
        def body(table_hbm, idx_hbm, out_hbm, idx_v, rows_v, gsem, osem):
            wid = lax.axis_index("s") * n_cores + lax.axis_index("c")
            base = wid * per_worker
            pltpu.sync_copy(idx_hbm.at[wid], idx_v)
            _sc_pipeline(
                n_chunks,
                lambda j, b: pltpu.make_async_copy(table_hbm.at[idx_v.at[j]], rows_v.at[b], gsem.at[b]),
                lambda j, b: pltpu.make_async_copy(rows_v.at[b], out_hbm.at[pl.ds(base + j * k, k)],
                                                   osem.at[b]))
        return body
    return _sc_row_mover(idx.shape[0], table.shape[1], table.dtype, "sc_row_gather", body_of)(table, idx)


def _sc_scatter_rows(src, idx):
    n_src = src.shape[0]

    def body_of(n_cores, per_worker, k, n_chunks):
        assert n_src % per_worker == 0

        def body(src_hbm, idx_hbm, out_hbm, idx_v, rows_v, gsem, osem):
            wid = lax.axis_index("s") * n_cores + lax.axis_index("c")
            base = lax.rem(wid * per_worker, n_src)
            pltpu.sync_copy(idx_hbm.at[wid], idx_v)
            _sc_pipeline(
                n_chunks,
                lambda j, b: pltpu.make_async_copy(src_hbm.at[pl.ds(base + j * k, k)], rows_v.at[b],
                                                   gsem.at[b]),
                lambda j, b: pltpu.make_async_copy(rows_v.at[b], out_hbm.at[idx_v.at[j]], osem.at[b]))
        return body
    return _sc_row_mover(idx.shape[0], src.shape[1], src.dtype, "sc_row_scatter", body_of)(src, idx)


IT_EXPERT, IT_TILE, IT_LO, IT_HI, IT_FIRST, IT_NEWEXP = range(6)


def _moe_kernel(it_ref, xs_ref, wg_ref, wu_ref, wd_ref, o_ref, wgu_s, wd_s):
    w = pl.program_id(0)
    f = wg_ref.shape[1]

    @pl.when(it_ref[IT_NEWEXP, w] == 1)
    def _():
        wgu_s[:, :f] = wg_ref[...].astype(BF16)
        wgu_s[:, f:] = wu_ref[...].astype(BF16)
        wd_s[...] = wd_ref[...].astype(BF16)

    tm = xs_ref.shape[0]
    x = _unpack_bf16_pairs(xs_ref[...]).astype(BF16)
    au = _dot(x, wgu_s[...])
    act = (_silu(au[:, :f]) * au[:, f:]).astype(BF16)
    y = _pack_bf16_pairs(_dot(act, wd_s[...]))
    row = it_ref[IT_TILE, w] * tm + lax.broadcasted_iota(I32, (tm, 1), 0)
    mine = (row >= it_ref[IT_LO, w]) & (row < it_ref[IT_HI, w])

    @pl.when(it_ref[IT_FIRST, w] == 1)
    def _():
        o_ref[...] = jnp.where(mine, y, jnp.zeros_like(y))

    @pl.when(it_ref[IT_FIRST, w] == 0)
    def _():
        o_ref[...] = jnp.where(mine, y, o_ref[...])


def _moe_experts(items, xs, w_gate, w_up, w_down, layer):
    n_rows, dp = xs.shape
    _, _, d, f = w_gate.shape
    tm = MOE_TILE
    wmap = lambda w, it: (layer, it[IT_EXPERT, w], 0, 0)
    rmap = lambda w, it: (it[IT_TILE, w], 0)
    return pl.pallas_call(
        _moe_kernel,
        out_shape=jax.ShapeDtypeStruct((n_rows, d // 2), jnp.uint32),
        grid_spec=pltpu.PrefetchScalarGridSpec(
            num_scalar_prefetch=1,
            grid=(n_rows // tm + N_EXPERTS - 1,),
            in_specs=[
                pl.BlockSpec((tm, dp), rmap),
                pl.BlockSpec((None, None, d, f), wmap),
                pl.BlockSpec((None, None, d, f), wmap),
                pl.BlockSpec((None, None, f, d), wmap),
            ],
            out_specs=pl.BlockSpec((tm, d // 2), rmap),
            scratch_shapes=[pltpu.VMEM((d, 2 * f), BF16), pltpu.VMEM((f, d), BF16)],
        ),
        compiler_params=_params(("arbitrary",)),
        name="moe_grouped_mlp",
    )(items, xs, w_gate, w_up, w_down)


def _moe_combine_kernel(x_ref, y0_ref, y1_ref, route_ref, mod_ref, gb_ref, *o_refs, n_prompt_tiles):
    r = route_ref[...]
    moe = r[:, 2:3] * _unpack_bf16_pairs(y0_ref[...]) + r[:, 3:4] * _unpack_bf16_pairs(y1_ref[...])
    x2 = _deepnorm(x_ref[...], mod_ref[5:6, :], moe) * gb_ref[0:1, :] + gb_ref[1:2, :]
    if len(o_refs) == 1:
        o_refs[0][...] = x2
    else:
        @pl.when(pl.program_id(0) < n_prompt_tiles)
        def _():
            o_refs[0][...] = x2

        @pl.when(pl.program_id(0) >= n_prompt_tiles)
        def _():
            o_refs[1][...] = x2


def _moe_combine(x1, ys, route, mod, gb, n_prompt, dec_seq, split):
    t_tok, d = x1.shape
    tl = _Tiles(n_prompt, dec_seq, TOKEN_TILE)
    tm = tl.tm
    nt = t_tok // tm
    if split:
        out_shape = (jax.ShapeDtypeStruct((n_prompt, d), F32), jax.ShapeDtypeStruct((t_tok - n_prompt, d), F32))
        out_specs = (pl.BlockSpec((tm, d), tl.prompt_part), pl.BlockSpec((tm, d), tl.latent_part))
    else:
        out_shape = jax.ShapeDtypeStruct((t_tok, d), F32)
        out_specs = pl.BlockSpec((tm, d), tl.row)
    return pl.pallas_call(
        functools.partial(_moe_combine_kernel, n_prompt_tiles=tl.npt),
        out_shape=out_shape,
        grid=(nt,),
        in_specs=[
            pl.BlockSpec((tm, d), tl.row),
            pl.BlockSpec((tm, ys.shape[1]), tl.row),
            pl.BlockSpec((tm, ys.shape[1]), lambda t: (t + nt, 0)),
            pl.BlockSpec((tm, LANES), tl.row),
            pl.BlockSpec((None, N_MOD, d), tl.cond),
            pl.BlockSpec(gb.shape, lambda t: (0, 0)),
        ],
        out_specs=out_specs,
        compiler_params=_params(("arbitrary",)),
        name="moe_combine_postnorm",
    )(x1, ys, ys, route, mod, gb)


def _rope_tables(n_lat):
    nf = HEAD_DIM // 4
    s = np.arange(n_lat)
    lane = np.arange(LANES)
    inv = ROPE_BASE ** (-(lane % nf).astype(np.float64) / nf)
    use_col = (lane % HEAD_DIM) >= HEAD_DIM // 2
    p = np.where(use_col[None, :], (s % GRID_W)[:, None], (s // GRID_W)[:, None]).astype(np.float64)
    ang = p * inv[None, :]
    sign = np.where((lane % (2 * nf)) < nf, -1.0, 1.0)
    return jnp.asarray(np.cos(ang), F32), jnp.asarray(np.sin(ang) * sign[None, :], F32)


def _dft_tables(n):
    k = np.arange(n)
    ang = 2.0 * np.pi * ((k[:, None] * k[None, :]) % n).astype(np.float64) / n
    return np.cos(ang), np.sin(ang)


def kernel(x_prompt, x_sample, cache_k, cache_v, c, c_ctx, w_ada, b_ada, ln_gain, ln_bias, w_qkv, w_attn_out,
           lambda_q1, lambda_k1, lambda_q2, lambda_k2, subln_gain, w_fourier_out, w_router_group,
           w_router_expert, w_expert_gate, w_expert_up, w_expert_down):
    bp, sp, d = x_prompt.shape
    bs, n_lat, _ = x_sample.shape
    n_ctx = cache_k.shape[2]
    n_prompt = bp * sp
    t_tok = n_prompt + bs * n_lat
    assert d == D_MODEL and n_prompt % n_lat == 0 and n_lat % TOKEN_TILE == 0 and sp == QKV_TILE

    cond = jnp.concatenate([c_ctx[None, :], c, jnp.zeros((SUBLANES - 1 - bs, d), F32)], axis=0)
    mods = _ada_all(cond, w_ada, b_ada).reshape(DEPTH, SUBLANES, N_MOD, d)

    cos, sin = _rope_tables(n_lat)
    cc, sc = _dft_tables(FOURIER_GROUP_DIM)
    dcs = jnp.asarray(np.concatenate([cc, sc], axis=1), BF16)
    seq_tabs = {s: tuple(jnp.asarray(m, BF16) for m in _dft_tables(s)) for s in (sp, n_lat)}
    attn_layers = [i for i in range(DEPTH) if i % 2 == 0]

    xs = (x_prompt.reshape(n_prompt, d), x_sample.reshape(bs * n_lat, d))
    prev_kv = []
    new_k = new_v = None
    for i in range(DEPTH):
        mod = mods[i]
        if i % 2 == 0:
            a = i // 2
            lam_init = 0.8 - 0.6 * math.exp(-0.3 * i)
            last_attn = i == attn_layers[-1]
            q, k, v, kf, vf = _qkv(xs, mod, w_qkv[a].astype(BF16), cos, sin, prev_kv, last_attn, n_prompt, n_lat,
                                   t_tok)
            if last_attn:
                new_k, new_v = kf, vf
            else:
                prev_kv.append((kf, vf))
            lam_vecs = jnp.stack([lambda_q1[a], lambda_k1[a], lambda_q2[a], lambda_k2[a]], axis=0)
            gain = subln_gain[a][None, :]
            kc = cache_k[:, a].reshape(bs * n_ctx, d).astype(BF16)
            vc = cache_v[:, a].reshape(bs * n_ctx, d).astype(BF16)
            mixed_p = _attn_prompt(lam_vecs, gain, q, k, v, bp, sp, lam_init)
            mixed_s = _attn_sample(lam_vecs, gain, q, k, v, kc, vc, bs, n_lat, n_ctx, n_prompt, lam_init)
            w_mix = w_attn_out[a]
        else:
            xc, xsn = _chan_dft(xs[0], mod, dcs, n_prompt, n_lat)
            mixed_p = _seq_dft(*seq_tabs[sp], xc, xsn, bp, sp, 0)
            mixed_s = _seq_dft(*seq_tabs[n_lat], xc, xsn, bs, n_lat, n_prompt)
            w_mix = w_fourier_out[i // 2]
        gb0 = jnp.stack([ln_gain[i, 0], ln_bias[i, 0]], axis=0)
        gb1 = jnp.stack([ln_gain[i, 1], ln_bias[i, 1]], axis=0)
        wr = jnp.concatenate([w_router_group[i], w_router_expert[i],
                              jnp.zeros((d, LANES - N_EXPERT_GROUPS - N_EXPERTS), F32)], axis=1)
        wr_hi = wr.astype(BF16)
        wr_lo = (wr - wr_hi.astype(F32)).astype(BF16)
        x1, hp, route = _mix_out(mixed_p, mixed_s, w_mix.astype(BF16), xs, mod, gb0,
                                 jnp.stack([wr_hi, wr_lo], axis=0), n_prompt, n_lat, t_tok)
        pos8, items = _moe_plan(route)
        pos = pos8[0:2].reshape(-1)
        xs_sorted = _sc_scatter_rows(hp, pos)
        ys = _moe_experts(items, xs_sorted, w_expert_gate, w_expert_up, w_expert_down, i)
        yg = _sc_gather_rows(ys, pos)
        last = i == DEPTH - 1
        out = _moe_combine(x1, yg, route, mod, gb1, n_prompt, n_lat, split=last)
        xs = out if last else (out,)

    y_prompt = xs[0].reshape(bp, sp, d)
    y_sample = xs[1].reshape(bs, n_lat, d)
    return (y_prompt, y_sample, new_k.reshape(bp, len(attn_layers), sp, N_HEADS, 2 * HEAD_DIM),
            new_v.reshape(bp, len(attn_layers), sp, N_HEADS, V_DIM))
```

```python
import functools
import math

import numpy as np
import jax
import jax.numpy as jnp
from jax import lax
from jax.experimental import pallas as pl
from jax.experimental.pallas import tpu as pltpu
from jax.experimental.pallas import tpu_sc as plsc

F32 = jnp.float32
BF16 = jnp.bfloat16
I32 = jnp.int32

D_MODEL = 1024
DEPTH = 4
GRID_W = 64
N_HEADS = 8
HEAD_DIM = 64
V_DIM = 2 * HEAD_DIM
ROPE_BASE = 10000.0
N_FOURIER_GROUPS = 8
FOURIER_GROUP_DIM = D_MODEL // N_FOURIER_GROUPS
N_EXPERT_GROUPS = 4
EXPERTS_PER_GROUP = 8
N_EXPERTS = N_EXPERT_GROUPS * EXPERTS_PER_GROUP
D_EXPERT = 256
N_MOD = 6
LN_EPS = 1e-5
DEEPNORM_ALPHA = (2.0 * DEPTH) ** 0.25
Q_SCALE = math.log2(math.e) * HEAD_DIM ** -0.5

LANES = 128
SUBLANES = 8
TOKEN_TILE = 512
CHAIN_ROWS = 256
QKV_TILE = 256
Q_TILE = 256
ATTN_ROW_CHUNK = 128
MOE_TILE = 512
PLAN_TILE = 1024
SC_BUFFER_BYTES = 128 * 1024
SC_MAX_INDEX_CHUNK = 128
VMEM_LIMIT = 48 * 1024 * 1024


def _params(semantics):
    return pltpu.CompilerParams(dimension_semantics=semantics, vmem_limit_bytes=VMEM_LIMIT)


def _layernorm(x, eps=LN_EPS):
    mu = jnp.mean(x, axis=-1, keepdims=True)
    xc = x - mu
    var = jnp.mean(xc * xc, axis=-1, keepdims=True)
    return xc * lax.rsqrt(var + eps)


def _deepnorm(x, branch_gate, branch):
    return _layernorm(x + (branch_gate * (1.0 / DEEPNORM_ALPHA)) * branch, LN_EPS / DEEPNORM_ALPHA ** 2)


def _silu(a):
    return a / (1.0 + jnp.exp(-a))


def _dot(a, b):
    return jnp.dot(a, b, preferred_element_type=F32)


class _Tiles:
    def __init__(self, n_prompt, dec_seq, tm):
        self.tm = tm
        self.npt = n_prompt // tm
        self.tps = dec_seq // tm

    def row(self, t):
        return (t, 0)

    def cond(self, t):
        return (jnp.where(t < self.npt, 0, (t - self.npt) // self.tps + 1), 0, 0)

    def latent_pos(self, t):
        return (jnp.maximum(t - self.npt, 0) % self.tps, 0)

    def prompt_part(self, t):
        return (jnp.minimum(t, self.npt - 1), 0)

    def latent_part(self, t):
        return (jnp.maximum(t - self.npt, 0), 0)

    def source_specs(self, arrays, d):
        if len(arrays) == 1:
            return [pl.BlockSpec((self.tm, d), self.row)]
        return [pl.BlockSpec((self.tm, d), self.prompt_part), pl.BlockSpec((self.tm, d), self.latent_part)]


def _read_rows(refs, n_prompt_tiles, rows=slice(None)):
    if len(refs) == 1:
        return refs[0][rows, :]
    return jnp.where(pl.program_id(0) < n_prompt_tiles, refs[0][rows, :], refs[1][rows, :])


def _ada_kernel(cond_ref, w_ref, b_ref, o_ref):
    a = _silu(cond_ref[...])
    o_ref[...] = _dot(a.astype(BF16), w_ref[...].astype(BF16)) + b_ref[...]


def _ada_all(cond, w_ada, b_ada):
    depth, d, n = w_ada.shape
    tn = n // 4
    return pl.pallas_call(
        _ada_kernel,
        out_shape=jax.ShapeDtypeStruct((depth, cond.shape[0], n), F32),
        grid=(depth, n // tn),
        in_specs=[
            pl.BlockSpec(cond.shape, lambda l, j: (0, 0)),
            pl.BlockSpec((None, d, tn), lambda l, j: (l, 0, j)),
            pl.BlockSpec((None, 1, tn), lambda l, j: (l, 0, j)),
        ],
        out_specs=pl.BlockSpec((None, cond.shape[0], tn), lambda l, j: (l, 0, j)),
        compiler_params=_params(("arbitrary", "arbitrary")),
        name="ada",
    )(cond, w_ada, b_ada.reshape(depth, 1, n))


def _rope(x, cos, sin_signed, first_half):
    outs = []
    for c in range(x.shape[1] // LANES):
        xc = x[:, c * LANES:(c + 1) * LANES]
        partner = jnp.where(first_half, pltpu.roll(xc, LANES - 16, 1), pltpu.roll(xc, 16, 1))
        outs.append(xc * cos + partner * sin_signed)
    return jnp.concatenate(outs, axis=1)


def _store_heads(cache_ref, slot, rows):
    for hd in range(N_HEADS):
        cache_ref[slot, :, hd, :] = rows[:, hd * V_DIM:(hd + 1) * V_DIM]


def _qkv_kernel(*refs, n_prompt_tiles, n_x, n_prev):
    x_refs, (mod_ref, w_ref, cos_ref, sin_ref) = refs[:n_x], refs[n_x:n_x + 4]
    prev = refs[n_x + 4:n_x + 4 + 2 * max(n_prev, 0)]
    q_ref, k_ref, v_ref, ko_ref, vo_ref = refs[n_x + 4 + 2 * max(n_prev, 0):]
    t = pl.program_id(0)
    h = _layernorm(_read_rows(x_refs, n_prompt_tiles)) * (1.0 + mod_ref[1:2, :]) + mod_ref[0:1, :]
    acc = _dot(h.astype(BF16), w_ref[...])
    d = w_ref.shape[0]
    q = acc[:, :d] * Q_SCALE
    k = acc[:, d:2 * d]
    v = acc[:, 2 * d:]
    v_ref[...] = v.astype(BF16)

    @pl.when(t < n_prompt_tiles)
    def _():
        q_ref[...] = q.astype(BF16)
        k_ref[...] = k.astype(BF16)
        if n_prev < 0:
            ko_ref[...] = k
            vo_ref[...] = v
        else:
            for a in range(n_prev):
                _store_heads(ko_ref, a, prev[2 * a][...])
                _store_heads(vo_ref, a, prev[2 * a + 1][...])
            _store_heads(ko_ref, n_prev, k)
            _store_heads(vo_ref, n_prev, v)

    @pl.when(t >= n_prompt_tiles)
    def _():
        lane = lax.broadcasted_iota(I32, (q.shape[0], LANES), 1)
        first_half = (lane % 32) < 16
        cos = cos_ref[...]
        sin = sin_ref[...]
        q_ref[...] = _rope(q, cos, sin, first_half).astype(BF16)
        k_ref[...] = _rope(k, cos, sin, first_half).astype(BF16)


def _qkv(xs, mod, w_qkv, cos, sin, prev_kv, finish_cache, n_prompt, dec_seq, t_tok):
    d = w_qkv.shape[0]
    tl = _Tiles(n_prompt, dec_seq, QKV_TILE)
    tm = tl.tm
    n_prev = len(prev_kv) if finish_cache else -1
    if finish_cache:
        n_slots = n_prev + 1
        kv_shape = jax.ShapeDtypeStruct((n_prompt // tm, n_slots, tm, N_HEADS, V_DIM), F32)
        kv_spec = pl.BlockSpec((None, n_slots, tm, N_HEADS, V_DIM), lambda t: (tl.prompt_part(t)[0], 0, 0, 0, 0))
    else:
        kv_shape = jax.ShapeDtypeStruct((n_prompt, d), F32)
        kv_spec = pl.BlockSpec((tm, d), tl.prompt_part)
    prev_flat = [a for kv in prev_kv for a in kv] if finish_cache else []
    return pl.pallas_call(
        functools.partial(_qkv_kernel, n_prompt_tiles=tl.npt, n_x=len(xs), n_prev=n_prev),
        out_shape=(
            jax.ShapeDtypeStruct((t_tok, d), BF16),
            jax.ShapeDtypeStruct((t_tok, d), BF16),
            jax.ShapeDtypeStruct((t_tok, d), BF16),
            kv_shape, kv_shape,
        ),
        grid=(t_tok // tm,),
        in_specs=tl.source_specs(xs, d) + [
            pl.BlockSpec((None, N_MOD, d), tl.cond),
            pl.BlockSpec(w_qkv.shape, lambda t: (0, 0)),
            pl.BlockSpec((tm, LANES), tl.latent_pos),
            pl.BlockSpec((tm, LANES), tl.latent_pos),
        ] + [pl.BlockSpec((tm, d), tl.prompt_part) for _ in prev_flat],
        out_specs=(
            pl.BlockSpec((tm, d), tl.row),
            pl.BlockSpec((tm, d), tl.row),
            pl.BlockSpec((tm, d), tl.row),
            kv_spec, kv_spec,
        ),
        compiler_params=_params(("arbitrary",)),
        name="ln_qkv_rope",
    )(*xs, mod, w_qkv, cos, sin, *prev_flat)


def _row_chains(n_rows):
    return [slice(r, r + CHAIN_ROWS) for r in range(0, n_rows, CHAIN_ROWS)]


def _chan_dft_kernel(x_ref, mod_ref, dcs_ref, xc_ref, xs_ref):
    g = FOURIER_GROUP_DIM
    for rows in _row_chains(x_ref.shape[0]):
        h = (_layernorm(x_ref[rows, :]) * (1.0 + mod_ref[1:2, :]) + mod_ref[0:1, :]).astype(BF16)
        for i in range(N_FOURIER_GROUPS):
            r = _dot(h[:, i * g:(i + 1) * g], dcs_ref[...])
            xc_ref[rows, i * g:(i + 1) * g] = r[:, :g].astype(BF16)
            xs_ref[rows, i * g:(i + 1) * g] = r[:, g:].astype(BF16)


def _chan_dft(x, mod, dcs, n_prompt, dec_seq):
    t_tok, d = x.shape
    tl = _Tiles(n_prompt, dec_seq, TOKEN_TILE)
    return pl.pallas_call(
        _chan_dft_kernel,
        out_shape=(jax.ShapeDtypeStruct((t_tok, d), BF16), jax.ShapeDtypeStruct((t_tok, d), BF16)),
        grid=(t_tok // tl.tm,),
        in_specs=[
            pl.BlockSpec((tl.tm, d), tl.row),
            pl.BlockSpec((None, N_MOD, d), tl.cond),
            pl.BlockSpec(dcs.shape, lambda t: (0, 0)),
        ],
        out_specs=(pl.BlockSpec((tl.tm, d), tl.row), pl.BlockSpec((tl.tm, d), tl.row)),
        compiler_params=_params(("arbitrary",)),
        name="ln_chan_dft",
    )(x, mod, dcs)


def _seq_dft_kernel(cs_ref, ss_ref, xc_ref, xs_ref, o_ref, *, norm):
    f = _dot(cs_ref[...], xc_ref[...]) - _dot(ss_ref[...], xs_ref[...])
    o_ref[...] = (f * norm).astype(BF16)


def _seq_dft(cs, ss, xc, xs, batch, seq, row_offset):
    d = xc.shape[1]
    tm = min(QKV_TILE, seq)
    spt = seq // tm
    off_seq = row_offset // seq
    return pl.pallas_call(
        functools.partial(_seq_dft_kernel, norm=1.0 / math.sqrt(seq * FOURIER_GROUP_DIM)),
        out_shape=jax.ShapeDtypeStruct((batch * seq, d), BF16),
        grid=(batch, spt),
        in_specs=[
            pl.BlockSpec((tm, seq), lambda b, i: (i, 0)),
            pl.BlockSpec((tm, seq), lambda b, i: (i, 0)),
            pl.BlockSpec((seq, d), lambda b, i: (off_seq + b, 0)),
            pl.BlockSpec((seq, d), lambda b, i: (off_seq + b, 0)),
        ],
        out_specs=pl.BlockSpec((tm, d), lambda b, i: (b * spt + i, 0)),
        compiler_params=_params(("arbitrary", "arbitrary")),
        name=f"seq_dft_{seq}",
    )(cs, ss, xc, xs)


def _diff_lambda(lam_ref, lam_init):
    lv = lam_ref[...]
    return (jnp.exp(jnp.sum(lv[0:1] * lv[1:2], axis=-1, keepdims=True))
            - jnp.exp(jnp.sum(lv[2:3] * lv[3:4], axis=-1, keepdims=True)) + lam_init)


def _diff_attn_head(q, k, v_ext, lam, gain, lam_init):
    tq = q.shape[0]
    lane = lax.broadcasted_iota(I32, q.shape, 1)
    zero = jnp.zeros_like(q)
    qq = jnp.concatenate([jnp.where(lane < HEAD_DIM, q, zero), jnp.where(lane >= HEAD_DIM, q, zero)], axis=0)
    parts = []
    for r in range(0, 2 * tq, ATTN_ROW_CHUNK):
        s = lax.dot_general(qq[r:r + ATTN_ROW_CHUNK], k, (((1,), (1,)), ((), ())), preferred_element_type=F32)
        e = jnp.exp2(s - jnp.max(s, axis=-1, keepdims=True)).astype(BF16)
        parts.append(_dot(e, v_ext))
    oe = jnp.concatenate(parts, axis=0)
    o = oe[:, :V_DIM] / oe[:, V_DIM:]
    o = o[:tq] - lam * o[tq:]
    o = o * lax.rsqrt(jnp.mean(o * o, axis=-1, keepdims=True) + LN_EPS)
    return o * gain * (1.0 - lam_init)


def _attn_prompt_kernel(lam_ref, gain_ref, q_ref, k_ref, v_ref, o_ref, *, lam_init):
    lam = _diff_lambda(lam_ref, lam_init)
    gain = gain_ref[...]
    ones = jnp.ones((k_ref.shape[0], V_DIM), BF16)
    for hd in range(N_HEADS):
        cols = slice(hd * V_DIM, (hd + 1) * V_DIM)
        v_ext = jnp.concatenate([v_ref[:, cols], ones], axis=1)
        o_ref[:, cols] = _diff_attn_head(q_ref[:, cols], k_ref[:, cols], v_ext, lam, gain, lam_init).astype(BF16)


def _attn_latent_kernel(lam_ref, gain_ref, q_ref, k_ref, v_ref, kc_ref, vc_ref, o_ref, kall_ref, vext_ref, *,
                        lam_init):
    n_new = k_ref.shape[0]

    @pl.when(pl.program_id(2) == 0)
    def _():
        kall_ref[:n_new, :] = k_ref[...]
        kall_ref[n_new:, :] = kc_ref[...]
        vext_ref[:n_new, :V_DIM] = v_ref[...]
        vext_ref[n_new:, :V_DIM] = vc_ref[...]
        vext_ref[:, V_DIM:] = jnp.ones((vext_ref.shape[0], V_DIM), BF16)

    o = _diff_attn_head(q_ref[...], kall_ref[...], vext_ref[...], _diff_lambda(lam_ref, lam_init), gain_ref[...],
                        lam_init)
    o_ref[...] = o.astype(BF16)


def _attn_prompt(lam_vecs, gain, q, k, v, batch, seq, lam_init):
    d = q.shape[1]
    blk = pl.BlockSpec((seq, d), lambda b: (b, 0))
    return pl.pallas_call(
        functools.partial(_attn_prompt_kernel, lam_init=lam_init),
        out_shape=jax.ShapeDtypeStruct((batch * seq, d), BF16),
        grid=(batch,),
        in_specs=[
            pl.BlockSpec(lam_vecs.shape, lambda b: (0, 0)),
            pl.BlockSpec(gain.shape, lambda b: (0, 0)),
            blk, blk, blk,
        ],
        out_specs=blk,
        compiler_params=_params(("arbitrary",)),
        name="diff_attn_ctx",
    )(lam_vecs, gain, q, k, v)


def _attn_sample(lam_vecs, gain, q, k, v, kc, vc, batch, seq, n_ctx, row_offset, lam_init):
    d = q.shape[1]
    tq = Q_TILE
    qpt = seq // tq
    off_seq = row_offset // seq
    off_tile = row_offset // tq
    qmap = lambda b, h, i: (off_tile + b * qpt + i, h)
    kmap = lambda b, h, i: (off_seq + b, h)
    cmap = lambda b, h, i: (b, h)
    return pl.pallas_call(
        functools.partial(_attn_latent_kernel, lam_init=lam_init),
        out_shape=jax.ShapeDtypeStruct((batch * seq, d), BF16),
        grid=(batch, N_HEADS, qpt),
        in_specs=[
            pl.BlockSpec(lam_vecs.shape, lambda b, h, i: (0, 0)),
            pl.BlockSpec(gain.shape, lambda b, h, i: (0, 0)),
            pl.BlockSpec((tq, V_DIM), qmap),
            pl.BlockSpec((seq, V_DIM), kmap),
            pl.BlockSpec((seq, V_DIM), kmap),
            pl.BlockSpec((n_ctx, V_DIM), cmap),
            pl.BlockSpec((n_ctx, V_DIM), cmap),
        ],
        out_specs=pl.BlockSpec((tq, V_DIM), lambda b, h, i: (b * qpt + i, h)),
        scratch_shapes=[pltpu.VMEM((seq + n_ctx, V_DIM), BF16), pltpu.VMEM((seq + n_ctx, 2 * V_DIM), BF16)],
        compiler_params=_params(("arbitrary", "arbitrary", "arbitrary")),
        name="diff_attn_latent",
    )(lam_vecs, gain, q, k, v, kc, vc)


def _route(lg):
    lane = lax.broadcasted_iota(I32, lg.shape, 1)
    lane_f = lane.astype(F32)
    neg = jnp.float32(-jnp.inf)
    big = jnp.float32(LANES)
    gl = jnp.where(lane < N_EXPERT_GROUPS, lg, neg)
    gmax = jnp.max(gl, axis=-1, keepdims=True)
    g_prob = 1.0 / jnp.sum(jnp.exp(gl - gmax), axis=-1, keepdims=True)
    g_idx = jnp.min(jnp.where(gl == gmax, lane_f, big), axis=-1, keepdims=True)
    lo = N_EXPERT_GROUPS + EXPERTS_PER_GROUP * g_idx
    el = jnp.where((lane_f >= lo) & (lane_f < lo + EXPERTS_PER_GROUP), lg, neg)
    m1 = jnp.max(el, axis=-1, keepdims=True)
    i1 = jnp.min(jnp.where(el == m1, lane_f, big), axis=-1, keepdims=True)
    el2 = jnp.where(lane_f == i1, neg, el)
    m2 = jnp.max(el2, axis=-1, keepdims=True)
    i2 = jnp.min(jnp.where(el2 == m2, lane_f, big), axis=-1, keepdims=True)
    t = jnp.exp(m2 - m1)
    w1 = g_prob / (1.0 + t)
    w2 = g_prob * t / (1.0 + t)
    out = jnp.where(lane == 0, i1 - N_EXPERT_GROUPS, 0.0)
    out = jnp.where(lane == 1, i2 - N_EXPERT_GROUPS, out)
    out = jnp.where(lane == 2, w1, out)
    out = jnp.where(lane == 3, w2, out)
    return out


def _pack_bf16_pairs(h):
    n = h.shape[1] // 2
    bits = lax.bitcast_convert_type(h.astype(BF16).astype(F32), jnp.uint32)
    return (bits[:, :n] >> 16) | bits[:, n:]


def _unpack_bf16_pairs(p):
    lo = lax.bitcast_convert_type(p << 16, F32)
    hi = lax.bitcast_convert_type(p & jnp.uint32(0xFFFF0000), F32)
    return jnp.concatenate([lo, hi], axis=1)


def _mix_out_kernel(*refs, n_prompt_tiles, n_x):
    ap_ref, as_ref, w_ref = refs[:3]
    x_refs = refs[3:3 + n_x]
    mod_ref, gb_ref, wr_ref, x1_ref, hp_ref, route_ref = refs[3 + n_x:]
    is_prompt = pl.program_id(0) < n_prompt_tiles
    for rows in _row_chains(x1_ref.shape[0]):
        a = jnp.where(is_prompt, ap_ref[rows, :], as_ref[rows, :])
        out = _dot(a, w_ref[...])
        x = _read_rows(x_refs, n_prompt_tiles, rows)
        x1 = _deepnorm(x, mod_ref[2:3, :], out) * gb_ref[0:1, :] + gb_ref[1:2, :]
        x1_ref[rows, :] = x1
        h2 = _layernorm(x1) * (1.0 + mod_ref[4:5, :]) + mod_ref[3:4, :]
        hp_ref[rows, :] = _pack_bf16_pairs(h2)
        hi = h2.astype(BF16)
        lo = (h2 - hi.astype(F32)).astype(BF16)
        logits = _dot(hi, wr_ref[0]) + _dot(lo, wr_ref[0]) + _dot(hi, wr_ref[1])
        route_ref[rows, :] = _route(logits)


def _mix_out(a_prompt, a_sample, w, xs, mod, gb, wr, n_prompt, dec_seq, t_tok):
    d = w.shape[1]
    tl = _Tiles(n_prompt, dec_seq, TOKEN_TILE)
    tm = tl.tm
    return pl.pallas_call(
        functools.partial(_mix_out_kernel, n_prompt_tiles=tl.npt, n_x=len(xs)),
        out_shape=(
            jax.ShapeDtypeStruct((t_tok, d), F32),
            jax.ShapeDtypeStruct((t_tok, d // 2), jnp.uint32),
            jax.ShapeDtypeStruct((t_tok, LANES), F32),
        ),
        grid=(t_tok // tm,),
        in_specs=[
            pl.BlockSpec((tm, d), tl.prompt_part),
            pl.BlockSpec((tm, d), tl.latent_part),
            pl.BlockSpec(w.shape, lambda t: (0, 0)),
        ] + tl.source_specs(xs, d) + [
            pl.BlockSpec((None, N_MOD, d), tl.cond),
            pl.BlockSpec(gb.shape, lambda t: (0, 0)),
            pl.BlockSpec(wr.shape, lambda t: (0, 0, 0)),
        ],
        out_specs=(
            pl.BlockSpec((tm, d), tl.row),
            pl.BlockSpec((tm, d // 2), tl.row),
            pl.BlockSpec((tm, LANES), tl.row),
        ),
        compiler_params=_params(("arbitrary",)),
        name="mix_out_postnorm_router",
    )(a_prompt, a_sample, w, *xs, mod, gb, wr)


def _lane_prefix_sum(x, lane):
    sh = 1
    while sh < LANES:
        x = x + jnp.where(lane >= sh, pltpu.roll(x, sh, 1), 0.0)
        sh *= 2
    return x


def _lane_suffix_min_exclusive(x, lane):
    big = float(LANES)
    y = jnp.where(lane + 1 < LANES, pltpu.roll(x, LANES - 1, 1), big)
    sh = 1
    while sh < LANES:
        y = jnp.minimum(y, jnp.where(lane + sh < LANES, pltpu.roll(y, LANES - sh, 1), big))
        sh *= 2
    return y


def _work_items(counts, starts, ends, lane8, n_moe_tiles):
    shift = int(math.log2(MOE_TILE))
    first_tile = (starts.astype(I32) >> shift).astype(F32)
    last_tile = ((ends.astype(I32) - 1) >> shift).astype(F32)
    n_it = jnp.where(counts > 0.0, last_tile - first_tile + 1.0, 0.0)
    it_end = _lane_prefix_sum(n_it, lane8)
    it_start = it_end - n_it
    total = it_end[0:1, LANES - 1:LANES]
    sub = lax.broadcasted_iota(I32, (LANES, LANES), 0)
    rows = lambda x: jnp.broadcast_to(x[0:1, :], (LANES, LANES))
    used = counts > 0.0
    weight_slot = ((_lane_prefix_sum(jnp.where(used, 1.0, 0.0), lane8) - 1.0).astype(I32) & 1).astype(F32)
    next_used = _lane_suffix_min_exclusive(jnp.where(used, lane8.astype(F32), float(LANES)), lane8)
    per_expert = (starts, ends, first_tile, it_start, it_end, weight_slot, next_used)
    stacked = jnp.zeros((LANES, LANES), F32)
    for j, vec in enumerate(per_expert):
        stacked = jnp.where(sub == j, rows(vec), stacked)
    cols = stacked.T
    col = lambda j: cols[:, j:j + 1]
    sub_f = sub.astype(F32)
    w = lax.broadcasted_iota(I32, (LANES, LANES), 1).astype(F32)
    ex = jnp.sum(jnp.where((sub < N_EXPERTS) & (col(4) <= w), 1.0, 0.0), axis=0, keepdims=True)
    ex = jnp.minimum(ex, N_EXPERTS - 1.0)
    w1 = w[0:1, :]
    valid = w1 < total
    ex = jnp.where(valid, ex, jnp.max(jnp.where(valid, ex, 0.0), axis=-1, keepdims=True))
    onehot = sub_f == ex
    pick = lambda j: jnp.sum(jnp.where(onehot, col(j), 0.0), axis=0, keepdims=True)
    tile = jnp.where(valid, pick(2) + (w1 - pick(3)), n_moe_tiles - 1.0)
    lo = jnp.where(valid, jnp.maximum(pick(0), tile * MOE_TILE), 0.0)
    hi = jnp.where(valid, jnp.minimum(pick(1), (tile + 1.0) * MOE_TILE), 0.0)
    b8 = lambda x: jnp.broadcast_to(x, (SUBLANES, LANES))
    ex8, tile8 = b8(ex), b8(tile)
    first = jnp.where((lane8 == 0) | (tile8 != pltpu.roll(tile8, 1, 1)), 1.0, 0.0)
    newexp = jnp.where((lane8 == 0) | (ex8 != pltpu.roll(ex8, 1, 1)), 1.0, 0.0)
    sub8 = lax.broadcasted_iota(I32, (SUBLANES, LANES), 0)
    table = jnp.zeros((SUBLANES, LANES), F32)
    for j, vec in enumerate((ex8, tile8, b8(lo), b8(hi), first, newexp, b8(pick(5)), b8(pick(6)))):
        table = jnp.where(sub8 == j, vec, table)
    return table


def _plan_kernel(route_ref, pos_ref, items_ref, tri_ref, carry_ref, tot_ref, *, n_moe_tiles):
    p = pl.program_id(0)
    t = pl.program_id(1)
    tm = route_ref.shape[0]
    r = route_ref[...]
    lane = lax.broadcasted_iota(I32, (tm, LANES), 1)
    lane_f = lane.astype(F32)
    e0 = r[:, 0:1]
    e1 = r[:, 1:2] + N_EXPERTS
    m = jnp.where((lane_f == e0) | (lane_f == e1), 1.0, 0.0)
    colsum = jnp.sum(m, axis=0, keepdims=True)

    @pl.when((p == 0) & (t == 0))
    def _():
        carry_ref[...] = jnp.zeros_like(carry_ref)
        row = lax.broadcasted_iota(I32, (tm, tm), 0)
        col = lax.broadcasted_iota(I32, (tm, tm), 1)
        tri_ref[...] = jnp.where(row > col, 1.0, 0.0).astype(BF16)

    @pl.when(p == 0)
    def _():
        carry_ref[...] += colsum

    @pl.when((p == 0) & (t == pl.num_programs(1) - 1))
    def _():
        tot_ref[...] = carry_ref[...]
        carry_ref[...] = jnp.zeros_like(carry_ref)

    @pl.when(p == 1)
    def _():
        lane8 = lax.broadcasted_iota(I32, (SUBLANES, LANES), 1)
        tot = tot_ref[...]
        is_first = lane8 < N_EXPERTS
        tot0 = jnp.where(is_first, tot, 0.0)
        counts = jnp.where(is_first, tot + pltpu.roll(tot, LANES - N_EXPERTS, 1), 0.0)
        ends = _lane_prefix_sum(counts, lane8)
        starts = ends - counts
        base = jnp.where(is_first, starts, pltpu.roll(starts + tot0, N_EXPERTS, 1))
        before = _dot(tri_ref[...], m.astype(BF16)) + carry_ref[0:1, :]
        carry_ref[...] += colsum
        rows = before + base[0:1, :]
        pos0 = jnp.sum(jnp.where(lane_f == e0, rows, 0.0), axis=-1, keepdims=True)
        pos1 = jnp.sum(jnp.where(lane_f == e1, rows, 0.0), axis=-1, keepdims=True)
        both = jnp.where(lane == 0, pos0, jnp.where(lane == 1, pos1, 0.0))
        pos_ref[...] = both.T[0:SUBLANES, :].astype(I32)

        @pl.when(t == 0)
        def _():
            items_ref[...] = _work_items(counts, starts, ends, lane8, n_moe_tiles).astype(I32)


def _moe_plan(route):
    t_tok = route.shape[0]
    tm = PLAN_TILE
    n_moe_tiles = 2 * t_tok // MOE_TILE
    assert n_moe_tiles + N_EXPERTS - 1 <= LANES and 2 * N_EXPERTS <= LANES and MOE_TILE & (MOE_TILE - 1) == 0
    return pl.pallas_call(
        functools.partial(_plan_kernel, n_moe_tiles=n_moe_tiles),
        out_shape=(jax.ShapeDtypeStruct((SUBLANES, t_tok), I32), jax.ShapeDtypeStruct((SUBLANES, LANES), I32)),
        grid=(2, t_tok // tm),
        in_specs=[pl.BlockSpec((tm, LANES), lambda p, t: (t, 0))],
        out_specs=(pl.BlockSpec((SUBLANES, tm), lambda p, t: (0, t * p)),
                   pl.BlockSpec((SUBLANES, LANES), lambda p, t: (0, 0))),
        scratch_shapes=[pltpu.VMEM((tm, tm), BF16), pltpu.VMEM((SUBLANES, LANES), F32),
                        pltpu.VMEM((SUBLANES, LANES), F32)],
        compiler_params=_params(("arbitrary", "arbitrary")),
        name="moe_positions",
    )(route)


def _sc_workers():
    info = plsc.get_sparse_core_info()
    return info.num_cores, info.num_cores * info.num_subcores


def _sc_pipeline(n_chunks, load, store):
    load(0, 0).start()
    for j in range(n_chunks):
        b = j % 2
        if j + 1 < n_chunks:
            if j >= 1:
                store(j - 1, 1 - b).wait()
            load(j + 1, 1 - b).start()
        load(j, b).wait()
        store(j, b).start()
    if n_chunks >= 2:
        store(n_chunks - 2, n_chunks % 2).wait()
    store(n_chunks - 1, (n_chunks - 1) % 2).wait()


def _sc_row_mover(n, d, dtype, name, body_of):
    n_cores, n_workers = _sc_workers()
    per_worker = n // n_workers
    k = SC_BUFFER_BYTES // (d * 4)
    n_chunks = per_worker // k
    assert n_chunks * k * n_workers == n and k <= SC_MAX_INDEX_CHUNK
    call = pl.kernel(
        body_of(n_cores, per_worker, k, n_chunks),
        out_type=jax.ShapeDtypeStruct((n, d), dtype),
        mesh=plsc.VectorSubcoreMesh(core_axis_name="c", subcore_axis_name="s"),
        scratch_types=[pltpu.VMEM((n_chunks, k), I32), pltpu.VMEM((2, k, d), dtype),
                       pltpu.SemaphoreType.DMA((2,)), pltpu.SemaphoreType.DMA((2,))],
        name=name,
    )
    return lambda rows, idx: call(rows, idx.reshape(n_workers, n_chunks, k))


def _sc_gather_rows(table, idx):
    def body_of(n_cores, per_worker, k, n_chunks):
        def body(table_hbm, idx_hbm, out_hbm, idx_v, rows_v, gsem, osem):
            wid = lax.axis_index("s") * n_cores + lax.axis_index("c")
            base = wid * per_worker
            pltpu.sync_copy(idx_hbm.at[wid], idx_v)
            _sc_pipeline(
                n_chunks,
                lambda j, b: pltpu.make_async_copy(table_hbm.at[idx_v.at[j]], rows_v.at[b], gsem.at[b]),
                lambda j, b: pltpu.make_async_copy(rows_v.at[b], out_hbm.at[pl.ds(base + j * k, k)],
                                                   osem.at[b]))
        return body
    return _sc_row_mover(idx.shape[0], table.shape[1], table.dtype, "sc_row_gather", body_of)(table, idx)


def _sc_scatter_rows(src, idx):
    n_src = src.shape[0]

    def body_of(n_cores, per_worker, k, n_chunks):
        assert n_src % per_worker == 0

        def body(src_hbm, idx_hbm, out_hbm, idx_v, rows_v, gsem, osem):
            wid = lax.axis_index("s") * n_cores + lax.axis_index("c")
            base = lax.rem(wid * per_worker, n_src)
            pltpu.sync_copy(idx_hbm.at[wid], idx_v)
            _sc_pipeline(
                n_chunks,
                lambda j, b: pltpu.make_async_copy(src_hbm.at[pl.ds(base + j * k, k)], rows_v.at[b],
                                                   gsem.at[b]),
                lambda j, b: pltpu.make_async_copy(rows_v.at[b], out_hbm.at[idx_v.at[j]], osem.at[b]))
        return body
    return _sc_row_mover(idx.shape[0], src.shape[1], src.dtype, "sc_row_scatter", body_of)(src, idx)


IT_EXPERT, IT_TILE, IT_LO, IT_HI, IT_FIRST, IT_NEWEXP, IT_SLOT, IT_NEXT = range(8)


def _moe_kernel(it_ref, xs_ref, wg_hbm, wu_hbm, wd_hbm, o_ref, wg_f, wu_f, wd_f, wgu_s, wd_s, sem, *, layer):
    w = pl.program_id(0)
    f = wg_f.shape[2]

    def weight_copies(expert, slot):
        return [pltpu.make_async_copy(src.at[layer, expert], dst.at[slot], sem.at[slot, j])
                for j, (src, dst) in enumerate(((wg_hbm, wg_f), (wu_hbm, wu_f), (wd_hbm, wd_f)))]

    @pl.when(it_ref[IT_NEWEXP, w] == 1)
    def _():
        expert = it_ref[IT_EXPERT, w]
        slot = it_ref[IT_SLOT, w]
        nxt = it_ref[IT_NEXT, w]

        @pl.when(w == 0)
        def _():
            for cp in weight_copies(expert, slot):
                cp.start()

        for cp in weight_copies(expert, slot):
            cp.wait()

        @pl.when(nxt < N_EXPERTS)
        def _():
            for cp in weight_copies(nxt, 1 - slot):
                cp.start()

        wgu_s[:, :f] = wg_f[slot].astype(BF16)
        wgu_s[:, f:] = wu_f[slot].astype(BF16)
        wd_s[...] = wd_f[slot].astype(BF16)

    tm = xs_ref.shape[0]
    x = _unpack_bf16_pairs(xs_ref[...]).astype(BF16)
    au = _dot(x, wgu_s[...])
    act = (_silu(au[:, :f]) * au[:, f:]).astype(BF16)
    y = _pack_bf16_pairs(_dot(act, wd_s[...]))
    row = it_ref[IT_TILE, w] * tm + lax.broadcasted_iota(I32, (tm, 1), 0)
    mine = (row >= it_ref[IT_LO, w]) & (row < it_ref[IT_HI, w])

    @pl.when(it_ref[IT_FIRST, w] == 1)
    def _():
        o_ref[...] = jnp.where(mine, y, jnp.zeros_like(y))

    @pl.when(it_ref[IT_FIRST, w] == 0)
    def _():
        o_ref[...] = jnp.where(mine, y, o_ref[...])


def _moe_experts(items, xs, w_gate, w_up, w_down, layer):
    n_rows, dp = xs.shape
    _, _, d, f = w_gate.shape
    tm = MOE_TILE
    rmap = lambda w, it: (it[IT_TILE, w], 0)
    hbm = pl.BlockSpec(memory_space=pl.ANY)
    return pl.pallas_call(
        functools.partial(_moe_kernel, layer=layer),
        out_shape=jax.ShapeDtypeStruct((n_rows, d // 2), jnp.uint32),
        grid_spec=pltpu.PrefetchScalarGridSpec(
            num_scalar_prefetch=1,
            grid=(n_rows // tm + N_EXPERTS - 1,),
            in_specs=[pl.BlockSpec((tm, dp), rmap), hbm, hbm, hbm],
            out_specs=pl.BlockSpec((tm, d // 2), rmap),
            scratch_shapes=[
                pltpu.VMEM((2, d, f), F32), pltpu.VMEM((2, d, f), F32), pltpu.VMEM((2, f, d), F32),
                pltpu.VMEM((d, 2 * f), BF16), pltpu.VMEM((f, d), BF16),
                pltpu.SemaphoreType.DMA((2, 3)),
            ],
        ),
        compiler_params=_params(("arbitrary",)),
        name="moe_grouped_mlp",
    )(items, xs, w_gate, w_up, w_down)


def _moe_combine_kernel(x_ref, y0_ref, y1_ref, route_ref, mod_ref, gb_ref, *o_refs, n_prompt_tiles):
    r = route_ref[...]
    moe = r[:, 2:3] * _unpack_bf16_pairs(y0_ref[...]) + r[:, 3:4] * _unpack_bf16_pairs(y1_ref[...])
    x2 = _deepnorm(x_ref[...], mod_ref[5:6, :], moe) * gb_ref[0:1, :] + gb_ref[1:2, :]
    if len(o_refs) == 1:
        o_refs[0][...] = x2
    else:
        @pl.when(pl.program_id(0) < n_prompt_tiles)
        def _():
            o_refs[0][...] = x2

        @pl.when(pl.program_id(0) >= n_prompt_tiles)
        def _():
            o_refs[1][...] = x2


def _moe_combine(x1, ys, route, mod, gb, n_prompt, dec_seq, split):
    t_tok, d = x1.shape
    tl = _Tiles(n_prompt, dec_seq, TOKEN_TILE)
    tm = tl.tm
    nt = t_tok // tm
    if split:
        out_shape = (jax.ShapeDtypeStruct((n_prompt, d), F32), jax.ShapeDtypeStruct((t_tok - n_prompt, d), F32))
        out_specs = (pl.BlockSpec((tm, d), tl.prompt_part), pl.BlockSpec((tm, d), tl.latent_part))
    else:
        out_shape = jax.ShapeDtypeStruct((t_tok, d), F32)
        out_specs = pl.BlockSpec((tm, d), tl.row)
    return pl.pallas_call(
        functools.partial(_moe_combine_kernel, n_prompt_tiles=tl.npt),
        out_shape=out_shape,
        grid=(nt,),
        in_specs=[
            pl.BlockSpec((tm, d), tl.row),
            pl.BlockSpec((tm, ys.shape[1]), tl.row),
            pl.BlockSpec((tm, ys.shape[1]), lambda t: (t + nt, 0)),
            pl.BlockSpec((tm, LANES), tl.row),
            pl.BlockSpec((None, N_MOD, d), tl.cond),
            pl.BlockSpec(gb.shape, lambda t: (0, 0)),
        ],
        out_specs=out_specs,
        compiler_params=_params(("arbitrary",)),
        name="moe_combine_postnorm",
    )(x1, ys, ys, route, mod, gb)


def _rope_tables(n_lat):
    nf = HEAD_DIM // 4
    s = np.arange(n_lat)
    lane = np.arange(LANES)
    inv = ROPE_BASE ** (-(lane % nf).astype(np.float64) / nf)
    use_col = (lane % HEAD_DIM) >= HEAD_DIM // 2
    p = np.where(use_col[None, :], (s % GRID_W)[:, None], (s // GRID_W)[:, None]).astype(np.float64)
    ang = p * inv[None, :]
    sign = np.where((lane % (2 * nf)) < nf, -1.0, 1.0)
    return jnp.asarray(np.cos(ang), F32), jnp.asarray(np.sin(ang) * sign[None, :], F32)


def _dft_tables(n):
    k = np.arange(n)
    ang = 2.0 * np.pi * ((k[:, None] * k[None, :]) % n).astype(np.float64) / n
    return np.cos(ang), np.sin(ang)


def kernel(x_prompt, x_sample, cache_k, cache_v, c, c_ctx, w_ada, b_ada, ln_gain, ln_bias, w_qkv, w_attn_out,
           lambda_q1, lambda_k1, lambda_q2, lambda_k2, subln_gain, w_fourier_out, w_router_group,
           w_router_expert, w_expert_gate, w_expert_up, w_expert_down):
    bp, sp, d = x_prompt.shape
    bs, n_lat, _ = x_sample.shape
    n_ctx = cache_k.shape[2]
    n_prompt = bp * sp
    t_tok = n_prompt + bs * n_lat
    assert d == D_MODEL and n_prompt % n_lat == 0 and n_lat % TOKEN_TILE == 0 and sp == QKV_TILE

    cond = jnp.concatenate([c_ctx[None, :], c, jnp.zeros((SUBLANES - 1 - bs, d), F32)], axis=0)
    mods = _ada_all(cond, w_ada, b_ada).reshape(DEPTH, SUBLANES, N_MOD, d)

    cos, sin = _rope_tables(n_lat)
    cc, sc = _dft_tables(FOURIER_GROUP_DIM)
    dcs = jnp.asarray(np.concatenate([cc, sc], axis=1), BF16)
    seq_tabs = {s: tuple(jnp.asarray(m, BF16) for m in _dft_tables(s)) for s in (sp, n_lat)}
    attn_layers = [i for i in range(DEPTH) if i % 2 == 0]

    xs = (x_prompt.reshape(n_prompt, d), x_sample.reshape(bs * n_lat, d))
    prev_kv = []
    new_k = new_v = None
    for i in range(DEPTH):
        mod = mods[i]
        if i % 2 == 0:
            a = i // 2
            lam_init = 0.8 - 0.6 * math.exp(-0.3 * i)
            last_attn = i == attn_layers[-1]
            q, k, v, kf, vf = _qkv(xs, mod, w_qkv[a].astype(BF16), cos, sin, prev_kv, last_attn, n_prompt, n_lat,
                                   t_tok)
            if last_attn:
                new_k, new_v = kf, vf
            else:
                prev_kv.append((kf, vf))
            lam_vecs = jnp.stack([lambda_q1[a], lambda_k1[a], lambda_q2[a], lambda_k2[a]], axis=0)
            gain = subln_gain[a][None, :]
            kc = cache_k[:, a].reshape(bs * n_ctx, d).astype(BF16)
            vc = cache_v[:, a].reshape(bs * n_ctx, d).astype(BF16)
            mixed_p = _attn_prompt(lam_vecs, gain, q, k, v, bp, sp, lam_init)
            mixed_s = _attn_sample(lam_vecs, gain, q, k, v, kc, vc, bs, n_lat, n_ctx, n_prompt, lam_init)
            w_mix = w_attn_out[a]
        else:
            xc, xsn = _chan_dft(xs[0], mod, dcs, n_prompt, n_lat)
            mixed_p = _seq_dft(*seq_tabs[sp], xc, xsn, bp, sp, 0)
            mixed_s = _seq_dft(*seq_tabs[n_lat], xc, xsn, bs, n_lat, n_prompt)
            w_mix = w_fourier_out[i // 2]
        gb0 = jnp.stack([ln_gain[i, 0], ln_bias[i, 0]], axis=0)
        gb1 = jnp.stack([ln_gain[i, 1], ln_bias[i, 1]], axis=0)
        wr = jnp.concatenate([w_router_group[i], w_router_expert[i],
                              jnp.zeros((d, LANES - N_EXPERT_GROUPS - N_EXPERTS), F32)], axis=1)
        wr_hi = wr.astype(BF16)
        wr_lo = (wr - wr_hi.astype(F32)).astype(BF16)
        x1, hp, route = _mix_out(mixed_p, mixed_s, w_mix.astype(BF16), xs, mod, gb0,
                                 jnp.stack([wr_hi, wr_lo], axis=0), n_prompt, n_lat, t_tok)
        pos8, items = _moe_plan(route)
        pos = pos8[0:2].reshape(-1)
        xs_sorted = _sc_scatter_rows(hp, pos)
        ys = _moe_experts(items, xs_sorted, w_expert_gate, w_expert_up, w_expert_down, i)
        yg = _sc_gather_rows(ys, pos)
        last = i == DEPTH - 1
        out = _moe_combine(x1, yg, route, mod, gb1, n_prompt, n_lat, split=last)
        xs = out if last else (out,)

    y_prompt = xs[0].reshape(bp, sp, d)
    y_sample = xs[1].reshape(bs, n_lat, d)
    return (y_prompt, y_sample, new_k.reshape(bp, len(attn_layers), sp, N_HEADS, 2 * HEAD_DIM),
            new_v.reshape(bp, len(attn_layers), sp, N_HEADS, V_DIM))
```

```python
import functools
import math

import numpy as np
import jax
import jax.numpy as jnp
from jax import lax
from jax.experimental import pallas as pl
from jax.experimental.pallas import tpu as pltpu
from jax.experimental.pallas import tpu_sc as plsc

F32 = jnp.float32
BF16 = jnp.bfloat16
I32 = jnp.int32

D_MODEL = 1024
DEPTH = 4
GRID_W = 64
N_HEADS = 8
HEAD_DIM = 64
V_DIM = 2 * HEAD_DIM
ROPE_BASE = 10000.0
N_FOURIER_GROUPS = 8
FOURIER_GROUP_DIM = D_MODEL // N_FOURIER_GROUPS
N_EXPERT_GROUPS = 4
EXPERTS_PER_GROUP = 8
N_EXPERTS = N_EXPERT_GROUPS * EXPERTS_PER_GROUP
D_EXPERT = 256
N_MOD = 6
LN_EPS = 1e-5
DEEPNORM_ALPHA = (2.0 * DEPTH) ** 0.25
Q_SCALE = math.log2(math.e) * HEAD_DIM ** -0.5

LANES = 128
SUBLANES = 8
TOKEN_TILE = 512
CHAIN_ROWS = 256
QKV_TILE = 256
Q_TILE = 512
ATTN_ROW_CHUNK = 128
MOE_TILE = 512
PLAN_TILE = 1024
SC_BUFFER_BYTES = 128 * 1024
SC_MAX_INDEX_CHUNK = 128
VMEM_LIMIT = 48 * 1024 * 1024


def _params(semantics):
    return pltpu.CompilerParams(dimension_semantics=semantics, vmem_limit_bytes=VMEM_LIMIT)


def _layernorm(x, eps=LN_EPS):
    mu = jnp.mean(x, axis=-1, keepdims=True)
    xc = x - mu
    var = jnp.mean(xc * xc, axis=-1, keepdims=True)
    return xc * lax.rsqrt(var + eps)


def _deepnorm(x, branch_gate, branch):
    return _layernorm(x + (branch_gate * (1.0 / DEEPNORM_ALPHA)) * branch, LN_EPS / DEEPNORM_ALPHA ** 2)


def _silu(a):
    return a / (1.0 + jnp.exp(-a))


def _dot(a, b):
    return jnp.dot(a, b, preferred_element_type=F32)


class _Tiles:
    def __init__(self, n_prompt, dec_seq, tm):
        self.tm = tm
        self.npt = n_prompt // tm
        self.tps = dec_seq // tm

    def row(self, t):
        return (t, 0)

    def cond(self, t):
        return (jnp.where(t < self.npt, 0, (t - self.npt) // self.tps + 1), 0, 0)

    def latent_pos(self, t):
        return (jnp.maximum(t - self.npt, 0) % self.tps, 0)

    def prompt_part(self, t):
        return (jnp.minimum(t, self.npt - 1), 0)

    def latent_part(self, t):
        return (jnp.maximum(t - self.npt, 0), 0)

    def source_specs(self, arrays, d):
        if len(arrays) == 1:
            return [pl.BlockSpec((self.tm, d), self.row)]
        return [pl.BlockSpec((self.tm, d), self.prompt_part), pl.BlockSpec((self.tm, d), self.latent_part)]


def _read_rows(refs, n_prompt_tiles, rows=slice(None)):
    if len(refs) == 1:
        return refs[0][rows, :]
    return jnp.where(pl.program_id(0) < n_prompt_tiles, refs[0][rows, :], refs[1][rows, :])


def _ada_kernel(cond_ref, w_ref, b_ref, o_ref):
    a = _silu(cond_ref[...])
    o_ref[...] = _dot(a.astype(BF16), w_ref[...].astype(BF16)) + b_ref[...]


def _ada_all(cond, w_ada, b_ada):
    depth, d, n = w_ada.shape
    tn = n // 4
    return pl.pallas_call(
        _ada_kernel,
        out_shape=jax.ShapeDtypeStruct((depth, cond.shape[0], n), F32),
        grid=(depth, n // tn),
        in_specs=[
            pl.BlockSpec(cond.shape, lambda l, j: (0, 0)),
            pl.BlockSpec((None, d, tn), lambda l, j: (l, 0, j)),
            pl.BlockSpec((None, 1, tn), lambda l, j: (l, 0, j)),
        ],
        out_specs=pl.BlockSpec((None, cond.shape[0], tn), lambda l, j: (l, 0, j)),
        compiler_params=_params(("arbitrary", "arbitrary")),
        name="ada",
    )(cond, w_ada, b_ada.reshape(depth, 1, n))


def _rope(x, cos, sin_signed, first_half):
    outs = []
    for c in range(x.shape[1] // LANES):
        xc = x[:, c * LANES:(c + 1) * LANES]
        partner = jnp.where(first_half, pltpu.roll(xc, LANES - 16, 1), pltpu.roll(xc, 16, 1))
        outs.append(xc * cos + partner * sin_signed)
    return jnp.concatenate(outs, axis=1)


def _store_heads(cache_ref, slot, rows):
    for hd in range(N_HEADS):
        cache_ref[slot, :, hd, :] = rows[:, hd * V_DIM:(hd + 1) * V_DIM]


def _qkv_kernel(*refs, n_prompt_tiles, n_x, n_prev):
    x_refs, (mod_ref, w_ref, cos_ref, sin_ref) = refs[:n_x], refs[n_x:n_x + 4]
    prev = refs[n_x + 4:n_x + 4 + 2 * max(n_prev, 0)]
    q_ref, k_ref, v_ref, ko_ref, vo_ref = refs[n_x + 4 + 2 * max(n_prev, 0):]
    t = pl.program_id(0)
    h = _layernorm(_read_rows(x_refs, n_prompt_tiles)) * (1.0 + mod_ref[1:2, :]) + mod_ref[0:1, :]
    acc = _dot(h.astype(BF16), w_ref[...])
    d = w_ref.shape[0]
    q = acc[:, :d] * Q_SCALE
    k = acc[:, d:2 * d]
    v = acc[:, 2 * d:]
    v_ref[...] = v.astype(BF16)

    @pl.when(t < n_prompt_tiles)
    def _():
        q_ref[...] = q.astype(BF16)
        k_ref[...] = k.astype(BF16)
        if n_prev < 0:
            ko_ref[...] = k
            vo_ref[...] = v
        else:
            for a in range(n_prev):
                _store_heads(ko_ref, a, prev[2 * a][...])
                _store_heads(vo_ref, a, prev[2 * a + 1][...])
            _store_heads(ko_ref, n_prev, k)
            _store_heads(vo_ref, n_prev, v)

    @pl.when(t >= n_prompt_tiles)
    def _():
        lane = lax.broadcasted_iota(I32, (q.shape[0], LANES), 1)
        first_half = (lane % 32) < 16
        cos = cos_ref[...]
        sin = sin_ref[...]
        q_ref[...] = _rope(q, cos, sin, first_half).astype(BF16)
        k_ref[...] = _rope(k, cos, sin, first_half).astype(BF16)


def _qkv(xs, mod, w_qkv, cos, sin, prev_kv, finish_cache, n_prompt, dec_seq, t_tok):
    d = w_qkv.shape[0]
    tl = _Tiles(n_prompt, dec_seq, QKV_TILE)
    tm = tl.tm
    n_prev = len(prev_kv) if finish_cache else -1
    if finish_cache:
        n_slots = n_prev + 1
        kv_shape = jax.ShapeDtypeStruct((n_prompt // tm, n_slots, tm, N_HEADS, V_DIM), F32)
        kv_spec = pl.BlockSpec((None, n_slots, tm, N_HEADS, V_DIM), lambda t: (tl.prompt_part(t)[0], 0, 0, 0, 0))
    else:
        kv_shape = jax.ShapeDtypeStruct((n_prompt, d), F32)
        kv_spec = pl.BlockSpec((tm, d), tl.prompt_part)
    prev_flat = [a for kv in prev_kv for a in kv] if finish_cache else []
    return pl.pallas_call(
        functools.partial(_qkv_kernel, n_prompt_tiles=tl.npt, n_x=len(xs), n_prev=n_prev),
        out_shape=(
            jax.ShapeDtypeStruct((t_tok, d), BF16),
            jax.ShapeDtypeStruct((t_tok, d), BF16),
            jax.ShapeDtypeStruct((t_tok, d), BF16),
            kv_shape, kv_shape,
        ),
        grid=(t_tok // tm,),
        in_specs=tl.source_specs(xs, d) + [
            pl.BlockSpec((None, N_MOD, d), tl.cond),
            pl.BlockSpec(w_qkv.shape, lambda t: (0, 0)),
            pl.BlockSpec((tm, LANES), tl.latent_pos),
            pl.BlockSpec((tm, LANES), tl.latent_pos),
        ] + [pl.BlockSpec((tm, d), tl.prompt_part) for _ in prev_flat],
        out_specs=(
            pl.BlockSpec((tm, d), tl.row),
            pl.BlockSpec((tm, d), tl.row),
            pl.BlockSpec((tm, d), tl.row),
            kv_spec, kv_spec,
        ),
        compiler_params=_params(("arbitrary",)),
        name="ln_qkv_rope",
    )(*xs, mod, w_qkv, cos, sin, *prev_flat)


def _row_chains(n_rows):
    return [slice(r, r + CHAIN_ROWS) for r in range(0, n_rows, CHAIN_ROWS)]


def _chan_dft_kernel(x_ref, mod_ref, dcs_ref, xc_ref, xs_ref):
    g = FOURIER_GROUP_DIM
    for rows in _row_chains(x_ref.shape[0]):
        h = (_layernorm(x_ref[rows, :]) * (1.0 + mod_ref[1:2, :]) + mod_ref[0:1, :]).astype(BF16)
        for i in range(N_FOURIER_GROUPS):
            r = _dot(h[:, i * g:(i + 1) * g], dcs_ref[...])
            xc_ref[rows, i * g:(i + 1) * g] = r[:, :g].astype(BF16)
            xs_ref[rows, i * g:(i + 1) * g] = r[:, g:].astype(BF16)


def _chan_dft(x, mod, dcs, n_prompt, dec_seq):
    t_tok, d = x.shape
    tl = _Tiles(n_prompt, dec_seq, TOKEN_TILE)
    return pl.pallas_call(
        _chan_dft_kernel,
        out_shape=(jax.ShapeDtypeStruct((t_tok, d), BF16), jax.ShapeDtypeStruct((t_tok, d), BF16)),
        grid=(t_tok // tl.tm,),
        in_specs=[
            pl.BlockSpec((tl.tm, d), tl.row),
            pl.BlockSpec((None, N_MOD, d), tl.cond),
            pl.BlockSpec(dcs.shape, lambda t: (0, 0)),
        ],
        out_specs=(pl.BlockSpec((tl.tm, d), tl.row), pl.BlockSpec((tl.tm, d), tl.row)),
        compiler_params=_params(("arbitrary",)),
        name="ln_chan_dft",
    )(x, mod, dcs)


def _seq_dft_kernel(cs_ref, ss_ref, xc_ref, xs_ref, o_ref, *, norm):
    f = _dot(cs_ref[...], xc_ref[...]) - _dot(ss_ref[...], xs_ref[...])
    o_ref[...] = (f * norm).astype(BF16)


def _seq_dft(cs, ss, xc, xs, batch, seq, row_offset):
    d = xc.shape[1]
    tm = min(QKV_TILE, seq)
    spt = seq // tm
    off_seq = row_offset // seq
    return pl.pallas_call(
        functools.partial(_seq_dft_kernel, norm=1.0 / math.sqrt(seq * FOURIER_GROUP_DIM)),
        out_shape=jax.ShapeDtypeStruct((batch * seq, d), BF16),
        grid=(batch, spt),
        in_specs=[
            pl.BlockSpec((tm, seq), lambda b, i: (i, 0)),
            pl.BlockSpec((tm, seq), lambda b, i: (i, 0)),
            pl.BlockSpec((seq, d), lambda b, i: (off_seq + b, 0)),
            pl.BlockSpec((seq, d), lambda b, i: (off_seq + b, 0)),
        ],
        out_specs=pl.BlockSpec((tm, d), lambda b, i: (b * spt + i, 0)),
        compiler_params=_params(("arbitrary", "arbitrary")),
        name=f"seq_dft_{seq}",
    )(cs, ss, xc, xs)


def _diff_lambda(lam_ref, lam_init):
    lv = lam_ref[...]
    return (jnp.exp(jnp.sum(lv[0:1] * lv[1:2], axis=-1, keepdims=True))
            - jnp.exp(jnp.sum(lv[2:3] * lv[3:4], axis=-1, keepdims=True)) + lam_init)


def _diff_attn_head(q, k, v_ext, lam, gain, lam_init):
    tq = q.shape[0]
    lane = lax.broadcasted_iota(I32, q.shape, 1)
    zero = jnp.zeros_like(q)
    qq = jnp.concatenate([jnp.where(lane < HEAD_DIM, q, zero), jnp.where(lane >= HEAD_DIM, q, zero)], axis=0)
    parts = []
    for r in range(0, 2 * tq, ATTN_ROW_CHUNK):
        s = lax.dot_general(qq[r:r + ATTN_ROW_CHUNK], k, (((1,), (1,)), ((), ())), preferred_element_type=F32)
        e = jnp.exp2(s - jnp.max(s, axis=-1, keepdims=True)).astype(BF16)
        parts.append(_dot(e, v_ext))
    oe = jnp.concatenate(parts, axis=0)
    o = oe[:, :V_DIM] / oe[:, V_DIM:]
    o = o[:tq] - lam * o[tq:]
    o = o * lax.rsqrt(jnp.mean(o * o, axis=-1, keepdims=True) + LN_EPS)
    return o * gain * (1.0 - lam_init)


def _attn_prompt_kernel(lam_ref, gain_ref, q_ref, k_ref, v_ref, o_ref, *, lam_init):
    lam = _diff_lambda(lam_ref, lam_init)
    gain = gain_ref[...]
    ones = jnp.ones((k_ref.shape[0], V_DIM), BF16)
    for hd in range(N_HEADS):
        cols = slice(hd * V_DIM, (hd + 1) * V_DIM)
        v_ext = jnp.concatenate([v_ref[:, cols], ones], axis=1)
        o_ref[:, cols] = _diff_attn_head(q_ref[:, cols], k_ref[:, cols], v_ext, lam, gain, lam_init).astype(BF16)


def _attn_latent_kernel(lam_ref, gain_ref, q_ref, k_ref, v_ref, kc_ref, vc_ref, o_ref, kall_ref, vext_ref, *,
                        lam_init):
    n_new = k_ref.shape[0]

    @pl.when(pl.program_id(2) == 0)
    def _():
        kall_ref[:n_new, :] = k_ref[...]
        kall_ref[n_new:, :] = kc_ref[...]
        vext_ref[:n_new, :V_DIM] = v_ref[...]
        vext_ref[n_new:, :V_DIM] = vc_ref[...]
        vext_ref[:, V_DIM:] = jnp.ones((vext_ref.shape[0], V_DIM), BF16)

    o = _diff_attn_head(q_ref[...], kall_ref[...], vext_ref[...], _diff_lambda(lam_ref, lam_init), gain_ref[...],
                        lam_init)
    o_ref[...] = o.astype(BF16)


def _attn_prompt(lam_vecs, gain, q, k, v, batch, seq, lam_init):
    d = q.shape[1]
    blk = pl.BlockSpec((seq, d), lambda b: (b, 0))
    return pl.pallas_call(
        functools.partial(_attn_prompt_kernel, lam_init=lam_init),
        out_shape=jax.ShapeDtypeStruct((batch * seq, d), BF16),
        grid=(batch,),
        in_specs=[
            pl.BlockSpec(lam_vecs.shape, lambda b: (0, 0)),
            pl.BlockSpec(gain.shape, lambda b: (0, 0)),
            blk, blk, blk,
        ],
        out_specs=blk,
        compiler_params=_params(("arbitrary",)),
        name="diff_attn_ctx",
    )(lam_vecs, gain, q, k, v)


def _attn_sample(lam_vecs, gain, q, k, v, kc, vc, batch, seq, n_ctx, row_offset, lam_init):
    d = q.shape[1]
    tq = Q_TILE
    qpt = seq // tq
    off_seq = row_offset // seq
    off_tile = row_offset // tq
    qmap = lambda b, h, i: (off_tile + b * qpt + i, h)
    kmap = lambda b, h, i: (off_seq + b, h)
    cmap = lambda b, h, i: (b, h)
    return pl.pallas_call(
        functools.partial(_attn_latent_kernel, lam_init=lam_init),
        out_shape=jax.ShapeDtypeStruct((batch * seq, d), BF16),
        grid=(batch, N_HEADS, qpt),
        in_specs=[
            pl.BlockSpec(lam_vecs.shape, lambda b, h, i: (0, 0)),
            pl.BlockSpec(gain.shape, lambda b, h, i: (0, 0)),
            pl.BlockSpec((tq, V_DIM), qmap),
            pl.BlockSpec((seq, V_DIM), kmap),
            pl.BlockSpec((seq, V_DIM), kmap),
            pl.BlockSpec((n_ctx, V_DIM), cmap),
            pl.BlockSpec((n_ctx, V_DIM), cmap),
        ],
        out_specs=pl.BlockSpec((tq, V_DIM), lambda b, h, i: (b * qpt + i, h)),
        scratch_shapes=[pltpu.VMEM((seq + n_ctx, V_DIM), BF16), pltpu.VMEM((seq + n_ctx, 2 * V_DIM), BF16)],
        compiler_params=_params(("arbitrary", "arbitrary", "arbitrary")),
        name="diff_attn_latent",
    )(lam_vecs, gain, q, k, v, kc, vc)


def _route(lg):
    lane = lax.broadcasted_iota(I32, lg.shape, 1)
    lane_f = lane.astype(F32)
    neg = jnp.float32(-jnp.inf)
    big = jnp.float32(LANES)
    gl = jnp.where(lane < N_EXPERT_GROUPS, lg, neg)
    gmax = jnp.max(gl, axis=-1, keepdims=True)
    g_prob = 1.0 / jnp.sum(jnp.exp(gl - gmax), axis=-1, keepdims=True)
    g_idx = jnp.min(jnp.where(gl == gmax, lane_f, big), axis=-1, keepdims=True)
    lo = N_EXPERT_GROUPS + EXPERTS_PER_GROUP * g_idx
    el = jnp.where((lane_f >= lo) & (lane_f < lo + EXPERTS_PER_GROUP), lg, neg)
    m1 = jnp.max(el, axis=-1, keepdims=True)
    i1 = jnp.min(jnp.where(el == m1, lane_f, big), axis=-1, keepdims=True)
    el2 = jnp.where(lane_f == i1, neg, el)
    m2 = jnp.max(el2, axis=-1, keepdims=True)
    i2 = jnp.min(jnp.where(el2 == m2, lane_f, big), axis=-1, keepdims=True)
    t = jnp.exp(m2 - m1)
    w1 = g_prob / (1.0 + t)
    w2 = g_prob * t / (1.0 + t)
    out = jnp.where(lane == 0, i1 - N_EXPERT_GROUPS, 0.0)
    out = jnp.where(lane == 1, i2 - N_EXPERT_GROUPS, out)
    out = jnp.where(lane == 2, w1, out)
    out = jnp.where(lane == 3, w2, out)
    return out


def _pack_bf16_pairs(h):
    n = h.shape[1] // 2
    bits = lax.bitcast_convert_type(h.astype(BF16).astype(F32), jnp.uint32)
    return (bits[:, :n] >> 16) | bits[:, n:]


def _unpack_bf16_pairs(p):
    lo = lax.bitcast_convert_type(p << 16, F32)
    hi = lax.bitcast_convert_type(p & jnp.uint32(0xFFFF0000), F32)
    return jnp.concatenate([lo, hi], axis=1)


def _mix_out_kernel(*refs, n_prompt_tiles, n_x):
    ap_ref, as_ref, w_ref = refs[:3]
    x_refs = refs[3:3 + n_x]
    mod_ref, gb_ref, wr_ref, x1_ref, hp_ref, route_ref = refs[3 + n_x:]
    is_prompt = pl.program_id(0) < n_prompt_tiles
    for rows in _row_chains(x1_ref.shape[0]):
        a = jnp.where(is_prompt, ap_ref[rows, :], as_ref[rows, :])
        out = _dot(a, w_ref[...])
        x = _read_rows(x_refs, n_prompt_tiles, rows)
        x1 = _deepnorm(x, mod_ref[2:3, :], out) * gb_ref[0:1, :] + gb_ref[1:2, :]
        x1_ref[rows, :] = x1
        h2 = _layernorm(x1) * (1.0 + mod_ref[4:5, :]) + mod_ref[3:4, :]
        hp_ref[rows, :] = _pack_bf16_pairs(h2)
        hi = h2.astype(BF16)
        lo = (h2 - hi.astype(F32)).astype(BF16)
        logits = _dot(hi, wr_ref[0]) + _dot(lo, wr_ref[0]) + _dot(hi, wr_ref[1])
        route_ref[rows, :] = _route(logits)


def _mix_out(a_prompt, a_sample, w, xs, mod, gb, wr, n_prompt, dec_seq, t_tok):
    d = w.shape[1]
    tl = _Tiles(n_prompt, dec_seq, TOKEN_TILE)
    tm = tl.tm
    return pl.pallas_call(
        functools.partial(_mix_out_kernel, n_prompt_tiles=tl.npt, n_x=len(xs)),
        out_shape=(
            jax.ShapeDtypeStruct((t_tok, d), F32),
            jax.ShapeDtypeStruct((t_tok, d // 2), jnp.uint32),
            jax.ShapeDtypeStruct((t_tok, LANES), F32),
        ),
        grid=(t_tok // tm,),
        in_specs=[
            pl.BlockSpec((tm, d), tl.prompt_part),
            pl.BlockSpec((tm, d), tl.latent_part),
            pl.BlockSpec(w.shape, lambda t: (0, 0)),
        ] + tl.source_specs(xs, d) + [
            pl.BlockSpec((None, N_MOD, d), tl.cond),
            pl.BlockSpec(gb.shape, lambda t: (0, 0)),
            pl.BlockSpec(wr.shape, lambda t: (0, 0, 0)),
        ],
        out_specs=(
            pl.BlockSpec((tm, d), tl.row),
            pl.BlockSpec((tm, d // 2), tl.row),
            pl.BlockSpec((tm, LANES), tl.row),
        ),
        compiler_params=_params(("arbitrary",)),
        name="mix_out_postnorm_router",
    )(a_prompt, a_sample, w, *xs, mod, gb, wr)


def _lane_prefix_sum(x, lane):
    sh = 1
    while sh < LANES:
        x = x + jnp.where(lane >= sh, pltpu.roll(x, sh, 1), 0.0)
        sh *= 2
    return x


def _lane_suffix_min_exclusive(x, lane):
    big = float(LANES)
    y = jnp.where(lane + 1 < LANES, pltpu.roll(x, LANES - 1, 1), big)
    sh = 1
    while sh < LANES:
        y = jnp.minimum(y, jnp.where(lane + sh < LANES, pltpu.roll(y, LANES - sh, 1), big))
        sh *= 2
    return y


def _work_items(counts, starts, ends, lane8, n_moe_tiles):
    shift = int(math.log2(MOE_TILE))
    first_tile = (starts.astype(I32) >> shift).astype(F32)
    last_tile = ((ends.astype(I32) - 1) >> shift).astype(F32)
    n_it = jnp.where(counts > 0.0, last_tile - first_tile + 1.0, 0.0)
    it_end = _lane_prefix_sum(n_it, lane8)
    it_start = it_end - n_it
    total = it_end[0:1, LANES - 1:LANES]
    sub = lax.broadcasted_iota(I32, (LANES, LANES), 0)
    rows = lambda x: jnp.broadcast_to(x[0:1, :], (LANES, LANES))
    used = counts > 0.0
    weight_slot = ((_lane_prefix_sum(jnp.where(used, 1.0, 0.0), lane8) - 1.0).astype(I32) & 1).astype(F32)
    next_used = _lane_suffix_min_exclusive(jnp.where(used, lane8.astype(F32), float(LANES)), lane8)
    per_expert = (starts, ends, first_tile, it_start, it_end, weight_slot, next_used)
    stacked = jnp.zeros((LANES, LANES), F32)
    for j, vec in enumerate(per_expert):
        stacked = jnp.where(sub == j, rows(vec), stacked)
    cols = stacked.T
    col = lambda j: cols[:, j:j + 1]
    sub_f = sub.astype(F32)
    w = lax.broadcasted_iota(I32, (LANES, LANES), 1).astype(F32)
    ex = jnp.sum(jnp.where((sub < N_EXPERTS) & (col(4) <= w), 1.0, 0.0), axis=0, keepdims=True)
    ex = jnp.minimum(ex, N_EXPERTS - 1.0)
    w1 = w[0:1, :]
    valid = w1 < total
    ex = jnp.where(valid, ex, jnp.max(jnp.where(valid, ex, 0.0), axis=-1, keepdims=True))
    onehot = sub_f == ex
    pick = lambda j: jnp.sum(jnp.where(onehot, col(j), 0.0), axis=0, keepdims=True)
    tile = jnp.where(valid, pick(2) + (w1 - pick(3)), n_moe_tiles - 1.0)
    lo = jnp.where(valid, jnp.maximum(pick(0), tile * MOE_TILE), 0.0)
    hi = jnp.where(valid, jnp.minimum(pick(1), (tile + 1.0) * MOE_TILE), 0.0)
    b8 = lambda x: jnp.broadcast_to(x, (SUBLANES, LANES))
    ex8, tile8 = b8(ex), b8(tile)
    first = jnp.where((lane8 == 0) | (tile8 != pltpu.roll(tile8, 1, 1)), 1.0, 0.0)
    newexp = jnp.where((lane8 == 0) | (ex8 != pltpu.roll(ex8, 1, 1)), 1.0, 0.0)
    sub8 = lax.broadcasted_iota(I32, (SUBLANES, LANES), 0)
    table = jnp.zeros((SUBLANES, LANES), F32)
    for j, vec in enumerate((ex8, tile8, b8(lo), b8(hi), first, newexp, b8(pick(5)), b8(pick(6)))):
        table = jnp.where(sub8 == j, vec, table)
    return table


def _plan_kernel(route_ref, pos_ref, items_ref, tri_ref, carry_ref, tot_ref, *, n_moe_tiles):
    p = pl.program_id(0)
    t = pl.program_id(1)
    tm = route_ref.shape[0]
    r = route_ref[...]
    lane = lax.broadcasted_iota(I32, (tm, LANES), 1)
    lane_f = lane.astype(F32)
    e0 = r[:, 0:1]
    e1 = r[:, 1:2] + N_EXPERTS
    m = jnp.where((lane_f == e0) | (lane_f == e1), 1.0, 0.0)
    colsum = jnp.sum(m, axis=0, keepdims=True)

    @pl.when((p == 0) & (t == 0))
    def _():
        carry_ref[...] = jnp.zeros_like(carry_ref)
        row = lax.broadcasted_iota(I32, (tm, tm), 0)
        col = lax.broadcasted_iota(I32, (tm, tm), 1)
        tri_ref[...] = jnp.where(row > col, 1.0, 0.0).astype(BF16)

    @pl.when(p == 0)
    def _():
        carry_ref[...] += colsum

    @pl.when((p == 0) & (t == pl.num_programs(1) - 1))
    def _():
        tot_ref[...] = carry_ref[...]
        carry_ref[...] = jnp.zeros_like(carry_ref)

    @pl.when(p == 1)
    def _():
        lane8 = lax.broadcasted_iota(I32, (SUBLANES, LANES), 1)
        tot = tot_ref[...]
        is_first = lane8 < N_EXPERTS
        tot0 = jnp.where(is_first, tot, 0.0)
        counts = jnp.where(is_first, tot + pltpu.roll(tot, LANES - N_EXPERTS, 1), 0.0)
        ends = _lane_prefix_sum(counts, lane8)
        starts = ends - counts
        base = jnp.where(is_first, starts, pltpu.roll(starts + tot0, N_EXPERTS, 1))
        before = _dot(tri_ref[...], m.astype(BF16)) + carry_ref[0:1, :]
        carry_ref[...] += colsum
        rows = before + base[0:1, :]
        pos0 = jnp.sum(jnp.where(lane_f == e0, rows, 0.0), axis=-1, keepdims=True)
        pos1 = jnp.sum(jnp.where(lane_f == e1, rows, 0.0), axis=-1, keepdims=True)
        both = jnp.where(lane == 0, pos0, jnp.where(lane == 1, pos1, 0.0))
        pos_ref[...] = both.T[0:SUBLANES, :].astype(I32)

        @pl.when(t == 0)
        def _():
            items_ref[...] = _work_items(counts, starts, ends, lane8, n_moe_tiles).astype(I32)


def _moe_plan(route):
    t_tok = route.shape[0]
    tm = PLAN_TILE
    n_moe_tiles = 2 * t_tok // MOE_TILE
    assert n_moe_tiles + N_EXPERTS - 1 <= LANES and 2 * N_EXPERTS <= LANES and MOE_TILE & (MOE_TILE - 1) == 0
    return pl.pallas_call(
        functools.partial(_plan_kernel, n_moe_tiles=n_moe_tiles),
        out_shape=(jax.ShapeDtypeStruct((SUBLANES, t_tok), I32), jax.ShapeDtypeStruct((SUBLANES, LANES), I32)),
        grid=(2, t_tok // tm),
        in_specs=[pl.BlockSpec((tm, LANES), lambda p, t: (t, 0))],
        out_specs=(pl.BlockSpec((SUBLANES, tm), lambda p, t: (0, t * p)),
                   pl.BlockSpec((SUBLANES, LANES), lambda p, t: (0, 0))),
        scratch_shapes=[pltpu.VMEM((tm, tm), BF16), pltpu.VMEM((SUBLANES, LANES), F32),
                        pltpu.VMEM((SUBLANES, LANES), F32)],
        compiler_params=_params(("arbitrary", "arbitrary")),
        name="moe_positions",
    )(route)


def _sc_workers():
    info = plsc.get_sparse_core_info()
    return info.num_cores, info.num_cores * info.num_subcores


def _sc_pipeline(n_chunks, loads, stores):
    def start(copies):
        for cp in copies:
            cp.start()

    def wait(copies):
        for cp in copies:
            cp.wait()

    start(loads(0, 0))
    for j in range(n_chunks):
        b = j % 2
        if j + 1 < n_chunks:
            if j >= 1:
                wait(stores(j - 1, 1 - b))
            start(loads(j + 1, 1 - b))
        wait(loads(j, b))
        start(stores(j, b))
    if n_chunks >= 2:
        wait(stores(n_chunks - 2, n_chunks % 2))
    wait(stores(n_chunks - 1, (n_chunks - 1) % 2))


def _sc_chunking(n_rows, d):
    n_cores, n_workers = _sc_workers()
    per_worker = n_rows // n_workers
    k = SC_BUFFER_BYTES // (d * 4)
    n_chunks = per_worker // k
    assert n_chunks * k * n_workers == n_rows and k <= SC_MAX_INDEX_CHUNK
    return n_cores, n_workers, per_worker, k, n_chunks


def _sc_gather_rows(table, idx):
    n, d = idx.shape[0], table.shape[1]
    n_cores, n_workers, per_worker, k, n_chunks = _sc_chunking(n, d)

    def body(table_hbm, idx_hbm, out_hbm, idx_v, rows_v, gsem, osem):
        wid = lax.axis_index("s") * n_cores + lax.axis_index("c")
        base = wid * per_worker
        pltpu.sync_copy(idx_hbm.at[wid], idx_v)
        _sc_pipeline(
            n_chunks,
            lambda j, b: [pltpu.make_async_copy(table_hbm.at[idx_v.at[j]], rows_v.at[b], gsem.at[b])],
            lambda j, b: [pltpu.make_async_copy(rows_v.at[b], out_hbm.at[pl.ds(base + j * k, k)], osem.at[b])])

    return pl.kernel(
        body,
        out_type=jax.ShapeDtypeStruct((n, d), table.dtype),
        mesh=plsc.VectorSubcoreMesh(core_axis_name="c", subcore_axis_name="s"),
        scratch_types=[pltpu.VMEM((n_chunks, k), I32), pltpu.VMEM((2, k, d), table.dtype),
                       pltpu.SemaphoreType.DMA((2,)), pltpu.SemaphoreType.DMA((2,))],
        name="sc_row_gather",
    )(table, idx.reshape(n_workers, n_chunks, k))


def _sc_scatter_rows(src, idx):
    n_lists, n_src = idx.shape
    d = src.shape[1]
    n_cores, n_workers, per_worker, k, n_chunks = _sc_chunking(n_src, d)

    def body(src_hbm, idx_hbm, out_hbm, idx_v, rows_v, gsem, osem):
        wid = lax.axis_index("s") * n_cores + lax.axis_index("c")
        base = wid * per_worker
        pltpu.sync_copy(idx_hbm.at[wid], idx_v)
        _sc_pipeline(
            n_chunks,
            lambda j, b: [pltpu.make_async_copy(src_hbm.at[pl.ds(base + j * k, k)], rows_v.at[b], gsem.at[b])],
            lambda j, b: [pltpu.make_async_copy(rows_v.at[b], out_hbm.at[idx_v.at[c, j]], osem.at[b, c])
                          for c in range(n_lists)])

    return pl.kernel(
        body,
        out_type=jax.ShapeDtypeStruct((n_lists * n_src, d), src.dtype),
        mesh=plsc.VectorSubcoreMesh(core_axis_name="c", subcore_axis_name="s"),
        scratch_types=[pltpu.VMEM((n_lists, n_chunks, k), I32), pltpu.VMEM((2, k, d), src.dtype),
                       pltpu.SemaphoreType.DMA((2,)), pltpu.SemaphoreType.DMA((2, n_lists))],
        name="sc_row_scatter",
    )(src, idx.reshape(n_lists, n_workers, n_chunks, k).transpose(1, 0, 2, 3))


IT_EXPERT, IT_TILE, IT_LO, IT_HI, IT_FIRST, IT_NEWEXP, IT_SLOT, IT_NEXT = range(8)


def _moe_kernel(it_ref, xs_ref, wg_hbm, wu_hbm, wd_hbm, o_ref, wg_f, wu_f, wd_f, wgu_s, wd_s, sem, *, layer):
    w = pl.program_id(0)
    f = wg_f.shape[2]

    def weight_copies(expert, slot):
        return [pltpu.make_async_copy(src.at[layer, expert], dst.at[slot], sem.at[slot, j])
                for j, (src, dst) in enumerate(((wg_hbm, wg_f), (wu_hbm, wu_f), (wd_hbm, wd_f)))]

    @pl.when(it_ref[IT_NEWEXP, w] == 1)
    def _():
        expert = it_ref[IT_EXPERT, w]
        slot = it_ref[IT_SLOT, w]
        nxt = it_ref[IT_NEXT, w]

        @pl.when(w == 0)
        def _():
            for cp in weight_copies(expert, slot):
                cp.start()

        for cp in weight_copies(expert, slot):
            cp.wait()

        @pl.when(nxt < N_EXPERTS)
        def _():
            for cp in weight_copies(nxt, 1 - slot):
                cp.start()

        wgu_s[:, :f] = wg_f[slot].astype(BF16)
        wgu_s[:, f:] = wu_f[slot].astype(BF16)
        wd_s[...] = wd_f[slot].astype(BF16)

    tm = xs_ref.shape[0]
    x = _unpack_bf16_pairs(xs_ref[...]).astype(BF16)
    au = _dot(x, wgu_s[...])
    act = (_silu(au[:, :f]) * au[:, f:]).astype(BF16)
    y = _pack_bf16_pairs(_dot(act, wd_s[...]))
    row = it_ref[IT_TILE, w] * tm + lax.broadcasted_iota(I32, (tm, 1), 0)
    mine = (row >= it_ref[IT_LO, w]) & (row < it_ref[IT_HI, w])

    @pl.when(it_ref[IT_FIRST, w] == 1)
    def _():
        o_ref[...] = jnp.where(mine, y, jnp.zeros_like(y))

    @pl.when(it_ref[IT_FIRST, w] == 0)
    def _():
        o_ref[...] = jnp.where(mine, y, o_ref[...])


def _moe_experts(items, xs, w_gate, w_up, w_down, layer):
    n_rows, dp = xs.shape
    _, _, d, f = w_gate.shape
    tm = MOE_TILE
    rmap = lambda w, it: (it[IT_TILE, w], 0)
    hbm = pl.BlockSpec(memory_space=pl.ANY)
    return pl.pallas_call(
        functools.partial(_moe_kernel, layer=layer),
        out_shape=jax.ShapeDtypeStruct((n_rows, d // 2), jnp.uint32),
        grid_spec=pltpu.PrefetchScalarGridSpec(
            num_scalar_prefetch=1,
            grid=(n_rows // tm + N_EXPERTS - 1,),
            in_specs=[pl.BlockSpec((tm, dp), rmap), hbm, hbm, hbm],
            out_specs=pl.BlockSpec((tm, d // 2), rmap),
            scratch_shapes=[
                pltpu.VMEM((2, d, f), F32), pltpu.VMEM((2, d, f), F32), pltpu.VMEM((2, f, d), F32),
                pltpu.VMEM((d, 2 * f), BF16), pltpu.VMEM((f, d), BF16),
                pltpu.SemaphoreType.DMA((2, 3)),
            ],
        ),
        compiler_params=_params(("arbitrary",)),
        name="moe_grouped_mlp",
    )(items, xs, w_gate, w_up, w_down)


def _moe_combine_kernel(x_ref, y0_ref, y1_ref, route_ref, mod_ref, gb_ref, *o_refs, n_prompt_tiles):
    r = route_ref[...]
    moe = r[:, 2:3] * _unpack_bf16_pairs(y0_ref[...]) + r[:, 3:4] * _unpack_bf16_pairs(y1_ref[...])
    x2 = _deepnorm(x_ref[...], mod_ref[5:6, :], moe) * gb_ref[0:1, :] + gb_ref[1:2, :]
    if len(o_refs) == 1:
        o_refs[0][...] = x2
    else:
        @pl.when(pl.program_id(0) < n_prompt_tiles)
        def _():
            o_refs[0][...] = x2

        @pl.when(pl.program_id(0) >= n_prompt_tiles)
        def _():
            o_refs[1][...] = x2


def _moe_combine(x1, ys, route, mod, gb, n_prompt, dec_seq, split):
    t_tok, d = x1.shape
    tl = _Tiles(n_prompt, dec_seq, TOKEN_TILE)
    tm = tl.tm
    nt = t_tok // tm
    if split:
        out_shape = (jax.ShapeDtypeStruct((n_prompt, d), F32), jax.ShapeDtypeStruct((t_tok - n_prompt, d), F32))
        out_specs = (pl.BlockSpec((tm, d), tl.prompt_part), pl.BlockSpec((tm, d), tl.latent_part))
    else:
        out_shape = jax.ShapeDtypeStruct((t_tok, d), F32)
        out_specs = pl.BlockSpec((tm, d), tl.row)
    return pl.pallas_call(
        functools.partial(_moe_combine_kernel, n_prompt_tiles=tl.npt),
        out_shape=out_shape,
        grid=(nt,),
        in_specs=[
            pl.BlockSpec((tm, d), tl.row),
            pl.BlockSpec((tm, ys.shape[1]), tl.row),
            pl.BlockSpec((tm, ys.shape[1]), lambda t: (t + nt, 0)),
            pl.BlockSpec((tm, LANES), tl.row),
            pl.BlockSpec((None, N_MOD, d), tl.cond),
            pl.BlockSpec(gb.shape, lambda t: (0, 0)),
        ],
        out_specs=out_specs,
        compiler_params=_params(("arbitrary",)),
        name="moe_combine_postnorm",
    )(x1, ys, ys, route, mod, gb)


def _rope_tables(n_lat):
    nf = HEAD_DIM // 4
    s = np.arange(n_lat)
    lane = np.arange(LANES)
    inv = ROPE_BASE ** (-(lane % nf).astype(np.float64) / nf)
    use_col = (lane % HEAD_DIM) >= HEAD_DIM // 2
    p = np.where(use_col[None, :], (s % GRID_W)[:, None], (s // GRID_W)[:, None]).astype(np.float64)
    ang = p * inv[None, :]
    sign = np.where((lane % (2 * nf)) < nf, -1.0, 1.0)
    return jnp.asarray(np.cos(ang), F32), jnp.asarray(np.sin(ang) * sign[None, :], F32)


def _dft_tables(n):
    k = np.arange(n)
    ang = 2.0 * np.pi * ((k[:, None] * k[None, :]) % n).astype(np.float64) / n
    return np.cos(ang), np.sin(ang)


def kernel(x_prompt, x_sample, cache_k, cache_v, c, c_ctx, w_ada, b_ada, ln_gain, ln_bias, w_qkv, w_attn_out,
           lambda_q1, lambda_k1, lambda_q2, lambda_k2, subln_gain, w_fourier_out, w_router_group,
           w_router_expert, w_expert_gate, w_expert_up, w_expert_down):
    bp, sp, d = x_prompt.shape
    bs, n_lat, _ = x_sample.shape
    n_ctx = cache_k.shape[2]
    n_prompt = bp * sp
    t_tok = n_prompt + bs * n_lat
    assert d == D_MODEL and n_prompt % n_lat == 0 and n_lat % TOKEN_TILE == 0 and sp == QKV_TILE

    cond = jnp.concatenate([c_ctx[None, :], c, jnp.zeros((SUBLANES - 1 - bs, d), F32)], axis=0)
    mods = _ada_all(cond, w_ada, b_ada).reshape(DEPTH, SUBLANES, N_MOD, d)

    cos, sin = _rope_tables(n_lat)
    cc, sc = _dft_tables(FOURIER_GROUP_DIM)
    dcs = jnp.asarray(np.concatenate([cc, sc], axis=1), BF16)
    seq_tabs = {s: tuple(jnp.asarray(m, BF16) for m in _dft_tables(s)) for s in (sp, n_lat)}
    attn_layers = [i for i in range(DEPTH) if i % 2 == 0]

    xs = (x_prompt.reshape(n_prompt, d), x_sample.reshape(bs * n_lat, d))
    prev_kv = []
    new_k = new_v = None
    for i in range(DEPTH):
        mod = mods[i]
        if i % 2 == 0:
            a = i // 2
            lam_init = 0.8 - 0.6 * math.exp(-0.3 * i)
            last_attn = i == attn_layers[-1]
            q, k, v, kf, vf = _qkv(xs, mod, w_qkv[a].astype(BF16), cos, sin, prev_kv, last_attn, n_prompt, n_lat,
                                   t_tok)
            if last_attn:
                new_k, new_v = kf, vf
            else:
                prev_kv.append((kf, vf))
            lam_vecs = jnp.stack([lambda_q1[a], lambda_k1[a], lambda_q2[a], lambda_k2[a]], axis=0)
            gain = subln_gain[a][None, :]
            kc = cache_k[:, a].reshape(bs * n_ctx, d).astype(BF16)
            vc = cache_v[:, a].reshape(bs * n_ctx, d).astype(BF16)
            mixed_p = _attn_prompt(lam_vecs, gain, q, k, v, bp, sp, lam_init)
            mixed_s = _attn_sample(lam_vecs, gain, q, k, v, kc, vc, bs, n_lat, n_ctx, n_prompt, lam_init)
            w_mix = w_attn_out[a]
        else:
            xc, xsn = _chan_dft(xs[0], mod, dcs, n_prompt, n_lat)
            mixed_p = _seq_dft(*seq_tabs[sp], xc, xsn, bp, sp, 0)
            mixed_s = _seq_dft(*seq_tabs[n_lat], xc, xsn, bs, n_lat, n_prompt)
            w_mix = w_fourier_out[i // 2]
        gb0 = jnp.stack([ln_gain[i, 0], ln_bias[i, 0]], axis=0)
        gb1 = jnp.stack([ln_gain[i, 1], ln_bias[i, 1]], axis=0)
        wr = jnp.concatenate([w_router_group[i], w_router_expert[i],
                              jnp.zeros((d, LANES - N_EXPERT_GROUPS - N_EXPERTS), F32)], axis=1)
        wr_hi = wr.astype(BF16)
        wr_lo = (wr - wr_hi.astype(F32)).astype(BF16)
        x1, hp, route = _mix_out(mixed_p, mixed_s, w_mix.astype(BF16), xs, mod, gb0,
                                 jnp.stack([wr_hi, wr_lo], axis=0), n_prompt, n_lat, t_tok)
        pos8, items = _moe_plan(route)
        pos = pos8[0:2]
        xs_sorted = _sc_scatter_rows(hp, pos)
        ys = _moe_experts(items, xs_sorted, w_expert_gate, w_expert_up, w_expert_down, i)
        yg = _sc_gather_rows(ys, pos.reshape(-1))
        last = i == DEPTH - 1
        out = _moe_combine(x1, yg, route, mod, gb1, n_prompt, n_lat, split=last)
        xs = out if last else (out,)

    y_prompt = xs[0].reshape(bp, sp, d)
    y_sample = xs[1].reshape(bs, n_lat, d)
    return (y_prompt, y_sample, new_k.reshape(bp, len(attn_layers), sp, N_HEADS, 2 * HEAD_DIM),
            new_v.reshape(bp, len(attn_layers), sp, N_HEADS, V_DIM))
```

```python
import functools
import math

import numpy as np
import jax
import jax.numpy as jnp
from jax import lax
from jax.experimental import pallas as pl
from jax.experimental.pallas import tpu as pltpu
from jax.experimental.pallas import tpu_sc as plsc

F32 = jnp.float32
BF16 = jnp.bfloat16
I32 = jnp.int32

D_MODEL = 1024
DEPTH = 4
GRID_W = 64
N_HEADS = 8
HEAD_DIM = 64
V_DIM = 2 * HEAD_DIM
ROPE_BASE = 10000.0
N_FOURIER_GROUPS = 8
FOURIER_GROUP_DIM = D_MODEL // N_FOURIER_GROUPS
N_EXPERT_GROUPS = 4
EXPERTS_PER_GROUP = 8
N_EXPERTS = N_EXPERT_GROUPS * EXPERTS_PER_GROUP
D_EXPERT = 256
N_MOD = 6
LN_EPS = 1e-5
DEEPNORM_ALPHA = (2.0 * DEPTH) ** 0.25
Q_SCALE = math.log2(math.e) * HEAD_DIM ** -0.5

LANES = 128
SUBLANES = 8
TOKEN_TILE = 512
CHAIN_ROWS = 256
QKV_TILE = 256
Q_TILE = 512
ATTN_ROW_CHUNK = 128
MOE_TILE = 512
PLAN_TILE = 1024
SC_BUFFER_BYTES = 128 * 1024
SC_MAX_INDEX_CHUNK = 128
VMEM_LIMIT = 48 * 1024 * 1024


def _params(semantics):
    return pltpu.CompilerParams(dimension_semantics=semantics, vmem_limit_bytes=VMEM_LIMIT)


def _layernorm(x, eps=LN_EPS):
    mu = jnp.mean(x, axis=-1, keepdims=True)
    xc = x - mu
    var = jnp.mean(xc * xc, axis=-1, keepdims=True)
    return xc * lax.rsqrt(var + eps)


def _deepnorm(x, branch_gate, branch):
    return _layernorm(x + (branch_gate * (1.0 / DEEPNORM_ALPHA)) * branch, LN_EPS / DEEPNORM_ALPHA ** 2)


def _silu(a):
    return a / (1.0 + jnp.exp(-a))


def _dot(a, b):
    return jnp.dot(a, b, preferred_element_type=F32)


class _Tiles:
    def __init__(self, n_prompt, dec_seq, tm):
        self.tm = tm
        self.npt = n_prompt // tm
        self.tps = dec_seq // tm

    def row(self, t):
        return (t, 0)

    def cond(self, t):
        return (jnp.where(t < self.npt, 0, (t - self.npt) // self.tps + 1), 0, 0)

    def latent_pos(self, t):
        return (jnp.maximum(t - self.npt, 0) % self.tps, 0)

    def prompt_part(self, t):
        return (jnp.minimum(t, self.npt - 1), 0)

    def latent_part(self, t):
        return (jnp.maximum(t - self.npt, 0), 0)

    def source_specs(self, arrays, d):
        if len(arrays) == 1:
            return [pl.BlockSpec((self.tm, d), self.row)]
        return [pl.BlockSpec((self.tm, d), self.prompt_part), pl.BlockSpec((self.tm, d), self.latent_part)]


def _read_rows(refs, n_prompt_tiles, rows=slice(None)):
    if len(refs) == 1:
        return refs[0][rows, :]
    return jnp.where(pl.program_id(0) < n_prompt_tiles, refs[0][rows, :], refs[1][rows, :])


def _ada_kernel(cond_ref, w_ref, b_ref, o_ref):
    a = _silu(cond_ref[...])
    o_ref[...] = _dot(a.astype(BF16), w_ref[...].astype(BF16)) + b_ref[...]


def _ada_all(cond, w_ada, b_ada):
    depth, d, n = w_ada.shape
    tn = n // 4
    return pl.pallas_call(
        _ada_kernel,
        out_shape=jax.ShapeDtypeStruct((depth, cond.shape[0], n), F32),
        grid=(depth, n // tn),
        in_specs=[
            pl.BlockSpec(cond.shape, lambda l, j: (0, 0)),
            pl.BlockSpec((None, d, tn), lambda l, j: (l, 0, j)),
            pl.BlockSpec((None, 1, tn), lambda l, j: (l, 0, j)),
        ],
        out_specs=pl.BlockSpec((None, cond.shape[0], tn), lambda l, j: (l, 0, j)),
        compiler_params=_params(("arbitrary", "arbitrary")),
        name="ada",
    )(cond, w_ada, b_ada.reshape(depth, 1, n))


def _rope(x, cos, sin_signed, first_half):
    outs = []
    for c in range(x.shape[1] // LANES):
        xc = x[:, c * LANES:(c + 1) * LANES]
        partner = jnp.where(first_half, pltpu.roll(xc, LANES - 16, 1), pltpu.roll(xc, 16, 1))
        outs.append(xc * cos + partner * sin_signed)
    return jnp.concatenate(outs, axis=1)


def _store_heads(cache_ref, slot, rows):
    for hd in range(N_HEADS):
        cache_ref[slot, :, hd, :] = rows[:, hd * V_DIM:(hd + 1) * V_DIM]


def _qkv_kernel(*refs, n_prompt_tiles, n_x, n_prev):
    x_refs, (mod_ref, w_ref, cos_ref, sin_ref) = refs[:n_x], refs[n_x:n_x + 4]
    prev = refs[n_x + 4:n_x + 4 + 2 * max(n_prev, 0)]
    q_ref, k_ref, v_ref, ko_ref, vo_ref = refs[n_x + 4 + 2 * max(n_prev, 0):]
    t = pl.program_id(0)
    h = (_layernorm(_read_rows(x_refs, n_prompt_tiles)) * (1.0 + mod_ref[1:2, :]) + mod_ref[0:1, :]).astype(BF16)
    d = w_ref.shape[0]
    project = lambda j: _dot(h, w_ref[:, j * d:(j + 1) * d])

    @pl.when(t < n_prompt_tiles)
    def _():
        q_ref[...] = (project(0) * Q_SCALE).astype(BF16)
        k = project(1)
        k_ref[...] = k.astype(BF16)
        v = project(2)
        v_ref[...] = v.astype(BF16)
        if n_prev < 0:
            ko_ref[...] = k
            vo_ref[...] = v
        else:
            for a in range(n_prev):
                _store_heads(ko_ref, a, prev[2 * a][...])
                _store_heads(vo_ref, a, prev[2 * a + 1][...])
            _store_heads(ko_ref, n_prev, k)
            _store_heads(vo_ref, n_prev, v)

    @pl.when(t >= n_prompt_tiles)
    def _():
        lane = lax.broadcasted_iota(I32, (h.shape[0], LANES), 1)
        first_half = (lane % 32) < 16
        cos = cos_ref[...]
        sin = sin_ref[...]
        q_ref[...] = _rope(project(0) * Q_SCALE, cos, sin, first_half).astype(BF16)
        k_ref[...] = _rope(project(1), cos, sin, first_half).astype(BF16)
        v_ref[...] = project(2).astype(BF16)


def _qkv(xs, mod, w_qkv, cos, sin, prev_kv, finish_cache, n_prompt, dec_seq, t_tok):
    d = w_qkv.shape[0]
    tl = _Tiles(n_prompt, dec_seq, QKV_TILE)
    tm = tl.tm
    n_prev = len(prev_kv) if finish_cache else -1
    if finish_cache:
        n_slots = n_prev + 1
        kv_shape = jax.ShapeDtypeStruct((n_prompt // tm, n_slots, tm, N_HEADS, V_DIM), F32)
        kv_spec = pl.BlockSpec((None, n_slots, tm, N_HEADS, V_DIM), lambda t: (tl.prompt_part(t)[0], 0, 0, 0, 0))
    else:
        kv_shape = jax.ShapeDtypeStruct((n_prompt, d), F32)
        kv_spec = pl.BlockSpec((tm, d), tl.prompt_part)
    prev_flat = [a for kv in prev_kv for a in kv] if finish_cache else []
    return pl.pallas_call(
        functools.partial(_qkv_kernel, n_prompt_tiles=tl.npt, n_x=len(xs), n_prev=n_prev),
        out_shape=(
            jax.ShapeDtypeStruct((t_tok, d), BF16),
            jax.ShapeDtypeStruct((t_tok, d), BF16),
            jax.ShapeDtypeStruct((t_tok, d), BF16),
            kv_shape, kv_shape,
        ),
        grid=(t_tok // tm,),
        in_specs=tl.source_specs(xs, d) + [
            pl.BlockSpec((None, N_MOD, d), tl.cond),
            pl.BlockSpec(w_qkv.shape, lambda t: (0, 0)),
            pl.BlockSpec((tm, LANES), tl.latent_pos),
            pl.BlockSpec((tm, LANES), tl.latent_pos),
        ] + [pl.BlockSpec((tm, d), tl.prompt_part) for _ in prev_flat],
        out_specs=(
            pl.BlockSpec((tm, d), tl.row),
            pl.BlockSpec((tm, d), tl.row),
            pl.BlockSpec((tm, d), tl.row),
            kv_spec, kv_spec,
        ),
        compiler_params=_params(("arbitrary",)),
        name="ln_qkv_rope",
    )(*xs, mod, w_qkv, cos, sin, *prev_flat)


def _row_chains(n_rows):
    return [slice(r, r + CHAIN_ROWS) for r in range(0, n_rows, CHAIN_ROWS)]


def _chan_dft_kernel(x_ref, mod_ref, dcs_ref, xc_ref, xs_ref):
    g = FOURIER_GROUP_DIM
    for rows in _row_chains(x_ref.shape[0]):
        h = (_layernorm(x_ref[rows, :]) * (1.0 + mod_ref[1:2, :]) + mod_ref[0:1, :]).astype(BF16)
        for i in range(N_FOURIER_GROUPS):
            r = _dot(h[:, i * g:(i + 1) * g], dcs_ref[...])
            xc_ref[rows, i * g:(i + 1) * g] = r[:, :g].astype(BF16)
            xs_ref[rows, i * g:(i + 1) * g] = r[:, g:].astype(BF16)


def _chan_dft(x, mod, dcs, n_prompt, dec_seq):
    t_tok, d = x.shape
    tl = _Tiles(n_prompt, dec_seq, TOKEN_TILE)
    return pl.pallas_call(
        _chan_dft_kernel,
        out_shape=(jax.ShapeDtypeStruct((t_tok, d), BF16), jax.ShapeDtypeStruct((t_tok, d), BF16)),
        grid=(t_tok // tl.tm,),
        in_specs=[
            pl.BlockSpec((tl.tm, d), tl.row),
            pl.BlockSpec((None, N_MOD, d), tl.cond),
            pl.BlockSpec(dcs.shape, lambda t: (0, 0)),
        ],
        out_specs=(pl.BlockSpec((tl.tm, d), tl.row), pl.BlockSpec((tl.tm, d), tl.row)),
        compiler_params=_params(("arbitrary",)),
        name="ln_chan_dft",
    )(x, mod, dcs)


def _seq_dft_kernel(cs_ref, ss_ref, xc_ref, xs_ref, o_ref, *, norm):
    f = _dot(cs_ref[...], xc_ref[...]) - _dot(ss_ref[...], xs_ref[...])
    o_ref[...] = (f * norm).astype(BF16)


def _seq_dft(cs, ss, xc, xs, batch, seq, row_offset):
    d = xc.shape[1]
    tm = min(QKV_TILE, seq)
    spt = seq // tm
    off_seq = row_offset // seq
    return pl.pallas_call(
        functools.partial(_seq_dft_kernel, norm=1.0 / math.sqrt(seq * FOURIER_GROUP_DIM)),
        out_shape=jax.ShapeDtypeStruct((batch * seq, d), BF16),
        grid=(batch, spt),
        in_specs=[
            pl.BlockSpec((tm, seq), lambda b, i: (i, 0)),
            pl.BlockSpec((tm, seq), lambda b, i: (i, 0)),
            pl.BlockSpec((seq, d), lambda b, i: (off_seq + b, 0)),
            pl.BlockSpec((seq, d), lambda b, i: (off_seq + b, 0)),
        ],
        out_specs=pl.BlockSpec((tm, d), lambda b, i: (b * spt + i, 0)),
        compiler_params=_params(("arbitrary", "arbitrary")),
        name=f"seq_dft_{seq}",
    )(cs, ss, xc, xs)


def _diff_lambda(lam_ref, lam_init):
    lv = lam_ref[...]
    return (jnp.exp(jnp.sum(lv[0:1] * lv[1:2], axis=-1, keepdims=True))
            - jnp.exp(jnp.sum(lv[2:3] * lv[3:4], axis=-1, keepdims=True)) + lam_init)


def _diff_attn_head(q, k, v_ext, lam, gain, lam_init):
    tq = q.shape[0]
    lane = lax.broadcasted_iota(I32, q.shape, 1)
    zero = jnp.zeros_like(q)
    qq = jnp.concatenate([jnp.where(lane < HEAD_DIM, q, zero), jnp.where(lane >= HEAD_DIM, q, zero)], axis=0)
    parts = []
    for r in range(0, 2 * tq, ATTN_ROW_CHUNK):
        s = lax.dot_general(qq[r:r + ATTN_ROW_CHUNK], k, (((1,), (1,)), ((), ())), preferred_element_type=F32)
        e = jnp.exp2(s - jnp.max(s, axis=-1, keepdims=True)).astype(BF16)
        parts.append(_dot(e, v_ext))
    oe = jnp.concatenate(parts, axis=0)
    o = oe[:, :V_DIM] / oe[:, V_DIM:]
    o = o[:tq] - lam * o[tq:]
    o = o * lax.rsqrt(jnp.mean(o * o, axis=-1, keepdims=True) + LN_EPS)
    return o * gain * (1.0 - lam_init)


def _attn_prompt_kernel(lam_ref, gain_ref, q_ref, k_ref, v_ref, o_ref, *, lam_init):
    lam = _diff_lambda(lam_ref, lam_init)
    gain = gain_ref[...]
    ones = jnp.ones((k_ref.shape[0], V_DIM), BF16)
    for hd in range(N_HEADS):
        cols = slice(hd * V_DIM, (hd + 1) * V_DIM)
        v_ext = jnp.concatenate([v_ref[:, cols], ones], axis=1)
        o_ref[:, cols] = _diff_attn_head(q_ref[:, cols], k_ref[:, cols], v_ext, lam, gain, lam_init).astype(BF16)


def _attn_latent_kernel(lam_ref, gain_ref, q_ref, k_ref, v_ref, kc_ref, vc_ref, o_ref, kall_ref, vext_ref, *,
                        lam_init):
    n_new = k_ref.shape[0]

    @pl.when(pl.program_id(2) == 0)
    def _():
        kall_ref[:n_new, :] = k_ref[...]
        kall_ref[n_new:, :] = kc_ref[...]
        vext_ref[:n_new, :V_DIM] = v_ref[...]
        vext_ref[n_new:, :V_DIM] = vc_ref[...]
        vext_ref[:, V_DIM:] = jnp.ones((vext_ref.shape[0], V_DIM), BF16)

    o = _diff_attn_head(q_ref[...], kall_ref[...], vext_ref[...], _diff_lambda(lam_ref, lam_init), gain_ref[...],
                        lam_init)
    o_ref[...] = o.astype(BF16)


def _attn_prompt(lam_vecs, gain, q, k, v, batch, seq, lam_init):
    d = q.shape[1]
    blk = pl.BlockSpec((seq, d), lambda b: (b, 0))
    return pl.pallas_call(
        functools.partial(_attn_prompt_kernel, lam_init=lam_init),
        out_shape=jax.ShapeDtypeStruct((batch * seq, d), BF16),
        grid=(batch,),
        in_specs=[
            pl.BlockSpec(lam_vecs.shape, lambda b: (0, 0)),
            pl.BlockSpec(gain.shape, lambda b: (0, 0)),
            blk, blk, blk,
        ],
        out_specs=blk,
        compiler_params=_params(("arbitrary",)),
        name="diff_attn_ctx",
    )(lam_vecs, gain, q, k, v)


def _attn_sample(lam_vecs, gain, q, k, v, kc, vc, batch, seq, n_ctx, row_offset, lam_init):
    d = q.shape[1]
    tq = Q_TILE
    qpt = seq // tq
    off_seq = row_offset // seq
    off_tile = row_offset // tq
    qmap = lambda b, h, i: (off_tile + b * qpt + i, h)
    kmap = lambda b, h, i: (off_seq + b, h)
    cmap = lambda b, h, i: (b, h)
    return pl.pallas_call(
        functools.partial(_attn_latent_kernel, lam_init=lam_init),
        out_shape=jax.ShapeDtypeStruct((batch * seq, d), BF16),
        grid=(batch, N_HEADS, qpt),
        in_specs=[
            pl.BlockSpec(lam_vecs.shape, lambda b, h, i: (0, 0)),
            pl.BlockSpec(gain.shape, lambda b, h, i: (0, 0)),
            pl.BlockSpec((tq, V_DIM), qmap),
            pl.BlockSpec((seq, V_DIM), kmap),
            pl.BlockSpec((seq, V_DIM), kmap),
            pl.BlockSpec((n_ctx, V_DIM), cmap),
            pl.BlockSpec((n_ctx, V_DIM), cmap),
        ],
        out_specs=pl.BlockSpec((tq, V_DIM), lambda b, h, i: (b * qpt + i, h)),
        scratch_shapes=[pltpu.VMEM((seq + n_ctx, V_DIM), BF16), pltpu.VMEM((seq + n_ctx, 2 * V_DIM), BF16)],
        compiler_params=_params(("arbitrary", "arbitrary", "arbitrary")),
        name="diff_attn_latent",
    )(lam_vecs, gain, q, k, v, kc, vc)


def _route(lg):
    lane = lax.broadcasted_iota(I32, lg.shape, 1)
    lane_f = lane.astype(F32)
    neg = jnp.float32(-jnp.inf)
    big = jnp.float32(LANES)
    gl = jnp.where(lane < N_EXPERT_GROUPS, lg, neg)
    gmax = jnp.max(gl, axis=-1, keepdims=True)
    g_prob = 1.0 / jnp.sum(jnp.exp(gl - gmax), axis=-1, keepdims=True)
    g_idx = jnp.min(jnp.where(gl == gmax, lane_f, big), axis=-1, keepdims=True)
    lo = N_EXPERT_GROUPS + EXPERTS_PER_GROUP * g_idx
    el = jnp.where((lane_f >= lo) & (lane_f < lo + EXPERTS_PER_GROUP), lg, neg)
    m1 = jnp.max(el, axis=-1, keepdims=True)
    i1 = jnp.min(jnp.where(el == m1, lane_f, big), axis=-1, keepdims=True)
    el2 = jnp.where(lane_f == i1, neg, el)
    m2 = jnp.max(el2, axis=-1, keepdims=True)
    i2 = jnp.min(jnp.where(el2 == m2, lane_f, big), axis=-1, keepdims=True)
    t = jnp.exp(m2 - m1)
    w1 = g_prob / (1.0 + t)
    w2 = g_prob * t / (1.0 + t)
    out = jnp.where(lane == 0, i1 - N_EXPERT_GROUPS, 0.0)
    out = jnp.where(lane == 1, i2 - N_EXPERT_GROUPS, out)
    out = jnp.where(lane == 2, w1, out)
    out = jnp.where(lane == 3, w2, out)
    return out


def _pack_bf16_pairs(h):
    n = h.shape[1] // 2
    bits = lax.bitcast_convert_type(h.astype(BF16).astype(F32), jnp.uint32)
    return (bits[:, :n] >> 16) | bits[:, n:]


def _unpack_bf16_pairs(p):
    lo = lax.bitcast_convert_type(p << 16, F32)
    hi = lax.bitcast_convert_type(p & jnp.uint32(0xFFFF0000), F32)
    return jnp.concatenate([lo, hi], axis=1)


def _mix_out_kernel(*refs, n_prompt_tiles, n_x):
    ap_ref, as_ref, w_ref = refs[:3]
    x_refs = refs[3:3 + n_x]
    mod_ref, gb_ref, wr_ref, x1_ref, hp_ref, route_ref = refs[3 + n_x:]
    is_prompt = pl.program_id(0) < n_prompt_tiles
    for rows in _row_chains(x1_ref.shape[0]):
        a = jnp.where(is_prompt, ap_ref[rows, :], as_ref[rows, :])
        out = _dot(a, w_ref[...])
        x = _read_rows(x_refs, n_prompt_tiles, rows)
        x1 = _deepnorm(x, mod_ref[2:3, :], out) * gb_ref[0:1, :] + gb_ref[1:2, :]
        x1_ref[rows, :] = x1
        h2 = _layernorm(x1) * (1.0 + mod_ref[4:5, :]) + mod_ref[3:4, :]
        hp_ref[rows, :] = _pack_bf16_pairs(h2)
        hi = h2.astype(BF16)
        lo = (h2 - hi.astype(F32)).astype(BF16)
        logits = _dot(hi, wr_ref[0]) + _dot(lo, wr_ref[0]) + _dot(hi, wr_ref[1])
        route_ref[rows, :] = _route(logits)


def _mix_out(a_prompt, a_sample, w, xs, mod, gb, wr, n_prompt, dec_seq, t_tok):
    d = w.shape[1]
    tl = _Tiles(n_prompt, dec_seq, TOKEN_TILE)
    tm = tl.tm
    return pl.pallas_call(
        functools.partial(_mix_out_kernel, n_prompt_tiles=tl.npt, n_x=len(xs)),
        out_shape=(
            jax.ShapeDtypeStruct((t_tok, d), F32),
            jax.ShapeDtypeStruct((t_tok, d // 2), jnp.uint32),
            jax.ShapeDtypeStruct((t_tok, LANES), F32),
        ),
        grid=(t_tok // tm,),
        in_specs=[
            pl.BlockSpec((tm, d), tl.prompt_part),
            pl.BlockSpec((tm, d), tl.latent_part),
            pl.BlockSpec(w.shape, lambda t: (0, 0)),
        ] + tl.source_specs(xs, d) + [
            pl.BlockSpec((None, N_MOD, d), tl.cond),
            pl.BlockSpec(gb.shape, lambda t: (0, 0)),
            pl.BlockSpec(wr.shape, lambda t: (0, 0, 0)),
        ],
        out_specs=(
            pl.BlockSpec((tm, d), tl.row),
            pl.BlockSpec((tm, d // 2), tl.row),
            pl.BlockSpec((tm, LANES), tl.row),
        ),
        compiler_params=_params(("arbitrary",)),
        name="mix_out_postnorm_router",
    )(a_prompt, a_sample, w, *xs, mod, gb, wr)


def _lane_prefix_sum(x, lane):
    sh = 1
    while sh < LANES:
        x = x + jnp.where(lane >= sh, pltpu.roll(x, sh, 1), 0.0)
        sh *= 2
    return x


def _lane_suffix_min_exclusive(x, lane):
    big = float(LANES)
    y = jnp.where(lane + 1 < LANES, pltpu.roll(x, LANES - 1, 1), big)
    sh = 1
    while sh < LANES:
        y = jnp.minimum(y, jnp.where(lane + sh < LANES, pltpu.roll(y, LANES - sh, 1), big))
        sh *= 2
    return y


def _work_items(counts, starts, ends, lane8, n_moe_tiles):
    shift = int(math.log2(MOE_TILE))
    first_tile = (starts.astype(I32) >> shift).astype(F32)
    last_tile = ((ends.astype(I32) - 1) >> shift).astype(F32)
    n_it = jnp.where(counts > 0.0, last_tile - first_tile + 1.0, 0.0)
    it_end = _lane_prefix_sum(n_it, lane8)
    it_start = it_end - n_it
    total = it_end[0:1, LANES - 1:LANES]
    sub = lax.broadcasted_iota(I32, (LANES, LANES), 0)
    rows = lambda x: jnp.broadcast_to(x[0:1, :], (LANES, LANES))
    used = counts > 0.0
    weight_slot = ((_lane_prefix_sum(jnp.where(used, 1.0, 0.0), lane8) - 1.0).astype(I32) & 1).astype(F32)
    next_used = _lane_suffix_min_exclusive(jnp.where(used, lane8.astype(F32), float(LANES)), lane8)
    per_expert = (starts, ends, first_tile, it_start, it_end, weight_slot, next_used)
    stacked = jnp.zeros((LANES, LANES), F32)
    for j, vec in enumerate(per_expert):
        stacked = jnp.where(sub == j, rows(vec), stacked)
    cols = stacked.T
    col = lambda j: cols[:, j:j + 1]
    sub_f = sub.astype(F32)
    w = lax.broadcasted_iota(I32, (LANES, LANES), 1).astype(F32)
    ex = jnp.sum(jnp.where((sub < N_EXPERTS) & (col(4) <= w), 1.0, 0.0), axis=0, keepdims=True)
    ex = jnp.minimum(ex, N_EXPERTS - 1.0)
    w1 = w[0:1, :]
    valid = w1 < total
    ex = jnp.where(valid, ex, jnp.max(jnp.where(valid, ex, 0.0), axis=-1, keepdims=True))
    onehot = sub_f == ex
    pick = lambda j: jnp.sum(jnp.where(onehot, col(j), 0.0), axis=0, keepdims=True)
    tile = jnp.where(valid, pick(2) + (w1 - pick(3)), n_moe_tiles - 1.0)
    lo = jnp.where(valid, jnp.maximum(pick(0), tile * MOE_TILE), 0.0)
    hi = jnp.where(valid, jnp.minimum(pick(1), (tile + 1.0) * MOE_TILE), 0.0)
    b8 = lambda x: jnp.broadcast_to(x, (SUBLANES, LANES))
    ex8, tile8 = b8(ex), b8(tile)
    first = jnp.where((lane8 == 0) | (tile8 != pltpu.roll(tile8, 1, 1)), 1.0, 0.0)
    newexp = jnp.where((lane8 == 0) | (ex8 != pltpu.roll(ex8, 1, 1)), 1.0, 0.0)
    sub8 = lax.broadcasted_iota(I32, (SUBLANES, LANES), 0)
    table = jnp.zeros((SUBLANES, LANES), F32)
    for j, vec in enumerate((ex8, tile8, b8(lo), b8(hi), first, newexp, b8(pick(5)), b8(pick(6)))):
        table = jnp.where(sub8 == j, vec, table)
    return table


def _plan_kernel(route_ref, pos_ref, items_ref, tri_ref, carry_ref, tot_ref, *, n_moe_tiles):
    p = pl.program_id(0)
    t = pl.program_id(1)
    tm = route_ref.shape[0]
    r = route_ref[...]
    lane = lax.broadcasted_iota(I32, (tm, LANES), 1)
    lane_f = lane.astype(F32)
    e0 = r[:, 0:1]
    e1 = r[:, 1:2] + N_EXPERTS
    m = jnp.where((lane_f == e0) | (lane_f == e1), 1.0, 0.0)
    colsum = jnp.sum(m, axis=0, keepdims=True)

    @pl.when((p == 0) & (t == 0))
    def _():
        carry_ref[...] = jnp.zeros_like(carry_ref)
        row = lax.broadcasted_iota(I32, (tm, tm), 0)
        col = lax.broadcasted_iota(I32, (tm, tm), 1)
        tri_ref[...] = jnp.where(row > col, 1.0, 0.0).astype(BF16)

    @pl.when(p == 0)
    def _():
        carry_ref[...] += colsum

    @pl.when((p == 0) & (t == pl.num_programs(1) - 1))
    def _():
        tot_ref[...] = carry_ref[...]
        carry_ref[...] = jnp.zeros_like(carry_ref)

    @pl.when(p == 1)
    def _():
        lane8 = lax.broadcasted_iota(I32, (SUBLANES, LANES), 1)
        tot = tot_ref[...]
        is_first = lane8 < N_EXPERTS
        tot0 = jnp.where(is_first, tot, 0.0)
        counts = jnp.where(is_first, tot + pltpu.roll(tot, LANES - N_EXPERTS, 1), 0.0)
        ends = _lane_prefix_sum(counts, lane8)
        starts = ends - counts
        base = jnp.where(is_first, starts, pltpu.roll(starts + tot0, N_EXPERTS, 1))
        before = _dot(tri_ref[...], m.astype(BF16)) + carry_ref[0:1, :]
        carry_ref[...] += colsum
        rows = before + base[0:1, :]
        pos0 = jnp.sum(jnp.where(lane_f == e0, rows, 0.0), axis=-1, keepdims=True)
        pos1 = jnp.sum(jnp.where(lane_f == e1, rows, 0.0), axis=-1, keepdims=True)
        both = jnp.where(lane == 0, pos0, jnp.where(lane == 1, pos1, 0.0))
        pos_ref[...] = both.T[0:SUBLANES, :].astype(I32)

        @pl.when(t == 0)
        def _():
            items_ref[...] = _work_items(counts, starts, ends, lane8, n_moe_tiles).astype(I32)


def _moe_plan(route):
    t_tok = route.shape[0]
    tm = PLAN_TILE
    n_moe_tiles = 2 * t_tok // MOE_TILE
    assert n_moe_tiles + N_EXPERTS - 1 <= LANES and 2 * N_EXPERTS <= LANES and MOE_TILE & (MOE_TILE - 1) == 0
    return pl.pallas_call(
        functools.partial(_plan_kernel, n_moe_tiles=n_moe_tiles),
        out_shape=(jax.ShapeDtypeStruct((SUBLANES, t_tok), I32), jax.ShapeDtypeStruct((SUBLANES, LANES), I32)),
        grid=(2, t_tok // tm),
        in_specs=[pl.BlockSpec((tm, LANES), lambda p, t: (t, 0))],
        out_specs=(pl.BlockSpec((SUBLANES, tm), lambda p, t: (0, t * p)),
                   pl.BlockSpec((SUBLANES, LANES), lambda p, t: (0, 0))),
        scratch_shapes=[pltpu.VMEM((tm, tm), BF16), pltpu.VMEM((SUBLANES, LANES), F32),
                        pltpu.VMEM((SUBLANES, LANES), F32)],
        compiler_params=_params(("arbitrary", "arbitrary")),
        name="moe_positions",
    )(route)


def _sc_workers():
    info = plsc.get_sparse_core_info()
    return info.num_cores, info.num_cores * info.num_subcores


def _sc_pipeline(n_chunks, loads, stores):
    def start(copies):
        for cp in copies:
            cp.start()

    def wait(copies):
        for cp in copies:
            cp.wait()

    start(loads(0, 0))
    for j in range(n_chunks):
        b = j % 2
        if j + 1 < n_chunks:
            if j >= 1:
                wait(stores(j - 1, 1 - b))
            start(loads(j + 1, 1 - b))
        wait(loads(j, b))
        start(stores(j, b))
    if n_chunks >= 2:
        wait(stores(n_chunks - 2, n_chunks % 2))
    wait(stores(n_chunks - 1, (n_chunks - 1) % 2))


def _sc_chunking(n_rows, d):
    n_cores, n_workers = _sc_workers()
    per_worker = n_rows // n_workers
    k = SC_BUFFER_BYTES // (d * 4)
    n_chunks = per_worker // k
    assert n_chunks * k * n_workers == n_rows and k <= SC_MAX_INDEX_CHUNK
    return n_cores, n_workers, per_worker, k, n_chunks


def _sc_gather_rows(table, idx):
    n, d = idx.shape[0], table.shape[1]
    n_cores, n_workers, per_worker, k, n_chunks = _sc_chunking(n, d)

    def body(table_hbm, idx_hbm, out_hbm, idx_v, rows_v, gsem, osem):
        wid = lax.axis_index("s") * n_cores + lax.axis_index("c")
        base = wid * per_worker
        pltpu.sync_copy(idx_hbm.at[wid], idx_v)
        _sc_pipeline(
            n_chunks,
            lambda j, b: [pltpu.make_async_copy(table_hbm.at[idx_v.at[j]], rows_v.at[b], gsem.at[b])],
            lambda j, b: [pltpu.make_async_copy(rows_v.at[b], out_hbm.at[pl.ds(base + j * k, k)], osem.at[b])])

    return pl.kernel(
        body,
        out_type=jax.ShapeDtypeStruct((n, d), table.dtype),
        mesh=plsc.VectorSubcoreMesh(core_axis_name="c", subcore_axis_name="s"),
        scratch_types=[pltpu.VMEM((n_chunks, k), I32), pltpu.VMEM((2, k, d), table.dtype),
                       pltpu.SemaphoreType.DMA((2,)), pltpu.SemaphoreType.DMA((2,))],
        name="sc_row_gather",
    )(table, idx.reshape(n_workers, n_chunks, k))


def _sc_scatter_rows(src, idx):
    n_lists, n_src = idx.shape
    d = src.shape[1]
    n_cores, n_workers, per_worker, k, n_chunks = _sc_chunking(n_src, d)

    def body(src_hbm, idx_hbm, out_hbm, idx_v, rows_v, gsem, osem):
        wid = lax.axis_index("s") * n_cores + lax.axis_index("c")
        base = wid * per_worker
        pltpu.sync_copy(idx_hbm.at[wid], idx_v)
        _sc_pipeline(
            n_chunks,
            lambda j, b: [pltpu.make_async_copy(src_hbm.at[pl.ds(base + j * k, k)], rows_v.at[b], gsem.at[b])],
            lambda j, b: [pltpu.make_async_copy(rows_v.at[b], out_hbm.at[idx_v.at[c, j]], osem.at[b, c])
                          for c in range(n_lists)])

    return pl.kernel(
        body,
        out_type=jax.ShapeDtypeStruct((n_lists * n_src, d), src.dtype),
        mesh=plsc.VectorSubcoreMesh(core_axis_name="c", subcore_axis_name="s"),
        scratch_types=[pltpu.VMEM((n_lists, n_chunks, k), I32), pltpu.VMEM((2, k, d), src.dtype),
                       pltpu.SemaphoreType.DMA((2,)), pltpu.SemaphoreType.DMA((2, n_lists))],
        name="sc_row_scatter",
    )(src, idx.reshape(n_lists, n_workers, n_chunks, k).transpose(1, 0, 2, 3))


IT_EXPERT, IT_TILE, IT_LO, IT_HI, IT_FIRST, IT_NEWEXP, IT_SLOT, IT_NEXT = range(8)


def _moe_kernel(it_ref, xs_ref, wg_hbm, wu_hbm, wd_hbm, o_ref, wg_f, wu_f, wd_f, wgu_s, wd_s, sem, *, layer):
    w = pl.program_id(0)
    f = wg_f.shape[2]

    def weight_copies(expert, slot):
        return [pltpu.make_async_copy(src.at[layer, expert], dst.at[slot], sem.at[slot, j])
                for j, (src, dst) in enumerate(((wg_hbm, wg_f), (wu_hbm, wu_f), (wd_hbm, wd_f)))]

    @pl.when(it_ref[IT_NEWEXP, w] == 1)
    def _():
        expert = it_ref[IT_EXPERT, w]
        slot = it_ref[IT_SLOT, w]
        nxt = it_ref[IT_NEXT, w]

        @pl.when(w == 0)
        def _():
            for cp in weight_copies(expert, slot):
                cp.start()

        for cp in weight_copies(expert, slot):
            cp.wait()

        @pl.when(nxt < N_EXPERTS)
        def _():
            for cp in weight_copies(nxt, 1 - slot):
                cp.start()

        wgu_s[:, :f] = wg_f[slot].astype(BF16)
        wgu_s[:, f:] = wu_f[slot].astype(BF16)
        wd_s[...] = wd_f[slot].astype(BF16)

    tm = xs_ref.shape[0]
    x = _unpack_bf16_pairs(xs_ref[...]).astype(BF16)
    au = _dot(x, wgu_s[...])
    act = (_silu(au[:, :f]) * au[:, f:]).astype(BF16)
    y = _pack_bf16_pairs(_dot(act, wd_s[...]))
    row = it_ref[IT_TILE, w] * tm + lax.broadcasted_iota(I32, (tm, 1), 0)
    mine = (row >= it_ref[IT_LO, w]) & (row < it_ref[IT_HI, w])

    @pl.when(it_ref[IT_FIRST, w] == 1)
    def _():
        o_ref[...] = jnp.where(mine, y, jnp.zeros_like(y))

    @pl.when(it_ref[IT_FIRST, w] == 0)
    def _():
        o_ref[...] = jnp.where(mine, y, o_ref[...])


def _moe_experts(items, xs, w_gate, w_up, w_down, layer):
    n_rows, dp = xs.shape
    _, _, d, f = w_gate.shape
    tm = MOE_TILE
    rmap = lambda w, it: (it[IT_TILE, w], 0)
    hbm = pl.BlockSpec(memory_space=pl.ANY)
    return pl.pallas_call(
        functools.partial(_moe_kernel, layer=layer),
        out_shape=jax.ShapeDtypeStruct((n_rows, d // 2), jnp.uint32),
        grid_spec=pltpu.PrefetchScalarGridSpec(
            num_scalar_prefetch=1,
            grid=(n_rows // tm + N_EXPERTS - 1,),
            in_specs=[pl.BlockSpec((tm, dp), rmap), hbm, hbm, hbm],
            out_specs=pl.BlockSpec((tm, d // 2), rmap),
            scratch_shapes=[
                pltpu.VMEM((2, d, f), F32), pltpu.VMEM((2, d, f), F32), pltpu.VMEM((2, f, d), F32),
                pltpu.VMEM((d, 2 * f), BF16), pltpu.VMEM((f, d), BF16),
                pltpu.SemaphoreType.DMA((2, 3)),
            ],
        ),
        compiler_params=_params(("arbitrary",)),
        name="moe_grouped_mlp",
    )(items, xs, w_gate, w_up, w_down)


def _moe_combine_kernel(x_ref, y0_ref, y1_ref, route_ref, mod_ref, gb_ref, *o_refs, n_prompt_tiles):
    r = route_ref[...]
    moe = r[:, 2:3] * _unpack_bf16_pairs(y0_ref[...]) + r[:, 3:4] * _unpack_bf16_pairs(y1_ref[...])
    x2 = _deepnorm(x_ref[...], mod_ref[5:6, :], moe) * gb_ref[0:1, :] + gb_ref[1:2, :]
    if len(o_refs) == 1:
        o_refs[0][...] = x2
    else:
        @pl.when(pl.program_id(0) < n_prompt_tiles)
        def _():
            o_refs[0][...] = x2

        @pl.when(pl.program_id(0) >= n_prompt_tiles)
        def _():
            o_refs[1][...] = x2


def _moe_combine(x1, ys, route, mod, gb, n_prompt, dec_seq, split):
    t_tok, d = x1.shape
    tl = _Tiles(n_prompt, dec_seq, TOKEN_TILE)
    tm = tl.tm
    nt = t_tok // tm
    if split:
        out_shape = (jax.ShapeDtypeStruct((n_prompt, d), F32), jax.ShapeDtypeStruct((t_tok - n_prompt, d), F32))
        out_specs = (pl.BlockSpec((tm, d), tl.prompt_part), pl.BlockSpec((tm, d), tl.latent_part))
    else:
        out_shape = jax.ShapeDtypeStruct((t_tok, d), F32)
        out_specs = pl.BlockSpec((tm, d), tl.row)
    return pl.pallas_call(
        functools.partial(_moe_combine_kernel, n_prompt_tiles=tl.npt),
        out_shape=out_shape,
        grid=(nt,),
        in_specs=[
            pl.BlockSpec((tm, d), tl.row),
            pl.BlockSpec((tm, ys.shape[1]), tl.row),
            pl.BlockSpec((tm, ys.shape[1]), lambda t: (t + nt, 0)),
            pl.BlockSpec((tm, LANES), tl.row),
            pl.BlockSpec((None, N_MOD, d), tl.cond),
            pl.BlockSpec(gb.shape, lambda t: (0, 0)),
        ],
        out_specs=out_specs,
        compiler_params=_params(("arbitrary",)),
        name="moe_combine_postnorm",
    )(x1, ys, ys, route, mod, gb)


def _rope_tables(n_lat):
    nf = HEAD_DIM // 4
    s = np.arange(n_lat)
    lane = np.arange(LANES)
    inv = ROPE_BASE ** (-(lane % nf).astype(np.float64) / nf)
    use_col = (lane % HEAD_DIM) >= HEAD_DIM // 2
    p = np.where(use_col[None, :], (s % GRID_W)[:, None], (s // GRID_W)[:, None]).astype(np.float64)
    ang = p * inv[None, :]
    sign = np.where((lane % (2 * nf)) < nf, -1.0, 1.0)
    return jnp.asarray(np.cos(ang), F32), jnp.asarray(np.sin(ang) * sign[None, :], F32)


def _dft_tables(n):
    k = np.arange(n)
    ang = 2.0 * np.pi * ((k[:, None] * k[None, :]) % n).astype(np.float64) / n
    return np.cos(ang), np.sin(ang)


def kernel(x_prompt, x_sample, cache_k, cache_v, c, c_ctx, w_ada, b_ada, ln_gain, ln_bias, w_qkv, w_attn_out,
           lambda_q1, lambda_k1, lambda_q2, lambda_k2, subln_gain, w_fourier_out, w_router_group,
           w_router_expert, w_expert_gate, w_expert_up, w_expert_down):
    bp, sp, d = x_prompt.shape
    bs, n_lat, _ = x_sample.shape
    n_ctx = cache_k.shape[2]
    n_prompt = bp * sp
    t_tok = n_prompt + bs * n_lat
    assert d == D_MODEL and n_prompt % n_lat == 0 and n_lat % TOKEN_TILE == 0 and sp == QKV_TILE

    cond = jnp.concatenate([c_ctx[None, :], c, jnp.zeros((SUBLANES - 1 - bs, d), F32)], axis=0)
    mods = _ada_all(cond, w_ada, b_ada).reshape(DEPTH, SUBLANES, N_MOD, d)

    cos, sin = _rope_tables(n_lat)
    cc, sc = _dft_tables(FOURIER_GROUP_DIM)
    dcs = jnp.asarray(np.concatenate([cc, sc], axis=1), BF16)
    seq_tabs = {s: tuple(jnp.asarray(m, BF16) for m in _dft_tables(s)) for s in (sp, n_lat)}
    attn_layers = [i for i in range(DEPTH) if i % 2 == 0]

    xs = (x_prompt.reshape(n_prompt, d), x_sample.reshape(bs * n_lat, d))
    prev_kv = []
    new_k = new_v = None
    for i in range(DEPTH):
        mod = mods[i]
        if i % 2 == 0:
            a = i // 2
            lam_init = 0.8 - 0.6 * math.exp(-0.3 * i)
            last_attn = i == attn_layers[-1]
            q, k, v, kf, vf = _qkv(xs, mod, w_qkv[a].astype(BF16), cos, sin, prev_kv, last_attn, n_prompt, n_lat,
                                   t_tok)
            if last_attn:
                new_k, new_v = kf, vf
            else:
                prev_kv.append((kf, vf))
            lam_vecs = jnp.stack([lambda_q1[a], lambda_k1[a], lambda_q2[a], lambda_k2[a]], axis=0)
            gain = subln_gain[a][None, :]
            kc = cache_k[:, a].reshape(bs * n_ctx, d).astype(BF16)
            vc = cache_v[:, a].reshape(bs * n_ctx, d).astype(BF16)
            mixed_p = _attn_prompt(lam_vecs, gain, q, k, v, bp, sp, lam_init)
            mixed_s = _attn_sample(lam_vecs, gain, q, k, v, kc, vc, bs, n_lat, n_ctx, n_prompt, lam_init)
            w_mix = w_attn_out[a]
        else:
            xc, xsn = _chan_dft(xs[0], mod, dcs, n_prompt, n_lat)
            mixed_p = _seq_dft(*seq_tabs[sp], xc, xsn, bp, sp, 0)
            mixed_s = _seq_dft(*seq_tabs[n_lat], xc, xsn, bs, n_lat, n_prompt)
            w_mix = w_fourier_out[i // 2]
        gb0 = jnp.stack([ln_gain[i, 0], ln_bias[i, 0]], axis=0)
        gb1 = jnp.stack([ln_gain[i, 1], ln_bias[i, 1]], axis=0)
        wr = jnp.concatenate([w_router_group[i], w_router_expert[i],
                              jnp.zeros((d, LANES - N_EXPERT_GROUPS - N_EXPERTS), F32)], axis=1)
        wr_hi = wr.astype(BF16)
        wr_lo = (wr - wr_hi.astype(F32)).astype(BF16)
        x1, hp, route = _mix_out(mixed_p, mixed_s, w_mix.astype(BF16), xs, mod, gb0,
                                 jnp.stack([wr_hi, wr_lo], axis=0), n_prompt, n_lat, t_tok)
        pos8, items = _moe_plan(route)
        pos = pos8[0:2]
        xs_sorted = _sc_scatter_rows(hp, pos)
        ys = _moe_experts(items, xs_sorted, w_expert_gate, w_expert_up, w_expert_down, i)
        yg = _sc_gather_rows(ys, pos.reshape(-1))
        last = i == DEPTH - 1
        out = _moe_combine(x1, yg, route, mod, gb1, n_prompt, n_lat, split=last)
        xs = out if last else (out,)

    y_prompt = xs[0].reshape(bp, sp, d)
    y_sample = xs[1].reshape(bs, n_lat, d)
    return (y_prompt, y_sample, new_k.reshape(bp, len(attn_layers), sp, N_HEADS, 2 * HEAD_DIM),
            new_v.reshape(bp, len(attn_layers), sp, N_HEADS, V_DIM))
```

```python
import functools
import math

import numpy as np
import jax
import jax.numpy as jnp
from jax import lax
from jax.experimental import pallas as pl
from jax.experimental.pallas import tpu as pltpu
from jax.experimental.pallas import tpu_sc as plsc

F32 = jnp.float32
BF16 = jnp.bfloat16
I32 = jnp.int32

D_MODEL = 1024
DEPTH = 4
GRID_W = 64
N_HEADS = 8
HEAD_DIM = 64
V_DIM = 2 * HEAD_DIM
ROPE_BASE = 10000.0
N_FOURIER_GROUPS = 8
FOURIER_GROUP_DIM = D_MODEL // N_FOURIER_GROUPS
N_EXPERT_GROUPS = 4
EXPERTS_PER_GROUP = 8
N_EXPERTS = N_EXPERT_GROUPS * EXPERTS_PER_GROUP
D_EXPERT = 256
N_MOD = 6
LN_EPS = 1e-5
DEEPNORM_ALPHA = (2.0 * DEPTH) ** 0.25
Q_SCALE = math.log2(math.e) * HEAD_DIM ** -0.5

LANES = 128
SUBLANES = 8
TOKEN_TILE = 512
CHAIN_ROWS = 256
QKV_TILE = 256
Q_TILE = 512
ATTN_ROW_CHUNK = 128
MOE_TILE = 512
PLAN_TILE = 1024
SC_BUFFER_BYTES = 128 * 1024
SC_MAX_INDEX_CHUNK = 128
VMEM_LIMIT = 48 * 1024 * 1024


def _params(semantics):
    return pltpu.CompilerParams(dimension_semantics=semantics, vmem_limit_bytes=VMEM_LIMIT)


def _layernorm(x, eps=LN_EPS):
    mu = jnp.mean(x, axis=-1, keepdims=True)
    xc = x - mu
    var = jnp.mean(xc * xc, axis=-1, keepdims=True)
    return xc * lax.rsqrt(var + eps)


def _deepnorm(x, branch_gate, branch):
    return _layernorm(x + (branch_gate * (1.0 / DEEPNORM_ALPHA)) * branch, LN_EPS / DEEPNORM_ALPHA ** 2)


def _silu(a):
    return a / (1.0 + jnp.exp(-a))


def _dot(a, b):
    return jnp.dot(a, b, preferred_element_type=F32)


class _Tiles:
    def __init__(self, n_prompt, dec_seq, tm):
        self.tm = tm
        self.npt = n_prompt // tm
        self.tps = dec_seq // tm

    def row(self, t):
        return (t, 0)

    def cond(self, t):
        return (jnp.where(t < self.npt, 0, (t - self.npt) // self.tps + 1), 0, 0)

    def latent_pos(self, t):
        return (jnp.maximum(t - self.npt, 0) % self.tps, 0)

    def prompt_part(self, t):
        return (jnp.minimum(t, self.npt - 1), 0)

    def latent_part(self, t):
        return (jnp.maximum(t - self.npt, 0), 0)

    def source_specs(self, arrays, d, tile_of=lambda step: step):
        maps = (self.row,) if len(arrays) == 1 else (self.prompt_part, self.latent_part)
        return [pl.BlockSpec((self.tm, d), lambda s, m=m: m(tile_of(s))) for m in maps]


def _read_rows(refs, is_prompt_tile, rows=slice(None)):
    if len(refs) == 1:
        return refs[0][rows, :]
    return jnp.where(is_prompt_tile, refs[0][rows, :], refs[1][rows, :])


def _ada_kernel(cond_ref, w_ref, b_ref, o_ref):
    a = _silu(cond_ref[...])
    o_ref[...] = _dot(a.astype(BF16), w_ref[...].astype(BF16)) + b_ref[...]


def _ada_all(cond, w_ada, b_ada):
    depth, d, n = w_ada.shape
    tn = n // 4
    return pl.pallas_call(
        _ada_kernel,
        out_shape=jax.ShapeDtypeStruct((depth, cond.shape[0], n), F32),
        grid=(depth, n // tn),
        in_specs=[
            pl.BlockSpec(cond.shape, lambda l, j: (0, 0)),
            pl.BlockSpec((None, d, tn), lambda l, j: (l, 0, j)),
            pl.BlockSpec((None, 1, tn), lambda l, j: (l, 0, j)),
        ],
        out_specs=pl.BlockSpec((None, cond.shape[0], tn), lambda l, j: (l, 0, j)),
        compiler_params=_params(("arbitrary", "arbitrary")),
        name="ada",
    )(cond, w_ada, b_ada.reshape(depth, 1, n))


def _rope(x, cos, sin_signed, first_half):
    outs = []
    for c in range(x.shape[1] // LANES):
        xc = x[:, c * LANES:(c + 1) * LANES]
        partner = jnp.where(first_half, pltpu.roll(xc, LANES - 16, 1), pltpu.roll(xc, 16, 1))
        outs.append(xc * cos + partner * sin_signed)
    return jnp.concatenate(outs, axis=1)


def _store_heads(cache_ref, slot, rows):
    for hd in range(N_HEADS):
        cache_ref[slot, :, hd, :] = rows[:, hd * V_DIM:(hd + 1) * V_DIM]


def _qkv_kernel(*refs, n_prompt_tiles, n_x, n_prev):
    x_refs, (mod_ref, w_ref, cos_ref, sin_ref) = refs[:n_x], refs[n_x:n_x + 4]
    prev = refs[n_x + 4:n_x + 4 + 2 * max(n_prev, 0)]
    q_ref, k_ref, v_ref, ko_ref, vo_ref = refs[n_x + 4 + 2 * max(n_prev, 0):]
    t = pl.program_id(0)
    h = (_layernorm(_read_rows(x_refs, t < n_prompt_tiles)) * (1.0 + mod_ref[1:2, :])
         + mod_ref[0:1, :]).astype(BF16)
    d = w_ref.shape[0]
    project = lambda j: _dot(h, w_ref[:, j * d:(j + 1) * d])

    @pl.when(t < n_prompt_tiles)
    def _():
        q_ref[...] = (project(0) * Q_SCALE).astype(BF16)
        k = project(1)
        k_ref[...] = k.astype(BF16)
        v = project(2)
        v_ref[...] = v.astype(BF16)
        if n_prev < 0:
            ko_ref[...] = k
            vo_ref[...] = v
        else:
            for a in range(n_prev):
                _store_heads(ko_ref, a, prev[2 * a][...])
                _store_heads(vo_ref, a, prev[2 * a + 1][...])
            _store_heads(ko_ref, n_prev, k)
            _store_heads(vo_ref, n_prev, v)

    @pl.when(t >= n_prompt_tiles)
    def _():
        lane = lax.broadcasted_iota(I32, (h.shape[0], LANES), 1)
        first_half = (lane % 32) < 16
        cos = cos_ref[...]
        sin = sin_ref[...]
        q_ref[...] = _rope(project(0) * Q_SCALE, cos, sin, first_half).astype(BF16)
        k_ref[...] = _rope(project(1), cos, sin, first_half).astype(BF16)
        v_ref[...] = project(2).astype(BF16)


def _qkv(xs, mod, w_qkv, cos, sin, prev_kv, finish_cache, n_prompt, dec_seq, t_tok):
    d = w_qkv.shape[0]
    tl = _Tiles(n_prompt, dec_seq, QKV_TILE)
    tm = tl.tm
    n_prev = len(prev_kv) if finish_cache else -1
    if finish_cache:
        n_slots = n_prev + 1
        kv_shape = jax.ShapeDtypeStruct((n_prompt // tm, n_slots, tm, N_HEADS, V_DIM), F32)
        kv_spec = pl.BlockSpec((None, n_slots, tm, N_HEADS, V_DIM), lambda t: (tl.prompt_part(t)[0], 0, 0, 0, 0))
    else:
        kv_shape = jax.ShapeDtypeStruct((n_prompt, d), F32)
        kv_spec = pl.BlockSpec((tm, d), tl.prompt_part)
    prev_flat = [a for kv in prev_kv for a in kv] if finish_cache else []
    return pl.pallas_call(
        functools.partial(_qkv_kernel, n_prompt_tiles=tl.npt, n_x=len(xs), n_prev=n_prev),
        out_shape=(
            jax.ShapeDtypeStruct((t_tok, d), BF16),
            jax.ShapeDtypeStruct((t_tok, d), BF16),
            jax.ShapeDtypeStruct((t_tok, d), BF16),
            kv_shape, kv_shape,
        ),
        grid=(t_tok // tm,),
        in_specs=tl.source_specs(xs, d) + [
            pl.BlockSpec((None, N_MOD, d), tl.cond),
            pl.BlockSpec(w_qkv.shape, lambda t: (0, 0)),
            pl.BlockSpec((tm, LANES), tl.latent_pos),
            pl.BlockSpec((tm, LANES), tl.latent_pos),
        ] + [pl.BlockSpec((tm, d), tl.prompt_part) for _ in prev_flat],
        out_specs=(
            pl.BlockSpec((tm, d), tl.row),
            pl.BlockSpec((tm, d), tl.row),
            pl.BlockSpec((tm, d), tl.row),
            kv_spec, kv_spec,
        ),
        compiler_params=_params(("arbitrary",)),
        name="ln_qkv_rope",
    )(*xs, mod, w_qkv, cos, sin, *prev_flat)


def _row_chains(n_rows):
    return [slice(r, r + CHAIN_ROWS) for r in range(0, n_rows, CHAIN_ROWS)]


def _chan_dft_kernel(x_ref, mod_ref, dcs_ref, xc_ref, xs_ref):
    g = FOURIER_GROUP_DIM
    for rows in _row_chains(x_ref.shape[0]):
        h = (_layernorm(x_ref[rows, :]) * (1.0 + mod_ref[1:2, :]) + mod_ref[0:1, :]).astype(BF16)
        for i in range(N_FOURIER_GROUPS):
            r = _dot(h[:, i * g:(i + 1) * g], dcs_ref[...])
            xc_ref[rows, i * g:(i + 1) * g] = r[:, :g].astype(BF16)
            xs_ref[rows, i * g:(i + 1) * g] = r[:, g:].astype(BF16)


def _chan_dft(x, mod, dcs, n_prompt, dec_seq):
    t_tok, d = x.shape
    tl = _Tiles(n_prompt, dec_seq, TOKEN_TILE)
    return pl.pallas_call(
        _chan_dft_kernel,
        out_shape=(jax.ShapeDtypeStruct((t_tok, d), BF16), jax.ShapeDtypeStruct((t_tok, d), BF16)),
        grid=(t_tok // tl.tm,),
        in_specs=[
            pl.BlockSpec((tl.tm, d), tl.row),
            pl.BlockSpec((None, N_MOD, d), tl.cond),
            pl.BlockSpec(dcs.shape, lambda t: (0, 0)),
        ],
        out_specs=(pl.BlockSpec((tl.tm, d), tl.row), pl.BlockSpec((tl.tm, d), tl.row)),
        compiler_params=_params(("arbitrary",)),
        name="ln_chan_dft",
    )(x, mod, dcs)


def _seq_dft_kernel(cs_ref, ss_ref, xc_ref, xs_ref, o_ref, *, norm):
    f = _dot(cs_ref[...], xc_ref[...]) - _dot(ss_ref[...], xs_ref[...])
    o_ref[...] = (f * norm).astype(BF16)


def _seq_dft(cs, ss, xc, xs, batch, seq, row_offset):
    d = xc.shape[1]
    tm = min(QKV_TILE, seq)
    spt = seq // tm
    off_seq = row_offset // seq
    return pl.pallas_call(
        functools.partial(_seq_dft_kernel, norm=1.0 / math.sqrt(seq * FOURIER_GROUP_DIM)),
        out_shape=jax.ShapeDtypeStruct((batch * seq, d), BF16),
        grid=(batch, spt),
        in_specs=[
            pl.BlockSpec((tm, seq), lambda b, i: (i, 0)),
            pl.BlockSpec((tm, seq), lambda b, i: (i, 0)),
            pl.BlockSpec((seq, d), lambda b, i: (off_seq + b, 0)),
            pl.BlockSpec((seq, d), lambda b, i: (off_seq + b, 0)),
        ],
        out_specs=pl.BlockSpec((tm, d), lambda b, i: (b * spt + i, 0)),
        compiler_params=_params(("arbitrary", "arbitrary")),
        name=f"seq_dft_{seq}",
    )(cs, ss, xc, xs)


def _diff_lambda(lam_ref, lam_init):
    lv = lam_ref[...]
    return (jnp.exp(jnp.sum(lv[0:1] * lv[1:2], axis=-1, keepdims=True))
            - jnp.exp(jnp.sum(lv[2:3] * lv[3:4], axis=-1, keepdims=True)) + lam_init)


def _diff_attn_head(q, k, v_ext, lam, gain, lam_init):
    tq = q.shape[0]
    lane = lax.broadcasted_iota(I32, q.shape, 1)
    zero = jnp.zeros_like(q)
    qq = jnp.concatenate([jnp.where(lane < HEAD_DIM, q, zero), jnp.where(lane >= HEAD_DIM, q, zero)], axis=0)
    parts = []
    for r in range(0, 2 * tq, ATTN_ROW_CHUNK):
        s = lax.dot_general(qq[r:r + ATTN_ROW_CHUNK], k, (((1,), (1,)), ((), ())), preferred_element_type=F32)
        e = jnp.exp2(s - jnp.max(s, axis=-1, keepdims=True)).astype(BF16)
        parts.append(_dot(e, v_ext))
    oe = jnp.concatenate(parts, axis=0)
    o = oe[:, :V_DIM] / oe[:, V_DIM:]
    o = o[:tq] - lam * o[tq:]
    o = o * lax.rsqrt(jnp.mean(o * o, axis=-1, keepdims=True) + LN_EPS)
    return o * gain * (1.0 - lam_init)


def _attn_prompt_kernel(lam_ref, gain_ref, q_ref, k_ref, v_ref, o_ref, *, lam_init):
    lam = _diff_lambda(lam_ref, lam_init)
    gain = gain_ref[...]
    ones = jnp.ones((k_ref.shape[0], V_DIM), BF16)
    for hd in range(N_HEADS):
        cols = slice(hd * V_DIM, (hd + 1) * V_DIM)
        v_ext = jnp.concatenate([v_ref[:, cols], ones], axis=1)
        o_ref[:, cols] = _diff_attn_head(q_ref[:, cols], k_ref[:, cols], v_ext, lam, gain, lam_init).astype(BF16)


def _attn_latent_kernel(lam_ref, gain_ref, q_ref, k_ref, v_ref, kc_ref, vc_ref, o_ref, kall_ref, vext_ref, *,
                        lam_init):
    n_new = k_ref.shape[0]

    @pl.when(pl.program_id(2) == 0)
    def _():
        kall_ref[:n_new, :] = k_ref[...]
        kall_ref[n_new:, :] = kc_ref[...]
        vext_ref[:n_new, :V_DIM] = v_ref[...]
        vext_ref[n_new:, :V_DIM] = vc_ref[...]
        vext_ref[:, V_DIM:] = jnp.ones((vext_ref.shape[0], V_DIM), BF16)

    o = _diff_attn_head(q_ref[...], kall_ref[...], vext_ref[...], _diff_lambda(lam_ref, lam_init), gain_ref[...],
                        lam_init)
    o_ref[...] = o.astype(BF16)


def _attn_prompt(lam_vecs, gain, q, k, v, batch, seq, lam_init):
    d = q.shape[1]
    blk = pl.BlockSpec((seq, d), lambda b: (b, 0))
    return pl.pallas_call(
        functools.partial(_attn_prompt_kernel, lam_init=lam_init),
        out_shape=jax.ShapeDtypeStruct((batch * seq, d), BF16),
        grid=(batch,),
        in_specs=[
            pl.BlockSpec(lam_vecs.shape, lambda b: (0, 0)),
            pl.BlockSpec(gain.shape, lambda b: (0, 0)),
            blk, blk, blk,
        ],
        out_specs=blk,
        compiler_params=_params(("arbitrary",)),
        name="diff_attn_ctx",
    )(lam_vecs, gain, q, k, v)


def _attn_sample(lam_vecs, gain, q, k, v, kc, vc, batch, seq, n_ctx, row_offset, lam_init):
    d = q.shape[1]
    tq = Q_TILE
    qpt = seq // tq
    off_seq = row_offset // seq
    off_tile = row_offset // tq
    qmap = lambda b, h, i: (off_tile + b * qpt + i, h)
    kmap = lambda b, h, i: (off_seq + b, h)
    cmap = lambda b, h, i: (b, h)
    return pl.pallas_call(
        functools.partial(_attn_latent_kernel, lam_init=lam_init),
        out_shape=jax.ShapeDtypeStruct((batch * seq, d), BF16),
        grid=(batch, N_HEADS, qpt),
        in_specs=[
            pl.BlockSpec(lam_vecs.shape, lambda b, h, i: (0, 0)),
            pl.BlockSpec(gain.shape, lambda b, h, i: (0, 0)),
            pl.BlockSpec((tq, V_DIM), qmap),
            pl.BlockSpec((seq, V_DIM), kmap),
            pl.BlockSpec((seq, V_DIM), kmap),
            pl.BlockSpec((n_ctx, V_DIM), cmap),
            pl.BlockSpec((n_ctx, V_DIM), cmap),
        ],
        out_specs=pl.BlockSpec((tq, V_DIM), lambda b, h, i: (b * qpt + i, h)),
        scratch_shapes=[pltpu.VMEM((seq + n_ctx, V_DIM), BF16), pltpu.VMEM((seq + n_ctx, 2 * V_DIM), BF16)],
        compiler_params=_params(("arbitrary", "arbitrary", "arbitrary")),
        name="diff_attn_latent",
    )(lam_vecs, gain, q, k, v, kc, vc)


def _route(lg):
    lane = lax.broadcasted_iota(I32, lg.shape, 1)
    lane_f = lane.astype(F32)
    neg = jnp.float32(-jnp.inf)
    big = jnp.float32(LANES)
    gl = jnp.where(lane < N_EXPERT_GROUPS, lg, neg)
    gmax = jnp.max(gl, axis=-1, keepdims=True)
    g_prob = 1.0 / jnp.sum(jnp.exp(gl - gmax), axis=-1, keepdims=True)
    g_idx = jnp.min(jnp.where(gl == gmax, lane_f, big), axis=-1, keepdims=True)
    lo = N_EXPERT_GROUPS + EXPERTS_PER_GROUP * g_idx
    el = jnp.where((lane_f >= lo) & (lane_f < lo + EXPERTS_PER_GROUP), lg, neg)
    m1 = jnp.max(el, axis=-1, keepdims=True)
    i1 = jnp.min(jnp.where(el == m1, lane_f, big), axis=-1, keepdims=True)
    el2 = jnp.where(lane_f == i1, neg, el)
    m2 = jnp.max(el2, axis=-1, keepdims=True)
    i2 = jnp.min(jnp.where(el2 == m2, lane_f, big), axis=-1, keepdims=True)
    t = jnp.exp(m2 - m1)
    w1 = g_prob / (1.0 + t)
    w2 = g_prob * t / (1.0 + t)
    out = jnp.where(lane == 0, i1 - N_EXPERT_GROUPS, 0.0)
    out = jnp.where(lane == 1, i2 - N_EXPERT_GROUPS, out)
    out = jnp.where(lane == 2, w1, out)
    out = jnp.where(lane == 3, w2, out)
    return out


def _pack_bf16_pairs(h):
    n = h.shape[1] // 2
    bits = lax.bitcast_convert_type(h.astype(BF16).astype(F32), jnp.uint32)
    return (bits[:, :n] >> 16) | bits[:, n:]


def _unpack_bf16_pairs(p):
    lo = lax.bitcast_convert_type(p << 16, F32)
    hi = lax.bitcast_convert_type(p & jnp.uint32(0xFFFF0000), F32)
    return jnp.concatenate([lo, hi], axis=1)


def _mix_out_kernel(*refs, n_prompt_tiles, n_tiles, n_x):
    ap_ref, as_ref, w_ref = refs[:3]
    x_refs = refs[3:3 + n_x]
    mod_ref, gb_ref, wr_ref, x1_ref, hp_ref, route_ref, proj_even, proj_odd = refs[3 + n_x:]
    s = pl.program_id(0)

    @pl.when(s == 0)
    def _():
        proj_odd[...] = jnp.zeros_like(proj_odd)

    def step(proj_new, proj_old):
        a = jnp.where(jnp.minimum(s, n_tiles - 1) < n_prompt_tiles, ap_ref[...], as_ref[...])
        proj_new[...] = _dot(a, w_ref[...])

        x = _read_rows(x_refs, jnp.maximum(s - 1, 0) < n_prompt_tiles)
        x1 = _deepnorm(x, mod_ref[2:3, :], proj_old[...]) * gb_ref[0:1, :] + gb_ref[1:2, :]
        x1_ref[...] = x1
        h2 = _layernorm(x1) * (1.0 + mod_ref[4:5, :]) + mod_ref[3:4, :]
        hp_ref[...] = _pack_bf16_pairs(h2)
        hi = h2.astype(BF16)
        lo = (h2 - hi.astype(F32)).astype(BF16)
        logits = _dot(hi, wr_ref[0]) + _dot(lo, wr_ref[0]) + _dot(hi, wr_ref[1])
        route_ref[...] = _route(logits)

    @pl.when(s % 2 == 0)
    def _():
        step(proj_even, proj_odd)

    @pl.when(s % 2 == 1)
    def _():
        step(proj_odd, proj_even)


def _mix_out(a_prompt, a_sample, w, xs, mod, gb, wr, n_prompt, dec_seq, t_tok):
    d = w.shape[1]
    tl = _Tiles(n_prompt, dec_seq, TOKEN_TILE)
    tm = tl.tm
    nt = t_tok // tm
    projected = lambda s: jnp.minimum(s, nt - 1)
    finished = lambda s: jnp.maximum(s - 1, 0)
    return pl.pallas_call(
        functools.partial(_mix_out_kernel, n_prompt_tiles=tl.npt, n_tiles=nt, n_x=len(xs)),
        out_shape=(
            jax.ShapeDtypeStruct((t_tok, d), F32),
            jax.ShapeDtypeStruct((t_tok, d // 2), jnp.uint32),
            jax.ShapeDtypeStruct((t_tok, LANES), F32),
        ),
        grid=(nt + 1,),
        in_specs=[
            pl.BlockSpec((tm, d), lambda s: tl.prompt_part(projected(s))),
            pl.BlockSpec((tm, d), lambda s: tl.latent_part(projected(s))),
            pl.BlockSpec(w.shape, lambda s: (0, 0)),
        ] + tl.source_specs(xs, d, finished) + [
            pl.BlockSpec((None, N_MOD, d), lambda s: tl.cond(finished(s))),
            pl.BlockSpec(gb.shape, lambda s: (0, 0)),
            pl.BlockSpec(wr.shape, lambda s: (0, 0, 0)),
        ],
        out_specs=(
            pl.BlockSpec((tm, d), lambda s: tl.row(finished(s))),
            pl.BlockSpec((tm, d // 2), lambda s: tl.row(finished(s))),
            pl.BlockSpec((tm, LANES), lambda s: tl.row(finished(s))),
        ),
        scratch_shapes=[pltpu.VMEM((tm, d), F32), pltpu.VMEM((tm, d), F32)],
        compiler_params=_params(("arbitrary",)),
        name="mix_out_postnorm_router",
    )(a_prompt, a_sample, w, *xs, mod, gb, wr)


def _lane_prefix_sum(x, lane):
    sh = 1
    while sh < LANES:
        x = x + jnp.where(lane >= sh, pltpu.roll(x, sh, 1), 0.0)
        sh *= 2
    return x


def _lane_suffix_min_exclusive(x, lane):
    big = float(LANES)
    y = jnp.where(lane + 1 < LANES, pltpu.roll(x, LANES - 1, 1), big)
    sh = 1
    while sh < LANES:
        y = jnp.minimum(y, jnp.where(lane + sh < LANES, pltpu.roll(y, LANES - sh, 1), big))
        sh *= 2
    return y


def _work_items(counts, starts, ends, lane8, n_moe_tiles):
    shift = int(math.log2(MOE_TILE))
    first_tile = (starts.astype(I32) >> shift).astype(F32)
    last_tile = ((ends.astype(I32) - 1) >> shift).astype(F32)
    n_it = jnp.where(counts > 0.0, last_tile - first_tile + 1.0, 0.0)
    it_end = _lane_prefix_sum(n_it, lane8)
    it_start = it_end - n_it
    total = it_end[0:1, LANES - 1:LANES]
    sub = lax.broadcasted_iota(I32, (LANES, LANES), 0)
    rows = lambda x: jnp.broadcast_to(x[0:1, :], (LANES, LANES))
    used = counts > 0.0
    weight_slot = ((_lane_prefix_sum(jnp.where(used, 1.0, 0.0), lane8) - 1.0).astype(I32) & 1).astype(F32)
    next_used = _lane_suffix_min_exclusive(jnp.where(used, lane8.astype(F32), float(LANES)), lane8)
    per_expert = (starts, ends, first_tile, it_start, it_end, weight_slot, next_used)
    stacked = jnp.zeros((LANES, LANES), F32)
    for j, vec in enumerate(per_expert):
        stacked = jnp.where(sub == j, rows(vec), stacked)
    cols = stacked.T
    col = lambda j: cols[:, j:j + 1]
    sub_f = sub.astype(F32)
    w = lax.broadcasted_iota(I32, (LANES, LANES), 1).astype(F32)
    ex = jnp.sum(jnp.where((sub < N_EXPERTS) & (col(4) <= w), 1.0, 0.0), axis=0, keepdims=True)
    ex = jnp.minimum(ex, N_EXPERTS - 1.0)
    w1 = w[0:1, :]
    valid = w1 < total
    ex = jnp.where(valid, ex, jnp.max(jnp.where(valid, ex, 0.0), axis=-1, keepdims=True))
    onehot = sub_f == ex
    pick = lambda j: jnp.sum(jnp.where(onehot, col(j), 0.0), axis=0, keepdims=True)
    tile = jnp.where(valid, pick(2) + (w1 - pick(3)), n_moe_tiles - 1.0)
    lo = jnp.where(valid, jnp.maximum(pick(0), tile * MOE_TILE), 0.0)
    hi = jnp.where(valid, jnp.minimum(pick(1), (tile + 1.0) * MOE_TILE), 0.0)
    b8 = lambda x: jnp.broadcast_to(x, (SUBLANES, LANES))
    ex8, tile8 = b8(ex), b8(tile)
    first = jnp.where((lane8 == 0) | (tile8 != pltpu.roll(tile8, 1, 1)), 1.0, 0.0)
    newexp = jnp.where((lane8 == 0) | (ex8 != pltpu.roll(ex8, 1, 1)), 1.0, 0.0)
    sub8 = lax.broadcasted_iota(I32, (SUBLANES, LANES), 0)
    table = jnp.zeros((SUBLANES, LANES), F32)
    for j, vec in enumerate((ex8, tile8, b8(lo), b8(hi), first, newexp, b8(pick(5)), b8(pick(6)))):
        table = jnp.where(sub8 == j, vec, table)
    return table


def _plan_kernel(route_ref, pos_ref, items_ref, tri_ref, carry_ref, tot_ref, *, n_moe_tiles):
    p = pl.program_id(0)
    t = pl.program_id(1)
    tm = route_ref.shape[0]
    r = route_ref[...]
    lane = lax.broadcasted_iota(I32, (tm, LANES), 1)
    lane_f = lane.astype(F32)
    e0 = r[:, 0:1]
    e1 = r[:, 1:2] + N_EXPERTS
    m = jnp.where((lane_f == e0) | (lane_f == e1), 1.0, 0.0)
    colsum = jnp.sum(m, axis=0, keepdims=True)

    @pl.when((p == 0) & (t == 0))
    def _():
        carry_ref[...] = jnp.zeros_like(carry_ref)
        row = lax.broadcasted_iota(I32, (tm, tm), 0)
        col = lax.broadcasted_iota(I32, (tm, tm), 1)
        tri_ref[...] = jnp.where(row > col, 1.0, 0.0).astype(BF16)

    @pl.when(p == 0)
    def _():
        carry_ref[...] += colsum

    @pl.when((p == 0) & (t == pl.num_programs(1) - 1))
    def _():
        tot_ref[...] = carry_ref[...]
        carry_ref[...] = jnp.zeros_like(carry_ref)

    @pl.when(p == 1)
    def _():
        lane8 = lax.broadcasted_iota(I32, (SUBLANES, LANES), 1)
        tot = tot_ref[...]
        is_first = lane8 < N_EXPERTS
        tot0 = jnp.where(is_first, tot, 0.0)
        counts = jnp.where(is_first, tot + pltpu.roll(tot, LANES - N_EXPERTS, 1), 0.0)
        ends = _lane_prefix_sum(counts, lane8)
        starts = ends - counts
        base = jnp.where(is_first, starts, pltpu.roll(starts + tot0, N_EXPERTS, 1))
        before = _dot(tri_ref[...], m.astype(BF16)) + carry_ref[0:1, :]
        carry_ref[...] += colsum
        rows = before + base[0:1, :]
        pos0 = jnp.sum(jnp.where(lane_f == e0, rows, 0.0), axis=-1, keepdims=True)
        pos1 = jnp.sum(jnp.where(lane_f == e1, rows, 0.0), axis=-1, keepdims=True)
        both = jnp.where(lane == 0, pos0, jnp.where(lane == 1, pos1, 0.0))
        pos_ref[...] = both.T[0:SUBLANES, :].astype(I32)

        @pl.when(t == 0)
        def _():
            items_ref[...] = _work_items(counts, starts, ends, lane8, n_moe_tiles).astype(I32)


def _moe_plan(route):
    t_tok = route.shape[0]
    tm = PLAN_TILE
    n_moe_tiles = 2 * t_tok // MOE_TILE
    assert n_moe_tiles + N_EXPERTS - 1 <= LANES and 2 * N_EXPERTS <= LANES and MOE_TILE & (MOE_TILE - 1) == 0
    return pl.pallas_call(
        functools.partial(_plan_kernel, n_moe_tiles=n_moe_tiles),
        out_shape=(jax.ShapeDtypeStruct((SUBLANES, t_tok), I32), jax.ShapeDtypeStruct((SUBLANES, LANES), I32)),
        grid=(2, t_tok // tm),
        in_specs=[pl.BlockSpec((tm, LANES), lambda p, t: (t, 0))],
        out_specs=(pl.BlockSpec((SUBLANES, tm), lambda p, t: (0, t * p)),
                   pl.BlockSpec((SUBLANES, LANES), lambda p, t: (0, 0))),
        scratch_shapes=[pltpu.VMEM((tm, tm), BF16), pltpu.VMEM((SUBLANES, LANES), F32),
                        pltpu.VMEM((SUBLANES, LANES), F32)],
        compiler_params=_params(("arbitrary", "arbitrary")),
        name="moe_positions",
    )(route)


def _sc_workers():
    info = plsc.get_sparse_core_info()
    return info.num_cores, info.num_cores * info.num_subcores


def _sc_pipeline(n_chunks, loads, stores):
    def start(copies):
        for cp in copies:
            cp.start()

    def wait(copies):
        for cp in copies:
            cp.wait()

    start(loads(0, 0))
    for j in range(n_chunks):
        b = j % 2
        if j + 1 < n_chunks:
            if j >= 1:
                wait(stores(j - 1, 1 - b))
            start(loads(j + 1, 1 - b))
        wait(loads(j, b))
        start(stores(j, b))
    if n_chunks >= 2:
        wait(stores(n_chunks - 2, n_chunks % 2))
    wait(stores(n_chunks - 1, (n_chunks - 1) % 2))


def _sc_chunking(n_rows, d):
    n_cores, n_workers = _sc_workers()
    per_worker = n_rows // n_workers
    k = SC_BUFFER_BYTES // (d * 4)
    n_chunks = per_worker // k
    assert n_chunks * k * n_workers == n_rows and k <= SC_MAX_INDEX_CHUNK
    return n_cores, n_workers, per_worker, k, n_chunks


def _sc_gather_rows(table, idx):
    n, d = idx.shape[0], table.shape[1]
    n_cores, n_workers, per_worker, k, n_chunks = _sc_chunking(n, d)

    def body(table_hbm, idx_hbm, out_hbm, idx_v, rows_v, gsem, osem):
        wid = lax.axis_index("s") * n_cores + lax.axis_index("c")
        base = wid * per_worker
        pltpu.sync_copy(idx_hbm.at[wid], idx_v)
        _sc_pipeline(
            n_chunks,
            lambda j, b: [pltpu.make_async_copy(table_hbm.at[idx_v.at[j]], rows_v.at[b], gsem.at[b])],
            lambda j, b: [pltpu.make_async_copy(rows_v.at[b], out_hbm.at[pl.ds(base + j * k, k)], osem.at[b])])

    return pl.kernel(
        body,
        out_type=jax.ShapeDtypeStruct((n, d), table.dtype),
        mesh=plsc.VectorSubcoreMesh(core_axis_name="c", subcore_axis_name="s"),
        scratch_types=[pltpu.VMEM((n_chunks, k), I32), pltpu.VMEM((2, k, d), table.dtype),
                       pltpu.SemaphoreType.DMA((2,)), pltpu.SemaphoreType.DMA((2,))],
        name="sc_row_gather",
    )(table, idx.reshape(n_workers, n_chunks, k))


def _sc_scatter_rows(src, idx):
    n_lists, n_src = idx.shape
    d = src.shape[1]
    n_cores, n_workers, per_worker, k, n_chunks = _sc_chunking(n_src, d)

    def body(src_hbm, idx_hbm, out_hbm, idx_v, rows_v, gsem, osem):
        wid = lax.axis_index("s") * n_cores + lax.axis_index("c")
        base = wid * per_worker
        pltpu.sync_copy(idx_hbm.at[wid], idx_v)
        _sc_pipeline(
            n_chunks,
            lambda j, b: [pltpu.make_async_copy(src_hbm.at[pl.ds(base + j * k, k)], rows_v.at[b], gsem.at[b])],
            lambda j, b: [pltpu.make_async_copy(rows_v.at[b], out_hbm.at[idx_v.at[c, j]], osem.at[b, c])
                          for c in range(n_lists)])

    return pl.kernel(
        body,
        out_type=jax.ShapeDtypeStruct((n_lists * n_src, d), src.dtype),
        mesh=plsc.VectorSubcoreMesh(core_axis_name="c", subcore_axis_name="s"),
        scratch_types=[pltpu.VMEM((n_lists, n_chunks, k), I32), pltpu.VMEM((2, k, d), src.dtype),
                       pltpu.SemaphoreType.DMA((2,)), pltpu.SemaphoreType.DMA((2, n_lists))],
        name="sc_row_scatter",
    )(src, idx.reshape(n_lists, n_workers, n_chunks, k).transpose(1, 0, 2, 3))


IT_EXPERT, IT_TILE, IT_LO, IT_HI, IT_FIRST, IT_NEWEXP, IT_SLOT, IT_NEXT = range(8)


def _moe_kernel(it_ref, xs_ref, wg_hbm, wu_hbm, wd_hbm, o_ref, wg_f, wu_f, wd_f, wgu_s, wd_s, sem, *, layer):
    w = pl.program_id(0)
    f = wg_f.shape[2]

    def weight_copies(expert, slot):
        return [pltpu.make_async_copy(src.at[layer, expert], dst.at[slot], sem.at[slot, j])
                for j, (src, dst) in enumerate(((wg_hbm, wg_f), (wu_hbm, wu_f), (wd_hbm, wd_f)))]

    @pl.when(it_ref[IT_NEWEXP, w] == 1)
    def _():
        expert = it_ref[IT_EXPERT, w]
        slot = it_ref[IT_SLOT, w]
        nxt = it_ref[IT_NEXT, w]

        @pl.when(w == 0)
        def _():
            for cp in weight_copies(expert, slot):
                cp.start()

        for cp in weight_copies(expert, slot):
            cp.wait()

        @pl.when(nxt < N_EXPERTS)
        def _():
            for cp in weight_copies(nxt, 1 - slot):
                cp.start()

        wgu_s[:, :f] = wg_f[slot].astype(BF16)
        wgu_s[:, f:] = wu_f[slot].astype(BF16)
        wd_s[...] = wd_f[slot].astype(BF16)

    tm = xs_ref.shape[0]
    x = _unpack_bf16_pairs(xs_ref[...]).astype(BF16)
    au = _dot(x, wgu_s[...])
    act = (_silu(au[:, :f]) * au[:, f:]).astype(BF16)
    y = _pack_bf16_pairs(_dot(act, wd_s[...]))
    row = it_ref[IT_TILE, w] * tm + lax.broadcasted_iota(I32, (tm, 1), 0)
    mine = (row >= it_ref[IT_LO, w]) & (row < it_ref[IT_HI, w])

    @pl.when(it_ref[IT_FIRST, w] == 1)
    def _():
        o_ref[...] = jnp.where(mine, y, jnp.zeros_like(y))

    @pl.when(it_ref[IT_FIRST, w] == 0)
    def _():
        o_ref[...] = jnp.where(mine, y, o_ref[...])


def _moe_experts(items, xs, w_gate, w_up, w_down, layer):
    n_rows, dp = xs.shape
    _, _, d, f = w_gate.shape
    tm = MOE_TILE
    rmap = lambda w, it: (it[IT_TILE, w], 0)
    hbm = pl.BlockSpec(memory_space=pl.ANY)
    return pl.pallas_call(
        functools.partial(_moe_kernel, layer=layer),
        out_shape=jax.ShapeDtypeStruct((n_rows, d // 2), jnp.uint32),
        grid_spec=pltpu.PrefetchScalarGridSpec(
            num_scalar_prefetch=1,
            grid=(n_rows // tm + N_EXPERTS - 1,),
            in_specs=[pl.BlockSpec((tm, dp), rmap), hbm, hbm, hbm],
            out_specs=pl.BlockSpec((tm, d // 2), rmap),
            scratch_shapes=[
                pltpu.VMEM((2, d, f), F32), pltpu.VMEM((2, d, f), F32), pltpu.VMEM((2, f, d), F32),
                pltpu.VMEM((d, 2 * f), BF16), pltpu.VMEM((f, d), BF16),
                pltpu.SemaphoreType.DMA((2, 3)),
            ],
        ),
        compiler_params=_params(("arbitrary",)),
        name="moe_grouped_mlp",
    )(items, xs, w_gate, w_up, w_down)


def _moe_combine_kernel(x_ref, y0_ref, y1_ref, route_ref, mod_ref, gb_ref, *o_refs, n_prompt_tiles):
    r = route_ref[...]
    moe = r[:, 2:3] * _unpack_bf16_pairs(y0_ref[...]) + r[:, 3:4] * _unpack_bf16_pairs(y1_ref[...])
    x2 = _deepnorm(x_ref[...], mod_ref[5:6, :], moe) * gb_ref[0:1, :] + gb_ref[1:2, :]
    if len(o_refs) == 1:
        o_refs[0][...] = x2
    else:
        @pl.when(pl.program_id(0) < n_prompt_tiles)
        def _():
            o_refs[0][...] = x2

        @pl.when(pl.program_id(0) >= n_prompt_tiles)
        def _():
            o_refs[1][...] = x2


def _moe_combine(x1, ys, route, mod, gb, n_prompt, dec_seq, split):
    t_tok, d = x1.shape
    tl = _Tiles(n_prompt, dec_seq, TOKEN_TILE)
    tm = tl.tm
    nt = t_tok // tm
    if split:
        out_shape = (jax.ShapeDtypeStruct((n_prompt, d), F32), jax.ShapeDtypeStruct((t_tok - n_prompt, d), F32))
        out_specs = (pl.BlockSpec((tm, d), tl.prompt_part), pl.BlockSpec((tm, d), tl.latent_part))
    else:
        out_shape = jax.ShapeDtypeStruct((t_tok, d), F32)
        out_specs = pl.BlockSpec((tm, d), tl.row)
    return pl.pallas_call(
        functools.partial(_moe_combine_kernel, n_prompt_tiles=tl.npt),
        out_shape=out_shape,
        grid=(nt,),
        in_specs=[
            pl.BlockSpec((tm, d), tl.row),
            pl.BlockSpec((tm, ys.shape[1]), tl.row),
            pl.BlockSpec((tm, ys.shape[1]), lambda t: (t + nt, 0)),
            pl.BlockSpec((tm, LANES), tl.row),
            pl.BlockSpec((None, N_MOD, d), tl.cond),
            pl.BlockSpec(gb.shape, lambda t: (0, 0)),
        ],
        out_specs=out_specs,
        compiler_params=_params(("arbitrary",)),
        name="moe_combine_postnorm",
    )(x1, ys, ys, route, mod, gb)


def _rope_tables(n_lat):
    nf = HEAD_DIM // 4
    s = np.arange(n_lat)
    lane = np.arange(LANES)
    inv = ROPE_BASE ** (-(lane % nf).astype(np.float64) / nf)
    use_col = (lane % HEAD_DIM) >= HEAD_DIM // 2
    p = np.where(use_col[None, :], (s % GRID_W)[:, None], (s // GRID_W)[:, None]).astype(np.float64)
    ang = p * inv[None, :]
    sign = np.where((lane % (2 * nf)) < nf, -1.0, 1.0)
    return jnp.asarray(np.cos(ang), F32), jnp.asarray(np.sin(ang) * sign[None, :], F32)


def _dft_tables(n):
    k = np.arange(n)
    ang = 2.0 * np.pi * ((k[:, None] * k[None, :]) % n).astype(np.float64) / n
    return np.cos(ang), np.sin(ang)


def kernel(x_prompt, x_sample, cache_k, cache_v, c, c_ctx, w_ada, b_ada, ln_gain, ln_bias, w_qkv, w_attn_out,
           lambda_q1, lambda_k1, lambda_q2, lambda_k2, subln_gain, w_fourier_out, w_router_group,
           w_router_expert, w_expert_gate, w_expert_up, w_expert_down):
    bp, sp, d = x_prompt.shape
    bs, n_lat, _ = x_sample.shape
    n_ctx = cache_k.shape[2]
    n_prompt = bp * sp
    t_tok = n_prompt + bs * n_lat
    assert d == D_MODEL and n_prompt % n_lat == 0 and n_lat % TOKEN_TILE == 0 and sp == QKV_TILE

    cond = jnp.concatenate([c_ctx[None, :], c, jnp.zeros((SUBLANES - 1 - bs, d), F32)], axis=0)
    mods = _ada_all(cond, w_ada, b_ada).reshape(DEPTH, SUBLANES, N_MOD, d)

    cos, sin = _rope_tables(n_lat)
    cc, sc = _dft_tables(FOURIER_GROUP_DIM)
    dcs = jnp.asarray(np.concatenate([cc, sc], axis=1), BF16)
    seq_tabs = {s: tuple(jnp.asarray(m, BF16) for m in _dft_tables(s)) for s in (sp, n_lat)}
    attn_layers = [i for i in range(DEPTH) if i % 2 == 0]

    xs = (x_prompt.reshape(n_prompt, d), x_sample.reshape(bs * n_lat, d))
    prev_kv = []
    new_k = new_v = None
    for i in range(DEPTH):
        mod = mods[i]
        if i % 2 == 0:
            a = i // 2
            lam_init = 0.8 - 0.6 * math.exp(-0.3 * i)
            last_attn = i == attn_layers[-1]
            q, k, v, kf, vf = _qkv(xs, mod, w_qkv[a].astype(BF16), cos, sin, prev_kv, last_attn, n_prompt, n_lat,
                                   t_tok)
            if last_attn:
                new_k, new_v = kf, vf
            else:
                prev_kv.append((kf, vf))
            lam_vecs = jnp.stack([lambda_q1[a], lambda_k1[a], lambda_q2[a], lambda_k2[a]], axis=0)
            gain = subln_gain[a][None, :]
            kc = cache_k[:, a].reshape(bs * n_ctx, d).astype(BF16)
            vc = cache_v[:, a].reshape(bs * n_ctx, d).astype(BF16)
            mixed_p = _attn_prompt(lam_vecs, gain, q, k, v, bp, sp, lam_init)
            mixed_s = _attn_sample(lam_vecs, gain, q, k, v, kc, vc, bs, n_lat, n_ctx, n_prompt, lam_init)
            w_mix = w_attn_out[a]
        else:
            xc, xsn = _chan_dft(xs[0], mod, dcs, n_prompt, n_lat)
            mixed_p = _seq_dft(*seq_tabs[sp], xc, xsn, bp, sp, 0)
            mixed_s = _seq_dft(*seq_tabs[n_lat], xc, xsn, bs, n_lat, n_prompt)
            w_mix = w_fourier_out[i // 2]
        gb0 = jnp.stack([ln_gain[i, 0], ln_bias[i, 0]], axis=0)
        gb1 = jnp.stack([ln_gain[i, 1], ln_bias[i, 1]], axis=0)
        wr = jnp.concatenate([w_router_group[i], w_router_expert[i],
                              jnp.zeros((d, LANES - N_EXPERT_GROUPS - N_EXPERTS), F32)], axis=1)
        wr_hi = wr.astype(BF16)
        wr_lo = (wr - wr_hi.astype(F32)).astype(BF16)
        x1, hp, route = _mix_out(mixed_p, mixed_s, w_mix.astype(BF16), xs, mod, gb0,
                                 jnp.stack([wr_hi, wr_lo], axis=0), n_prompt, n_lat, t_tok)
        pos8, items = _moe_plan(route)
        pos = pos8[0:2]
        xs_sorted = _sc_scatter_rows(hp, pos)
        ys = _moe_experts(items, xs_sorted, w_expert_gate, w_expert_up, w_expert_down, i)
        yg = _sc_gather_rows(ys, pos.reshape(-1))
        last = i == DEPTH - 1
        out = _moe_combine(x1, yg, route, mod, gb1, n_prompt, n_lat, split=last)
        xs = out if last else (out,)

    y_prompt = xs[0].reshape(bp, sp, d)
    y_sample = xs[1].reshape(bs, n_lat, d)
    return (y_prompt, y_sample, new_k.reshape(bp, len(attn_layers), sp, N_HEADS, 2 * HEAD_DIM),
            new_v.reshape(bp, len(attn_layers), sp, N_HEADS, V_DIM))
```

```python
import functools
import math

import numpy as np
import jax
import jax.numpy as jnp
from jax import lax
from jax.experimental import pallas as pl
from jax.experimental.pallas import tpu as pltpu
from jax.experimental.pallas import tpu_sc as plsc

F32 = jnp.float32
BF16 = jnp.bfloat16
I32 = jnp.int32

D_MODEL = 1024
DEPTH = 4
GRID_W = 64
N_HEADS = 8
HEAD_DIM = 64
V_DIM = 2 * HEAD_DIM
ROPE_BASE = 10000.0
N_FOURIER_GROUPS = 8
FOURIER_GROUP_DIM = D_MODEL // N_FOURIER_GROUPS
N_EXPERT_GROUPS = 4
EXPERTS_PER_GROUP = 8
N_EXPERTS = N_EXPERT_GROUPS * EXPERTS_PER_GROUP
D_EXPERT = 256
N_MOD = 6
LN_EPS = 1e-5
DEEPNORM_ALPHA = (2.0 * DEPTH) ** 0.25
Q_SCALE = math.log2(math.e) * HEAD_DIM ** -0.5

LANES = 128
SUBLANES = 8
TOKEN_TILE = 512
CHAIN_ROWS = 256
QKV_TILE = 256
Q_TILE = 512
ATTN_ROW_CHUNK = 128
MOE_TILE = 512
PLAN_TILE = 1024
SC_BUFFER_BYTES = 128 * 1024
SC_MAX_INDEX_CHUNK = 128
VMEM_LIMIT = 48 * 1024 * 1024


def _params(semantics):
    return pltpu.CompilerParams(dimension_semantics=semantics, vmem_limit_bytes=VMEM_LIMIT)


def _layernorm(x, eps=LN_EPS):
    mu = jnp.mean(x, axis=-1, keepdims=True)
    xc = x - mu
    var = jnp.mean(xc * xc, axis=-1, keepdims=True)
    return xc * lax.rsqrt(var + eps)


def _deepnorm(x, branch_gate, branch):
    return _layernorm(x + (branch_gate * (1.0 / DEEPNORM_ALPHA)) * branch, LN_EPS / DEEPNORM_ALPHA ** 2)


def _silu(a):
    return a / (1.0 + jnp.exp(-a))


def _dot(a, b):
    return jnp.dot(a, b, preferred_element_type=F32)


class _Tiles:
    def __init__(self, n_prompt, dec_seq, tm):
        self.tm = tm
        self.npt = n_prompt // tm
        self.tps = dec_seq // tm

    def row(self, t):
        return (t, 0)

    def cond(self, t):
        return (jnp.where(t < self.npt, 0, (t - self.npt) // self.tps + 1), 0, 0)

    def latent_pos(self, t):
        return (jnp.maximum(t - self.npt, 0) % self.tps, 0)

    def prompt_part(self, t):
        return (jnp.minimum(t, self.npt - 1), 0)

    def latent_part(self, t):
        return (jnp.maximum(t - self.npt, 0), 0)

    def source_specs(self, arrays, d, tile_of=lambda step: step):
        maps = (self.row,) if len(arrays) == 1 else (self.prompt_part, self.latent_part)
        return [pl.BlockSpec((self.tm, d), lambda s, m=m: m(tile_of(s))) for m in maps]


def _read_rows(refs, is_prompt_tile, rows=slice(None)):
    if len(refs) == 1:
        return refs[0][rows, :]
    return jnp.where(is_prompt_tile, refs[0][rows, :], refs[1][rows, :])


def _ada_kernel(cond_ref, w_ref, b_ref, o_ref):
    a = _silu(cond_ref[...])
    o_ref[...] = _dot(a.astype(BF16), w_ref[...].astype(BF16)) + b_ref[...]


def _ada_all(cond, w_ada, b_ada):
    depth, d, n = w_ada.shape
    tn = n // 4
    return pl.pallas_call(
        _ada_kernel,
        out_shape=jax.ShapeDtypeStruct((depth, cond.shape[0], n), F32),
        grid=(depth, n // tn),
        in_specs=[
            pl.BlockSpec(cond.shape, lambda l, j: (0, 0)),
            pl.BlockSpec((None, d, tn), lambda l, j: (l, 0, j)),
            pl.BlockSpec((None, 1, tn), lambda l, j: (l, 0, j)),
        ],
        out_specs=pl.BlockSpec((None, cond.shape[0], tn), lambda l, j: (l, 0, j)),
        compiler_params=_params(("arbitrary", "arbitrary")),
        name="ada",
    )(cond, w_ada, b_ada.reshape(depth, 1, n))


def _rope(x, cos, sin_signed, first_half):
    outs = []
    for c in range(x.shape[1] // LANES):
        xc = x[:, c * LANES:(c + 1) * LANES]
        partner = jnp.where(first_half, pltpu.roll(xc, LANES - 16, 1), pltpu.roll(xc, 16, 1))
        outs.append(xc * cos + partner * sin_signed)
    return jnp.concatenate(outs, axis=1)


def _store_heads(cache_ref, slot, rows):
    cache_ref[slot] = pltpu.einshape("s(hd)->shd", rows, h=N_HEADS)


def _qkv_kernel(*refs, n_prompt_tiles, n_x, n_prev):
    x_refs, (mod_ref, w_ref, cos_ref, sin_ref) = refs[:n_x], refs[n_x:n_x + 4]
    prev = refs[n_x + 4:n_x + 4 + 2 * max(n_prev, 0)]
    q_ref, k_ref, v_ref, ko_ref, vo_ref = refs[n_x + 4 + 2 * max(n_prev, 0):]
    t = pl.program_id(0)
    h = (_layernorm(_read_rows(x_refs, t < n_prompt_tiles)) * (1.0 + mod_ref[1:2, :])
         + mod_ref[0:1, :]).astype(BF16)
    d = w_ref.shape[0]
    project = lambda j: _dot(h, w_ref[:, j * d:(j + 1) * d])

    @pl.when(t < n_prompt_tiles)
    def _():
        q_ref[...] = (project(0) * Q_SCALE).astype(BF16)
        k = project(1)
        k_ref[...] = k.astype(BF16)
        v = project(2)
        v_ref[...] = v.astype(BF16)
        if n_prev < 0:
            ko_ref[...] = k
            vo_ref[...] = v
        else:
            for a in range(n_prev):
                _store_heads(ko_ref, a, prev[2 * a][...])
                _store_heads(vo_ref, a, prev[2 * a + 1][...])
            _store_heads(ko_ref, n_prev, k)
            _store_heads(vo_ref, n_prev, v)

    @pl.when(t >= n_prompt_tiles)
    def _():
        lane = lax.broadcasted_iota(I32, (h.shape[0], LANES), 1)
        first_half = (lane % 32) < 16
        cos = cos_ref[...]
        sin = sin_ref[...]
        q_ref[...] = _rope(project(0) * Q_SCALE, cos, sin, first_half).astype(BF16)
        k_ref[...] = _rope(project(1), cos, sin, first_half).astype(BF16)
        v_ref[...] = project(2).astype(BF16)


def _qkv(xs, mod, w_qkv, cos, sin, prev_kv, finish_cache, n_prompt, dec_seq, t_tok):
    d = w_qkv.shape[0]
    tl = _Tiles(n_prompt, dec_seq, QKV_TILE)
    tm = tl.tm
    n_prev = len(prev_kv) if finish_cache else -1
    if finish_cache:
        n_slots = n_prev + 1
        kv_shape = jax.ShapeDtypeStruct((n_prompt // tm, n_slots, tm, N_HEADS, V_DIM), F32)
        kv_spec = pl.BlockSpec((None, n_slots, tm, N_HEADS, V_DIM), lambda t: (tl.prompt_part(t)[0], 0, 0, 0, 0))
    else:
        kv_shape = jax.ShapeDtypeStruct((n_prompt, d), F32)
        kv_spec = pl.BlockSpec((tm, d), tl.prompt_part)
    prev_flat = [a for kv in prev_kv for a in kv] if finish_cache else []
    return pl.pallas_call(
        functools.partial(_qkv_kernel, n_prompt_tiles=tl.npt, n_x=len(xs), n_prev=n_prev),
        out_shape=(
            jax.ShapeDtypeStruct((t_tok, d), BF16),
            jax.ShapeDtypeStruct((t_tok, d), BF16),
            jax.ShapeDtypeStruct((t_tok, d), BF16),
            kv_shape, kv_shape,
        ),
        grid=(t_tok // tm,),
        in_specs=tl.source_specs(xs, d) + [
            pl.BlockSpec((None, N_MOD, d), tl.cond),
            pl.BlockSpec(w_qkv.shape, lambda t: (0, 0)),
            pl.BlockSpec((tm, LANES), tl.latent_pos),
            pl.BlockSpec((tm, LANES), tl.latent_pos),
        ] + [pl.BlockSpec((tm, d), tl.prompt_part) for _ in prev_flat],
        out_specs=(
            pl.BlockSpec((tm, d), tl.row),
            pl.BlockSpec((tm, d), tl.row),
            pl.BlockSpec((tm, d), tl.row),
            kv_spec, kv_spec,
        ),
        compiler_params=_params(("arbitrary",)),
        name="ln_qkv_rope",
    )(*xs, mod, w_qkv, cos, sin, *prev_flat)


def _row_chains(n_rows):
    return [slice(r, r + CHAIN_ROWS) for r in range(0, n_rows, CHAIN_ROWS)]


def _chan_dft_kernel(x_ref, mod_ref, dcs_ref, xc_ref, xs_ref):
    g = FOURIER_GROUP_DIM
    for rows in _row_chains(x_ref.shape[0]):
        h = (_layernorm(x_ref[rows, :]) * (1.0 + mod_ref[1:2, :]) + mod_ref[0:1, :]).astype(BF16)
        for i in range(N_FOURIER_GROUPS):
            r = _dot(h[:, i * g:(i + 1) * g], dcs_ref[...])
            xc_ref[rows, i * g:(i + 1) * g] = r[:, :g].astype(BF16)
            xs_ref[rows, i * g:(i + 1) * g] = r[:, g:].astype(BF16)


def _chan_dft(x, mod, dcs, n_prompt, dec_seq):
    t_tok, d = x.shape
    tl = _Tiles(n_prompt, dec_seq, TOKEN_TILE)
    return pl.pallas_call(
        _chan_dft_kernel,
        out_shape=(jax.ShapeDtypeStruct((t_tok, d), BF16), jax.ShapeDtypeStruct((t_tok, d), BF16)),
        grid=(t_tok // tl.tm,),
        in_specs=[
            pl.BlockSpec((tl.tm, d), tl.row),
            pl.BlockSpec((None, N_MOD, d), tl.cond),
            pl.BlockSpec(dcs.shape, lambda t: (0, 0)),
        ],
        out_specs=(pl.BlockSpec((tl.tm, d), tl.row), pl.BlockSpec((tl.tm, d), tl.row)),
        compiler_params=_params(("arbitrary",)),
        name="ln_chan_dft",
    )(x, mod, dcs)


def _seq_dft_kernel(cs_ref, ss_ref, xc_ref, xs_ref, o_ref, *, norm):
    f = _dot(cs_ref[...], xc_ref[...]) - _dot(ss_ref[...], xs_ref[...])
    o_ref[...] = (f * norm).astype(BF16)


def _seq_dft(cs, ss, xc, xs, batch, seq, row_offset):
    d = xc.shape[1]
    tm = min(QKV_TILE, seq)
    spt = seq // tm
    off_seq = row_offset // seq
    return pl.pallas_call(
        functools.partial(_seq_dft_kernel, norm=1.0 / math.sqrt(seq * FOURIER_GROUP_DIM)),
        out_shape=jax.ShapeDtypeStruct((batch * seq, d), BF16),
        grid=(batch, spt),
        in_specs=[
            pl.BlockSpec((tm, seq), lambda b, i: (i, 0)),
            pl.BlockSpec((tm, seq), lambda b, i: (i, 0)),
            pl.BlockSpec((seq, d), lambda b, i: (off_seq + b, 0)),
            pl.BlockSpec((seq, d), lambda b, i: (off_seq + b, 0)),
        ],
        out_specs=pl.BlockSpec((tm, d), lambda b, i: (b * spt + i, 0)),
        compiler_params=_params(("arbitrary", "arbitrary")),
        name=f"seq_dft_{seq}",
    )(cs, ss, xc, xs)


def _diff_lambda(lam_ref, lam_init):
    lv = lam_ref[...]
    return (jnp.exp(jnp.sum(lv[0:1] * lv[1:2], axis=-1, keepdims=True))
            - jnp.exp(jnp.sum(lv[2:3] * lv[3:4], axis=-1, keepdims=True)) + lam_init)


def _diff_attn_head(q, k, v_ext, lam, gain, lam_init):
    tq = q.shape[0]
    lane = lax.broadcasted_iota(I32, q.shape, 1)
    zero = jnp.zeros_like(q)
    qq = jnp.concatenate([jnp.where(lane < HEAD_DIM, q, zero), jnp.where(lane >= HEAD_DIM, q, zero)], axis=0)
    parts = []
    for r in range(0, 2 * tq, ATTN_ROW_CHUNK):
        s = lax.dot_general(qq[r:r + ATTN_ROW_CHUNK], k, (((1,), (1,)), ((), ())), preferred_element_type=F32)
        e = jnp.exp2(s - jnp.max(s, axis=-1, keepdims=True)).astype(BF16)
        parts.append(_dot(e, v_ext))
    oe = jnp.concatenate(parts, axis=0)
    o = oe[:, :V_DIM] / oe[:, V_DIM:]
    o = o[:tq] - lam * o[tq:]
    o = o * lax.rsqrt(jnp.mean(o * o, axis=-1, keepdims=True) + LN_EPS)
    return o * gain * (1.0 - lam_init)


def _attn_prompt_kernel(lam_ref, gain_ref, q_ref, k_ref, v_ref, o_ref, *, lam_init):
    lam = _diff_lambda(lam_ref, lam_init)
    gain = gain_ref[...]
    ones = jnp.ones((k_ref.shape[0], V_DIM), BF16)
    for hd in range(N_HEADS):
        cols = slice(hd * V_DIM, (hd + 1) * V_DIM)
        v_ext = jnp.concatenate([v_ref[:, cols], ones], axis=1)
        o_ref[:, cols] = _diff_attn_head(q_ref[:, cols], k_ref[:, cols], v_ext, lam, gain, lam_init).astype(BF16)


def _attn_latent_kernel(lam_ref, gain_ref, q_ref, k_ref, v_ref, kc_ref, vc_ref, o_ref, kall_ref, vext_ref, *,
                        lam_init):
    n_new = k_ref.shape[0]

    @pl.when(pl.program_id(2) == 0)
    def _():
        kall_ref[:n_new, :] = k_ref[...]
        kall_ref[n_new:, :] = kc_ref[...]
        vext_ref[:n_new, :V_DIM] = v_ref[...]
        vext_ref[n_new:, :V_DIM] = vc_ref[...]
        vext_ref[:, V_DIM:] = jnp.ones((vext_ref.shape[0], V_DIM), BF16)

    o = _diff_attn_head(q_ref[...], kall_ref[...], vext_ref[...], _diff_lambda(lam_ref, lam_init), gain_ref[...],
                        lam_init)
    o_ref[...] = o.astype(BF16)


def _attn_prompt(lam_vecs, gain, q, k, v, batch, seq, lam_init):
    d = q.shape[1]
    blk = pl.BlockSpec((seq, d), lambda b: (b, 0))
    return pl.pallas_call(
        functools.partial(_attn_prompt_kernel, lam_init=lam_init),
        out_shape=jax.ShapeDtypeStruct((batch * seq, d), BF16),
        grid=(batch,),
        in_specs=[
            pl.BlockSpec(lam_vecs.shape, lambda b: (0, 0)),
            pl.BlockSpec(gain.shape, lambda b: (0, 0)),
            blk, blk, blk,
        ],
        out_specs=blk,
        compiler_params=_params(("arbitrary",)),
        name="diff_attn_ctx",
    )(lam_vecs, gain, q, k, v)


def _attn_sample(lam_vecs, gain, q, k, v, kc, vc, batch, seq, n_ctx, row_offset, lam_init):
    d = q.shape[1]
    tq = Q_TILE
    qpt = seq // tq
    off_seq = row_offset // seq
    off_tile = row_offset // tq
    qmap = lambda b, h, i: (off_tile + b * qpt + i, h)
    kmap = lambda b, h, i: (off_seq + b, h)
    cmap = lambda b, h, i: (b, h)
    return pl.pallas_call(
        functools.partial(_attn_latent_kernel, lam_init=lam_init),
        out_shape=jax.ShapeDtypeStruct((batch * seq, d), BF16),
        grid=(batch, N_HEADS, qpt),
        in_specs=[
            pl.BlockSpec(lam_vecs.shape, lambda b, h, i: (0, 0)),
            pl.BlockSpec(gain.shape, lambda b, h, i: (0, 0)),
            pl.BlockSpec((tq, V_DIM), qmap),
            pl.BlockSpec((seq, V_DIM), kmap),
            pl.BlockSpec((seq, V_DIM), kmap),
            pl.BlockSpec((n_ctx, V_DIM), cmap),
            pl.BlockSpec((n_ctx, V_DIM), cmap),
        ],
        out_specs=pl.BlockSpec((tq, V_DIM), lambda b, h, i: (b * qpt + i, h)),
        scratch_shapes=[pltpu.VMEM((seq + n_ctx, V_DIM), BF16), pltpu.VMEM((seq + n_ctx, 2 * V_DIM), BF16)],
        compiler_params=_params(("arbitrary", "arbitrary", "arbitrary")),
        name="diff_attn_latent",
    )(lam_vecs, gain, q, k, v, kc, vc)


def _route(lg):
    lane = lax.broadcasted_iota(I32, lg.shape, 1)
    lane_f = lane.astype(F32)
    neg = jnp.float32(-jnp.inf)
    big = jnp.float32(LANES)
    gl = jnp.where(lane < N_EXPERT_GROUPS, lg, neg)
    gmax = jnp.max(gl, axis=-1, keepdims=True)
    g_prob = 1.0 / jnp.sum(jnp.exp(gl - gmax), axis=-1, keepdims=True)
    g_idx = jnp.min(jnp.where(gl == gmax, lane_f, big), axis=-1, keepdims=True)
    lo = N_EXPERT_GROUPS + EXPERTS_PER_GROUP * g_idx
    el = jnp.where((lane_f >= lo) & (lane_f < lo + EXPERTS_PER_GROUP), lg, neg)
    m1 = jnp.max(el, axis=-1, keepdims=True)
    i1 = jnp.min(jnp.where(el == m1, lane_f, big), axis=-1, keepdims=True)
    el2 = jnp.where(lane_f == i1, neg, el)
    m2 = jnp.max(el2, axis=-1, keepdims=True)
    i2 = jnp.min(jnp.where(el2 == m2, lane_f, big), axis=-1, keepdims=True)
    t = jnp.exp(m2 - m1)
    w1 = g_prob / (1.0 + t)
    w2 = g_prob * t / (1.0 + t)
    out = jnp.where(lane == 0, i1 - N_EXPERT_GROUPS, 0.0)
    out = jnp.where(lane == 1, i2 - N_EXPERT_GROUPS, out)
    out = jnp.where(lane == 2, w1, out)
    out = jnp.where(lane == 3, w2, out)
    return out


def _pack_bf16_pairs(h):
    n = h.shape[1] // 2
    bits = lax.bitcast_convert_type(h.astype(BF16).astype(F32), jnp.uint32)
    return (bits[:, :n] >> 16) | bits[:, n:]


def _unpack_bf16_pairs(p):
    lo = lax.bitcast_convert_type(p << 16, F32)
    hi = lax.bitcast_convert_type(p & jnp.uint32(0xFFFF0000), F32)
    return jnp.concatenate([lo, hi], axis=1)


def _mix_out_kernel(*refs, n_prompt_tiles, n_x):
    ap_ref, as_ref, w_ref = refs[:3]
    x_refs = refs[3:3 + n_x]
    mod_ref, gb_ref, wr_ref, x1_ref, hp_ref, route_ref = refs[3 + n_x:]
    is_prompt = pl.program_id(0) < n_prompt_tiles
    for rows in _row_chains(x1_ref.shape[0]):
        a = jnp.where(is_prompt, ap_ref[rows, :], as_ref[rows, :])
        out = _dot(a, w_ref[...])
        x = _read_rows(x_refs, is_prompt, rows)
        x1 = _deepnorm(x, mod_ref[2:3, :], out) * gb_ref[0:1, :] + gb_ref[1:2, :]
        x1_ref[rows, :] = x1
        h2 = _layernorm(x1) * (1.0 + mod_ref[4:5, :]) + mod_ref[3:4, :]
        hp_ref[rows, :] = _pack_bf16_pairs(h2)
        hi = h2.astype(BF16)
        lo = (h2 - hi.astype(F32)).astype(BF16)
        logits = _dot(hi, wr_ref[0]) + _dot(lo, wr_ref[0]) + _dot(hi, wr_ref[1])
        route_ref[rows, :] = _route(logits)


def _mix_out(a_prompt, a_sample, w, xs, mod, gb, wr, n_prompt, dec_seq, t_tok):
    d = w.shape[1]
    tl = _Tiles(n_prompt, dec_seq, TOKEN_TILE)
    tm = tl.tm
    return pl.pallas_call(
        functools.partial(_mix_out_kernel, n_prompt_tiles=tl.npt, n_x=len(xs)),
        out_shape=(
            jax.ShapeDtypeStruct((t_tok, d), F32),
            jax.ShapeDtypeStruct((t_tok, d // 2), jnp.uint32),
            jax.ShapeDtypeStruct((t_tok, LANES), F32),
        ),
        grid=(t_tok // tm,),
        in_specs=[
            pl.BlockSpec((tm, d), tl.prompt_part),
            pl.BlockSpec((tm, d), tl.latent_part),
            pl.BlockSpec(w.shape, lambda t: (0, 0)),
        ] + tl.source_specs(xs, d) + [
            pl.BlockSpec((None, N_MOD, d), tl.cond),
            pl.BlockSpec(gb.shape, lambda t: (0, 0)),
            pl.BlockSpec(wr.shape, lambda t: (0, 0, 0)),
        ],
        out_specs=(
            pl.BlockSpec((tm, d), tl.row),
            pl.BlockSpec((tm, d // 2), tl.row),
            pl.BlockSpec((tm, LANES), tl.row),
        ),
        compiler_params=_params(("arbitrary",)),
        name="mix_out_postnorm_router",
    )(a_prompt, a_sample, w, *xs, mod, gb, wr)


def _lane_prefix_sum(x, lane):
    sh = 1
    while sh < LANES:
        x = x + jnp.where(lane >= sh, pltpu.roll(x, sh, 1), 0.0)
        sh *= 2
    return x


def _lane_suffix_min_exclusive(x, lane):
    big = float(LANES)
    y = jnp.where(lane + 1 < LANES, pltpu.roll(x, LANES - 1, 1), big)
    sh = 1
    while sh < LANES:
        y = jnp.minimum(y, jnp.where(lane + sh < LANES, pltpu.roll(y, LANES - sh, 1), big))
        sh *= 2
    return y


def _work_items(counts, starts, ends, lane8, n_moe_tiles):
    shift = int(math.log2(MOE_TILE))
    first_tile = (starts.astype(I32) >> shift).astype(F32)
    last_tile = ((ends.astype(I32) - 1) >> shift).astype(F32)
    n_it = jnp.where(counts > 0.0, last_tile - first_tile + 1.0, 0.0)
    it_end = _lane_prefix_sum(n_it, lane8)
    it_start = it_end - n_it
    total = it_end[0:1, LANES - 1:LANES]
    sub = lax.broadcasted_iota(I32, (LANES, LANES), 0)
    rows = lambda x: jnp.broadcast_to(x[0:1, :], (LANES, LANES))
    used = counts > 0.0
    weight_slot = ((_lane_prefix_sum(jnp.where(used, 1.0, 0.0), lane8) - 1.0).astype(I32) & 1).astype(F32)
    next_used = _lane_suffix_min_exclusive(jnp.where(used, lane8.astype(F32), float(LANES)), lane8)
    per_expert = (starts, ends, first_tile, it_start, it_end, weight_slot, next_used)
    stacked = jnp.zeros((LANES, LANES), F32)
    for j, vec in enumerate(per_expert):
        stacked = jnp.where(sub == j, rows(vec), stacked)
    cols = stacked.T
    col = lambda j: cols[:, j:j + 1]
    sub_f = sub.astype(F32)
    w = lax.broadcasted_iota(I32, (LANES, LANES), 1).astype(F32)
    ex = jnp.sum(jnp.where((sub < N_EXPERTS) & (col(4) <= w), 1.0, 0.0), axis=0, keepdims=True)
    ex = jnp.minimum(ex, N_EXPERTS - 1.0)
    w1 = w[0:1, :]
    valid = w1 < total
    ex = jnp.where(valid, ex, jnp.max(jnp.where(valid, ex, 0.0), axis=-1, keepdims=True))
    onehot = sub_f == ex
    pick = lambda j: jnp.sum(jnp.where(onehot, col(j), 0.0), axis=0, keepdims=True)
    tile = jnp.where(valid, pick(2) + (w1 - pick(3)), n_moe_tiles - 1.0)
    lo = jnp.where(valid, jnp.maximum(pick(0), tile * MOE_TILE), 0.0)
    hi = jnp.where(valid, jnp.minimum(pick(1), (tile + 1.0) * MOE_TILE), 0.0)
    b8 = lambda x: jnp.broadcast_to(x, (SUBLANES, LANES))
    ex8, tile8 = b8(ex), b8(tile)
    first = jnp.where((lane8 == 0) | (tile8 != pltpu.roll(tile8, 1, 1)), 1.0, 0.0)
    newexp = jnp.where((lane8 == 0) | (ex8 != pltpu.roll(ex8, 1, 1)), 1.0, 0.0)
    sub8 = lax.broadcasted_iota(I32, (SUBLANES, LANES), 0)
    table = jnp.zeros((SUBLANES, LANES), F32)
    for j, vec in enumerate((ex8, tile8, b8(lo), b8(hi), first, newexp, b8(pick(5)), b8(pick(6)))):
        table = jnp.where(sub8 == j, vec, table)
    return table


def _plan_kernel(route_ref, pos_ref, items_ref, tri_ref, carry_ref, tot_ref, *, n_moe_tiles):
    p = pl.program_id(0)
    t = pl.program_id(1)
    tm = route_ref.shape[0]
    r = route_ref[...]
    lane = lax.broadcasted_iota(I32, (tm, LANES), 1)
    lane_f = lane.astype(F32)
    e0 = r[:, 0:1]
    e1 = r[:, 1:2] + N_EXPERTS
    m = jnp.where((lane_f == e0) | (lane_f == e1), 1.0, 0.0)
    colsum = jnp.sum(m, axis=0, keepdims=True)

    @pl.when((p == 0) & (t == 0))
    def _():
        carry_ref[...] = jnp.zeros_like(carry_ref)
        row = lax.broadcasted_iota(I32, (tm, tm), 0)
        col = lax.broadcasted_iota(I32, (tm, tm), 1)
        tri_ref[...] = jnp.where(row > col, 1.0, 0.0).astype(BF16)

    @pl.when(p == 0)
    def _():
        carry_ref[...] += colsum

    @pl.when((p == 0) & (t == pl.num_programs(1) - 1))
    def _():
        tot_ref[...] = carry_ref[...]
        carry_ref[...] = jnp.zeros_like(carry_ref)

    @pl.when(p == 1)
    def _():
        lane8 = lax.broadcasted_iota(I32, (SUBLANES, LANES), 1)
        tot = tot_ref[...]
        is_first = lane8 < N_EXPERTS
        tot0 = jnp.where(is_first, tot, 0.0)
        counts = jnp.where(is_first, tot + pltpu.roll(tot, LANES - N_EXPERTS, 1), 0.0)
        ends = _lane_prefix_sum(counts, lane8)
        starts = ends - counts
        base = jnp.where(is_first, starts, pltpu.roll(starts + tot0, N_EXPERTS, 1))
        before = _dot(tri_ref[...], m.astype(BF16)) + carry_ref[0:1, :]
        carry_ref[...] += colsum
        rows = before + base[0:1, :]
        pos0 = jnp.sum(jnp.where(lane_f == e0, rows, 0.0), axis=-1, keepdims=True)
        pos1 = jnp.sum(jnp.where(lane_f == e1, rows, 0.0), axis=-1, keepdims=True)
        both = jnp.where(lane == 0, pos0, jnp.where(lane == 1, pos1, 0.0))
        pos_ref[...] = both.T[0:SUBLANES, :].astype(I32)

        @pl.when(t == 0)
        def _():
            items_ref[...] = _work_items(counts, starts, ends, lane8, n_moe_tiles).astype(I32)


def _moe_plan(route):
    t_tok = route.shape[0]
    tm = PLAN_TILE
    n_moe_tiles = 2 * t_tok // MOE_TILE
    assert n_moe_tiles + N_EXPERTS - 1 <= LANES and 2 * N_EXPERTS <= LANES and MOE_TILE & (MOE_TILE - 1) == 0
    return pl.pallas_call(
        functools.partial(_plan_kernel, n_moe_tiles=n_moe_tiles),
        out_shape=(jax.ShapeDtypeStruct((SUBLANES, t_tok), I32), jax.ShapeDtypeStruct((SUBLANES, LANES), I32)),
        grid=(2, t_tok // tm),
        in_specs=[pl.BlockSpec((tm, LANES), lambda p, t: (t, 0))],
        out_specs=(pl.BlockSpec((SUBLANES, tm), lambda p, t: (0, t * p)),
                   pl.BlockSpec((SUBLANES, LANES), lambda p, t: (0, 0))),
        scratch_shapes=[pltpu.VMEM((tm, tm), BF16), pltpu.VMEM((SUBLANES, LANES), F32),
                        pltpu.VMEM((SUBLANES, LANES), F32)],
        compiler_params=_params(("arbitrary", "arbitrary")),
        name="moe_positions",
    )(route)


def _sc_workers():
    info = plsc.get_sparse_core_info()
    return info.num_cores, info.num_cores * info.num_subcores


def _sc_pipeline(n_chunks, loads, stores):
    def start(copies):
        for cp in copies:
            cp.start()

    def wait(copies):
        for cp in copies:
            cp.wait()

    start(loads(0, 0))
    for j in range(n_chunks):
        b = j % 2
        if j + 1 < n_chunks:
            if j >= 1:
                wait(stores(j - 1, 1 - b))
            start(loads(j + 1, 1 - b))
        wait(loads(j, b))
        start(stores(j, b))
    if n_chunks >= 2:
        wait(stores(n_chunks - 2, n_chunks % 2))
    wait(stores(n_chunks - 1, (n_chunks - 1) % 2))


def _sc_chunking(n_rows, d):
    n_cores, n_workers = _sc_workers()
    per_worker = n_rows // n_workers
    k = SC_BUFFER_BYTES // (d * 4)
    n_chunks = per_worker // k
    assert n_chunks * k * n_workers == n_rows and k <= SC_MAX_INDEX_CHUNK
    return n_cores, n_workers, per_worker, k, n_chunks


def _sc_gather_rows(table, idx):
    n, d = idx.shape[0], table.shape[1]
    n_cores, n_workers, per_worker, k, n_chunks = _sc_chunking(n, d)

    def body(table_hbm, idx_hbm, out_hbm, idx_v, rows_v, gsem, osem):
        wid = lax.axis_index("s") * n_cores + lax.axis_index("c")
        base = wid * per_worker
        pltpu.sync_copy(idx_hbm.at[wid], idx_v)
        _sc_pipeline(
            n_chunks,
            lambda j, b: [pltpu.make_async_copy(table_hbm.at[idx_v.at[j]], rows_v.at[b], gsem.at[b])],
            lambda j, b: [pltpu.make_async_copy(rows_v.at[b], out_hbm.at[pl.ds(base + j * k, k)], osem.at[b])])

    return pl.kernel(
        body,
        out_type=jax.ShapeDtypeStruct((n, d), table.dtype),
        mesh=plsc.VectorSubcoreMesh(core_axis_name="c", subcore_axis_name="s"),
        scratch_types=[pltpu.VMEM((n_chunks, k), I32), pltpu.VMEM((2, k, d), table.dtype),
                       pltpu.SemaphoreType.DMA((2,)), pltpu.SemaphoreType.DMA((2,))],
        name="sc_row_gather",
    )(table, idx.reshape(n_workers, n_chunks, k))


def _sc_scatter_rows(src, idx):
    n_lists, n_src = idx.shape
    d = src.shape[1]
    n_cores, n_workers, per_worker, k, n_chunks = _sc_chunking(n_src, d)

    def body(src_hbm, idx_hbm, out_hbm, idx_v, rows_v, gsem, osem):
        wid = lax.axis_index("s") * n_cores + lax.axis_index("c")
        base = wid * per_worker
        pltpu.sync_copy(idx_hbm.at[wid], idx_v)
        _sc_pipeline(
            n_chunks,
            lambda j, b: [pltpu.make_async_copy(src_hbm.at[pl.ds(base + j * k, k)], rows_v.at[b], gsem.at[b])],
            lambda j, b: [pltpu.make_async_copy(rows_v.at[b], out_hbm.at[idx_v.at[c, j]], osem.at[b, c])
                          for c in range(n_lists)])

    return pl.kernel(
        body,
        out_type=jax.ShapeDtypeStruct((n_lists * n_src, d), src.dtype),
        mesh=plsc.VectorSubcoreMesh(core_axis_name="c", subcore_axis_name="s"),
        scratch_types=[pltpu.VMEM((n_lists, n_chunks, k), I32), pltpu.VMEM((2, k, d), src.dtype),
                       pltpu.SemaphoreType.DMA((2,)), pltpu.SemaphoreType.DMA((2, n_lists))],
        name="sc_row_scatter",
    )(src, idx.reshape(n_lists, n_workers, n_chunks, k).transpose(1, 0, 2, 3))


IT_EXPERT, IT_TILE, IT_LO, IT_HI, IT_FIRST, IT_NEWEXP, IT_SLOT, IT_NEXT = range(8)


def _moe_kernel(it_ref, xs_ref, wg_hbm, wu_hbm, wd_hbm, o_ref, wg_f, wu_f, wd_f, wgu_s, wd_s, sem, *, layer):
    w = pl.program_id(0)
    f = wg_f.shape[2]

    def weight_copies(expert, slot):
        return [pltpu.make_async_copy(src.at[layer, expert], dst.at[slot], sem.at[slot, j])
                for j, (src, dst) in enumerate(((wg_hbm, wg_f), (wu_hbm, wu_f), (wd_hbm, wd_f)))]

    @pl.when(it_ref[IT_NEWEXP, w] == 1)
    def _():
        expert = it_ref[IT_EXPERT, w]
        slot = it_ref[IT_SLOT, w]
        nxt = it_ref[IT_NEXT, w]

        @pl.when(w == 0)
        def _():
            for cp in weight_copies(expert, slot):
                cp.start()

        for cp in weight_copies(expert, slot):
            cp.wait()

        @pl.when(nxt < N_EXPERTS)
        def _():
            for cp in weight_copies(nxt, 1 - slot):
                cp.start()

        wgu_s[:, :f] = wg_f[slot].astype(BF16)
        wgu_s[:, f:] = wu_f[slot].astype(BF16)
        wd_s[...] = wd_f[slot].astype(BF16)

    tm = xs_ref.shape[0]
    x = _unpack_bf16_pairs(xs_ref[...]).astype(BF16)
    au = _dot(x, wgu_s[...])
    act = (_silu(au[:, :f]) * au[:, f:]).astype(BF16)
    y = _pack_bf16_pairs(_dot(act, wd_s[...]))
    row = it_ref[IT_TILE, w] * tm + lax.broadcasted_iota(I32, (tm, 1), 0)
    mine = (row >= it_ref[IT_LO, w]) & (row < it_ref[IT_HI, w])

    @pl.when(it_ref[IT_FIRST, w] == 1)
    def _():
        o_ref[...] = jnp.where(mine, y, jnp.zeros_like(y))

    @pl.when(it_ref[IT_FIRST, w] == 0)
    def _():
        o_ref[...] = jnp.where(mine, y, o_ref[...])


def _moe_experts(items, xs, w_gate, w_up, w_down, layer):
    n_rows, dp = xs.shape
    _, _, d, f = w_gate.shape
    tm = MOE_TILE
    rmap = lambda w, it: (it[IT_TILE, w], 0)
    hbm = pl.BlockSpec(memory_space=pl.ANY)
    return pl.pallas_call(
        functools.partial(_moe_kernel, layer=layer),
        out_shape=jax.ShapeDtypeStruct((n_rows, d // 2), jnp.uint32),
        grid_spec=pltpu.PrefetchScalarGridSpec(
            num_scalar_prefetch=1,
            grid=(n_rows // tm + N_EXPERTS - 1,),
            in_specs=[pl.BlockSpec((tm, dp), rmap), hbm, hbm, hbm],
            out_specs=pl.BlockSpec((tm, d // 2), rmap),
            scratch_shapes=[
                pltpu.VMEM((2, d, f), F32), pltpu.VMEM((2, d, f), F32), pltpu.VMEM((2, f, d), F32),
                pltpu.VMEM((d, 2 * f), BF16), pltpu.VMEM((f, d), BF16),
                pltpu.SemaphoreType.DMA((2, 3)),
            ],
        ),
        compiler_params=_params(("arbitrary",)),
        name="moe_grouped_mlp",
    )(items, xs, w_gate, w_up, w_down)


def _moe_combine_kernel(x_ref, y0_ref, y1_ref, route_ref, mod_ref, gb_ref, *o_refs, n_prompt_tiles):
    r = route_ref[...]
    moe = r[:, 2:3] * _unpack_bf16_pairs(y0_ref[...]) + r[:, 3:4] * _unpack_bf16_pairs(y1_ref[...])
    x2 = _deepnorm(x_ref[...], mod_ref[5:6, :], moe) * gb_ref[0:1, :] + gb_ref[1:2, :]
    if len(o_refs) == 1:
        o_refs[0][...] = x2
    else:
        @pl.when(pl.program_id(0) < n_prompt_tiles)
        def _():
            o_refs[0][...] = x2

        @pl.when(pl.program_id(0) >= n_prompt_tiles)
        def _():
            o_refs[1][...] = x2


def _moe_combine(x1, ys, route, mod, gb, n_prompt, dec_seq, split):
    t_tok, d = x1.shape
    tl = _Tiles(n_prompt, dec_seq, TOKEN_TILE)
    tm = tl.tm
    nt = t_tok // tm
    if split:
        out_shape = (jax.ShapeDtypeStruct((n_prompt, d), F32), jax.ShapeDtypeStruct((t_tok - n_prompt, d), F32))
        out_specs = (pl.BlockSpec((tm, d), tl.prompt_part), pl.BlockSpec((tm, d), tl.latent_part))
    else:
        out_shape = jax.ShapeDtypeStruct((t_tok, d), F32)
        out_specs = pl.BlockSpec((tm, d), tl.row)
    return pl.pallas_call(
        functools.partial(_moe_combine_kernel, n_prompt_tiles=tl.npt),
        out_shape=out_shape,
        grid=(nt,),
        in_specs=[
            pl.BlockSpec((tm, d), tl.row),
            pl.BlockSpec((tm, ys.shape[1]), tl.row),
            pl.BlockSpec((tm, ys.shape[1]), lambda t: (t + nt, 0)),
            pl.BlockSpec((tm, LANES), tl.row),
            pl.BlockSpec((None, N_MOD, d), tl.cond),
            pl.BlockSpec(gb.shape, lambda t: (0, 0)),
        ],
        out_specs=out_specs,
        compiler_params=_params(("arbitrary",)),
        name="moe_combine_postnorm",
    )(x1, ys, ys, route, mod, gb)


def _rope_tables(n_lat):
    nf = HEAD_DIM // 4
    s = np.arange(n_lat)
    lane = np.arange(LANES)
    inv = ROPE_BASE ** (-(lane % nf).astype(np.float64) / nf)
    use_col = (lane % HEAD_DIM) >= HEAD_DIM // 2
    p = np.where(use_col[None, :], (s % GRID_W)[:, None], (s // GRID_W)[:, None]).astype(np.float64)
    ang = p * inv[None, :]
    sign = np.where((lane % (2 * nf)) < nf, -1.0, 1.0)
    return jnp.asarray(np.cos(ang), F32), jnp.asarray(np.sin(ang) * sign[None, :], F32)


def _dft_tables(n):
    k = np.arange(n)
    ang = 2.0 * np.pi * ((k[:, None] * k[None, :]) % n).astype(np.float64) / n
    return np.cos(ang), np.sin(ang)


def kernel(x_prompt, x_sample, cache_k, cache_v, c, c_ctx, w_ada, b_ada, ln_gain, ln_bias, w_qkv, w_attn_out,
           lambda_q1, lambda_k1, lambda_q2, lambda_k2, subln_gain, w_fourier_out, w_router_group,
           w_router_expert, w_expert_gate, w_expert_up, w_expert_down):
    bp, sp, d = x_prompt.shape
    bs, n_lat, _ = x_sample.shape
    n_ctx = cache_k.shape[2]
    n_prompt = bp * sp
    t_tok = n_prompt + bs * n_lat
    assert d == D_MODEL and n_prompt % n_lat == 0 and n_lat % TOKEN_TILE == 0 and sp == QKV_TILE

    cond = jnp.concatenate([c_ctx[None, :], c, jnp.zeros((SUBLANES - 1 - bs, d), F32)], axis=0)
    mods = _ada_all(cond, w_ada, b_ada).reshape(DEPTH, SUBLANES, N_MOD, d)

    cos, sin = _rope_tables(n_lat)
    cc, sc = _dft_tables(FOURIER_GROUP_DIM)
    dcs = jnp.asarray(np.concatenate([cc, sc], axis=1), BF16)
    seq_tabs = {s: tuple(jnp.asarray(m, BF16) for m in _dft_tables(s)) for s in (sp, n_lat)}
    attn_layers = [i for i in range(DEPTH) if i % 2 == 0]

    xs = (x_prompt.reshape(n_prompt, d), x_sample.reshape(bs * n_lat, d))
    prev_kv = []
    new_k = new_v = None
    for i in range(DEPTH):
        mod = mods[i]
        if i % 2 == 0:
            a = i // 2
            lam_init = 0.8 - 0.6 * math.exp(-0.3 * i)
            last_attn = i == attn_layers[-1]
            q, k, v, kf, vf = _qkv(xs, mod, w_qkv[a].astype(BF16), cos, sin, prev_kv, last_attn, n_prompt, n_lat,
                                   t_tok)
            if last_attn:
                new_k, new_v = kf, vf
            else:
                prev_kv.append((kf, vf))
            lam_vecs = jnp.stack([lambda_q1[a], lambda_k1[a], lambda_q2[a], lambda_k2[a]], axis=0)
            gain = subln_gain[a][None, :]
            kc = cache_k[:, a].reshape(bs * n_ctx, d).astype(BF16)
            vc = cache_v[:, a].reshape(bs * n_ctx, d).astype(BF16)
            mixed_p = _attn_prompt(lam_vecs, gain, q, k, v, bp, sp, lam_init)
            mixed_s = _attn_sample(lam_vecs, gain, q, k, v, kc, vc, bs, n_lat, n_ctx, n_prompt, lam_init)
            w_mix = w_attn_out[a]
        else:
            xc, xsn = _chan_dft(xs[0], mod, dcs, n_prompt, n_lat)
            mixed_p = _seq_dft(*seq_tabs[sp], xc, xsn, bp, sp, 0)
            mixed_s = _seq_dft(*seq_tabs[n_lat], xc, xsn, bs, n_lat, n_prompt)
            w_mix = w_fourier_out[i // 2]
        gb0 = jnp.stack([ln_gain[i, 0], ln_bias[i, 0]], axis=0)
        gb1 = jnp.stack([ln_gain[i, 1], ln_bias[i, 1]], axis=0)
        wr = jnp.concatenate([w_router_group[i], w_router_expert[i],
                              jnp.zeros((d, LANES - N_EXPERT_GROUPS - N_EXPERTS), F32)], axis=1)
        wr_hi = wr.astype(BF16)
        wr_lo = (wr - wr_hi.astype(F32)).astype(BF16)
        x1, hp, route = _mix_out(mixed_p, mixed_s, w_mix.astype(BF16), xs, mod, gb0,
                                 jnp.stack([wr_hi, wr_lo], axis=0), n_prompt, n_lat, t_tok)
        pos8, items = _moe_plan(route)
        pos = pos8[0:2]
        xs_sorted = _sc_scatter_rows(hp, pos)
        ys = _moe_experts(items, xs_sorted, w_expert_gate, w_expert_up, w_expert_down, i)
        yg = _sc_gather_rows(ys, pos.reshape(-1))
        last = i == DEPTH - 1
        out = _moe_combine(x1, yg, route, mod, gb1, n_prompt, n_lat, split=last)
        xs = out if last else (out,)

    y_prompt = xs[0].reshape(bp, sp, d)
    y_sample = xs[1].reshape(bs, n_lat, d)
    return (y_prompt, y_sample, new_k.reshape(bp, len(attn_layers), sp, N_HEADS, 2 * HEAD_DIM),
            new_v.reshape(bp, len(attn_layers), sp, N_HEADS, V_DIM))
```

```python
import functools
import math

import numpy as np
import jax
import jax.numpy as jnp
from jax import lax
from jax.experimental import pallas as pl
from jax.experimental.pallas import tpu as pltpu
from jax.experimental.pallas import tpu_sc as plsc

F32 = jnp.float32
BF16 = jnp.bfloat16
I32 = jnp.int32

D_MODEL = 1024
DEPTH = 4
GRID_W = 64
N_HEADS = 8
HEAD_DIM = 64
V_DIM = 2 * HEAD_DIM
ROPE_BASE = 10000.0
N_FOURIER_GROUPS = 8
FOURIER_GROUP_DIM = D_MODEL // N_FOURIER_GROUPS
N_EXPERT_GROUPS = 4
EXPERTS_PER_GROUP = 8
N_EXPERTS = N_EXPERT_GROUPS * EXPERTS_PER_GROUP
D_EXPERT = 256
N_MOD = 6
LN_EPS = 1e-5
DEEPNORM_ALPHA = (2.0 * DEPTH) ** 0.25
Q_SCALE = math.log2(math.e) * HEAD_DIM ** -0.5

LANES = 128
SUBLANES = 8
TOKEN_TILE = 512
CHAIN_ROWS = 256
QKV_TILE = 256
Q_TILE = 512
ATTN_ROW_CHUNK = 128
MOE_TILE = 512
PLAN_TILE = 1024
SC_BUFFER_BYTES = 128 * 1024
SC_MAX_INDEX_CHUNK = 128
VMEM_LIMIT = 48 * 1024 * 1024


def _params(semantics):
    return pltpu.CompilerParams(dimension_semantics=semantics, vmem_limit_bytes=VMEM_LIMIT)


def _layernorm(x, eps=LN_EPS):
    mu = jnp.mean(x, axis=-1, keepdims=True)
    xc = x - mu
    var = jnp.mean(xc * xc, axis=-1, keepdims=True)
    return xc * lax.rsqrt(var + eps)


def _deepnorm(x, branch_gate, branch):
    return _layernorm(x + (branch_gate * (1.0 / DEEPNORM_ALPHA)) * branch, LN_EPS / DEEPNORM_ALPHA ** 2)


def _silu(a):
    return a / (1.0 + jnp.exp(-a))


def _dot(a, b):
    return jnp.dot(a, b, preferred_element_type=F32)


class _Tiles:
    def __init__(self, n_prompt, dec_seq, tm):
        self.tm = tm
        self.npt = n_prompt // tm
        self.tps = dec_seq // tm

    def row(self, t):
        return (t, 0)

    def cond(self, t):
        return (jnp.where(t < self.npt, 0, (t - self.npt) // self.tps + 1), 0, 0)

    def latent_pos(self, t):
        return (jnp.maximum(t - self.npt, 0) % self.tps, 0)

    def prompt_part(self, t):
        return (jnp.minimum(t, self.npt - 1), 0)

    def latent_part(self, t):
        return (jnp.maximum(t - self.npt, 0), 0)

    def stream_specs(self, stream, n_tiles):
        kind, arrays = stream
        tm = self.tm
        if kind == "merged":
            return [pl.BlockSpec((tm, arrays[0].shape[1]), self.row)]
        if kind == "pair":
            return [pl.BlockSpec((tm, arrays[0].shape[1]), self.prompt_part),
                    pl.BlockSpec((tm, arrays[1].shape[1]), self.latent_part)]
        x1, yg, route, mod, gb = arrays
        return [pl.BlockSpec((tm, x1.shape[1]), self.row),
                pl.BlockSpec((tm, yg.shape[1]), self.row),
                pl.BlockSpec((tm, yg.shape[1]), lambda t: (t + n_tiles, 0)),
                pl.BlockSpec((tm, route.shape[1]), self.row),
                pl.BlockSpec((None,) + mod.shape[1:], self.cond),
                pl.BlockSpec(gb.shape, lambda t: (0, 0))]


_STREAM_REFS = {"merged": 1, "pair": 2, "pending": 6}


def _stream_args(stream):
    kind, arrays = stream
    if kind == "pending":
        x1, yg, route, mod, gb = arrays
        return [x1, yg, yg, route, mod, gb]
    return list(arrays)


def _finish_moe(x1_ref, y0_ref, y1_ref, route_ref, mod_ref, gb_ref, rows=slice(None)):
    r = route_ref[rows, :]
    moe = r[:, 2:3] * _unpack_bf16_pairs(y0_ref[rows, :]) + r[:, 3:4] * _unpack_bf16_pairs(y1_ref[rows, :])
    return _deepnorm(x1_ref[rows, :], mod_ref[5:6, :], moe) * gb_ref[0:1, :] + gb_ref[1:2, :]


def _stream_rows(kind, refs, is_prompt_tile, rows=slice(None)):
    if kind == "merged":
        return refs[0][rows, :]
    if kind == "pair":
        return jnp.where(is_prompt_tile, refs[0][rows, :], refs[1][rows, :])
    return _finish_moe(*refs, rows)


def _ada_kernel(cond_ref, w_ref, b_ref, o_ref):
    a = _silu(cond_ref[...])
    o_ref[...] = _dot(a.astype(BF16), w_ref[...].astype(BF16)) + b_ref[...]


def _ada_all(cond, w_ada, b_ada):
    depth, d, n = w_ada.shape
    tn = n // 4
    return pl.pallas_call(
        _ada_kernel,
        out_shape=jax.ShapeDtypeStruct((depth, cond.shape[0], n), F32),
        grid=(depth, n // tn),
        in_specs=[
            pl.BlockSpec(cond.shape, lambda l, j: (0, 0)),
            pl.BlockSpec((None, d, tn), lambda l, j: (l, 0, j)),
            pl.BlockSpec((None, 1, tn), lambda l, j: (l, 0, j)),
        ],
        out_specs=pl.BlockSpec((None, cond.shape[0], tn), lambda l, j: (l, 0, j)),
        compiler_params=_params(("arbitrary", "arbitrary")),
        name="ada",
    )(cond, w_ada, b_ada.reshape(depth, 1, n))


def _rope(x, cos, sin_signed, first_half):
    outs = []
    for c in range(x.shape[1] // LANES):
        xc = x[:, c * LANES:(c + 1) * LANES]
        partner = jnp.where(first_half, pltpu.roll(xc, LANES - 16, 1), pltpu.roll(xc, 16, 1))
        outs.append(xc * cos + partner * sin_signed)
    return jnp.concatenate(outs, axis=1)


def _store_heads(cache_ref, slot, rows):
    cache_ref[slot] = pltpu.einshape("s(hd)->shd", rows, h=N_HEADS)


def _qkv_kernel(*refs, n_prompt_tiles, stream_kind, n_prev):
    n_x = _STREAM_REFS[stream_kind]
    x_refs, (mod_ref, w_ref, cos_ref, sin_ref) = refs[:n_x], refs[n_x:n_x + 4]
    prev = refs[n_x + 4:n_x + 4 + 2 * max(n_prev, 0)]
    q_ref, k_ref, v_ref, ko_ref, vo_ref = refs[n_x + 4 + 2 * max(n_prev, 0):][:5]
    t = pl.program_id(0)
    x = _stream_rows(stream_kind, x_refs, t < n_prompt_tiles)
    if stream_kind == "pending":
        refs[-1][...] = x
    h = (_layernorm(x) * (1.0 + mod_ref[1:2, :]) + mod_ref[0:1, :]).astype(BF16)
    d = w_ref.shape[0]
    project = lambda j: _dot(h, w_ref[:, j * d:(j + 1) * d])

    @pl.when(t < n_prompt_tiles)
    def _():
        q_ref[...] = (project(0) * Q_SCALE).astype(BF16)
        k = project(1)
        k_ref[...] = k.astype(BF16)
        v = project(2)
        v_ref[...] = v.astype(BF16)
        if n_prev < 0:
            ko_ref[...] = k
            vo_ref[...] = v
        else:
            for a in range(n_prev):
                _store_heads(ko_ref, a, prev[2 * a][...])
                _store_heads(vo_ref, a, prev[2 * a + 1][...])
            _store_heads(ko_ref, n_prev, k)
            _store_heads(vo_ref, n_prev, v)

    @pl.when(t >= n_prompt_tiles)
    def _():
        lane = lax.broadcasted_iota(I32, (h.shape[0], LANES), 1)
        first_half = (lane % 32) < 16
        cos = cos_ref[...]
        sin = sin_ref[...]
        q_ref[...] = _rope(project(0) * Q_SCALE, cos, sin, first_half).astype(BF16)
        k_ref[...] = _rope(project(1), cos, sin, first_half).astype(BF16)
        v_ref[...] = project(2).astype(BF16)


def _qkv(stream, mod, w_qkv, cos, sin, prev_kv, finish_cache, n_prompt, dec_seq, t_tok):
    d = w_qkv.shape[0]
    tl = _Tiles(n_prompt, dec_seq, QKV_TILE)
    tm = tl.tm
    pending = stream[0] == "pending"
    row_out = jax.ShapeDtypeStruct((t_tok, d), F32), pl.BlockSpec((tm, d), tl.row)
    n_prev = len(prev_kv) if finish_cache else -1
    if finish_cache:
        n_slots = n_prev + 1
        kv_shape = jax.ShapeDtypeStruct((n_prompt // tm, n_slots, tm, N_HEADS, V_DIM), F32)
        kv_spec = pl.BlockSpec((None, n_slots, tm, N_HEADS, V_DIM), lambda t: (tl.prompt_part(t)[0], 0, 0, 0, 0))
    else:
        kv_shape = jax.ShapeDtypeStruct((n_prompt, d), F32)
        kv_spec = pl.BlockSpec((tm, d), tl.prompt_part)
    prev_flat = [a for kv in prev_kv for a in kv] if finish_cache else []
    return pl.pallas_call(
        functools.partial(_qkv_kernel, n_prompt_tiles=tl.npt, stream_kind=stream[0], n_prev=n_prev),
        out_shape=(
            jax.ShapeDtypeStruct((t_tok, d), BF16),
            jax.ShapeDtypeStruct((t_tok, d), BF16),
            jax.ShapeDtypeStruct((t_tok, d), BF16),
            kv_shape, kv_shape,
        ) + ((row_out[0],) if pending else ()),
        grid=(t_tok // tm,),
        in_specs=tl.stream_specs(stream, t_tok // tm) + [
            pl.BlockSpec((None, N_MOD, d), tl.cond),
            pl.BlockSpec(w_qkv.shape, lambda t: (0, 0)),
            pl.BlockSpec((tm, LANES), tl.latent_pos),
            pl.BlockSpec((tm, LANES), tl.latent_pos),
        ] + [pl.BlockSpec((tm, d), tl.prompt_part) for _ in prev_flat],
        out_specs=(
            pl.BlockSpec((tm, d), tl.row),
            pl.BlockSpec((tm, d), tl.row),
            pl.BlockSpec((tm, d), tl.row),
            kv_spec, kv_spec,
        ) + ((row_out[1],) if pending else ()),
        compiler_params=_params(("arbitrary",)),
        name="ln_qkv_rope",
    )(*_stream_args(stream), mod, w_qkv, cos, sin, *prev_flat)


def _row_chains(n_rows):
    return [slice(r, r + CHAIN_ROWS) for r in range(0, n_rows, CHAIN_ROWS)]


def _chan_dft_kernel(*refs, n_prompt_tiles, stream_kind):
    n_x = _STREAM_REFS[stream_kind]
    x_refs, (mod_ref, dcs_ref, xc_ref, xs_ref) = refs[:n_x], refs[n_x:n_x + 4]
    g = FOURIER_GROUP_DIM
    is_prompt = pl.program_id(0) < n_prompt_tiles
    for rows in _row_chains(xc_ref.shape[0]):
        x = _stream_rows(stream_kind, x_refs, is_prompt, rows)
        if stream_kind == "pending":
            refs[-1][rows, :] = x
        h = (_layernorm(x) * (1.0 + mod_ref[1:2, :]) + mod_ref[0:1, :]).astype(BF16)
        for i in range(N_FOURIER_GROUPS):
            r = _dot(h[:, i * g:(i + 1) * g], dcs_ref[...])
            xc_ref[rows, i * g:(i + 1) * g] = r[:, :g].astype(BF16)
            xs_ref[rows, i * g:(i + 1) * g] = r[:, g:].astype(BF16)


def _chan_dft(stream, mod, dcs, n_prompt, dec_seq, t_tok):
    d = mod.shape[-1]
    tl = _Tiles(n_prompt, dec_seq, TOKEN_TILE)
    tm = tl.tm
    pending = stream[0] == "pending"
    half = jax.ShapeDtypeStruct((t_tok, d), BF16)
    return pl.pallas_call(
        functools.partial(_chan_dft_kernel, n_prompt_tiles=tl.npt, stream_kind=stream[0]),
        out_shape=(half, half) + ((jax.ShapeDtypeStruct((t_tok, d), F32),) if pending else ()),
        grid=(t_tok // tm,),
        in_specs=tl.stream_specs(stream, t_tok // tm) + [
            pl.BlockSpec((None, N_MOD, d), tl.cond),
            pl.BlockSpec(dcs.shape, lambda t: (0, 0)),
        ],
        out_specs=(pl.BlockSpec((tm, d), tl.row),) * (3 if pending else 2),
        compiler_params=_params(("arbitrary",)),
        name="ln_chan_dft",
    )(*_stream_args(stream), mod, dcs)


def _seq_dft_kernel(cs_ref, ss_ref, xc_ref, xs_ref, o_ref, *, norm):
    f = _dot(cs_ref[...], xc_ref[...]) - _dot(ss_ref[...], xs_ref[...])
    o_ref[...] = (f * norm).astype(BF16)


def _seq_dft(cs, ss, xc, xs, batch, seq, row_offset):
    d = xc.shape[1]
    tm = min(QKV_TILE, seq)
    spt = seq // tm
    off_seq = row_offset // seq
    return pl.pallas_call(
        functools.partial(_seq_dft_kernel, norm=1.0 / math.sqrt(seq * FOURIER_GROUP_DIM)),
        out_shape=jax.ShapeDtypeStruct((batch * seq, d), BF16),
        grid=(batch, spt),
        in_specs=[
            pl.BlockSpec((tm, seq), lambda b, i: (i, 0)),
            pl.BlockSpec((tm, seq), lambda b, i: (i, 0)),
            pl.BlockSpec((seq, d), lambda b, i: (off_seq + b, 0)),
            pl.BlockSpec((seq, d), lambda b, i: (off_seq + b, 0)),
        ],
        out_specs=pl.BlockSpec((tm, d), lambda b, i: (b * spt + i, 0)),
        compiler_params=_params(("arbitrary", "arbitrary")),
        name=f"seq_dft_{seq}",
    )(cs, ss, xc, xs)


def _diff_lambda(lam_ref, lam_init):
    lv = lam_ref[...]
    return (jnp.exp(jnp.sum(lv[0:1] * lv[1:2], axis=-1, keepdims=True))
            - jnp.exp(jnp.sum(lv[2:3] * lv[3:4], axis=-1, keepdims=True)) + lam_init)


def _diff_attn_head(q, k, v_ext, lam, gain, lam_init):
    tq = q.shape[0]
    lane = lax.broadcasted_iota(I32, q.shape, 1)
    zero = jnp.zeros_like(q)
    qq = jnp.concatenate([jnp.where(lane < HEAD_DIM, q, zero), jnp.where(lane >= HEAD_DIM, q, zero)], axis=0)
    parts = []
    for r in range(0, 2 * tq, ATTN_ROW_CHUNK):
        s = lax.dot_general(qq[r:r + ATTN_ROW_CHUNK], k, (((1,), (1,)), ((), ())), preferred_element_type=F32)
        e = jnp.exp2(s - jnp.max(s, axis=-1, keepdims=True)).astype(BF16)
        parts.append(_dot(e, v_ext))
    oe = jnp.concatenate(parts, axis=0)
    o = oe[:, :V_DIM] / oe[:, V_DIM:]
    o = o[:tq] - lam * o[tq:]
    o = o * lax.rsqrt(jnp.mean(o * o, axis=-1, keepdims=True) + LN_EPS)
    return o * gain * (1.0 - lam_init)


def _attn_prompt_kernel(lam_ref, gain_ref, q_ref, k_ref, v_ref, o_ref, *, lam_init):
    lam = _diff_lambda(lam_ref, lam_init)
    gain = gain_ref[...]
    ones = jnp.ones((k_ref.shape[0], V_DIM), BF16)
    for hd in range(N_HEADS):
        cols = slice(hd * V_DIM, (hd + 1) * V_DIM)
        v_ext = jnp.concatenate([v_ref[:, cols], ones], axis=1)
        o_ref[:, cols] = _diff_attn_head(q_ref[:, cols], k_ref[:, cols], v_ext, lam, gain, lam_init).astype(BF16)


def _attn_latent_kernel(lam_ref, gain_ref, q_ref, k_ref, v_ref, kc_ref, vc_ref, o_ref, kall_ref, vext_ref, *,
                        lam_init):
    n_new = k_ref.shape[0]

    @pl.when(pl.program_id(2) == 0)
    def _():
        kall_ref[:n_new, :] = k_ref[...]
        kall_ref[n_new:, :] = kc_ref[...]
        vext_ref[:n_new, :V_DIM] = v_ref[...]
        vext_ref[n_new:, :V_DIM] = vc_ref[...]
        vext_ref[:, V_DIM:] = jnp.ones((vext_ref.shape[0], V_DIM), BF16)

    o = _diff_attn_head(q_ref[...], kall_ref[...], vext_ref[...], _diff_lambda(lam_ref, lam_init), gain_ref[...],
                        lam_init)
    o_ref[...] = o.astype(BF16)


def _attn_prompt(lam_vecs, gain, q, k, v, batch, seq, lam_init):
    d = q.shape[1]
    blk = pl.BlockSpec((seq, d), lambda b: (b, 0))
    return pl.pallas_call(
        functools.partial(_attn_prompt_kernel, lam_init=lam_init),
        out_shape=jax.ShapeDtypeStruct((batch * seq, d), BF16),
        grid=(batch,),
        in_specs=[
            pl.BlockSpec(lam_vecs.shape, lambda b: (0, 0)),
            pl.BlockSpec(gain.shape, lambda b: (0, 0)),
            blk, blk, blk,
        ],
        out_specs=blk,
        compiler_params=_params(("arbitrary",)),
        name="diff_attn_ctx",
    )(lam_vecs, gain, q, k, v)


def _attn_sample(lam_vecs, gain, q, k, v, kc, vc, batch, seq, n_ctx, row_offset, lam_init):
    d = q.shape[1]
    tq = Q_TILE
    qpt = seq // tq
    off_seq = row_offset // seq
    off_tile = row_offset // tq
    qmap = lambda b, h, i: (off_tile + b * qpt + i, h)
    kmap = lambda b, h, i: (off_seq + b, h)
    cmap = lambda b, h, i: (b, h)
    return pl.pallas_call(
        functools.partial(_attn_latent_kernel, lam_init=lam_init),
        out_shape=jax.ShapeDtypeStruct((batch * seq, d), BF16),
        grid=(batch, N_HEADS, qpt),
        in_specs=[
            pl.BlockSpec(lam_vecs.shape, lambda b, h, i: (0, 0)),
            pl.BlockSpec(gain.shape, lambda b, h, i: (0, 0)),
            pl.BlockSpec((tq, V_DIM), qmap),
            pl.BlockSpec((seq, V_DIM), kmap),
            pl.BlockSpec((seq, V_DIM), kmap),
            pl.BlockSpec((n_ctx, V_DIM), cmap),
            pl.BlockSpec((n_ctx, V_DIM), cmap),
        ],
        out_specs=pl.BlockSpec((tq, V_DIM), lambda b, h, i: (b * qpt + i, h)),
        scratch_shapes=[pltpu.VMEM((seq + n_ctx, V_DIM), BF16), pltpu.VMEM((seq + n_ctx, 2 * V_DIM), BF16)],
        compiler_params=_params(("arbitrary", "arbitrary", "arbitrary")),
        name="diff_attn_latent",
    )(lam_vecs, gain, q, k, v, kc, vc)


def _route(lg):
    lane = lax.broadcasted_iota(I32, lg.shape, 1)
    lane_f = lane.astype(F32)
    neg = jnp.float32(-jnp.inf)
    big = jnp.float32(LANES)
    gl = jnp.where(lane < N_EXPERT_GROUPS, lg, neg)
    gmax = jnp.max(gl, axis=-1, keepdims=True)
    g_prob = 1.0 / jnp.sum(jnp.exp(gl - gmax), axis=-1, keepdims=True)
    g_idx = jnp.min(jnp.where(gl == gmax, lane_f, big), axis=-1, keepdims=True)
    lo = N_EXPERT_GROUPS + EXPERTS_PER_GROUP * g_idx
    el = jnp.where((lane_f >= lo) & (lane_f < lo + EXPERTS_PER_GROUP), lg, neg)
    m1 = jnp.max(el, axis=-1, keepdims=True)
    i1 = jnp.min(jnp.where(el == m1, lane_f, big), axis=-1, keepdims=True)
    el2 = jnp.where(lane_f == i1, neg, el)
    m2 = jnp.max(el2, axis=-1, keepdims=True)
    i2 = jnp.min(jnp.where(el2 == m2, lane_f, big), axis=-1, keepdims=True)
    t = jnp.exp(m2 - m1)
    w1 = g_prob / (1.0 + t)
    w2 = g_prob * t / (1.0 + t)
    out = jnp.where(lane == 0, i1 - N_EXPERT_GROUPS, 0.0)
    out = jnp.where(lane == 1, i2 - N_EXPERT_GROUPS, out)
    out = jnp.where(lane == 2, w1, out)
    out = jnp.where(lane == 3, w2, out)
    return out


def _pack_bf16_pairs(h):
    n = h.shape[1] // 2
    bits = lax.bitcast_convert_type(h.astype(BF16).astype(F32), jnp.uint32)
    return (bits[:, :n] >> 16) | bits[:, n:]


def _unpack_bf16_pairs(p):
    lo = lax.bitcast_convert_type(p << 16, F32)
    hi = lax.bitcast_convert_type(p & jnp.uint32(0xFFFF0000), F32)
    return jnp.concatenate([lo, hi], axis=1)


def _mix_out_kernel(*refs, n_prompt_tiles, stream_kind):
    n_x = _STREAM_REFS[stream_kind]
    ap_ref, as_ref, w_ref = refs[:3]
    x_refs = refs[3:3 + n_x]
    mod_ref, gb_ref, wr_ref, x1_ref, hp_ref, route_ref = refs[3 + n_x:]
    is_prompt = pl.program_id(0) < n_prompt_tiles
    for rows in _row_chains(x1_ref.shape[0]):
        a = jnp.where(is_prompt, ap_ref[rows, :], as_ref[rows, :])
        out = _dot(a, w_ref[...])
        x = _stream_rows(stream_kind, x_refs, is_prompt, rows)
        x1 = _deepnorm(x, mod_ref[2:3, :], out) * gb_ref[0:1, :] + gb_ref[1:2, :]
        x1_ref[rows, :] = x1
        h2 = _layernorm(x1) * (1.0 + mod_ref[4:5, :]) + mod_ref[3:4, :]
        hp_ref[rows, :] = _pack_bf16_pairs(h2)
        hi = h2.astype(BF16)
        lo = (h2 - hi.astype(F32)).astype(BF16)
        logits = _dot(hi, wr_ref[0]) + _dot(lo, wr_ref[0]) + _dot(hi, wr_ref[1])
        route_ref[rows, :] = _route(logits)


def _mix_out(a_prompt, a_sample, w, stream, mod, gb, wr, n_prompt, dec_seq, t_tok):
    d = w.shape[1]
    tl = _Tiles(n_prompt, dec_seq, TOKEN_TILE)
    tm = tl.tm
    return pl.pallas_call(
        functools.partial(_mix_out_kernel, n_prompt_tiles=tl.npt, stream_kind=stream[0]),
        out_shape=(
            jax.ShapeDtypeStruct((t_tok, d), F32),
            jax.ShapeDtypeStruct((t_tok, d // 2), jnp.uint32),
            jax.ShapeDtypeStruct((t_tok, LANES), F32),
        ),
        grid=(t_tok // tm,),
        in_specs=[
            pl.BlockSpec((tm, d), tl.prompt_part),
            pl.BlockSpec((tm, d), tl.latent_part),
            pl.BlockSpec(w.shape, lambda t: (0, 0)),
        ] + tl.stream_specs(stream, t_tok // tm) + [
            pl.BlockSpec((None, N_MOD, d), tl.cond),
            pl.BlockSpec(gb.shape, lambda t: (0, 0)),
            pl.BlockSpec(wr.shape, lambda t: (0, 0, 0)),
        ],
        out_specs=(
            pl.BlockSpec((tm, d), tl.row),
            pl.BlockSpec((tm, d // 2), tl.row),
            pl.BlockSpec((tm, LANES), tl.row),
        ),
        compiler_params=_params(("arbitrary",)),
        name="mix_out_postnorm_router",
    )(a_prompt, a_sample, w, *_stream_args(stream), mod, gb, wr)


def _lane_prefix_sum(x, lane):
    sh = 1
    while sh < LANES:
        x = x + jnp.where(lane >= sh, pltpu.roll(x, sh, 1), 0.0)
        sh *= 2
    return x


def _lane_suffix_min_exclusive(x, lane):
    big = float(LANES)
    y = jnp.where(lane + 1 < LANES, pltpu.roll(x, LANES - 1, 1), big)
    sh = 1
    while sh < LANES:
        y = jnp.minimum(y, jnp.where(lane + sh < LANES, pltpu.roll(y, LANES - sh, 1), big))
        sh *= 2
    return y


def _work_items(counts, starts, ends, lane8, n_moe_tiles):
    shift = int(math.log2(MOE_TILE))
    first_tile = (starts.astype(I32) >> shift).astype(F32)
    last_tile = ((ends.astype(I32) - 1) >> shift).astype(F32)
    n_it = jnp.where(counts > 0.0, last_tile - first_tile + 1.0, 0.0)
    it_end = _lane_prefix_sum(n_it, lane8)
    it_start = it_end - n_it
    total = it_end[0:1, LANES - 1:LANES]
    sub = lax.broadcasted_iota(I32, (LANES, LANES), 0)
    rows = lambda x: jnp.broadcast_to(x[0:1, :], (LANES, LANES))
    used = counts > 0.0
    weight_slot = ((_lane_prefix_sum(jnp.where(used, 1.0, 0.0), lane8) - 1.0).astype(I32) & 1).astype(F32)
    next_used = _lane_suffix_min_exclusive(jnp.where(used, lane8.astype(F32), float(LANES)), lane8)
    per_expert = (starts, ends, first_tile, it_start, it_end, weight_slot, next_used)
    stacked = jnp.zeros((LANES, LANES), F32)
    for j, vec in enumerate(per_expert):
        stacked = jnp.where(sub == j, rows(vec), stacked)
    cols = stacked.T
    col = lambda j: cols[:, j:j + 1]
    sub_f = sub.astype(F32)
    w = lax.broadcasted_iota(I32, (LANES, LANES), 1).astype(F32)
    ex = jnp.sum(jnp.where((sub < N_EXPERTS) & (col(4) <= w), 1.0, 0.0), axis=0, keepdims=True)
    ex = jnp.minimum(ex, N_EXPERTS - 1.0)
    w1 = w[0:1, :]
    valid = w1 < total
    ex = jnp.where(valid, ex, jnp.max(jnp.where(valid, ex, 0.0), axis=-1, keepdims=True))
    onehot = sub_f == ex
    pick = lambda j: jnp.sum(jnp.where(onehot, col(j), 0.0), axis=0, keepdims=True)
    tile = jnp.where(valid, pick(2) + (w1 - pick(3)), n_moe_tiles - 1.0)
    lo = jnp.where(valid, jnp.maximum(pick(0), tile * MOE_TILE), 0.0)
    hi = jnp.where(valid, jnp.minimum(pick(1), (tile + 1.0) * MOE_TILE), 0.0)
    b8 = lambda x: jnp.broadcast_to(x, (SUBLANES, LANES))
    ex8, tile8 = b8(ex), b8(tile)
    first = jnp.where((lane8 == 0) | (tile8 != pltpu.roll(tile8, 1, 1)), 1.0, 0.0)
    newexp = jnp.where((lane8 == 0) | (ex8 != pltpu.roll(ex8, 1, 1)), 1.0, 0.0)
    sub8 = lax.broadcasted_iota(I32, (SUBLANES, LANES), 0)
    table = jnp.zeros((SUBLANES, LANES), F32)
    for j, vec in enumerate((ex8, tile8, b8(lo), b8(hi), first, newexp, b8(pick(5)), b8(pick(6)))):
        table = jnp.where(sub8 == j, vec, table)
    return table


def _plan_kernel(route_ref, pos_ref, items_ref, tri_ref, carry_ref, tot_ref, *, n_moe_tiles):
    p = pl.program_id(0)
    t = pl.program_id(1)
    tm = route_ref.shape[0]
    r = route_ref[...]
    lane = lax.broadcasted_iota(I32, (tm, LANES), 1)
    lane_f = lane.astype(F32)
    e0 = r[:, 0:1]
    e1 = r[:, 1:2] + N_EXPERTS
    m = jnp.where((lane_f == e0) | (lane_f == e1), 1.0, 0.0)
    colsum = jnp.sum(m, axis=0, keepdims=True)

    @pl.when((p == 0) & (t == 0))
    def _():
        carry_ref[...] = jnp.zeros_like(carry_ref)
        row = lax.broadcasted_iota(I32, (tm, tm), 0)
        col = lax.broadcasted_iota(I32, (tm, tm), 1)
        tri_ref[...] = jnp.where(row > col, 1.0, 0.0).astype(BF16)

    @pl.when(p == 0)
    def _():
        carry_ref[...] += colsum

    @pl.when((p == 0) & (t == pl.num_programs(1) - 1))
    def _():
        tot_ref[...] = carry_ref[...]
        carry_ref[...] = jnp.zeros_like(carry_ref)

    @pl.when(p == 1)
    def _():
        lane8 = lax.broadcasted_iota(I32, (SUBLANES, LANES), 1)
        tot = tot_ref[...]
        is_first = lane8 < N_EXPERTS
        tot0 = jnp.where(is_first, tot, 0.0)
        counts = jnp.where(is_first, tot + pltpu.roll(tot, LANES - N_EXPERTS, 1), 0.0)
        ends = _lane_prefix_sum(counts, lane8)
        starts = ends - counts
        base = jnp.where(is_first, starts, pltpu.roll(starts + tot0, N_EXPERTS, 1))
        before = _dot(tri_ref[...], m.astype(BF16)) + carry_ref[0:1, :]
        carry_ref[...] += colsum
        rows = before + base[0:1, :]
        pos0 = jnp.sum(jnp.where(lane_f == e0, rows, 0.0), axis=-1, keepdims=True)
        pos1 = jnp.sum(jnp.where(lane_f == e1, rows, 0.0), axis=-1, keepdims=True)
        both = jnp.where(lane == 0, pos0, jnp.where(lane == 1, pos1, 0.0))
        pos_ref[...] = both.T[0:SUBLANES, :].astype(I32)

        @pl.when(t == 0)
        def _():
            items_ref[...] = _work_items(counts, starts, ends, lane8, n_moe_tiles).astype(I32)


def _moe_plan(route):
    t_tok = route.shape[0]
    tm = PLAN_TILE
    n_moe_tiles = 2 * t_tok // MOE_TILE
    assert n_moe_tiles + N_EXPERTS - 1 <= LANES and 2 * N_EXPERTS <= LANES and MOE_TILE & (MOE_TILE - 1) == 0
    return pl.pallas_call(
        functools.partial(_plan_kernel, n_moe_tiles=n_moe_tiles),
        out_shape=(jax.ShapeDtypeStruct((SUBLANES, t_tok), I32), jax.ShapeDtypeStruct((SUBLANES, LANES), I32)),
        grid=(2, t_tok // tm),
        in_specs=[pl.BlockSpec((tm, LANES), lambda p, t: (t, 0))],
        out_specs=(pl.BlockSpec((SUBLANES, tm), lambda p, t: (0, t * p)),
                   pl.BlockSpec((SUBLANES, LANES), lambda p, t: (0, 0))),
        scratch_shapes=[pltpu.VMEM((tm, tm), BF16), pltpu.VMEM((SUBLANES, LANES), F32),
                        pltpu.VMEM((SUBLANES, LANES), F32)],
        compiler_params=_params(("arbitrary", "arbitrary")),
        name="moe_positions",
    )(route)


def _sc_workers():
    info = plsc.get_sparse_core_info()
    return info.num_cores, info.num_cores * info.num_subcores


def _sc_pipeline(n_chunks, loads, stores):
    def start(copies):
        for cp in copies:
            cp.start()

    def wait(copies):
        for cp in copies:
            cp.wait()

    start(loads(0, 0))
    for j in range(n_chunks):
        b = j % 2
        if j + 1 < n_chunks:
            if j >= 1:
                wait(stores(j - 1, 1 - b))
            start(loads(j + 1, 1 - b))
        wait(loads(j, b))
        start(stores(j, b))
    if n_chunks >= 2:
        wait(stores(n_chunks - 2, n_chunks % 2))
    wait(stores(n_chunks - 1, (n_chunks - 1) % 2))


def _sc_chunking(n_rows, d):
    n_cores, n_workers = _sc_workers()
    per_worker = n_rows // n_workers
    k = SC_BUFFER_BYTES // (d * 4)
    n_chunks = per_worker // k
    assert n_chunks * k * n_workers == n_rows and k <= SC_MAX_INDEX_CHUNK
    return n_cores, n_workers, per_worker, k, n_chunks


def _sc_gather_rows(table, idx):
    n, d = idx.shape[0], table.shape[1]
    n_cores, n_workers, per_worker, k, n_chunks = _sc_chunking(n, d)

    def body(table_hbm, idx_hbm, out_hbm, idx_v, rows_v, gsem, osem):
        wid = lax.axis_index("s") * n_cores + lax.axis_index("c")
        base = wid * per_worker
        pltpu.sync_copy(idx_hbm.at[wid], idx_v)
        _sc_pipeline(
            n_chunks,
            lambda j, b: [pltpu.make_async_copy(table_hbm.at[idx_v.at[j]], rows_v.at[b], gsem.at[b])],
            lambda j, b: [pltpu.make_async_copy(rows_v.at[b], out_hbm.at[pl.ds(base + j * k, k)], osem.at[b])])

    return pl.kernel(
        body,
        out_type=jax.ShapeDtypeStruct((n, d), table.dtype),
        mesh=plsc.VectorSubcoreMesh(core_axis_name="c", subcore_axis_name="s"),
        scratch_types=[pltpu.VMEM((n_chunks, k), I32), pltpu.VMEM((2, k, d), table.dtype),
                       pltpu.SemaphoreType.DMA((2,)), pltpu.SemaphoreType.DMA((2,))],
        name="sc_row_gather",
    )(table, idx.reshape(n_workers, n_chunks, k))


def _sc_scatter_rows(src, idx):
    n_lists, n_src = idx.shape
    d = src.shape[1]
    n_cores, n_workers, per_worker, k, n_chunks = _sc_chunking(n_src, d)

    def body(src_hbm, idx_hbm, out_hbm, idx_v, rows_v, gsem, osem):
        wid = lax.axis_index("s") * n_cores + lax.axis_index("c")
        base = wid * per_worker
        pltpu.sync_copy(idx_hbm.at[wid], idx_v)
        _sc_pipeline(
            n_chunks,
            lambda j, b: [pltpu.make_async_copy(src_hbm.at[pl.ds(base + j * k, k)], rows_v.at[b], gsem.at[b])],
            lambda j, b: [pltpu.make_async_copy(rows_v.at[b], out_hbm.at[idx_v.at[c, j]], osem.at[b, c])
                          for c in range(n_lists)])

    return pl.kernel(
        body,
        out_type=jax.ShapeDtypeStruct((n_lists * n_src, d), src.dtype),
        mesh=plsc.VectorSubcoreMesh(core_axis_name="c", subcore_axis_name="s"),
        scratch_types=[pltpu.VMEM((n_lists, n_chunks, k), I32), pltpu.VMEM((2, k, d), src.dtype),
                       pltpu.SemaphoreType.DMA((2,)), pltpu.SemaphoreType.DMA((2, n_lists))],
        name="sc_row_scatter",
    )(src, idx.reshape(n_lists, n_workers, n_chunks, k).transpose(1, 0, 2, 3))


IT_EXPERT, IT_TILE, IT_LO, IT_HI, IT_FIRST, IT_NEWEXP, IT_SLOT, IT_NEXT = range(8)


def _moe_kernel(it_ref, xs_ref, wg_hbm, wu_hbm, wd_hbm, o_ref, wg_f, wu_f, wd_f, wgu_s, wd_s, sem, *, layer):
    w = pl.program_id(0)
    f = wg_f.shape[2]

    def weight_copies(expert, slot):
        return [pltpu.make_async_copy(src.at[layer, expert], dst.at[slot], sem.at[slot, j])
                for j, (src, dst) in enumerate(((wg_hbm, wg_f), (wu_hbm, wu_f), (wd_hbm, wd_f)))]

    @pl.when(it_ref[IT_NEWEXP, w] == 1)
    def _():
        expert = it_ref[IT_EXPERT, w]
        slot = it_ref[IT_SLOT, w]
        nxt = it_ref[IT_NEXT, w]

        @pl.when(w == 0)
        def _():
            for cp in weight_copies(expert, slot):
                cp.start()

        for cp in weight_copies(expert, slot):
            cp.wait()

        @pl.when(nxt < N_EXPERTS)
        def _():
            for cp in weight_copies(nxt, 1 - slot):
                cp.start()

        wgu_s[:, :f] = wg_f[slot].astype(BF16)
        wgu_s[:, f:] = wu_f[slot].astype(BF16)
        wd_s[...] = wd_f[slot].astype(BF16)

    tm = xs_ref.shape[0]
    x = _unpack_bf16_pairs(xs_ref[...]).astype(BF16)
    au = _dot(x, wgu_s[...])
    act = (_silu(au[:, :f]) * au[:, f:]).astype(BF16)
    y = _pack_bf16_pairs(_dot(act, wd_s[...]))
    row = it_ref[IT_TILE, w] * tm + lax.broadcasted_iota(I32, (tm, 1), 0)
    mine = (row >= it_ref[IT_LO, w]) & (row < it_ref[IT_HI, w])

    @pl.when(it_ref[IT_FIRST, w] == 1)
    def _():
        o_ref[...] = jnp.where(mine, y, jnp.zeros_like(y))

    @pl.when(it_ref[IT_FIRST, w] == 0)
    def _():
        o_ref[...] = jnp.where(mine, y, o_ref[...])


def _moe_experts(items, xs, w_gate, w_up, w_down, layer):
    n_rows, dp = xs.shape
    _, _, d, f = w_gate.shape
    tm = MOE_TILE
    rmap = lambda w, it: (it[IT_TILE, w], 0)
    hbm = pl.BlockSpec(memory_space=pl.ANY)
    return pl.pallas_call(
        functools.partial(_moe_kernel, layer=layer),
        out_shape=jax.ShapeDtypeStruct((n_rows, d // 2), jnp.uint32),
        grid_spec=pltpu.PrefetchScalarGridSpec(
            num_scalar_prefetch=1,
            grid=(n_rows // tm + N_EXPERTS - 1,),
            in_specs=[pl.BlockSpec((tm, dp), rmap), hbm, hbm, hbm],
            out_specs=pl.BlockSpec((tm, d // 2), rmap),
            scratch_shapes=[
                pltpu.VMEM((2, d, f), F32), pltpu.VMEM((2, d, f), F32), pltpu.VMEM((2, f, d), F32),
                pltpu.VMEM((d, 2 * f), BF16), pltpu.VMEM((f, d), BF16),
                pltpu.SemaphoreType.DMA((2, 3)),
            ],
        ),
        compiler_params=_params(("arbitrary",)),
        name="moe_grouped_mlp",
    )(items, xs, w_gate, w_up, w_down)


def _final_postnorm_kernel(*refs, n_prompt_tiles):
    x2 = _finish_moe(*refs[:6])
    out_prompt, out_latent = refs[6:]

    @pl.when(pl.program_id(0) < n_prompt_tiles)
    def _():
        out_prompt[...] = x2

    @pl.when(pl.program_id(0) >= n_prompt_tiles)
    def _():
        out_latent[...] = x2


def _final_postnorm(stream, n_prompt, dec_seq, t_tok):
    d = stream[1][0].shape[1]
    tl = _Tiles(n_prompt, dec_seq, TOKEN_TILE)
    tm = tl.tm
    return pl.pallas_call(
        functools.partial(_final_postnorm_kernel, n_prompt_tiles=tl.npt),
        out_shape=(jax.ShapeDtypeStruct((n_prompt, d), F32), jax.ShapeDtypeStruct((t_tok - n_prompt, d), F32)),
        grid=(t_tok // tm,),
        in_specs=tl.stream_specs(stream, t_tok // tm),
        out_specs=(pl.BlockSpec((tm, d), tl.prompt_part), pl.BlockSpec((tm, d), tl.latent_part)),
        compiler_params=_params(("arbitrary",)),
        name="moe_combine_postnorm",
    )(*_stream_args(stream))


def _rope_tables(n_lat):
    nf = HEAD_DIM // 4
    s = np.arange(n_lat)
    lane = np.arange(LANES)
    inv = ROPE_BASE ** (-(lane % nf).astype(np.float64) / nf)
    use_col = (lane % HEAD_DIM) >= HEAD_DIM // 2
    p = np.where(use_col[None, :], (s % GRID_W)[:, None], (s // GRID_W)[:, None]).astype(np.float64)
    ang = p * inv[None, :]
    sign = np.where((lane % (2 * nf)) < nf, -1.0, 1.0)
    return jnp.asarray(np.cos(ang), F32), jnp.asarray(np.sin(ang) * sign[None, :], F32)


def _dft_tables(n):
    k = np.arange(n)
    ang = 2.0 * np.pi * ((k[:, None] * k[None, :]) % n).astype(np.float64) / n
    return np.cos(ang), np.sin(ang)


def kernel(x_prompt, x_sample, cache_k, cache_v, c, c_ctx, w_ada, b_ada, ln_gain, ln_bias, w_qkv, w_attn_out,
           lambda_q1, lambda_k1, lambda_q2, lambda_k2, subln_gain, w_fourier_out, w_router_group,
           w_router_expert, w_expert_gate, w_expert_up, w_expert_down):
    bp, sp, d = x_prompt.shape
    bs, n_lat, _ = x_sample.shape
    n_ctx = cache_k.shape[2]
    n_prompt = bp * sp
    t_tok = n_prompt + bs * n_lat
    assert d == D_MODEL and n_prompt % n_lat == 0 and n_lat % TOKEN_TILE == 0 and sp == QKV_TILE

    cond = jnp.concatenate([c_ctx[None, :], c, jnp.zeros((SUBLANES - 1 - bs, d), F32)], axis=0)
    mods = _ada_all(cond, w_ada, b_ada).reshape(DEPTH, SUBLANES, N_MOD, d)

    cos, sin = _rope_tables(n_lat)
    cc, sc = _dft_tables(FOURIER_GROUP_DIM)
    dcs = jnp.asarray(np.concatenate([cc, sc], axis=1), BF16)
    seq_tabs = {s: tuple(jnp.asarray(m, BF16) for m in _dft_tables(s)) for s in (sp, n_lat)}
    attn_layers = [i for i in range(DEPTH) if i % 2 == 0]

    stream = ("pair", (x_prompt.reshape(n_prompt, d), x_sample.reshape(bs * n_lat, d)))
    prev_kv = []
    new_k = new_v = None
    for i in range(DEPTH):
        mod = mods[i]
        if i % 2 == 0:
            a = i // 2
            lam_init = 0.8 - 0.6 * math.exp(-0.3 * i)
            last_attn = i == attn_layers[-1]
            q, k, v, kf, vf, *finished = _qkv(stream, mod, w_qkv[a].astype(BF16), cos, sin, prev_kv, last_attn,
                                              n_prompt, n_lat, t_tok)
            if last_attn:
                new_k, new_v = kf, vf
            else:
                prev_kv.append((kf, vf))
            lam_vecs = jnp.stack([lambda_q1[a], lambda_k1[a], lambda_q2[a], lambda_k2[a]], axis=0)
            gain = subln_gain[a][None, :]
            kc = cache_k[:, a].reshape(bs * n_ctx, d).astype(BF16)
            vc = cache_v[:, a].reshape(bs * n_ctx, d).astype(BF16)
            mixed_p = _attn_prompt(lam_vecs, gain, q, k, v, bp, sp, lam_init)
            mixed_s = _attn_sample(lam_vecs, gain, q, k, v, kc, vc, bs, n_lat, n_ctx, n_prompt, lam_init)
            w_mix = w_attn_out[a]
        else:
            xc, xsn, *finished = _chan_dft(stream, mod, dcs, n_prompt, n_lat, t_tok)
            mixed_p = _seq_dft(*seq_tabs[sp], xc, xsn, bp, sp, 0)
            mixed_s = _seq_dft(*seq_tabs[n_lat], xc, xsn, bs, n_lat, n_prompt)
            w_mix = w_fourier_out[i // 2]
        if finished:
            stream = ("merged", (finished[0],))
        gb0 = jnp.stack([ln_gain[i, 0], ln_bias[i, 0]], axis=0)
        gb1 = jnp.stack([ln_gain[i, 1], ln_bias[i, 1]], axis=0)
        wr = jnp.concatenate([w_router_group[i], w_router_expert[i],
                              jnp.zeros((d, LANES - N_EXPERT_GROUPS - N_EXPERTS), F32)], axis=1)
        wr_hi = wr.astype(BF16)
        wr_lo = (wr - wr_hi.astype(F32)).astype(BF16)
        x1, hp, route = _mix_out(mixed_p, mixed_s, w_mix.astype(BF16), stream, mod, gb0,
                                 jnp.stack([wr_hi, wr_lo], axis=0), n_prompt, n_lat, t_tok)
        pos8, items = _moe_plan(route)
        pos = pos8[0:2]
        xs_sorted = _sc_scatter_rows(hp, pos)
        ys = _moe_experts(items, xs_sorted, w_expert_gate, w_expert_up, w_expert_down, i)
        yg = _sc_gather_rows(ys, pos.reshape(-1))
        stream = ("pending", (x1, yg, route, mods[i], gb1))

    out_prompt, out_latent = _final_postnorm(stream, n_prompt, n_lat, t_tok)
    y_prompt = out_prompt.reshape(bp, sp, d)
    y_sample = out_latent.reshape(bs, n_lat, d)
    return (y_prompt, y_sample, new_k.reshape(bp, len(attn_layers), sp, N_HEADS, 2 * HEAD_DIM),
            new_v.reshape(bp, len(attn_layers), sp, N_HEADS, V_DIM))
```

```python
import functools
import math

import numpy as np
import jax
import jax.numpy as jnp
from jax import lax
from jax.experimental import pallas as pl
from jax.experimental.pallas import tpu as pltpu
from jax.experimental.pallas import tpu_sc as plsc

F32 = jnp.float32
BF16 = jnp.bfloat16
I32 = jnp.int32

D_MODEL = 1024
DEPTH = 4
GRID_W = 64
N_HEADS = 8
HEAD_DIM = 64
V_DIM = 2 * HEAD_DIM
ROPE_BASE = 10000.0
N_FOURIER_GROUPS = 8
FOURIER_GROUP_DIM = D_MODEL // N_FOURIER_GROUPS
N_EXPERT_GROUPS = 4
EXPERTS_PER_GROUP = 8
N_EXPERTS = N_EXPERT_GROUPS * EXPERTS_PER_GROUP
D_EXPERT = 256
N_MOD = 6
LN_EPS = 1e-5
DEEPNORM_ALPHA = (2.0 * DEPTH) ** 0.25
Q_SCALE = math.log2(math.e) * HEAD_DIM ** -0.5

LANES = 128
SUBLANES = 8
TOKEN_TILE = 512
CHAIN_ROWS = 256
QKV_TILE = 256
Q_TILE = 512
ATTN_ROW_CHUNK = 128
MOE_TILE = 512
PLAN_TILE = 1024
SC_BUFFER_BYTES = 128 * 1024
SC_MAX_INDEX_CHUNK = 128
VMEM_LIMIT = 48 * 1024 * 1024


def _params(semantics):
    return pltpu.CompilerParams(dimension_semantics=semantics, vmem_limit_bytes=VMEM_LIMIT)


def _layernorm(x, eps=LN_EPS):
    mu = jnp.mean(x, axis=-1, keepdims=True)
    xc = x - mu
    var = jnp.mean(xc * xc, axis=-1, keepdims=True)
    return xc * lax.rsqrt(var + eps)


def _deepnorm(x, branch_gate, branch):
    return _layernorm(x + (branch_gate * (1.0 / DEEPNORM_ALPHA)) * branch, LN_EPS / DEEPNORM_ALPHA ** 2)


def _silu(a):
    return a / (1.0 + jnp.exp(-a))


def _dot(a, b):
    return jnp.dot(a, b, preferred_element_type=F32)


class _Tiles:
    def __init__(self, n_prompt, dec_seq, tm):
        self.tm = tm
        self.npt = n_prompt // tm
        self.tps = dec_seq // tm

    def row(self, t):
        return (t, 0)

    def cond(self, t):
        return (jnp.where(t < self.npt, 0, (t - self.npt) // self.tps + 1), 0, 0)

    def latent_pos(self, t):
        return (jnp.maximum(t - self.npt, 0) % self.tps, 0)

    def prompt_part(self, t):
        return (jnp.minimum(t, self.npt - 1), 0)

    def latent_part(self, t):
        return (jnp.maximum(t - self.npt, 0), 0)

    def stream_specs(self, stream, n_tiles):
        kind, arrays = stream
        tm = self.tm
        if kind == "merged":
            return [pl.BlockSpec((tm, arrays[0].shape[1]), self.row)]
        if kind == "pair":
            return [pl.BlockSpec((tm, arrays[0].shape[1]), self.prompt_part),
                    pl.BlockSpec((tm, arrays[1].shape[1]), self.latent_part)]
        x1, yg, route, mod, gb = arrays
        return [pl.BlockSpec((tm, x1.shape[1]), self.row),
                pl.BlockSpec((tm, yg.shape[1]), self.row),
                pl.BlockSpec((tm, yg.shape[1]), lambda t: (t + n_tiles, 0)),
                pl.BlockSpec((tm, route.shape[1]), self.row),
                pl.BlockSpec((None,) + mod.shape[1:], self.cond),
                pl.BlockSpec(gb.shape, lambda t: (0, 0))]


_STREAM_REFS = {"merged": 1, "pair": 2, "pending": 6}


def _stream_args(stream):
    kind, arrays = stream
    if kind == "pending":
        x1, yg, route, mod, gb = arrays
        return [x1, yg, yg, route, mod, gb]
    return list(arrays)


def _finish_moe(x1_ref, y0_ref, y1_ref, route_ref, mod_ref, gb_ref, rows=slice(None)):
    r = route_ref[rows, :]
    moe = r[:, 2:3] * _unpack_bf16_pairs(y0_ref[rows, :]) + r[:, 3:4] * _unpack_bf16_pairs(y1_ref[rows, :])
    return _deepnorm(x1_ref[rows, :], mod_ref[5:6, :], moe) * gb_ref[0:1, :] + gb_ref[1:2, :]


def _stream_rows(kind, refs, is_prompt_tile, rows=slice(None)):
    if kind == "merged":
        return refs[0][rows, :]
    if kind == "pair":
        return jnp.where(is_prompt_tile, refs[0][rows, :], refs[1][rows, :])
    return _finish_moe(*refs, rows)


def _ada_kernel(cond_ref, w_ref, b_ref, o_ref):
    a = _silu(cond_ref[...])
    o_ref[...] = _dot(a.astype(BF16), w_ref[...].astype(BF16)) + b_ref[...]


def _ada_all(cond, w_ada, b_ada):
    depth, d, n = w_ada.shape
    tn = n // 4
    return pl.pallas_call(
        _ada_kernel,
        out_shape=jax.ShapeDtypeStruct((depth, cond.shape[0], n), F32),
        grid=(depth, n // tn),
        in_specs=[
            pl.BlockSpec(cond.shape, lambda l, j: (0, 0)),
            pl.BlockSpec((None, d, tn), lambda l, j: (l, 0, j)),
            pl.BlockSpec((None, 1, tn), lambda l, j: (l, 0, j)),
        ],
        out_specs=pl.BlockSpec((None, cond.shape[0], tn), lambda l, j: (l, 0, j)),
        compiler_params=_params(("arbitrary", "arbitrary")),
        name="ada",
    )(cond, w_ada, b_ada.reshape(depth, 1, n))


def _rope(x, cos, sin_signed, first_half):
    outs = []
    for c in range(x.shape[1] // LANES):
        xc = x[:, c * LANES:(c + 1) * LANES]
        partner = jnp.where(first_half, pltpu.roll(xc, LANES - 16, 1), pltpu.roll(xc, 16, 1))
        outs.append(xc * cos + partner * sin_signed)
    return jnp.concatenate(outs, axis=1)


def _store_heads(cache_ref, slot, rows):
    cache_ref[slot] = pltpu.einshape("s(hd)->shd", rows, h=N_HEADS)


def _qkv_kernel(*refs, n_prompt_tiles, stream_kind, n_prev):
    n_x = _STREAM_REFS[stream_kind]
    x_refs, (mod_ref, w_ref, cos_ref, sin_ref) = refs[:n_x], refs[n_x:n_x + 4]
    prev = refs[n_x + 4:n_x + 4 + 2 * max(n_prev, 0)]
    q_ref, k_ref, v_ref, ko_ref, vo_ref = refs[n_x + 4 + 2 * max(n_prev, 0):][:5]
    t = pl.program_id(0)
    x = _stream_rows(stream_kind, x_refs, t < n_prompt_tiles)
    if stream_kind == "pending":
        refs[-1][...] = x
    h = (_layernorm(x) * (1.0 + mod_ref[1:2, :]) + mod_ref[0:1, :]).astype(BF16)
    d = w_ref.shape[0]
    project = lambda j: _dot(h, w_ref[:, j * d:(j + 1) * d])

    @pl.when(t < n_prompt_tiles)
    def _():
        q_ref[...] = (project(0) * Q_SCALE).astype(BF16)
        k = project(1)
        k_ref[...] = k.astype(BF16)
        v = project(2)
        v_ref[...] = v.astype(BF16)
        if n_prev < 0:
            ko_ref[...] = k
            vo_ref[...] = v
        else:
            for a in range(n_prev):
                _store_heads(ko_ref, a, prev[2 * a][...])
                _store_heads(vo_ref, a, prev[2 * a + 1][...])
            _store_heads(ko_ref, n_prev, k)
            _store_heads(vo_ref, n_prev, v)

    @pl.when(t >= n_prompt_tiles)
    def _():
        lane = lax.broadcasted_iota(I32, (h.shape[0], LANES), 1)
        first_half = (lane % 32) < 16
        cos = cos_ref[...]
        sin = sin_ref[...]
        q_ref[...] = _rope(project(0) * Q_SCALE, cos, sin, first_half).astype(BF16)
        k_ref[...] = _rope(project(1), cos, sin, first_half).astype(BF16)
        v_ref[...] = project(2).astype(BF16)


def _qkv(stream, mod, w_qkv, cos, sin, prev_kv, finish_cache, n_prompt, dec_seq, t_tok):
    d = w_qkv.shape[0]
    tl = _Tiles(n_prompt, dec_seq, QKV_TILE)
    tm = tl.tm
    pending = stream[0] == "pending"
    row_out = jax.ShapeDtypeStruct((t_tok, d), F32), pl.BlockSpec((tm, d), tl.row)
    n_prev = len(prev_kv) if finish_cache else -1
    if finish_cache:
        n_slots = n_prev + 1
        kv_shape = jax.ShapeDtypeStruct((n_prompt // tm, n_slots, tm, N_HEADS, V_DIM), F32)
        kv_spec = pl.BlockSpec((None, n_slots, tm, N_HEADS, V_DIM), lambda t: (tl.prompt_part(t)[0], 0, 0, 0, 0))
    else:
        kv_shape = jax.ShapeDtypeStruct((n_prompt, d), F32)
        kv_spec = pl.BlockSpec((tm, d), tl.prompt_part)
    prev_flat = [a for kv in prev_kv for a in kv] if finish_cache else []
    return pl.pallas_call(
        functools.partial(_qkv_kernel, n_prompt_tiles=tl.npt, stream_kind=stream[0], n_prev=n_prev),
        out_shape=(
            jax.ShapeDtypeStruct((t_tok, d), BF16),
            jax.ShapeDtypeStruct((t_tok, d), BF16),
            jax.ShapeDtypeStruct((t_tok, d), BF16),
            kv_shape, kv_shape,
        ) + ((row_out[0],) if pending else ()),
        grid=(t_tok // tm,),
        in_specs=tl.stream_specs(stream, t_tok // tm) + [
            pl.BlockSpec((None, N_MOD, d), tl.cond),
            pl.BlockSpec(w_qkv.shape, lambda t: (0, 0)),
            pl.BlockSpec((tm, LANES), tl.latent_pos),
            pl.BlockSpec((tm, LANES), tl.latent_pos),
        ] + [pl.BlockSpec((tm, d), tl.prompt_part) for _ in prev_flat],
        out_specs=(
            pl.BlockSpec((tm, d), tl.row),
            pl.BlockSpec((tm, d), tl.row),
            pl.BlockSpec((tm, d), tl.row),
            kv_spec, kv_spec,
        ) + ((row_out[1],) if pending else ()),
        compiler_params=_params(("arbitrary",)),
        name="ln_qkv_rope",
    )(*_stream_args(stream), mod, w_qkv, cos, sin, *prev_flat)


def _row_chains(n_rows):
    return [slice(r, r + CHAIN_ROWS) for r in range(0, n_rows, CHAIN_ROWS)]


def _chan_dft_kernel(*refs, n_prompt_tiles, stream_kind):
    n_x = _STREAM_REFS[stream_kind]
    x_refs, (mod_ref, dcs_ref, xc_ref, xs_ref) = refs[:n_x], refs[n_x:n_x + 4]
    g = FOURIER_GROUP_DIM
    is_prompt = pl.program_id(0) < n_prompt_tiles
    for rows in _row_chains(xc_ref.shape[0]):
        x = _stream_rows(stream_kind, x_refs, is_prompt, rows)
        if stream_kind == "pending":
            refs[-1][rows, :] = x
        h = (_layernorm(x) * (1.0 + mod_ref[1:2, :]) + mod_ref[0:1, :]).astype(BF16)
        for i in range(N_FOURIER_GROUPS):
            r = _dot(h[:, i * g:(i + 1) * g], dcs_ref[...])
            xc_ref[rows, i * g:(i + 1) * g] = r[:, :g].astype(BF16)
            xs_ref[rows, i * g:(i + 1) * g] = r[:, g:].astype(BF16)


def _chan_dft(stream, mod, dcs, n_prompt, dec_seq, t_tok):
    d = mod.shape[-1]
    tl = _Tiles(n_prompt, dec_seq, TOKEN_TILE)
    tm = tl.tm
    pending = stream[0] == "pending"
    half = jax.ShapeDtypeStruct((t_tok, d), BF16)
    return pl.pallas_call(
        functools.partial(_chan_dft_kernel, n_prompt_tiles=tl.npt, stream_kind=stream[0]),
        out_shape=(half, half) + ((jax.ShapeDtypeStruct((t_tok, d), F32),) if pending else ()),
        grid=(t_tok // tm,),
        in_specs=tl.stream_specs(stream, t_tok // tm) + [
            pl.BlockSpec((None, N_MOD, d), tl.cond),
            pl.BlockSpec(dcs.shape, lambda t: (0, 0)),
        ],
        out_specs=(pl.BlockSpec((tm, d), tl.row),) * (3 if pending else 2),
        compiler_params=_params(("arbitrary",)),
        name="ln_chan_dft",
    )(*_stream_args(stream), mod, dcs)


def _seq_dft_kernel(cs_ref, ss_ref, xc_ref, xs_ref, o_ref, *, norm):
    f = _dot(cs_ref[...], xc_ref[...]) - _dot(ss_ref[...], xs_ref[...])
    o_ref[...] = (f * norm).astype(BF16)


def _seq_dft(cs, ss, xc, xs, batch, seq, row_offset):
    d = xc.shape[1]
    tm = min(QKV_TILE, seq)
    spt = seq // tm
    off_seq = row_offset // seq
    return pl.pallas_call(
        functools.partial(_seq_dft_kernel, norm=1.0 / math.sqrt(seq * FOURIER_GROUP_DIM)),
        out_shape=jax.ShapeDtypeStruct((batch * seq, d), BF16),
        grid=(batch, spt),
        in_specs=[
            pl.BlockSpec((tm, seq), lambda b, i: (i, 0)),
            pl.BlockSpec((tm, seq), lambda b, i: (i, 0)),
            pl.BlockSpec((seq, d), lambda b, i: (off_seq + b, 0)),
            pl.BlockSpec((seq, d), lambda b, i: (off_seq + b, 0)),
        ],
        out_specs=pl.BlockSpec((tm, d), lambda b, i: (b * spt + i, 0)),
        compiler_params=_params(("arbitrary", "arbitrary")),
        name=f"seq_dft_{seq}",
    )(cs, ss, xc, xs)


def _diff_lambda(lam_ref, lam_init):
    lv = lam_ref[...]
    return (jnp.exp(jnp.sum(lv[0:1] * lv[1:2], axis=-1, keepdims=True))
            - jnp.exp(jnp.sum(lv[2:3] * lv[3:4], axis=-1, keepdims=True)) + lam_init)


def _diff_attn_head(q, k, v_ext, lam, gain, lam_init):
    tq = q.shape[0]
    lane = lax.broadcasted_iota(I32, q.shape, 1)
    zero = jnp.zeros_like(q)
    qq = jnp.concatenate([jnp.where(lane < HEAD_DIM, q, zero), jnp.where(lane >= HEAD_DIM, q, zero)], axis=0)
    parts = []
    for r in range(0, 2 * tq, ATTN_ROW_CHUNK):
        s = lax.dot_general(qq[r:r + ATTN_ROW_CHUNK], k, (((1,), (1,)), ((), ())), preferred_element_type=F32)
        e = jnp.exp2(s - jnp.max(s, axis=-1, keepdims=True)).astype(BF16)
        parts.append(_dot(e, v_ext))
    oe = jnp.concatenate(parts, axis=0)
    o = oe[:, :V_DIM] / oe[:, V_DIM:]
    o = o[:tq] - lam * o[tq:]
    o = o * lax.rsqrt(jnp.mean(o * o, axis=-1, keepdims=True) + LN_EPS)
    return o * gain * (1.0 - lam_init)


def _attn_prompt_kernel(lam_ref, gain_ref, q_ref, k_ref, v_ref, o_ref, *, lam_init):
    lam = _diff_lambda(lam_ref, lam_init)
    gain = gain_ref[...]
    ones = jnp.ones((k_ref.shape[0], V_DIM), BF16)
    for hd in range(N_HEADS):
        cols = slice(hd * V_DIM, (hd + 1) * V_DIM)
        v_ext = jnp.concatenate([v_ref[:, cols], ones], axis=1)
        o_ref[:, cols] = _diff_attn_head(q_ref[:, cols], k_ref[:, cols], v_ext, lam, gain, lam_init).astype(BF16)


def _attn_latent_kernel(lam_ref, gain_ref, q_ref, k_ref, v_ref, kc_ref, vc_ref, o_ref, kall_ref, vext_ref, *,
                        lam_init):
    n_new = k_ref.shape[0]

    @pl.when(pl.program_id(2) == 0)
    def _():
        kall_ref[:n_new, :] = k_ref[...]
        kall_ref[n_new:, :] = kc_ref[...]
        vext_ref[:n_new, :V_DIM] = v_ref[...]
        vext_ref[n_new:, :V_DIM] = vc_ref[...]
        vext_ref[:, V_DIM:] = jnp.ones((vext_ref.shape[0], V_DIM), BF16)

    o = _diff_attn_head(q_ref[...], kall_ref[...], vext_ref[...], _diff_lambda(lam_ref, lam_init), gain_ref[...],
                        lam_init)
    o_ref[...] = o.astype(BF16)


def _attn_prompt(lam_vecs, gain, q, k, v, batch, seq, lam_init):
    d = q.shape[1]
    blk = pl.BlockSpec((seq, d), lambda b: (b, 0))
    return pl.pallas_call(
        functools.partial(_attn_prompt_kernel, lam_init=lam_init),
        out_shape=jax.ShapeDtypeStruct((batch * seq, d), BF16),
        grid=(batch,),
        in_specs=[
            pl.BlockSpec(lam_vecs.shape, lambda b: (0, 0)),
            pl.BlockSpec(gain.shape, lambda b: (0, 0)),
            blk, blk, blk,
        ],
        out_specs=blk,
        compiler_params=_params(("arbitrary",)),
        name="diff_attn_ctx",
    )(lam_vecs, gain, q, k, v)


def _attn_sample(lam_vecs, gain, q, k, v, kc, vc, batch, seq, n_ctx, row_offset, lam_init):
    d = q.shape[1]
    tq = Q_TILE
    qpt = seq // tq
    off_seq = row_offset // seq
    off_tile = row_offset // tq
    qmap = lambda b, h, i: (off_tile + b * qpt + i, h)
    kmap = lambda b, h, i: (off_seq + b, h)
    cmap = lambda b, h, i: (b, h)
    return pl.pallas_call(
        functools.partial(_attn_latent_kernel, lam_init=lam_init),
        out_shape=jax.ShapeDtypeStruct((batch * seq, d), BF16),
        grid=(batch, N_HEADS, qpt),
        in_specs=[
            pl.BlockSpec(lam_vecs.shape, lambda b, h, i: (0, 0)),
            pl.BlockSpec(gain.shape, lambda b, h, i: (0, 0)),
            pl.BlockSpec((tq, V_DIM), qmap),
            pl.BlockSpec((seq, V_DIM), kmap),
            pl.BlockSpec((seq, V_DIM), kmap),
            pl.BlockSpec((n_ctx, V_DIM), cmap),
            pl.BlockSpec((n_ctx, V_DIM), cmap),
        ],
        out_specs=pl.BlockSpec((tq, V_DIM), lambda b, h, i: (b * qpt + i, h)),
        scratch_shapes=[pltpu.VMEM((seq + n_ctx, V_DIM), BF16), pltpu.VMEM((seq + n_ctx, 2 * V_DIM), BF16)],
        compiler_params=_params(("arbitrary", "arbitrary", "arbitrary")),
        name="diff_attn_latent",
    )(lam_vecs, gain, q, k, v, kc, vc)


def _route(lg):
    lane = lax.broadcasted_iota(I32, lg.shape, 1)
    lane_f = lane.astype(F32)
    neg = jnp.float32(-jnp.inf)
    big = jnp.float32(LANES)
    gl = jnp.where(lane < N_EXPERT_GROUPS, lg, neg)
    gmax = jnp.max(gl, axis=-1, keepdims=True)
    g_prob = 1.0 / jnp.sum(jnp.exp(gl - gmax), axis=-1, keepdims=True)
    g_idx = jnp.min(jnp.where(gl == gmax, lane_f, big), axis=-1, keepdims=True)
    lo = N_EXPERT_GROUPS + EXPERTS_PER_GROUP * g_idx
    el = jnp.where((lane_f >= lo) & (lane_f < lo + EXPERTS_PER_GROUP), lg, neg)
    m1 = jnp.max(el, axis=-1, keepdims=True)
    i1 = jnp.min(jnp.where(el == m1, lane_f, big), axis=-1, keepdims=True)
    el2 = jnp.where(lane_f == i1, neg, el)
    m2 = jnp.max(el2, axis=-1, keepdims=True)
    i2 = jnp.min(jnp.where(el2 == m2, lane_f, big), axis=-1, keepdims=True)
    t = jnp.exp(m2 - m1)
    w1 = g_prob / (1.0 + t)
    w2 = g_prob * t / (1.0 + t)
    out = jnp.where(lane == 0, i1 - N_EXPERT_GROUPS, 0.0)
    out = jnp.where(lane == 1, i2 - N_EXPERT_GROUPS, out)
    out = jnp.where(lane == 2, w1, out)
    out = jnp.where(lane == 3, w2, out)
    return out


def _pack_bf16_pairs(h):
    n = h.shape[1] // 2
    bits = lax.bitcast_convert_type(h.astype(BF16).astype(F32), jnp.uint32)
    return (bits[:, :n] >> 16) | bits[:, n:]


def _unpack_bf16_pairs(p):
    lo = lax.bitcast_convert_type(p << 16, F32)
    hi = lax.bitcast_convert_type(p & jnp.uint32(0xFFFF0000), F32)
    return jnp.concatenate([lo, hi], axis=1)


def _mix_out_kernel(*refs, n_prompt_tiles, stream_kind):
    n_x = _STREAM_REFS[stream_kind]
    ap_ref, as_ref, w_ref = refs[:3]
    x_refs = refs[3:3 + n_x]
    mod_ref, gb_ref, wr_ref, x1_ref, hp_ref, route_ref = refs[3 + n_x:]
    is_prompt = pl.program_id(0) < n_prompt_tiles
    for rows in _row_chains(x1_ref.shape[0]):
        a = jnp.where(is_prompt, ap_ref[rows, :], as_ref[rows, :])
        out = _dot(a, w_ref[...])
        x = _stream_rows(stream_kind, x_refs, is_prompt, rows)
        x1 = _deepnorm(x, mod_ref[2:3, :], out) * gb_ref[0:1, :] + gb_ref[1:2, :]
        x1_ref[rows, :] = x1
        h2 = _layernorm(x1) * (1.0 + mod_ref[4:5, :]) + mod_ref[3:4, :]
        hp_ref[rows, :] = _pack_bf16_pairs(h2)
        hi = h2.astype(BF16)
        lo = (h2 - hi.astype(F32)).astype(BF16)
        logits = _dot(hi, wr_ref[0]) + _dot(lo, wr_ref[0]) + _dot(hi, wr_ref[1])
        route_ref[rows, :] = _route(logits)


def _mix_out(a_prompt, a_sample, w, stream, mod, gb, wr, n_prompt, dec_seq, t_tok):
    d = w.shape[1]
    tl = _Tiles(n_prompt, dec_seq, TOKEN_TILE)
    tm = tl.tm
    return pl.pallas_call(
        functools.partial(_mix_out_kernel, n_prompt_tiles=tl.npt, stream_kind=stream[0]),
        out_shape=(
            jax.ShapeDtypeStruct((t_tok, d), F32),
            jax.ShapeDtypeStruct((t_tok, d // 2), jnp.uint32),
            jax.ShapeDtypeStruct((t_tok, LANES), F32),
        ),
        grid=(t_tok // tm,),
        in_specs=[
            pl.BlockSpec((tm, d), tl.prompt_part),
            pl.BlockSpec((tm, d), tl.latent_part),
            pl.BlockSpec(w.shape, lambda t: (0, 0)),
        ] + tl.stream_specs(stream, t_tok // tm) + [
            pl.BlockSpec((None, N_MOD, d), tl.cond),
            pl.BlockSpec(gb.shape, lambda t: (0, 0)),
            pl.BlockSpec(wr.shape, lambda t: (0, 0, 0)),
        ],
        out_specs=(
            pl.BlockSpec((tm, d), tl.row),
            pl.BlockSpec((tm, d // 2), tl.row),
            pl.BlockSpec((tm, LANES), tl.row),
        ),
        compiler_params=_params(("arbitrary",)),
        name="mix_out_postnorm_router",
    )(a_prompt, a_sample, w, *_stream_args(stream), mod, gb, wr)


def _lane_prefix_sum(x, lane):
    sh = 1
    while sh < LANES:
        x = x + jnp.where(lane >= sh, pltpu.roll(x, sh, 1), 0.0)
        sh *= 2
    return x


def _lane_suffix_min_exclusive(x, lane):
    big = float(LANES)
    y = jnp.where(lane + 1 < LANES, pltpu.roll(x, LANES - 1, 1), big)
    sh = 1
    while sh < LANES:
        y = jnp.minimum(y, jnp.where(lane + sh < LANES, pltpu.roll(y, LANES - sh, 1), big))
        sh *= 2
    return y


def _work_items(counts, starts, ends, lane8, n_moe_tiles):
    shift = int(math.log2(MOE_TILE))
    first_tile = (starts.astype(I32) >> shift).astype(F32)
    last_tile = ((ends.astype(I32) - 1) >> shift).astype(F32)
    n_it = jnp.where(counts > 0.0, last_tile - first_tile + 1.0, 0.0)
    it_end = _lane_prefix_sum(n_it, lane8)
    it_start = it_end - n_it
    total = it_end[0:1, LANES - 1:LANES]
    sub = lax.broadcasted_iota(I32, (LANES, LANES), 0)
    rows = lambda x: jnp.broadcast_to(x[0:1, :], (LANES, LANES))
    used = counts > 0.0
    weight_slot = ((_lane_prefix_sum(jnp.where(used, 1.0, 0.0), lane8) - 1.0).astype(I32) & 1).astype(F32)
    next_used = _lane_suffix_min_exclusive(jnp.where(used, lane8.astype(F32), float(LANES)), lane8)
    per_expert = (starts, ends, first_tile, it_start, it_end, weight_slot, next_used)
    stacked = jnp.zeros((LANES, LANES), F32)
    for j, vec in enumerate(per_expert):
        stacked = jnp.where(sub == j, rows(vec), stacked)
    cols = stacked.T
    col = lambda j: cols[:, j:j + 1]
    sub_f = sub.astype(F32)
    w = lax.broadcasted_iota(I32, (LANES, LANES), 1).astype(F32)
    ex = jnp.sum(jnp.where((sub < N_EXPERTS) & (col(4) <= w), 1.0, 0.0), axis=0, keepdims=True)
    ex = jnp.minimum(ex, N_EXPERTS - 1.0)
    w1 = w[0:1, :]
    valid = w1 < total
    ex = jnp.where(valid, ex, jnp.max(jnp.where(valid, ex, 0.0), axis=-1, keepdims=True))
    onehot = sub_f == ex
    pick = lambda j: jnp.sum(jnp.where(onehot, col(j), 0.0), axis=0, keepdims=True)
    tile = jnp.where(valid, pick(2) + (w1 - pick(3)), n_moe_tiles - 1.0)
    lo = jnp.where(valid, jnp.maximum(pick(0), tile * MOE_TILE), 0.0)
    hi = jnp.where(valid, jnp.minimum(pick(1), (tile + 1.0) * MOE_TILE), 0.0)
    b8 = lambda x: jnp.broadcast_to(x, (SUBLANES, LANES))
    ex8, tile8 = b8(ex), b8(tile)
    first = jnp.where((lane8 == 0) | (tile8 != pltpu.roll(tile8, 1, 1)), 1.0, 0.0)
    newexp = jnp.where((lane8 == 0) | (ex8 != pltpu.roll(ex8, 1, 1)), 1.0, 0.0)
    sub8 = lax.broadcasted_iota(I32, (SUBLANES, LANES), 0)
    table = jnp.zeros((SUBLANES, LANES), F32)
    for j, vec in enumerate((ex8, tile8, b8(lo), b8(hi), first, newexp, b8(pick(5)), b8(pick(6)))):
        table = jnp.where(sub8 == j, vec, table)
    return table


def _plan_kernel(route_ref, pos_ref, items_ref, tri_ref, carry_ref, tot_ref, *, n_moe_tiles):
    p = pl.program_id(0)
    t = pl.program_id(1)
    tm = route_ref.shape[0]
    r = route_ref[...]
    lane = lax.broadcasted_iota(I32, (tm, LANES), 1)
    lane_f = lane.astype(F32)
    e0 = r[:, 0:1]
    e1 = r[:, 1:2] + N_EXPERTS
    m = jnp.where((lane_f == e0) | (lane_f == e1), 1.0, 0.0)
    colsum = jnp.sum(m, axis=0, keepdims=True)

    @pl.when((p == 0) & (t == 0))
    def _():
        carry_ref[...] = jnp.zeros_like(carry_ref)
        row = lax.broadcasted_iota(I32, (tm, tm), 0)
        col = lax.broadcasted_iota(I32, (tm, tm), 1)
        tri_ref[...] = jnp.where(row > col, 1.0, 0.0).astype(BF16)

    @pl.when(p == 0)
    def _():
        carry_ref[...] += colsum

    @pl.when((p == 0) & (t == pl.num_programs(1) - 1))
    def _():
        tot_ref[...] = carry_ref[...]
        carry_ref[...] = jnp.zeros_like(carry_ref)

    @pl.when(p == 1)
    def _():
        lane8 = lax.broadcasted_iota(I32, (SUBLANES, LANES), 1)
        tot = tot_ref[...]
        is_first = lane8 < N_EXPERTS
        tot0 = jnp.where(is_first, tot, 0.0)
        counts = jnp.where(is_first, tot + pltpu.roll(tot, LANES - N_EXPERTS, 1), 0.0)
        ends = _lane_prefix_sum(counts, lane8)
        starts = ends - counts
        base = jnp.where(is_first, starts, pltpu.roll(starts + tot0, N_EXPERTS, 1))
        before = _dot(tri_ref[...], m.astype(BF16)) + carry_ref[0:1, :]
        carry_ref[...] += colsum
        rows = before + base[0:1, :]
        pos0 = jnp.sum(jnp.where(lane_f == e0, rows, 0.0), axis=-1, keepdims=True)
        pos1 = jnp.sum(jnp.where(lane_f == e1, rows, 0.0), axis=-1, keepdims=True)
        both = jnp.where(lane == 0, pos0, jnp.where(lane == 1, pos1, 0.0))
        pos_ref[...] = both.T[0:SUBLANES, :].astype(I32)

        @pl.when(t == 0)
        def _():
            items_ref[...] = _work_items(counts, starts, ends, lane8, n_moe_tiles).astype(I32)


def _moe_plan(route):
    t_tok = route.shape[0]
    tm = PLAN_TILE
    n_moe_tiles = 2 * t_tok // MOE_TILE
    assert n_moe_tiles + N_EXPERTS - 1 <= LANES and 2 * N_EXPERTS <= LANES and MOE_TILE & (MOE_TILE - 1) == 0
    return pl.pallas_call(
        functools.partial(_plan_kernel, n_moe_tiles=n_moe_tiles),
        out_shape=(jax.ShapeDtypeStruct((SUBLANES, t_tok), I32), jax.ShapeDtypeStruct((SUBLANES, LANES), I32)),
        grid=(2, t_tok // tm),
        in_specs=[pl.BlockSpec((tm, LANES), lambda p, t: (t, 0))],
        out_specs=(pl.BlockSpec((SUBLANES, tm), lambda p, t: (0, t * p)),
                   pl.BlockSpec((SUBLANES, LANES), lambda p, t: (0, 0))),
        scratch_shapes=[pltpu.VMEM((tm, tm), BF16), pltpu.VMEM((SUBLANES, LANES), F32),
                        pltpu.VMEM((SUBLANES, LANES), F32)],
        compiler_params=_params(("arbitrary", "arbitrary")),
        name="moe_positions",
    )(route)


def _sc_workers():
    info = plsc.get_sparse_core_info()
    return info.num_cores, info.num_cores * info.num_subcores


def _sc_pipeline(n_chunks, loads, stores):
    def start(copies):
        for cp in copies:
            cp.start()

    def wait(copies):
        for cp in copies:
            cp.wait()

    start(loads(0, 0))
    for j in range(n_chunks):
        b = j % 2
        if j + 1 < n_chunks:
            if j >= 1:
                wait(stores(j - 1, 1 - b))
            start(loads(j + 1, 1 - b))
        wait(loads(j, b))
        start(stores(j, b))
    if n_chunks >= 2:
        wait(stores(n_chunks - 2, n_chunks % 2))
    wait(stores(n_chunks - 1, (n_chunks - 1) % 2))


def _sc_chunking(n_rows, d):
    n_cores, n_workers = _sc_workers()
    per_worker = n_rows // n_workers
    k = SC_BUFFER_BYTES // (d * 4)
    n_chunks = per_worker // k
    assert n_chunks * k * n_workers == n_rows and k <= SC_MAX_INDEX_CHUNK
    return n_cores, n_workers, per_worker, k, n_chunks


def _sc_gather_rows(table, idx):
    n, d = idx.shape[0], table.shape[1]
    n_cores, n_workers, per_worker, k, n_chunks = _sc_chunking(n, d)

    def body(table_hbm, idx_hbm, out_hbm, idx_v, rows_v, gsem, osem):
        wid = lax.axis_index("s") * n_cores + lax.axis_index("c")
        base = wid * per_worker
        pltpu.sync_copy(idx_hbm.at[wid], idx_v)
        _sc_pipeline(
            n_chunks,
            lambda j, b: [pltpu.make_async_copy(table_hbm.at[idx_v.at[j]], rows_v.at[b], gsem.at[b])],
            lambda j, b: [pltpu.make_async_copy(rows_v.at[b], out_hbm.at[pl.ds(base + j * k, k)], osem.at[b])])

    return pl.kernel(
        body,
        out_type=jax.ShapeDtypeStruct((n, d), table.dtype),
        mesh=plsc.VectorSubcoreMesh(core_axis_name="c", subcore_axis_name="s"),
        scratch_types=[pltpu.VMEM((n_chunks, k), I32), pltpu.VMEM((2, k, d), table.dtype),
                       pltpu.SemaphoreType.DMA((2,)), pltpu.SemaphoreType.DMA((2,))],
        name="sc_row_gather",
    )(table, idx.reshape(n_workers, n_chunks, k))


def _sc_scatter_rows(src, idx):
    n_lists, n_src = idx.shape
    d = src.shape[1]
    n_cores, n_workers, per_worker, k, n_chunks = _sc_chunking(n_src, d)

    def body(src_hbm, idx_hbm, out_hbm, idx_v, rows_v, gsem, osem):
        wid = lax.axis_index("s") * n_cores + lax.axis_index("c")
        base = wid * per_worker
        pltpu.sync_copy(idx_hbm.at[wid], idx_v)
        _sc_pipeline(
            n_chunks,
            lambda j, b: [pltpu.make_async_copy(src_hbm.at[pl.ds(base + j * k, k)], rows_v.at[b], gsem.at[b])],
            lambda j, b: [pltpu.make_async_copy(rows_v.at[b], out_hbm.at[idx_v.at[c, j]], osem.at[b, c])
                          for c in range(n_lists)])

    return pl.kernel(
        body,
        out_type=jax.ShapeDtypeStruct((n_lists * n_src, d), src.dtype),
        mesh=plsc.VectorSubcoreMesh(core_axis_name="c", subcore_axis_name="s"),
        scratch_types=[pltpu.VMEM((n_lists, n_chunks, k), I32), pltpu.VMEM((2, k, d), src.dtype),
                       pltpu.SemaphoreType.DMA((2,)), pltpu.SemaphoreType.DMA((2, n_lists))],
        name="sc_row_scatter",
    )(src, idx.reshape(n_lists, n_workers, n_chunks, k).transpose(1, 0, 2, 3))


IT_EXPERT, IT_TILE, IT_LO, IT_HI, IT_FIRST, IT_NEWEXP, IT_SLOT, IT_NEXT = range(8)


def _moe_kernel(it_ref, xs_ref, wg_hbm, wu_hbm, wd_hbm, o_ref, wg_f, wu_f, wd_f, sem, *, layer):
    w = pl.program_id(0)
    f = wg_f.shape[2]

    def weight_copies(expert, slot):
        return [pltpu.make_async_copy(src.at[layer, expert], dst.at[slot], sem.at[slot, j])
                for j, (src, dst) in enumerate(((wg_hbm, wg_f), (wu_hbm, wu_f), (wd_hbm, wd_f)))]

    @pl.when(it_ref[IT_NEWEXP, w] == 1)
    def _():
        expert = it_ref[IT_EXPERT, w]
        slot = it_ref[IT_SLOT, w]
        nxt = it_ref[IT_NEXT, w]

        @pl.when(w == 0)
        def _():
            for cp in weight_copies(expert, slot):
                cp.start()

        for cp in weight_copies(expert, slot):
            cp.wait()

        @pl.when(nxt < N_EXPERTS)
        def _():
            for cp in weight_copies(nxt, 1 - slot):
                cp.start()

    slot = it_ref[IT_SLOT, w]
    tm = xs_ref.shape[0]
    x = _unpack_bf16_pairs(xs_ref[...]).astype(BF16)
    a = _dot(x, wg_f[slot].astype(BF16))
    u = _dot(x, wu_f[slot].astype(BF16))
    act = (_silu(a) * u).astype(BF16)
    y = _pack_bf16_pairs(_dot(act, wd_f[slot].astype(BF16)))
    row = it_ref[IT_TILE, w] * tm + lax.broadcasted_iota(I32, (tm, 1), 0)
    mine = (row >= it_ref[IT_LO, w]) & (row < it_ref[IT_HI, w])

    @pl.when(it_ref[IT_FIRST, w] == 1)
    def _():
        o_ref[...] = jnp.where(mine, y, jnp.zeros_like(y))

    @pl.when(it_ref[IT_FIRST, w] == 0)
    def _():
        o_ref[...] = jnp.where(mine, y, o_ref[...])


def _moe_experts(items, xs, w_gate, w_up, w_down, layer):
    n_rows, dp = xs.shape
    _, _, d, f = w_gate.shape
    tm = MOE_TILE
    rmap = lambda w, it: (it[IT_TILE, w], 0)
    hbm = pl.BlockSpec(memory_space=pl.ANY)
    return pl.pallas_call(
        functools.partial(_moe_kernel, layer=layer),
        out_shape=jax.ShapeDtypeStruct((n_rows, d // 2), jnp.uint32),
        grid_spec=pltpu.PrefetchScalarGridSpec(
            num_scalar_prefetch=1,
            grid=(n_rows // tm + N_EXPERTS - 1,),
            in_specs=[pl.BlockSpec((tm, dp), rmap), hbm, hbm, hbm],
            out_specs=pl.BlockSpec((tm, d // 2), rmap),
            scratch_shapes=[
                pltpu.VMEM((2, d, f), F32), pltpu.VMEM((2, d, f), F32), pltpu.VMEM((2, f, d), F32),
                pltpu.SemaphoreType.DMA((2, 3)),
            ],
        ),
        compiler_params=_params(("arbitrary",)),
        name="moe_grouped_mlp",
    )(items, xs, w_gate, w_up, w_down)


def _final_postnorm_kernel(*refs, n_prompt_tiles):
    x2 = _finish_moe(*refs[:6])
    out_prompt, out_latent = refs[6:]

    @pl.when(pl.program_id(0) < n_prompt_tiles)
    def _():
        out_prompt[...] = x2

    @pl.when(pl.program_id(0) >= n_prompt_tiles)
    def _():
        out_latent[...] = x2


def _final_postnorm(stream, n_prompt, dec_seq, t_tok):
    d = stream[1][0].shape[1]
    tl = _Tiles(n_prompt, dec_seq, TOKEN_TILE)
    tm = tl.tm
    return pl.pallas_call(
        functools.partial(_final_postnorm_kernel, n_prompt_tiles=tl.npt),
        out_shape=(jax.ShapeDtypeStruct((n_prompt, d), F32), jax.ShapeDtypeStruct((t_tok - n_prompt, d), F32)),
        grid=(t_tok // tm,),
        in_specs=tl.stream_specs(stream, t_tok // tm),
        out_specs=(pl.BlockSpec((tm, d), tl.prompt_part), pl.BlockSpec((tm, d), tl.latent_part)),
        compiler_params=_params(("arbitrary",)),
        name="moe_combine_postnorm",
    )(*_stream_args(stream))


def _rope_tables(n_lat):
    nf = HEAD_DIM // 4
    s = np.arange(n_lat)
    lane = np.arange(LANES)
    inv = ROPE_BASE ** (-(lane % nf).astype(np.float64) / nf)
    use_col = (lane % HEAD_DIM) >= HEAD_DIM // 2
    p = np.where(use_col[None, :], (s % GRID_W)[:, None], (s // GRID_W)[:, None]).astype(np.float64)
    ang = p * inv[None, :]
    sign = np.where((lane % (2 * nf)) < nf, -1.0, 1.0)
    return jnp.asarray(np.cos(ang), F32), jnp.asarray(np.sin(ang) * sign[None, :], F32)


def _dft_tables(n):
    k = np.arange(n)
    ang = 2.0 * np.pi * ((k[:, None] * k[None, :]) % n).astype(np.float64) / n
    return np.cos(ang), np.sin(ang)


def kernel(x_prompt, x_sample, cache_k, cache_v, c, c_ctx, w_ada, b_ada, ln_gain, ln_bias, w_qkv, w_attn_out,
           lambda_q1, lambda_k1, lambda_q2, lambda_k2, subln_gain, w_fourier_out, w_router_group,
           w_router_expert, w_expert_gate, w_expert_up, w_expert_down):
    bp, sp, d = x_prompt.shape
    bs, n_lat, _ = x_sample.shape
    n_ctx = cache_k.shape[2]
    n_prompt = bp * sp
    t_tok = n_prompt + bs * n_lat
    assert d == D_MODEL and n_prompt % n_lat == 0 and n_lat % TOKEN_TILE == 0 and sp == QKV_TILE

    cond = jnp.concatenate([c_ctx[None, :], c, jnp.zeros((SUBLANES - 1 - bs, d), F32)], axis=0)
    mods = _ada_all(cond, w_ada, b_ada).reshape(DEPTH, SUBLANES, N_MOD, d)

    cos, sin = _rope_tables(n_lat)
    cc, sc = _dft_tables(FOURIER_GROUP_DIM)
    dcs = jnp.asarray(np.concatenate([cc, sc], axis=1), BF16)
    seq_tabs = {s: tuple(jnp.asarray(m, BF16) for m in _dft_tables(s)) for s in (sp, n_lat)}
    attn_layers = [i for i in range(DEPTH) if i % 2 == 0]

    stream = ("pair", (x_prompt.reshape(n_prompt, d), x_sample.reshape(bs * n_lat, d)))
    prev_kv = []
    new_k = new_v = None
    for i in range(DEPTH):
        mod = mods[i]
        if i % 2 == 0:
            a = i // 2
            lam_init = 0.8 - 0.6 * math.exp(-0.3 * i)
            last_attn = i == attn_layers[-1]
            q, k, v, kf, vf, *finished = _qkv(stream, mod, w_qkv[a].astype(BF16), cos, sin, prev_kv, last_attn,
                                              n_prompt, n_lat, t_tok)
            if last_attn:
                new_k, new_v = kf, vf
            else:
                prev_kv.append((kf, vf))
            lam_vecs = jnp.stack([lambda_q1[a], lambda_k1[a], lambda_q2[a], lambda_k2[a]], axis=0)
            gain = subln_gain[a][None, :]
            kc = cache_k[:, a].reshape(bs * n_ctx, d).astype(BF16)
            vc = cache_v[:, a].reshape(bs * n_ctx, d).astype(BF16)
            mixed_p = _attn_prompt(lam_vecs, gain, q, k, v, bp, sp, lam_init)
            mixed_s = _attn_sample(lam_vecs, gain, q, k, v, kc, vc, bs, n_lat, n_ctx, n_prompt, lam_init)
            w_mix = w_attn_out[a]
        else:
            xc, xsn, *finished = _chan_dft(stream, mod, dcs, n_prompt, n_lat, t_tok)
            mixed_p = _seq_dft(*seq_tabs[sp], xc, xsn, bp, sp, 0)
            mixed_s = _seq_dft(*seq_tabs[n_lat], xc, xsn, bs, n_lat, n_prompt)
            w_mix = w_fourier_out[i // 2]
        if finished:
            stream = ("merged", (finished[0],))
        gb0 = jnp.stack([ln_gain[i, 0], ln_bias[i, 0]], axis=0)
        gb1 = jnp.stack([ln_gain[i, 1], ln_bias[i, 1]], axis=0)
        wr = jnp.concatenate([w_router_group[i], w_router_expert[i],
                              jnp.zeros((d, LANES - N_EXPERT_GROUPS - N_EXPERTS), F32)], axis=1)
        wr_hi = wr.astype(BF16)
        wr_lo = (wr - wr_hi.astype(F32)).astype(BF16)
        x1, hp, route = _mix_out(mixed_p, mixed_s, w_mix.astype(BF16), stream, mod, gb0,
                                 jnp.stack([wr_hi, wr_lo], axis=0), n_prompt, n_lat, t_tok)
        pos8, items = _moe_plan(route)
        pos = pos8[0:2]
        xs_sorted = _sc_scatter_rows(hp, pos)
        ys = _moe_experts(items, xs_sorted, w_expert_gate, w_expert_up, w_expert_down, i)
        yg = _sc_gather_rows(ys, pos.reshape(-1))
        stream = ("pending", (x1, yg, route, mods[i], gb1))

    out_prompt, out_latent = _final_postnorm(stream, n_prompt, n_lat, t_tok)
    y_prompt = out_prompt.reshape(bp, sp, d)
    y_sample = out_latent.reshape(bs, n_lat, d)
    return (y_prompt, y_sample, new_k.reshape(bp, len(attn_layers), sp, N_HEADS, 2 * HEAD_DIM),
            new_v.reshape(bp, len(attn_layers), sp, N_HEADS, V_DIM))
```

```python
import functools
import math

import numpy as np
import jax
import jax.numpy as jnp
from jax import lax
from jax.experimental import pallas as pl
from jax.experimental.pallas import tpu as pltpu
from jax.experimental.pallas import tpu_sc as plsc

F32 = jnp.float32
BF16 = jnp.bfloat16
I32 = jnp.int32

D_MODEL = 1024
DEPTH = 4
GRID_W = 64
N_HEADS = 8
HEAD_DIM = 64
V_DIM = 2 * HEAD_DIM
ROPE_BASE = 10000.0
N_FOURIER_GROUPS = 8
FOURIER_GROUP_DIM = D_MODEL // N_FOURIER_GROUPS
N_EXPERT_GROUPS = 4
EXPERTS_PER_GROUP = 8
N_EXPERTS = N_EXPERT_GROUPS * EXPERTS_PER_GROUP
D_EXPERT = 256
N_MOD = 6
LN_EPS = 1e-5
DEEPNORM_ALPHA = (2.0 * DEPTH) ** 0.25
Q_SCALE = math.log2(math.e) * HEAD_DIM ** -0.5

LANES = 128
SUBLANES = 8
TOKEN_TILE = 512
CHAIN_ROWS = 256
QKV_TILE = 256
Q_TILE = 1024
ATTN_ROW_CHUNK = 128
MOE_TILE = 512
PLAN_TILE = 1024
SC_BUFFER_BYTES = 128 * 1024
SC_MAX_INDEX_CHUNK = 128
VMEM_LIMIT = 48 * 1024 * 1024


def _params(semantics):
    return pltpu.CompilerParams(dimension_semantics=semantics, vmem_limit_bytes=VMEM_LIMIT)


def _layernorm(x, eps=LN_EPS):
    mu = jnp.mean(x, axis=-1, keepdims=True)
    xc = x - mu
    var = jnp.mean(xc * xc, axis=-1, keepdims=True)
    return xc * lax.rsqrt(var + eps)


def _deepnorm(x, branch_gate, branch):
    return _layernorm(x + (branch_gate * (1.0 / DEEPNORM_ALPHA)) * branch, LN_EPS / DEEPNORM_ALPHA ** 2)


def _silu(a):
    return a / (1.0 + jnp.exp(-a))


def _dot(a, b):
    return jnp.dot(a, b, preferred_element_type=F32)


class _Tiles:
    def __init__(self, n_prompt, dec_seq, tm):
        self.tm = tm
        self.npt = n_prompt // tm
        self.tps = dec_seq // tm

    def row(self, t):
        return (t, 0)

    def cond(self, t):
        return (jnp.where(t < self.npt, 0, (t - self.npt) // self.tps + 1), 0, 0)

    def latent_pos(self, t):
        return (jnp.maximum(t - self.npt, 0) % self.tps, 0)

    def prompt_part(self, t):
        return (jnp.minimum(t, self.npt - 1), 0)

    def latent_part(self, t):
        return (jnp.maximum(t - self.npt, 0), 0)

    def stream_specs(self, stream, n_tiles):
        kind, arrays = stream
        tm = self.tm
        if kind == "merged":
            return [pl.BlockSpec((tm, arrays[0].shape[1]), self.row)]
        if kind == "pair":
            return [pl.BlockSpec((tm, arrays[0].shape[1]), self.prompt_part),
                    pl.BlockSpec((tm, arrays[1].shape[1]), self.latent_part)]
        x1, yg, route, mod, gb = arrays
        return [pl.BlockSpec((tm, x1.shape[1]), self.row),
                pl.BlockSpec((tm, yg.shape[1]), self.row),
                pl.BlockSpec((tm, yg.shape[1]), lambda t: (t + n_tiles, 0)),
                pl.BlockSpec((tm, route.shape[1]), self.row),
                pl.BlockSpec((None,) + mod.shape[1:], self.cond),
                pl.BlockSpec(gb.shape, lambda t: (0, 0))]


_STREAM_REFS = {"merged": 1, "pair": 2, "pending": 6}


def _stream_args(stream):
    kind, arrays = stream
    if kind == "pending":
        x1, yg, route, mod, gb = arrays
        return [x1, yg, yg, route, mod, gb]
    return list(arrays)


def _finish_moe(x1_ref, y0_ref, y1_ref, route_ref, mod_ref, gb_ref, rows=slice(None)):
    r = route_ref[rows, :]
    moe = r[:, 2:3] * _unpack_bf16_pairs(y0_ref[rows, :]) + r[:, 3:4] * _unpack_bf16_pairs(y1_ref[rows, :])
    return _deepnorm(x1_ref[rows, :], mod_ref[5:6, :], moe) * gb_ref[0:1, :] + gb_ref[1:2, :]


def _stream_rows(kind, refs, is_prompt_tile, rows=slice(None)):
    if kind == "merged":
        return refs[0][rows, :]
    if kind == "pair":
        return jnp.where(is_prompt_tile, refs[0][rows, :], refs[1][rows, :])
    return _finish_moe(*refs, rows)


def _ada_kernel(cond_ref, w_ref, b_ref, o_ref):
    a = _silu(cond_ref[...])
    o_ref[...] = _dot(a.astype(BF16), w_ref[...].astype(BF16)) + b_ref[...]


def _ada_all(cond, w_ada, b_ada):
    depth, d, n = w_ada.shape
    tn = n // 4
    return pl.pallas_call(
        _ada_kernel,
        out_shape=jax.ShapeDtypeStruct((depth, cond.shape[0], n), F32),
        grid=(depth, n // tn),
        in_specs=[
            pl.BlockSpec(cond.shape, lambda l, j: (0, 0)),
            pl.BlockSpec((None, d, tn), lambda l, j: (l, 0, j)),
            pl.BlockSpec((None, 1, tn), lambda l, j: (l, 0, j)),
        ],
        out_specs=pl.BlockSpec((None, cond.shape[0], tn), lambda l, j: (l, 0, j)),
        compiler_params=_params(("arbitrary", "arbitrary")),
        name="ada",
    )(cond, w_ada, b_ada.reshape(depth, 1, n))


def _rope(x, cos, sin_signed, first_half):
    outs = []
    for c in range(x.shape[1] // LANES):
        xc = x[:, c * LANES:(c + 1) * LANES]
        partner = jnp.where(first_half, pltpu.roll(xc, LANES - 16, 1), pltpu.roll(xc, 16, 1))
        outs.append(xc * cos + partner * sin_signed)
    return jnp.concatenate(outs, axis=1)


def _store_heads(cache_ref, slot, rows):
    cache_ref[slot] = pltpu.einshape("s(hd)->shd", rows, h=N_HEADS)


def _qkv_kernel(*refs, n_prompt_tiles, stream_kind, n_prev):
    n_x = _STREAM_REFS[stream_kind]
    x_refs, (mod_ref, w_ref, cos_ref, sin_ref) = refs[:n_x], refs[n_x:n_x + 4]
    prev = refs[n_x + 4:n_x + 4 + 2 * max(n_prev, 0)]
    q_ref, k_ref, v_ref, ko_ref, vo_ref = refs[n_x + 4 + 2 * max(n_prev, 0):][:5]
    t = pl.program_id(0)
    x = _stream_rows(stream_kind, x_refs, t < n_prompt_tiles)
    if stream_kind == "pending":
        refs[-1][...] = x
    h = (_layernorm(x) * (1.0 + mod_ref[1:2, :]) + mod_ref[0:1, :]).astype(BF16)
    d = w_ref.shape[0]
    project = lambda j: _dot(h, w_ref[:, j * d:(j + 1) * d])

    @pl.when(t < n_prompt_tiles)
    def _():
        q_ref[...] = (project(0) * Q_SCALE).astype(BF16)
        k = project(1)
        k_ref[...] = k.astype(BF16)
        v = project(2)
        v_ref[...] = v.astype(BF16)
        if n_prev < 0:
            ko_ref[...] = k
            vo_ref[...] = v
        else:
            for a in range(n_prev):
                _store_heads(ko_ref, a, prev[2 * a][...])
                _store_heads(vo_ref, a, prev[2 * a + 1][...])
            _store_heads(ko_ref, n_prev, k)
            _store_heads(vo_ref, n_prev, v)

    @pl.when(t >= n_prompt_tiles)
    def _():
        lane = lax.broadcasted_iota(I32, (h.shape[0], LANES), 1)
        first_half = (lane % 32) < 16
        cos = cos_ref[...]
        sin = sin_ref[...]
        q_ref[...] = _rope(project(0) * Q_SCALE, cos, sin, first_half).astype(BF16)
        k_ref[...] = _rope(project(1), cos, sin, first_half).astype(BF16)
        v_ref[...] = project(2).astype(BF16)


def _qkv(stream, mod, w_qkv, cos, sin, prev_kv, finish_cache, n_prompt, dec_seq, t_tok):
    d = w_qkv.shape[0]
    tl = _Tiles(n_prompt, dec_seq, QKV_TILE)
    tm = tl.tm
    pending = stream[0] == "pending"
    row_out = jax.ShapeDtypeStruct((t_tok, d), F32), pl.BlockSpec((tm, d), tl.row)
    n_prev = len(prev_kv) if finish_cache else -1
    if finish_cache:
        n_slots = n_prev + 1
        kv_shape = jax.ShapeDtypeStruct((n_prompt // tm, n_slots, tm, N_HEADS, V_DIM), F32)
        kv_spec = pl.BlockSpec((None, n_slots, tm, N_HEADS, V_DIM), lambda t: (tl.prompt_part(t)[0], 0, 0, 0, 0))
    else:
        kv_shape = jax.ShapeDtypeStruct((n_prompt, d), F32)
        kv_spec = pl.BlockSpec((tm, d), tl.prompt_part)
    prev_flat = [a for kv in prev_kv for a in kv] if finish_cache else []
    return pl.pallas_call(
        functools.partial(_qkv_kernel, n_prompt_tiles=tl.npt, stream_kind=stream[0], n_prev=n_prev),
        out_shape=(
            jax.ShapeDtypeStruct((t_tok, d), BF16),
            jax.ShapeDtypeStruct((t_tok, d), BF16),
            jax.ShapeDtypeStruct((t_tok, d), BF16),
            kv_shape, kv_shape,
        ) + ((row_out[0],) if pending else ()),
        grid=(t_tok // tm,),
        in_specs=tl.stream_specs(stream, t_tok // tm) + [
            pl.BlockSpec((None, N_MOD, d), tl.cond),
            pl.BlockSpec(w_qkv.shape, lambda t: (0, 0)),
            pl.BlockSpec((tm, LANES), tl.latent_pos),
            pl.BlockSpec((tm, LANES), tl.latent_pos),
        ] + [pl.BlockSpec((tm, d), tl.prompt_part) for _ in prev_flat],
        out_specs=(
            pl.BlockSpec((tm, d), tl.row),
            pl.BlockSpec((tm, d), tl.row),
            pl.BlockSpec((tm, d), tl.row),
            kv_spec, kv_spec,
        ) + ((row_out[1],) if pending else ()),
        compiler_params=_params(("arbitrary",)),
        name="ln_qkv_rope",
    )(*_stream_args(stream), mod, w_qkv, cos, sin, *prev_flat)


def _row_chains(n_rows):
    return [slice(r, r + CHAIN_ROWS) for r in range(0, n_rows, CHAIN_ROWS)]


def _chan_dft_kernel(*refs, n_prompt_tiles, stream_kind):
    n_x = _STREAM_REFS[stream_kind]
    x_refs, (mod_ref, dcs_ref, xc_ref, xs_ref) = refs[:n_x], refs[n_x:n_x + 4]
    g = FOURIER_GROUP_DIM
    is_prompt = pl.program_id(0) < n_prompt_tiles
    for rows in _row_chains(xc_ref.shape[0]):
        x = _stream_rows(stream_kind, x_refs, is_prompt, rows)
        if stream_kind == "pending":
            refs[-1][rows, :] = x
        h = (_layernorm(x) * (1.0 + mod_ref[1:2, :]) + mod_ref[0:1, :]).astype(BF16)
        for i in range(N_FOURIER_GROUPS):
            r = _dot(h[:, i * g:(i + 1) * g], dcs_ref[...])
            xc_ref[rows, i * g:(i + 1) * g] = r[:, :g].astype(BF16)
            xs_ref[rows, i * g:(i + 1) * g] = r[:, g:].astype(BF16)


def _chan_dft(stream, mod, dcs, n_prompt, dec_seq, t_tok):
    d = mod.shape[-1]
    tl = _Tiles(n_prompt, dec_seq, TOKEN_TILE)
    tm = tl.tm
    pending = stream[0] == "pending"
    half = jax.ShapeDtypeStruct((t_tok, d), BF16)
    return pl.pallas_call(
        functools.partial(_chan_dft_kernel, n_prompt_tiles=tl.npt, stream_kind=stream[0]),
        out_shape=(half, half) + ((jax.ShapeDtypeStruct((t_tok, d), F32),) if pending else ()),
        grid=(t_tok // tm,),
        in_specs=tl.stream_specs(stream, t_tok // tm) + [
            pl.BlockSpec((None, N_MOD, d), tl.cond),
            pl.BlockSpec(dcs.shape, lambda t: (0, 0)),
        ],
        out_specs=(pl.BlockSpec((tm, d), tl.row),) * (3 if pending else 2),
        compiler_params=_params(("arbitrary",)),
        name="ln_chan_dft",
    )(*_stream_args(stream), mod, dcs)


def _seq_dft_kernel(cs_ref, ss_ref, xc_ref, xs_ref, o_ref, *, norm):
    f = _dot(cs_ref[...], xc_ref[...]) - _dot(ss_ref[...], xs_ref[...])
    o_ref[...] = (f * norm).astype(BF16)


def _seq_dft(cs, ss, xc, xs, batch, seq, row_offset):
    d = xc.shape[1]
    tm = min(QKV_TILE, seq)
    spt = seq // tm
    off_seq = row_offset // seq
    return pl.pallas_call(
        functools.partial(_seq_dft_kernel, norm=1.0 / math.sqrt(seq * FOURIER_GROUP_DIM)),
        out_shape=jax.ShapeDtypeStruct((batch * seq, d), BF16),
        grid=(batch, spt),
        in_specs=[
            pl.BlockSpec((tm, seq), lambda b, i: (i, 0)),
            pl.BlockSpec((tm, seq), lambda b, i: (i, 0)),
            pl.BlockSpec((seq, d), lambda b, i: (off_seq + b, 0)),
            pl.BlockSpec((seq, d), lambda b, i: (off_seq + b, 0)),
        ],
        out_specs=pl.BlockSpec((tm, d), lambda b, i: (b * spt + i, 0)),
        compiler_params=_params(("arbitrary", "arbitrary")),
        name=f"seq_dft_{seq}",
    )(cs, ss, xc, xs)


def _diff_lambda(lam_ref, lam_init):
    lv = lam_ref[...]
    return (jnp.exp(jnp.sum(lv[0:1] * lv[1:2], axis=-1, keepdims=True))
            - jnp.exp(jnp.sum(lv[2:3] * lv[3:4], axis=-1, keepdims=True)) + lam_init)


def _diff_attn_head(q, k, v_ext, lam, gain, lam_init):
    tq = q.shape[0]
    lane = lax.broadcasted_iota(I32, q.shape, 1)
    zero = jnp.zeros_like(q)
    qq = jnp.concatenate([jnp.where(lane < HEAD_DIM, q, zero), jnp.where(lane >= HEAD_DIM, q, zero)], axis=0)
    parts = []
    for r in range(0, 2 * tq, ATTN_ROW_CHUNK):
        s = lax.dot_general(qq[r:r + ATTN_ROW_CHUNK], k, (((1,), (1,)), ((), ())), preferred_element_type=F32)
        e = jnp.exp2(s - jnp.max(s, axis=-1, keepdims=True)).astype(BF16)
        parts.append(_dot(e, v_ext))
    oe = jnp.concatenate(parts, axis=0)
    o = oe[:, :V_DIM] / oe[:, V_DIM:]
    o = o[:tq] - lam * o[tq:]
    o = o * lax.rsqrt(jnp.mean(o * o, axis=-1, keepdims=True) + LN_EPS)
    return o * gain * (1.0 - lam_init)


def _attn_prompt_kernel(lam_ref, gain_ref, q_ref, k_ref, v_ref, o_ref, *, lam_init):
    lam = _diff_lambda(lam_ref, lam_init)
    gain = gain_ref[...]
    ones = jnp.ones((k_ref.shape[0], V_DIM), BF16)
    for hd in range(N_HEADS):
        cols = slice(hd * V_DIM, (hd + 1) * V_DIM)
        v_ext = jnp.concatenate([v_ref[:, cols], ones], axis=1)
        o_ref[:, cols] = _diff_attn_head(q_ref[:, cols], k_ref[:, cols], v_ext, lam, gain, lam_init).astype(BF16)


def _attn_latent_kernel(lam_ref, gain_ref, q_ref, k_ref, v_ref, kc_ref, vc_ref, o_ref, kall_ref, vext_ref, *,
                        lam_init):
    n_new = k_ref.shape[0]

    @pl.when(pl.program_id(2) == 0)
    def _():
        kall_ref[:n_new, :] = k_ref[...]
        kall_ref[n_new:, :] = kc_ref[...]
        vext_ref[:n_new, :V_DIM] = v_ref[...]
        vext_ref[n_new:, :V_DIM] = vc_ref[...]
        vext_ref[:, V_DIM:] = jnp.ones((vext_ref.shape[0], V_DIM), BF16)

    o = _diff_attn_head(q_ref[...], kall_ref[...], vext_ref[...], _diff_lambda(lam_ref, lam_init), gain_ref[...],
                        lam_init)
    o_ref[...] = o.astype(BF16)


def _attn_prompt(lam_vecs, gain, q, k, v, batch, seq, lam_init):
    d = q.shape[1]
    blk = pl.BlockSpec((seq, d), lambda b: (b, 0))
    return pl.pallas_call(
        functools.partial(_attn_prompt_kernel, lam_init=lam_init),
        out_shape=jax.ShapeDtypeStruct((batch * seq, d), BF16),
        grid=(batch,),
        in_specs=[
            pl.BlockSpec(lam_vecs.shape, lambda b: (0, 0)),
            pl.BlockSpec(gain.shape, lambda b: (0, 0)),
            blk, blk, blk,
        ],
        out_specs=blk,
        compiler_params=_params(("arbitrary",)),
        name="diff_attn_ctx",
    )(lam_vecs, gain, q, k, v)


def _attn_sample(lam_vecs, gain, q, k, v, kc, vc, batch, seq, n_ctx, row_offset, lam_init):
    d = q.shape[1]
    tq = Q_TILE
    qpt = seq // tq
    off_seq = row_offset // seq
    off_tile = row_offset // tq
    qmap = lambda b, h, i: (off_tile + b * qpt + i, h)
    kmap = lambda b, h, i: (off_seq + b, h)
    cmap = lambda b, h, i: (b, h)
    return pl.pallas_call(
        functools.partial(_attn_latent_kernel, lam_init=lam_init),
        out_shape=jax.ShapeDtypeStruct((batch * seq, d), BF16),
        grid=(batch, N_HEADS, qpt),
        in_specs=[
            pl.BlockSpec(lam_vecs.shape, lambda b, h, i: (0, 0)),
            pl.BlockSpec(gain.shape, lambda b, h, i: (0, 0)),
            pl.BlockSpec((tq, V_DIM), qmap),
            pl.BlockSpec((seq, V_DIM), kmap),
            pl.BlockSpec((seq, V_DIM), kmap),
            pl.BlockSpec((n_ctx, V_DIM), cmap),
            pl.BlockSpec((n_ctx, V_DIM), cmap),
        ],
        out_specs=pl.BlockSpec((tq, V_DIM), lambda b, h, i: (b * qpt + i, h)),
        scratch_shapes=[pltpu.VMEM((seq + n_ctx, V_DIM), BF16), pltpu.VMEM((seq + n_ctx, 2 * V_DIM), BF16)],
        compiler_params=_params(("arbitrary", "arbitrary", "arbitrary")),
        name="diff_attn_latent",
    )(lam_vecs, gain, q, k, v, kc, vc)


def _route(lg):
    lane = lax.broadcasted_iota(I32, lg.shape, 1)
    lane_f = lane.astype(F32)
    neg = jnp.float32(-jnp.inf)
    big = jnp.float32(LANES)
    gl = jnp.where(lane < N_EXPERT_GROUPS, lg, neg)
    gmax = jnp.max(gl, axis=-1, keepdims=True)
    g_prob = 1.0 / jnp.sum(jnp.exp(gl - gmax), axis=-1, keepdims=True)
    g_idx = jnp.min(jnp.where(gl == gmax, lane_f, big), axis=-1, keepdims=True)
    lo = N_EXPERT_GROUPS + EXPERTS_PER_GROUP * g_idx
    el = jnp.where((lane_f >= lo) & (lane_f < lo + EXPERTS_PER_GROUP), lg, neg)
    m1 = jnp.max(el, axis=-1, keepdims=True)
    i1 = jnp.min(jnp.where(el == m1, lane_f, big), axis=-1, keepdims=True)
    el2 = jnp.where(lane_f == i1, neg, el)
    m2 = jnp.max(el2, axis=-1, keepdims=True)
    i2 = jnp.min(jnp.where(el2 == m2, lane_f, big), axis=-1, keepdims=True)
    t = jnp.exp(m2 - m1)
    w1 = g_prob / (1.0 + t)
    w2 = g_prob * t / (1.0 + t)
    out = jnp.where(lane == 0, i1 - N_EXPERT_GROUPS, 0.0)
    out = jnp.where(lane == 1, i2 - N_EXPERT_GROUPS, out)
    out = jnp.where(lane == 2, w1, out)
    out = jnp.where(lane == 3, w2, out)
    return out


def _pack_bf16_pairs(h):
    n = h.shape[1] // 2
    bits = lax.bitcast_convert_type(h.astype(BF16).astype(F32), jnp.uint32)
    return (bits[:, :n] >> 16) | bits[:, n:]


def _unpack_bf16_pairs(p):
    lo = lax.bitcast_convert_type(p << 16, F32)
    hi = lax.bitcast_convert_type(p & jnp.uint32(0xFFFF0000), F32)
    return jnp.concatenate([lo, hi], axis=1)


def _mix_out_kernel(*refs, n_prompt_tiles, stream_kind):
    n_x = _STREAM_REFS[stream_kind]
    ap_ref, as_ref, w_ref = refs[:3]
    x_refs = refs[3:3 + n_x]
    mod_ref, gb_ref, wr_ref, x1_ref, hp_ref, route_ref = refs[3 + n_x:]
    is_prompt = pl.program_id(0) < n_prompt_tiles
    for rows in _row_chains(x1_ref.shape[0]):
        a = jnp.where(is_prompt, ap_ref[rows, :], as_ref[rows, :])
        out = _dot(a, w_ref[...])
        x = _stream_rows(stream_kind, x_refs, is_prompt, rows)
        x1 = _deepnorm(x, mod_ref[2:3, :], out) * gb_ref[0:1, :] + gb_ref[1:2, :]
        x1_ref[rows, :] = x1
        h2 = _layernorm(x1) * (1.0 + mod_ref[4:5, :]) + mod_ref[3:4, :]
        hp_ref[rows, :] = _pack_bf16_pairs(h2)
        hi = h2.astype(BF16)
        lo = (h2 - hi.astype(F32)).astype(BF16)
        logits = _dot(hi, wr_ref[0]) + _dot(lo, wr_ref[0]) + _dot(hi, wr_ref[1])
        route_ref[rows, :] = _route(logits)


def _mix_out(a_prompt, a_sample, w, stream, mod, gb, wr, n_prompt, dec_seq, t_tok):
    d = w.shape[1]
    tl = _Tiles(n_prompt, dec_seq, TOKEN_TILE)
    tm = tl.tm
    return pl.pallas_call(
        functools.partial(_mix_out_kernel, n_prompt_tiles=tl.npt, stream_kind=stream[0]),
        out_shape=(
            jax.ShapeDtypeStruct((t_tok, d), F32),
            jax.ShapeDtypeStruct((t_tok, d // 2), jnp.uint32),
            jax.ShapeDtypeStruct((t_tok, LANES), F32),
        ),
        grid=(t_tok // tm,),
        in_specs=[
            pl.BlockSpec((tm, d), tl.prompt_part),
            pl.BlockSpec((tm, d), tl.latent_part),
            pl.BlockSpec(w.shape, lambda t: (0, 0)),
        ] + tl.stream_specs(stream, t_tok // tm) + [
            pl.BlockSpec((None, N_MOD, d), tl.cond),
            pl.BlockSpec(gb.shape, lambda t: (0, 0)),
            pl.BlockSpec(wr.shape, lambda t: (0, 0, 0)),
        ],
        out_specs=(
            pl.BlockSpec((tm, d), tl.row),
            pl.BlockSpec((tm, d // 2), tl.row),
            pl.BlockSpec((tm, LANES), tl.row),
        ),
        compiler_params=_params(("arbitrary",)),
        name="mix_out_postnorm_router",
    )(a_prompt, a_sample, w, *_stream_args(stream), mod, gb, wr)


def _lane_prefix_sum(x, lane):
    sh = 1
    while sh < LANES:
        x = x + jnp.where(lane >= sh, pltpu.roll(x, sh, 1), 0.0)
        sh *= 2
    return x


def _lane_suffix_min_exclusive(x, lane):
    big = float(LANES)
    y = jnp.where(lane + 1 < LANES, pltpu.roll(x, LANES - 1, 1), big)
    sh = 1
    while sh < LANES:
        y = jnp.minimum(y, jnp.where(lane + sh < LANES, pltpu.roll(y, LANES - sh, 1), big))
        sh *= 2
    return y


def _work_items(counts, starts, ends, lane8, n_moe_tiles):
    shift = int(math.log2(MOE_TILE))
    first_tile = (starts.astype(I32) >> shift).astype(F32)
    last_tile = ((ends.astype(I32) - 1) >> shift).astype(F32)
    n_it = jnp.where(counts > 0.0, last_tile - first_tile + 1.0, 0.0)
    it_end = _lane_prefix_sum(n_it, lane8)
    it_start = it_end - n_it
    total = it_end[0:1, LANES - 1:LANES]
    sub = lax.broadcasted_iota(I32, (LANES, LANES), 0)
    rows = lambda x: jnp.broadcast_to(x[0:1, :], (LANES, LANES))
    used = counts > 0.0
    weight_slot = ((_lane_prefix_sum(jnp.where(used, 1.0, 0.0), lane8) - 1.0).astype(I32) & 1).astype(F32)
    next_used = _lane_suffix_min_exclusive(jnp.where(used, lane8.astype(F32), float(LANES)), lane8)
    per_expert = (starts, ends, first_tile, it_start, it_end, weight_slot, next_used)
    stacked = jnp.zeros((LANES, LANES), F32)
    for j, vec in enumerate(per_expert):
        stacked = jnp.where(sub == j, rows(vec), stacked)
    cols = stacked.T
    col = lambda j: cols[:, j:j + 1]
    sub_f = sub.astype(F32)
    w = lax.broadcasted_iota(I32, (LANES, LANES), 1).astype(F32)
    ex = jnp.sum(jnp.where((sub < N_EXPERTS) & (col(4) <= w), 1.0, 0.0), axis=0, keepdims=True)
    ex = jnp.minimum(ex, N_EXPERTS - 1.0)
    w1 = w[0:1, :]
    valid = w1 < total
    ex = jnp.where(valid, ex, jnp.max(jnp.where(valid, ex, 0.0), axis=-1, keepdims=True))
    onehot = sub_f == ex
    pick = lambda j: jnp.sum(jnp.where(onehot, col(j), 0.0), axis=0, keepdims=True)
    tile = jnp.where(valid, pick(2) + (w1 - pick(3)), n_moe_tiles - 1.0)
    lo = jnp.where(valid, jnp.maximum(pick(0), tile * MOE_TILE), 0.0)
    hi = jnp.where(valid, jnp.minimum(pick(1), (tile + 1.0) * MOE_TILE), 0.0)
    b8 = lambda x: jnp.broadcast_to(x, (SUBLANES, LANES))
    ex8, tile8 = b8(ex), b8(tile)
    first = jnp.where((lane8 == 0) | (tile8 != pltpu.roll(tile8, 1, 1)), 1.0, 0.0)
    newexp = jnp.where((lane8 == 0) | (ex8 != pltpu.roll(ex8, 1, 1)), 1.0, 0.0)
    sub8 = lax.broadcasted_iota(I32, (SUBLANES, LANES), 0)
    table = jnp.zeros((SUBLANES, LANES), F32)
    for j, vec in enumerate((ex8, tile8, b8(lo), b8(hi), first, newexp, b8(pick(5)), b8(pick(6)))):
        table = jnp.where(sub8 == j, vec, table)
    return table


def _plan_kernel(route_ref, pos_ref, items_ref, tri_ref, carry_ref, tot_ref, *, n_moe_tiles):
    p = pl.program_id(0)
    t = pl.program_id(1)
    tm = route_ref.shape[0]
    r = route_ref[...]
    lane = lax.broadcasted_iota(I32, (tm, LANES), 1)
    lane_f = lane.astype(F32)
    e0 = r[:, 0:1]
    e1 = r[:, 1:2] + N_EXPERTS
    m = jnp.where((lane_f == e0) | (lane_f == e1), 1.0, 0.0)
    colsum = jnp.sum(m, axis=0, keepdims=True)

    @pl.when((p == 0) & (t == 0))
    def _():
        carry_ref[...] = jnp.zeros_like(carry_ref)
        row = lax.broadcasted_iota(I32, (tm, tm), 0)
        col = lax.broadcasted_iota(I32, (tm, tm), 1)
        tri_ref[...] = jnp.where(row > col, 1.0, 0.0).astype(BF16)

    @pl.when(p == 0)
    def _():
        carry_ref[...] += colsum

    @pl.when((p == 0) & (t == pl.num_programs(1) - 1))
    def _():
        tot_ref[...] = carry_ref[...]
        carry_ref[...] = jnp.zeros_like(carry_ref)

    @pl.when(p == 1)
    def _():
        lane8 = lax.broadcasted_iota(I32, (SUBLANES, LANES), 1)
        tot = tot_ref[...]
        is_first = lane8 < N_EXPERTS
        tot0 = jnp.where(is_first, tot, 0.0)
        counts = jnp.where(is_first, tot + pltpu.roll(tot, LANES - N_EXPERTS, 1), 0.0)
        ends = _lane_prefix_sum(counts, lane8)
        starts = ends - counts
        base = jnp.where(is_first, starts, pltpu.roll(starts + tot0, N_EXPERTS, 1))
        before = _dot(tri_ref[...], m.astype(BF16)) + carry_ref[0:1, :]
        carry_ref[...] += colsum
        rows = before + base[0:1, :]
        pos0 = jnp.sum(jnp.where(lane_f == e0, rows, 0.0), axis=-1, keepdims=True)
        pos1 = jnp.sum(jnp.where(lane_f == e1, rows, 0.0), axis=-1, keepdims=True)
        both = jnp.where(lane == 0, pos0, jnp.where(lane == 1, pos1, 0.0))
        pos_ref[...] = both.T[0:SUBLANES, :].astype(I32)

        @pl.when(t == 0)
        def _():
            items_ref[...] = _work_items(counts, starts, ends, lane8, n_moe_tiles).astype(I32)


def _moe_plan(route):
    t_tok = route.shape[0]
    tm = PLAN_TILE
    n_moe_tiles = 2 * t_tok // MOE_TILE
    assert n_moe_tiles + N_EXPERTS - 1 <= LANES and 2 * N_EXPERTS <= LANES and MOE_TILE & (MOE_TILE - 1) == 0
    return pl.pallas_call(
        functools.partial(_plan_kernel, n_moe_tiles=n_moe_tiles),
        out_shape=(jax.ShapeDtypeStruct((SUBLANES, t_tok), I32), jax.ShapeDtypeStruct((SUBLANES, LANES), I32)),
        grid=(2, t_tok // tm),
        in_specs=[pl.BlockSpec((tm, LANES), lambda p, t: (t, 0))],
        out_specs=(pl.BlockSpec((SUBLANES, tm), lambda p, t: (0, t * p)),
                   pl.BlockSpec((SUBLANES, LANES), lambda p, t: (0, 0))),
        scratch_shapes=[pltpu.VMEM((tm, tm), BF16), pltpu.VMEM((SUBLANES, LANES), F32),
                        pltpu.VMEM((SUBLANES, LANES), F32)],
        compiler_params=_params(("arbitrary", "arbitrary")),
        name="moe_positions",
    )(route)


def _sc_workers():
    info = plsc.get_sparse_core_info()
    return info.num_cores, info.num_cores * info.num_subcores


def _sc_pipeline(n_chunks, loads, stores):
    def start(copies):
        for cp in copies:
            cp.start()

    def wait(copies):
        for cp in copies:
            cp.wait()

    start(loads(0, 0))
    for j in range(n_chunks):
        b = j % 2
        if j + 1 < n_chunks:
            if j >= 1:
                wait(stores(j - 1, 1 - b))
            start(loads(j + 1, 1 - b))
        wait(loads(j, b))
        start(stores(j, b))
    if n_chunks >= 2:
        wait(stores(n_chunks - 2, n_chunks % 2))
    wait(stores(n_chunks - 1, (n_chunks - 1) % 2))


def _sc_chunking(n_rows, d):
    n_cores, n_workers = _sc_workers()
    per_worker = n_rows // n_workers
    k = SC_BUFFER_BYTES // (d * 4)
    n_chunks = per_worker // k
    assert n_chunks * k * n_workers == n_rows and k <= SC_MAX_INDEX_CHUNK
    return n_cores, n_workers, per_worker, k, n_chunks


def _sc_gather_rows(table, idx):
    n, d = idx.shape[0], table.shape[1]
    n_cores, n_workers, per_worker, k, n_chunks = _sc_chunking(n, d)

    def body(table_hbm, idx_hbm, out_hbm, idx_v, rows_v, gsem, osem):
        wid = lax.axis_index("s") * n_cores + lax.axis_index("c")
        base = wid * per_worker
        pltpu.sync_copy(idx_hbm.at[wid], idx_v)
        _sc_pipeline(
            n_chunks,
            lambda j, b: [pltpu.make_async_copy(table_hbm.at[idx_v.at[j]], rows_v.at[b], gsem.at[b])],
            lambda j, b: [pltpu.make_async_copy(rows_v.at[b], out_hbm.at[pl.ds(base + j * k, k)], osem.at[b])])

    return pl.kernel(
        body,
        out_type=jax.ShapeDtypeStruct((n, d), table.dtype),
        mesh=plsc.VectorSubcoreMesh(core_axis_name="c", subcore_axis_name="s"),
        scratch_types=[pltpu.VMEM((n_chunks, k), I32), pltpu.VMEM((2, k, d), table.dtype),
                       pltpu.SemaphoreType.DMA((2,)), pltpu.SemaphoreType.DMA((2,))],
        name="sc_row_gather",
    )(table, idx.reshape(n_workers, n_chunks, k))


def _sc_scatter_rows(src, idx):
    n_lists, n_src = idx.shape
    d = src.shape[1]
    n_cores, n_workers, per_worker, k, n_chunks = _sc_chunking(n_src, d)

    def body(src_hbm, idx_hbm, out_hbm, idx_v, rows_v, gsem, osem):
        wid = lax.axis_index("s") * n_cores + lax.axis_index("c")
        base = wid * per_worker
        pltpu.sync_copy(idx_hbm.at[wid], idx_v)
        _sc_pipeline(
            n_chunks,
            lambda j, b: [pltpu.make_async_copy(src_hbm.at[pl.ds(base + j * k, k)], rows_v.at[b], gsem.at[b])],
            lambda j, b: [pltpu.make_async_copy(rows_v.at[b], out_hbm.at[idx_v.at[c, j]], osem.at[b, c])
                          for c in range(n_lists)])

    return pl.kernel(
        body,
        out_type=jax.ShapeDtypeStruct((n_lists * n_src, d), src.dtype),
        mesh=plsc.VectorSubcoreMesh(core_axis_name="c", subcore_axis_name="s"),
        scratch_types=[pltpu.VMEM((n_lists, n_chunks, k), I32), pltpu.VMEM((2, k, d), src.dtype),
                       pltpu.SemaphoreType.DMA((2,)), pltpu.SemaphoreType.DMA((2, n_lists))],
        name="sc_row_scatter",
    )(src, idx.reshape(n_lists, n_workers, n_chunks, k).transpose(1, 0, 2, 3))


IT_EXPERT, IT_TILE, IT_LO, IT_HI, IT_FIRST, IT_NEWEXP, IT_SLOT, IT_NEXT = range(8)


def _moe_kernel(it_ref, xs_ref, wg_hbm, wu_hbm, wd_hbm, o_ref, wg_f, wu_f, wd_f, sem, *, layer):
    w = pl.program_id(0)
    f = wg_f.shape[2]

    def weight_copies(expert, slot):
        return [pltpu.make_async_copy(src.at[layer, expert], dst.at[slot], sem.at[slot, j])
                for j, (src, dst) in enumerate(((wg_hbm, wg_f), (wu_hbm, wu_f), (wd_hbm, wd_f)))]

    @pl.when(it_ref[IT_NEWEXP, w] == 1)
    def _():
        expert = it_ref[IT_EXPERT, w]
        slot = it_ref[IT_SLOT, w]
        nxt = it_ref[IT_NEXT, w]

        @pl.when(w == 0)
        def _():
            for cp in weight_copies(expert, slot):
                cp.start()

        for cp in weight_copies(expert, slot):
            cp.wait()

        @pl.when(nxt < N_EXPERTS)
        def _():
            for cp in weight_copies(nxt, 1 - slot):
                cp.start()

    slot = it_ref[IT_SLOT, w]
    tm = xs_ref.shape[0]
    x = _unpack_bf16_pairs(xs_ref[...]).astype(BF16)
    a = _dot(x, wg_f[slot].astype(BF16))
    u = _dot(x, wu_f[slot].astype(BF16))
    act = (_silu(a) * u).astype(BF16)
    y = _pack_bf16_pairs(_dot(act, wd_f[slot].astype(BF16)))
    row = it_ref[IT_TILE, w] * tm + lax.broadcasted_iota(I32, (tm, 1), 0)
    mine = (row >= it_ref[IT_LO, w]) & (row < it_ref[IT_HI, w])

    @pl.when(it_ref[IT_FIRST, w] == 1)
    def _():
        o_ref[...] = jnp.where(mine, y, jnp.zeros_like(y))

    @pl.when(it_ref[IT_FIRST, w] == 0)
    def _():
        o_ref[...] = jnp.where(mine, y, o_ref[...])


def _moe_experts(items, xs, w_gate, w_up, w_down, layer):
    n_rows, dp = xs.shape
    _, _, d, f = w_gate.shape
    tm = MOE_TILE
    rmap = lambda w, it: (it[IT_TILE, w], 0)
    hbm = pl.BlockSpec(memory_space=pl.ANY)
    return pl.pallas_call(
        functools.partial(_moe_kernel, layer=layer),
        out_shape=jax.ShapeDtypeStruct((n_rows, d // 2), jnp.uint32),
        grid_spec=pltpu.PrefetchScalarGridSpec(
            num_scalar_prefetch=1,
            grid=(n_rows // tm + N_EXPERTS - 1,),
            in_specs=[pl.BlockSpec((tm, dp), rmap), hbm, hbm, hbm],
            out_specs=pl.BlockSpec((tm, d // 2), rmap),
            scratch_shapes=[
                pltpu.VMEM((2, d, f), F32), pltpu.VMEM((2, d, f), F32), pltpu.VMEM((2, f, d), F32),
                pltpu.SemaphoreType.DMA((2, 3)),
            ],
        ),
        compiler_params=_params(("arbitrary",)),
        name="moe_grouped_mlp",
    )(items, xs, w_gate, w_up, w_down)


def _final_postnorm_kernel(*refs, n_prompt_tiles):
    x2 = _finish_moe(*refs[:6])
    out_prompt, out_latent = refs[6:]

    @pl.when(pl.program_id(0) < n_prompt_tiles)
    def _():
        out_prompt[...] = x2

    @pl.when(pl.program_id(0) >= n_prompt_tiles)
    def _():
        out_latent[...] = x2


def _final_postnorm(stream, n_prompt, dec_seq, t_tok):
    d = stream[1][0].shape[1]
    tl = _Tiles(n_prompt, dec_seq, TOKEN_TILE)
    tm = tl.tm
    return pl.pallas_call(
        functools.partial(_final_postnorm_kernel, n_prompt_tiles=tl.npt),
        out_shape=(jax.ShapeDtypeStruct((n_prompt, d), F32), jax.ShapeDtypeStruct((t_tok - n_prompt, d), F32)),
        grid=(t_tok // tm,),
        in_specs=tl.stream_specs(stream, t_tok // tm),
        out_specs=(pl.BlockSpec((tm, d), tl.prompt_part), pl.BlockSpec((tm, d), tl.latent_part)),
        compiler_params=_params(("arbitrary",)),
        name="moe_combine_postnorm",
    )(*_stream_args(stream))


def _rope_tables(n_lat):
    nf = HEAD_DIM // 4
    s = np.arange(n_lat)
    lane = np.arange(LANES)
    inv = ROPE_BASE ** (-(lane % nf).astype(np.float64) / nf)
    use_col = (lane % HEAD_DIM) >= HEAD_DIM // 2
    p = np.where(use_col[None, :], (s % GRID_W)[:, None], (s // GRID_W)[:, None]).astype(np.float64)
    ang = p * inv[None, :]
    sign = np.where((lane % (2 * nf)) < nf, -1.0, 1.0)
    return jnp.asarray(np.cos(ang), F32), jnp.asarray(np.sin(ang) * sign[None, :], F32)


def _dft_tables(n):
    k = np.arange(n)
    ang = 2.0 * np.pi * ((k[:, None] * k[None, :]) % n).astype(np.float64) / n
    return np.cos(ang), np.sin(ang)


def kernel(x_prompt, x_sample, cache_k, cache_v, c, c_ctx, w_ada, b_ada, ln_gain, ln_bias, w_qkv, w_attn_out,
           lambda_q1, lambda_k1, lambda_q2, lambda_k2, subln_gain, w_fourier_out, w_router_group,
           w_router_expert, w_expert_gate, w_expert_up, w_expert_down):
    bp, sp, d = x_prompt.shape
    bs, n_lat, _ = x_sample.shape
    n_ctx = cache_k.shape[2]
    n_prompt = bp * sp
    t_tok = n_prompt + bs * n_lat
    assert d == D_MODEL and n_prompt % n_lat == 0 and n_lat % TOKEN_TILE == 0 and sp == QKV_TILE

    cond = jnp.concatenate([c_ctx[None, :], c, jnp.zeros((SUBLANES - 1 - bs, d), F32)], axis=0)
    mods = _ada_all(cond, w_ada, b_ada).reshape(DEPTH, SUBLANES, N_MOD, d)

    cos, sin = _rope_tables(n_lat)
    cc, sc = _dft_tables(FOURIER_GROUP_DIM)
    dcs = jnp.asarray(np.concatenate([cc, sc], axis=1), BF16)
    seq_tabs = {s: tuple(jnp.asarray(m, BF16) for m in _dft_tables(s)) for s in (sp, n_lat)}
    attn_layers = [i for i in range(DEPTH) if i % 2 == 0]

    stream = ("pair", (x_prompt.reshape(n_prompt, d), x_sample.reshape(bs * n_lat, d)))
    prev_kv = []
    new_k = new_v = None
    for i in range(DEPTH):
        mod = mods[i]
        if i % 2 == 0:
            a = i // 2
            lam_init = 0.8 - 0.6 * math.exp(-0.3 * i)
            last_attn = i == attn_layers[-1]
            q, k, v, kf, vf, *finished = _qkv(stream, mod, w_qkv[a].astype(BF16), cos, sin, prev_kv, last_attn,
                                              n_prompt, n_lat, t_tok)
            if last_attn:
                new_k, new_v = kf, vf
            else:
                prev_kv.append((kf, vf))
            lam_vecs = jnp.stack([lambda_q1[a], lambda_k1[a], lambda_q2[a], lambda_k2[a]], axis=0)
            gain = subln_gain[a][None, :]
            kc = cache_k[:, a].reshape(bs * n_ctx, d).astype(BF16)
            vc = cache_v[:, a].reshape(bs * n_ctx, d).astype(BF16)
            mixed_p = _attn_prompt(lam_vecs, gain, q, k, v, bp, sp, lam_init)
            mixed_s = _attn_sample(lam_vecs, gain, q, k, v, kc, vc, bs, n_lat, n_ctx, n_prompt, lam_init)
            w_mix = w_attn_out[a]
        else:
            xc, xsn, *finished = _chan_dft(stream, mod, dcs, n_prompt, n_lat, t_tok)
            mixed_p = _seq_dft(*seq_tabs[sp], xc, xsn, bp, sp, 0)
            mixed_s = _seq_dft(*seq_tabs[n_lat], xc, xsn, bs, n_lat, n_prompt)
            w_mix = w_fourier_out[i // 2]
        if finished:
            stream = ("merged", (finished[0],))
        gb0 = jnp.stack([ln_gain[i, 0], ln_bias[i, 0]], axis=0)
        gb1 = jnp.stack([ln_gain[i, 1], ln_bias[i, 1]], axis=0)
        wr = jnp.concatenate([w_router_group[i], w_router_expert[i],
                              jnp.zeros((d, LANES - N_EXPERT_GROUPS - N_EXPERTS), F32)], axis=1)
        wr_hi = wr.astype(BF16)
        wr_lo = (wr - wr_hi.astype(F32)).astype(BF16)
        x1, hp, route = _mix_out(mixed_p, mixed_s, w_mix.astype(BF16), stream, mod, gb0,
                                 jnp.stack([wr_hi, wr_lo], axis=0), n_prompt, n_lat, t_tok)
        pos8, items = _moe_plan(route)
        pos = pos8[0:2]
        xs_sorted = _sc_scatter_rows(hp, pos)
        ys = _moe_experts(items, xs_sorted, w_expert_gate, w_expert_up, w_expert_down, i)
        yg = _sc_gather_rows(ys, pos.reshape(-1))
        stream = ("pending", (x1, yg, route, mods[i], gb1))

    out_prompt, out_latent = _final_postnorm(stream, n_prompt, n_lat, t_tok)
    y_prompt = out_prompt.reshape(bp, sp, d)
    y_sample = out_latent.reshape(bs, n_lat, d)
    return (y_prompt, y_sample, new_k.reshape(bp, len(attn_layers), sp, N_HEADS, 2 * HEAD_DIM),
            new_v.reshape(bp, len(attn_layers), sp, N_HEADS, V_DIM))
```

```python
import functools
import math

import numpy as np
import jax
import jax.numpy as jnp
from jax import lax
from jax.experimental import pallas as pl
from jax.experimental.pallas import tpu as pltpu
from jax.experimental.pallas import tpu_sc as plsc

F32 = jnp.float32
BF16 = jnp.bfloat16
I32 = jnp.int32

D_MODEL = 1024
DEPTH = 4
GRID_W = 64
N_HEADS = 8
HEAD_DIM = 64
V_DIM = 2 * HEAD_DIM
ROPE_BASE = 10000.0
N_FOURIER_GROUPS = 8
FOURIER_GROUP_DIM = D_MODEL // N_FOURIER_GROUPS
N_EXPERT_GROUPS = 4
EXPERTS_PER_GROUP = 8
N_EXPERTS = N_EXPERT_GROUPS * EXPERTS_PER_GROUP
D_EXPERT = 256
N_MOD = 6
LN_EPS = 1e-5
DEEPNORM_ALPHA = (2.0 * DEPTH) ** 0.25
Q_SCALE = math.log2(math.e) * HEAD_DIM ** -0.5

LANES = 128
SUBLANES = 8
TOKEN_TILE = 512
CHAIN_ROWS = 256
QKV_TILE = 256
Q_TILE = 2048
ATTN_ROW_CHUNK = 128
MOE_TILE = 512
PLAN_TILE = 1024
SC_BUFFER_BYTES = 128 * 1024
SC_MAX_INDEX_CHUNK = 128
VMEM_LIMIT = 48 * 1024 * 1024


def _params(semantics):
    return pltpu.CompilerParams(dimension_semantics=semantics, vmem_limit_bytes=VMEM_LIMIT)


def _layernorm(x, eps=LN_EPS):
    mu = jnp.mean(x, axis=-1, keepdims=True)
    xc = x - mu
    var = jnp.mean(xc * xc, axis=-1, keepdims=True)
    return xc * lax.rsqrt(var + eps)


def _deepnorm(x, branch_gate, branch):
    return _layernorm(x + (branch_gate * (1.0 / DEEPNORM_ALPHA)) * branch, LN_EPS / DEEPNORM_ALPHA ** 2)


def _silu(a):
    return a / (1.0 + jnp.exp(-a))


def _dot(a, b):
    return jnp.dot(a, b, preferred_element_type=F32)


class _Tiles:
    def __init__(self, n_prompt, dec_seq, tm):
        self.tm = tm
        self.npt = n_prompt // tm
        self.tps = dec_seq // tm

    def row(self, t):
        return (t, 0)

    def cond(self, t):
        return (jnp.where(t < self.npt, 0, (t - self.npt) // self.tps + 1), 0, 0)

    def latent_pos(self, t):
        return (jnp.maximum(t - self.npt, 0) % self.tps, 0)

    def prompt_part(self, t):
        return (jnp.minimum(t, self.npt - 1), 0)

    def latent_part(self, t):
        return (jnp.maximum(t - self.npt, 0), 0)

    def stream_specs(self, stream, n_tiles):
        kind, arrays = stream
        tm = self.tm
        if kind == "merged":
            return [pl.BlockSpec((tm, arrays[0].shape[1]), self.row)]
        if kind == "pair":
            return [pl.BlockSpec((tm, arrays[0].shape[1]), self.prompt_part),
                    pl.BlockSpec((tm, arrays[1].shape[1]), self.latent_part)]
        x1, yg, route, mod, gb = arrays
        return [pl.BlockSpec((tm, x1.shape[1]), self.row),
                pl.BlockSpec((tm, yg.shape[1]), self.row),
                pl.BlockSpec((tm, yg.shape[1]), lambda t: (t + n_tiles, 0)),
                pl.BlockSpec((tm, route.shape[1]), self.row),
                pl.BlockSpec((None,) + mod.shape[1:], self.cond),
                pl.BlockSpec(gb.shape, lambda t: (0, 0))]


_STREAM_REFS = {"merged": 1, "pair": 2, "pending": 6}


def _stream_args(stream):
    kind, arrays = stream
    if kind == "pending":
        x1, yg, route, mod, gb = arrays
        return [x1, yg, yg, route, mod, gb]
    return list(arrays)


def _finish_moe(x1_ref, y0_ref, y1_ref, route_ref, mod_ref, gb_ref, rows=slice(None)):
    r = route_ref[rows, :]
    moe = r[:, 2:3] * _unpack_bf16_pairs(y0_ref[rows, :]) + r[:, 3:4] * _unpack_bf16_pairs(y1_ref[rows, :])
    return _deepnorm(x1_ref[rows, :], mod_ref[5:6, :], moe) * gb_ref[0:1, :] + gb_ref[1:2, :]


def _stream_rows(kind, refs, is_prompt_tile, rows=slice(None)):
    if kind == "merged":
        return refs[0][rows, :]
    if kind == "pair":
        return jnp.where(is_prompt_tile, refs[0][rows, :], refs[1][rows, :])
    return _finish_moe(*refs, rows)


def _ada_kernel(cond_ref, w_ref, b_ref, o_ref):
    a = _silu(cond_ref[...])
    o_ref[...] = _dot(a.astype(BF16), w_ref[...].astype(BF16)) + b_ref[...]


def _ada_all(cond, w_ada, b_ada):
    depth, d, n = w_ada.shape
    tn = n // 4
    return pl.pallas_call(
        _ada_kernel,
        out_shape=jax.ShapeDtypeStruct((depth, cond.shape[0], n), F32),
        grid=(depth, n // tn),
        in_specs=[
            pl.BlockSpec(cond.shape, lambda l, j: (0, 0)),
            pl.BlockSpec((None, d, tn), lambda l, j: (l, 0, j)),
            pl.BlockSpec((None, 1, tn), lambda l, j: (l, 0, j)),
        ],
        out_specs=pl.BlockSpec((None, cond.shape[0], tn), lambda l, j: (l, 0, j)),
        compiler_params=_params(("arbitrary", "arbitrary")),
        name="ada",
    )(cond, w_ada, b_ada.reshape(depth, 1, n))


def _rope(x, cos, sin_signed, first_half):
    outs = []
    for c in range(x.shape[1] // LANES):
        xc = x[:, c * LANES:(c + 1) * LANES]
        partner = jnp.where(first_half, pltpu.roll(xc, LANES - 16, 1), pltpu.roll(xc, 16, 1))
        outs.append(xc * cos + partner * sin_signed)
    return jnp.concatenate(outs, axis=1)


def _store_heads(cache_ref, slot, rows):
    cache_ref[slot] = pltpu.einshape("s(hd)->shd", rows, h=N_HEADS)


def _qkv_kernel(*refs, n_prompt_tiles, stream_kind, n_prev):
    n_x = _STREAM_REFS[stream_kind]
    x_refs, (mod_ref, w_ref, cos_ref, sin_ref) = refs[:n_x], refs[n_x:n_x + 4]
    prev = refs[n_x + 4:n_x + 4 + 2 * max(n_prev, 0)]
    q_ref, k_ref, v_ref, ko_ref, vo_ref = refs[n_x + 4 + 2 * max(n_prev, 0):][:5]
    t = pl.program_id(0)
    x = _stream_rows(stream_kind, x_refs, t < n_prompt_tiles)
    if stream_kind == "pending":
        refs[-1][...] = x
    h = (_layernorm(x) * (1.0 + mod_ref[1:2, :]) + mod_ref[0:1, :]).astype(BF16)
    d = w_ref.shape[0]
    project = lambda j: _dot(h, w_ref[:, j * d:(j + 1) * d])

    @pl.when(t < n_prompt_tiles)
    def _():
        q_ref[...] = (project(0) * Q_SCALE).astype(BF16)
        k = project(1)
        k_ref[...] = k.astype(BF16)
        v = project(2)
        v_ref[...] = v.astype(BF16)
        if n_prev < 0:
            ko_ref[...] = k
            vo_ref[...] = v
        else:
            for a in range(n_prev):
                _store_heads(ko_ref, a, prev[2 * a][...])
                _store_heads(vo_ref, a, prev[2 * a + 1][...])
            _store_heads(ko_ref, n_prev, k)
            _store_heads(vo_ref, n_prev, v)

    @pl.when(t >= n_prompt_tiles)
    def _():
        lane = lax.broadcasted_iota(I32, (h.shape[0], LANES), 1)
        first_half = (lane % 32) < 16
        cos = cos_ref[...]
        sin = sin_ref[...]
        q_ref[...] = _rope(project(0) * Q_SCALE, cos, sin, first_half).astype(BF16)
        k_ref[...] = _rope(project(1), cos, sin, first_half).astype(BF16)
        v_ref[...] = project(2).astype(BF16)


def _qkv(stream, mod, w_qkv, cos, sin, prev_kv, finish_cache, n_prompt, dec_seq, t_tok):
    d = w_qkv.shape[0]
    tl = _Tiles(n_prompt, dec_seq, QKV_TILE)
    tm = tl.tm
    pending = stream[0] == "pending"
    row_out = jax.ShapeDtypeStruct((t_tok, d), F32), pl.BlockSpec((tm, d), tl.row)
    n_prev = len(prev_kv) if finish_cache else -1
    if finish_cache:
        n_slots = n_prev + 1
        kv_shape = jax.ShapeDtypeStruct((n_prompt // tm, n_slots, tm, N_HEADS, V_DIM), F32)
        kv_spec = pl.BlockSpec((None, n_slots, tm, N_HEADS, V_DIM), lambda t: (tl.prompt_part(t)[0], 0, 0, 0, 0))
    else:
        kv_shape = jax.ShapeDtypeStruct((n_prompt, d), F32)
        kv_spec = pl.BlockSpec((tm, d), tl.prompt_part)
    prev_flat = [a for kv in prev_kv for a in kv] if finish_cache else []
    return pl.pallas_call(
        functools.partial(_qkv_kernel, n_prompt_tiles=tl.npt, stream_kind=stream[0], n_prev=n_prev),
        out_shape=(
            jax.ShapeDtypeStruct((t_tok, d), BF16),
            jax.ShapeDtypeStruct((t_tok, d), BF16),
            jax.ShapeDtypeStruct((t_tok, d), BF16),
            kv_shape, kv_shape,
        ) + ((row_out[0],) if pending else ()),
        grid=(t_tok // tm,),
        in_specs=tl.stream_specs(stream, t_tok // tm) + [
            pl.BlockSpec((None, N_MOD, d), tl.cond),
            pl.BlockSpec(w_qkv.shape, lambda t: (0, 0)),
            pl.BlockSpec((tm, LANES), tl.latent_pos),
            pl.BlockSpec((tm, LANES), tl.latent_pos),
        ] + [pl.BlockSpec((tm, d), tl.prompt_part) for _ in prev_flat],
        out_specs=(
            pl.BlockSpec((tm, d), tl.row),
            pl.BlockSpec((tm, d), tl.row),
            pl.BlockSpec((tm, d), tl.row),
            kv_spec, kv_spec,
        ) + ((row_out[1],) if pending else ()),
        compiler_params=_params(("arbitrary",)),
        name="ln_qkv_rope",
    )(*_stream_args(stream), mod, w_qkv, cos, sin, *prev_flat)


def _row_chains(n_rows):
    return [slice(r, r + CHAIN_ROWS) for r in range(0, n_rows, CHAIN_ROWS)]


def _chan_dft_kernel(*refs, n_prompt_tiles, stream_kind):
    n_x = _STREAM_REFS[stream_kind]
    x_refs, (mod_ref, dcs_ref, xc_ref, xs_ref) = refs[:n_x], refs[n_x:n_x + 4]
    g = FOURIER_GROUP_DIM
    is_prompt = pl.program_id(0) < n_prompt_tiles
    for rows in _row_chains(xc_ref.shape[0]):
        x = _stream_rows(stream_kind, x_refs, is_prompt, rows)
        if stream_kind == "pending":
            refs[-1][rows, :] = x
        h = (_layernorm(x) * (1.0 + mod_ref[1:2, :]) + mod_ref[0:1, :]).astype(BF16)
        for i in range(N_FOURIER_GROUPS):
            r = _dot(h[:, i * g:(i + 1) * g], dcs_ref[...])
            xc_ref[rows, i * g:(i + 1) * g] = r[:, :g].astype(BF16)
            xs_ref[rows, i * g:(i + 1) * g] = r[:, g:].astype(BF16)


def _chan_dft(stream, mod, dcs, n_prompt, dec_seq, t_tok):
    d = mod.shape[-1]
    tl = _Tiles(n_prompt, dec_seq, TOKEN_TILE)
    tm = tl.tm
    pending = stream[0] == "pending"
    half = jax.ShapeDtypeStruct((t_tok, d), BF16)
    return pl.pallas_call(
        functools.partial(_chan_dft_kernel, n_prompt_tiles=tl.npt, stream_kind=stream[0]),
        out_shape=(half, half) + ((jax.ShapeDtypeStruct((t_tok, d), F32),) if pending else ()),
        grid=(t_tok // tm,),
        in_specs=tl.stream_specs(stream, t_tok // tm) + [
            pl.BlockSpec((None, N_MOD, d), tl.cond),
            pl.BlockSpec(dcs.shape, lambda t: (0, 0)),
        ],
        out_specs=(pl.BlockSpec((tm, d), tl.row),) * (3 if pending else 2),
        compiler_params=_params(("arbitrary",)),
        name="ln_chan_dft",
    )(*_stream_args(stream), mod, dcs)


def _seq_dft_kernel(cs_ref, ss_ref, xc_ref, xs_ref, o_ref, *, norm):
    f = _dot(cs_ref[...], xc_ref[...]) - _dot(ss_ref[...], xs_ref[...])
    o_ref[...] = (f * norm).astype(BF16)


def _seq_dft(cs, ss, xc, xs, batch, seq, row_offset):
    d = xc.shape[1]
    tm = min(QKV_TILE, seq)
    spt = seq // tm
    off_seq = row_offset // seq
    return pl.pallas_call(
        functools.partial(_seq_dft_kernel, norm=1.0 / math.sqrt(seq * FOURIER_GROUP_DIM)),
        out_shape=jax.ShapeDtypeStruct((batch * seq, d), BF16),
        grid=(batch, spt),
        in_specs=[
            pl.BlockSpec((tm, seq), lambda b, i: (i, 0)),
            pl.BlockSpec((tm, seq), lambda b, i: (i, 0)),
            pl.BlockSpec((seq, d), lambda b, i: (off_seq + b, 0)),
            pl.BlockSpec((seq, d), lambda b, i: (off_seq + b, 0)),
        ],
        out_specs=pl.BlockSpec((tm, d), lambda b, i: (b * spt + i, 0)),
        compiler_params=_params(("arbitrary", "arbitrary")),
        name=f"seq_dft_{seq}",
    )(cs, ss, xc, xs)


def _diff_lambda(lam_ref, lam_init):
    lv = lam_ref[...]
    return (jnp.exp(jnp.sum(lv[0:1] * lv[1:2], axis=-1, keepdims=True))
            - jnp.exp(jnp.sum(lv[2:3] * lv[3:4], axis=-1, keepdims=True)) + lam_init)


def _diff_attn_head(q, k, v_ext, lam, gain, lam_init):
    tq = q.shape[0]
    lane = lax.broadcasted_iota(I32, q.shape, 1)
    zero = jnp.zeros_like(q)
    qq = jnp.concatenate([jnp.where(lane < HEAD_DIM, q, zero), jnp.where(lane >= HEAD_DIM, q, zero)], axis=0)
    parts = []
    for r in range(0, 2 * tq, ATTN_ROW_CHUNK):
        s = lax.dot_general(qq[r:r + ATTN_ROW_CHUNK], k, (((1,), (1,)), ((), ())), preferred_element_type=F32)
        e = jnp.exp2(s - jnp.max(s, axis=-1, keepdims=True)).astype(BF16)
        parts.append(_dot(e, v_ext))
    oe = jnp.concatenate(parts, axis=0)
    o = oe[:, :V_DIM] / oe[:, V_DIM:]
    o = o[:tq] - lam * o[tq:]
    o = o * lax.rsqrt(jnp.mean(o * o, axis=-1, keepdims=True) + LN_EPS)
    return o * gain * (1.0 - lam_init)


def _attn_prompt_kernel(lam_ref, gain_ref, q_ref, k_ref, v_ref, o_ref, *, lam_init):
    lam = _diff_lambda(lam_ref, lam_init)
    gain = gain_ref[...]
    ones = jnp.ones((k_ref.shape[0], V_DIM), BF16)
    for hd in range(N_HEADS):
        cols = slice(hd * V_DIM, (hd + 1) * V_DIM)
        v_ext = jnp.concatenate([v_ref[:, cols], ones], axis=1)
        o_ref[:, cols] = _diff_attn_head(q_ref[:, cols], k_ref[:, cols], v_ext, lam, gain, lam_init).astype(BF16)


def _attn_latent_kernel(lam_ref, gain_ref, q_ref, k_ref, v_ref, kc_ref, vc_ref, o_ref, kall_ref, vext_ref, *,
                        lam_init):
    n_new = k_ref.shape[0]

    @pl.when(pl.program_id(2) == 0)
    def _():
        kall_ref[:n_new, :] = k_ref[...]
        kall_ref[n_new:, :] = kc_ref[...]
        vext_ref[:n_new, :V_DIM] = v_ref[...]
        vext_ref[n_new:, :V_DIM] = vc_ref[...]
        vext_ref[:, V_DIM:] = jnp.ones((vext_ref.shape[0], V_DIM), BF16)

    o = _diff_attn_head(q_ref[...], kall_ref[...], vext_ref[...], _diff_lambda(lam_ref, lam_init), gain_ref[...],
                        lam_init)
    o_ref[...] = o.astype(BF16)


def _attn_prompt(lam_vecs, gain, q, k, v, batch, seq, lam_init):
    d = q.shape[1]
    blk = pl.BlockSpec((seq, d), lambda b: (b, 0))
    return pl.pallas_call(
        functools.partial(_attn_prompt_kernel, lam_init=lam_init),
        out_shape=jax.ShapeDtypeStruct((batch * seq, d), BF16),
        grid=(batch,),
        in_specs=[
            pl.BlockSpec(lam_vecs.shape, lambda b: (0, 0)),
            pl.BlockSpec(gain.shape, lambda b: (0, 0)),
            blk, blk, blk,
        ],
        out_specs=blk,
        compiler_params=_params(("arbitrary",)),
        name="diff_attn_ctx",
    )(lam_vecs, gain, q, k, v)


def _attn_sample(lam_vecs, gain, q, k, v, kc, vc, batch, seq, n_ctx, row_offset, lam_init):
    d = q.shape[1]
    tq = Q_TILE
    qpt = seq // tq
    off_seq = row_offset // seq
    off_tile = row_offset // tq
    qmap = lambda b, h, i: (off_tile + b * qpt + i, h)
    kmap = lambda b, h, i: (off_seq + b, h)
    cmap = lambda b, h, i: (b, h)
    return pl.pallas_call(
        functools.partial(_attn_latent_kernel, lam_init=lam_init),
        out_shape=jax.ShapeDtypeStruct((batch * seq, d), BF16),
        grid=(batch, N_HEADS, qpt),
        in_specs=[
            pl.BlockSpec(lam_vecs.shape, lambda b, h, i: (0, 0)),
            pl.BlockSpec(gain.shape, lambda b, h, i: (0, 0)),
            pl.BlockSpec((tq, V_DIM), qmap),
            pl.BlockSpec((seq, V_DIM), kmap),
            pl.BlockSpec((seq, V_DIM), kmap),
            pl.BlockSpec((n_ctx, V_DIM), cmap),
            pl.BlockSpec((n_ctx, V_DIM), cmap),
        ],
        out_specs=pl.BlockSpec((tq, V_DIM), lambda b, h, i: (b * qpt + i, h)),
        scratch_shapes=[pltpu.VMEM((seq + n_ctx, V_DIM), BF16), pltpu.VMEM((seq + n_ctx, 2 * V_DIM), BF16)],
        compiler_params=_params(("arbitrary", "arbitrary", "arbitrary")),
        name="diff_attn_latent",
    )(lam_vecs, gain, q, k, v, kc, vc)


def _route(lg):
    lane = lax.broadcasted_iota(I32, lg.shape, 1)
    lane_f = lane.astype(F32)
    neg = jnp.float32(-jnp.inf)
    big = jnp.float32(LANES)
    gl = jnp.where(lane < N_EXPERT_GROUPS, lg, neg)
    gmax = jnp.max(gl, axis=-1, keepdims=True)
    g_prob = 1.0 / jnp.sum(jnp.exp(gl - gmax), axis=-1, keepdims=True)
    g_idx = jnp.min(jnp.where(gl == gmax, lane_f, big), axis=-1, keepdims=True)
    lo = N_EXPERT_GROUPS + EXPERTS_PER_GROUP * g_idx
    el = jnp.where((lane_f >= lo) & (lane_f < lo + EXPERTS_PER_GROUP), lg, neg)
    m1 = jnp.max(el, axis=-1, keepdims=True)
    i1 = jnp.min(jnp.where(el == m1, lane_f, big), axis=-1, keepdims=True)
    el2 = jnp.where(lane_f == i1, neg, el)
    m2 = jnp.max(el2, axis=-1, keepdims=True)
    i2 = jnp.min(jnp.where(el2 == m2, lane_f, big), axis=-1, keepdims=True)
    t = jnp.exp(m2 - m1)
    w1 = g_prob / (1.0 + t)
    w2 = g_prob * t / (1.0 + t)
    out = jnp.where(lane == 0, i1 - N_EXPERT_GROUPS, 0.0)
    out = jnp.where(lane == 1, i2 - N_EXPERT_GROUPS, out)
    out = jnp.where(lane == 2, w1, out)
    out = jnp.where(lane == 3, w2, out)
    return out


def _pack_bf16_pairs(h):
    n = h.shape[1] // 2
    bits = lax.bitcast_convert_type(h.astype(BF16).astype(F32), jnp.uint32)
    return (bits[:, :n] >> 16) | bits[:, n:]


def _unpack_bf16_pairs(p):
    lo = lax.bitcast_convert_type(p << 16, F32)
    hi = lax.bitcast_convert_type(p & jnp.uint32(0xFFFF0000), F32)
    return jnp.concatenate([lo, hi], axis=1)


def _mix_out_kernel(*refs, n_prompt_tiles, stream_kind):
    n_x = _STREAM_REFS[stream_kind]
    ap_ref, as_ref, w_ref = refs[:3]
    x_refs = refs[3:3 + n_x]
    mod_ref, gb_ref, wr_ref, x1_ref, hp_ref, route_ref = refs[3 + n_x:]
    is_prompt = pl.program_id(0) < n_prompt_tiles
    for rows in _row_chains(x1_ref.shape[0]):
        a = jnp.where(is_prompt, ap_ref[rows, :], as_ref[rows, :])
        out = _dot(a, w_ref[...])
        x = _stream_rows(stream_kind, x_refs, is_prompt, rows)
        x1 = _deepnorm(x, mod_ref[2:3, :], out) * gb_ref[0:1, :] + gb_ref[1:2, :]
        x1_ref[rows, :] = x1
        h2 = _layernorm(x1) * (1.0 + mod_ref[4:5, :]) + mod_ref[3:4, :]
        hp_ref[rows, :] = _pack_bf16_pairs(h2)
        hi = h2.astype(BF16)
        lo = (h2 - hi.astype(F32)).astype(BF16)
        logits = _dot(hi, wr_ref[0]) + _dot(lo, wr_ref[0]) + _dot(hi, wr_ref[1])
        route_ref[rows, :] = _route(logits)


def _mix_out(a_prompt, a_sample, w, stream, mod, gb, wr, n_prompt, dec_seq, t_tok):
    d = w.shape[1]
    tl = _Tiles(n_prompt, dec_seq, TOKEN_TILE)
    tm = tl.tm
    return pl.pallas_call(
        functools.partial(_mix_out_kernel, n_prompt_tiles=tl.npt, stream_kind=stream[0]),
        out_shape=(
            jax.ShapeDtypeStruct((t_tok, d), F32),
            jax.ShapeDtypeStruct((t_tok, d // 2), jnp.uint32),
            jax.ShapeDtypeStruct((t_tok, LANES), F32),
        ),
        grid=(t_tok // tm,),
        in_specs=[
            pl.BlockSpec((tm, d), tl.prompt_part),
            pl.BlockSpec((tm, d), tl.latent_part),
            pl.BlockSpec(w.shape, lambda t: (0, 0)),
        ] + tl.stream_specs(stream, t_tok // tm) + [
            pl.BlockSpec((None, N_MOD, d), tl.cond),
            pl.BlockSpec(gb.shape, lambda t: (0, 0)),
            pl.BlockSpec(wr.shape, lambda t: (0, 0, 0)),
        ],
        out_specs=(
            pl.BlockSpec((tm, d), tl.row),
            pl.BlockSpec((tm, d // 2), tl.row),
            pl.BlockSpec((tm, LANES), tl.row),
        ),
        compiler_params=_params(("arbitrary",)),
        name="mix_out_postnorm_router",
    )(a_prompt, a_sample, w, *_stream_args(stream), mod, gb, wr)


def _lane_prefix_sum(x, lane):
    sh = 1
    while sh < LANES:
        x = x + jnp.where(lane >= sh, pltpu.roll(x, sh, 1), 0.0)
        sh *= 2
    return x


def _lane_suffix_min_exclusive(x, lane):
    big = float(LANES)
    y = jnp.where(lane + 1 < LANES, pltpu.roll(x, LANES - 1, 1), big)
    sh = 1
    while sh < LANES:
        y = jnp.minimum(y, jnp.where(lane + sh < LANES, pltpu.roll(y, LANES - sh, 1), big))
        sh *= 2
    return y


def _work_items(counts, starts, ends, lane8, n_moe_tiles):
    shift = int(math.log2(MOE_TILE))
    first_tile = (starts.astype(I32) >> shift).astype(F32)
    last_tile = ((ends.astype(I32) - 1) >> shift).astype(F32)
    n_it = jnp.where(counts > 0.0, last_tile - first_tile + 1.0, 0.0)
    it_end = _lane_prefix_sum(n_it, lane8)
    it_start = it_end - n_it
    total = it_end[0:1, LANES - 1:LANES]
    sub = lax.broadcasted_iota(I32, (LANES, LANES), 0)
    rows = lambda x: jnp.broadcast_to(x[0:1, :], (LANES, LANES))
    used = counts > 0.0
    weight_slot = ((_lane_prefix_sum(jnp.where(used, 1.0, 0.0), lane8) - 1.0).astype(I32) & 1).astype(F32)
    next_used = _lane_suffix_min_exclusive(jnp.where(used, lane8.astype(F32), float(LANES)), lane8)
    per_expert = (starts, ends, first_tile, it_start, it_end, weight_slot, next_used)
    stacked = jnp.zeros((LANES, LANES), F32)
    for j, vec in enumerate(per_expert):
        stacked = jnp.where(sub == j, rows(vec), stacked)
    cols = stacked.T
    col = lambda j: cols[:, j:j + 1]
    sub_f = sub.astype(F32)
    w = lax.broadcasted_iota(I32, (LANES, LANES), 1).astype(F32)
    ex = jnp.sum(jnp.where((sub < N_EXPERTS) & (col(4) <= w), 1.0, 0.0), axis=0, keepdims=True)
    ex = jnp.minimum(ex, N_EXPERTS - 1.0)
    w1 = w[0:1, :]
    valid = w1 < total
    ex = jnp.where(valid, ex, jnp.max(jnp.where(valid, ex, 0.0), axis=-1, keepdims=True))
    onehot = sub_f == ex
    pick = lambda j: jnp.sum(jnp.where(onehot, col(j), 0.0), axis=0, keepdims=True)
    tile = jnp.where(valid, pick(2) + (w1 - pick(3)), n_moe_tiles - 1.0)
    lo = jnp.where(valid, jnp.maximum(pick(0), tile * MOE_TILE), 0.0)
    hi = jnp.where(valid, jnp.minimum(pick(1), (tile + 1.0) * MOE_TILE), 0.0)
    b8 = lambda x: jnp.broadcast_to(x, (SUBLANES, LANES))
    ex8, tile8 = b8(ex), b8(tile)
    first = jnp.where((lane8 == 0) | (tile8 != pltpu.roll(tile8, 1, 1)), 1.0, 0.0)
    newexp = jnp.where((lane8 == 0) | (ex8 != pltpu.roll(ex8, 1, 1)), 1.0, 0.0)
    sub8 = lax.broadcasted_iota(I32, (SUBLANES, LANES), 0)
    table = jnp.zeros((SUBLANES, LANES), F32)
    for j, vec in enumerate((ex8, tile8, b8(lo), b8(hi), first, newexp, b8(pick(5)), b8(pick(6)))):
        table = jnp.where(sub8 == j, vec, table)
    return table


def _plan_kernel(route_ref, pos_ref, items_ref, tri_ref, carry_ref, tot_ref, *, n_moe_tiles):
    p = pl.program_id(0)
    t = pl.program_id(1)
    tm = route_ref.shape[0]
    r = route_ref[...]
    lane = lax.broadcasted_iota(I32, (tm, LANES), 1)
    lane_f = lane.astype(F32)
    e0 = r[:, 0:1]
    e1 = r[:, 1:2] + N_EXPERTS
    m = jnp.where((lane_f == e0) | (lane_f == e1), 1.0, 0.0)
    colsum = jnp.sum(m, axis=0, keepdims=True)

    @pl.when((p == 0) & (t == 0))
    def _():
        carry_ref[...] = jnp.zeros_like(carry_ref)
        row = lax.broadcasted_iota(I32, (tm, tm), 0)
        col = lax.broadcasted_iota(I32, (tm, tm), 1)
        tri_ref[...] = jnp.where(row > col, 1.0, 0.0).astype(BF16)

    @pl.when(p == 0)
    def _():
        carry_ref[...] += colsum

    @pl.when((p == 0) & (t == pl.num_programs(1) - 1))
    def _():
        tot_ref[...] = carry_ref[...]
        carry_ref[...] = jnp.zeros_like(carry_ref)

    @pl.when(p == 1)
    def _():
        lane8 = lax.broadcasted_iota(I32, (SUBLANES, LANES), 1)
        tot = tot_ref[...]
        is_first = lane8 < N_EXPERTS
        tot0 = jnp.where(is_first, tot, 0.0)
        counts = jnp.where(is_first, tot + pltpu.roll(tot, LANES - N_EXPERTS, 1), 0.0)
        ends = _lane_prefix_sum(counts, lane8)
        starts = ends - counts
        base = jnp.where(is_first, starts, pltpu.roll(starts + tot0, N_EXPERTS, 1))
        before = _dot(tri_ref[...], m.astype(BF16)) + carry_ref[0:1, :]
        carry_ref[...] += colsum
        rows = before + base[0:1, :]
        pos0 = jnp.sum(jnp.where(lane_f == e0, rows, 0.0), axis=-1, keepdims=True)
        pos1 = jnp.sum(jnp.where(lane_f == e1, rows, 0.0), axis=-1, keepdims=True)
        both = jnp.where(lane == 0, pos0, jnp.where(lane == 1, pos1, 0.0))
        pos_ref[...] = both.T[0:SUBLANES, :].astype(I32)

        @pl.when(t == 0)
        def _():
            items_ref[...] = _work_items(counts, starts, ends, lane8, n_moe_tiles).astype(I32)


def _moe_plan(route):
    t_tok = route.shape[0]
    tm = PLAN_TILE
    n_moe_tiles = 2 * t_tok // MOE_TILE
    assert n_moe_tiles + N_EXPERTS - 1 <= LANES and 2 * N_EXPERTS <= LANES and MOE_TILE & (MOE_TILE - 1) == 0
    return pl.pallas_call(
        functools.partial(_plan_kernel, n_moe_tiles=n_moe_tiles),
        out_shape=(jax.ShapeDtypeStruct((SUBLANES, t_tok), I32), jax.ShapeDtypeStruct((SUBLANES, LANES), I32)),
        grid=(2, t_tok // tm),
        in_specs=[pl.BlockSpec((tm, LANES), lambda p, t: (t, 0))],
        out_specs=(pl.BlockSpec((SUBLANES, tm), lambda p, t: (0, t * p)),
                   pl.BlockSpec((SUBLANES, LANES), lambda p, t: (0, 0))),
        scratch_shapes=[pltpu.VMEM((tm, tm), BF16), pltpu.VMEM((SUBLANES, LANES), F32),
                        pltpu.VMEM((SUBLANES, LANES), F32)],
        compiler_params=_params(("arbitrary", "arbitrary")),
        name="moe_positions",
    )(route)


def _sc_workers():
    info = plsc.get_sparse_core_info()
    return info.num_cores, info.num_cores * info.num_subcores


def _sc_pipeline(n_chunks, loads, stores):
    def start(copies):
        for cp in copies:
            cp.start()

    def wait(copies):
        for cp in copies:
            cp.wait()

    start(loads(0, 0))
    for j in range(n_chunks):
        b = j % 2
        if j + 1 < n_chunks:
            if j >= 1:
                wait(stores(j - 1, 1 - b))
            start(loads(j + 1, 1 - b))
        wait(loads(j, b))
        start(stores(j, b))
    if n_chunks >= 2:
        wait(stores(n_chunks - 2, n_chunks % 2))
    wait(stores(n_chunks - 1, (n_chunks - 1) % 2))


def _sc_chunking(n_rows, d):
    n_cores, n_workers = _sc_workers()
    per_worker = n_rows // n_workers
    k = SC_BUFFER_BYTES // (d * 4)
    n_chunks = per_worker // k
    assert n_chunks * k * n_workers == n_rows and k <= SC_MAX_INDEX_CHUNK
    return n_cores, n_workers, per_worker, k, n_chunks


def _sc_gather_rows(table, idx):
    n, d = idx.shape[0], table.shape[1]
    n_cores, n_workers, per_worker, k, n_chunks = _sc_chunking(n, d)

    def body(table_hbm, idx_hbm, out_hbm, idx_v, rows_v, gsem, osem):
        wid = lax.axis_index("s") * n_cores + lax.axis_index("c")
        base = wid * per_worker
        pltpu.sync_copy(idx_hbm.at[wid], idx_v)
        _sc_pipeline(
            n_chunks,
            lambda j, b: [pltpu.make_async_copy(table_hbm.at[idx_v.at[j]], rows_v.at[b], gsem.at[b])],
            lambda j, b: [pltpu.make_async_copy(rows_v.at[b], out_hbm.at[pl.ds(base + j * k, k)], osem.at[b])])

    return pl.kernel(
        body,
        out_type=jax.ShapeDtypeStruct((n, d), table.dtype),
        mesh=plsc.VectorSubcoreMesh(core_axis_name="c", subcore_axis_name="s"),
        scratch_types=[pltpu.VMEM((n_chunks, k), I32), pltpu.VMEM((2, k, d), table.dtype),
                       pltpu.SemaphoreType.DMA((2,)), pltpu.SemaphoreType.DMA((2,))],
        name="sc_row_gather",
    )(table, idx.reshape(n_workers, n_chunks, k))


def _sc_scatter_rows(src, idx):
    n_lists, n_src = idx.shape
    d = src.shape[1]
    n_cores, n_workers, per_worker, k, n_chunks = _sc_chunking(n_src, d)

    def body(src_hbm, idx_hbm, out_hbm, idx_v, rows_v, gsem, osem):
        wid = lax.axis_index("s") * n_cores + lax.axis_index("c")
        base = wid * per_worker
        pltpu.sync_copy(idx_hbm.at[wid], idx_v)
        _sc_pipeline(
            n_chunks,
            lambda j, b: [pltpu.make_async_copy(src_hbm.at[pl.ds(base + j * k, k)], rows_v.at[b], gsem.at[b])],
            lambda j, b: [pltpu.make_async_copy(rows_v.at[b], out_hbm.at[idx_v.at[c, j]], osem.at[b, c])
                          for c in range(n_lists)])

    return pl.kernel(
        body,
        out_type=jax.ShapeDtypeStruct((n_lists * n_src, d), src.dtype),
        mesh=plsc.VectorSubcoreMesh(core_axis_name="c", subcore_axis_name="s"),
        scratch_types=[pltpu.VMEM((n_lists, n_chunks, k), I32), pltpu.VMEM((2, k, d), src.dtype),
                       pltpu.SemaphoreType.DMA((2,)), pltpu.SemaphoreType.DMA((2, n_lists))],
        name="sc_row_scatter",
    )(src, idx.reshape(n_lists, n_workers, n_chunks, k).transpose(1, 0, 2, 3))


IT_EXPERT, IT_TILE, IT_LO, IT_HI, IT_FIRST, IT_NEWEXP, IT_SLOT, IT_NEXT = range(8)


def _moe_kernel(it_ref, xs_ref, wg_hbm, wu_hbm, wd_hbm, o_ref, wg_f, wu_f, wd_f, sem, *, layer):
    w = pl.program_id(0)
    f = wg_f.shape[2]

    def weight_copies(expert, slot):
        return [pltpu.make_async_copy(src.at[layer, expert], dst.at[slot], sem.at[slot, j])
                for j, (src, dst) in enumerate(((wg_hbm, wg_f), (wu_hbm, wu_f), (wd_hbm, wd_f)))]

    @pl.when(it_ref[IT_NEWEXP, w] == 1)
    def _():
        expert = it_ref[IT_EXPERT, w]
        slot = it_ref[IT_SLOT, w]
        nxt = it_ref[IT_NEXT, w]

        @pl.when(w == 0)
        def _():
            for cp in weight_copies(expert, slot):
                cp.start()

        for cp in weight_copies(expert, slot):
            cp.wait()

        @pl.when(nxt < N_EXPERTS)
        def _():
            for cp in weight_copies(nxt, 1 - slot):
                cp.start()

    slot = it_ref[IT_SLOT, w]
    tm = xs_ref.shape[0]
    x = _unpack_bf16_pairs(xs_ref[...]).astype(BF16)
    a = _dot(x, wg_f[slot].astype(BF16))
    u = _dot(x, wu_f[slot].astype(BF16))
    act = (_silu(a) * u).astype(BF16)
    y = _pack_bf16_pairs(_dot(act, wd_f[slot].astype(BF16)))
    row = it_ref[IT_TILE, w] * tm + lax.broadcasted_iota(I32, (tm, 1), 0)
    mine = (row >= it_ref[IT_LO, w]) & (row < it_ref[IT_HI, w])

    @pl.when(it_ref[IT_FIRST, w] == 1)
    def _():
        o_ref[...] = jnp.where(mine, y, jnp.zeros_like(y))

    @pl.when(it_ref[IT_FIRST, w] == 0)
    def _():
        o_ref[...] = jnp.where(mine, y, o_ref[...])


def _moe_experts(items, xs, w_gate, w_up, w_down, layer):
    n_rows, dp = xs.shape
    _, _, d, f = w_gate.shape
    tm = MOE_TILE
    rmap = lambda w, it: (it[IT_TILE, w], 0)
    hbm = pl.BlockSpec(memory_space=pl.ANY)
    return pl.pallas_call(
        functools.partial(_moe_kernel, layer=layer),
        out_shape=jax.ShapeDtypeStruct((n_rows, d // 2), jnp.uint32),
        grid_spec=pltpu.PrefetchScalarGridSpec(
            num_scalar_prefetch=1,
            grid=(n_rows // tm + N_EXPERTS - 1,),
            in_specs=[pl.BlockSpec((tm, dp), rmap), hbm, hbm, hbm],
            out_specs=pl.BlockSpec((tm, d // 2), rmap),
            scratch_shapes=[
                pltpu.VMEM((2, d, f), F32), pltpu.VMEM((2, d, f), F32), pltpu.VMEM((2, f, d), F32),
                pltpu.SemaphoreType.DMA((2, 3)),
            ],
        ),
        compiler_params=_params(("arbitrary",)),
        name="moe_grouped_mlp",
    )(items, xs, w_gate, w_up, w_down)


def _final_postnorm_kernel(*refs, n_prompt_tiles):
    x2 = _finish_moe(*refs[:6])
    out_prompt, out_latent = refs[6:]

    @pl.when(pl.program_id(0) < n_prompt_tiles)
    def _():
        out_prompt[...] = x2

    @pl.when(pl.program_id(0) >= n_prompt_tiles)
    def _():
        out_latent[...] = x2


def _final_postnorm(stream, n_prompt, dec_seq, t_tok):
    d = stream[1][0].shape[1]
    tl = _Tiles(n_prompt, dec_seq, TOKEN_TILE)
    tm = tl.tm
    return pl.pallas_call(
        functools.partial(_final_postnorm_kernel, n_prompt_tiles=tl.npt),
        out_shape=(jax.ShapeDtypeStruct((n_prompt, d), F32), jax.ShapeDtypeStruct((t_tok - n_prompt, d), F32)),
        grid=(t_tok // tm,),
        in_specs=tl.stream_specs(stream, t_tok // tm),
        out_specs=(pl.BlockSpec((tm, d), tl.prompt_part), pl.BlockSpec((tm, d), tl.latent_part)),
        compiler_params=_params(("arbitrary",)),
        name="moe_combine_postnorm",
    )(*_stream_args(stream))


def _rope_tables(n_lat):
    nf = HEAD_DIM // 4
    s = np.arange(n_lat)
    lane = np.arange(LANES)
    inv = ROPE_BASE ** (-(lane % nf).astype(np.float64) / nf)
    use_col = (lane % HEAD_DIM) >= HEAD_DIM // 2
    p = np.where(use_col[None, :], (s % GRID_W)[:, None], (s // GRID_W)[:, None]).astype(np.float64)
    ang = p * inv[None, :]
    sign = np.where((lane % (2 * nf)) < nf, -1.0, 1.0)
    return jnp.asarray(np.cos(ang), F32), jnp.asarray(np.sin(ang) * sign[None, :], F32)


def _dft_tables(n):
    k = np.arange(n)
    ang = 2.0 * np.pi * ((k[:, None] * k[None, :]) % n).astype(np.float64) / n
    return np.cos(ang), np.sin(ang)


def kernel(x_prompt, x_sample, cache_k, cache_v, c, c_ctx, w_ada, b_ada, ln_gain, ln_bias, w_qkv, w_attn_out,
           lambda_q1, lambda_k1, lambda_q2, lambda_k2, subln_gain, w_fourier_out, w_router_group,
           w_router_expert, w_expert_gate, w_expert_up, w_expert_down):
    bp, sp, d = x_prompt.shape
    bs, n_lat, _ = x_sample.shape
    n_ctx = cache_k.shape[2]
    n_prompt = bp * sp
    t_tok = n_prompt + bs * n_lat
    assert d == D_MODEL and n_prompt % n_lat == 0 and n_lat % TOKEN_TILE == 0 and sp == QKV_TILE

    cond = jnp.concatenate([c_ctx[None, :], c, jnp.zeros((SUBLANES - 1 - bs, d), F32)], axis=0)
    mods = _ada_all(cond, w_ada, b_ada).reshape(DEPTH, SUBLANES, N_MOD, d)

    cos, sin = _rope_tables(n_lat)
    cc, sc = _dft_tables(FOURIER_GROUP_DIM)
    dcs = jnp.asarray(np.concatenate([cc, sc], axis=1), BF16)
    seq_tabs = {s: tuple(jnp.asarray(m, BF16) for m in _dft_tables(s)) for s in (sp, n_lat)}
    attn_layers = [i for i in range(DEPTH) if i % 2 == 0]

    stream = ("pair", (x_prompt.reshape(n_prompt, d), x_sample.reshape(bs * n_lat, d)))
    prev_kv = []
    new_k = new_v = None
    for i in range(DEPTH):
        mod = mods[i]
        if i % 2 == 0:
            a = i // 2
            lam_init = 0.8 - 0.6 * math.exp(-0.3 * i)
            last_attn = i == attn_layers[-1]
            q, k, v, kf, vf, *finished = _qkv(stream, mod, w_qkv[a].astype(BF16), cos, sin, prev_kv, last_attn,
                                              n_prompt, n_lat, t_tok)
            if last_attn:
                new_k, new_v = kf, vf
            else:
                prev_kv.append((kf, vf))
            lam_vecs = jnp.stack([lambda_q1[a], lambda_k1[a], lambda_q2[a], lambda_k2[a]], axis=0)
            gain = subln_gain[a][None, :]
            kc = cache_k[:, a].reshape(bs * n_ctx, d).astype(BF16)
            vc = cache_v[:, a].reshape(bs * n_ctx, d).astype(BF16)
            mixed_p = _attn_prompt(lam_vecs, gain, q, k, v, bp, sp, lam_init)
            mixed_s = _attn_sample(lam_vecs, gain, q, k, v, kc, vc, bs, n_lat, n_ctx, n_prompt, lam_init)
            w_mix = w_attn_out[a]
        else:
            xc, xsn, *finished = _chan_dft(stream, mod, dcs, n_prompt, n_lat, t_tok)
            mixed_p = _seq_dft(*seq_tabs[sp], xc, xsn, bp, sp, 0)
            mixed_s = _seq_dft(*seq_tabs[n_lat], xc, xsn, bs, n_lat, n_prompt)
            w_mix = w_fourier_out[i // 2]
        if finished:
            stream = ("merged", (finished[0],))
        gb0 = jnp.stack([ln_gain[i, 0], ln_bias[i, 0]], axis=0)
        gb1 = jnp.stack([ln_gain[i, 1], ln_bias[i, 1]], axis=0)
        wr = jnp.concatenate([w_router_group[i], w_router_expert[i],
                              jnp.zeros((d, LANES - N_EXPERT_GROUPS - N_EXPERTS), F32)], axis=1)
        wr_hi = wr.astype(BF16)
        wr_lo = (wr - wr_hi.astype(F32)).astype(BF16)
        x1, hp, route = _mix_out(mixed_p, mixed_s, w_mix.astype(BF16), stream, mod, gb0,
                                 jnp.stack([wr_hi, wr_lo], axis=0), n_prompt, n_lat, t_tok)
        pos8, items = _moe_plan(route)
        pos = pos8[0:2]
        xs_sorted = _sc_scatter_rows(hp, pos)
        ys = _moe_experts(items, xs_sorted, w_expert_gate, w_expert_up, w_expert_down, i)
        yg = _sc_gather_rows(ys, pos.reshape(-1))
        stream = ("pending", (x1, yg, route, mods[i], gb1))

    out_prompt, out_latent = _final_postnorm(stream, n_prompt, n_lat, t_tok)
    y_prompt = out_prompt.reshape(bp, sp, d)
    y_sample = out_latent.reshape(bs, n_lat, d)
    return (y_prompt, y_sample, new_k.reshape(bp, len(attn_layers), sp, N_HEADS, 2 * HEAD_DIM),
            new_v.reshape(bp, len(attn_layers), sp, N_HEADS, V_DIM))
```

```python
import functools
import math

import numpy as np
import jax
import jax.numpy as jnp
from jax import lax
from jax.experimental import pallas as pl
from jax.experimental.pallas import tpu as pltpu
from jax.experimental.pallas import tpu_sc as plsc

F32 = jnp.float32
BF16 = jnp.bfloat16
I32 = jnp.int32

D_MODEL = 1024
DEPTH = 4
GRID_W = 64
N_HEADS = 8
HEAD_DIM = 64
V_DIM = 2 * HEAD_DIM
ROPE_BASE = 10000.0
N_FOURIER_GROUPS = 8
FOURIER_GROUP_DIM = D_MODEL // N_FOURIER_GROUPS
N_EXPERT_GROUPS = 4
EXPERTS_PER_GROUP = 8
N_EXPERTS = N_EXPERT_GROUPS * EXPERTS_PER_GROUP
D_EXPERT = 256
N_MOD = 6
LN_EPS = 1e-5
DEEPNORM_ALPHA = (2.0 * DEPTH) ** 0.25
Q_SCALE = math.log2(math.e) * HEAD_DIM ** -0.5

LANES = 128
SUBLANES = 8
TOKEN_TILE = 512
CHAIN_ROWS = 256
QKV_TILE = 256
Q_TILE = 2048
ATTN_ROW_CHUNK = 128
MOE_TILE = 512
PLAN_TILE = 1024
SC_BUFFER_BYTES = 128 * 1024
SC_MAX_INDEX_CHUNK = 128
VMEM_LIMIT = 48 * 1024 * 1024


def _params(semantics):
    return pltpu.CompilerParams(dimension_semantics=semantics, vmem_limit_bytes=VMEM_LIMIT)


def _layernorm(x, eps=LN_EPS):
    mu = jnp.mean(x, axis=-1, keepdims=True)
    xc = x - mu
    var = jnp.mean(xc * xc, axis=-1, keepdims=True)
    return xc * lax.rsqrt(var + eps)


def _deepnorm(x, branch_gate, branch):
    return _layernorm(x + (branch_gate * (1.0 / DEEPNORM_ALPHA)) * branch, LN_EPS / DEEPNORM_ALPHA ** 2)


def _silu(a):
    return a / (1.0 + jnp.exp(-a))


def _dot(a, b):
    return jnp.dot(a, b, preferred_element_type=F32)


class _Tiles:
    def __init__(self, n_prompt, dec_seq, tm):
        self.tm = tm
        self.npt = n_prompt // tm
        self.tps = dec_seq // tm

    def row(self, t):
        return (t, 0)

    def cond(self, t):
        return (jnp.where(t < self.npt, 0, (t - self.npt) // self.tps + 1), 0, 0)

    def latent_pos(self, t):
        return (jnp.maximum(t - self.npt, 0) % self.tps, 0)

    def prompt_part(self, t):
        return (jnp.minimum(t, self.npt - 1), 0)

    def latent_part(self, t):
        return (jnp.maximum(t - self.npt, 0), 0)

    def stream_specs(self, stream, n_tiles):
        kind, arrays = stream
        tm = self.tm
        if kind == "merged":
            return [pl.BlockSpec((tm, arrays[0].shape[1]), self.row)]
        if kind == "pair":
            return [pl.BlockSpec((tm, arrays[0].shape[1]), self.prompt_part),
                    pl.BlockSpec((tm, arrays[1].shape[1]), self.latent_part)]
        x1, yg, route, mod, gb = arrays
        return [pl.BlockSpec((tm, x1.shape[1]), self.row),
                pl.BlockSpec((tm, yg.shape[1]), self.row),
                pl.BlockSpec((tm, yg.shape[1]), lambda t: (t + n_tiles, 0)),
                pl.BlockSpec((tm, route.shape[1]), self.row),
                pl.BlockSpec((None,) + mod.shape[1:], self.cond),
                pl.BlockSpec(gb.shape, lambda t: (0, 0))]


_STREAM_REFS = {"merged": 1, "pair": 2, "pending": 6}


def _stream_args(stream):
    kind, arrays = stream
    if kind == "pending":
        x1, yg, route, mod, gb = arrays
        return [x1, yg, yg, route, mod, gb]
    return list(arrays)


def _finish_moe(x1_ref, y0_ref, y1_ref, route_ref, mod_ref, gb_ref, rows=slice(None)):
    r = route_ref[rows, :]
    moe = r[:, 2:3] * _unpack_bf16_pairs(y0_ref[rows, :]) + r[:, 3:4] * _unpack_bf16_pairs(y1_ref[rows, :])
    return _deepnorm(x1_ref[rows, :], mod_ref[5:6, :], moe) * gb_ref[0:1, :] + gb_ref[1:2, :]


def _stream_rows(kind, refs, is_prompt_tile, rows=slice(None)):
    if kind == "merged":
        return refs[0][rows, :]
    if kind == "pair":
        return jnp.where(is_prompt_tile, refs[0][rows, :], refs[1][rows, :])
    return _finish_moe(*refs, rows)


def _ada_kernel(cond_ref, w_ref, b_ref, o_ref):
    a = _silu(cond_ref[...])
    o_ref[...] = _dot(a.astype(BF16), w_ref[...].astype(BF16)) + b_ref[...]


def _ada_all(cond, w_ada, b_ada):
    depth, d, n = w_ada.shape
    tn = n // 4
    return pl.pallas_call(
        _ada_kernel,
        out_shape=jax.ShapeDtypeStruct((depth, cond.shape[0], n), F32),
        grid=(depth, n // tn),
        in_specs=[
            pl.BlockSpec(cond.shape, lambda l, j: (0, 0)),
            pl.BlockSpec((None, d, tn), lambda l, j: (l, 0, j)),
            pl.BlockSpec((None, 1, tn), lambda l, j: (l, 0, j)),
        ],
        out_specs=pl.BlockSpec((None, cond.shape[0], tn), lambda l, j: (l, 0, j)),
        compiler_params=_params(("arbitrary", "arbitrary")),
        name="ada",
    )(cond, w_ada, b_ada.reshape(depth, 1, n))


def _rope(x, cos, sin_signed, first_half):
    outs = []
    for c in range(x.shape[1] // LANES):
        xc = x[:, c * LANES:(c + 1) * LANES]
        partner = jnp.where(first_half, pltpu.roll(xc, LANES - 16, 1), pltpu.roll(xc, 16, 1))
        outs.append(xc * cos + partner * sin_signed)
    return jnp.concatenate(outs, axis=1)


def _store_heads(cache_ref, slot, rows):
    cache_ref[slot] = pltpu.einshape("s(hd)->shd", rows, h=N_HEADS)


def _qkv_kernel(*refs, n_prompt_tiles, stream_kind, n_prev):
    n_x = _STREAM_REFS[stream_kind]
    x_refs, (mod_ref, w_ref, cos_ref, sin_ref) = refs[:n_x], refs[n_x:n_x + 4]
    prev = refs[n_x + 4:n_x + 4 + 2 * max(n_prev, 0)]
    q_ref, k_ref, v_ref, ko_ref, vo_ref = refs[n_x + 4 + 2 * max(n_prev, 0):][:5]
    t = pl.program_id(0)
    x = _stream_rows(stream_kind, x_refs, t < n_prompt_tiles)
    if stream_kind == "pending":
        refs[-1][...] = x
    h = (_layernorm(x) * (1.0 + mod_ref[1:2, :]) + mod_ref[0:1, :]).astype(BF16)
    d = w_ref.shape[0]
    project = lambda j: _dot(h, w_ref[:, j * d:(j + 1) * d])

    @pl.when(t < n_prompt_tiles)
    def _():
        q_ref[...] = (project(0) * Q_SCALE).astype(BF16)
        k = project(1)
        k_ref[...] = k.astype(BF16)
        v = project(2)
        v_ref[...] = v.astype(BF16)
        if n_prev < 0:
            ko_ref[...] = k
            vo_ref[...] = v
        else:
            for a in range(n_prev):
                _store_heads(ko_ref, a, prev[2 * a][...])
                _store_heads(vo_ref, a, prev[2 * a + 1][...])
            _store_heads(ko_ref, n_prev, k)
            _store_heads(vo_ref, n_prev, v)

    @pl.when(t >= n_prompt_tiles)
    def _():
        lane = lax.broadcasted_iota(I32, (h.shape[0], LANES), 1)
        first_half = (lane % 32) < 16
        cos = cos_ref[...]
        sin = sin_ref[...]
        q_ref[...] = _rope(project(0) * Q_SCALE, cos, sin, first_half).astype(BF16)
        k_ref[...] = _rope(project(1), cos, sin, first_half).astype(BF16)
        v_ref[...] = project(2).astype(BF16)


def _qkv(stream, mod, w_qkv, cos, sin, prev_kv, finish_cache, n_prompt, dec_seq, t_tok):
    d = w_qkv.shape[0]
    tl = _Tiles(n_prompt, dec_seq, QKV_TILE)
    tm = tl.tm
    pending = stream[0] == "pending"
    row_out = jax.ShapeDtypeStruct((t_tok, d), F32), pl.BlockSpec((tm, d), tl.row)
    n_prev = len(prev_kv) if finish_cache else -1
    if finish_cache:
        n_slots = n_prev + 1
        kv_shape = jax.ShapeDtypeStruct((n_prompt // tm, n_slots, tm, N_HEADS, V_DIM), F32)
        kv_spec = pl.BlockSpec((None, n_slots, tm, N_HEADS, V_DIM), lambda t: (tl.prompt_part(t)[0], 0, 0, 0, 0))
    else:
        kv_shape = jax.ShapeDtypeStruct((n_prompt, d), F32)
        kv_spec = pl.BlockSpec((tm, d), tl.prompt_part)
    prev_flat = [a for kv in prev_kv for a in kv] if finish_cache else []
    return pl.pallas_call(
        functools.partial(_qkv_kernel, n_prompt_tiles=tl.npt, stream_kind=stream[0], n_prev=n_prev),
        out_shape=(
            jax.ShapeDtypeStruct((t_tok, d), BF16),
            jax.ShapeDtypeStruct((t_tok, d), BF16),
            jax.ShapeDtypeStruct((t_tok, d), BF16),
            kv_shape, kv_shape,
        ) + ((row_out[0],) if pending else ()),
        grid=(t_tok // tm,),
        in_specs=tl.stream_specs(stream, t_tok // tm) + [
            pl.BlockSpec((None, N_MOD, d), tl.cond),
            pl.BlockSpec(w_qkv.shape, lambda t: (0, 0)),
            pl.BlockSpec((tm, LANES), tl.latent_pos),
            pl.BlockSpec((tm, LANES), tl.latent_pos),
        ] + [pl.BlockSpec((tm, d), tl.prompt_part) for _ in prev_flat],
        out_specs=(
            pl.BlockSpec((tm, d), tl.row),
            pl.BlockSpec((tm, d), tl.row),
            pl.BlockSpec((tm, d), tl.row),
            kv_spec, kv_spec,
        ) + ((row_out[1],) if pending else ()),
        compiler_params=_params(("arbitrary",)),
        name="ln_qkv_rope",
    )(*_stream_args(stream), mod, w_qkv, cos, sin, *prev_flat)


def _row_chains(n_rows):
    return [slice(r, r + CHAIN_ROWS) for r in range(0, n_rows, CHAIN_ROWS)]


def _chan_dft_kernel(*refs, n_prompt_tiles, stream_kind):
    n_x = _STREAM_REFS[stream_kind]
    x_refs, (mod_ref, dcs_ref, xc_ref, xs_ref) = refs[:n_x], refs[n_x:n_x + 4]
    g = FOURIER_GROUP_DIM
    is_prompt = pl.program_id(0) < n_prompt_tiles
    for rows in _row_chains(xc_ref.shape[0]):
        x = _stream_rows(stream_kind, x_refs, is_prompt, rows)
        if stream_kind == "pending":
            refs[-1][rows, :] = x
        h = (_layernorm(x) * (1.0 + mod_ref[1:2, :]) + mod_ref[0:1, :]).astype(BF16)
        for i in range(N_FOURIER_GROUPS):
            r = _dot(h[:, i * g:(i + 1) * g], dcs_ref[...])
            xc_ref[rows, i * g:(i + 1) * g] = r[:, :g].astype(BF16)
            xs_ref[rows, i * g:(i + 1) * g] = r[:, g:].astype(BF16)


def _chan_dft(stream, mod, dcs, n_prompt, dec_seq, t_tok):
    d = mod.shape[-1]
    tl = _Tiles(n_prompt, dec_seq, TOKEN_TILE)
    tm = tl.tm
    pending = stream[0] == "pending"
    half = jax.ShapeDtypeStruct((t_tok, d), BF16)
    return pl.pallas_call(
        functools.partial(_chan_dft_kernel, n_prompt_tiles=tl.npt, stream_kind=stream[0]),
        out_shape=(half, half) + ((jax.ShapeDtypeStruct((t_tok, d), F32),) if pending else ()),
        grid=(t_tok // tm,),
        in_specs=tl.stream_specs(stream, t_tok // tm) + [
            pl.BlockSpec((None, N_MOD, d), tl.cond),
            pl.BlockSpec(dcs.shape, lambda t: (0, 0)),
        ],
        out_specs=(pl.BlockSpec((tm, d), tl.row),) * (3 if pending else 2),
        compiler_params=_params(("arbitrary",)),
        name="ln_chan_dft",
    )(*_stream_args(stream), mod, dcs)


def _seq_dft_kernel(cs_ref, ss_ref, xc_ref, xs_ref, o_ref, *, norm):
    f = _dot(cs_ref[...], xc_ref[...]) - _dot(ss_ref[...], xs_ref[...])
    o_ref[...] = (f * norm).astype(BF16)


def _seq_dft(cs, ss, xc, xs, batch, seq, row_offset):
    d = xc.shape[1]
    tm = min(QKV_TILE, seq)
    spt = seq // tm
    off_seq = row_offset // seq
    return pl.pallas_call(
        functools.partial(_seq_dft_kernel, norm=1.0 / math.sqrt(seq * FOURIER_GROUP_DIM)),
        out_shape=jax.ShapeDtypeStruct((batch * seq, d), BF16),
        grid=(batch, spt),
        in_specs=[
            pl.BlockSpec((tm, seq), lambda b, i: (i, 0)),
            pl.BlockSpec((tm, seq), lambda b, i: (i, 0)),
            pl.BlockSpec((seq, d), lambda b, i: (off_seq + b, 0)),
            pl.BlockSpec((seq, d), lambda b, i: (off_seq + b, 0)),
        ],
        out_specs=pl.BlockSpec((tm, d), lambda b, i: (b * spt + i, 0)),
        compiler_params=_params(("arbitrary", "arbitrary")),
        name=f"seq_dft_{seq}",
    )(cs, ss, xc, xs)


def _diff_lambda(lam_ref, lam_init):
    lv = lam_ref[...]
    return (jnp.exp(jnp.sum(lv[0:1] * lv[1:2], axis=-1, keepdims=True))
            - jnp.exp(jnp.sum(lv[2:3] * lv[3:4], axis=-1, keepdims=True)) + lam_init)


def _diff_attn_head(q, k, v_ext, lam, gain, lam_init):
    tq = q.shape[0]
    lane = lax.broadcasted_iota(I32, q.shape, 1)
    zero = jnp.zeros_like(q)
    qq = jnp.concatenate([jnp.where(lane < HEAD_DIM, q, zero), jnp.where(lane >= HEAD_DIM, q, zero)], axis=0)
    parts = []
    for r in range(0, 2 * tq, ATTN_ROW_CHUNK):
        s = lax.dot_general(qq[r:r + ATTN_ROW_CHUNK], k, (((1,), (1,)), ((), ())), preferred_element_type=F32)
        e = jnp.exp2(s - jnp.max(s, axis=-1, keepdims=True)).astype(BF16)
        parts.append(_dot(e, v_ext))
    oe = jnp.concatenate(parts, axis=0)
    o = oe[:, :V_DIM] / oe[:, V_DIM:]
    o = o[:tq] - lam * o[tq:]
    o = o * lax.rsqrt(jnp.mean(o * o, axis=-1, keepdims=True) + LN_EPS)
    return o * gain * (1.0 - lam_init)


def _attn_prompt_kernel(lam_ref, gain_ref, q_ref, k_ref, v_ref, o_ref, *, lam_init):
    lam = _diff_lambda(lam_ref, lam_init)
    gain = gain_ref[...]
    ones = jnp.ones((k_ref.shape[0], V_DIM), BF16)
    for hd in range(N_HEADS):
        cols = slice(hd * V_DIM, (hd + 1) * V_DIM)
        v_ext = jnp.concatenate([v_ref[:, cols], ones], axis=1)
        o_ref[:, cols] = _diff_attn_head(q_ref[:, cols], k_ref[:, cols], v_ext, lam, gain, lam_init).astype(BF16)


def _attn_latent_kernel(lam_ref, gain_ref, q_ref, k_ref, v_ref, kc_ref, vc_ref, o_ref, kall_ref, vext_ref, *,
                        lam_init):
    n_new = k_ref.shape[0]

    @pl.when(pl.program_id(2) == 0)
    def _():
        kall_ref[:n_new, :] = k_ref[...]
        kall_ref[n_new:, :] = kc_ref[...]
        vext_ref[:n_new, :V_DIM] = v_ref[...]
        vext_ref[n_new:, :V_DIM] = vc_ref[...]
        vext_ref[:, V_DIM:] = jnp.ones((vext_ref.shape[0], V_DIM), BF16)

    o = _diff_attn_head(q_ref[...], kall_ref[...], vext_ref[...], _diff_lambda(lam_ref, lam_init), gain_ref[...],
                        lam_init)
    o_ref[...] = o.astype(BF16)


def _attn_prompt(lam_vecs, gain, q, k, v, batch, seq, lam_init):
    d = q.shape[1]
    blk = pl.BlockSpec((seq, d), lambda b: (b, 0))
    return pl.pallas_call(
        functools.partial(_attn_prompt_kernel, lam_init=lam_init),
        out_shape=jax.ShapeDtypeStruct((batch * seq, d), BF16),
        grid=(batch,),
        in_specs=[
            pl.BlockSpec(lam_vecs.shape, lambda b: (0, 0)),
            pl.BlockSpec(gain.shape, lambda b: (0, 0)),
            blk, blk, blk,
        ],
        out_specs=blk,
        compiler_params=_params(("arbitrary",)),
        name="diff_attn_ctx",
    )(lam_vecs, gain, q, k, v)


def _attn_sample(lam_vecs, gain, q, k, v, kc, vc, batch, seq, n_ctx, row_offset, lam_init):
    d = q.shape[1]
    tq = Q_TILE
    qpt = seq // tq
    off_seq = row_offset // seq
    off_tile = row_offset // tq
    qmap = lambda b, h, i: (off_tile + b * qpt + i, h)
    kmap = lambda b, h, i: (off_seq + b, h)
    cmap = lambda b, h, i: (b, h)
    return pl.pallas_call(
        functools.partial(_attn_latent_kernel, lam_init=lam_init),
        out_shape=jax.ShapeDtypeStruct((batch * seq, d), BF16),
        grid=(batch, N_HEADS, qpt),
        in_specs=[
            pl.BlockSpec(lam_vecs.shape, lambda b, h, i: (0, 0)),
            pl.BlockSpec(gain.shape, lambda b, h, i: (0, 0)),
            pl.BlockSpec((tq, V_DIM), qmap),
            pl.BlockSpec((seq, V_DIM), kmap),
            pl.BlockSpec((seq, V_DIM), kmap),
            pl.BlockSpec((n_ctx, V_DIM), cmap),
            pl.BlockSpec((n_ctx, V_DIM), cmap),
        ],
        out_specs=pl.BlockSpec((tq, V_DIM), lambda b, h, i: (b * qpt + i, h)),
        scratch_shapes=[pltpu.VMEM((seq + n_ctx, V_DIM), BF16), pltpu.VMEM((seq + n_ctx, 2 * V_DIM), BF16)],
        compiler_params=_params(("arbitrary", "arbitrary", "arbitrary")),
        name="diff_attn_latent",
    )(lam_vecs, gain, q, k, v, kc, vc)


def _route(lg):
    lane = lax.broadcasted_iota(I32, lg.shape, 1)
    lane_f = lane.astype(F32)
    neg = jnp.float32(-jnp.inf)
    big = jnp.float32(LANES)
    gl = jnp.where(lane < N_EXPERT_GROUPS, lg, neg)
    gmax = jnp.max(gl, axis=-1, keepdims=True)
    g_prob = 1.0 / jnp.sum(jnp.exp(gl - gmax), axis=-1, keepdims=True)
    g_idx = jnp.min(jnp.where(gl == gmax, lane_f, big), axis=-1, keepdims=True)
    lo = N_EXPERT_GROUPS + EXPERTS_PER_GROUP * g_idx
    el = jnp.where((lane_f >= lo) & (lane_f < lo + EXPERTS_PER_GROUP), lg, neg)
    m1 = jnp.max(el, axis=-1, keepdims=True)
    i1 = jnp.min(jnp.where(el == m1, lane_f, big), axis=-1, keepdims=True)
    el2 = jnp.where(lane_f == i1, neg, el)
    m2 = jnp.max(el2, axis=-1, keepdims=True)
    i2 = jnp.min(jnp.where(el2 == m2, lane_f, big), axis=-1, keepdims=True)
    t = jnp.exp(m2 - m1)
    w1 = g_prob / (1.0 + t)
    w2 = g_prob * t / (1.0 + t)
    out = jnp.where(lane == 0, i1 - N_EXPERT_GROUPS, 0.0)
    out = jnp.where(lane == 1, i2 - N_EXPERT_GROUPS, out)
    out = jnp.where(lane == 2, w1, out)
    out = jnp.where(lane == 3, w2, out)
    return out


def _pack_bf16_pairs(h):
    n = h.shape[1] // 2
    bits = lax.bitcast_convert_type(h.astype(BF16).astype(F32), jnp.uint32)
    return (bits[:, :n] >> 16) | bits[:, n:]


def _unpack_bf16_pairs(p):
    lo = lax.bitcast_convert_type(p << 16, F32)
    hi = lax.bitcast_convert_type(p & jnp.uint32(0xFFFF0000), F32)
    return jnp.concatenate([lo, hi], axis=1)


def _mix_out_kernel(*refs, n_prompt_tiles, stream_kind):
    n_x = _STREAM_REFS[stream_kind]
    ap_ref, as_ref, w_ref = refs[:3]
    x_refs = refs[3:3 + n_x]
    mod_ref, gb_ref, wr_ref, x1_ref, hp_ref, route_ref, cnt_ref = refs[3 + n_x:]
    is_prompt = pl.program_id(0) < n_prompt_tiles
    bucket_counts = jnp.zeros((1, LANES), F32)
    for rows in _row_chains(x1_ref.shape[0]):
        a = jnp.where(is_prompt, ap_ref[rows, :], as_ref[rows, :])
        out = _dot(a, w_ref[...])
        x = _stream_rows(stream_kind, x_refs, is_prompt, rows)
        x1 = _deepnorm(x, mod_ref[2:3, :], out) * gb_ref[0:1, :] + gb_ref[1:2, :]
        x1_ref[rows, :] = x1
        h2 = _layernorm(x1) * (1.0 + mod_ref[4:5, :]) + mod_ref[3:4, :]
        hp_ref[rows, :] = _pack_bf16_pairs(h2)
        hi = h2.astype(BF16)
        lo = (h2 - hi.astype(F32)).astype(BF16)
        logits = _dot(hi, wr_ref[0]) + _dot(lo, wr_ref[0]) + _dot(hi, wr_ref[1])
        routed = _route(logits)
        route_ref[rows, :] = routed
        bucket_counts = bucket_counts + jnp.sum(_bucket_onehot(routed)[0], axis=0, keepdims=True)
    first_row = lax.broadcasted_iota(I32, cnt_ref.shape, 0) == 0
    cnt_ref[...] = jnp.where(first_row, jnp.broadcast_to(bucket_counts, cnt_ref.shape), 0.0)


def _mix_out(a_prompt, a_sample, w, stream, mod, gb, wr, n_prompt, dec_seq, t_tok):
    d = w.shape[1]
    tl = _Tiles(n_prompt, dec_seq, TOKEN_TILE)
    tm = tl.tm
    return pl.pallas_call(
        functools.partial(_mix_out_kernel, n_prompt_tiles=tl.npt, stream_kind=stream[0]),
        out_shape=(
            jax.ShapeDtypeStruct((t_tok, d), F32),
            jax.ShapeDtypeStruct((t_tok, d // 2), jnp.uint32),
            jax.ShapeDtypeStruct((t_tok, LANES), F32),
            jax.ShapeDtypeStruct((t_tok // tm * SUBLANES, LANES), F32),
        ),
        grid=(t_tok // tm,),
        in_specs=[
            pl.BlockSpec((tm, d), tl.prompt_part),
            pl.BlockSpec((tm, d), tl.latent_part),
            pl.BlockSpec(w.shape, lambda t: (0, 0)),
        ] + tl.stream_specs(stream, t_tok // tm) + [
            pl.BlockSpec((None, N_MOD, d), tl.cond),
            pl.BlockSpec(gb.shape, lambda t: (0, 0)),
            pl.BlockSpec(wr.shape, lambda t: (0, 0, 0)),
        ],
        out_specs=(
            pl.BlockSpec((tm, d), tl.row),
            pl.BlockSpec((tm, d // 2), tl.row),
            pl.BlockSpec((tm, LANES), tl.row),
            pl.BlockSpec((SUBLANES, LANES), tl.row),
        ),
        compiler_params=_params(("arbitrary",)),
        name="mix_out_postnorm_router",
    )(a_prompt, a_sample, w, *_stream_args(stream), mod, gb, wr)


def _lane_prefix_sum(x, lane):
    sh = 1
    while sh < LANES:
        x = x + jnp.where(lane >= sh, pltpu.roll(x, sh, 1), 0.0)
        sh *= 2
    return x


def _lane_suffix_min_exclusive(x, lane):
    big = float(LANES)
    y = jnp.where(lane + 1 < LANES, pltpu.roll(x, LANES - 1, 1), big)
    sh = 1
    while sh < LANES:
        y = jnp.minimum(y, jnp.where(lane + sh < LANES, pltpu.roll(y, LANES - sh, 1), big))
        sh *= 2
    return y


def _work_items(counts, starts, ends, lane8, n_moe_tiles):
    shift = int(math.log2(MOE_TILE))
    first_tile = (starts.astype(I32) >> shift).astype(F32)
    last_tile = ((ends.astype(I32) - 1) >> shift).astype(F32)
    n_it = jnp.where(counts > 0.0, last_tile - first_tile + 1.0, 0.0)
    it_end = _lane_prefix_sum(n_it, lane8)
    it_start = it_end - n_it
    total = it_end[0:1, LANES - 1:LANES]
    sub = lax.broadcasted_iota(I32, (LANES, LANES), 0)
    rows = lambda x: jnp.broadcast_to(x[0:1, :], (LANES, LANES))
    used = counts > 0.0
    weight_slot = ((_lane_prefix_sum(jnp.where(used, 1.0, 0.0), lane8) - 1.0).astype(I32) & 1).astype(F32)
    next_used = _lane_suffix_min_exclusive(jnp.where(used, lane8.astype(F32), float(LANES)), lane8)
    per_expert = (starts, ends, first_tile, it_start, it_end, weight_slot, next_used)
    stacked = jnp.zeros((LANES, LANES), F32)
    for j, vec in enumerate(per_expert):
        stacked = jnp.where(sub == j, rows(vec), stacked)
    cols = stacked.T
    col = lambda j: cols[:, j:j + 1]
    sub_f = sub.astype(F32)
    w = lax.broadcasted_iota(I32, (LANES, LANES), 1).astype(F32)
    ex = jnp.sum(jnp.where((sub < N_EXPERTS) & (col(4) <= w), 1.0, 0.0), axis=0, keepdims=True)
    ex = jnp.minimum(ex, N_EXPERTS - 1.0)
    w1 = w[0:1, :]
    valid = w1 < total
    ex = jnp.where(valid, ex, jnp.max(jnp.where(valid, ex, 0.0), axis=-1, keepdims=True))
    onehot = sub_f == ex
    pick = lambda j: jnp.sum(jnp.where(onehot, col(j), 0.0), axis=0, keepdims=True)
    tile = jnp.where(valid, pick(2) + (w1 - pick(3)), n_moe_tiles - 1.0)
    lo = jnp.where(valid, jnp.maximum(pick(0), tile * MOE_TILE), 0.0)
    hi = jnp.where(valid, jnp.minimum(pick(1), (tile + 1.0) * MOE_TILE), 0.0)
    b8 = lambda x: jnp.broadcast_to(x, (SUBLANES, LANES))
    ex8, tile8 = b8(ex), b8(tile)
    first = jnp.where((lane8 == 0) | (tile8 != pltpu.roll(tile8, 1, 1)), 1.0, 0.0)
    newexp = jnp.where((lane8 == 0) | (ex8 != pltpu.roll(ex8, 1, 1)), 1.0, 0.0)
    sub8 = lax.broadcasted_iota(I32, (SUBLANES, LANES), 0)
    table = jnp.zeros((SUBLANES, LANES), F32)
    for j, vec in enumerate((ex8, tile8, b8(lo), b8(hi), first, newexp, b8(pick(5)), b8(pick(6)))):
        table = jnp.where(sub8 == j, vec, table)
    return table


def _bucket_onehot(route_rows):
    lane_f = lax.broadcasted_iota(I32, route_rows.shape, 1).astype(F32)
    b0 = route_rows[:, 0:1]
    b1 = route_rows[:, 1:2] + N_EXPERTS
    return jnp.where((lane_f == b0) | (lane_f == b1), 1.0, 0.0), b0, b1


def _plan_kernel(route_ref, cnt_ref, pos_ref, items_ref, tri_ref, carry_ref, *, n_moe_tiles):
    t = pl.program_id(0)
    tm = route_ref.shape[0]
    m, b0, b1 = _bucket_onehot(route_ref[...])
    lane = lax.broadcasted_iota(I32, (tm, LANES), 1)
    lane_f = lane.astype(F32)

    @pl.when(t == 0)
    def _():
        carry_ref[...] = jnp.zeros_like(carry_ref)
        row = lax.broadcasted_iota(I32, (tm, tm), 0)
        col = lax.broadcasted_iota(I32, (tm, tm), 1)
        tri_ref[...] = jnp.where(row > col, 1.0, 0.0).astype(BF16)

    lane8 = lax.broadcasted_iota(I32, (SUBLANES, LANES), 1)
    tot = jnp.broadcast_to(jnp.sum(cnt_ref[...], axis=0, keepdims=True), (SUBLANES, LANES))
    is_first = lane8 < N_EXPERTS
    tot0 = jnp.where(is_first, tot, 0.0)
    counts = jnp.where(is_first, tot + pltpu.roll(tot, LANES - N_EXPERTS, 1), 0.0)
    ends = _lane_prefix_sum(counts, lane8)
    starts = ends - counts
    base = jnp.where(is_first, starts, pltpu.roll(starts + tot0, N_EXPERTS, 1))
    before = _dot(tri_ref[...], m.astype(BF16)) + carry_ref[0:1, :]
    carry_ref[...] += jnp.sum(m, axis=0, keepdims=True)
    rows = before + base[0:1, :]
    pos0 = jnp.sum(jnp.where(lane_f == b0, rows, 0.0), axis=-1, keepdims=True)
    pos1 = jnp.sum(jnp.where(lane_f == b1, rows, 0.0), axis=-1, keepdims=True)
    both = jnp.where(lane == 0, pos0, jnp.where(lane == 1, pos1, 0.0))
    pos_ref[...] = both.T[0:SUBLANES, :].astype(I32)

    @pl.when(t == 0)
    def _():
        items_ref[...] = _work_items(counts, starts, ends, lane8, n_moe_tiles).astype(I32)


def _moe_plan(route, bucket_counts):
    t_tok = route.shape[0]
    tm = PLAN_TILE
    n_moe_tiles = 2 * t_tok // MOE_TILE
    assert n_moe_tiles + N_EXPERTS - 1 <= LANES and 2 * N_EXPERTS <= LANES and MOE_TILE & (MOE_TILE - 1) == 0
    return pl.pallas_call(
        functools.partial(_plan_kernel, n_moe_tiles=n_moe_tiles),
        out_shape=(jax.ShapeDtypeStruct((SUBLANES, t_tok), I32), jax.ShapeDtypeStruct((SUBLANES, LANES), I32)),
        grid=(t_tok // tm,),
        in_specs=[pl.BlockSpec((tm, LANES), lambda t: (t, 0)),
                  pl.BlockSpec(bucket_counts.shape, lambda t: (0, 0))],
        out_specs=(pl.BlockSpec((SUBLANES, tm), lambda t: (0, t)),
                   pl.BlockSpec((SUBLANES, LANES), lambda t: (0, 0))),
        scratch_shapes=[pltpu.VMEM((tm, tm), BF16), pltpu.VMEM((SUBLANES, LANES), F32)],
        compiler_params=_params(("arbitrary",)),
        name="moe_positions",
    )(route, bucket_counts)


def _sc_workers():
    info = plsc.get_sparse_core_info()
    return info.num_cores, info.num_cores * info.num_subcores


def _sc_pipeline(n_chunks, loads, stores):
    def start(copies):
        for cp in copies:
            cp.start()

    def wait(copies):
        for cp in copies:
            cp.wait()

    start(loads(0, 0))
    for j in range(n_chunks):
        b = j % 2
        if j + 1 < n_chunks:
            if j >= 1:
                wait(stores(j - 1, 1 - b))
            start(loads(j + 1, 1 - b))
        wait(loads(j, b))
        start(stores(j, b))
    if n_chunks >= 2:
        wait(stores(n_chunks - 2, n_chunks % 2))
    wait(stores(n_chunks - 1, (n_chunks - 1) % 2))


def _sc_chunking(n_rows, d):
    n_cores, n_workers = _sc_workers()
    per_worker = n_rows // n_workers
    k = SC_BUFFER_BYTES // (d * 4)
    n_chunks = per_worker // k
    assert n_chunks * k * n_workers == n_rows and k <= SC_MAX_INDEX_CHUNK
    return n_cores, n_workers, per_worker, k, n_chunks


def _sc_gather_rows(table, idx):
    n, d = idx.shape[0], table.shape[1]
    n_cores, n_workers, per_worker, k, n_chunks = _sc_chunking(n, d)

    def body(table_hbm, idx_hbm, out_hbm, idx_v, rows_v, gsem, osem):
        wid = lax.axis_index("s") * n_cores + lax.axis_index("c")
        base = wid * per_worker
        pltpu.sync_copy(idx_hbm.at[wid], idx_v)
        _sc_pipeline(
            n_chunks,
            lambda j, b: [pltpu.make_async_copy(table_hbm.at[idx_v.at[j]], rows_v.at[b], gsem.at[b])],
            lambda j, b: [pltpu.make_async_copy(rows_v.at[b], out_hbm.at[pl.ds(base + j * k, k)], osem.at[b])])

    return pl.kernel(
        body,
        out_type=jax.ShapeDtypeStruct((n, d), table.dtype),
        mesh=plsc.VectorSubcoreMesh(core_axis_name="c", subcore_axis_name="s"),
        scratch_types=[pltpu.VMEM((n_chunks, k), I32), pltpu.VMEM((2, k, d), table.dtype),
                       pltpu.SemaphoreType.DMA((2,)), pltpu.SemaphoreType.DMA((2,))],
        name="sc_row_gather",
    )(table, idx.reshape(n_workers, n_chunks, k))


def _sc_scatter_rows(src, idx):
    n_lists, n_src = idx.shape
    d = src.shape[1]
    n_cores, n_workers, per_worker, k, n_chunks = _sc_chunking(n_src, d)

    def body(src_hbm, idx_hbm, out_hbm, idx_v, rows_v, gsem, osem):
        wid = lax.axis_index("s") * n_cores + lax.axis_index("c")
        base = wid * per_worker
        pltpu.sync_copy(idx_hbm.at[wid], idx_v)
        _sc_pipeline(
            n_chunks,
            lambda j, b: [pltpu.make_async_copy(src_hbm.at[pl.ds(base + j * k, k)], rows_v.at[b], gsem.at[b])],
            lambda j, b: [pltpu.make_async_copy(rows_v.at[b], out_hbm.at[idx_v.at[c, j]], osem.at[b, c])
                          for c in range(n_lists)])

    return pl.kernel(
        body,
        out_type=jax.ShapeDtypeStruct((n_lists * n_src, d), src.dtype),
        mesh=plsc.VectorSubcoreMesh(core_axis_name="c", subcore_axis_name="s"),
        scratch_types=[pltpu.VMEM((n_lists, n_chunks, k), I32), pltpu.VMEM((2, k, d), src.dtype),
                       pltpu.SemaphoreType.DMA((2,)), pltpu.SemaphoreType.DMA((2, n_lists))],
        name="sc_row_scatter",
    )(src, idx.reshape(n_lists, n_workers, n_chunks, k).transpose(1, 0, 2, 3))


IT_EXPERT, IT_TILE, IT_LO, IT_HI, IT_FIRST, IT_NEWEXP, IT_SLOT, IT_NEXT = range(8)


def _moe_kernel(it_ref, xs_ref, wg_hbm, wu_hbm, wd_hbm, o_ref, wg_f, wu_f, wd_f, cursor, sem, *, layer):
    t = pl.program_id(0)
    tm = xs_ref.shape[0]

    def weight_copies(expert, slot):
        return [pltpu.make_async_copy(src.at[layer, expert], dst.at[slot], sem.at[slot, j])
                for j, (src, dst) in enumerate(((wg_hbm, wg_f), (wu_hbm, wu_f), (wd_hbm, wd_f)))]

    @pl.when(t == 0)
    def _():
        cursor[0] = 0

    first_item = cursor[0]
    x = _unpack_bf16_pairs(xs_ref[...]).astype(BF16)
    row = t * tm + lax.broadcasted_iota(I32, (tm, 1), 0)

    def in_tile(w):
        return (it_ref[IT_TILE, w] == t) & (it_ref[IT_HI, w] > it_ref[IT_LO, w])

    def work_item(w):
        @pl.when(it_ref[IT_NEWEXP, w] == 1)
        def _():
            expert = it_ref[IT_EXPERT, w]
            slot = it_ref[IT_SLOT, w]
            nxt = it_ref[IT_NEXT, w]

            @pl.when(w == 0)
            def _():
                for cp in weight_copies(expert, slot):
                    cp.start()

            for cp in weight_copies(expert, slot):
                cp.wait()

            @pl.when(nxt < N_EXPERTS)
            def _():
                for cp in weight_copies(nxt, 1 - slot):
                    cp.start()

        slot = it_ref[IT_SLOT, w]
        a = _dot(x, wg_f[slot].astype(BF16))
        u = _dot(x, wu_f[slot].astype(BF16))
        act = (_silu(a) * u).astype(BF16)
        y = _pack_bf16_pairs(_dot(act, wd_f[slot].astype(BF16)))
        mine = (row >= it_ref[IT_LO, w]) & (row < it_ref[IT_HI, w])

        @pl.when(w == first_item)
        def _():
            o_ref[...] = jnp.where(mine, y, jnp.zeros_like(y))

        @pl.when(w != first_item)
        def _():
            o_ref[...] = jnp.where(mine, y, o_ref[...])

        return w + 1

    cursor[0] = lax.while_loop(in_tile, work_item, first_item)


def _moe_experts(items, xs, w_gate, w_up, w_down, layer):
    n_rows, dp = xs.shape
    _, _, d, f = w_gate.shape
    tm = MOE_TILE
    hbm = pl.BlockSpec(memory_space=pl.ANY)
    return pl.pallas_call(
        functools.partial(_moe_kernel, layer=layer),
        out_shape=jax.ShapeDtypeStruct((n_rows, d // 2), jnp.uint32),
        grid_spec=pltpu.PrefetchScalarGridSpec(
            num_scalar_prefetch=1,
            grid=(n_rows // tm,),
            in_specs=[pl.BlockSpec((tm, dp), lambda t, it: (t, 0)), hbm, hbm, hbm],
            out_specs=pl.BlockSpec((tm, d // 2), lambda t, it: (t, 0)),
            scratch_shapes=[
                pltpu.VMEM((2, d, f), F32), pltpu.VMEM((2, d, f), F32), pltpu.VMEM((2, f, d), F32),
                pltpu.SMEM((1,), I32), pltpu.SemaphoreType.DMA((2, 3)),
            ],
        ),
        compiler_params=_params(("arbitrary",)),
        name="moe_grouped_mlp",
    )(items, xs, w_gate, w_up, w_down)


def _final_postnorm_kernel(*refs, n_prompt_tiles):
    x2 = _finish_moe(*refs[:6])
    out_prompt, out_latent = refs[6:]

    @pl.when(pl.program_id(0) < n_prompt_tiles)
    def _():
        out_prompt[...] = x2

    @pl.when(pl.program_id(0) >= n_prompt_tiles)
    def _():
        out_latent[...] = x2


def _final_postnorm(stream, n_prompt, dec_seq, t_tok):
    d = stream[1][0].shape[1]
    tl = _Tiles(n_prompt, dec_seq, TOKEN_TILE)
    tm = tl.tm
    return pl.pallas_call(
        functools.partial(_final_postnorm_kernel, n_prompt_tiles=tl.npt),
        out_shape=(jax.ShapeDtypeStruct((n_prompt, d), F32), jax.ShapeDtypeStruct((t_tok - n_prompt, d), F32)),
        grid=(t_tok // tm,),
        in_specs=tl.stream_specs(stream, t_tok // tm),
        out_specs=(pl.BlockSpec((tm, d), tl.prompt_part), pl.BlockSpec((tm, d), tl.latent_part)),
        compiler_params=_params(("arbitrary",)),
        name="moe_combine_postnorm",
    )(*_stream_args(stream))


def _rope_tables(n_lat):
    nf = HEAD_DIM // 4
    s = np.arange(n_lat)
    lane = np.arange(LANES)
    inv = ROPE_BASE ** (-(lane % nf).astype(np.float64) / nf)
    use_col = (lane % HEAD_DIM) >= HEAD_DIM // 2
    p = np.where(use_col[None, :], (s % GRID_W)[:, None], (s // GRID_W)[:, None]).astype(np.float64)
    ang = p * inv[None, :]
    sign = np.where((lane % (2 * nf)) < nf, -1.0, 1.0)
    return jnp.asarray(np.cos(ang), F32), jnp.asarray(np.sin(ang) * sign[None, :], F32)


def _dft_tables(n):
    k = np.arange(n)
    ang = 2.0 * np.pi * ((k[:, None] * k[None, :]) % n).astype(np.float64) / n
    return np.cos(ang), np.sin(ang)


def kernel(x_prompt, x_sample, cache_k, cache_v, c, c_ctx, w_ada, b_ada, ln_gain, ln_bias, w_qkv, w_attn_out,
           lambda_q1, lambda_k1, lambda_q2, lambda_k2, subln_gain, w_fourier_out, w_router_group,
           w_router_expert, w_expert_gate, w_expert_up, w_expert_down):
    bp, sp, d = x_prompt.shape
    bs, n_lat, _ = x_sample.shape
    n_ctx = cache_k.shape[2]
    n_prompt = bp * sp
    t_tok = n_prompt + bs * n_lat
    assert d == D_MODEL and n_prompt % n_lat == 0 and n_lat % TOKEN_TILE == 0 and sp == QKV_TILE

    cond = jnp.concatenate([c_ctx[None, :], c, jnp.zeros((SUBLANES - 1 - bs, d), F32)], axis=0)
    mods = _ada_all(cond, w_ada, b_ada).reshape(DEPTH, SUBLANES, N_MOD, d)

    cos, sin = _rope_tables(n_lat)
    cc, sc = _dft_tables(FOURIER_GROUP_DIM)
    dcs = jnp.asarray(np.concatenate([cc, sc], axis=1), BF16)
    seq_tabs = {s: tuple(jnp.asarray(m, BF16) for m in _dft_tables(s)) for s in (sp, n_lat)}
    attn_layers = [i for i in range(DEPTH) if i % 2 == 0]

    stream = ("pair", (x_prompt.reshape(n_prompt, d), x_sample.reshape(bs * n_lat, d)))
    prev_kv = []
    new_k = new_v = None
    for i in range(DEPTH):
        mod = mods[i]
        if i % 2 == 0:
            a = i // 2
            lam_init = 0.8 - 0.6 * math.exp(-0.3 * i)
            last_attn = i == attn_layers[-1]
            q, k, v, kf, vf, *finished = _qkv(stream, mod, w_qkv[a].astype(BF16), cos, sin, prev_kv, last_attn,
                                              n_prompt, n_lat, t_tok)
            if last_attn:
                new_k, new_v = kf, vf
            else:
                prev_kv.append((kf, vf))
            lam_vecs = jnp.stack([lambda_q1[a], lambda_k1[a], lambda_q2[a], lambda_k2[a]], axis=0)
            gain = subln_gain[a][None, :]
            kc = cache_k[:, a].reshape(bs * n_ctx, d).astype(BF16)
            vc = cache_v[:, a].reshape(bs * n_ctx, d).astype(BF16)
            mixed_p = _attn_prompt(lam_vecs, gain, q, k, v, bp, sp, lam_init)
            mixed_s = _attn_sample(lam_vecs, gain, q, k, v, kc, vc, bs, n_lat, n_ctx, n_prompt, lam_init)
            w_mix = w_attn_out[a]
        else:
            xc, xsn, *finished = _chan_dft(stream, mod, dcs, n_prompt, n_lat, t_tok)
            mixed_p = _seq_dft(*seq_tabs[sp], xc, xsn, bp, sp, 0)
            mixed_s = _seq_dft(*seq_tabs[n_lat], xc, xsn, bs, n_lat, n_prompt)
            w_mix = w_fourier_out[i // 2]
        if finished:
            stream = ("merged", (finished[0],))
        gb0 = jnp.stack([ln_gain[i, 0], ln_bias[i, 0]], axis=0)
        gb1 = jnp.stack([ln_gain[i, 1], ln_bias[i, 1]], axis=0)
        wr = jnp.concatenate([w_router_group[i], w_router_expert[i],
                              jnp.zeros((d, LANES - N_EXPERT_GROUPS - N_EXPERTS), F32)], axis=1)
        wr_hi = wr.astype(BF16)
        wr_lo = (wr - wr_hi.astype(F32)).astype(BF16)
        x1, hp, route, bucket_counts = _mix_out(mixed_p, mixed_s, w_mix.astype(BF16), stream, mod, gb0,
                                                jnp.stack([wr_hi, wr_lo], axis=0), n_prompt, n_lat, t_tok)
        pos8, items = _moe_plan(route, bucket_counts)
        pos = pos8[0:2]
        xs_sorted = _sc_scatter_rows(hp, pos)
        ys = _moe_experts(items, xs_sorted, w_expert_gate, w_expert_up, w_expert_down, i)
        yg = _sc_gather_rows(ys, pos.reshape(-1))
        stream = ("pending", (x1, yg, route, mods[i], gb1))

    out_prompt, out_latent = _final_postnorm(stream, n_prompt, n_lat, t_tok)
    y_prompt = out_prompt.reshape(bp, sp, d)
    y_sample = out_latent.reshape(bs, n_lat, d)
    return (y_prompt, y_sample, new_k.reshape(bp, len(attn_layers), sp, N_HEADS, 2 * HEAD_DIM),
            new_v.reshape(bp, len(attn_layers), sp, N_HEADS, V_DIM))
```

```python
import functools
import math

import numpy as np
import jax
import jax.numpy as jnp
from jax import lax
from jax.experimental import pallas as pl
from jax.experimental.pallas import tpu as pltpu
from jax.experimental.pallas import tpu_sc as plsc

F32 = jnp.float32
BF16 = jnp.bfloat16
I32 = jnp.int32

D_MODEL = 1024
DEPTH = 4
GRID_W = 64
N_HEADS = 8
HEAD_DIM = 64
V_DIM = 2 * HEAD_DIM
ROPE_BASE = 10000.0
N_FOURIER_GROUPS = 8
FOURIER_GROUP_DIM = D_MODEL // N_FOURIER_GROUPS
N_EXPERT_GROUPS = 4
EXPERTS_PER_GROUP = 8
N_EXPERTS = N_EXPERT_GROUPS * EXPERTS_PER_GROUP
D_EXPERT = 256
N_MOD = 6
LN_EPS = 1e-5
DEEPNORM_ALPHA = (2.0 * DEPTH) ** 0.25
Q_SCALE = math.log2(math.e) * HEAD_DIM ** -0.5

LANES = 128
SUBLANES = 8
TOKEN_TILE = 512
CHAIN_ROWS = 256
QKV_TILE = 256
Q_TILE = 2048
ATTN_ROW_CHUNK = 128
MOE_TILE = 512
PLAN_TILE = 1024
SC_BUFFER_BYTES = 128 * 1024
SC_MAX_INDEX_CHUNK = 128
VMEM_LIMIT = 48 * 1024 * 1024


def _params(semantics):
    return pltpu.CompilerParams(dimension_semantics=semantics, vmem_limit_bytes=VMEM_LIMIT)


def _layernorm(x, eps=LN_EPS):
    mu = jnp.mean(x, axis=-1, keepdims=True)
    xc = x - mu
    var = jnp.mean(xc * xc, axis=-1, keepdims=True)
    return xc * lax.rsqrt(var + eps)


def _deepnorm(x, branch_gate, branch):
    return _layernorm(x + (branch_gate * (1.0 / DEEPNORM_ALPHA)) * branch, LN_EPS / DEEPNORM_ALPHA ** 2)


def _silu(a):
    return a / (1.0 + jnp.exp(-a))


def _dot(a, b):
    return jnp.dot(a, b, preferred_element_type=F32)


class _Tiles:
    def __init__(self, n_prompt, dec_seq, tm):
        self.tm = tm
        self.npt = n_prompt // tm
        self.tps = dec_seq // tm

    def row(self, t):
        return (t, 0)

    def cond(self, t):
        return (jnp.where(t < self.npt, 0, (t - self.npt) // self.tps + 1), 0, 0)

    def latent_pos(self, t):
        return (jnp.maximum(t - self.npt, 0) % self.tps, 0)

    def prompt_part(self, t):
        return (jnp.minimum(t, self.npt - 1), 0)

    def latent_part(self, t):
        return (jnp.maximum(t - self.npt, 0), 0)

    def stream_specs(self, stream, n_tiles):
        kind, arrays = stream
        tm = self.tm
        if kind == "merged":
            return [pl.BlockSpec((tm, arrays[0].shape[1]), self.row)]
        if kind == "pair":
            return [pl.BlockSpec((tm, arrays[0].shape[1]), self.prompt_part),
                    pl.BlockSpec((tm, arrays[1].shape[1]), self.latent_part)]
        x1, yg, route, mod, gb = arrays
        return [pl.BlockSpec((tm, x1.shape[1]), self.row),
                pl.BlockSpec((tm, yg.shape[1]), self.row),
                pl.BlockSpec((tm, yg.shape[1]), lambda t: (t + n_tiles, 0)),
                pl.BlockSpec((tm, route.shape[1]), self.row),
                pl.BlockSpec((None,) + mod.shape[1:], self.cond),
                pl.BlockSpec(gb.shape, lambda t: (0, 0))]


_STREAM_REFS = {"merged": 1, "pair": 2, "pending": 6}


def _stream_args(stream):
    kind, arrays = stream
    if kind == "pending":
        x1, yg, route, mod, gb = arrays
        return [x1, yg, yg, route, mod, gb]
    return list(arrays)


def _finish_moe(x1_ref, y0_ref, y1_ref, route_ref, mod_ref, gb_ref, rows=slice(None)):
    r = route_ref[rows, :]
    moe = r[:, 2:3] * _unpack_bf16_pairs(y0_ref[rows, :]) + r[:, 3:4] * _unpack_bf16_pairs(y1_ref[rows, :])
    return _deepnorm(x1_ref[rows, :], mod_ref[5:6, :], moe) * gb_ref[0:1, :] + gb_ref[1:2, :]


def _stream_rows(kind, refs, is_prompt_tile, rows=slice(None)):
    if kind == "merged":
        return refs[0][rows, :]
    if kind == "pair":
        return jnp.where(is_prompt_tile, refs[0][rows, :], refs[1][rows, :])
    return _finish_moe(*refs, rows)


def _ada_kernel(cond_ref, w_ref, b_ref, o_ref):
    a = _silu(cond_ref[...])
    o_ref[...] = _dot(a.astype(BF16), w_ref[...].astype(BF16)) + b_ref[...]


def _ada_all(cond, w_ada, b_ada):
    depth, d, n = w_ada.shape
    tn = n // 4
    return pl.pallas_call(
        _ada_kernel,
        out_shape=jax.ShapeDtypeStruct((depth, cond.shape[0], n), F32),
        grid=(depth, n // tn),
        in_specs=[
            pl.BlockSpec(cond.shape, lambda l, j: (0, 0)),
            pl.BlockSpec((None, d, tn), lambda l, j: (l, 0, j)),
            pl.BlockSpec((None, 1, tn), lambda l, j: (l, 0, j)),
        ],
        out_specs=pl.BlockSpec((None, cond.shape[0], tn), lambda l, j: (l, 0, j)),
        compiler_params=_params(("arbitrary", "arbitrary")),
        name="ada",
    )(cond, w_ada, b_ada.reshape(depth, 1, n))


def _rope(x, cos, sin_signed, first_half):
    outs = []
    for c in range(x.shape[1] // LANES):
        xc = x[:, c * LANES:(c + 1) * LANES]
        partner = jnp.where(first_half, pltpu.roll(xc, LANES - 16, 1), pltpu.roll(xc, 16, 1))
        outs.append(xc * cos + partner * sin_signed)
    return jnp.concatenate(outs, axis=1)


def _store_heads(cache_ref, slot, rows):
    cache_ref[slot] = pltpu.einshape("s(hd)->shd", rows, h=N_HEADS)


def _qkv_kernel(*refs, n_prompt_tiles, stream_kind, n_prev):
    n_x = _STREAM_REFS[stream_kind]
    x_refs, (mod_ref, w_ref, cos_ref, sin_ref) = refs[:n_x], refs[n_x:n_x + 4]
    prev = refs[n_x + 4:n_x + 4 + 2 * max(n_prev, 0)]
    q_ref, k_ref, v_ref, ko_ref, vo_ref = refs[n_x + 4 + 2 * max(n_prev, 0):][:5]
    t = pl.program_id(0)
    x = _stream_rows(stream_kind, x_refs, t < n_prompt_tiles)
    if stream_kind == "pending":
        refs[-1][...] = x
    h = (_layernorm(x) * (1.0 + mod_ref[1:2, :]) + mod_ref[0:1, :]).astype(BF16)
    d = w_ref.shape[0]
    project = lambda j: _dot(h, w_ref[:, j * d:(j + 1) * d])

    @pl.when(t < n_prompt_tiles)
    def _():
        q_ref[...] = (project(0) * Q_SCALE).astype(BF16)
        k = project(1)
        k_ref[...] = k.astype(BF16)
        v = project(2)
        v_ref[...] = v.astype(BF16)
        if n_prev < 0:
            ko_ref[...] = k
            vo_ref[...] = v
        else:
            for a in range(n_prev):
                _store_heads(ko_ref, a, prev[2 * a][...])
                _store_heads(vo_ref, a, prev[2 * a + 1][...])
            _store_heads(ko_ref, n_prev, k)
            _store_heads(vo_ref, n_prev, v)

    @pl.when(t >= n_prompt_tiles)
    def _():
        lane = lax.broadcasted_iota(I32, (h.shape[0], LANES), 1)
        first_half = (lane % 32) < 16
        cos = cos_ref[...]
        sin = sin_ref[...]
        q_ref[...] = _rope(project(0) * Q_SCALE, cos, sin, first_half).astype(BF16)
        k_ref[...] = _rope(project(1), cos, sin, first_half).astype(BF16)
        v_ref[...] = project(2).astype(BF16)


def _qkv(stream, mod, w_qkv, cos, sin, prev_kv, finish_cache, n_prompt, dec_seq, t_tok):
    d = w_qkv.shape[0]
    tl = _Tiles(n_prompt, dec_seq, QKV_TILE)
    tm = tl.tm
    pending = stream[0] == "pending"
    row_out = jax.ShapeDtypeStruct((t_tok, d), F32), pl.BlockSpec((tm, d), tl.row)
    n_prev = len(prev_kv) if finish_cache else -1
    if finish_cache:
        n_slots = n_prev + 1
        kv_shape = jax.ShapeDtypeStruct((n_prompt // tm, n_slots, tm, N_HEADS, V_DIM), F32)
        kv_spec = pl.BlockSpec((None, n_slots, tm, N_HEADS, V_DIM), lambda t: (tl.prompt_part(t)[0], 0, 0, 0, 0))
    else:
        kv_shape = jax.ShapeDtypeStruct((n_prompt, d), F32)
        kv_spec = pl.BlockSpec((tm, d), tl.prompt_part)
    prev_flat = [a for kv in prev_kv for a in kv] if finish_cache else []
    return pl.pallas_call(
        functools.partial(_qkv_kernel, n_prompt_tiles=tl.npt, stream_kind=stream[0], n_prev=n_prev),
        out_shape=(
            jax.ShapeDtypeStruct((t_tok, d), BF16),
            jax.ShapeDtypeStruct((t_tok, d), BF16),
            jax.ShapeDtypeStruct((t_tok, d), BF16),
            kv_shape, kv_shape,
        ) + ((row_out[0],) if pending else ()),
        grid=(t_tok // tm,),
        in_specs=tl.stream_specs(stream, t_tok // tm) + [
            pl.BlockSpec((None, N_MOD, d), tl.cond),
            pl.BlockSpec(w_qkv.shape, lambda t: (0, 0)),
            pl.BlockSpec((tm, LANES), tl.latent_pos),
            pl.BlockSpec((tm, LANES), tl.latent_pos),
        ] + [pl.BlockSpec((tm, d), tl.prompt_part) for _ in prev_flat],
        out_specs=(
            pl.BlockSpec((tm, d), tl.row),
            pl.BlockSpec((tm, d), tl.row),
            pl.BlockSpec((tm, d), tl.row),
            kv_spec, kv_spec,
        ) + ((row_out[1],) if pending else ()),
        compiler_params=_params(("arbitrary",)),
        name="ln_qkv_rope",
    )(*_stream_args(stream), mod, w_qkv, cos, sin, *prev_flat)


def _row_chains(n_rows):
    return [slice(r, r + CHAIN_ROWS) for r in range(0, n_rows, CHAIN_ROWS)]


def _chan_dft_kernel(*refs, n_prompt_tiles, stream_kind):
    n_x = _STREAM_REFS[stream_kind]
    x_refs, (mod_ref, dcs_ref, xc_ref, xs_ref) = refs[:n_x], refs[n_x:n_x + 4]
    g = FOURIER_GROUP_DIM
    is_prompt = pl.program_id(0) < n_prompt_tiles
    for rows in _row_chains(xc_ref.shape[0]):
        x = _stream_rows(stream_kind, x_refs, is_prompt, rows)
        if stream_kind == "pending":
            refs[-1][rows, :] = x
        h = (_layernorm(x) * (1.0 + mod_ref[1:2, :]) + mod_ref[0:1, :]).astype(BF16)
        for i in range(N_FOURIER_GROUPS):
            r = _dot(h[:, i * g:(i + 1) * g], dcs_ref[...])
            xc_ref[rows, i * g:(i + 1) * g] = r[:, :g].astype(BF16)
            xs_ref[rows, i * g:(i + 1) * g] = r[:, g:].astype(BF16)


def _chan_dft(stream, mod, dcs, n_prompt, dec_seq, t_tok):
    d = mod.shape[-1]
    tl = _Tiles(n_prompt, dec_seq, TOKEN_TILE)
    tm = tl.tm
    pending = stream[0] == "pending"
    half = jax.ShapeDtypeStruct((t_tok, d), BF16)
    return pl.pallas_call(
        functools.partial(_chan_dft_kernel, n_prompt_tiles=tl.npt, stream_kind=stream[0]),
        out_shape=(half, half) + ((jax.ShapeDtypeStruct((t_tok, d), F32),) if pending else ()),
        grid=(t_tok // tm,),
        in_specs=tl.stream_specs(stream, t_tok // tm) + [
            pl.BlockSpec((None, N_MOD, d), tl.cond),
            pl.BlockSpec(dcs.shape, lambda t: (0, 0)),
        ],
        out_specs=(pl.BlockSpec((tm, d), tl.row),) * (3 if pending else 2),
        compiler_params=_params(("arbitrary",)),
        name="ln_chan_dft",
    )(*_stream_args(stream), mod, dcs)


def _seq_dft_kernel(cs_ref, ss_ref, xc_ref, xs_ref, o_ref, *, norm):
    f = _dot(cs_ref[...], xc_ref[...]) - _dot(ss_ref[...], xs_ref[...])
    o_ref[...] = (f * norm).astype(BF16)


def _seq_dft(cs, ss, xc, xs, batch, seq, row_offset):
    d = xc.shape[1]
    tm = min(QKV_TILE, seq)
    spt = seq // tm
    off_seq = row_offset // seq
    return pl.pallas_call(
        functools.partial(_seq_dft_kernel, norm=1.0 / math.sqrt(seq * FOURIER_GROUP_DIM)),
        out_shape=jax.ShapeDtypeStruct((batch * seq, d), BF16),
        grid=(batch, spt),
        in_specs=[
            pl.BlockSpec((tm, seq), lambda b, i: (i, 0)),
            pl.BlockSpec((tm, seq), lambda b, i: (i, 0)),
            pl.BlockSpec((seq, d), lambda b, i: (off_seq + b, 0)),
            pl.BlockSpec((seq, d), lambda b, i: (off_seq + b, 0)),
        ],
        out_specs=pl.BlockSpec((tm, d), lambda b, i: (b * spt + i, 0)),
        compiler_params=_params(("arbitrary", "arbitrary")),
        name=f"seq_dft_{seq}",
    )(cs, ss, xc, xs)


def _diff_lambda(lam_ref, lam_init):
    lv = lam_ref[...]
    return (jnp.exp(jnp.sum(lv[0:1] * lv[1:2], axis=-1, keepdims=True))
            - jnp.exp(jnp.sum(lv[2:3] * lv[3:4], axis=-1, keepdims=True)) + lam_init)


def _diff_attn_head(q, k, v_ext, lam, gain, lam_init):
    tq = q.shape[0]
    lane = lax.broadcasted_iota(I32, q.shape, 1)
    zero = jnp.zeros_like(q)
    qq = jnp.concatenate([jnp.where(lane < HEAD_DIM, q, zero), jnp.where(lane >= HEAD_DIM, q, zero)], axis=0)
    parts = []
    for r in range(0, 2 * tq, ATTN_ROW_CHUNK):
        s = lax.dot_general(qq[r:r + ATTN_ROW_CHUNK], k, (((1,), (1,)), ((), ())), preferred_element_type=F32)
        e = jnp.exp2(s - jnp.max(s, axis=-1, keepdims=True)).astype(BF16)
        parts.append(_dot(e, v_ext))
    oe = jnp.concatenate(parts, axis=0)
    o = oe[:, :V_DIM] / oe[:, V_DIM:]
    o = o[:tq] - lam * o[tq:]
    o = o * lax.rsqrt(jnp.mean(o * o, axis=-1, keepdims=True) + LN_EPS)
    return o * gain * (1.0 - lam_init)


def _attn_prompt_kernel(lam_ref, gain_ref, q_ref, k_ref, v_ref, o_ref, *, lam_init):
    lam = _diff_lambda(lam_ref, lam_init)
    gain = gain_ref[...]
    ones = jnp.ones((k_ref.shape[0], V_DIM), BF16)
    for hd in range(N_HEADS):
        cols = slice(hd * V_DIM, (hd + 1) * V_DIM)
        v_ext = jnp.concatenate([v_ref[:, cols], ones], axis=1)
        o_ref[:, cols] = _diff_attn_head(q_ref[:, cols], k_ref[:, cols], v_ext, lam, gain, lam_init).astype(BF16)


def _attn_latent_kernel(lam_ref, gain_ref, q_ref, k_ref, v_ref, kc_ref, vc_ref, o_ref, kall_ref, vext_ref, *,
                        lam_init):
    n_new = k_ref.shape[0]

    @pl.when(pl.program_id(2) == 0)
    def _():
        kall_ref[:n_new, :] = k_ref[...]
        kall_ref[n_new:, :] = kc_ref[...]
        vext_ref[:n_new, :V_DIM] = v_ref[...]
        vext_ref[n_new:, :V_DIM] = vc_ref[...]
        vext_ref[:, V_DIM:] = jnp.ones((vext_ref.shape[0], V_DIM), BF16)

    o = _diff_attn_head(q_ref[...], kall_ref[...], vext_ref[...], _diff_lambda(lam_ref, lam_init), gain_ref[...],
                        lam_init)
    o_ref[...] = o.astype(BF16)


def _attn_prompt(lam_vecs, gain, q, k, v, batch, seq, lam_init):
    d = q.shape[1]
    blk = pl.BlockSpec((seq, d), lambda b: (b, 0))
    return pl.pallas_call(
        functools.partial(_attn_prompt_kernel, lam_init=lam_init),
        out_shape=jax.ShapeDtypeStruct((batch * seq, d), BF16),
        grid=(batch,),
        in_specs=[
            pl.BlockSpec(lam_vecs.shape, lambda b: (0, 0)),
            pl.BlockSpec(gain.shape, lambda b: (0, 0)),
            blk, blk, blk,
        ],
        out_specs=blk,
        compiler_params=_params(("arbitrary",)),
        name="diff_attn_ctx",
    )(lam_vecs, gain, q, k, v)


def _attn_sample(lam_vecs, gain, q, k, v, kc, vc, batch, seq, n_ctx, row_offset, lam_init):
    d = q.shape[1]
    tq = Q_TILE
    qpt = seq // tq
    off_seq = row_offset // seq
    off_tile = row_offset // tq
    qmap = lambda b, h, i: (off_tile + b * qpt + i, h)
    kmap = lambda b, h, i: (off_seq + b, h)
    cmap = lambda b, h, i: (b, h)
    return pl.pallas_call(
        functools.partial(_attn_latent_kernel, lam_init=lam_init),
        out_shape=jax.ShapeDtypeStruct((batch * seq, d), BF16),
        grid=(batch, N_HEADS, qpt),
        in_specs=[
            pl.BlockSpec(lam_vecs.shape, lambda b, h, i: (0, 0)),
            pl.BlockSpec(gain.shape, lambda b, h, i: (0, 0)),
            pl.BlockSpec((tq, V_DIM), qmap),
            pl.BlockSpec((seq, V_DIM), kmap),
            pl.BlockSpec((seq, V_DIM), kmap),
            pl.BlockSpec((n_ctx, V_DIM), cmap),
            pl.BlockSpec((n_ctx, V_DIM), cmap),
        ],
        out_specs=pl.BlockSpec((tq, V_DIM), lambda b, h, i: (b * qpt + i, h)),
        scratch_shapes=[pltpu.VMEM((seq + n_ctx, V_DIM), BF16), pltpu.VMEM((seq + n_ctx, 2 * V_DIM), BF16)],
        compiler_params=_params(("arbitrary", "arbitrary", "arbitrary")),
        name="diff_attn_latent",
    )(lam_vecs, gain, q, k, v, kc, vc)


def _route(lg):
    lane = lax.broadcasted_iota(I32, lg.shape, 1)
    lane_f = lane.astype(F32)
    neg = jnp.float32(-jnp.inf)
    big = jnp.float32(LANES)
    gl = jnp.where(lane < N_EXPERT_GROUPS, lg, neg)
    gmax = jnp.max(gl, axis=-1, keepdims=True)
    g_prob = 1.0 / jnp.sum(jnp.exp(gl - gmax), axis=-1, keepdims=True)
    g_idx = jnp.min(jnp.where(gl == gmax, lane_f, big), axis=-1, keepdims=True)
    lo = N_EXPERT_GROUPS + EXPERTS_PER_GROUP * g_idx
    el = jnp.where((lane_f >= lo) & (lane_f < lo + EXPERTS_PER_GROUP), lg, neg)
    m1 = jnp.max(el, axis=-1, keepdims=True)
    i1 = jnp.min(jnp.where(el == m1, lane_f, big), axis=-1, keepdims=True)
    el2 = jnp.where(lane_f == i1, neg, el)
    m2 = jnp.max(el2, axis=-1, keepdims=True)
    i2 = jnp.min(jnp.where(el2 == m2, lane_f, big), axis=-1, keepdims=True)
    t = jnp.exp(m2 - m1)
    w1 = g_prob / (1.0 + t)
    w2 = g_prob * t / (1.0 + t)
    out = jnp.where(lane == 0, i1 - N_EXPERT_GROUPS, 0.0)
    out = jnp.where(lane == 1, i2 - N_EXPERT_GROUPS, out)
    out = jnp.where(lane == 2, w1, out)
    out = jnp.where(lane == 3, w2, out)
    return out


def _pack_bf16_pairs(h):
    n = h.shape[1] // 2
    bits = lax.bitcast_convert_type(h.astype(BF16).astype(F32), jnp.uint32)
    return (bits[:, :n] >> 16) | bits[:, n:]


def _unpack_bf16_pairs(p):
    lo = lax.bitcast_convert_type(p << 16, F32)
    hi = lax.bitcast_convert_type(p & jnp.uint32(0xFFFF0000), F32)
    return jnp.concatenate([lo, hi], axis=1)


def _mix_out_kernel(*refs, n_prompt_tiles, stream_kind):
    n_x = _STREAM_REFS[stream_kind]
    ap_ref, as_ref, w_ref = refs[:3]
    x_refs = refs[3:3 + n_x]
    mod_ref, gb_ref, wr_ref, x1_ref, hp_ref, route_ref = refs[3 + n_x:]
    is_prompt = pl.program_id(0) < n_prompt_tiles
    for rows in _row_chains(x1_ref.shape[0]):
        a = jnp.where(is_prompt, ap_ref[rows, :], as_ref[rows, :])
        out = _dot(a, w_ref[...])
        x = _stream_rows(stream_kind, x_refs, is_prompt, rows)
        x1 = _deepnorm(x, mod_ref[2:3, :], out) * gb_ref[0:1, :] + gb_ref[1:2, :]
        x1_ref[rows, :] = x1
        h2 = _layernorm(x1) * (1.0 + mod_ref[4:5, :]) + mod_ref[3:4, :]
        hp_ref[rows, :] = _pack_bf16_pairs(h2)
        route_ref[rows, :] = _route(_dot(h2.astype(BF16), wr_ref[...]))


def _mix_out(a_prompt, a_sample, w, stream, mod, gb, wr, n_prompt, dec_seq, t_tok):
    d = w.shape[1]
    tl = _Tiles(n_prompt, dec_seq, TOKEN_TILE)
    tm = tl.tm
    return pl.pallas_call(
        functools.partial(_mix_out_kernel, n_prompt_tiles=tl.npt, stream_kind=stream[0]),
        out_shape=(
            jax.ShapeDtypeStruct((t_tok, d), F32),
            jax.ShapeDtypeStruct((t_tok, d // 2), jnp.uint32),
            jax.ShapeDtypeStruct((t_tok, LANES), F32),
        ),
        grid=(t_tok // tm,),
        in_specs=[
            pl.BlockSpec((tm, d), tl.prompt_part),
            pl.BlockSpec((tm, d), tl.latent_part),
            pl.BlockSpec(w.shape, lambda t: (0, 0)),
        ] + tl.stream_specs(stream, t_tok // tm) + [
            pl.BlockSpec((None, N_MOD, d), tl.cond),
            pl.BlockSpec(gb.shape, lambda t: (0, 0)),
            pl.BlockSpec(wr.shape, lambda t: (0, 0)),
        ],
        out_specs=(
            pl.BlockSpec((tm, d), tl.row),
            pl.BlockSpec((tm, d // 2), tl.row),
            pl.BlockSpec((tm, LANES), tl.row),
        ),
        compiler_params=_params(("arbitrary",)),
        name="mix_out_postnorm_router",
    )(a_prompt, a_sample, w, *_stream_args(stream), mod, gb, wr)


def _lane_prefix_sum(x, lane):
    sh = 1
    while sh < LANES:
        x = x + jnp.where(lane >= sh, pltpu.roll(x, sh, 1), 0.0)
        sh *= 2
    return x


def _lane_suffix_min_exclusive(x, lane):
    big = float(LANES)
    y = jnp.where(lane + 1 < LANES, pltpu.roll(x, LANES - 1, 1), big)
    sh = 1
    while sh < LANES:
        y = jnp.minimum(y, jnp.where(lane + sh < LANES, pltpu.roll(y, LANES - sh, 1), big))
        sh *= 2
    return y


def _work_items(counts, starts, ends, lane8, n_moe_tiles):
    shift = int(math.log2(MOE_TILE))
    first_tile = (starts.astype(I32) >> shift).astype(F32)
    last_tile = ((ends.astype(I32) - 1) >> shift).astype(F32)
    n_it = jnp.where(counts > 0.0, last_tile - first_tile + 1.0, 0.0)
    it_end = _lane_prefix_sum(n_it, lane8)
    it_start = it_end - n_it
    total = it_end[0:1, LANES - 1:LANES]
    sub = lax.broadcasted_iota(I32, (LANES, LANES), 0)
    rows = lambda x: jnp.broadcast_to(x[0:1, :], (LANES, LANES))
    used = counts > 0.0
    weight_slot = ((_lane_prefix_sum(jnp.where(used, 1.0, 0.0), lane8) - 1.0).astype(I32) & 1).astype(F32)
    next_used = _lane_suffix_min_exclusive(jnp.where(used, lane8.astype(F32), float(LANES)), lane8)
    per_expert = (starts, ends, first_tile, it_start, it_end, weight_slot, next_used)
    stacked = jnp.zeros((LANES, LANES), F32)
    for j, vec in enumerate(per_expert):
        stacked = jnp.where(sub == j, rows(vec), stacked)
    cols = stacked.T
    col = lambda j: cols[:, j:j + 1]
    sub_f = sub.astype(F32)
    w = lax.broadcasted_iota(I32, (LANES, LANES), 1).astype(F32)
    ex = jnp.sum(jnp.where((sub < N_EXPERTS) & (col(4) <= w), 1.0, 0.0), axis=0, keepdims=True)
    ex = jnp.minimum(ex, N_EXPERTS - 1.0)
    w1 = w[0:1, :]
    valid = w1 < total
    ex = jnp.where(valid, ex, jnp.max(jnp.where(valid, ex, 0.0), axis=-1, keepdims=True))
    onehot = sub_f == ex
    pick = lambda j: jnp.sum(jnp.where(onehot, col(j), 0.0), axis=0, keepdims=True)
    tile = jnp.where(valid, pick(2) + (w1 - pick(3)), n_moe_tiles - 1.0)
    lo = jnp.where(valid, jnp.maximum(pick(0), tile * MOE_TILE), 0.0)
    hi = jnp.where(valid, jnp.minimum(pick(1), (tile + 1.0) * MOE_TILE), 0.0)
    b8 = lambda x: jnp.broadcast_to(x, (SUBLANES, LANES))
    ex8, tile8 = b8(ex), b8(tile)
    first = jnp.where((lane8 == 0) | (tile8 != pltpu.roll(tile8, 1, 1)), 1.0, 0.0)
    newexp = jnp.where((lane8 == 0) | (ex8 != pltpu.roll(ex8, 1, 1)), 1.0, 0.0)
    sub8 = lax.broadcasted_iota(I32, (SUBLANES, LANES), 0)
    table = jnp.zeros((SUBLANES, LANES), F32)
    for j, vec in enumerate((ex8, tile8, b8(lo), b8(hi), first, newexp, b8(pick(5)), b8(pick(6)))):
        table = jnp.where(sub8 == j, vec, table)
    return table


def _plan_kernel(route_ref, pos_ref, items_ref, tri_ref, carry_ref, tot_ref, *, n_moe_tiles):
    p = pl.program_id(0)
    t = pl.program_id(1)
    tm = route_ref.shape[0]
    r = route_ref[...]
    lane = lax.broadcasted_iota(I32, (tm, LANES), 1)
    lane_f = lane.astype(F32)
    e0 = r[:, 0:1]
    e1 = r[:, 1:2] + N_EXPERTS
    m = jnp.where((lane_f == e0) | (lane_f == e1), 1.0, 0.0)
    colsum = jnp.sum(m, axis=0, keepdims=True)

    @pl.when((p == 0) & (t == 0))
    def _():
        carry_ref[...] = jnp.zeros_like(carry_ref)
        row = lax.broadcasted_iota(I32, (tm, tm), 0)
        col = lax.broadcasted_iota(I32, (tm, tm), 1)
        tri_ref[...] = jnp.where(row > col, 1.0, 0.0).astype(BF16)

    @pl.when(p == 0)
    def _():
        carry_ref[...] += colsum

    @pl.when((p == 0) & (t == pl.num_programs(1) - 1))
    def _():
        tot_ref[...] = carry_ref[...]
        carry_ref[...] = jnp.zeros_like(carry_ref)

    @pl.when(p == 1)
    def _():
        lane8 = lax.broadcasted_iota(I32, (SUBLANES, LANES), 1)
        tot = tot_ref[...]
        is_first = lane8 < N_EXPERTS
        tot0 = jnp.where(is_first, tot, 0.0)
        counts = jnp.where(is_first, tot + pltpu.roll(tot, LANES - N_EXPERTS, 1), 0.0)
        ends = _lane_prefix_sum(counts, lane8)
        starts = ends - counts
        base = jnp.where(is_first, starts, pltpu.roll(starts + tot0, N_EXPERTS, 1))
        before = _dot(tri_ref[...], m.astype(BF16)) + carry_ref[0:1, :]
        carry_ref[...] += colsum
        rows = before + base[0:1, :]
        pos0 = jnp.sum(jnp.where(lane_f == e0, rows, 0.0), axis=-1, keepdims=True)
        pos1 = jnp.sum(jnp.where(lane_f == e1, rows, 0.0), axis=-1, keepdims=True)
        both = jnp.where(lane == 0, pos0, jnp.where(lane == 1, pos1, 0.0))
        pos_ref[...] = both.T[0:SUBLANES, :].astype(I32)

        @pl.when(t == 0)
        def _():
            items_ref[...] = _work_items(counts, starts, ends, lane8, n_moe_tiles).astype(I32)


def _moe_plan(route):
    t_tok = route.shape[0]
    tm = PLAN_TILE
    n_moe_tiles = 2 * t_tok // MOE_TILE
    assert n_moe_tiles + N_EXPERTS - 1 <= LANES and 2 * N_EXPERTS <= LANES and MOE_TILE & (MOE_TILE - 1) == 0
    return pl.pallas_call(
        functools.partial(_plan_kernel, n_moe_tiles=n_moe_tiles),
        out_shape=(jax.ShapeDtypeStruct((SUBLANES, t_tok), I32), jax.ShapeDtypeStruct((SUBLANES, LANES), I32)),
        grid=(2, t_tok // tm),
        in_specs=[pl.BlockSpec((tm, LANES), lambda p, t: (t, 0))],
        out_specs=(pl.BlockSpec((SUBLANES, tm), lambda p, t: (0, t * p)),
                   pl.BlockSpec((SUBLANES, LANES), lambda p, t: (0, 0))),
        scratch_shapes=[pltpu.VMEM((tm, tm), BF16), pltpu.VMEM((SUBLANES, LANES), F32),
                        pltpu.VMEM((SUBLANES, LANES), F32)],
        compiler_params=_params(("arbitrary", "arbitrary")),
        name="moe_positions",
    )(route)


def _sc_workers():
    info = plsc.get_sparse_core_info()
    return info.num_cores, info.num_cores * info.num_subcores


def _sc_pipeline(n_chunks, loads, stores):
    def start(copies):
        for cp in copies:
            cp.start()

    def wait(copies):
        for cp in copies:
            cp.wait()

    start(loads(0, 0))
    for j in range(n_chunks):
        b = j % 2
        if j + 1 < n_chunks:
            if j >= 1:
                wait(stores(j - 1, 1 - b))
            start(loads(j + 1, 1 - b))
        wait(loads(j, b))
        start(stores(j, b))
    if n_chunks >= 2:
        wait(stores(n_chunks - 2, n_chunks % 2))
    wait(stores(n_chunks - 1, (n_chunks - 1) % 2))


def _sc_chunking(n_rows, d):
    n_cores, n_workers = _sc_workers()
    per_worker = n_rows // n_workers
    k = SC_BUFFER_BYTES // (d * 4)
    n_chunks = per_worker // k
    assert n_chunks * k * n_workers == n_rows and k <= SC_MAX_INDEX_CHUNK
    return n_cores, n_workers, per_worker, k, n_chunks


def _sc_gather_rows(table, idx):
    n, d = idx.shape[0], table.shape[1]
    n_cores, n_workers, per_worker, k, n_chunks = _sc_chunking(n, d)

    def body(table_hbm, idx_hbm, out_hbm, idx_v, rows_v, gsem, osem):
        wid = lax.axis_index("s") * n_cores + lax.axis_index("c")
        base = wid * per_worker
        pltpu.sync_copy(idx_hbm.at[wid], idx_v)
        _sc_pipeline(
            n_chunks,
            lambda j, b: [pltpu.make_async_copy(table_hbm.at[idx_v.at[j]], rows_v.at[b], gsem.at[b])],
            lambda j, b: [pltpu.make_async_copy(rows_v.at[b], out_hbm.at[pl.ds(base + j * k, k)], osem.at[b])])

    return pl.kernel(
        body,
        out_type=jax.ShapeDtypeStruct((n, d), table.dtype),
        mesh=plsc.VectorSubcoreMesh(core_axis_name="c", subcore_axis_name="s"),
        scratch_types=[pltpu.VMEM((n_chunks, k), I32), pltpu.VMEM((2, k, d), table.dtype),
                       pltpu.SemaphoreType.DMA((2,)), pltpu.SemaphoreType.DMA((2,))],
        name="sc_row_gather",
    )(table, idx.reshape(n_workers, n_chunks, k))


def _sc_scatter_rows(src, idx):
    n_lists, n_src = idx.shape
    d = src.shape[1]
    n_cores, n_workers, per_worker, k, n_chunks = _sc_chunking(n_src, d)

    def body(src_hbm, idx_hbm, out_hbm, idx_v, rows_v, gsem, osem):
        wid = lax.axis_index("s") * n_cores + lax.axis_index("c")
        base = wid * per_worker
        pltpu.sync_copy(idx_hbm.at[wid], idx_v)
        _sc_pipeline(
            n_chunks,
            lambda j, b: [pltpu.make_async_copy(src_hbm.at[pl.ds(base + j * k, k)], rows_v.at[b], gsem.at[b])],
            lambda j, b: [pltpu.make_async_copy(rows_v.at[b], out_hbm.at[idx_v.at[c, j]], osem.at[b, c])
                          for c in range(n_lists)])

    return pl.kernel(
        body,
        out_type=jax.ShapeDtypeStruct((n_lists * n_src, d), src.dtype),
        mesh=plsc.VectorSubcoreMesh(core_axis_name="c", subcore_axis_name="s"),
        scratch_types=[pltpu.VMEM((n_lists, n_chunks, k), I32), pltpu.VMEM((2, k, d), src.dtype),
                       pltpu.SemaphoreType.DMA((2,)), pltpu.SemaphoreType.DMA((2, n_lists))],
        name="sc_row_scatter",
    )(src, idx.reshape(n_lists, n_workers, n_chunks, k).transpose(1, 0, 2, 3))


IT_EXPERT, IT_TILE, IT_LO, IT_HI, IT_FIRST, IT_NEWEXP, IT_SLOT, IT_NEXT = range(8)


def _moe_kernel(it_ref, xs_ref, wg_hbm, wu_hbm, wd_hbm, o_ref, wg_f, wu_f, wd_f, sem, *, layer):
    w = pl.program_id(0)

    def weight_copies(expert, slot):
        return [pltpu.make_async_copy(src.at[layer, expert], dst.at[slot], sem.at[slot, j])
                for j, (src, dst) in enumerate(((wg_hbm, wg_f), (wu_hbm, wu_f), (wd_hbm, wd_f)))]

    @pl.when(it_ref[IT_NEWEXP, w] == 1)
    def _():
        expert = it_ref[IT_EXPERT, w]
        slot = it_ref[IT_SLOT, w]
        nxt = it_ref[IT_NEXT, w]

        @pl.when(w == 0)
        def _():
            for cp in weight_copies(expert, slot):
                cp.start()

        for cp in weight_copies(expert, slot):
            cp.wait()

        @pl.when(nxt < N_EXPERTS)
        def _():
            for cp in weight_copies(nxt, 1 - slot):
                cp.start()

    slot = it_ref[IT_SLOT, w]
    tm = xs_ref.shape[0]
    x = _unpack_bf16_pairs(xs_ref[...]).astype(BF16)
    a = _dot(x, wg_f[slot].astype(BF16))
    u = _dot(x, wu_f[slot].astype(BF16))
    act = (_silu(a) * u).astype(BF16)
    y = _pack_bf16_pairs(_dot(act, wd_f[slot].astype(BF16)))
    row = it_ref[IT_TILE, w] * tm + lax.broadcasted_iota(I32, (tm, 1), 0)
    mine = (row >= it_ref[IT_LO, w]) & (row < it_ref[IT_HI, w])

    @pl.when(it_ref[IT_FIRST, w] == 1)
    def _():
        o_ref[...] = jnp.where(mine, y, jnp.zeros_like(y))

    @pl.when(it_ref[IT_FIRST, w] == 0)
    def _():
        o_ref[...] = jnp.where(mine, y, o_ref[...])


def _moe_experts(items, xs, w_gate, w_up, w_down, layer):
    n_rows, dp = xs.shape
    _, _, d, f = w_gate.shape
    tm = MOE_TILE
    rmap = lambda w, it: (it[IT_TILE, w], 0)
    hbm = pl.BlockSpec(memory_space=pl.ANY)
    return pl.pallas_call(
        functools.partial(_moe_kernel, layer=layer),
        out_shape=jax.ShapeDtypeStruct((n_rows, d // 2), jnp.uint32),
        grid_spec=pltpu.PrefetchScalarGridSpec(
            num_scalar_prefetch=1,
            grid=(n_rows // tm + N_EXPERTS - 1,),
            in_specs=[pl.BlockSpec((tm, dp), rmap), hbm, hbm, hbm],
            out_specs=pl.BlockSpec((tm, d // 2), rmap),
            scratch_shapes=[
                pltpu.VMEM((2, d, f), F32), pltpu.VMEM((2, d, f), F32), pltpu.VMEM((2, f, d), F32),
                pltpu.SemaphoreType.DMA((2, 3)),
            ],
        ),
        compiler_params=_params(("arbitrary",)),
        name="moe_grouped_mlp",
    )(items, xs, w_gate, w_up, w_down)


def _final_postnorm_kernel(*refs, n_prompt_tiles):
    x2 = _finish_moe(*refs[:6])
    out_prompt, out_latent = refs[6:]

    @pl.when(pl.program_id(0) < n_prompt_tiles)
    def _():
        out_prompt[...] = x2

    @pl.when(pl.program_id(0) >= n_prompt_tiles)
    def _():
        out_latent[...] = x2


def _final_postnorm(stream, n_prompt, dec_seq, t_tok):
    d = stream[1][0].shape[1]
    tl = _Tiles(n_prompt, dec_seq, TOKEN_TILE)
    tm = tl.tm
    return pl.pallas_call(
        functools.partial(_final_postnorm_kernel, n_prompt_tiles=tl.npt),
        out_shape=(jax.ShapeDtypeStruct((n_prompt, d), F32), jax.ShapeDtypeStruct((t_tok - n_prompt, d), F32)),
        grid=(t_tok // tm,),
        in_specs=tl.stream_specs(stream, t_tok // tm),
        out_specs=(pl.BlockSpec((tm, d), tl.prompt_part), pl.BlockSpec((tm, d), tl.latent_part)),
        compiler_params=_params(("arbitrary",)),
        name="moe_combine_postnorm",
    )(*_stream_args(stream))


def _rope_tables(n_lat):
    nf = HEAD_DIM // 4
    s = np.arange(n_lat)
    lane = np.arange(LANES)
    inv = ROPE_BASE ** (-(lane % nf).astype(np.float64) / nf)
    use_col = (lane % HEAD_DIM) >= HEAD_DIM // 2
    p = np.where(use_col[None, :], (s % GRID_W)[:, None], (s // GRID_W)[:, None]).astype(np.float64)
    ang = p * inv[None, :]
    sign = np.where((lane % (2 * nf)) < nf, -1.0, 1.0)
    return jnp.asarray(np.cos(ang), F32), jnp.asarray(np.sin(ang) * sign[None, :], F32)


def _dft_tables(n):
    k = np.arange(n)
    ang = 2.0 * np.pi * ((k[:, None] * k[None, :]) % n).astype(np.float64) / n
    return np.cos(ang), np.sin(ang)


def kernel(x_prompt, x_sample, cache_k, cache_v, c, c_ctx, w_ada, b_ada, ln_gain, ln_bias, w_qkv, w_attn_out,
           lambda_q1, lambda_k1, lambda_q2, lambda_k2, subln_gain, w_fourier_out, w_router_group,
           w_router_expert, w_expert_gate, w_expert_up, w_expert_down):
    bp, sp, d = x_prompt.shape
    bs, n_lat, _ = x_sample.shape
    n_ctx = cache_k.shape[2]
    n_prompt = bp * sp
    t_tok = n_prompt + bs * n_lat
    assert d == D_MODEL and n_prompt % n_lat == 0 and n_lat % TOKEN_TILE == 0 and sp == QKV_TILE

    cond = jnp.concatenate([c_ctx[None, :], c, jnp.zeros((SUBLANES - 1 - bs, d), F32)], axis=0)
    mods = _ada_all(cond, w_ada, b_ada).reshape(DEPTH, SUBLANES, N_MOD, d)

    cos, sin = _rope_tables(n_lat)
    cc, sc = _dft_tables(FOURIER_GROUP_DIM)
    dcs = jnp.asarray(np.concatenate([cc, sc], axis=1), BF16)
    seq_tabs = {s: tuple(jnp.asarray(m, BF16) for m in _dft_tables(s)) for s in (sp, n_lat)}
    attn_layers = [i for i in range(DEPTH) if i % 2 == 0]

    stream = ("pair", (x_prompt.reshape(n_prompt, d), x_sample.reshape(bs * n_lat, d)))
    prev_kv = []
    new_k = new_v = None
    for i in range(DEPTH):
        mod = mods[i]
        if i % 2 == 0:
            a = i // 2
            lam_init = 0.8 - 0.6 * math.exp(-0.3 * i)
            last_attn = i == attn_layers[-1]
            q, k, v, kf, vf, *finished = _qkv(stream, mod, w_qkv[a].astype(BF16), cos, sin, prev_kv, last_attn,
                                              n_prompt, n_lat, t_tok)
            if last_attn:
                new_k, new_v = kf, vf
            else:
                prev_kv.append((kf, vf))
            lam_vecs = jnp.stack([lambda_q1[a], lambda_k1[a], lambda_q2[a], lambda_k2[a]], axis=0)
            gain = subln_gain[a][None, :]
            kc = cache_k[:, a].reshape(bs * n_ctx, d).astype(BF16)
            vc = cache_v[:, a].reshape(bs * n_ctx, d).astype(BF16)
            mixed_p = _attn_prompt(lam_vecs, gain, q, k, v, bp, sp, lam_init)
            mixed_s = _attn_sample(lam_vecs, gain, q, k, v, kc, vc, bs, n_lat, n_ctx, n_prompt, lam_init)
            w_mix = w_attn_out[a]
        else:
            xc, xsn, *finished = _chan_dft(stream, mod, dcs, n_prompt, n_lat, t_tok)
            mixed_p = _seq_dft(*seq_tabs[sp], xc, xsn, bp, sp, 0)
            mixed_s = _seq_dft(*seq_tabs[n_lat], xc, xsn, bs, n_lat, n_prompt)
            w_mix = w_fourier_out[i // 2]
        if finished:
            stream = ("merged", (finished[0],))
        gb0 = jnp.stack([ln_gain[i, 0], ln_bias[i, 0]], axis=0)
        gb1 = jnp.stack([ln_gain[i, 1], ln_bias[i, 1]], axis=0)
        wr = jnp.concatenate([w_router_group[i], w_router_expert[i],
                              jnp.zeros((d, LANES - N_EXPERT_GROUPS - N_EXPERTS), F32)], axis=1)
        x1, hp, route = _mix_out(mixed_p, mixed_s, w_mix.astype(BF16), stream, mod, gb0, wr.astype(BF16),
                                 n_prompt, n_lat, t_tok)
        pos8, items = _moe_plan(route)
        pos = pos8[0:2]
        xs_sorted = _sc_scatter_rows(hp, pos)
        ys = _moe_experts(items, xs_sorted, w_expert_gate, w_expert_up, w_expert_down, i)
        yg = _sc_gather_rows(ys, pos.reshape(-1))
        stream = ("pending", (x1, yg, route, mods[i], gb1))

    out_prompt, out_latent = _final_postnorm(stream, n_prompt, n_lat, t_tok)
    y_prompt = out_prompt.reshape(bp, sp, d)
    y_sample = out_latent.reshape(bs, n_lat, d)
    return (y_prompt, y_sample, new_k.reshape(bp, len(attn_layers), sp, N_HEADS, 2 * HEAD_DIM),
            new_v.reshape(bp, len(attn_layers), sp, N_HEADS, V_DIM))
```

```python
import functools
import math

import numpy as np
import jax
import jax.numpy as jnp
from jax import lax
from jax.experimental import pallas as pl
from jax.experimental.pallas import tpu as pltpu
from jax.experimental.pallas import tpu_sc as plsc

F32 = jnp.float32
BF16 = jnp.bfloat16
I32 = jnp.int32

D_MODEL = 1024
DEPTH = 4
GRID_W = 64
N_HEADS = 8
HEAD_DIM = 64
V_DIM = 2 * HEAD_DIM
ROPE_BASE = 10000.0
N_FOURIER_GROUPS = 8
FOURIER_GROUP_DIM = D_MODEL // N_FOURIER_GROUPS
N_EXPERT_GROUPS = 4
EXPERTS_PER_GROUP = 8
N_EXPERTS = N_EXPERT_GROUPS * EXPERTS_PER_GROUP
D_EXPERT = 256
N_MOD = 6
LN_EPS = 1e-5
DEEPNORM_ALPHA = (2.0 * DEPTH) ** 0.25
Q_SCALE = math.log2(math.e) * HEAD_DIM ** -0.5

LANES = 128
SUBLANES = 8
TOKEN_TILE = 512
CHAIN_ROWS = 256
QKV_TILE = 256
Q_TILE = 2048
ATTN_ROW_CHUNK = 128
MOE_TILE = 512
PLAN_TILE = 1024
SC_BUFFER_BYTES = 128 * 1024
SC_MAX_INDEX_CHUNK = 128
VMEM_LIMIT = 48 * 1024 * 1024


def _params(semantics):
    return pltpu.CompilerParams(dimension_semantics=semantics, vmem_limit_bytes=VMEM_LIMIT)


def _layernorm(x, eps=LN_EPS):
    mu = jnp.mean(x, axis=-1, keepdims=True)
    xc = x - mu
    var = jnp.mean(xc * xc, axis=-1, keepdims=True)
    return xc * lax.rsqrt(var + eps)


def _deepnorm(x, branch_gate, branch):
    return _layernorm(x + (branch_gate * (1.0 / DEEPNORM_ALPHA)) * branch, LN_EPS / DEEPNORM_ALPHA ** 2)


def _silu(a):
    return a / (1.0 + jnp.exp(-a))


def _dot(a, b):
    return jnp.dot(a, b, preferred_element_type=F32)


class _Tiles:
    def __init__(self, n_prompt, dec_seq, tm):
        self.tm = tm
        self.npt = n_prompt // tm
        self.tps = dec_seq // tm

    def row(self, t):
        return (t, 0)

    def cond(self, t):
        return (jnp.where(t < self.npt, 0, (t - self.npt) // self.tps + 1), 0, 0)

    def latent_pos(self, t):
        return (jnp.maximum(t - self.npt, 0) % self.tps, 0)

    def prompt_part(self, t):
        return (jnp.minimum(t, self.npt - 1), 0)

    def latent_part(self, t):
        return (jnp.maximum(t - self.npt, 0), 0)

    def stream_specs(self, stream, n_tiles):
        kind, arrays = stream
        tm = self.tm
        if kind == "merged":
            return [pl.BlockSpec((tm, arrays[0].shape[1]), self.row)]
        if kind == "pair":
            return [pl.BlockSpec((tm, arrays[0].shape[1]), self.prompt_part),
                    pl.BlockSpec((tm, arrays[1].shape[1]), self.latent_part)]
        x1, yg, route, mod, gb = arrays
        return [pl.BlockSpec((tm, x1.shape[1]), self.row),
                pl.BlockSpec((tm, yg.shape[1]), self.row),
                pl.BlockSpec((tm, yg.shape[1]), lambda t: (t + n_tiles, 0)),
                pl.BlockSpec((tm, route.shape[1]), self.row),
                pl.BlockSpec((None,) + mod.shape[1:], self.cond),
                pl.BlockSpec(gb.shape, lambda t: (0, 0))]


_STREAM_REFS = {"merged": 1, "pair": 2, "pending": 6}


def _stream_args(stream):
    kind, arrays = stream
    if kind == "pending":
        x1, yg, route, mod, gb = arrays
        return [x1, yg, yg, route, mod, gb]
    return list(arrays)


def _finish_moe(x1_ref, y0_ref, y1_ref, route_ref, mod_ref, gb_ref, rows=slice(None)):
    r = route_ref[rows, :]
    moe = r[:, 2:3] * _unpack_bf16_pairs(y0_ref[rows, :]) + r[:, 3:4] * _unpack_bf16_pairs(y1_ref[rows, :])
    return _deepnorm(x1_ref[rows, :], mod_ref[5:6, :], moe) * gb_ref[0:1, :] + gb_ref[1:2, :]


def _stream_rows(kind, refs, is_prompt_tile, rows=slice(None)):
    if kind == "merged":
        return refs[0][rows, :]
    if kind == "pair":
        return jnp.where(is_prompt_tile, refs[0][rows, :], refs[1][rows, :])
    return _finish_moe(*refs, rows)


def _ada_kernel(cond_ref, w_ref, b_ref, o_ref):
    a = _silu(cond_ref[...])
    o_ref[...] = _dot(a.astype(BF16), w_ref[...].astype(BF16)) + b_ref[...]


def _ada_layer(cond, w_ada, b_ada, layer):
    depth, d, n = w_ada.shape
    tn = n // 4
    return pl.pallas_call(
        _ada_kernel,
        out_shape=jax.ShapeDtypeStruct((cond.shape[0], n), F32),
        grid=(n // tn,),
        in_specs=[
            pl.BlockSpec(cond.shape, lambda j: (0, 0)),
            pl.BlockSpec((None, d, tn), lambda j: (layer, 0, j)),
            pl.BlockSpec((None, 1, tn), lambda j: (layer, 0, j)),
        ],
        out_specs=pl.BlockSpec((cond.shape[0], tn), lambda j: (0, j)),
        compiler_params=_params(("arbitrary",)),
        name="ada",
    )(cond, w_ada, b_ada.reshape(depth, 1, n))


def _rope(x, cos, sin_signed, first_half):
    outs = []
    for c in range(x.shape[1] // LANES):
        xc = x[:, c * LANES:(c + 1) * LANES]
        partner = jnp.where(first_half, pltpu.roll(xc, LANES - 16, 1), pltpu.roll(xc, 16, 1))
        outs.append(xc * cos + partner * sin_signed)
    return jnp.concatenate(outs, axis=1)


def _store_heads(cache_ref, slot, rows):
    cache_ref[slot] = pltpu.einshape("s(hd)->shd", rows, h=N_HEADS)


def _qkv_kernel(*refs, n_prompt_tiles, stream_kind, n_prev):
    n_x = _STREAM_REFS[stream_kind]
    x_refs, (mod_ref, w_ref, cos_ref, sin_ref) = refs[:n_x], refs[n_x:n_x + 4]
    prev = refs[n_x + 4:n_x + 4 + 2 * max(n_prev, 0)]
    q_ref, k_ref, v_ref, ko_ref, vo_ref = refs[n_x + 4 + 2 * max(n_prev, 0):][:5]
    t = pl.program_id(0)
    x = _stream_rows(stream_kind, x_refs, t < n_prompt_tiles)
    if stream_kind == "pending":
        refs[-1][...] = x
    h = (_layernorm(x) * (1.0 + mod_ref[1:2, :]) + mod_ref[0:1, :]).astype(BF16)
    d = w_ref.shape[0]
    project = lambda j: _dot(h, w_ref[:, j * d:(j + 1) * d])

    @pl.when(t < n_prompt_tiles)
    def _():
        q_ref[...] = (project(0) * Q_SCALE).astype(BF16)
        k = project(1)
        k_ref[...] = k.astype(BF16)
        v = project(2)
        v_ref[...] = v.astype(BF16)
        if n_prev < 0:
            ko_ref[...] = k
            vo_ref[...] = v
        else:
            for a in range(n_prev):
                _store_heads(ko_ref, a, prev[2 * a][...])
                _store_heads(vo_ref, a, prev[2 * a + 1][...])
            _store_heads(ko_ref, n_prev, k)
            _store_heads(vo_ref, n_prev, v)

    @pl.when(t >= n_prompt_tiles)
    def _():
        lane = lax.broadcasted_iota(I32, (h.shape[0], LANES), 1)
        first_half = (lane % 32) < 16
        cos = cos_ref[...]
        sin = sin_ref[...]
        q_ref[...] = _rope(project(0) * Q_SCALE, cos, sin, first_half).astype(BF16)
        k_ref[...] = _rope(project(1), cos, sin, first_half).astype(BF16)
        v_ref[...] = project(2).astype(BF16)


def _qkv(stream, mod, w_qkv, cos, sin, prev_kv, finish_cache, n_prompt, dec_seq, t_tok):
    d = w_qkv.shape[0]
    tl = _Tiles(n_prompt, dec_seq, QKV_TILE)
    tm = tl.tm
    pending = stream[0] == "pending"
    row_out = jax.ShapeDtypeStruct((t_tok, d), F32), pl.BlockSpec((tm, d), tl.row)
    n_prev = len(prev_kv) if finish_cache else -1
    if finish_cache:
        n_slots = n_prev + 1
        kv_shape = jax.ShapeDtypeStruct((n_prompt // tm, n_slots, tm, N_HEADS, V_DIM), F32)
        kv_spec = pl.BlockSpec((None, n_slots, tm, N_HEADS, V_DIM), lambda t: (tl.prompt_part(t)[0], 0, 0, 0, 0))
    else:
        kv_shape = jax.ShapeDtypeStruct((n_prompt, d), F32)
        kv_spec = pl.BlockSpec((tm, d), tl.prompt_part)
    prev_flat = [a for kv in prev_kv for a in kv] if finish_cache else []
    return pl.pallas_call(
        functools.partial(_qkv_kernel, n_prompt_tiles=tl.npt, stream_kind=stream[0], n_prev=n_prev),
        out_shape=(
            jax.ShapeDtypeStruct((t_tok, d), BF16),
            jax.ShapeDtypeStruct((t_tok, d), BF16),
            jax.ShapeDtypeStruct((t_tok, d), BF16),
            kv_shape, kv_shape,
        ) + ((row_out[0],) if pending else ()),
        grid=(t_tok // tm,),
        in_specs=tl.stream_specs(stream, t_tok // tm) + [
            pl.BlockSpec((None, N_MOD, d), tl.cond),
            pl.BlockSpec(w_qkv.shape, lambda t: (0, 0)),
            pl.BlockSpec((tm, LANES), tl.latent_pos),
            pl.BlockSpec((tm, LANES), tl.latent_pos),
        ] + [pl.BlockSpec((tm, d), tl.prompt_part) for _ in prev_flat],
        out_specs=(
            pl.BlockSpec((tm, d), tl.row),
            pl.BlockSpec((tm, d), tl.row),
            pl.BlockSpec((tm, d), tl.row),
            kv_spec, kv_spec,
        ) + ((row_out[1],) if pending else ()),
        compiler_params=_params(("arbitrary",)),
        name="ln_qkv_rope",
    )(*_stream_args(stream), mod, w_qkv, cos, sin, *prev_flat)


def _row_chains(n_rows):
    return [slice(r, r + CHAIN_ROWS) for r in range(0, n_rows, CHAIN_ROWS)]


def _chan_dft_kernel(*refs, n_prompt_tiles, stream_kind):
    n_x = _STREAM_REFS[stream_kind]
    x_refs, (mod_ref, dcs_ref, xc_ref, xs_ref) = refs[:n_x], refs[n_x:n_x + 4]
    g = FOURIER_GROUP_DIM
    is_prompt = pl.program_id(0) < n_prompt_tiles
    for rows in _row_chains(xc_ref.shape[0]):
        x = _stream_rows(stream_kind, x_refs, is_prompt, rows)
        if stream_kind == "pending":
            refs[-1][rows, :] = x
        h = (_layernorm(x) * (1.0 + mod_ref[1:2, :]) + mod_ref[0:1, :]).astype(BF16)
        for i in range(N_FOURIER_GROUPS):
            r = _dot(h[:, i * g:(i + 1) * g], dcs_ref[...])
            xc_ref[rows, i * g:(i + 1) * g] = r[:, :g].astype(BF16)
            xs_ref[rows, i * g:(i + 1) * g] = r[:, g:].astype(BF16)


def _chan_dft(stream, mod, dcs, n_prompt, dec_seq, t_tok):
    d = mod.shape[-1]
    tl = _Tiles(n_prompt, dec_seq, TOKEN_TILE)
    tm = tl.tm
    pending = stream[0] == "pending"
    half = jax.ShapeDtypeStruct((t_tok, d), BF16)
    return pl.pallas_call(
        functools.partial(_chan_dft_kernel, n_prompt_tiles=tl.npt, stream_kind=stream[0]),
        out_shape=(half, half) + ((jax.ShapeDtypeStruct((t_tok, d), F32),) if pending else ()),
        grid=(t_tok // tm,),
        in_specs=tl.stream_specs(stream, t_tok // tm) + [
            pl.BlockSpec((None, N_MOD, d), tl.cond),
            pl.BlockSpec(dcs.shape, lambda t: (0, 0)),
        ],
        out_specs=(pl.BlockSpec((tm, d), tl.row),) * (3 if pending else 2),
        compiler_params=_params(("arbitrary",)),
        name="ln_chan_dft",
    )(*_stream_args(stream), mod, dcs)


def _seq_dft_kernel(cs_ref, ss_ref, xc_ref, xs_ref, o_ref, *, norm):
    f = _dot(cs_ref[...], xc_ref[...]) - _dot(ss_ref[...], xs_ref[...])
    o_ref[...] = (f * norm).astype(BF16)


def _seq_dft(cs, ss, xc, xs, batch, seq, row_offset):
    d = xc.shape[1]
    tm = min(QKV_TILE, seq)
    spt = seq // tm
    off_seq = row_offset // seq
    return pl.pallas_call(
        functools.partial(_seq_dft_kernel, norm=1.0 / math.sqrt(seq * FOURIER_GROUP_DIM)),
        out_shape=jax.ShapeDtypeStruct((batch * seq, d), BF16),
        grid=(batch, spt),
        in_specs=[
            pl.BlockSpec((tm, seq), lambda b, i: (i, 0)),
            pl.BlockSpec((tm, seq), lambda b, i: (i, 0)),
            pl.BlockSpec((seq, d), lambda b, i: (off_seq + b, 0)),
            pl.BlockSpec((seq, d), lambda b, i: (off_seq + b, 0)),
        ],
        out_specs=pl.BlockSpec((tm, d), lambda b, i: (b * spt + i, 0)),
        compiler_params=_params(("arbitrary", "arbitrary")),
        name=f"seq_dft_{seq}",
    )(cs, ss, xc, xs)


def _diff_lambda(lam_ref, lam_init):
    lv = lam_ref[...]
    return (jnp.exp(jnp.sum(lv[0:1] * lv[1:2], axis=-1, keepdims=True))
            - jnp.exp(jnp.sum(lv[2:3] * lv[3:4], axis=-1, keepdims=True)) + lam_init)


def _diff_attn_head(q, k, v_ext, lam, gain, lam_init):
    tq = q.shape[0]
    lane = lax.broadcasted_iota(I32, q.shape, 1)
    zero = jnp.zeros_like(q)
    qq = jnp.concatenate([jnp.where(lane < HEAD_DIM, q, zero), jnp.where(lane >= HEAD_DIM, q, zero)], axis=0)
    parts = []
    for r in range(0, 2 * tq, ATTN_ROW_CHUNK):
        s = lax.dot_general(qq[r:r + ATTN_ROW_CHUNK], k, (((1,), (1,)), ((), ())), preferred_element_type=F32)
        e = jnp.exp2(s - jnp.max(s, axis=-1, keepdims=True)).astype(BF16)
        parts.append(_dot(e, v_ext))
    oe = jnp.concatenate(parts, axis=0)
    o = oe[:, :V_DIM] / oe[:, V_DIM:]
    o = o[:tq] - lam * o[tq:]
    o = o * lax.rsqrt(jnp.mean(o * o, axis=-1, keepdims=True) + LN_EPS)
    return o * gain * (1.0 - lam_init)


def _attn_prompt_kernel(lam_ref, gain_ref, q_ref, k_ref, v_ref, o_ref, *, lam_init):
    lam = _diff_lambda(lam_ref, lam_init)
    gain = gain_ref[...]
    ones = jnp.ones((k_ref.shape[0], V_DIM), BF16)
    for hd in range(N_HEADS):
        cols = slice(hd * V_DIM, (hd + 1) * V_DIM)
        v_ext = jnp.concatenate([v_ref[:, cols], ones], axis=1)
        o_ref[:, cols] = _diff_attn_head(q_ref[:, cols], k_ref[:, cols], v_ext, lam, gain, lam_init).astype(BF16)


def _attn_latent_kernel(lam_ref, gain_ref, q_ref, k_ref, v_ref, kc_ref, vc_ref, o_ref, kall_ref, vext_ref, *,
                        lam_init):
    n_new = k_ref.shape[0]

    @pl.when(pl.program_id(2) == 0)
    def _():
        kall_ref[:n_new, :] = k_ref[...]
        kall_ref[n_new:, :] = kc_ref[...]
        vext_ref[:n_new, :V_DIM] = v_ref[...]
        vext_ref[n_new:, :V_DIM] = vc_ref[...]
        vext_ref[:, V_DIM:] = jnp.ones((vext_ref.shape[0], V_DIM), BF16)

    o = _diff_attn_head(q_ref[...], kall_ref[...], vext_ref[...], _diff_lambda(lam_ref, lam_init), gain_ref[...],
                        lam_init)
    o_ref[...] = o.astype(BF16)


def _attn_prompt(lam_vecs, gain, q, k, v, batch, seq, lam_init):
    d = q.shape[1]
    blk = pl.BlockSpec((seq, d), lambda b: (b, 0))
    return pl.pallas_call(
        functools.partial(_attn_prompt_kernel, lam_init=lam_init),
        out_shape=jax.ShapeDtypeStruct((batch * seq, d), BF16),
        grid=(batch,),
        in_specs=[
            pl.BlockSpec(lam_vecs.shape, lambda b: (0, 0)),
            pl.BlockSpec(gain.shape, lambda b: (0, 0)),
            blk, blk, blk,
        ],
        out_specs=blk,
        compiler_params=_params(("arbitrary",)),
        name="diff_attn_ctx",
    )(lam_vecs, gain, q, k, v)


def _attn_sample(lam_vecs, gain, q, k, v, kc, vc, batch, seq, n_ctx, row_offset, lam_init):
    d = q.shape[1]
    tq = Q_TILE
    qpt = seq // tq
    off_seq = row_offset // seq
    off_tile = row_offset // tq
    qmap = lambda b, h, i: (off_tile + b * qpt + i, h)
    kmap = lambda b, h, i: (off_seq + b, h)
    cmap = lambda b, h, i: (b, h)
    return pl.pallas_call(
        functools.partial(_attn_latent_kernel, lam_init=lam_init),
        out_shape=jax.ShapeDtypeStruct((batch * seq, d), BF16),
        grid=(batch, N_HEADS, qpt),
        in_specs=[
            pl.BlockSpec(lam_vecs.shape, lambda b, h, i: (0, 0)),
            pl.BlockSpec(gain.shape, lambda b, h, i: (0, 0)),
            pl.BlockSpec((tq, V_DIM), qmap),
            pl.BlockSpec((seq, V_DIM), kmap),
            pl.BlockSpec((seq, V_DIM), kmap),
            pl.BlockSpec((n_ctx, V_DIM), cmap),
            pl.BlockSpec((n_ctx, V_DIM), cmap),
        ],
        out_specs=pl.BlockSpec((tq, V_DIM), lambda b, h, i: (b * qpt + i, h)),
        scratch_shapes=[pltpu.VMEM((seq + n_ctx, V_DIM), BF16), pltpu.VMEM((seq + n_ctx, 2 * V_DIM), BF16)],
        compiler_params=_params(("arbitrary", "arbitrary", "arbitrary")),
        name="diff_attn_latent",
    )(lam_vecs, gain, q, k, v, kc, vc)


def _route(lg):
    lane = lax.broadcasted_iota(I32, lg.shape, 1)
    lane_f = lane.astype(F32)
    neg = jnp.float32(-jnp.inf)
    big = jnp.float32(LANES)
    gl = jnp.where(lane < N_EXPERT_GROUPS, lg, neg)
    gmax = jnp.max(gl, axis=-1, keepdims=True)
    g_prob = 1.0 / jnp.sum(jnp.exp(gl - gmax), axis=-1, keepdims=True)
    g_idx = jnp.min(jnp.where(gl == gmax, lane_f, big), axis=-1, keepdims=True)
    lo = N_EXPERT_GROUPS + EXPERTS_PER_GROUP * g_idx
    el = jnp.where((lane_f >= lo) & (lane_f < lo + EXPERTS_PER_GROUP), lg, neg)
    m1 = jnp.max(el, axis=-1, keepdims=True)
    i1 = jnp.min(jnp.where(el == m1, lane_f, big), axis=-1, keepdims=True)
    el2 = jnp.where(lane_f == i1, neg, el)
    m2 = jnp.max(el2, axis=-1, keepdims=True)
    i2 = jnp.min(jnp.where(el2 == m2, lane_f, big), axis=-1, keepdims=True)
    t = jnp.exp(m2 - m1)
    w1 = g_prob / (1.0 + t)
    w2 = g_prob * t / (1.0 + t)
    out = jnp.where(lane == 0, i1 - N_EXPERT_GROUPS, 0.0)
    out = jnp.where(lane == 1, i2 - N_EXPERT_GROUPS, out)
    out = jnp.where(lane == 2, w1, out)
    out = jnp.where(lane == 3, w2, out)
    return out


def _pack_bf16_pairs(h):
    n = h.shape[1] // 2
    bits = lax.bitcast_convert_type(h.astype(BF16).astype(F32), jnp.uint32)
    return (bits[:, :n] >> 16) | bits[:, n:]


def _unpack_bf16_pairs(p):
    lo = lax.bitcast_convert_type(p << 16, F32)
    hi = lax.bitcast_convert_type(p & jnp.uint32(0xFFFF0000), F32)
    return jnp.concatenate([lo, hi], axis=1)


def _mix_out_kernel(*refs, n_prompt_tiles, stream_kind):
    n_x = _STREAM_REFS[stream_kind]
    ap_ref, as_ref, w_ref = refs[:3]
    x_refs = refs[3:3 + n_x]
    mod_ref, gb_ref, wr_ref, x1_ref, hp_ref, route_ref = refs[3 + n_x:]
    is_prompt = pl.program_id(0) < n_prompt_tiles
    for rows in _row_chains(x1_ref.shape[0]):
        a = jnp.where(is_prompt, ap_ref[rows, :], as_ref[rows, :])
        out = _dot(a, w_ref[...])
        x = _stream_rows(stream_kind, x_refs, is_prompt, rows)
        x1 = _deepnorm(x, mod_ref[2:3, :], out) * gb_ref[0:1, :] + gb_ref[1:2, :]
        x1_ref[rows, :] = x1
        h2 = _layernorm(x1) * (1.0 + mod_ref[4:5, :]) + mod_ref[3:4, :]
        hp_ref[rows, :] = _pack_bf16_pairs(h2)
        route_ref[rows, :] = _route(_dot(h2.astype(BF16), wr_ref[...]))


def _mix_out(a_prompt, a_sample, w, stream, mod, gb, wr, n_prompt, dec_seq, t_tok):
    d = w.shape[1]
    tl = _Tiles(n_prompt, dec_seq, TOKEN_TILE)
    tm = tl.tm
    return pl.pallas_call(
        functools.partial(_mix_out_kernel, n_prompt_tiles=tl.npt, stream_kind=stream[0]),
        out_shape=(
            jax.ShapeDtypeStruct((t_tok, d), F32),
            jax.ShapeDtypeStruct((t_tok, d // 2), jnp.uint32),
            jax.ShapeDtypeStruct((t_tok, LANES), F32),
        ),
        grid=(t_tok // tm,),
        in_specs=[
            pl.BlockSpec((tm, d), tl.prompt_part),
            pl.BlockSpec((tm, d), tl.latent_part),
            pl.BlockSpec(w.shape, lambda t: (0, 0)),
        ] + tl.stream_specs(stream, t_tok // tm) + [
            pl.BlockSpec((None, N_MOD, d), tl.cond),
            pl.BlockSpec(gb.shape, lambda t: (0, 0)),
            pl.BlockSpec(wr.shape, lambda t: (0, 0)),
        ],
        out_specs=(
            pl.BlockSpec((tm, d), tl.row),
            pl.BlockSpec((tm, d // 2), tl.row),
            pl.BlockSpec((tm, LANES), tl.row),
        ),
        compiler_params=_params(("arbitrary",)),
        name="mix_out_postnorm_router",
    )(a_prompt, a_sample, w, *_stream_args(stream), mod, gb, wr)


def _lane_prefix_sum(x, lane):
    sh = 1
    while sh < LANES:
        x = x + jnp.where(lane >= sh, pltpu.roll(x, sh, 1), 0.0)
        sh *= 2
    return x


def _lane_suffix_min_exclusive(x, lane):
    big = float(LANES)
    y = jnp.where(lane + 1 < LANES, pltpu.roll(x, LANES - 1, 1), big)
    sh = 1
    while sh < LANES:
        y = jnp.minimum(y, jnp.where(lane + sh < LANES, pltpu.roll(y, LANES - sh, 1), big))
        sh *= 2
    return y


def _work_items(counts, starts, ends, lane8, n_moe_tiles):
    shift = int(math.log2(MOE_TILE))
    first_tile = (starts.astype(I32) >> shift).astype(F32)
    last_tile = ((ends.astype(I32) - 1) >> shift).astype(F32)
    n_it = jnp.where(counts > 0.0, last_tile - first_tile + 1.0, 0.0)
    it_end = _lane_prefix_sum(n_it, lane8)
    it_start = it_end - n_it
    total = it_end[0:1, LANES - 1:LANES]
    sub = lax.broadcasted_iota(I32, (LANES, LANES), 0)
    rows = lambda x: jnp.broadcast_to(x[0:1, :], (LANES, LANES))
    used = counts > 0.0
    weight_slot = ((_lane_prefix_sum(jnp.where(used, 1.0, 0.0), lane8) - 1.0).astype(I32) & 1).astype(F32)
    next_used = _lane_suffix_min_exclusive(jnp.where(used, lane8.astype(F32), float(LANES)), lane8)
    per_expert = (starts, ends, first_tile, it_start, it_end, weight_slot, next_used)
    stacked = jnp.zeros((LANES, LANES), F32)
    for j, vec in enumerate(per_expert):
        stacked = jnp.where(sub == j, rows(vec), stacked)
    cols = stacked.T
    col = lambda j: cols[:, j:j + 1]
    sub_f = sub.astype(F32)
    w = lax.broadcasted_iota(I32, (LANES, LANES), 1).astype(F32)
    ex = jnp.sum(jnp.where((sub < N_EXPERTS) & (col(4) <= w), 1.0, 0.0), axis=0, keepdims=True)
    ex = jnp.minimum(ex, N_EXPERTS - 1.0)
    w1 = w[0:1, :]
    valid = w1 < total
    ex = jnp.where(valid, ex, jnp.max(jnp.where(valid, ex, 0.0), axis=-1, keepdims=True))
    onehot = sub_f == ex
    pick = lambda j: jnp.sum(jnp.where(onehot, col(j), 0.0), axis=0, keepdims=True)
    tile = jnp.where(valid, pick(2) + (w1 - pick(3)), n_moe_tiles - 1.0)
    lo = jnp.where(valid, jnp.maximum(pick(0), tile * MOE_TILE), 0.0)
    hi = jnp.where(valid, jnp.minimum(pick(1), (tile + 1.0) * MOE_TILE), 0.0)
    b8 = lambda x: jnp.broadcast_to(x, (SUBLANES, LANES))
    ex8, tile8 = b8(ex), b8(tile)
    first = jnp.where((lane8 == 0) | (tile8 != pltpu.roll(tile8, 1, 1)), 1.0, 0.0)
    newexp = jnp.where((lane8 == 0) | (ex8 != pltpu.roll(ex8, 1, 1)), 1.0, 0.0)
    sub8 = lax.broadcasted_iota(I32, (SUBLANES, LANES), 0)
    table = jnp.zeros((SUBLANES, LANES), F32)
    for j, vec in enumerate((ex8, tile8, b8(lo), b8(hi), first, newexp, b8(pick(5)), b8(pick(6)))):
        table = jnp.where(sub8 == j, vec, table)
    return table


def _plan_kernel(route_ref, pos_ref, items_ref, tri_ref, carry_ref, tot_ref, *, n_moe_tiles):
    p = pl.program_id(0)
    t = pl.program_id(1)
    tm = route_ref.shape[0]
    r = route_ref[...]
    lane = lax.broadcasted_iota(I32, (tm, LANES), 1)
    lane_f = lane.astype(F32)
    e0 = r[:, 0:1]
    e1 = r[:, 1:2] + N_EXPERTS
    m = jnp.where((lane_f == e0) | (lane_f == e1), 1.0, 0.0)
    colsum = jnp.sum(m, axis=0, keepdims=True)

    @pl.when((p == 0) & (t == 0))
    def _():
        carry_ref[...] = jnp.zeros_like(carry_ref)
        row = lax.broadcasted_iota(I32, (tm, tm), 0)
        col = lax.broadcasted_iota(I32, (tm, tm), 1)
        tri_ref[...] = jnp.where(row > col, 1.0, 0.0).astype(BF16)

    @pl.when(p == 0)
    def _():
        carry_ref[...] += colsum

    @pl.when((p == 0) & (t == pl.num_programs(1) - 1))
    def _():
        tot_ref[...] = carry_ref[...]
        carry_ref[...] = jnp.zeros_like(carry_ref)

    @pl.when(p == 1)
    def _():
        lane8 = lax.broadcasted_iota(I32, (SUBLANES, LANES), 1)
        tot = tot_ref[...]
        is_first = lane8 < N_EXPERTS
        tot0 = jnp.where(is_first, tot, 0.0)
        counts = jnp.where(is_first, tot + pltpu.roll(tot, LANES - N_EXPERTS, 1), 0.0)
        ends = _lane_prefix_sum(counts, lane8)
        starts = ends - counts
        base = jnp.where(is_first, starts, pltpu.roll(starts + tot0, N_EXPERTS, 1))
        before = _dot(tri_ref[...], m.astype(BF16)) + carry_ref[0:1, :]
        carry_ref[...] += colsum
        rows = before + base[0:1, :]
        pos0 = jnp.sum(jnp.where(lane_f == e0, rows, 0.0), axis=-1, keepdims=True)
        pos1 = jnp.sum(jnp.where(lane_f == e1, rows, 0.0), axis=-1, keepdims=True)
        both = jnp.where(lane == 0, pos0, jnp.where(lane == 1, pos1, 0.0))
        pos_ref[...] = both.T[0:SUBLANES, :].astype(I32)

        @pl.when(t == 0)
        def _():
            items_ref[...] = _work_items(counts, starts, ends, lane8, n_moe_tiles).astype(I32)


def _moe_plan(route):
    t_tok = route.shape[0]
    tm = PLAN_TILE
    n_moe_tiles = 2 * t_tok // MOE_TILE
    assert n_moe_tiles + N_EXPERTS - 1 <= LANES and 2 * N_EXPERTS <= LANES and MOE_TILE & (MOE_TILE - 1) == 0
    return pl.pallas_call(
        functools.partial(_plan_kernel, n_moe_tiles=n_moe_tiles),
        out_shape=(jax.ShapeDtypeStruct((SUBLANES, t_tok), I32), jax.ShapeDtypeStruct((SUBLANES, LANES), I32)),
        grid=(2, t_tok // tm),
        in_specs=[pl.BlockSpec((tm, LANES), lambda p, t: (t, 0))],
        out_specs=(pl.BlockSpec((SUBLANES, tm), lambda p, t: (0, t * p)),
                   pl.BlockSpec((SUBLANES, LANES), lambda p, t: (0, 0))),
        scratch_shapes=[pltpu.VMEM((tm, tm), BF16), pltpu.VMEM((SUBLANES, LANES), F32),
                        pltpu.VMEM((SUBLANES, LANES), F32)],
        compiler_params=_params(("arbitrary", "arbitrary")),
        name="moe_positions",
    )(route)


def _sc_workers():
    info = plsc.get_sparse_core_info()
    return info.num_cores, info.num_cores * info.num_subcores


def _sc_pipeline(n_chunks, loads, stores):
    def start(copies):
        for cp in copies:
            cp.start()

    def wait(copies):
        for cp in copies:
            cp.wait()

    start(loads(0, 0))
    for j in range(n_chunks):
        b = j % 2
        if j + 1 < n_chunks:
            if j >= 1:
                wait(stores(j - 1, 1 - b))
            start(loads(j + 1, 1 - b))
        wait(loads(j, b))
        start(stores(j, b))
    if n_chunks >= 2:
        wait(stores(n_chunks - 2, n_chunks % 2))
    wait(stores(n_chunks - 1, (n_chunks - 1) % 2))


def _sc_chunking(n_rows, d):
    n_cores, n_workers = _sc_workers()
    per_worker = n_rows // n_workers
    k = SC_BUFFER_BYTES // (d * 4)
    n_chunks = per_worker // k
    assert n_chunks * k * n_workers == n_rows and k <= SC_MAX_INDEX_CHUNK
    return n_cores, n_workers, per_worker, k, n_chunks


def _sc_gather_rows(table, idx):
    n, d = idx.shape[0], table.shape[1]
    n_cores, n_workers, per_worker, k, n_chunks = _sc_chunking(n, d)

    def body(table_hbm, idx_hbm, out_hbm, idx_v, rows_v, gsem, osem):
        wid = lax.axis_index("s") * n_cores + lax.axis_index("c")
        base = wid * per_worker
        pltpu.sync_copy(idx_hbm.at[wid], idx_v)
        _sc_pipeline(
            n_chunks,
            lambda j, b: [pltpu.make_async_copy(table_hbm.at[idx_v.at[j]], rows_v.at[b], gsem.at[b])],
            lambda j, b: [pltpu.make_async_copy(rows_v.at[b], out_hbm.at[pl.ds(base + j * k, k)], osem.at[b])])

    return pl.kernel(
        body,
        out_type=jax.ShapeDtypeStruct((n, d), table.dtype),
        mesh=plsc.VectorSubcoreMesh(core_axis_name="c", subcore_axis_name="s"),
        scratch_types=[pltpu.VMEM((n_chunks, k), I32), pltpu.VMEM((2, k, d), table.dtype),
                       pltpu.SemaphoreType.DMA((2,)), pltpu.SemaphoreType.DMA((2,))],
        name="sc_row_gather",
    )(table, idx.reshape(n_workers, n_chunks, k))


def _sc_scatter_rows(src, idx):
    n_lists, n_src = idx.shape
    d = src.shape[1]
    n_cores, n_workers, per_worker, k, n_chunks = _sc_chunking(n_src, d)

    def body(src_hbm, idx_hbm, out_hbm, idx_v, rows_v, gsem, osem):
        wid = lax.axis_index("s") * n_cores + lax.axis_index("c")
        base = wid * per_worker
        pltpu.sync_copy(idx_hbm.at[wid], idx_v)
        _sc_pipeline(
            n_chunks,
            lambda j, b: [pltpu.make_async_copy(src_hbm.at[pl.ds(base + j * k, k)], rows_v.at[b], gsem.at[b])],
            lambda j, b: [pltpu.make_async_copy(rows_v.at[b], out_hbm.at[idx_v.at[c, j]], osem.at[b, c])
                          for c in range(n_lists)])

    return pl.kernel(
        body,
        out_type=jax.ShapeDtypeStruct((n_lists * n_src, d), src.dtype),
        mesh=plsc.VectorSubcoreMesh(core_axis_name="c", subcore_axis_name="s"),
        scratch_types=[pltpu.VMEM((n_lists, n_chunks, k), I32), pltpu.VMEM((2, k, d), src.dtype),
                       pltpu.SemaphoreType.DMA((2,)), pltpu.SemaphoreType.DMA((2, n_lists))],
        name="sc_row_scatter",
    )(src, idx.reshape(n_lists, n_workers, n_chunks, k).transpose(1, 0, 2, 3))


IT_EXPERT, IT_TILE, IT_LO, IT_HI, IT_FIRST, IT_NEWEXP, IT_SLOT, IT_NEXT = range(8)


def _moe_kernel(it_ref, xs_ref, wg_hbm, wu_hbm, wd_hbm, o_ref, wg_f, wu_f, wd_f, sem, *, layer):
    w = pl.program_id(0)

    def weight_copies(expert, slot):
        return [pltpu.make_async_copy(src.at[layer, expert], dst.at[slot], sem.at[slot, j])
                for j, (src, dst) in enumerate(((wg_hbm, wg_f), (wu_hbm, wu_f), (wd_hbm, wd_f)))]

    @pl.when(it_ref[IT_NEWEXP, w] == 1)
    def _():
        expert = it_ref[IT_EXPERT, w]
        slot = it_ref[IT_SLOT, w]
        nxt = it_ref[IT_NEXT, w]

        @pl.when(w == 0)
        def _():
            for cp in weight_copies(expert, slot):
                cp.start()

        for cp in weight_copies(expert, slot):
            cp.wait()

        @pl.when(nxt < N_EXPERTS)
        def _():
            for cp in weight_copies(nxt, 1 - slot):
                cp.start()

    slot = it_ref[IT_SLOT, w]
    tm = xs_ref.shape[0]
    x = _unpack_bf16_pairs(xs_ref[...]).astype(BF16)
    a = _dot(x, wg_f[slot].astype(BF16))
    u = _dot(x, wu_f[slot].astype(BF16))
    act = (_silu(a) * u).astype(BF16)
    y = _pack_bf16_pairs(_dot(act, wd_f[slot].astype(BF16)))
    row = it_ref[IT_TILE, w] * tm + lax.broadcasted_iota(I32, (tm, 1), 0)
    mine = (row >= it_ref[IT_LO, w]) & (row < it_ref[IT_HI, w])

    @pl.when(it_ref[IT_FIRST, w] == 1)
    def _():
        o_ref[...] = jnp.where(mine, y, jnp.zeros_like(y))

    @pl.when(it_ref[IT_FIRST, w] == 0)
    def _():
        o_ref[...] = jnp.where(mine, y, o_ref[...])


def _moe_experts(items, xs, w_gate, w_up, w_down, layer):
    n_rows, dp = xs.shape
    _, _, d, f = w_gate.shape
    tm = MOE_TILE
    rmap = lambda w, it: (it[IT_TILE, w], 0)
    hbm = pl.BlockSpec(memory_space=pl.ANY)
    return pl.pallas_call(
        functools.partial(_moe_kernel, layer=layer),
        out_shape=jax.ShapeDtypeStruct((n_rows, d // 2), jnp.uint32),
        grid_spec=pltpu.PrefetchScalarGridSpec(
            num_scalar_prefetch=1,
            grid=(n_rows // tm + N_EXPERTS - 1,),
            in_specs=[pl.BlockSpec((tm, dp), rmap), hbm, hbm, hbm],
            out_specs=pl.BlockSpec((tm, d // 2), rmap),
            scratch_shapes=[
                pltpu.VMEM((2, d, f), F32), pltpu.VMEM((2, d, f), F32), pltpu.VMEM((2, f, d), F32),
                pltpu.SemaphoreType.DMA((2, 3)),
            ],
        ),
        compiler_params=_params(("arbitrary",)),
        name="moe_grouped_mlp",
    )(items, xs, w_gate, w_up, w_down)


def _final_postnorm_kernel(*refs, n_prompt_tiles):
    x2 = _finish_moe(*refs[:6])
    out_prompt, out_latent = refs[6:]

    @pl.when(pl.program_id(0) < n_prompt_tiles)
    def _():
        out_prompt[...] = x2

    @pl.when(pl.program_id(0) >= n_prompt_tiles)
    def _():
        out_latent[...] = x2


def _final_postnorm(stream, n_prompt, dec_seq, t_tok):
    d = stream[1][0].shape[1]
    tl = _Tiles(n_prompt, dec_seq, TOKEN_TILE)
    tm = tl.tm
    return pl.pallas_call(
        functools.partial(_final_postnorm_kernel, n_prompt_tiles=tl.npt),
        out_shape=(jax.ShapeDtypeStruct((n_prompt, d), F32), jax.ShapeDtypeStruct((t_tok - n_prompt, d), F32)),
        grid=(t_tok // tm,),
        in_specs=tl.stream_specs(stream, t_tok // tm),
        out_specs=(pl.BlockSpec((tm, d), tl.prompt_part), pl.BlockSpec((tm, d), tl.latent_part)),
        compiler_params=_params(("arbitrary",)),
        name="moe_combine_postnorm",
    )(*_stream_args(stream))


def _rope_tables(n_lat):
    nf = HEAD_DIM // 4
    s = np.arange(n_lat)
    lane = np.arange(LANES)
    inv = ROPE_BASE ** (-(lane % nf).astype(np.float64) / nf)
    use_col = (lane % HEAD_DIM) >= HEAD_DIM // 2
    p = np.where(use_col[None, :], (s % GRID_W)[:, None], (s // GRID_W)[:, None]).astype(np.float64)
    ang = p * inv[None, :]
    sign = np.where((lane % (2 * nf)) < nf, -1.0, 1.0)
    return jnp.asarray(np.cos(ang), F32), jnp.asarray(np.sin(ang) * sign[None, :], F32)


def _dft_tables(n):
    k = np.arange(n)
    ang = 2.0 * np.pi * ((k[:, None] * k[None, :]) % n).astype(np.float64) / n
    return np.cos(ang), np.sin(ang)


def kernel(x_prompt, x_sample, cache_k, cache_v, c, c_ctx, w_ada, b_ada, ln_gain, ln_bias, w_qkv, w_attn_out,
           lambda_q1, lambda_k1, lambda_q2, lambda_k2, subln_gain, w_fourier_out, w_router_group,
           w_router_expert, w_expert_gate, w_expert_up, w_expert_down):
    bp, sp, d = x_prompt.shape
    bs, n_lat, _ = x_sample.shape
    n_ctx = cache_k.shape[2]
    n_prompt = bp * sp
    t_tok = n_prompt + bs * n_lat
    assert d == D_MODEL and n_prompt % n_lat == 0 and n_lat % TOKEN_TILE == 0 and sp == QKV_TILE

    cond = jnp.concatenate([c_ctx[None, :], c, jnp.zeros((SUBLANES - 1 - bs, d), F32)], axis=0)
    mods = [_ada_layer(cond, w_ada, b_ada, i).reshape(SUBLANES, N_MOD, d) for i in range(DEPTH)]

    cos, sin = _rope_tables(n_lat)
    cc, sc = _dft_tables(FOURIER_GROUP_DIM)
    dcs = jnp.asarray(np.concatenate([cc, sc], axis=1), BF16)
    seq_tabs = {s: tuple(jnp.asarray(m, BF16) for m in _dft_tables(s)) for s in (sp, n_lat)}
    attn_layers = [i for i in range(DEPTH) if i % 2 == 0]

    stream = ("pair", (x_prompt.reshape(n_prompt, d), x_sample.reshape(bs * n_lat, d)))
    prev_kv = []
    new_k = new_v = None
    for i in range(DEPTH):
        mod = mods[i]
        if i % 2 == 0:
            a = i // 2
            lam_init = 0.8 - 0.6 * math.exp(-0.3 * i)
            last_attn = i == attn_layers[-1]
            q, k, v, kf, vf, *finished = _qkv(stream, mod, w_qkv[a].astype(BF16), cos, sin, prev_kv, last_attn,
                                              n_prompt, n_lat, t_tok)
            if last_attn:
                new_k, new_v = kf, vf
            else:
                prev_kv.append((kf, vf))
            lam_vecs = jnp.stack([lambda_q1[a], lambda_k1[a], lambda_q2[a], lambda_k2[a]], axis=0)
            gain = subln_gain[a][None, :]
            kc = cache_k[:, a].reshape(bs * n_ctx, d).astype(BF16)
            vc = cache_v[:, a].reshape(bs * n_ctx, d).astype(BF16)
            mixed_p = _attn_prompt(lam_vecs, gain, q, k, v, bp, sp, lam_init)
            mixed_s = _attn_sample(lam_vecs, gain, q, k, v, kc, vc, bs, n_lat, n_ctx, n_prompt, lam_init)
            w_mix = w_attn_out[a]
        else:
            xc, xsn, *finished = _chan_dft(stream, mod, dcs, n_prompt, n_lat, t_tok)
            mixed_p = _seq_dft(*seq_tabs[sp], xc, xsn, bp, sp, 0)
            mixed_s = _seq_dft(*seq_tabs[n_lat], xc, xsn, bs, n_lat, n_prompt)
            w_mix = w_fourier_out[i // 2]
        if finished:
            stream = ("merged", (finished[0],))
        gb0 = jnp.stack([ln_gain[i, 0], ln_bias[i, 0]], axis=0)
        gb1 = jnp.stack([ln_gain[i, 1], ln_bias[i, 1]], axis=0)
        wr = jnp.concatenate([w_router_group[i], w_router_expert[i],
                              jnp.zeros((d, LANES - N_EXPERT_GROUPS - N_EXPERTS), F32)], axis=1)
        x1, hp, route = _mix_out(mixed_p, mixed_s, w_mix.astype(BF16), stream, mod, gb0, wr.astype(BF16),
                                 n_prompt, n_lat, t_tok)
        pos8, items = _moe_plan(route)
        pos = pos8[0:2]
        xs_sorted = _sc_scatter_rows(hp, pos)
        ys = _moe_experts(items, xs_sorted, w_expert_gate, w_expert_up, w_expert_down, i)
        yg = _sc_gather_rows(ys, pos.reshape(-1))
        stream = ("pending", (x1, yg, route, mods[i], gb1))

    out_prompt, out_latent = _final_postnorm(stream, n_prompt, n_lat, t_tok)
    y_prompt = out_prompt.reshape(bp, sp, d)
    y_sample = out_latent.reshape(bs, n_lat, d)
    return (y_prompt, y_sample, new_k.reshape(bp, len(attn_layers), sp, N_HEADS, 2 * HEAD_DIM),
            new_v.reshape(bp, len(attn_layers), sp, N_HEADS, V_DIM))
```

```python
import functools
import math

import numpy as np
import jax
import jax.numpy as jnp
from jax import lax
from jax.experimental import pallas as pl
from jax.experimental.pallas import tpu as pltpu
from jax.experimental.pallas import tpu_sc as plsc

F32 = jnp.float32
BF16 = jnp.bfloat16
I32 = jnp.int32

D_MODEL = 1024
DEPTH = 4
GRID_W = 64
N_HEADS = 8
HEAD_DIM = 64
V_DIM = 2 * HEAD_DIM
ROPE_BASE = 10000.0
N_FOURIER_GROUPS = 8
FOURIER_GROUP_DIM = D_MODEL // N_FOURIER_GROUPS
N_EXPERT_GROUPS = 4
EXPERTS_PER_GROUP = 8
N_EXPERTS = N_EXPERT_GROUPS * EXPERTS_PER_GROUP
D_EXPERT = 256
N_MOD = 6
LN_EPS = 1e-5
DEEPNORM_ALPHA = (2.0 * DEPTH) ** 0.25
Q_SCALE = math.log2(math.e) * HEAD_DIM ** -0.5

LANES = 128
SUBLANES = 8
TOKEN_TILE = 512
CHAIN_ROWS = 256
QKV_TILE = 256
Q_TILE = 2048
ATTN_ROW_CHUNK = 128
MOE_TILE = 512
PLAN_TILE = 1024
ADA_COLUMNS = 512
SC_BUFFER_BYTES = 128 * 1024
SC_MAX_INDEX_CHUNK = 128
VMEM_LIMIT = 48 * 1024 * 1024


def _params(semantics):
    return pltpu.CompilerParams(dimension_semantics=semantics, vmem_limit_bytes=VMEM_LIMIT)


def _layernorm(x, eps=LN_EPS):
    mu = jnp.mean(x, axis=-1, keepdims=True)
    xc = x - mu
    var = jnp.mean(xc * xc, axis=-1, keepdims=True)
    return xc * lax.rsqrt(var + eps)


def _deepnorm(x, branch_gate, branch):
    return _layernorm(x + (branch_gate * (1.0 / DEEPNORM_ALPHA)) * branch, LN_EPS / DEEPNORM_ALPHA ** 2)


def _silu(a):
    return a / (1.0 + jnp.exp(-a))


def _dot(a, b):
    return jnp.dot(a, b, preferred_element_type=F32)


class _Tiles:
    def __init__(self, n_prompt, dec_seq, tm):
        self.tm = tm
        self.npt = n_prompt // tm
        self.tps = dec_seq // tm

    def row(self, t):
        return (t, 0)

    def cond(self, t):
        return (jnp.where(t < self.npt, 0, (t - self.npt) // self.tps + 1), 0, 0)

    def latent_pos(self, t):
        return (jnp.maximum(t - self.npt, 0) % self.tps, 0)

    def prompt_part(self, t):
        return (jnp.minimum(t, self.npt - 1), 0)

    def latent_part(self, t):
        return (jnp.maximum(t - self.npt, 0), 0)

    def stream_specs(self, stream, n_tiles):
        kind, arrays = stream
        tm = self.tm
        if kind == "merged":
            return [pl.BlockSpec((tm, arrays[0].shape[1]), self.row)]
        if kind == "pair":
            return [pl.BlockSpec((tm, arrays[0].shape[1]), self.prompt_part),
                    pl.BlockSpec((tm, arrays[1].shape[1]), self.latent_part)]
        x1, yg, route, mod, gb = arrays
        return [pl.BlockSpec((tm, x1.shape[1]), self.row),
                pl.BlockSpec((tm, yg.shape[1]), self.row),
                pl.BlockSpec((tm, yg.shape[1]), lambda t: (t + n_tiles, 0)),
                pl.BlockSpec((tm, route.shape[1]), self.row),
                pl.BlockSpec((None,) + mod.shape[1:], self.cond),
                pl.BlockSpec(gb.shape, lambda t: (0, 0))]


_STREAM_REFS = {"merged": 1, "pair": 2, "pending": 6}


def _stream_args(stream):
    kind, arrays = stream
    if kind == "pending":
        x1, yg, route, mod, gb = arrays
        return [x1, yg, yg, route, mod, gb]
    return list(arrays)


def _finish_moe(x1_ref, y0_ref, y1_ref, route_ref, mod_ref, gb_ref, rows=slice(None)):
    r = route_ref[rows, :]
    moe = r[:, 2:3] * _unpack_bf16_pairs(y0_ref[rows, :]) + r[:, 3:4] * _unpack_bf16_pairs(y1_ref[rows, :])
    return _deepnorm(x1_ref[rows, :], mod_ref[5:6, :], moe) * gb_ref[0:1, :] + gb_ref[1:2, :]


def _stream_rows(kind, refs, is_prompt_tile, rows=slice(None)):
    if kind == "merged":
        return refs[0][rows, :]
    if kind == "pair":
        return jnp.where(is_prompt_tile, refs[0][rows, :], refs[1][rows, :])
    return _finish_moe(*refs, rows)


def _ada_kernel(cond_ref, w_ref, b_ref, o_ref):
    a = _silu(cond_ref[...])
    o_ref[...] = _dot(a.astype(BF16), w_ref[...].astype(BF16)) + b_ref[...]


def _ada_layer(cond, w_ada, b_ada, layer):
    depth, d, n = w_ada.shape
    tn = ADA_COLUMNS
    return pl.pallas_call(
        _ada_kernel,
        out_shape=jax.ShapeDtypeStruct((cond.shape[0], n), F32),
        grid=(n // tn,),
        in_specs=[
            pl.BlockSpec(cond.shape, lambda j: (0, 0)),
            pl.BlockSpec((None, d, tn), lambda j: (layer, 0, j)),
            pl.BlockSpec((None, 1, tn), lambda j: (layer, 0, j)),
        ],
        out_specs=pl.BlockSpec((cond.shape[0], tn), lambda j: (0, j)),
        compiler_params=_params(("arbitrary",)),
        name="ada",
    )(cond, w_ada, b_ada.reshape(depth, 1, n))


def _rope(x, cos, sin_signed, first_half):
    outs = []
    for c in range(x.shape[1] // LANES):
        xc = x[:, c * LANES:(c + 1) * LANES]
        partner = jnp.where(first_half, pltpu.roll(xc, LANES - 16, 1), pltpu.roll(xc, 16, 1))
        outs.append(xc * cos + partner * sin_signed)
    return jnp.concatenate(outs, axis=1)


def _store_heads(cache_ref, slot, rows):
    cache_ref[slot] = pltpu.einshape("s(hd)->shd", rows, h=N_HEADS)


def _qkv_kernel(*refs, n_prompt_tiles, stream_kind, n_prev):
    n_x = _STREAM_REFS[stream_kind]
    x_refs, (mod_ref, w_ref, cos_ref, sin_ref) = refs[:n_x], refs[n_x:n_x + 4]
    prev = refs[n_x + 4:n_x + 4 + 2 * max(n_prev, 0)]
    q_ref, k_ref, v_ref, ko_ref, vo_ref = refs[n_x + 4 + 2 * max(n_prev, 0):][:5]
    t = pl.program_id(0)
    d = w_ref.shape[0]

    def projector(is_prompt):
        x = _stream_rows(stream_kind, x_refs, is_prompt)
        if stream_kind == "pending":
            refs[-1][...] = x
        h = (_layernorm(x) * (1.0 + mod_ref[1:2, :]) + mod_ref[0:1, :]).astype(BF16)
        return lambda j: _dot(h, w_ref[:, j * d:(j + 1) * d])

    @pl.when(t < n_prompt_tiles)
    def _():
        project = projector(True)
        q_ref[...] = (project(0) * Q_SCALE).astype(BF16)
        k = project(1)
        k_ref[...] = k.astype(BF16)
        v = project(2)
        v_ref[...] = v.astype(BF16)
        if n_prev < 0:
            ko_ref[...] = k
            vo_ref[...] = v
        else:
            for a in range(n_prev):
                _store_heads(ko_ref, a, prev[2 * a][...])
                _store_heads(vo_ref, a, prev[2 * a + 1][...])
            _store_heads(ko_ref, n_prev, k)
            _store_heads(vo_ref, n_prev, v)

    @pl.when(t >= n_prompt_tiles)
    def _():
        project = projector(False)
        lane = lax.broadcasted_iota(I32, (q_ref.shape[0], LANES), 1)
        first_half = (lane % 32) < 16
        cos = cos_ref[...]
        sin = sin_ref[...]
        q_ref[...] = _rope(project(0) * Q_SCALE, cos, sin, first_half).astype(BF16)
        k_ref[...] = _rope(project(1), cos, sin, first_half).astype(BF16)
        v_ref[...] = project(2).astype(BF16)


def _qkv(stream, mod, w_qkv, cos, sin, prev_kv, finish_cache, n_prompt, dec_seq, t_tok):
    d = w_qkv.shape[0]
    tl = _Tiles(n_prompt, dec_seq, QKV_TILE)
    tm = tl.tm
    pending = stream[0] == "pending"
    row_out = jax.ShapeDtypeStruct((t_tok, d), F32), pl.BlockSpec((tm, d), tl.row)
    n_prev = len(prev_kv) if finish_cache else -1
    if finish_cache:
        n_slots = n_prev + 1
        kv_shape = jax.ShapeDtypeStruct((n_prompt // tm, n_slots, tm, N_HEADS, V_DIM), F32)
        kv_spec = pl.BlockSpec((None, n_slots, tm, N_HEADS, V_DIM), lambda t: (tl.prompt_part(t)[0], 0, 0, 0, 0))
    else:
        kv_shape = jax.ShapeDtypeStruct((n_prompt, d), F32)
        kv_spec = pl.BlockSpec((tm, d), tl.prompt_part)
    prev_flat = [a for kv in prev_kv for a in kv] if finish_cache else []
    return pl.pallas_call(
        functools.partial(_qkv_kernel, n_prompt_tiles=tl.npt, stream_kind=stream[0], n_prev=n_prev),
        out_shape=(
            jax.ShapeDtypeStruct((t_tok, d), BF16),
            jax.ShapeDtypeStruct((t_tok, d), BF16),
            jax.ShapeDtypeStruct((t_tok, d), BF16),
            kv_shape, kv_shape,
        ) + ((row_out[0],) if pending else ()),
        grid=(t_tok // tm,),
        in_specs=tl.stream_specs(stream, t_tok // tm) + [
            pl.BlockSpec((None, N_MOD, d), tl.cond),
            pl.BlockSpec(w_qkv.shape, lambda t: (0, 0)),
            pl.BlockSpec((tm, LANES), tl.latent_pos),
            pl.BlockSpec((tm, LANES), tl.latent_pos),
        ] + [pl.BlockSpec((tm, d), tl.prompt_part) for _ in prev_flat],
        out_specs=(
            pl.BlockSpec((tm, d), tl.row),
            pl.BlockSpec((tm, d), tl.row),
            pl.BlockSpec((tm, d), tl.row),
            kv_spec, kv_spec,
        ) + ((row_out[1],) if pending else ()),
        compiler_params=_params(("arbitrary",)),
        name="ln_qkv_rope",
    )(*_stream_args(stream), mod, w_qkv, cos, sin, *prev_flat)


def _row_chains(n_rows):
    return [slice(r, r + CHAIN_ROWS) for r in range(0, n_rows, CHAIN_ROWS)]


def _chan_dft_kernel(*refs, n_prompt_tiles, stream_kind):
    n_x = _STREAM_REFS[stream_kind]
    x_refs, (mod_ref, dcs_ref, xc_ref, xs_ref) = refs[:n_x], refs[n_x:n_x + 4]
    g = FOURIER_GROUP_DIM
    is_prompt = pl.program_id(0) < n_prompt_tiles
    for rows in _row_chains(xc_ref.shape[0]):
        x = _stream_rows(stream_kind, x_refs, is_prompt, rows)
        if stream_kind == "pending":
            refs[-1][rows, :] = x
        h = (_layernorm(x) * (1.0 + mod_ref[1:2, :]) + mod_ref[0:1, :]).astype(BF16)
        for i in range(N_FOURIER_GROUPS):
            r = _dot(h[:, i * g:(i + 1) * g], dcs_ref[...])
            xc_ref[rows, i * g:(i + 1) * g] = r[:, :g].astype(BF16)
            xs_ref[rows, i * g:(i + 1) * g] = r[:, g:].astype(BF16)


def _chan_dft(stream, mod, dcs, n_prompt, dec_seq, t_tok):
    d = mod.shape[-1]
    tl = _Tiles(n_prompt, dec_seq, TOKEN_TILE)
    tm = tl.tm
    pending = stream[0] == "pending"
    half = jax.ShapeDtypeStruct((t_tok, d), BF16)
    return pl.pallas_call(
        functools.partial(_chan_dft_kernel, n_prompt_tiles=tl.npt, stream_kind=stream[0]),
        out_shape=(half, half) + ((jax.ShapeDtypeStruct((t_tok, d), F32),) if pending else ()),
        grid=(t_tok // tm,),
        in_specs=tl.stream_specs(stream, t_tok // tm) + [
            pl.BlockSpec((None, N_MOD, d), tl.cond),
            pl.BlockSpec(dcs.shape, lambda t: (0, 0)),
        ],
        out_specs=(pl.BlockSpec((tm, d), tl.row),) * (3 if pending else 2),
        compiler_params=_params(("arbitrary",)),
        name="ln_chan_dft",
    )(*_stream_args(stream), mod, dcs)


def _seq_dft_kernel(cs_ref, ss_ref, xc_ref, xs_ref, o_ref, *, norm):
    f = _dot(cs_ref[...], xc_ref[...]) - _dot(ss_ref[...], xs_ref[...])
    o_ref[...] = (f * norm).astype(BF16)


def _seq_dft(cs, ss, xc, xs, batch, seq, row_offset):
    d = xc.shape[1]
    tm = min(QKV_TILE, seq)
    spt = seq // tm
    off_seq = row_offset // seq
    return pl.pallas_call(
        functools.partial(_seq_dft_kernel, norm=1.0 / math.sqrt(seq * FOURIER_GROUP_DIM)),
        out_shape=jax.ShapeDtypeStruct((batch * seq, d), BF16),
        grid=(batch, spt),
        in_specs=[
            pl.BlockSpec((tm, seq), lambda b, i: (i, 0)),
            pl.BlockSpec((tm, seq), lambda b, i: (i, 0)),
            pl.BlockSpec((seq, d), lambda b, i: (off_seq + b, 0)),
            pl.BlockSpec((seq, d), lambda b, i: (off_seq + b, 0)),
        ],
        out_specs=pl.BlockSpec((tm, d), lambda b, i: (b * spt + i, 0)),
        compiler_params=_params(("arbitrary", "arbitrary")),
        name=f"seq_dft_{seq}",
    )(cs, ss, xc, xs)


def _diff_lambda(lam_ref, lam_init):
    lv = lam_ref[...]
    return (jnp.exp(jnp.sum(lv[0:1] * lv[1:2], axis=-1, keepdims=True))
            - jnp.exp(jnp.sum(lv[2:3] * lv[3:4], axis=-1, keepdims=True)) + lam_init)


def _diff_attn_head(q, k, v_ext, lam, gain, lam_init):
    tq = q.shape[0]
    lane = lax.broadcasted_iota(I32, q.shape, 1)
    zero = jnp.zeros_like(q)
    qq = jnp.concatenate([jnp.where(lane < HEAD_DIM, q, zero), jnp.where(lane >= HEAD_DIM, q, zero)], axis=0)
    parts = []
    for r in range(0, 2 * tq, ATTN_ROW_CHUNK):
        s = lax.dot_general(qq[r:r + ATTN_ROW_CHUNK], k, (((1,), (1,)), ((), ())), preferred_element_type=F32)
        e = jnp.exp2(s - jnp.max(s, axis=-1, keepdims=True)).astype(BF16)
        parts.append(_dot(e, v_ext))
    oe = jnp.concatenate(parts, axis=0)
    o = oe[:, :V_DIM] / oe[:, V_DIM:]
    o = o[:tq] - lam * o[tq:]
    o = o * lax.rsqrt(jnp.mean(o * o, axis=-1, keepdims=True) + LN_EPS)
    return o * gain * (1.0 - lam_init)


def _attn_prompt_kernel(lam_ref, gain_ref, q_ref, k_ref, v_ref, o_ref, *, lam_init):
    lam = _diff_lambda(lam_ref, lam_init)
    gain = gain_ref[...]
    ones = jnp.ones((k_ref.shape[0], V_DIM), BF16)
    for hd in range(N_HEADS):
        cols = slice(hd * V_DIM, (hd + 1) * V_DIM)
        v_ext = jnp.concatenate([v_ref[:, cols], ones], axis=1)
        o_ref[:, cols] = _diff_attn_head(q_ref[:, cols], k_ref[:, cols], v_ext, lam, gain, lam_init).astype(BF16)


def _attn_latent_kernel(lam_ref, gain_ref, q_ref, k_ref, v_ref, kc_ref, vc_ref, o_ref, kall_ref, vext_ref, *,
                        lam_init):
    n_new = k_ref.shape[0]

    @pl.when(pl.program_id(2) == 0)
    def _():
        kall_ref[:n_new, :] = k_ref[...]
        kall_ref[n_new:, :] = kc_ref[...]
        vext_ref[:n_new, :V_DIM] = v_ref[...]
        vext_ref[n_new:, :V_DIM] = vc_ref[...]
        vext_ref[:, V_DIM:] = jnp.ones((vext_ref.shape[0], V_DIM), BF16)

    o = _diff_attn_head(q_ref[...], kall_ref[...], vext_ref[...], _diff_lambda(lam_ref, lam_init), gain_ref[...],
                        lam_init)
    o_ref[...] = o.astype(BF16)


def _attn_prompt(lam_vecs, gain, q, k, v, batch, seq, lam_init):
    d = q.shape[1]
    blk = pl.BlockSpec((seq, d), lambda b: (b, 0))
    return pl.pallas_call(
        functools.partial(_attn_prompt_kernel, lam_init=lam_init),
        out_shape=jax.ShapeDtypeStruct((batch * seq, d), BF16),
        grid=(batch,),
        in_specs=[
            pl.BlockSpec(lam_vecs.shape, lambda b: (0, 0)),
            pl.BlockSpec(gain.shape, lambda b: (0, 0)),
            blk, blk, blk,
        ],
        out_specs=blk,
        compiler_params=_params(("arbitrary",)),
        name="diff_attn_ctx",
    )(lam_vecs, gain, q, k, v)


def _attn_sample(lam_vecs, gain, q, k, v, kc, vc, batch, seq, n_ctx, row_offset, lam_init):
    d = q.shape[1]
    tq = Q_TILE
    qpt = seq // tq
    off_seq = row_offset // seq
    off_tile = row_offset // tq
    qmap = lambda b, h, i: (off_tile + b * qpt + i, h)
    kmap = lambda b, h, i: (off_seq + b, h)
    cmap = lambda b, h, i: (b, h)
    return pl.pallas_call(
        functools.partial(_attn_latent_kernel, lam_init=lam_init),
        out_shape=jax.ShapeDtypeStruct((batch * seq, d), BF16),
        grid=(batch, N_HEADS, qpt),
        in_specs=[
            pl.BlockSpec(lam_vecs.shape, lambda b, h, i: (0, 0)),
            pl.BlockSpec(gain.shape, lambda b, h, i: (0, 0)),
            pl.BlockSpec((tq, V_DIM), qmap),
            pl.BlockSpec((seq, V_DIM), kmap),
            pl.BlockSpec((seq, V_DIM), kmap),
            pl.BlockSpec((n_ctx, V_DIM), cmap),
            pl.BlockSpec((n_ctx, V_DIM), cmap),
        ],
        out_specs=pl.BlockSpec((tq, V_DIM), lambda b, h, i: (b * qpt + i, h)),
        scratch_shapes=[pltpu.VMEM((seq + n_ctx, V_DIM), BF16), pltpu.VMEM((seq + n_ctx, 2 * V_DIM), BF16)],
        compiler_params=_params(("arbitrary", "arbitrary", "arbitrary")),
        name="diff_attn_latent",
    )(lam_vecs, gain, q, k, v, kc, vc)


def _route(lg):
    lane = lax.broadcasted_iota(I32, lg.shape, 1)
    lane_f = lane.astype(F32)
    neg = jnp.float32(-jnp.inf)
    big = jnp.float32(LANES)
    gl = jnp.where(lane < N_EXPERT_GROUPS, lg, neg)
    gmax = jnp.max(gl, axis=-1, keepdims=True)
    g_prob = 1.0 / jnp.sum(jnp.exp(gl - gmax), axis=-1, keepdims=True)
    g_idx = jnp.min(jnp.where(gl == gmax, lane_f, big), axis=-1, keepdims=True)
    lo = N_EXPERT_GROUPS + EXPERTS_PER_GROUP * g_idx
    el = jnp.where((lane_f >= lo) & (lane_f < lo + EXPERTS_PER_GROUP), lg, neg)
    m1 = jnp.max(el, axis=-1, keepdims=True)
    i1 = jnp.min(jnp.where(el == m1, lane_f, big), axis=-1, keepdims=True)
    el2 = jnp.where(lane_f == i1, neg, el)
    m2 = jnp.max(el2, axis=-1, keepdims=True)
    i2 = jnp.min(jnp.where(el2 == m2, lane_f, big), axis=-1, keepdims=True)
    t = jnp.exp(m2 - m1)
    w1 = g_prob / (1.0 + t)
    w2 = g_prob * t / (1.0 + t)
    out = jnp.where(lane == 0, i1 - N_EXPERT_GROUPS, 0.0)
    out = jnp.where(lane == 1, i2 - N_EXPERT_GROUPS, out)
    out = jnp.where(lane == 2, w1, out)
    out = jnp.where(lane == 3, w2, out)
    return out


def _pack_bf16_pairs(h):
    n = h.shape[1] // 2
    bits = lax.bitcast_convert_type(h.astype(BF16).astype(F32), jnp.uint32)
    return (bits[:, :n] >> 16) | bits[:, n:]


def _unpack_bf16_pairs(p):
    lo = lax.bitcast_convert_type(p << 16, F32)
    hi = lax.bitcast_convert_type(p & jnp.uint32(0xFFFF0000), F32)
    return jnp.concatenate([lo, hi], axis=1)


def _mix_out_kernel(*refs, n_prompt_tiles, stream_kind):
    n_x = _STREAM_REFS[stream_kind]
    ap_ref, as_ref, w_ref = refs[:3]
    x_refs = refs[3:3 + n_x]
    mod_ref, gb_ref, wr_ref, x1_ref, hp_ref, route_ref = refs[3 + n_x:]
    is_prompt = pl.program_id(0) < n_prompt_tiles
    for rows in _row_chains(x1_ref.shape[0]):
        a = jnp.where(is_prompt, ap_ref[rows, :], as_ref[rows, :])
        out = _dot(a, w_ref[...])
        x = _stream_rows(stream_kind, x_refs, is_prompt, rows)
        x1 = _deepnorm(x, mod_ref[2:3, :], out) * gb_ref[0:1, :] + gb_ref[1:2, :]
        x1_ref[rows, :] = x1
        h2 = _layernorm(x1) * (1.0 + mod_ref[4:5, :]) + mod_ref[3:4, :]
        hp_ref[rows, :] = _pack_bf16_pairs(h2)
        route_ref[rows, :] = _route(_dot(h2.astype(BF16), wr_ref[...]))


def _mix_out(a_prompt, a_sample, w, stream, mod, gb, wr, n_prompt, dec_seq, t_tok):
    d = w.shape[1]
    tl = _Tiles(n_prompt, dec_seq, TOKEN_TILE)
    tm = tl.tm
    return pl.pallas_call(
        functools.partial(_mix_out_kernel, n_prompt_tiles=tl.npt, stream_kind=stream[0]),
        out_shape=(
            jax.ShapeDtypeStruct((t_tok, d), F32),
            jax.ShapeDtypeStruct((t_tok, d // 2), jnp.uint32),
            jax.ShapeDtypeStruct((t_tok, LANES), F32),
        ),
        grid=(t_tok // tm,),
        in_specs=[
            pl.BlockSpec((tm, d), tl.prompt_part),
            pl.BlockSpec((tm, d), tl.latent_part),
            pl.BlockSpec(w.shape, lambda t: (0, 0)),
        ] + tl.stream_specs(stream, t_tok // tm) + [
            pl.BlockSpec((None, N_MOD, d), tl.cond),
            pl.BlockSpec(gb.shape, lambda t: (0, 0)),
            pl.BlockSpec(wr.shape, lambda t: (0, 0)),
        ],
        out_specs=(
            pl.BlockSpec((tm, d), tl.row),
            pl.BlockSpec((tm, d // 2), tl.row),
            pl.BlockSpec((tm, LANES), tl.row),
        ),
        compiler_params=_params(("arbitrary",)),
        name="mix_out_postnorm_router",
    )(a_prompt, a_sample, w, *_stream_args(stream), mod, gb, wr)


def _lane_prefix_sum(x, lane):
    sh = 1
    while sh < LANES:
        x = x + jnp.where(lane >= sh, pltpu.roll(x, sh, 1), 0.0)
        sh *= 2
    return x


def _lane_suffix_min_exclusive(x, lane):
    big = float(LANES)
    y = jnp.where(lane + 1 < LANES, pltpu.roll(x, LANES - 1, 1), big)
    sh = 1
    while sh < LANES:
        y = jnp.minimum(y, jnp.where(lane + sh < LANES, pltpu.roll(y, LANES - sh, 1), big))
        sh *= 2
    return y


def _work_items(counts, starts, ends, lane8, n_moe_tiles):
    shift = int(math.log2(MOE_TILE))
    first_tile = (starts.astype(I32) >> shift).astype(F32)
    last_tile = ((ends.astype(I32) - 1) >> shift).astype(F32)
    n_it = jnp.where(counts > 0.0, last_tile - first_tile + 1.0, 0.0)
    it_end = _lane_prefix_sum(n_it, lane8)
    it_start = it_end - n_it
    total = it_end[0:1, LANES - 1:LANES]
    sub = lax.broadcasted_iota(I32, (LANES, LANES), 0)
    rows = lambda x: jnp.broadcast_to(x[0:1, :], (LANES, LANES))
    used = counts > 0.0
    weight_slot = ((_lane_prefix_sum(jnp.where(used, 1.0, 0.0), lane8) - 1.0).astype(I32) & 1).astype(F32)
    next_used = _lane_suffix_min_exclusive(jnp.where(used, lane8.astype(F32), float(LANES)), lane8)
    per_expert = (starts, ends, first_tile, it_start, it_end, weight_slot, next_used)
    stacked = jnp.zeros((LANES, LANES), F32)
    for j, vec in enumerate(per_expert):
        stacked = jnp.where(sub == j, rows(vec), stacked)
    cols = stacked.T
    col = lambda j: cols[:, j:j + 1]
    sub_f = sub.astype(F32)
    w = lax.broadcasted_iota(I32, (LANES, LANES), 1).astype(F32)
    ex = jnp.sum(jnp.where((sub < N_EXPERTS) & (col(4) <= w), 1.0, 0.0), axis=0, keepdims=True)
    ex = jnp.minimum(ex, N_EXPERTS - 1.0)
    w1 = w[0:1, :]
    valid = w1 < total
    ex = jnp.where(valid, ex, jnp.max(jnp.where(valid, ex, 0.0), axis=-1, keepdims=True))
    onehot = sub_f == ex
    pick = lambda j: jnp.sum(jnp.where(onehot, col(j), 0.0), axis=0, keepdims=True)
    tile = jnp.where(valid, pick(2) + (w1 - pick(3)), n_moe_tiles - 1.0)
    lo = jnp.where(valid, jnp.maximum(pick(0), tile * MOE_TILE), 0.0)
    hi = jnp.where(valid, jnp.minimum(pick(1), (tile + 1.0) * MOE_TILE), 0.0)
    b8 = lambda x: jnp.broadcast_to(x, (SUBLANES, LANES))
    ex8, tile8 = b8(ex), b8(tile)
    first = jnp.where((lane8 == 0) | (tile8 != pltpu.roll(tile8, 1, 1)), 1.0, 0.0)
    newexp = jnp.where((lane8 == 0) | (ex8 != pltpu.roll(ex8, 1, 1)), 1.0, 0.0)
    sub8 = lax.broadcasted_iota(I32, (SUBLANES, LANES), 0)
    table = jnp.zeros((SUBLANES, LANES), F32)
    for j, vec in enumerate((ex8, tile8, b8(lo), b8(hi), first, newexp, b8(pick(5)), b8(pick(6)))):
        table = jnp.where(sub8 == j, vec, table)
    return table


def _plan_kernel(route_ref, pos_ref, items_ref, tri_ref, carry_ref, tot_ref, *, n_moe_tiles):
    p = pl.program_id(0)
    t = pl.program_id(1)
    tm = route_ref.shape[0]
    r = route_ref[...]
    lane = lax.broadcasted_iota(I32, (tm, LANES), 1)
    lane_f = lane.astype(F32)
    e0 = r[:, 0:1]
    e1 = r[:, 1:2] + N_EXPERTS
    m = jnp.where((lane_f == e0) | (lane_f == e1), 1.0, 0.0)
    colsum = jnp.sum(m, axis=0, keepdims=True)

    @pl.when((p == 0) & (t == 0))
    def _():
        carry_ref[...] = jnp.zeros_like(carry_ref)
        row = lax.broadcasted_iota(I32, (tm, tm), 0)
        col = lax.broadcasted_iota(I32, (tm, tm), 1)
        tri_ref[...] = jnp.where(row > col, 1.0, 0.0).astype(BF16)

    @pl.when(p == 0)
    def _():
        carry_ref[...] += colsum

    @pl.when((p == 0) & (t == pl.num_programs(1) - 1))
    def _():
        tot_ref[...] = carry_ref[...]
        carry_ref[...] = jnp.zeros_like(carry_ref)

    @pl.when(p == 1)
    def _():
        lane8 = lax.broadcasted_iota(I32, (SUBLANES, LANES), 1)
        tot = tot_ref[...]
        is_first = lane8 < N_EXPERTS
        tot0 = jnp.where(is_first, tot, 0.0)
        counts = jnp.where(is_first, tot + pltpu.roll(tot, LANES - N_EXPERTS, 1), 0.0)
        ends = _lane_prefix_sum(counts, lane8)
        starts = ends - counts
        base = jnp.where(is_first, starts, pltpu.roll(starts + tot0, N_EXPERTS, 1))
        before = _dot(tri_ref[...], m.astype(BF16)) + carry_ref[0:1, :]
        carry_ref[...] += colsum
        rows = before + base[0:1, :]
        pos0 = jnp.sum(jnp.where(lane_f == e0, rows, 0.0), axis=-1, keepdims=True)
        pos1 = jnp.sum(jnp.where(lane_f == e1, rows, 0.0), axis=-1, keepdims=True)
        both = jnp.where(lane == 0, pos0, jnp.where(lane == 1, pos1, 0.0))
        pos_ref[...] = both.T[0:SUBLANES, :].astype(I32)

        @pl.when(t == 0)
        def _():
            items_ref[...] = _work_items(counts, starts, ends, lane8, n_moe_tiles).astype(I32)


def _moe_plan(route):
    t_tok = route.shape[0]
    tm = PLAN_TILE
    n_moe_tiles = 2 * t_tok // MOE_TILE
    assert n_moe_tiles + N_EXPERTS - 1 <= LANES and 2 * N_EXPERTS <= LANES and MOE_TILE & (MOE_TILE - 1) == 0
    return pl.pallas_call(
        functools.partial(_plan_kernel, n_moe_tiles=n_moe_tiles),
        out_shape=(jax.ShapeDtypeStruct((SUBLANES, t_tok), I32), jax.ShapeDtypeStruct((SUBLANES, LANES), I32)),
        grid=(2, t_tok // tm),
        in_specs=[pl.BlockSpec((tm, LANES), lambda p, t: (t, 0))],
        out_specs=(pl.BlockSpec((SUBLANES, tm), lambda p, t: (0, t * p)),
                   pl.BlockSpec((SUBLANES, LANES), lambda p, t: (0, 0))),
        scratch_shapes=[pltpu.VMEM((tm, tm), BF16), pltpu.VMEM((SUBLANES, LANES), F32),
                        pltpu.VMEM((SUBLANES, LANES), F32)],
        compiler_params=_params(("arbitrary", "arbitrary")),
        name="moe_positions",
    )(route)


def _sc_workers():
    info = plsc.get_sparse_core_info()
    return info.num_cores, info.num_cores * info.num_subcores


def _sc_pipeline(n_chunks, loads, stores):
    def start(copies):
        for cp in copies:
            cp.start()

    def wait(copies):
        for cp in copies:
            cp.wait()

    start(loads(0, 0))
    for j in range(n_chunks):
        b = j % 2
        if j + 1 < n_chunks:
            if j >= 1:
                wait(stores(j - 1, 1 - b))
            start(loads(j + 1, 1 - b))
        wait(loads(j, b))
        start(stores(j, b))
    if n_chunks >= 2:
        wait(stores(n_chunks - 2, n_chunks % 2))
    wait(stores(n_chunks - 1, (n_chunks - 1) % 2))


def _sc_chunking(n_rows, d):
    n_cores, n_workers = _sc_workers()
    per_worker = n_rows // n_workers
    k = SC_BUFFER_BYTES // (d * 4)
    n_chunks = per_worker // k
    assert n_chunks * k * n_workers == n_rows and k <= SC_MAX_INDEX_CHUNK
    return n_cores, n_workers, per_worker, k, n_chunks


def _sc_gather_rows(table, idx):
    n, d = idx.shape[0], table.shape[1]
    n_cores, n_workers, per_worker, k, n_chunks = _sc_chunking(n, d)

    def body(table_hbm, idx_hbm, out_hbm, idx_v, rows_v, gsem, osem):
        wid = lax.axis_index("s") * n_cores + lax.axis_index("c")
        base = wid * per_worker
        pltpu.sync_copy(idx_hbm.at[wid], idx_v)
        _sc_pipeline(
            n_chunks,
            lambda j, b: [pltpu.make_async_copy(table_hbm.at[idx_v.at[j]], rows_v.at[b], gsem.at[b])],
            lambda j, b: [pltpu.make_async_copy(rows_v.at[b], out_hbm.at[pl.ds(base + j * k, k)], osem.at[b])])

    return pl.kernel(
        body,
        out_type=jax.ShapeDtypeStruct((n, d), table.dtype),
        mesh=plsc.VectorSubcoreMesh(core_axis_name="c", subcore_axis_name="s"),
        scratch_types=[pltpu.VMEM((n_chunks, k), I32), pltpu.VMEM((2, k, d), table.dtype),
                       pltpu.SemaphoreType.DMA((2,)), pltpu.SemaphoreType.DMA((2,))],
        name="sc_row_gather",
    )(table, idx.reshape(n_workers, n_chunks, k))


def _sc_scatter_rows(src, idx):
    n_lists, n_src = idx.shape
    d = src.shape[1]
    n_cores, n_workers, per_worker, k, n_chunks = _sc_chunking(n_src, d)

    def body(src_hbm, idx_hbm, out_hbm, idx_v, rows_v, gsem, osem):
        wid = lax.axis_index("s") * n_cores + lax.axis_index("c")
        base = wid * per_worker
        pltpu.sync_copy(idx_hbm.at[wid], idx_v)
        _sc_pipeline(
            n_chunks,
            lambda j, b: [pltpu.make_async_copy(src_hbm.at[pl.ds(base + j * k, k)], rows_v.at[b], gsem.at[b])],
            lambda j, b: [pltpu.make_async_copy(rows_v.at[b], out_hbm.at[idx_v.at[c, j]], osem.at[b, c])
                          for c in range(n_lists)])

    return pl.kernel(
        body,
        out_type=jax.ShapeDtypeStruct((n_lists * n_src, d), src.dtype),
        mesh=plsc.VectorSubcoreMesh(core_axis_name="c", subcore_axis_name="s"),
        scratch_types=[pltpu.VMEM((n_lists, n_chunks, k), I32), pltpu.VMEM((2, k, d), src.dtype),
                       pltpu.SemaphoreType.DMA((2,)), pltpu.SemaphoreType.DMA((2, n_lists))],
        name="sc_row_scatter",
    )(src, idx.reshape(n_lists, n_workers, n_chunks, k).transpose(1, 0, 2, 3))


IT_EXPERT, IT_TILE, IT_LO, IT_HI, IT_FIRST, IT_NEWEXP, IT_SLOT, IT_NEXT = range(8)


def _moe_kernel(it_ref, xs_ref, wg_hbm, wu_hbm, wd_hbm, o_ref, wg_f, wu_f, wd_f, sem, *, layer):
    w = pl.program_id(0)

    def weight_copies(expert, slot):
        return [pltpu.make_async_copy(src.at[layer, expert], dst.at[slot], sem.at[slot, j])
                for j, (src, dst) in enumerate(((wg_hbm, wg_f), (wu_hbm, wu_f), (wd_hbm, wd_f)))]

    @pl.when(it_ref[IT_NEWEXP, w] == 1)
    def _():
        expert = it_ref[IT_EXPERT, w]
        slot = it_ref[IT_SLOT, w]
        nxt = it_ref[IT_NEXT, w]

        @pl.when(w == 0)
        def _():
            for cp in weight_copies(expert, slot):
                cp.start()

        for cp in weight_copies(expert, slot):
            cp.wait()

        @pl.when(nxt < N_EXPERTS)
        def _():
            for cp in weight_copies(nxt, 1 - slot):
                cp.start()

    slot = it_ref[IT_SLOT, w]
    tm = xs_ref.shape[0]
    x = _unpack_bf16_pairs(xs_ref[...]).astype(BF16)
    a = _dot(x, wg_f[slot].astype(BF16))
    u = _dot(x, wu_f[slot].astype(BF16))
    act = (_silu(a) * u).astype(BF16)
    y = _pack_bf16_pairs(_dot(act, wd_f[slot].astype(BF16)))
    row = it_ref[IT_TILE, w] * tm + lax.broadcasted_iota(I32, (tm, 1), 0)
    mine = (row >= it_ref[IT_LO, w]) & (row < it_ref[IT_HI, w])

    @pl.when(it_ref[IT_FIRST, w] == 1)
    def _():
        o_ref[...] = jnp.where(mine, y, jnp.zeros_like(y))

    @pl.when(it_ref[IT_FIRST, w] == 0)
    def _():
        o_ref[...] = jnp.where(mine, y, o_ref[...])


def _moe_experts(items, xs, w_gate, w_up, w_down, layer):
    n_rows, dp = xs.shape
    _, _, d, f = w_gate.shape
    tm = MOE_TILE
    rmap = lambda w, it: (it[IT_TILE, w], 0)
    hbm = pl.BlockSpec(memory_space=pl.ANY)
    return pl.pallas_call(
        functools.partial(_moe_kernel, layer=layer),
        out_shape=jax.ShapeDtypeStruct((n_rows, d // 2), jnp.uint32),
        grid_spec=pltpu.PrefetchScalarGridSpec(
            num_scalar_prefetch=1,
            grid=(n_rows // tm + N_EXPERTS - 1,),
            in_specs=[pl.BlockSpec((tm, dp), rmap), hbm, hbm, hbm],
            out_specs=pl.BlockSpec((tm, d // 2), rmap),
            scratch_shapes=[
                pltpu.VMEM((2, d, f), F32), pltpu.VMEM((2, d, f), F32), pltpu.VMEM((2, f, d), F32),
                pltpu.SemaphoreType.DMA((2, 3)),
            ],
        ),
        compiler_params=_params(("arbitrary",)),
        name="moe_grouped_mlp",
    )(items, xs, w_gate, w_up, w_down)


def _final_postnorm_kernel(*refs, n_prompt_tiles):
    x2 = _finish_moe(*refs[:6])
    out_prompt, out_latent = refs[6:]

    @pl.when(pl.program_id(0) < n_prompt_tiles)
    def _():
        out_prompt[...] = x2

    @pl.when(pl.program_id(0) >= n_prompt_tiles)
    def _():
        out_latent[...] = x2


def _final_postnorm(stream, n_prompt, dec_seq, t_tok):
    d = stream[1][0].shape[1]
    tl = _Tiles(n_prompt, dec_seq, TOKEN_TILE)
    tm = tl.tm
    return pl.pallas_call(
        functools.partial(_final_postnorm_kernel, n_prompt_tiles=tl.npt),
        out_shape=(jax.ShapeDtypeStruct((n_prompt, d), F32), jax.ShapeDtypeStruct((t_tok - n_prompt, d), F32)),
        grid=(t_tok // tm,),
        in_specs=tl.stream_specs(stream, t_tok // tm),
        out_specs=(pl.BlockSpec((tm, d), tl.prompt_part), pl.BlockSpec((tm, d), tl.latent_part)),
        compiler_params=_params(("arbitrary",)),
        name="moe_combine_postnorm",
    )(*_stream_args(stream))


def _rope_tables(n_lat):
    nf = HEAD_DIM // 4
    s = np.arange(n_lat)
    lane = np.arange(LANES)
    inv = ROPE_BASE ** (-(lane % nf).astype(np.float64) / nf)
    use_col = (lane % HEAD_DIM) >= HEAD_DIM // 2
    p = np.where(use_col[None, :], (s % GRID_W)[:, None], (s // GRID_W)[:, None]).astype(np.float64)
    ang = p * inv[None, :]
    sign = np.where((lane % (2 * nf)) < nf, -1.0, 1.0)
    return jnp.asarray(np.cos(ang), F32), jnp.asarray(np.sin(ang) * sign[None, :], F32)


def _dft_tables(n):
    k = np.arange(n)
    ang = 2.0 * np.pi * ((k[:, None] * k[None, :]) % n).astype(np.float64) / n
    return np.cos(ang), np.sin(ang)


def kernel(x_prompt, x_sample, cache_k, cache_v, c, c_ctx, w_ada, b_ada, ln_gain, ln_bias, w_qkv, w_attn_out,
           lambda_q1, lambda_k1, lambda_q2, lambda_k2, subln_gain, w_fourier_out, w_router_group,
           w_router_expert, w_expert_gate, w_expert_up, w_expert_down):
    bp, sp, d = x_prompt.shape
    bs, n_lat, _ = x_sample.shape
    n_ctx = cache_k.shape[2]
    n_prompt = bp * sp
    t_tok = n_prompt + bs * n_lat
    assert d == D_MODEL and n_prompt % n_lat == 0 and n_lat % TOKEN_TILE == 0 and sp == QKV_TILE

    cond = jnp.concatenate([c_ctx[None, :], c, jnp.zeros((SUBLANES - 1 - bs, d), F32)], axis=0)
    mods = [_ada_layer(cond, w_ada, b_ada, i).reshape(SUBLANES, N_MOD, d) for i in range(DEPTH)]

    cos, sin = _rope_tables(n_lat)
    cc, sc = _dft_tables(FOURIER_GROUP_DIM)
    dcs = jnp.asarray(np.concatenate([cc, sc], axis=1), BF16)
    seq_tabs = {s: tuple(jnp.asarray(m, BF16) for m in _dft_tables(s)) for s in (sp, n_lat)}
    attn_layers = [i for i in range(DEPTH) if i % 2 == 0]

    stream = ("pair", (x_prompt.reshape(n_prompt, d), x_sample.reshape(bs * n_lat, d)))
    prev_kv = []
    new_k = new_v = None
    for i in range(DEPTH):
        mod = mods[i]
        if i % 2 == 0:
            a = i // 2
            lam_init = 0.8 - 0.6 * math.exp(-0.3 * i)
            last_attn = i == attn_layers[-1]
            q, k, v, kf, vf, *finished = _qkv(stream, mod, w_qkv[a].astype(BF16), cos, sin, prev_kv, last_attn,
                                              n_prompt, n_lat, t_tok)
            if last_attn:
                new_k, new_v = kf, vf
            else:
                prev_kv.append((kf, vf))
            lam_vecs = jnp.stack([lambda_q1[a], lambda_k1[a], lambda_q2[a], lambda_k2[a]], axis=0)
            gain = subln_gain[a][None, :]
            kc = cache_k[:, a].reshape(bs * n_ctx, d).astype(BF16)
            vc = cache_v[:, a].reshape(bs * n_ctx, d).astype(BF16)
            mixed_p = _attn_prompt(lam_vecs, gain, q, k, v, bp, sp, lam_init)
            mixed_s = _attn_sample(lam_vecs, gain, q, k, v, kc, vc, bs, n_lat, n_ctx, n_prompt, lam_init)
            w_mix = w_attn_out[a]
        else:
            xc, xsn, *finished = _chan_dft(stream, mod, dcs, n_prompt, n_lat, t_tok)
            mixed_p = _seq_dft(*seq_tabs[sp], xc, xsn, bp, sp, 0)
            mixed_s = _seq_dft(*seq_tabs[n_lat], xc, xsn, bs, n_lat, n_prompt)
            w_mix = w_fourier_out[i // 2]
        if finished:
            stream = ("merged", (finished[0],))
        gb0 = jnp.stack([ln_gain[i, 0], ln_bias[i, 0]], axis=0)
        gb1 = jnp.stack([ln_gain[i, 1], ln_bias[i, 1]], axis=0)
        wr = jnp.concatenate([w_router_group[i], w_router_expert[i],
                              jnp.zeros((d, LANES - N_EXPERT_GROUPS - N_EXPERTS), F32)], axis=1)
        x1, hp, route = _mix_out(mixed_p, mixed_s, w_mix.astype(BF16), stream, mod, gb0, wr.astype(BF16),
                                 n_prompt, n_lat, t_tok)
        pos8, items = _moe_plan(route)
        pos = pos8[0:2]
        xs_sorted = _sc_scatter_rows(hp, pos)
        ys = _moe_experts(items, xs_sorted, w_expert_gate, w_expert_up, w_expert_down, i)
        yg = _sc_gather_rows(ys, pos.reshape(-1))
        stream = ("pending", (x1, yg, route, mods[i], gb1))

    out_prompt, out_latent = _final_postnorm(stream, n_prompt, n_lat, t_tok)
    y_prompt = out_prompt.reshape(bp, sp, d)
    y_sample = out_latent.reshape(bs, n_lat, d)
    return (y_prompt, y_sample, new_k.reshape(bp, len(attn_layers), sp, N_HEADS, 2 * HEAD_DIM),
            new_v.reshape(bp, len(attn_layers), sp, N_HEADS, V_DIM))
```

```python
import functools
import math

import numpy as np
import jax
import jax.numpy as jnp
from jax import lax
from jax.experimental import pallas as pl
from jax.experimental.pallas import tpu as pltpu
from jax.experimental.pallas import tpu_sc as plsc

F32 = jnp.float32
BF16 = jnp.bfloat16
I32 = jnp.int32

D_MODEL = 1024
DEPTH = 4
GRID_W = 64
N_HEADS = 8
HEAD_DIM = 64
V_DIM = 2 * HEAD_DIM
ROPE_BASE = 10000.0
N_FOURIER_GROUPS = 8
FOURIER_GROUP_DIM = D_MODEL // N_FOURIER_GROUPS
N_EXPERT_GROUPS = 4
EXPERTS_PER_GROUP = 8
N_EXPERTS = N_EXPERT_GROUPS * EXPERTS_PER_GROUP
D_EXPERT = 256
N_MOD = 6
LN_EPS = 1e-5
DEEPNORM_ALPHA = (2.0 * DEPTH) ** 0.25
Q_SCALE = math.log2(math.e) * HEAD_DIM ** -0.5

LANES = 128
SUBLANES = 8
BF16_SUBLANES = 16
TOKEN_TILE = 512
CHAIN_ROWS = 256
QKV_TILE = 256
Q_TILE = 2048
ATTN_ROW_CHUNK = 128
MOE_TILE = 512
PLAN_TILE = 1024
ADA_COLUMNS = 1536
SC_BUFFER_BYTES = 128 * 1024
SC_MAX_INDEX_CHUNK = 128
VMEM_LIMIT = 48 * 1024 * 1024


def _params(semantics):
    return pltpu.CompilerParams(dimension_semantics=semantics, vmem_limit_bytes=VMEM_LIMIT)


def _layernorm(x, eps=LN_EPS):
    mu = jnp.mean(x, axis=-1, keepdims=True)
    xc = x - mu
    var = jnp.mean(xc * xc, axis=-1, keepdims=True)
    return xc * lax.rsqrt(var + eps)


def _deepnorm(x, branch_gate, branch):
    return _layernorm(x + (branch_gate * (1.0 / DEEPNORM_ALPHA)) * branch, LN_EPS / DEEPNORM_ALPHA ** 2)


def _silu(a):
    return a / (1.0 + jnp.exp(-a))


def _dot(a, b):
    return jnp.dot(a, b, preferred_element_type=F32)


class _Tiles:
    def __init__(self, n_prompt, dec_seq, tm):
        self.tm = tm
        self.npt = n_prompt // tm
        self.tps = dec_seq // tm

    def row(self, t):
        return (t, 0)

    def cond(self, t):
        return (jnp.where(t < self.npt, 0, (t - self.npt) // self.tps + 1), 0, 0)

    def latent_pos(self, t):
        return (jnp.maximum(t - self.npt, 0) % self.tps, 0)

    def prompt_part(self, t):
        return (jnp.minimum(t, self.npt - 1), 0)

    def latent_part(self, t):
        return (jnp.maximum(t - self.npt, 0), 0)

    def stream_specs(self, stream, n_tiles):
        kind, arrays = stream
        tm = self.tm
        if kind == "merged":
            return [pl.BlockSpec((tm, arrays[0].shape[1]), self.row)]
        if kind == "pair":
            return [pl.BlockSpec((tm, arrays[0].shape[1]), self.prompt_part),
                    pl.BlockSpec((tm, arrays[1].shape[1]), self.latent_part)]
        x1, yg, route, mod, gb = arrays
        return [pl.BlockSpec((tm, x1.shape[1]), self.row),
                pl.BlockSpec((tm, yg.shape[1]), self.row),
                pl.BlockSpec((tm, yg.shape[1]), lambda t: (t + n_tiles, 0)),
                pl.BlockSpec((tm, route.shape[1]), self.row),
                pl.BlockSpec((None,) + mod.shape[1:], self.cond),
                pl.BlockSpec(gb.shape, lambda t: (0, 0))]


_STREAM_REFS = {"merged": 1, "pair": 2, "pending": 6}


def _stream_args(stream):
    kind, arrays = stream
    if kind == "pending":
        x1, yg, route, mod, gb = arrays
        return [x1, yg, yg, route, mod, gb]
    return list(arrays)


def _finish_moe(x1_ref, y0_ref, y1_ref, route_ref, mod_ref, gb_ref, rows=slice(None)):
    r = route_ref[rows, :]
    moe = r[:, 2:3] * _unpack_bf16_pairs(y0_ref[rows, :]) + r[:, 3:4] * _unpack_bf16_pairs(y1_ref[rows, :])
    return _deepnorm(x1_ref[rows, :], mod_ref[5:6, :], moe) * gb_ref[0:1, :] + gb_ref[1:2, :]


def _stream_rows(kind, refs, is_prompt_tile, rows=slice(None)):
    if kind == "merged":
        return refs[0][rows, :]
    if kind == "pair":
        return jnp.where(is_prompt_tile, refs[0][rows, :], refs[1][rows, :])
    return _finish_moe(*refs, rows)


def _ada_kernel(cond_ref, w_ref, b_ref, o_ref):
    a = _silu(cond_ref[...])
    o_ref[...] = _dot(a.astype(BF16), w_ref[...].astype(BF16)) + b_ref[...]


def _ada_layer(cond, w_ada, b_ada, layer):
    depth, d, n = w_ada.shape
    tn = ADA_COLUMNS
    return pl.pallas_call(
        _ada_kernel,
        out_shape=jax.ShapeDtypeStruct((cond.shape[0], n), F32),
        grid=(n // tn,),
        in_specs=[
            pl.BlockSpec(cond.shape, lambda j: (0, 0)),
            pl.BlockSpec((None, d, tn), lambda j: (layer, 0, j)),
            pl.BlockSpec((None, 1, tn), lambda j: (layer, 0, j)),
        ],
        out_specs=pl.BlockSpec((cond.shape[0], tn), lambda j: (0, j)),
        compiler_params=_params(("arbitrary",)),
        name="ada",
    )(cond, w_ada, b_ada.reshape(depth, 1, n))


def _rope(x, cos, sin_signed, first_half):
    outs = []
    for c in range(x.shape[1] // LANES):
        xc = x[:, c * LANES:(c + 1) * LANES]
        partner = jnp.where(first_half, pltpu.roll(xc, LANES - 16, 1), pltpu.roll(xc, 16, 1))
        outs.append(xc * cos + partner * sin_signed)
    return jnp.concatenate(outs, axis=1)


def _store_heads(cache_ref, slot, rows):
    cache_ref[slot] = pltpu.einshape("s(hd)->shd", rows, h=N_HEADS)


def _qkv_kernel(*refs, n_prompt_tiles, stream_kind, n_prev):
    n_x = _STREAM_REFS[stream_kind]
    x_refs, (mod_ref, w_ref, cos_ref, sin_ref) = refs[:n_x], refs[n_x:n_x + 4]
    prev = refs[n_x + 4:n_x + 4 + 2 * max(n_prev, 0)]
    q_ref, k_ref, v_ref, ko_ref, vo_ref = refs[n_x + 4 + 2 * max(n_prev, 0):][:5]
    t = pl.program_id(0)
    d = w_ref.shape[0]

    def projector(is_prompt):
        x = _stream_rows(stream_kind, x_refs, is_prompt)
        if stream_kind == "pending":
            refs[-1][...] = x
        h = (_layernorm(x) * (1.0 + mod_ref[1:2, :]) + mod_ref[0:1, :]).astype(BF16)
        return lambda j: _dot(h, w_ref[:, j * d:(j + 1) * d])

    @pl.when(t < n_prompt_tiles)
    def _():
        project = projector(True)
        q_ref[...] = (project(0) * Q_SCALE).astype(BF16)
        k = project(1)
        k_ref[...] = k.astype(BF16)
        v = project(2)
        v_ref[...] = v.astype(BF16)
        if n_prev < 0:
            ko_ref[...] = k
            vo_ref[...] = v
        else:
            for a in range(n_prev):
                _store_heads(ko_ref, a, prev[2 * a][...])
                _store_heads(vo_ref, a, prev[2 * a + 1][...])
            _store_heads(ko_ref, n_prev, k)
            _store_heads(vo_ref, n_prev, v)

    @pl.when(t >= n_prompt_tiles)
    def _():
        project = projector(False)
        lane = lax.broadcasted_iota(I32, (q_ref.shape[0], LANES), 1)
        first_half = (lane % 32) < 16
        cos = cos_ref[...]
        sin = sin_ref[...]
        q_ref[...] = _rope(project(0) * Q_SCALE, cos, sin, first_half).astype(BF16)
        k_ref[...] = _rope(project(1), cos, sin, first_half).astype(BF16)
        v_ref[...] = project(2).astype(BF16)


def _qkv(stream, mod, w_qkv, cos, sin, prev_kv, finish_cache, n_prompt, dec_seq, t_tok):
    d = w_qkv.shape[0]
    tl = _Tiles(n_prompt, dec_seq, QKV_TILE)
    tm = tl.tm
    pending = stream[0] == "pending"
    row_out = jax.ShapeDtypeStruct((t_tok, d), F32), pl.BlockSpec((tm, d), tl.row)
    n_prev = len(prev_kv) if finish_cache else -1
    if finish_cache:
        n_slots = n_prev + 1
        kv_shape = jax.ShapeDtypeStruct((n_prompt // tm, n_slots, tm, N_HEADS, V_DIM), F32)
        kv_spec = pl.BlockSpec((None, n_slots, tm, N_HEADS, V_DIM), lambda t: (tl.prompt_part(t)[0], 0, 0, 0, 0))
    else:
        kv_shape = jax.ShapeDtypeStruct((n_prompt, d), F32)
        kv_spec = pl.BlockSpec((tm, d), tl.prompt_part)
    prev_flat = [a for kv in prev_kv for a in kv] if finish_cache else []
    return pl.pallas_call(
        functools.partial(_qkv_kernel, n_prompt_tiles=tl.npt, stream_kind=stream[0], n_prev=n_prev),
        out_shape=(
            jax.ShapeDtypeStruct((t_tok, d), BF16),
            jax.ShapeDtypeStruct((t_tok, d), BF16),
            jax.ShapeDtypeStruct((t_tok, d), BF16),
            kv_shape, kv_shape,
        ) + ((row_out[0],) if pending else ()),
        grid=(t_tok // tm,),
        in_specs=tl.stream_specs(stream, t_tok // tm) + [
            pl.BlockSpec((None, N_MOD, d), tl.cond),
            pl.BlockSpec(w_qkv.shape, lambda t: (0, 0)),
            pl.BlockSpec((tm, LANES), tl.latent_pos),
            pl.BlockSpec((tm, LANES), tl.latent_pos),
        ] + [pl.BlockSpec((tm, d), tl.prompt_part) for _ in prev_flat],
        out_specs=(
            pl.BlockSpec((tm, d), tl.row),
            pl.BlockSpec((tm, d), tl.row),
            pl.BlockSpec((tm, d), tl.row),
            kv_spec, kv_spec,
        ) + ((row_out[1],) if pending else ()),
        compiler_params=_params(("arbitrary",)),
        name="ln_qkv_rope",
    )(*_stream_args(stream), mod, w_qkv, cos, sin, *prev_flat)


def _row_chains(n_rows):
    return [slice(r, r + CHAIN_ROWS) for r in range(0, n_rows, CHAIN_ROWS)]


def _chan_dft_kernel(*refs, n_prompt_tiles, stream_kind):
    n_x = _STREAM_REFS[stream_kind]
    x_refs, (mod_ref, dcs_ref, xc_ref, xs_ref) = refs[:n_x], refs[n_x:n_x + 4]
    g = FOURIER_GROUP_DIM
    is_prompt = pl.program_id(0) < n_prompt_tiles
    for rows in _row_chains(xc_ref.shape[0]):
        x = _stream_rows(stream_kind, x_refs, is_prompt, rows)
        if stream_kind == "pending":
            refs[-1][rows, :] = x
        h = (_layernorm(x) * (1.0 + mod_ref[1:2, :]) + mod_ref[0:1, :]).astype(BF16)
        for i in range(N_FOURIER_GROUPS):
            r = _dot(h[:, i * g:(i + 1) * g], dcs_ref[...])
            xc_ref[rows, i * g:(i + 1) * g] = r[:, :g].astype(BF16)
            xs_ref[rows, i * g:(i + 1) * g] = r[:, g:].astype(BF16)


def _chan_dft(stream, mod, dcs, n_prompt, dec_seq, t_tok):
    d = mod.shape[-1]
    tl = _Tiles(n_prompt, dec_seq, TOKEN_TILE)
    tm = tl.tm
    pending = stream[0] == "pending"
    half = jax.ShapeDtypeStruct((t_tok, d), BF16)
    return pl.pallas_call(
        functools.partial(_chan_dft_kernel, n_prompt_tiles=tl.npt, stream_kind=stream[0]),
        out_shape=(half, half) + ((jax.ShapeDtypeStruct((t_tok, d), F32),) if pending else ()),
        grid=(t_tok // tm,),
        in_specs=tl.stream_specs(stream, t_tok // tm) + [
            pl.BlockSpec((None, N_MOD, d), tl.cond),
            pl.BlockSpec(dcs.shape, lambda t: (0, 0)),
        ],
        out_specs=(pl.BlockSpec((tm, d), tl.row),) * (3 if pending else 2),
        compiler_params=_params(("arbitrary",)),
        name="ln_chan_dft",
    )(*_stream_args(stream), mod, dcs)


def _seq_dft_kernel(cs_ref, ss_ref, xc_ref, xs_ref, o_ref, *, norm):
    f = _dot(cs_ref[...], xc_ref[...]) - _dot(ss_ref[...], xs_ref[...])
    o_ref[...] = (f * norm).astype(BF16)


def _seq_dft(cs, ss, xc, xs, batch, seq, row_offset):
    d = xc.shape[1]
    tm = min(QKV_TILE, seq)
    spt = seq // tm
    off_seq = row_offset // seq
    return pl.pallas_call(
        functools.partial(_seq_dft_kernel, norm=1.0 / math.sqrt(seq * FOURIER_GROUP_DIM)),
        out_shape=jax.ShapeDtypeStruct((batch * seq, d), BF16),
        grid=(batch, spt),
        in_specs=[
            pl.BlockSpec((tm, seq), lambda b, i: (i, 0)),
            pl.BlockSpec((tm, seq), lambda b, i: (i, 0)),
            pl.BlockSpec((seq, d), lambda b, i: (off_seq + b, 0)),
            pl.BlockSpec((seq, d), lambda b, i: (off_seq + b, 0)),
        ],
        out_specs=pl.BlockSpec((tm, d), lambda b, i: (b * spt + i, 0)),
        compiler_params=_params(("arbitrary", "arbitrary")),
        name=f"seq_dft_{seq}",
    )(cs, ss, xc, xs)


def _seq_dft_mirror_kernel(cs_ref, ss_ref, xc_ref, xs_ref, lo_ref, hi_ref, *, norm):
    tm = lo_ref.shape[0]
    a = _dot(cs_ref[...], xc_ref[...])
    b = _dot(ss_ref[...], xs_ref[...])
    lo_ref[...] = ((a[:tm] - b[:tm]) * norm).astype(BF16)
    both = ((a + b) * norm).astype(BF16)
    r = lax.broadcasted_iota(I32, (tm, cs_ref.shape[0]), 0)
    c = lax.broadcasted_iota(I32, (tm, cs_ref.shape[0]), 1)
    flip = jnp.where(c == tm - r, 1.0, 0.0).astype(BF16)
    hi_ref[...] = _dot(flip, both).astype(BF16)


def _seq_dft_mirror(cs, ss, xc, xs, batch, seq, row_offset):
    d = xc.shape[1]
    tm = QKV_TILE
    half_tiles = seq // tm // 2
    off_seq = row_offset // seq
    half = jax.ShapeDtypeStruct((batch * seq // 2, d), BF16)
    return pl.pallas_call(
        functools.partial(_seq_dft_mirror_kernel, norm=1.0 / math.sqrt(seq * FOURIER_GROUP_DIM)),
        out_shape=(half, half),
        grid=(batch, half_tiles),
        in_specs=[
            pl.BlockSpec((pl.Element(tm + BF16_SUBLANES), pl.Element(seq)), lambda b, i: (i * tm, 0)),
            pl.BlockSpec((pl.Element(tm + BF16_SUBLANES), pl.Element(seq)), lambda b, i: (i * tm, 0)),
            pl.BlockSpec((seq, d), lambda b, i: (off_seq + b, 0)),
            pl.BlockSpec((seq, d), lambda b, i: (off_seq + b, 0)),
        ],
        out_specs=(pl.BlockSpec((tm, d), lambda b, i: (b * half_tiles + i, 0)),
                   pl.BlockSpec((tm, d), lambda b, i: (b * half_tiles + half_tiles - 1 - i, 0))),
        compiler_params=_params(("arbitrary", "arbitrary")),
        name=f"seq_dft_mirror_{seq}",
    )(cs, ss, xc, xs)


def _diff_lambda(lam_ref, lam_init):
    lv = lam_ref[...]
    return (jnp.exp(jnp.sum(lv[0:1] * lv[1:2], axis=-1, keepdims=True))
            - jnp.exp(jnp.sum(lv[2:3] * lv[3:4], axis=-1, keepdims=True)) + lam_init)


def _diff_attn_head(q, k, v_ext, lam, gain, lam_init):
    tq = q.shape[0]
    lane = lax.broadcasted_iota(I32, q.shape, 1)
    zero = jnp.zeros_like(q)
    qq = jnp.concatenate([jnp.where(lane < HEAD_DIM, q, zero), jnp.where(lane >= HEAD_DIM, q, zero)], axis=0)
    parts = []
    for r in range(0, 2 * tq, ATTN_ROW_CHUNK):
        s = lax.dot_general(qq[r:r + ATTN_ROW_CHUNK], k, (((1,), (1,)), ((), ())), preferred_element_type=F32)
        e = jnp.exp2(s - jnp.max(s, axis=-1, keepdims=True)).astype(BF16)
        parts.append(_dot(e, v_ext))
    oe = jnp.concatenate(parts, axis=0)
    o = oe[:, :V_DIM] / oe[:, V_DIM:]
    o = o[:tq] - lam * o[tq:]
    o = o * lax.rsqrt(jnp.mean(o * o, axis=-1, keepdims=True) + LN_EPS)
    return o * gain * (1.0 - lam_init)


def _attn_prompt_kernel(lam_ref, gain_ref, q_ref, k_ref, v_ref, o_ref, *, lam_init):
    lam = _diff_lambda(lam_ref, lam_init)
    gain = gain_ref[...]
    ones = jnp.ones((k_ref.shape[0], V_DIM), BF16)
    for hd in range(N_HEADS):
        cols = slice(hd * V_DIM, (hd + 1) * V_DIM)
        v_ext = jnp.concatenate([v_ref[:, cols], ones], axis=1)
        o_ref[:, cols] = _diff_attn_head(q_ref[:, cols], k_ref[:, cols], v_ext, lam, gain, lam_init).astype(BF16)


def _attn_latent_kernel(lam_ref, gain_ref, q_ref, k_ref, v_ref, kc_ref, vc_ref, o_ref, kall_ref, vext_ref, *,
                        lam_init):
    n_new = k_ref.shape[0]

    @pl.when(pl.program_id(2) == 0)
    def _():
        kall_ref[:n_new, :] = k_ref[...]
        kall_ref[n_new:, :] = kc_ref[...]
        vext_ref[:n_new, :V_DIM] = v_ref[...]
        vext_ref[n_new:, :V_DIM] = vc_ref[...]
        vext_ref[:, V_DIM:] = jnp.ones((vext_ref.shape[0], V_DIM), BF16)

    o = _diff_attn_head(q_ref[...], kall_ref[...], vext_ref[...], _diff_lambda(lam_ref, lam_init), gain_ref[...],
                        lam_init)
    o_ref[...] = o.astype(BF16)


def _attn_prompt(lam_vecs, gain, q, k, v, batch, seq, lam_init):
    d = q.shape[1]
    blk = pl.BlockSpec((seq, d), lambda b: (b, 0))
    return pl.pallas_call(
        functools.partial(_attn_prompt_kernel, lam_init=lam_init),
        out_shape=jax.ShapeDtypeStruct((batch * seq, d), BF16),
        grid=(batch,),
        in_specs=[
            pl.BlockSpec(lam_vecs.shape, lambda b: (0, 0)),
            pl.BlockSpec(gain.shape, lambda b: (0, 0)),
            blk, blk, blk,
        ],
        out_specs=blk,
        compiler_params=_params(("arbitrary",)),
        name="diff_attn_ctx",
    )(lam_vecs, gain, q, k, v)


def _attn_sample(lam_vecs, gain, q, k, v, kc, vc, batch, seq, n_ctx, row_offset, lam_init):
    d = q.shape[1]
    tq = Q_TILE
    qpt = seq // tq
    off_seq = row_offset // seq
    off_tile = row_offset // tq
    qmap = lambda b, h, i: (off_tile + b * qpt + i, h)
    kmap = lambda b, h, i: (off_seq + b, h)
    cmap = lambda b, h, i: (b, h)
    return pl.pallas_call(
        functools.partial(_attn_latent_kernel, lam_init=lam_init),
        out_shape=jax.ShapeDtypeStruct((batch * seq, d), BF16),
        grid=(batch, N_HEADS, qpt),
        in_specs=[
            pl.BlockSpec(lam_vecs.shape, lambda b, h, i: (0, 0)),
            pl.BlockSpec(gain.shape, lambda b, h, i: (0, 0)),
            pl.BlockSpec((tq, V_DIM), qmap),
            pl.BlockSpec((seq, V_DIM), kmap),
            pl.BlockSpec((seq, V_DIM), kmap),
            pl.BlockSpec((n_ctx, V_DIM), cmap),
            pl.BlockSpec((n_ctx, V_DIM), cmap),
        ],
        out_specs=pl.BlockSpec((tq, V_DIM), lambda b, h, i: (b * qpt + i, h)),
        scratch_shapes=[pltpu.VMEM((seq + n_ctx, V_DIM), BF16), pltpu.VMEM((seq + n_ctx, 2 * V_DIM), BF16)],
        compiler_params=_params(("arbitrary", "arbitrary", "arbitrary")),
        name="diff_attn_latent",
    )(lam_vecs, gain, q, k, v, kc, vc)


def _route(lg):
    lane = lax.broadcasted_iota(I32, lg.shape, 1)
    lane_f = lane.astype(F32)
    neg = jnp.float32(-jnp.inf)
    big = jnp.float32(LANES)
    gl = jnp.where(lane < N_EXPERT_GROUPS, lg, neg)
    gmax = jnp.max(gl, axis=-1, keepdims=True)
    g_prob = 1.0 / jnp.sum(jnp.exp(gl - gmax), axis=-1, keepdims=True)
    g_idx = jnp.min(jnp.where(gl == gmax, lane_f, big), axis=-1, keepdims=True)
    lo = N_EXPERT_GROUPS + EXPERTS_PER_GROUP * g_idx
    el = jnp.where((lane_f >= lo) & (lane_f < lo + EXPERTS_PER_GROUP), lg, neg)
    m1 = jnp.max(el, axis=-1, keepdims=True)
    i1 = jnp.min(jnp.where(el == m1, lane_f, big), axis=-1, keepdims=True)
    el2 = jnp.where(lane_f == i1, neg, el)
    m2 = jnp.max(el2, axis=-1, keepdims=True)
    i2 = jnp.min(jnp.where(el2 == m2, lane_f, big), axis=-1, keepdims=True)
    t = jnp.exp(m2 - m1)
    w1 = g_prob / (1.0 + t)
    w2 = g_prob * t / (1.0 + t)
    out = jnp.where(lane == 0, i1 - N_EXPERT_GROUPS, 0.0)
    out = jnp.where(lane == 1, i2 - N_EXPERT_GROUPS, out)
    out = jnp.where(lane == 2, w1, out)
    out = jnp.where(lane == 3, w2, out)
    return out


def _pack_bf16_pairs(h):
    n = h.shape[1] // 2
    bits = lax.bitcast_convert_type(h.astype(BF16).astype(F32), jnp.uint32)
    return (bits[:, :n] >> 16) | bits[:, n:]


def _unpack_bf16_pairs(p):
    lo = lax.bitcast_convert_type(p << 16, F32)
    hi = lax.bitcast_convert_type(p & jnp.uint32(0xFFFF0000), F32)
    return jnp.concatenate([lo, hi], axis=1)


def _mix_out_kernel(*refs, n_prompt_tiles, tiles_per_seq, n_mixed, stream_kind):
    n_x = _STREAM_REFS[stream_kind]
    a_refs, w_ref = refs[:n_mixed], refs[n_mixed]
    x_refs = refs[n_mixed + 1:n_mixed + 1 + n_x]
    mod_ref, gb_ref, wr_ref, x1_ref, hp_ref, route_ref = refs[n_mixed + 1 + n_x:]
    t = pl.program_id(0)
    is_prompt = t < n_prompt_tiles
    first_half = (t - n_prompt_tiles) % tiles_per_seq < tiles_per_seq // 2
    for rows in _row_chains(x1_ref.shape[0]):
        a = a_refs[1][rows, :] if n_mixed == 2 else jnp.where(first_half, a_refs[1][rows, :], a_refs[2][rows, :])
        a = jnp.where(is_prompt, a_refs[0][rows, :], a)
        out = _dot(a, w_ref[...])
        x = _stream_rows(stream_kind, x_refs, is_prompt, rows)
        x1 = _deepnorm(x, mod_ref[2:3, :], out) * gb_ref[0:1, :] + gb_ref[1:2, :]
        x1_ref[rows, :] = x1
        h2 = _layernorm(x1) * (1.0 + mod_ref[4:5, :]) + mod_ref[3:4, :]
        hp_ref[rows, :] = _pack_bf16_pairs(h2)
        route_ref[rows, :] = _route(_dot(h2.astype(BF16), wr_ref[...]))


def _mix_out(mixed, w, stream, mod, gb, wr, n_prompt, dec_seq, t_tok):
    d = w.shape[1]
    tl = _Tiles(n_prompt, dec_seq, TOKEN_TILE)
    tm = tl.tm
    if len(mixed) == 2:
        latent_specs = [pl.BlockSpec((tm, d), tl.latent_part)]
    else:
        half = tl.tps // 2

        def half_map(second):
            def index(t):
                lt = jnp.maximum(t - tl.npt, 0)
                within = lt % tl.tps - second * half
                return (lt // tl.tps * half + jnp.clip(within, 0, half - 1), 0)
            return index
        latent_specs = [pl.BlockSpec((tm, d), half_map(0)), pl.BlockSpec((tm, d), half_map(1))]
    return pl.pallas_call(
        functools.partial(_mix_out_kernel, n_prompt_tiles=tl.npt, tiles_per_seq=tl.tps, n_mixed=len(mixed),
                          stream_kind=stream[0]),
        out_shape=(
            jax.ShapeDtypeStruct((t_tok, d), F32),
            jax.ShapeDtypeStruct((t_tok, d // 2), jnp.uint32),
            jax.ShapeDtypeStruct((t_tok, LANES), F32),
        ),
        grid=(t_tok // tm,),
        in_specs=[pl.BlockSpec((tm, d), tl.prompt_part)] + latent_specs + [
            pl.BlockSpec(w.shape, lambda t: (0, 0)),
        ] + tl.stream_specs(stream, t_tok // tm) + [
            pl.BlockSpec((None, N_MOD, d), tl.cond),
            pl.BlockSpec(gb.shape, lambda t: (0, 0)),
            pl.BlockSpec(wr.shape, lambda t: (0, 0)),
        ],
        out_specs=(
            pl.BlockSpec((tm, d), tl.row),
            pl.BlockSpec((tm, d // 2), tl.row),
            pl.BlockSpec((tm, LANES), tl.row),
        ),
        compiler_params=_params(("arbitrary",)),
        name="mix_out_postnorm_router",
    )(*mixed, w, *_stream_args(stream), mod, gb, wr)


def _lane_prefix_sum(x, lane):
    sh = 1
    while sh < LANES:
        x = x + jnp.where(lane >= sh, pltpu.roll(x, sh, 1), 0.0)
        sh *= 2
    return x


def _lane_suffix_min_exclusive(x, lane):
    big = float(LANES)
    y = jnp.where(lane + 1 < LANES, pltpu.roll(x, LANES - 1, 1), big)
    sh = 1
    while sh < LANES:
        y = jnp.minimum(y, jnp.where(lane + sh < LANES, pltpu.roll(y, LANES - sh, 1), big))
        sh *= 2
    return y


def _work_items(counts, starts, ends, lane8, n_moe_tiles):
    shift = int(math.log2(MOE_TILE))
    first_tile = (starts.astype(I32) >> shift).astype(F32)
    last_tile = ((ends.astype(I32) - 1) >> shift).astype(F32)
    n_it = jnp.where(counts > 0.0, last_tile - first_tile + 1.0, 0.0)
    it_end = _lane_prefix_sum(n_it, lane8)
    it_start = it_end - n_it
    total = it_end[0:1, LANES - 1:LANES]
    sub = lax.broadcasted_iota(I32, (LANES, LANES), 0)
    rows = lambda x: jnp.broadcast_to(x[0:1, :], (LANES, LANES))
    used = counts > 0.0
    weight_slot = ((_lane_prefix_sum(jnp.where(used, 1.0, 0.0), lane8) - 1.0).astype(I32) & 1).astype(F32)
    next_used = _lane_suffix_min_exclusive(jnp.where(used, lane8.astype(F32), float(LANES)), lane8)
    per_expert = (starts, ends, first_tile, it_start, it_end, weight_slot, next_used)
    stacked = jnp.zeros((LANES, LANES), F32)
    for j, vec in enumerate(per_expert):
        stacked = jnp.where(sub == j, rows(vec), stacked)
    cols = stacked.T
    col = lambda j: cols[:, j:j + 1]
    sub_f = sub.astype(F32)
    w = lax.broadcasted_iota(I32, (LANES, LANES), 1).astype(F32)
    ex = jnp.sum(jnp.where((sub < N_EXPERTS) & (col(4) <= w), 1.0, 0.0), axis=0, keepdims=True)
    ex = jnp.minimum(ex, N_EXPERTS - 1.0)
    w1 = w[0:1, :]
    valid = w1 < total
    ex = jnp.where(valid, ex, jnp.max(jnp.where(valid, ex, 0.0), axis=-1, keepdims=True))
    onehot = sub_f == ex
    pick = lambda j: jnp.sum(jnp.where(onehot, col(j), 0.0), axis=0, keepdims=True)
    tile = jnp.where(valid, pick(2) + (w1 - pick(3)), n_moe_tiles - 1.0)
    lo = jnp.where(valid, jnp.maximum(pick(0), tile * MOE_TILE), 0.0)
    hi = jnp.where(valid, jnp.minimum(pick(1), (tile + 1.0) * MOE_TILE), 0.0)
    b8 = lambda x: jnp.broadcast_to(x, (SUBLANES, LANES))
    ex8, tile8 = b8(ex), b8(tile)
    first = jnp.where((lane8 == 0) | (tile8 != pltpu.roll(tile8, 1, 1)), 1.0, 0.0)
    newexp = jnp.where((lane8 == 0) | (ex8 != pltpu.roll(ex8, 1, 1)), 1.0, 0.0)
    sub8 = lax.broadcasted_iota(I32, (SUBLANES, LANES), 0)
    table = jnp.zeros((SUBLANES, LANES), F32)
    for j, vec in enumerate((ex8, tile8, b8(lo), b8(hi), first, newexp, b8(pick(5)), b8(pick(6)))):
        table = jnp.where(sub8 == j, vec, table)
    return table


def _plan_kernel(route_ref, pos_ref, items_ref, tri_ref, carry_ref, tot_ref, *, n_moe_tiles):
    p = pl.program_id(0)
    t = pl.program_id(1)
    tm = route_ref.shape[0]
    r = route_ref[...]
    lane = lax.broadcasted_iota(I32, (tm, LANES), 1)
    lane_f = lane.astype(F32)
    e0 = r[:, 0:1]
    e1 = r[:, 1:2] + N_EXPERTS
    m = jnp.where((lane_f == e0) | (lane_f == e1), 1.0, 0.0)
    colsum = jnp.sum(m, axis=0, keepdims=True)

    @pl.when((p == 0) & (t == 0))
    def _():
        carry_ref[...] = jnp.zeros_like(carry_ref)
        row = lax.broadcasted_iota(I32, (tm, tm), 0)
        col = lax.broadcasted_iota(I32, (tm, tm), 1)
        tri_ref[...] = jnp.where(row > col, 1.0, 0.0).astype(BF16)

    @pl.when(p == 0)
    def _():
        carry_ref[...] += colsum

    @pl.when((p == 0) & (t == pl.num_programs(1) - 1))
    def _():
        tot_ref[...] = carry_ref[...]
        carry_ref[...] = jnp.zeros_like(carry_ref)

    @pl.when(p == 1)
    def _():
        lane8 = lax.broadcasted_iota(I32, (SUBLANES, LANES), 1)
        tot = tot_ref[...]
        is_first = lane8 < N_EXPERTS
        tot0 = jnp.where(is_first, tot, 0.0)
        counts = jnp.where(is_first, tot + pltpu.roll(tot, LANES - N_EXPERTS, 1), 0.0)
        ends = _lane_prefix_sum(counts, lane8)
        starts = ends - counts
        base = jnp.where(is_first, starts, pltpu.roll(starts + tot0, N_EXPERTS, 1))
        before = _dot(tri_ref[...], m.astype(BF16)) + carry_ref[0:1, :]
        carry_ref[...] += colsum
        rows = before + base[0:1, :]
        pos0 = jnp.sum(jnp.where(lane_f == e0, rows, 0.0), axis=-1, keepdims=True)
        pos1 = jnp.sum(jnp.where(lane_f == e1, rows, 0.0), axis=-1, keepdims=True)
        both = jnp.where(lane == 0, pos0, jnp.where(lane == 1, pos1, 0.0))
        pos_ref[...] = both.T[0:SUBLANES, :].astype(I32)

        @pl.when(t == 0)
        def _():
            items_ref[...] = _work_items(counts, starts, ends, lane8, n_moe_tiles).astype(I32)


def _moe_plan(route):
    t_tok = route.shape[0]
    tm = PLAN_TILE
    n_moe_tiles = 2 * t_tok // MOE_TILE
    assert n_moe_tiles + N_EXPERTS - 1 <= LANES and 2 * N_EXPERTS <= LANES and MOE_TILE & (MOE_TILE - 1) == 0
    return pl.pallas_call(
        functools.partial(_plan_kernel, n_moe_tiles=n_moe_tiles),
        out_shape=(jax.ShapeDtypeStruct((SUBLANES, t_tok), I32), jax.ShapeDtypeStruct((SUBLANES, LANES), I32)),
        grid=(2, t_tok // tm),
        in_specs=[pl.BlockSpec((tm, LANES), lambda p, t: (t, 0))],
        out_specs=(pl.BlockSpec((SUBLANES, tm), lambda p, t: (0, t * p)),
                   pl.BlockSpec((SUBLANES, LANES), lambda p, t: (0, 0))),
        scratch_shapes=[pltpu.VMEM((tm, tm), BF16), pltpu.VMEM((SUBLANES, LANES), F32),
                        pltpu.VMEM((SUBLANES, LANES), F32)],
        compiler_params=_params(("arbitrary", "arbitrary")),
        name="moe_positions",
    )(route)


def _sc_workers():
    info = plsc.get_sparse_core_info()
    return info.num_cores, info.num_cores * info.num_subcores


def _sc_pipeline(n_chunks, loads, stores):
    def start(copies):
        for cp in copies:
            cp.start()

    def wait(copies):
        for cp in copies:
            cp.wait()

    start(loads(0, 0))
    for j in range(n_chunks):
        b = j % 2
        if j + 1 < n_chunks:
            if j >= 1:
                wait(stores(j - 1, 1 - b))
            start(loads(j + 1, 1 - b))
        wait(loads(j, b))
        start(stores(j, b))
    if n_chunks >= 2:
        wait(stores(n_chunks - 2, n_chunks % 2))
    wait(stores(n_chunks - 1, (n_chunks - 1) % 2))


def _sc_chunking(n_rows, d):
    n_cores, n_workers = _sc_workers()
    per_worker = n_rows // n_workers
    k = SC_BUFFER_BYTES // (d * 4)
    n_chunks = per_worker // k
    assert n_chunks * k * n_workers == n_rows and k <= SC_MAX_INDEX_CHUNK
    return n_cores, n_workers, per_worker, k, n_chunks


def _sc_gather_rows(table, idx):
    n, d = idx.shape[0], table.shape[1]
    n_cores, n_workers, per_worker, k, n_chunks = _sc_chunking(n, d)

    def body(table_hbm, idx_hbm, out_hbm, idx_v, rows_v, gsem, osem):
        wid = lax.axis_index("s") * n_cores + lax.axis_index("c")
        base = wid * per_worker
        pltpu.sync_copy(idx_hbm.at[wid], idx_v)
        _sc_pipeline(
            n_chunks,
            lambda j, b: [pltpu.make_async_copy(table_hbm.at[idx_v.at[j]], rows_v.at[b], gsem.at[b])],
            lambda j, b: [pltpu.make_async_copy(rows_v.at[b], out_hbm.at[pl.ds(base + j * k, k)], osem.at[b])])

    return pl.kernel(
        body,
        out_type=jax.ShapeDtypeStruct((n, d), table.dtype),
        mesh=plsc.VectorSubcoreMesh(core_axis_name="c", subcore_axis_name="s"),
        scratch_types=[pltpu.VMEM((n_chunks, k), I32), pltpu.VMEM((2, k, d), table.dtype),
                       pltpu.SemaphoreType.DMA((2,)), pltpu.SemaphoreType.DMA((2,))],
        name="sc_row_gather",
    )(table, idx.reshape(n_workers, n_chunks, k))


def _sc_scatter_rows(src, idx):
    n_lists, n_src = idx.shape
    d = src.shape[1]
    n_cores, n_workers, per_worker, k, n_chunks = _sc_chunking(n_src, d)

    def body(src_hbm, idx_hbm, out_hbm, idx_v, rows_v, gsem, osem):
        wid = lax.axis_index("s") * n_cores + lax.axis_index("c")
        base = wid * per_worker
        pltpu.sync_copy(idx_hbm.at[wid], idx_v)
        _sc_pipeline(
            n_chunks,
            lambda j, b: [pltpu.make_async_copy(src_hbm.at[pl.ds(base + j * k, k)], rows_v.at[b], gsem.at[b])],
            lambda j, b: [pltpu.make_async_copy(rows_v.at[b], out_hbm.at[idx_v.at[c, j]], osem.at[b, c])
                          for c in range(n_lists)])

    return pl.kernel(
        body,
        out_type=jax.ShapeDtypeStruct((n_lists * n_src, d), src.dtype),
        mesh=plsc.VectorSubcoreMesh(core_axis_name="c", subcore_axis_name="s"),
        scratch_types=[pltpu.VMEM((n_lists, n_chunks, k), I32), pltpu.VMEM((2, k, d), src.dtype),
                       pltpu.SemaphoreType.DMA((2,)), pltpu.SemaphoreType.DMA((2, n_lists))],
        name="sc_row_scatter",
    )(src, idx.reshape(n_lists, n_workers, n_chunks, k).transpose(1, 0, 2, 3))


IT_EXPERT, IT_TILE, IT_LO, IT_HI, IT_FIRST, IT_NEWEXP, IT_SLOT, IT_NEXT = range(8)


def _moe_kernel(it_ref, xs_ref, wg_hbm, wu_hbm, wd_hbm, o_ref, wg_f, wu_f, wd_f, sem, *, layer):
    w = pl.program_id(0)

    def weight_copies(expert, slot):
        return [pltpu.make_async_copy(src.at[layer, expert], dst.at[slot], sem.at[slot, j])
                for j, (src, dst) in enumerate(((wg_hbm, wg_f), (wu_hbm, wu_f), (wd_hbm, wd_f)))]

    @pl.when(it_ref[IT_NEWEXP, w] == 1)
    def _():
        expert = it_ref[IT_EXPERT, w]
        slot = it_ref[IT_SLOT, w]
        nxt = it_ref[IT_NEXT, w]

        @pl.when(w == 0)
        def _():
            for cp in weight_copies(expert, slot):
                cp.start()

        for cp in weight_copies(expert, slot):
            cp.wait()

        @pl.when(nxt < N_EXPERTS)
        def _():
            for cp in weight_copies(nxt, 1 - slot):
                cp.start()

    slot = it_ref[IT_SLOT, w]
    tm = xs_ref.shape[0]
    x = _unpack_bf16_pairs(xs_ref[...]).astype(BF16)
    a = _dot(x, wg_f[slot].astype(BF16))
    u = _dot(x, wu_f[slot].astype(BF16))
    act = (_silu(a) * u).astype(BF16)
    y = _pack_bf16_pairs(_dot(act, wd_f[slot].astype(BF16)))
    row = it_ref[IT_TILE, w] * tm + lax.broadcasted_iota(I32, (tm, 1), 0)
    mine = (row >= it_ref[IT_LO, w]) & (row < it_ref[IT_HI, w])

    @pl.when(it_ref[IT_FIRST, w] == 1)
    def _():
        o_ref[...] = jnp.where(mine, y, jnp.zeros_like(y))

    @pl.when(it_ref[IT_FIRST, w] == 0)
    def _():
        o_ref[...] = jnp.where(mine, y, o_ref[...])


def _moe_experts(items, xs, w_gate, w_up, w_down, layer):
    n_rows, dp = xs.shape
    _, _, d, f = w_gate.shape
    tm = MOE_TILE
    rmap = lambda w, it: (it[IT_TILE, w], 0)
    hbm = pl.BlockSpec(memory_space=pl.ANY)
    return pl.pallas_call(
        functools.partial(_moe_kernel, layer=layer),
        out_shape=jax.ShapeDtypeStruct((n_rows, d // 2), jnp.uint32),
        grid_spec=pltpu.PrefetchScalarGridSpec(
            num_scalar_prefetch=1,
            grid=(n_rows // tm + N_EXPERTS - 1,),
            in_specs=[pl.BlockSpec((tm, dp), rmap), hbm, hbm, hbm],
            out_specs=pl.BlockSpec((tm, d // 2), rmap),
            scratch_shapes=[
                pltpu.VMEM((2, d, f), F32), pltpu.VMEM((2, d, f), F32), pltpu.VMEM((2, f, d), F32),
                pltpu.SemaphoreType.DMA((2, 3)),
            ],
        ),
        compiler_params=_params(("arbitrary",)),
        name="moe_grouped_mlp",
    )(items, xs, w_gate, w_up, w_down)


def _final_postnorm_kernel(*refs, n_prompt_tiles):
    x2 = _finish_moe(*refs[:6])
    out_prompt, out_latent = refs[6:]

    @pl.when(pl.program_id(0) < n_prompt_tiles)
    def _():
        out_prompt[...] = x2

    @pl.when(pl.program_id(0) >= n_prompt_tiles)
    def _():
        out_latent[...] = x2


def _final_postnorm(stream, n_prompt, dec_seq, t_tok):
    d = stream[1][0].shape[1]
    tl = _Tiles(n_prompt, dec_seq, TOKEN_TILE)
    tm = tl.tm
    return pl.pallas_call(
        functools.partial(_final_postnorm_kernel, n_prompt_tiles=tl.npt),
        out_shape=(jax.ShapeDtypeStruct((n_prompt, d), F32), jax.ShapeDtypeStruct((t_tok - n_prompt, d), F32)),
        grid=(t_tok // tm,),
        in_specs=tl.stream_specs(stream, t_tok // tm),
        out_specs=(pl.BlockSpec((tm, d), tl.prompt_part), pl.BlockSpec((tm, d), tl.latent_part)),
        compiler_params=_params(("arbitrary",)),
        name="moe_combine_postnorm",
    )(*_stream_args(stream))


def _rope_tables(n_lat):
    nf = HEAD_DIM // 4
    s = np.arange(n_lat)
    lane = np.arange(LANES)
    inv = ROPE_BASE ** (-(lane % nf).astype(np.float64) / nf)
    use_col = (lane % HEAD_DIM) >= HEAD_DIM // 2
    p = np.where(use_col[None, :], (s % GRID_W)[:, None], (s // GRID_W)[:, None]).astype(np.float64)
    ang = p * inv[None, :]
    sign = np.where((lane % (2 * nf)) < nf, -1.0, 1.0)
    return jnp.asarray(np.cos(ang), F32), jnp.asarray(np.sin(ang) * sign[None, :], F32)


def _dft_tables(n):
    k = np.arange(n)
    ang = 2.0 * np.pi * ((k[:, None] * k[None, :]) % n).astype(np.float64) / n
    return np.cos(ang), np.sin(ang)


def kernel(x_prompt, x_sample, cache_k, cache_v, c, c_ctx, w_ada, b_ada, ln_gain, ln_bias, w_qkv, w_attn_out,
           lambda_q1, lambda_k1, lambda_q2, lambda_k2, subln_gain, w_fourier_out, w_router_group,
           w_router_expert, w_expert_gate, w_expert_up, w_expert_down):
    bp, sp, d = x_prompt.shape
    bs, n_lat, _ = x_sample.shape
    n_ctx = cache_k.shape[2]
    n_prompt = bp * sp
    t_tok = n_prompt + bs * n_lat
    assert d == D_MODEL and n_prompt % n_lat == 0 and n_lat % TOKEN_TILE == 0 and sp == QKV_TILE

    cond = jnp.concatenate([c_ctx[None, :], c, jnp.zeros((SUBLANES - 1 - bs, d), F32)], axis=0)
    mods = [_ada_layer(cond, w_ada, b_ada, i).reshape(SUBLANES, N_MOD, d) for i in range(DEPTH)]

    cos, sin = _rope_tables(n_lat)
    cc, sc = _dft_tables(FOURIER_GROUP_DIM)
    dcs = jnp.asarray(np.concatenate([cc, sc], axis=1), BF16)
    seq_tabs = {s: tuple(jnp.asarray(m, BF16) for m in _dft_tables(s)) for s in (sp, n_lat)}
    attn_layers = [i for i in range(DEPTH) if i % 2 == 0]

    stream = ("pair", (x_prompt.reshape(n_prompt, d), x_sample.reshape(bs * n_lat, d)))
    prev_kv = []
    new_k = new_v = None
    for i in range(DEPTH):
        mod = mods[i]
        if i % 2 == 0:
            a = i // 2
            lam_init = 0.8 - 0.6 * math.exp(-0.3 * i)
            last_attn = i == attn_layers[-1]
            q, k, v, kf, vf, *finished = _qkv(stream, mod, w_qkv[a].astype(BF16), cos, sin, prev_kv, last_attn,
                                              n_prompt, n_lat, t_tok)
            if last_attn:
                new_k, new_v = kf, vf
            else:
                prev_kv.append((kf, vf))
            lam_vecs = jnp.stack([lambda_q1[a], lambda_k1[a], lambda_q2[a], lambda_k2[a]], axis=0)
            gain = subln_gain[a][None, :]
            kc = cache_k[:, a].reshape(bs * n_ctx, d).astype(BF16)
            vc = cache_v[:, a].reshape(bs * n_ctx, d).astype(BF16)
            mixed = (_attn_prompt(lam_vecs, gain, q, k, v, bp, sp, lam_init),
                     _attn_sample(lam_vecs, gain, q, k, v, kc, vc, bs, n_lat, n_ctx, n_prompt, lam_init))
            w_mix = w_attn_out[a]
        else:
            xc, xsn, *finished = _chan_dft(stream, mod, dcs, n_prompt, n_lat, t_tok)
            mixed = (_seq_dft(*seq_tabs[sp], xc, xsn, bp, sp, 0),
                     *_seq_dft_mirror(*seq_tabs[n_lat], xc, xsn, bs, n_lat, n_prompt))
            w_mix = w_fourier_out[i // 2]
        if finished:
            stream = ("merged", (finished[0],))
        gb0 = jnp.stack([ln_gain[i, 0], ln_bias[i, 0]], axis=0)
        gb1 = jnp.stack([ln_gain[i, 1], ln_bias[i, 1]], axis=0)
        wr = jnp.concatenate([w_router_group[i], w_router_expert[i],
                              jnp.zeros((d, LANES - N_EXPERT_GROUPS - N_EXPERTS), F32)], axis=1)
        x1, hp, route = _mix_out(mixed, w_mix.astype(BF16), stream, mod, gb0, wr.astype(BF16), n_prompt, n_lat,
                                 t_tok)
        pos8, items = _moe_plan(route)
        pos = pos8[0:2]
        xs_sorted = _sc_scatter_rows(hp, pos)
        ys = _moe_experts(items, xs_sorted, w_expert_gate, w_expert_up, w_expert_down, i)
        yg = _sc_gather_rows(ys, pos.reshape(-1))
        stream = ("pending", (x1, yg, route, mods[i], gb1))

    out_prompt, out_latent = _final_postnorm(stream, n_prompt, n_lat, t_tok)
    y_prompt = out_prompt.reshape(bp, sp, d)
    y_sample = out_latent.reshape(bs, n_lat, d)
    return (y_prompt, y_sample, new_k.reshape(bp, len(attn_layers), sp, N_HEADS, 2 * HEAD_DIM),
            new_v.reshape(bp, len(attn_layers), sp, N_HEADS, V_DIM))
```

```python
import functools
import math

import numpy as np
import jax
import jax.numpy as jnp
from jax import lax
from jax.experimental import pallas as pl
from jax.experimental.pallas import tpu as pltpu
from jax.experimental.pallas import tpu_sc as plsc

F32 = jnp.float32
BF16 = jnp.bfloat16
I32 = jnp.int32

D_MODEL = 1024
DEPTH = 4
GRID_W = 64
N_HEADS = 8
HEAD_DIM = 64
V_DIM = 2 * HEAD_DIM
ROPE_BASE = 10000.0
ROPE_FREQS = HEAD_DIM // 4
N_FOURIER_GROUPS = 8
FOURIER_GROUP_DIM = D_MODEL // N_FOURIER_GROUPS
N_EXPERT_GROUPS = 4
EXPERTS_PER_GROUP = 8
N_EXPERTS = N_EXPERT_GROUPS * EXPERTS_PER_GROUP
N_MOD = 6
LN_EPS = 1e-5
DEEPNORM_ALPHA = (2.0 * DEPTH) ** 0.25
Q_SCALE = math.log2(math.e) * HEAD_DIM ** -0.5

LANES = 128
SUBLANES = 8
BF16_SUBLANES = 16
TOKEN_TILE = 512
CHAIN_ROWS = 256
QKV_TILE = 256
Q_TILE = 2048
ATTN_ROW_CHUNK = 128
MOE_TILE = 512
PLAN_TILE = 1024
ADA_COLUMNS = 1536
SC_BUFFER_BYTES = 128 * 1024
SC_MAX_INDEX_CHUNK = 128
VMEM_LIMIT = 48 * 1024 * 1024


def _params(semantics):
    return pltpu.CompilerParams(dimension_semantics=semantics, vmem_limit_bytes=VMEM_LIMIT)


def _layernorm(x, eps=LN_EPS):
    mu = jnp.mean(x, axis=-1, keepdims=True)
    xc = x - mu
    var = jnp.mean(xc * xc, axis=-1, keepdims=True)
    return xc * lax.rsqrt(var + eps)


def _deepnorm(x, branch_gate, branch):
    return _layernorm(x + (branch_gate * (1.0 / DEEPNORM_ALPHA)) * branch, LN_EPS / DEEPNORM_ALPHA ** 2)


def _silu(a):
    return a / (1.0 + jnp.exp(-a))


def _dot(a, b):
    return jnp.dot(a, b, preferred_element_type=F32)


class _Tiles:
    def __init__(self, n_prompt, dec_seq, tm):
        self.tm = tm
        self.npt = n_prompt // tm
        self.tps = dec_seq // tm

    def row(self, t):
        return (t, 0)

    def cond(self, t):
        return (jnp.where(t < self.npt, 0, (t - self.npt) // self.tps + 1), 0, 0)

    def latent_pos(self, t):
        return (jnp.maximum(t - self.npt, 0) % self.tps, 0)

    def prompt_part(self, t):
        return (jnp.minimum(t, self.npt - 1), 0)

    def latent_part(self, t):
        return (jnp.maximum(t - self.npt, 0), 0)

    def stream_specs(self, stream, n_tiles):
        kind, arrays = stream
        tm = self.tm
        if kind == "merged":
            return [pl.BlockSpec((tm, arrays[0].shape[1]), self.row)]
        if kind == "pair":
            return [pl.BlockSpec((tm, arrays[0].shape[1]), self.prompt_part),
                    pl.BlockSpec((tm, arrays[1].shape[1]), self.latent_part)]
        x1, yg, route, mod, gb = arrays
        return [pl.BlockSpec((tm, x1.shape[1]), self.row),
                pl.BlockSpec((tm, yg.shape[1]), self.row),
                pl.BlockSpec((tm, yg.shape[1]), lambda t: (t + n_tiles, 0)),
                pl.BlockSpec((tm, route.shape[1]), self.row),
                pl.BlockSpec((None,) + mod.shape[1:], self.cond),
                pl.BlockSpec(gb.shape, lambda t: (0, 0))]


_STREAM_REFS = {"merged": 1, "pair": 2, "pending": 6}


def _stream_args(stream):
    kind, arrays = stream
    if kind == "pending":
        x1, yg, route, mod, gb = arrays
        return [x1, yg, yg, route, mod, gb]
    return list(arrays)


def _finish_moe(x1_ref, y0_ref, y1_ref, route_ref, mod_ref, gb_ref, rows=slice(None)):
    r = route_ref[rows, :]
    moe = r[:, 2:3] * _unpack_bf16_pairs(y0_ref[rows, :]) + r[:, 3:4] * _unpack_bf16_pairs(y1_ref[rows, :])
    return _deepnorm(x1_ref[rows, :], mod_ref[5:6, :], moe) * gb_ref[0:1, :] + gb_ref[1:2, :]


def _stream_rows(kind, refs, is_prompt_tile, rows=slice(None)):
    if kind == "merged":
        return refs[0][rows, :]
    if kind == "pair":
        return jnp.where(is_prompt_tile, refs[0][rows, :], refs[1][rows, :])
    return _finish_moe(*refs, rows)


def _ada_kernel(cond_ref, w_ref, b_ref, o_ref):
    a = _silu(cond_ref[...])
    o_ref[...] = _dot(a.astype(BF16), w_ref[...].astype(BF16)) + b_ref[...]


def _ada_layer(cond, w_ada, b_ada, layer):
    depth, d, n = w_ada.shape
    tn = ADA_COLUMNS
    return pl.pallas_call(
        _ada_kernel,
        out_shape=jax.ShapeDtypeStruct((cond.shape[0], n), F32),
        grid=(n // tn,),
        in_specs=[
            pl.BlockSpec(cond.shape, lambda j: (0, 0)),
            pl.BlockSpec((None, d, tn), lambda j: (layer, 0, j)),
            pl.BlockSpec((None, 1, tn), lambda j: (layer, 0, j)),
        ],
        out_specs=pl.BlockSpec((cond.shape[0], tn), lambda j: (0, j)),
        compiler_params=_params(("arbitrary",)),
        name="ada",
    )(cond, w_ada, b_ada.reshape(depth, 1, n))


def _rope(x, cos, sin_signed, first_half):
    outs = []
    for c in range(x.shape[1] // LANES):
        xc = x[:, c * LANES:(c + 1) * LANES]
        partner = jnp.where(first_half, pltpu.roll(xc, LANES - ROPE_FREQS, 1), pltpu.roll(xc, ROPE_FREQS, 1))
        outs.append(xc * cos + partner * sin_signed)
    return jnp.concatenate(outs, axis=1)


def _store_heads(cache_ref, slot, rows):
    cache_ref[slot] = pltpu.einshape("s(hd)->shd", rows, h=N_HEADS)


def _qkv_kernel(*refs, n_prompt_tiles, stream_kind, n_prev):
    n_x = _STREAM_REFS[stream_kind]
    x_refs, (mod_ref, w_ref, cos_ref, sin_ref) = refs[:n_x], refs[n_x:n_x + 4]
    prev = refs[n_x + 4:n_x + 4 + 2 * max(n_prev, 0)]
    q_ref, k_ref, v_ref, ko_ref, vo_ref = refs[n_x + 4 + 2 * max(n_prev, 0):][:5]
    t = pl.program_id(0)
    d = w_ref.shape[0]

    def projector(is_prompt):
        x = _stream_rows(stream_kind, x_refs, is_prompt)
        if stream_kind == "pending":
            refs[-1][...] = x
        h = (_layernorm(x) * (1.0 + mod_ref[1:2, :]) + mod_ref[0:1, :]).astype(BF16)
        return lambda j: _dot(h, w_ref[:, j * d:(j + 1) * d])

    @pl.when(t < n_prompt_tiles)
    def _():
        project = projector(True)
        q_ref[...] = (project(0) * Q_SCALE).astype(BF16)
        k = project(1)
        k_ref[...] = k.astype(BF16)
        v = project(2)
        v_ref[...] = v.astype(BF16)
        if n_prev < 0:
            ko_ref[...] = k
            vo_ref[...] = v
        else:
            for a in range(n_prev):
                _store_heads(ko_ref, a, prev[2 * a][...])
                _store_heads(vo_ref, a, prev[2 * a + 1][...])
            _store_heads(ko_ref, n_prev, k)
            _store_heads(vo_ref, n_prev, v)

    @pl.when(t >= n_prompt_tiles)
    def _():
        project = projector(False)
        lane = lax.broadcasted_iota(I32, (q_ref.shape[0], LANES), 1)
        first_half = (lane % (2 * ROPE_FREQS)) < ROPE_FREQS
        cos = cos_ref[...]
        sin = sin_ref[...]
        q_ref[...] = _rope(project(0) * Q_SCALE, cos, sin, first_half).astype(BF16)
        k_ref[...] = _rope(project(1), cos, sin, first_half).astype(BF16)
        v_ref[...] = project(2).astype(BF16)


def _qkv(stream, mod, w_qkv, cos, sin, prev_kv, finish_cache, n_prompt, dec_seq, t_tok):
    d = w_qkv.shape[0]
    tl = _Tiles(n_prompt, dec_seq, QKV_TILE)
    tm = tl.tm
    pending = stream[0] == "pending"
    row_out = jax.ShapeDtypeStruct((t_tok, d), F32), pl.BlockSpec((tm, d), tl.row)
    n_prev = len(prev_kv) if finish_cache else -1
    if finish_cache:
        n_slots = n_prev + 1
        kv_shape = jax.ShapeDtypeStruct((n_prompt // tm, n_slots, tm, N_HEADS, V_DIM), F32)
        kv_spec = pl.BlockSpec((None, n_slots, tm, N_HEADS, V_DIM), lambda t: (tl.prompt_part(t)[0], 0, 0, 0, 0))
    else:
        kv_shape = jax.ShapeDtypeStruct((n_prompt, d), F32)
        kv_spec = pl.BlockSpec((tm, d), tl.prompt_part)
    prev_flat = [a for kv in prev_kv for a in kv] if finish_cache else []
    return pl.pallas_call(
        functools.partial(_qkv_kernel, n_prompt_tiles=tl.npt, stream_kind=stream[0], n_prev=n_prev),
        out_shape=(
            jax.ShapeDtypeStruct((t_tok, d), BF16),
            jax.ShapeDtypeStruct((t_tok, d), BF16),
            jax.ShapeDtypeStruct((t_tok, d), BF16),
            kv_shape, kv_shape,
        ) + ((row_out[0],) if pending else ()),
        grid=(t_tok // tm,),
        in_specs=tl.stream_specs(stream, t_tok // tm) + [
            pl.BlockSpec((None, N_MOD, d), tl.cond),
            pl.BlockSpec(w_qkv.shape, lambda t: (0, 0)),
            pl.BlockSpec((tm, LANES), tl.latent_pos),
            pl.BlockSpec((tm, LANES), tl.latent_pos),
        ] + [pl.BlockSpec((tm, d), tl.prompt_part) for _ in prev_flat],
        out_specs=(
            pl.BlockSpec((tm, d), tl.row),
            pl.BlockSpec((tm, d), tl.row),
            pl.BlockSpec((tm, d), tl.row),
            kv_spec, kv_spec,
        ) + ((row_out[1],) if pending else ()),
        compiler_params=_params(("arbitrary",)),
        name="ln_qkv_rope",
    )(*_stream_args(stream), mod, w_qkv, cos, sin, *prev_flat)


def _row_chains(n_rows):
    return [slice(r, r + CHAIN_ROWS) for r in range(0, n_rows, CHAIN_ROWS)]


def _chan_dft_kernel(*refs, n_prompt_tiles, stream_kind):
    n_x = _STREAM_REFS[stream_kind]
    x_refs, (mod_ref, dcs_ref, xc_ref, xs_ref) = refs[:n_x], refs[n_x:n_x + 4]
    g = FOURIER_GROUP_DIM
    is_prompt = pl.program_id(0) < n_prompt_tiles
    for rows in _row_chains(xc_ref.shape[0]):
        x = _stream_rows(stream_kind, x_refs, is_prompt, rows)
        if stream_kind == "pending":
            refs[-1][rows, :] = x
        h = (_layernorm(x) * (1.0 + mod_ref[1:2, :]) + mod_ref[0:1, :]).astype(BF16)
        for i in range(N_FOURIER_GROUPS):
            r = _dot(h[:, i * g:(i + 1) * g], dcs_ref[...])
            xc_ref[rows, i * g:(i + 1) * g] = r[:, :g].astype(BF16)
            xs_ref[rows, i * g:(i + 1) * g] = r[:, g:].astype(BF16)


def _chan_dft(stream, mod, dcs, n_prompt, dec_seq, t_tok):
    d = mod.shape[-1]
    tl = _Tiles(n_prompt, dec_seq, TOKEN_TILE)
    tm = tl.tm
    pending = stream[0] == "pending"
    half = jax.ShapeDtypeStruct((t_tok, d), BF16)
    return pl.pallas_call(
        functools.partial(_chan_dft_kernel, n_prompt_tiles=tl.npt, stream_kind=stream[0]),
        out_shape=(half, half) + ((jax.ShapeDtypeStruct((t_tok, d), F32),) if pending else ()),
        grid=(t_tok // tm,),
        in_specs=tl.stream_specs(stream, t_tok // tm) + [
            pl.BlockSpec((None, N_MOD, d), tl.cond),
            pl.BlockSpec(dcs.shape, lambda t: (0, 0)),
        ],
        out_specs=(pl.BlockSpec((tm, d), tl.row),) * (3 if pending else 2),
        compiler_params=_params(("arbitrary",)),
        name="ln_chan_dft",
    )(*_stream_args(stream), mod, dcs)


def _seq_dft_kernel(cs_ref, ss_ref, xc_ref, xs_ref, o_ref, *, norm):
    f = _dot(cs_ref[...], xc_ref[...]) - _dot(ss_ref[...], xs_ref[...])
    o_ref[...] = (f * norm).astype(BF16)


def _seq_dft(cs, ss, xc, xs, batch, seq, row_offset):
    d = xc.shape[1]
    tm = min(QKV_TILE, seq)
    spt = seq // tm
    off_seq = row_offset // seq
    return pl.pallas_call(
        functools.partial(_seq_dft_kernel, norm=1.0 / math.sqrt(seq * FOURIER_GROUP_DIM)),
        out_shape=jax.ShapeDtypeStruct((batch * seq, d), BF16),
        grid=(batch, spt),
        in_specs=[
            pl.BlockSpec((tm, seq), lambda b, i: (i, 0)),
            pl.BlockSpec((tm, seq), lambda b, i: (i, 0)),
            pl.BlockSpec((seq, d), lambda b, i: (off_seq + b, 0)),
            pl.BlockSpec((seq, d), lambda b, i: (off_seq + b, 0)),
        ],
        out_specs=pl.BlockSpec((tm, d), lambda b, i: (b * spt + i, 0)),
        compiler_params=_params(("arbitrary", "arbitrary")),
        name=f"seq_dft_{seq}",
    )(cs, ss, xc, xs)


def _seq_dft_mirror_kernel(cs_ref, ss_ref, xc_ref, xs_ref, lo_ref, hi_ref, *, norm):
    tm = lo_ref.shape[0]
    a = _dot(cs_ref[...], xc_ref[...])
    b = _dot(ss_ref[...], xs_ref[...])
    lo_ref[...] = ((a[:tm] - b[:tm]) * norm).astype(BF16)
    both = ((a + b) * norm).astype(BF16)
    r = lax.broadcasted_iota(I32, (tm, cs_ref.shape[0]), 0)
    c = lax.broadcasted_iota(I32, (tm, cs_ref.shape[0]), 1)
    flip = jnp.where(c == tm - r, 1.0, 0.0).astype(BF16)
    hi_ref[...] = _dot(flip, both).astype(BF16)


def _seq_dft_mirror(cs, ss, xc, xs, batch, seq, row_offset):
    d = xc.shape[1]
    tm = QKV_TILE
    half_tiles = seq // tm // 2
    off_seq = row_offset // seq
    half = jax.ShapeDtypeStruct((batch * seq // 2, d), BF16)
    return pl.pallas_call(
        functools.partial(_seq_dft_mirror_kernel, norm=1.0 / math.sqrt(seq * FOURIER_GROUP_DIM)),
        out_shape=(half, half),
        grid=(batch, half_tiles),
        in_specs=[
            pl.BlockSpec((pl.Element(tm + BF16_SUBLANES), pl.Element(seq)), lambda b, i: (i * tm, 0)),
            pl.BlockSpec((pl.Element(tm + BF16_SUBLANES), pl.Element(seq)), lambda b, i: (i * tm, 0)),
            pl.BlockSpec((seq, d), lambda b, i: (off_seq + b, 0)),
            pl.BlockSpec((seq, d), lambda b, i: (off_seq + b, 0)),
        ],
        out_specs=(pl.BlockSpec((tm, d), lambda b, i: (b * half_tiles + i, 0)),
                   pl.BlockSpec((tm, d), lambda b, i: (b * half_tiles + half_tiles - 1 - i, 0))),
        compiler_params=_params(("arbitrary", "arbitrary")),
        name=f"seq_dft_mirror_{seq}",
    )(cs, ss, xc, xs)


def _diff_lambda(lam_ref, lam_init):
    lv = lam_ref[...]
    return (jnp.exp(jnp.sum(lv[0:1] * lv[1:2], axis=-1, keepdims=True))
            - jnp.exp(jnp.sum(lv[2:3] * lv[3:4], axis=-1, keepdims=True)) + lam_init)


def _diff_attn_head(q, k, v_ext, lam, gain, lam_init):
    tq = q.shape[0]
    lane = lax.broadcasted_iota(I32, q.shape, 1)
    zero = jnp.zeros_like(q)
    qq = jnp.concatenate([jnp.where(lane < HEAD_DIM, q, zero), jnp.where(lane >= HEAD_DIM, q, zero)], axis=0)
    parts = []
    for r in range(0, 2 * tq, ATTN_ROW_CHUNK):
        s = lax.dot_general(qq[r:r + ATTN_ROW_CHUNK], k, (((1,), (1,)), ((), ())), preferred_element_type=F32)
        e = jnp.exp2(s - jnp.max(s, axis=-1, keepdims=True)).astype(BF16)
        parts.append(_dot(e, v_ext))
    oe = jnp.concatenate(parts, axis=0)
    o = oe[:, :V_DIM] / oe[:, V_DIM:]
    o = o[:tq] - lam * o[tq:]
    o = o * lax.rsqrt(jnp.mean(o * o, axis=-1, keepdims=True) + LN_EPS)
    return o * gain * (1.0 - lam_init)


def _attn_prompt_kernel(lam_ref, gain_ref, q_ref, k_ref, v_ref, o_ref, *, lam_init):
    lam = _diff_lambda(lam_ref, lam_init)
    gain = gain_ref[...]
    ones = jnp.ones((k_ref.shape[0], V_DIM), BF16)
    for hd in range(N_HEADS):
        cols = slice(hd * V_DIM, (hd + 1) * V_DIM)
        v_ext = jnp.concatenate([v_ref[:, cols], ones], axis=1)
        o_ref[:, cols] = _diff_attn_head(q_ref[:, cols], k_ref[:, cols], v_ext, lam, gain, lam_init).astype(BF16)


def _attn_latent_kernel(lam_ref, gain_ref, q_ref, k_ref, v_ref, kc_ref, vc_ref, o_ref, kall_ref, vext_ref, *,
                        lam_init):
    n_new = k_ref.shape[0]

    @pl.when(pl.program_id(2) == 0)
    def _():
        kall_ref[:n_new, :] = k_ref[...]
        kall_ref[n_new:, :] = kc_ref[...]
        vext_ref[:n_new, :V_DIM] = v_ref[...]
        vext_ref[n_new:, :V_DIM] = vc_ref[...]
        vext_ref[:, V_DIM:] = jnp.ones((vext_ref.shape[0], V_DIM), BF16)

    o = _diff_attn_head(q_ref[...], kall_ref[...], vext_ref[...], _diff_lambda(lam_ref, lam_init), gain_ref[...],
                        lam_init)
    o_ref[...] = o.astype(BF16)


def _attn_prompt(lam_vecs, gain, q, k, v, batch, seq, lam_init):
    d = q.shape[1]
    blk = pl.BlockSpec((seq, d), lambda b: (b, 0))
    return pl.pallas_call(
        functools.partial(_attn_prompt_kernel, lam_init=lam_init),
        out_shape=jax.ShapeDtypeStruct((batch * seq, d), BF16),
        grid=(batch,),
        in_specs=[
            pl.BlockSpec(lam_vecs.shape, lambda b: (0, 0)),
            pl.BlockSpec(gain.shape, lambda b: (0, 0)),
            blk, blk, blk,
        ],
        out_specs=blk,
        compiler_params=_params(("arbitrary",)),
        name="diff_attn_ctx",
    )(lam_vecs, gain, q, k, v)


def _attn_sample(lam_vecs, gain, q, k, v, kc, vc, batch, seq, n_ctx, row_offset, lam_init):
    d = q.shape[1]
    tq = Q_TILE
    qpt = seq // tq
    off_seq = row_offset // seq
    off_tile = row_offset // tq
    qmap = lambda b, h, i: (off_tile + b * qpt + i, h)
    kmap = lambda b, h, i: (off_seq + b, h)
    cmap = lambda b, h, i: (b, h)
    return pl.pallas_call(
        functools.partial(_attn_latent_kernel, lam_init=lam_init),
        out_shape=jax.ShapeDtypeStruct((batch * seq, d), BF16),
        grid=(batch, N_HEADS, qpt),
        in_specs=[
            pl.BlockSpec(lam_vecs.shape, lambda b, h, i: (0, 0)),
            pl.BlockSpec(gain.shape, lambda b, h, i: (0, 0)),
            pl.BlockSpec((tq, V_DIM), qmap),
            pl.BlockSpec((seq, V_DIM), kmap),
            pl.BlockSpec((seq, V_DIM), kmap),
            pl.BlockSpec((n_ctx, V_DIM), cmap),
            pl.BlockSpec((n_ctx, V_DIM), cmap),
        ],
        out_specs=pl.BlockSpec((tq, V_DIM), lambda b, h, i: (b * qpt + i, h)),
        scratch_shapes=[pltpu.VMEM((seq + n_ctx, V_DIM), BF16), pltpu.VMEM((seq + n_ctx, 2 * V_DIM), BF16)],
        compiler_params=_params(("arbitrary", "arbitrary", "arbitrary")),
        name="diff_attn_latent",
    )(lam_vecs, gain, q, k, v, kc, vc)


def _route(lg):
    lane = lax.broadcasted_iota(I32, lg.shape, 1)
    lane_f = lane.astype(F32)
    neg = jnp.float32(-jnp.inf)
    big = jnp.float32(LANES)
    gl = jnp.where(lane < N_EXPERT_GROUPS, lg, neg)
    gmax = jnp.max(gl, axis=-1, keepdims=True)
    g_prob = 1.0 / jnp.sum(jnp.exp(gl - gmax), axis=-1, keepdims=True)
    g_idx = jnp.min(jnp.where(gl == gmax, lane_f, big), axis=-1, keepdims=True)
    lo = N_EXPERT_GROUPS + EXPERTS_PER_GROUP * g_idx
    el = jnp.where((lane_f >= lo) & (lane_f < lo + EXPERTS_PER_GROUP), lg, neg)
    m1 = jnp.max(el, axis=-1, keepdims=True)
    i1 = jnp.min(jnp.where(el == m1, lane_f, big), axis=-1, keepdims=True)
    el2 = jnp.where(lane_f == i1, neg, el)
    m2 = jnp.max(el2, axis=-1, keepdims=True)
    i2 = jnp.min(jnp.where(el2 == m2, lane_f, big), axis=-1, keepdims=True)
    t = jnp.exp(m2 - m1)
    w1 = g_prob / (1.0 + t)
    w2 = g_prob * t / (1.0 + t)
    out = jnp.where(lane == 0, i1 - N_EXPERT_GROUPS, 0.0)
    out = jnp.where(lane == 1, i2 - N_EXPERT_GROUPS, out)
    out = jnp.where(lane == 2, w1, out)
    out = jnp.where(lane == 3, w2, out)
    return out


def _pack_bf16_pairs(h):
    n = h.shape[1] // 2
    bits = lax.bitcast_convert_type(h.astype(BF16).astype(F32), jnp.uint32)
    return (bits[:, :n] >> 16) | bits[:, n:]


def _unpack_bf16_pairs(p):
    lo = lax.bitcast_convert_type(p << 16, F32)
    hi = lax.bitcast_convert_type(p & jnp.uint32(0xFFFF0000), F32)
    return jnp.concatenate([lo, hi], axis=1)


def _mix_out_kernel(*refs, n_prompt_tiles, tiles_per_seq, n_mixed, stream_kind):
    n_x = _STREAM_REFS[stream_kind]
    a_refs, w_ref = refs[:n_mixed], refs[n_mixed]
    x_refs = refs[n_mixed + 1:n_mixed + 1 + n_x]
    mod_ref, gb_ref, wr_ref, x1_ref, hp_ref, route_ref = refs[n_mixed + 1 + n_x:]
    t = pl.program_id(0)
    is_prompt = t < n_prompt_tiles
    first_half = (t - n_prompt_tiles) % tiles_per_seq < tiles_per_seq // 2
    for rows in _row_chains(x1_ref.shape[0]):
        a = a_refs[1][rows, :] if n_mixed == 2 else jnp.where(first_half, a_refs[1][rows, :], a_refs[2][rows, :])
        a = jnp.where(is_prompt, a_refs[0][rows, :], a)
        out = _dot(a, w_ref[...])
        x = _stream_rows(stream_kind, x_refs, is_prompt, rows)
        x1 = _deepnorm(x, mod_ref[2:3, :], out) * gb_ref[0:1, :] + gb_ref[1:2, :]
        x1_ref[rows, :] = x1
        h2 = _layernorm(x1) * (1.0 + mod_ref[4:5, :]) + mod_ref[3:4, :]
        hp_ref[rows, :] = _pack_bf16_pairs(h2)
        route_ref[rows, :] = _route(_dot(h2.astype(BF16), wr_ref[...]))


def _mix_out(mixed, w, stream, mod, gb, wr, n_prompt, dec_seq, t_tok):
    d = w.shape[1]
    tl = _Tiles(n_prompt, dec_seq, TOKEN_TILE)
    tm = tl.tm
    if len(mixed) == 2:
        latent_specs = [pl.BlockSpec((tm, d), tl.latent_part)]
    else:
        half = tl.tps // 2

        def half_map(second):
            def index(t):
                lt = jnp.maximum(t - tl.npt, 0)
                within = lt % tl.tps - second * half
                return (lt // tl.tps * half + jnp.clip(within, 0, half - 1), 0)
            return index
        latent_specs = [pl.BlockSpec((tm, d), half_map(0)), pl.BlockSpec((tm, d), half_map(1))]
    return pl.pallas_call(
        functools.partial(_mix_out_kernel, n_prompt_tiles=tl.npt, tiles_per_seq=tl.tps, n_mixed=len(mixed),
                          stream_kind=stream[0]),
        out_shape=(
            jax.ShapeDtypeStruct((t_tok, d), F32),
            jax.ShapeDtypeStruct((t_tok, d // 2), jnp.uint32),
            jax.ShapeDtypeStruct((t_tok, LANES), F32),
        ),
        grid=(t_tok // tm,),
        in_specs=[pl.BlockSpec((tm, d), tl.prompt_part)] + latent_specs + [
            pl.BlockSpec(w.shape, lambda t: (0, 0)),
        ] + tl.stream_specs(stream, t_tok // tm) + [
            pl.BlockSpec((None, N_MOD, d), tl.cond),
            pl.BlockSpec(gb.shape, lambda t: (0, 0)),
            pl.BlockSpec(wr.shape, lambda t: (0, 0)),
        ],
        out_specs=(
            pl.BlockSpec((tm, d), tl.row),
            pl.BlockSpec((tm, d // 2), tl.row),
            pl.BlockSpec((tm, LANES), tl.row),
        ),
        compiler_params=_params(("arbitrary",)),
        name="mix_out_postnorm_router",
    )(*mixed, w, *_stream_args(stream), mod, gb, wr)


def _lane_prefix_sum(x, lane):
    sh = 1
    while sh < LANES:
        x = x + jnp.where(lane >= sh, pltpu.roll(x, sh, 1), 0.0)
        sh *= 2
    return x


def _lane_suffix_min_exclusive(x, lane):
    big = float(LANES)
    y = jnp.where(lane + 1 < LANES, pltpu.roll(x, LANES - 1, 1), big)
    sh = 1
    while sh < LANES:
        y = jnp.minimum(y, jnp.where(lane + sh < LANES, pltpu.roll(y, LANES - sh, 1), big))
        sh *= 2
    return y


def _work_items(counts, starts, ends, lane8, n_moe_tiles):
    shift = int(math.log2(MOE_TILE))
    first_tile = (starts.astype(I32) >> shift).astype(F32)
    last_tile = ((ends.astype(I32) - 1) >> shift).astype(F32)
    n_it = jnp.where(counts > 0.0, last_tile - first_tile + 1.0, 0.0)
    it_end = _lane_prefix_sum(n_it, lane8)
    it_start = it_end - n_it
    total = it_end[0:1, LANES - 1:LANES]
    sub = lax.broadcasted_iota(I32, (LANES, LANES), 0)
    rows = lambda x: jnp.broadcast_to(x[0:1, :], (LANES, LANES))
    used = counts > 0.0
    weight_slot = ((_lane_prefix_sum(jnp.where(used, 1.0, 0.0), lane8) - 1.0).astype(I32) & 1).astype(F32)
    next_used = _lane_suffix_min_exclusive(jnp.where(used, lane8.astype(F32), float(LANES)), lane8)
    per_expert = (starts, ends, first_tile, it_start, it_end, weight_slot, next_used)
    stacked = jnp.zeros((LANES, LANES), F32)
    for j, vec in enumerate(per_expert):
        stacked = jnp.where(sub == j, rows(vec), stacked)
    cols = stacked.T
    col = lambda j: cols[:, j:j + 1]
    sub_f = sub.astype(F32)
    w = lax.broadcasted_iota(I32, (LANES, LANES), 1).astype(F32)
    ex = jnp.sum(jnp.where((sub < N_EXPERTS) & (col(4) <= w), 1.0, 0.0), axis=0, keepdims=True)
    ex = jnp.minimum(ex, N_EXPERTS - 1.0)
    w1 = w[0:1, :]
    valid = w1 < total
    ex = jnp.where(valid, ex, jnp.max(jnp.where(valid, ex, 0.0), axis=-1, keepdims=True))
    onehot = sub_f == ex
    pick = lambda j: jnp.sum(jnp.where(onehot, col(j), 0.0), axis=0, keepdims=True)
    tile = jnp.where(valid, pick(2) + (w1 - pick(3)), n_moe_tiles - 1.0)
    lo = jnp.where(valid, jnp.maximum(pick(0), tile * MOE_TILE), 0.0)
    hi = jnp.where(valid, jnp.minimum(pick(1), (tile + 1.0) * MOE_TILE), 0.0)
    b8 = lambda x: jnp.broadcast_to(x, (SUBLANES, LANES))
    ex8, tile8 = b8(ex), b8(tile)
    first = jnp.where((lane8 == 0) | (tile8 != pltpu.roll(tile8, 1, 1)), 1.0, 0.0)
    newexp = jnp.where((lane8 == 0) | (ex8 != pltpu.roll(ex8, 1, 1)), 1.0, 0.0)
    sub8 = lax.broadcasted_iota(I32, (SUBLANES, LANES), 0)
    table = jnp.zeros((SUBLANES, LANES), F32)
    for j, vec in enumerate((ex8, tile8, b8(lo), b8(hi), first, newexp, b8(pick(5)), b8(pick(6)))):
        table = jnp.where(sub8 == j, vec, table)
    return table


def _plan_kernel(route_ref, pos_ref, items_ref, tri_ref, carry_ref, tot_ref, *, n_moe_tiles):
    p = pl.program_id(0)
    t = pl.program_id(1)
    tm = route_ref.shape[0]
    r = route_ref[...]
    lane = lax.broadcasted_iota(I32, (tm, LANES), 1)
    lane_f = lane.astype(F32)
    e0 = r[:, 0:1]
    e1 = r[:, 1:2] + N_EXPERTS
    m = jnp.where((lane_f == e0) | (lane_f == e1), 1.0, 0.0)
    colsum = jnp.sum(m, axis=0, keepdims=True)

    @pl.when((p == 0) & (t == 0))
    def _():
        carry_ref[...] = jnp.zeros_like(carry_ref)
        row = lax.broadcasted_iota(I32, (tm, tm), 0)
        col = lax.broadcasted_iota(I32, (tm, tm), 1)
        tri_ref[...] = jnp.where(row > col, 1.0, 0.0).astype(BF16)

    @pl.when(p == 0)
    def _():
        carry_ref[...] += colsum

    @pl.when((p == 0) & (t == pl.num_programs(1) - 1))
    def _():
        tot_ref[...] = carry_ref[...]
        carry_ref[...] = jnp.zeros_like(carry_ref)

    @pl.when(p == 1)
    def _():
        lane8 = lax.broadcasted_iota(I32, (SUBLANES, LANES), 1)
        tot = tot_ref[...]
        is_first = lane8 < N_EXPERTS
        tot0 = jnp.where(is_first, tot, 0.0)
        counts = jnp.where(is_first, tot + pltpu.roll(tot, LANES - N_EXPERTS, 1), 0.0)
        ends = _lane_prefix_sum(counts, lane8)
        starts = ends - counts
        base = jnp.where(is_first, starts, pltpu.roll(starts + tot0, N_EXPERTS, 1))
        before = _dot(tri_ref[...], m.astype(BF16)) + carry_ref[0:1, :]
        carry_ref[...] += colsum
        rows = before + base[0:1, :]
        pos0 = jnp.sum(jnp.where(lane_f == e0, rows, 0.0), axis=-1, keepdims=True)
        pos1 = jnp.sum(jnp.where(lane_f == e1, rows, 0.0), axis=-1, keepdims=True)
        both = jnp.where(lane == 0, pos0, jnp.where(lane == 1, pos1, 0.0))
        pos_ref[...] = both.T[0:SUBLANES, :].astype(I32)

        @pl.when(t == 0)
        def _():
            items_ref[...] = _work_items(counts, starts, ends, lane8, n_moe_tiles).astype(I32)


def _moe_plan(route):
    t_tok = route.shape[0]
    tm = PLAN_TILE
    n_moe_tiles = 2 * t_tok // MOE_TILE
    assert n_moe_tiles + N_EXPERTS - 1 <= LANES and 2 * N_EXPERTS <= LANES and MOE_TILE & (MOE_TILE - 1) == 0
    return pl.pallas_call(
        functools.partial(_plan_kernel, n_moe_tiles=n_moe_tiles),
        out_shape=(jax.ShapeDtypeStruct((SUBLANES, t_tok), I32), jax.ShapeDtypeStruct((SUBLANES, LANES), I32)),
        grid=(2, t_tok // tm),
        in_specs=[pl.BlockSpec((tm, LANES), lambda p, t: (t, 0))],
        out_specs=(pl.BlockSpec((SUBLANES, tm), lambda p, t: (0, t * p)),
                   pl.BlockSpec((SUBLANES, LANES), lambda p, t: (0, 0))),
        scratch_shapes=[pltpu.VMEM((tm, tm), BF16), pltpu.VMEM((SUBLANES, LANES), F32),
                        pltpu.VMEM((SUBLANES, LANES), F32)],
        compiler_params=_params(("arbitrary", "arbitrary")),
        name="moe_positions",
    )(route)


def _sc_workers():
    info = plsc.get_sparse_core_info()
    return info.num_cores, info.num_cores * info.num_subcores


def _sc_pipeline(n_chunks, loads, stores):
    def start(copies):
        for cp in copies:
            cp.start()

    def wait(copies):
        for cp in copies:
            cp.wait()

    start(loads(0, 0))
    for j in range(n_chunks):
        b = j % 2
        if j + 1 < n_chunks:
            if j >= 1:
                wait(stores(j - 1, 1 - b))
            start(loads(j + 1, 1 - b))
        wait(loads(j, b))
        start(stores(j, b))
    if n_chunks >= 2:
        wait(stores(n_chunks - 2, n_chunks % 2))
    wait(stores(n_chunks - 1, (n_chunks - 1) % 2))


def _sc_chunking(n_rows, d):
    n_cores, n_workers = _sc_workers()
    per_worker = n_rows // n_workers
    k = SC_BUFFER_BYTES // (d * 4)
    n_chunks = per_worker // k
    assert n_chunks * k * n_workers == n_rows and k <= SC_MAX_INDEX_CHUNK
    return n_cores, n_workers, per_worker, k, n_chunks


def _sc_gather_rows(table, idx):
    n, d = idx.shape[0], table.shape[1]
    n_cores, n_workers, per_worker, k, n_chunks = _sc_chunking(n, d)

    def body(table_hbm, idx_hbm, out_hbm, idx_v, rows_v, gsem, osem):
        wid = lax.axis_index("s") * n_cores + lax.axis_index("c")
        base = wid * per_worker
        pltpu.sync_copy(idx_hbm.at[wid], idx_v)
        _sc_pipeline(
            n_chunks,
            lambda j, b: [pltpu.make_async_copy(table_hbm.at[idx_v.at[j]], rows_v.at[b], gsem.at[b])],
            lambda j, b: [pltpu.make_async_copy(rows_v.at[b], out_hbm.at[pl.ds(base + j * k, k)], osem.at[b])])

    return pl.kernel(
        body,
        out_type=jax.ShapeDtypeStruct((n, d), table.dtype),
        mesh=plsc.VectorSubcoreMesh(core_axis_name="c", subcore_axis_name="s"),
        scratch_types=[pltpu.VMEM((n_chunks, k), I32), pltpu.VMEM((2, k, d), table.dtype),
                       pltpu.SemaphoreType.DMA((2,)), pltpu.SemaphoreType.DMA((2,))],
        name="sc_row_gather",
    )(table, idx.reshape(n_workers, n_chunks, k))


def _sc_scatter_rows(src, idx):
    n_lists, n_src = idx.shape
    d = src.shape[1]
    n_cores, n_workers, per_worker, k, n_chunks = _sc_chunking(n_src, d)

    def body(src_hbm, idx_hbm, out_hbm, idx_v, rows_v, gsem, osem):
        wid = lax.axis_index("s") * n_cores + lax.axis_index("c")
        base = wid * per_worker
        pltpu.sync_copy(idx_hbm.at[wid], idx_v)
        _sc_pipeline(
            n_chunks,
            lambda j, b: [pltpu.make_async_copy(src_hbm.at[pl.ds(base + j * k, k)], rows_v.at[b], gsem.at[b])],
            lambda j, b: [pltpu.make_async_copy(rows_v.at[b], out_hbm.at[idx_v.at[c, j]], osem.at[b, c])
                          for c in range(n_lists)])

    return pl.kernel(
        body,
        out_type=jax.ShapeDtypeStruct((n_lists * n_src, d), src.dtype),
        mesh=plsc.VectorSubcoreMesh(core_axis_name="c", subcore_axis_name="s"),
        scratch_types=[pltpu.VMEM((n_lists, n_chunks, k), I32), pltpu.VMEM((2, k, d), src.dtype),
                       pltpu.SemaphoreType.DMA((2,)), pltpu.SemaphoreType.DMA((2, n_lists))],
        name="sc_row_scatter",
    )(src, idx.reshape(n_lists, n_workers, n_chunks, k).transpose(1, 0, 2, 3))


IT_EXPERT, IT_TILE, IT_LO, IT_HI, IT_FIRST, IT_NEWEXP, IT_SLOT, IT_NEXT = range(8)


def _moe_kernel(it_ref, xs_ref, wg_hbm, wu_hbm, wd_hbm, o_ref, wg_f, wu_f, wd_f, sem, *, layer):
    w = pl.program_id(0)

    def weight_copies(expert, slot):
        return [pltpu.make_async_copy(src.at[layer, expert], dst.at[slot], sem.at[slot, j])
                for j, (src, dst) in enumerate(((wg_hbm, wg_f), (wu_hbm, wu_f), (wd_hbm, wd_f)))]

    @pl.when(it_ref[IT_NEWEXP, w] == 1)
    def _():
        expert = it_ref[IT_EXPERT, w]
        slot = it_ref[IT_SLOT, w]
        nxt = it_ref[IT_NEXT, w]

        @pl.when(w == 0)
        def _():
            for cp in weight_copies(expert, slot):
                cp.start()

        for cp in weight_copies(expert, slot):
            cp.wait()

        @pl.when(nxt < N_EXPERTS)
        def _():
            for cp in weight_copies(nxt, 1 - slot):
                cp.start()

    slot = it_ref[IT_SLOT, w]
    tm = xs_ref.shape[0]
    x = _unpack_bf16_pairs(xs_ref[...]).astype(BF16)
    a = _dot(x, wg_f[slot].astype(BF16))
    u = _dot(x, wu_f[slot].astype(BF16))
    act = (_silu(a) * u).astype(BF16)
    y = _pack_bf16_pairs(_dot(act, wd_f[slot].astype(BF16)))
    row = it_ref[IT_TILE, w] * tm + lax.broadcasted_iota(I32, (tm, 1), 0)
    mine = (row >= it_ref[IT_LO, w]) & (row < it_ref[IT_HI, w])

    @pl.when(it_ref[IT_FIRST, w] == 1)
    def _():
        o_ref[...] = jnp.where(mine, y, jnp.zeros_like(y))

    @pl.when(it_ref[IT_FIRST, w] == 0)
    def _():
        o_ref[...] = jnp.where(mine, y, o_ref[...])


def _moe_experts(items, xs, w_gate, w_up, w_down, layer):
    n_rows, dp = xs.shape
    _, _, d, f = w_gate.shape
    tm = MOE_TILE
    rmap = lambda w, it: (it[IT_TILE, w], 0)
    hbm = pl.BlockSpec(memory_space=pl.ANY)
    return pl.pallas_call(
        functools.partial(_moe_kernel, layer=layer),
        out_shape=jax.ShapeDtypeStruct((n_rows, d // 2), jnp.uint32),
        grid_spec=pltpu.PrefetchScalarGridSpec(
            num_scalar_prefetch=1,
            grid=(n_rows // tm + N_EXPERTS - 1,),
            in_specs=[pl.BlockSpec((tm, dp), rmap), hbm, hbm, hbm],
            out_specs=pl.BlockSpec((tm, d // 2), rmap),
            scratch_shapes=[
                pltpu.VMEM((2, d, f), F32), pltpu.VMEM((2, d, f), F32), pltpu.VMEM((2, f, d), F32),
                pltpu.SemaphoreType.DMA((2, 3)),
            ],
        ),
        compiler_params=_params(("arbitrary",)),
        name="moe_grouped_mlp",
    )(items, xs, w_gate, w_up, w_down)


def _final_postnorm_kernel(*refs, n_prompt_tiles):
    x2 = _finish_moe(*refs[:6])
    out_prompt, out_latent = refs[6:]

    @pl.when(pl.program_id(0) < n_prompt_tiles)
    def _():
        out_prompt[...] = x2

    @pl.when(pl.program_id(0) >= n_prompt_tiles)
    def _():
        out_latent[...] = x2


def _final_postnorm(stream, n_prompt, dec_seq, t_tok):
    d = stream[1][0].shape[1]
    tl = _Tiles(n_prompt, dec_seq, TOKEN_TILE)
    tm = tl.tm
    return pl.pallas_call(
        functools.partial(_final_postnorm_kernel, n_prompt_tiles=tl.npt),
        out_shape=(jax.ShapeDtypeStruct((n_prompt, d), F32), jax.ShapeDtypeStruct((t_tok - n_prompt, d), F32)),
        grid=(t_tok // tm,),
        in_specs=tl.stream_specs(stream, t_tok // tm),
        out_specs=(pl.BlockSpec((tm, d), tl.prompt_part), pl.BlockSpec((tm, d), tl.latent_part)),
        compiler_params=_params(("arbitrary",)),
        name="moe_combine_postnorm",
    )(*_stream_args(stream))


def _rope_tables(n_lat):
    nf = ROPE_FREQS
    s = np.arange(n_lat)
    lane = np.arange(LANES)
    inv = ROPE_BASE ** (-(lane % nf).astype(np.float64) / nf)
    use_col = (lane % HEAD_DIM) >= HEAD_DIM // 2
    p = np.where(use_col[None, :], (s % GRID_W)[:, None], (s // GRID_W)[:, None]).astype(np.float64)
    ang = p * inv[None, :]
    sign = np.where((lane % (2 * nf)) < nf, -1.0, 1.0)
    return jnp.asarray(np.cos(ang), F32), jnp.asarray(np.sin(ang) * sign[None, :], F32)


def _dft_tables(n):
    k = np.arange(n)
    ang = 2.0 * np.pi * ((k[:, None] * k[None, :]) % n).astype(np.float64) / n
    return np.cos(ang), np.sin(ang)


def kernel(x_prompt, x_sample, cache_k, cache_v, c, c_ctx, w_ada, b_ada, ln_gain, ln_bias, w_qkv, w_attn_out,
           lambda_q1, lambda_k1, lambda_q2, lambda_k2, subln_gain, w_fourier_out, w_router_group,
           w_router_expert, w_expert_gate, w_expert_up, w_expert_down):
    bp, sp, d = x_prompt.shape
    bs, n_lat, _ = x_sample.shape
    n_ctx = cache_k.shape[2]
    n_prompt = bp * sp
    t_tok = n_prompt + bs * n_lat
    assert d == D_MODEL and n_prompt % n_lat == 0 and n_lat % TOKEN_TILE == 0 and sp == QKV_TILE

    cond = jnp.concatenate([c_ctx[None, :], c, jnp.zeros((SUBLANES - 1 - bs, d), F32)], axis=0)
    mods = [_ada_layer(cond, w_ada, b_ada, i).reshape(SUBLANES, N_MOD, d) for i in range(DEPTH)]

    cos, sin = _rope_tables(n_lat)
    cc, sc = _dft_tables(FOURIER_GROUP_DIM)
    dcs = jnp.asarray(np.concatenate([cc, sc], axis=1), BF16)
    seq_tabs = {s: tuple(jnp.asarray(m, BF16) for m in _dft_tables(s)) for s in (sp, n_lat)}
    attn_layers = [i for i in range(DEPTH) if i % 2 == 0]

    stream = ("pair", (x_prompt.reshape(n_prompt, d), x_sample.reshape(bs * n_lat, d)))
    prev_kv = []
    new_k = new_v = None
    for i in range(DEPTH):
        mod = mods[i]
        if i % 2 == 0:
            a = i // 2
            lam_init = 0.8 - 0.6 * math.exp(-0.3 * i)
            last_attn = i == attn_layers[-1]
            q, k, v, kf, vf, *finished = _qkv(stream, mod, w_qkv[a].astype(BF16), cos, sin, prev_kv, last_attn,
                                              n_prompt, n_lat, t_tok)
            if last_attn:
                new_k, new_v = kf, vf
            else:
                prev_kv.append((kf, vf))
            lam_vecs = jnp.stack([lambda_q1[a], lambda_k1[a], lambda_q2[a], lambda_k2[a]], axis=0)
            gain = subln_gain[a][None, :]
            kc = cache_k[:, a].reshape(bs * n_ctx, d).astype(BF16)
            vc = cache_v[:, a].reshape(bs * n_ctx, d).astype(BF16)
            mixed = (_attn_prompt(lam_vecs, gain, q, k, v, bp, sp, lam_init),
                     _attn_sample(lam_vecs, gain, q, k, v, kc, vc, bs, n_lat, n_ctx, n_prompt, lam_init))
            w_mix = w_attn_out[a]
        else:
            xc, xsn, *finished = _chan_dft(stream, mod, dcs, n_prompt, n_lat, t_tok)
            mixed = (_seq_dft(*seq_tabs[sp], xc, xsn, bp, sp, 0),
                     *_seq_dft_mirror(*seq_tabs[n_lat], xc, xsn, bs, n_lat, n_prompt))
            w_mix = w_fourier_out[i // 2]
        if finished:
            stream = ("merged", (finished[0],))
        gb0 = jnp.stack([ln_gain[i, 0], ln_bias[i, 0]], axis=0)
        gb1 = jnp.stack([ln_gain[i, 1], ln_bias[i, 1]], axis=0)
        wr = jnp.concatenate([w_router_group[i], w_router_expert[i],
                              jnp.zeros((d, LANES - N_EXPERT_GROUPS - N_EXPERTS), F32)], axis=1)
        x1, hp, route = _mix_out(mixed, w_mix.astype(BF16), stream, mod, gb0, wr.astype(BF16), n_prompt, n_lat,
                                 t_tok)
        pos8, items = _moe_plan(route)
        pos = pos8[0:2]
        xs_sorted = _sc_scatter_rows(hp, pos)
        ys = _moe_experts(items, xs_sorted, w_expert_gate, w_expert_up, w_expert_down, i)
        yg = _sc_gather_rows(ys, pos.reshape(-1))
        stream = ("pending", (x1, yg, route, mods[i], gb1))

    out_prompt, out_latent = _final_postnorm(stream, n_prompt, n_lat, t_tok)
    y_prompt = out_prompt.reshape(bp, sp, d)
    y_sample = out_latent.reshape(bs, n_lat, d)
    return (y_prompt, y_sample, new_k.reshape(bp, len(attn_layers), sp, N_HEADS, 2 * HEAD_DIM),
            new_v.reshape(bp, len(attn_layers), sp, N_HEADS, V_DIM))
```

```python
import functools
import math

import numpy as np
import jax
import jax.numpy as jnp
from jax import lax
from jax.experimental import pallas as pl
from jax.experimental.pallas import tpu as pltpu
from jax.experimental.pallas import tpu_sc as plsc

F32 = jnp.float32
BF16 = jnp.bfloat16
I32 = jnp.int32

D_MODEL = 1024
DEPTH = 4
GRID_W = 64
N_HEADS = 8
HEAD_DIM = 64
V_DIM = 2 * HEAD_DIM
ROPE_BASE = 10000.0
ROPE_FREQS = HEAD_DIM // 4
N_FOURIER_GROUPS = 8
FOURIER_GROUP_DIM = D_MODEL // N_FOURIER_GROUPS
N_EXPERT_GROUPS = 4
EXPERTS_PER_GROUP = 8
N_EXPERTS = N_EXPERT_GROUPS * EXPERTS_PER_GROUP
N_MOD = 6
LN_EPS = 1e-5
DEEPNORM_ALPHA = (2.0 * DEPTH) ** 0.25
Q_SCALE = math.log2(math.e) * HEAD_DIM ** -0.5

LANES = 128
SUBLANES = 8
BF16_SUBLANES = 16
TOKEN_TILE = 512
CHAIN_ROWS = 256
QKV_TILE = 256
Q_TILE = 2048
ATTN_ROW_CHUNK = 128
MOE_TILE = 512
PLAN_TILE = 1024
ADA_COLUMNS = 1536
SC_BUFFER_BYTES = 64 * 1024
SC_MAX_INDEX_CHUNK = 128
VMEM_LIMIT = 48 * 1024 * 1024


def _params(semantics):
    return pltpu.CompilerParams(dimension_semantics=semantics, vmem_limit_bytes=VMEM_LIMIT)


def _layernorm(x, eps=LN_EPS):
    mu = jnp.mean(x, axis=-1, keepdims=True)
    xc = x - mu
    var = jnp.mean(xc * xc, axis=-1, keepdims=True)
    return xc * lax.rsqrt(var + eps)


def _deepnorm(x, branch_gate, branch):
    return _layernorm(x + (branch_gate * (1.0 / DEEPNORM_ALPHA)) * branch, LN_EPS / DEEPNORM_ALPHA ** 2)


def _silu(a):
    return a / (1.0 + jnp.exp(-a))


def _dot(a, b):
    return jnp.dot(a, b, preferred_element_type=F32)


class _Tiles:
    def __init__(self, n_prompt, dec_seq, tm):
        self.tm = tm
        self.npt = n_prompt // tm
        self.tps = dec_seq // tm

    def row(self, t):
        return (t, 0)

    def cond(self, t):
        return (jnp.where(t < self.npt, 0, (t - self.npt) // self.tps + 1), 0, 0)

    def latent_pos(self, t):
        return (jnp.maximum(t - self.npt, 0) % self.tps, 0)

    def prompt_part(self, t):
        return (jnp.minimum(t, self.npt - 1), 0)

    def latent_part(self, t):
        return (jnp.maximum(t - self.npt, 0), 0)

    def stream_specs(self, stream, n_tiles):
        kind, arrays = stream
        tm = self.tm
        if kind == "merged":
            return [pl.BlockSpec((tm, arrays[0].shape[1]), self.row)]
        if kind == "pair":
            return [pl.BlockSpec((tm, arrays[0].shape[1]), self.prompt_part),
                    pl.BlockSpec((tm, arrays[1].shape[1]), self.latent_part)]
        x1, yg, route, mod, gb = arrays
        return [pl.BlockSpec((tm, x1.shape[1]), self.row),
                pl.BlockSpec((tm, yg.shape[1]), self.row),
                pl.BlockSpec((tm, yg.shape[1]), lambda t: (t + n_tiles, 0)),
                pl.BlockSpec((tm, route.shape[1]), self.row),
                pl.BlockSpec((None,) + mod.shape[1:], self.cond),
                pl.BlockSpec(gb.shape, lambda t: (0, 0))]


_STREAM_REFS = {"merged": 1, "pair": 2, "pending": 6}


def _stream_args(stream):
    kind, arrays = stream
    if kind == "pending":
        x1, yg, route, mod, gb = arrays
        return [x1, yg, yg, route, mod, gb]
    return list(arrays)


def _finish_moe(x1_ref, y0_ref, y1_ref, route_ref, mod_ref, gb_ref, rows=slice(None)):
    r = route_ref[rows, :]
    moe = r[:, 2:3] * _unpack_bf16_pairs(y0_ref[rows, :]) + r[:, 3:4] * _unpack_bf16_pairs(y1_ref[rows, :])
    return _deepnorm(x1_ref[rows, :], mod_ref[5:6, :], moe) * gb_ref[0:1, :] + gb_ref[1:2, :]


def _stream_rows(kind, refs, is_prompt_tile, rows=slice(None)):
    if kind == "merged":
        return refs[0][rows, :]
    if kind == "pair":
        return jnp.where(is_prompt_tile, refs[0][rows, :], refs[1][rows, :])
    return _finish_moe(*refs, rows)


def _ada_kernel(cond_ref, w_ref, b_ref, o_ref):
    a = _silu(cond_ref[...])
    o_ref[...] = _dot(a.astype(BF16), w_ref[...].astype(BF16)) + b_ref[...]


def _ada_layer(cond, w_ada, b_ada, layer):
    depth, d, n = w_ada.shape
    tn = ADA_COLUMNS
    return pl.pallas_call(
        _ada_kernel,
        out_shape=jax.ShapeDtypeStruct((cond.shape[0], n), F32),
        grid=(n // tn,),
        in_specs=[
            pl.BlockSpec(cond.shape, lambda j: (0, 0)),
            pl.BlockSpec((None, d, tn), lambda j: (layer, 0, j)),
            pl.BlockSpec((None, 1, tn), lambda j: (layer, 0, j)),
        ],
        out_specs=pl.BlockSpec((cond.shape[0], tn), lambda j: (0, j)),
        compiler_params=_params(("arbitrary",)),
        name="ada",
    )(cond, w_ada, b_ada.reshape(depth, 1, n))


def _rope(x, cos, sin_signed, first_half):
    outs = []
    for c in range(x.shape[1] // LANES):
        xc = x[:, c * LANES:(c + 1) * LANES]
        partner = jnp.where(first_half, pltpu.roll(xc, LANES - ROPE_FREQS, 1), pltpu.roll(xc, ROPE_FREQS, 1))
        outs.append(xc * cos + partner * sin_signed)
    return jnp.concatenate(outs, axis=1)


def _store_heads(cache_ref, slot, rows):
    cache_ref[slot] = pltpu.einshape("s(hd)->shd", rows, h=N_HEADS)


def _qkv_kernel(*refs, n_prompt_tiles, stream_kind, n_prev):
    n_x = _STREAM_REFS[stream_kind]
    x_refs, (mod_ref, w_ref, cos_ref, sin_ref) = refs[:n_x], refs[n_x:n_x + 4]
    prev = refs[n_x + 4:n_x + 4 + 2 * max(n_prev, 0)]
    q_ref, k_ref, v_ref, ko_ref, vo_ref = refs[n_x + 4 + 2 * max(n_prev, 0):][:5]
    t = pl.program_id(0)
    d = w_ref.shape[0]

    def projector(is_prompt):
        x = _stream_rows(stream_kind, x_refs, is_prompt)
        if stream_kind == "pending":
            refs[-1][...] = x
        h = (_layernorm(x) * (1.0 + mod_ref[1:2, :]) + mod_ref[0:1, :]).astype(BF16)
        return lambda j: _dot(h, w_ref[:, j * d:(j + 1) * d])

    @pl.when(t < n_prompt_tiles)
    def _():
        project = projector(True)
        q_ref[...] = (project(0) * Q_SCALE).astype(BF16)
        k = project(1)
        k_ref[...] = k.astype(BF16)
        v = project(2)
        v_ref[...] = v.astype(BF16)
        if n_prev < 0:
            ko_ref[...] = k
            vo_ref[...] = v
        else:
            for a in range(n_prev):
                _store_heads(ko_ref, a, prev[2 * a][...])
                _store_heads(vo_ref, a, prev[2 * a + 1][...])
            _store_heads(ko_ref, n_prev, k)
            _store_heads(vo_ref, n_prev, v)

    @pl.when(t >= n_prompt_tiles)
    def _():
        project = projector(False)
        lane = lax.broadcasted_iota(I32, (q_ref.shape[0], LANES), 1)
        first_half = (lane % (2 * ROPE_FREQS)) < ROPE_FREQS
        cos = cos_ref[...]
        sin = sin_ref[...]
        q_ref[...] = _rope(project(0) * Q_SCALE, cos, sin, first_half).astype(BF16)
        k_ref[...] = _rope(project(1), cos, sin, first_half).astype(BF16)
        v_ref[...] = project(2).astype(BF16)


def _qkv(stream, mod, w_qkv, cos, sin, prev_kv, finish_cache, n_prompt, dec_seq, t_tok):
    d = w_qkv.shape[0]
    tl = _Tiles(n_prompt, dec_seq, QKV_TILE)
    tm = tl.tm
    pending = stream[0] == "pending"
    row_out = jax.ShapeDtypeStruct((t_tok, d), F32), pl.BlockSpec((tm, d), tl.row)
    n_prev = len(prev_kv) if finish_cache else -1
    if finish_cache:
        n_slots = n_prev + 1
        kv_shape = jax.ShapeDtypeStruct((n_prompt // tm, n_slots, tm, N_HEADS, V_DIM), F32)
        kv_spec = pl.BlockSpec((None, n_slots, tm, N_HEADS, V_DIM), lambda t: (tl.prompt_part(t)[0], 0, 0, 0, 0))
    else:
        kv_shape = jax.ShapeDtypeStruct((n_prompt, d), F32)
        kv_spec = pl.BlockSpec((tm, d), tl.prompt_part)
    prev_flat = [a for kv in prev_kv for a in kv] if finish_cache else []
    return pl.pallas_call(
        functools.partial(_qkv_kernel, n_prompt_tiles=tl.npt, stream_kind=stream[0], n_prev=n_prev),
        out_shape=(
            jax.ShapeDtypeStruct((t_tok, d), BF16),
            jax.ShapeDtypeStruct((t_tok, d), BF16),
            jax.ShapeDtypeStruct((t_tok, d), BF16),
            kv_shape, kv_shape,
        ) + ((row_out[0],) if pending else ()),
        grid=(t_tok // tm,),
        in_specs=tl.stream_specs(stream, t_tok // tm) + [
            pl.BlockSpec((None, N_MOD, d), tl.cond),
            pl.BlockSpec(w_qkv.shape, lambda t: (0, 0)),
            pl.BlockSpec((tm, LANES), tl.latent_pos),
            pl.BlockSpec((tm, LANES), tl.latent_pos),
        ] + [pl.BlockSpec((tm, d), tl.prompt_part) for _ in prev_flat],
        out_specs=(
            pl.BlockSpec((tm, d), tl.row),
            pl.BlockSpec((tm, d), tl.row),
            pl.BlockSpec((tm, d), tl.row),
            kv_spec, kv_spec,
        ) + ((row_out[1],) if pending else ()),
        compiler_params=_params(("arbitrary",)),
        name="ln_qkv_rope",
    )(*_stream_args(stream), mod, w_qkv, cos, sin, *prev_flat)


def _row_chains(n_rows):
    return [slice(r, r + CHAIN_ROWS) for r in range(0, n_rows, CHAIN_ROWS)]


def _chan_dft_kernel(*refs, n_prompt_tiles, stream_kind):
    n_x = _STREAM_REFS[stream_kind]
    x_refs, (mod_ref, dcs_ref, xc_ref, xs_ref) = refs[:n_x], refs[n_x:n_x + 4]
    g = FOURIER_GROUP_DIM
    is_prompt = pl.program_id(0) < n_prompt_tiles
    for rows in _row_chains(xc_ref.shape[0]):
        x = _stream_rows(stream_kind, x_refs, is_prompt, rows)
        if stream_kind == "pending":
            refs[-1][rows, :] = x
        h = (_layernorm(x) * (1.0 + mod_ref[1:2, :]) + mod_ref[0:1, :]).astype(BF16)
        for i in range(N_FOURIER_GROUPS):
            r = _dot(h[:, i * g:(i + 1) * g], dcs_ref[...])
            xc_ref[rows, i * g:(i + 1) * g] = r[:, :g].astype(BF16)
            xs_ref[rows, i * g:(i + 1) * g] = r[:, g:].astype(BF16)


def _chan_dft(stream, mod, dcs, n_prompt, dec_seq, t_tok):
    d = mod.shape[-1]
    tl = _Tiles(n_prompt, dec_seq, TOKEN_TILE)
    tm = tl.tm
    pending = stream[0] == "pending"
    half = jax.ShapeDtypeStruct((t_tok, d), BF16)
    return pl.pallas_call(
        functools.partial(_chan_dft_kernel, n_prompt_tiles=tl.npt, stream_kind=stream[0]),
        out_shape=(half, half) + ((jax.ShapeDtypeStruct((t_tok, d), F32),) if pending else ()),
        grid=(t_tok // tm,),
        in_specs=tl.stream_specs(stream, t_tok // tm) + [
            pl.BlockSpec((None, N_MOD, d), tl.cond),
            pl.BlockSpec(dcs.shape, lambda t: (0, 0)),
        ],
        out_specs=(pl.BlockSpec((tm, d), tl.row),) * (3 if pending else 2),
        compiler_params=_params(("arbitrary",)),
        name="ln_chan_dft",
    )(*_stream_args(stream), mod, dcs)


def _seq_dft_kernel(cs_ref, ss_ref, xc_ref, xs_ref, o_ref, *, norm):
    f = _dot(cs_ref[...], xc_ref[...]) - _dot(ss_ref[...], xs_ref[...])
    o_ref[...] = (f * norm).astype(BF16)


def _seq_dft(cs, ss, xc, xs, batch, seq, row_offset):
    d = xc.shape[1]
    tm = min(QKV_TILE, seq)
    spt = seq // tm
    off_seq = row_offset // seq
    return pl.pallas_call(
        functools.partial(_seq_dft_kernel, norm=1.0 / math.sqrt(seq * FOURIER_GROUP_DIM)),
        out_shape=jax.ShapeDtypeStruct((batch * seq, d), BF16),
        grid=(batch, spt),
        in_specs=[
            pl.BlockSpec((tm, seq), lambda b, i: (i, 0)),
            pl.BlockSpec((tm, seq), lambda b, i: (i, 0)),
            pl.BlockSpec((seq, d), lambda b, i: (off_seq + b, 0)),
            pl.BlockSpec((seq, d), lambda b, i: (off_seq + b, 0)),
        ],
        out_specs=pl.BlockSpec((tm, d), lambda b, i: (b * spt + i, 0)),
        compiler_params=_params(("arbitrary", "arbitrary")),
        name=f"seq_dft_{seq}",
    )(cs, ss, xc, xs)


def _seq_dft_mirror_kernel(cs_ref, ss_ref, xc_ref, xs_ref, lo_ref, hi_ref, *, norm):
    tm = lo_ref.shape[0]
    a = _dot(cs_ref[...], xc_ref[...])
    b = _dot(ss_ref[...], xs_ref[...])
    lo_ref[...] = ((a[:tm] - b[:tm]) * norm).astype(BF16)
    both = ((a + b) * norm).astype(BF16)
    r = lax.broadcasted_iota(I32, (tm, cs_ref.shape[0]), 0)
    c = lax.broadcasted_iota(I32, (tm, cs_ref.shape[0]), 1)
    flip = jnp.where(c == tm - r, 1.0, 0.0).astype(BF16)
    hi_ref[...] = _dot(flip, both).astype(BF16)


def _seq_dft_mirror(cs, ss, xc, xs, batch, seq, row_offset):
    d = xc.shape[1]
    tm = QKV_TILE
    half_tiles = seq // tm // 2
    off_seq = row_offset // seq
    half = jax.ShapeDtypeStruct((batch * seq // 2, d), BF16)
    return pl.pallas_call(
        functools.partial(_seq_dft_mirror_kernel, norm=1.0 / math.sqrt(seq * FOURIER_GROUP_DIM)),
        out_shape=(half, half),
        grid=(batch, half_tiles),
        in_specs=[
            pl.BlockSpec((pl.Element(tm + BF16_SUBLANES), pl.Element(seq)), lambda b, i: (i * tm, 0)),
            pl.BlockSpec((pl.Element(tm + BF16_SUBLANES), pl.Element(seq)), lambda b, i: (i * tm, 0)),
            pl.BlockSpec((seq, d), lambda b, i: (off_seq + b, 0)),
            pl.BlockSpec((seq, d), lambda b, i: (off_seq + b, 0)),
        ],
        out_specs=(pl.BlockSpec((tm, d), lambda b, i: (b * half_tiles + i, 0)),
                   pl.BlockSpec((tm, d), lambda b, i: (b * half_tiles + half_tiles - 1 - i, 0))),
        compiler_params=_params(("arbitrary", "arbitrary")),
        name=f"seq_dft_mirror_{seq}",
    )(cs, ss, xc, xs)


def _diff_lambda(lam_ref, lam_init):
    lv = lam_ref[...]
    return (jnp.exp(jnp.sum(lv[0:1] * lv[1:2], axis=-1, keepdims=True))
            - jnp.exp(jnp.sum(lv[2:3] * lv[3:4], axis=-1, keepdims=True)) + lam_init)


def _diff_attn_head(q, k, v_ext, lam, gain, lam_init):
    tq = q.shape[0]
    lane = lax.broadcasted_iota(I32, q.shape, 1)
    zero = jnp.zeros_like(q)
    qq = jnp.concatenate([jnp.where(lane < HEAD_DIM, q, zero), jnp.where(lane >= HEAD_DIM, q, zero)], axis=0)
    parts = []
    for r in range(0, 2 * tq, ATTN_ROW_CHUNK):
        s = lax.dot_general(qq[r:r + ATTN_ROW_CHUNK], k, (((1,), (1,)), ((), ())), preferred_element_type=F32)
        e = jnp.exp2(s - jnp.max(s, axis=-1, keepdims=True)).astype(BF16)
        parts.append(_dot(e, v_ext))
    oe = jnp.concatenate(parts, axis=0)
    o = oe[:, :V_DIM] / oe[:, V_DIM:]
    o = o[:tq] - lam * o[tq:]
    o = o * lax.rsqrt(jnp.mean(o * o, axis=-1, keepdims=True) + LN_EPS)
    return o * gain * (1.0 - lam_init)


def _attn_prompt_kernel(lam_ref, gain_ref, q_ref, k_ref, v_ref, o_ref, *, lam_init):
    lam = _diff_lambda(lam_ref, lam_init)
    gain = gain_ref[...]
    ones = jnp.ones((k_ref.shape[0], V_DIM), BF16)
    for hd in range(N_HEADS):
        cols = slice(hd * V_DIM, (hd + 1) * V_DIM)
        v_ext = jnp.concatenate([v_ref[:, cols], ones], axis=1)
        o_ref[:, cols] = _diff_attn_head(q_ref[:, cols], k_ref[:, cols], v_ext, lam, gain, lam_init).astype(BF16)


def _attn_latent_kernel(lam_ref, gain_ref, q_ref, k_ref, v_ref, kc_ref, vc_ref, o_ref, kall_ref, vext_ref, *,
                        lam_init):
    n_new = k_ref.shape[0]

    @pl.when(pl.program_id(2) == 0)
    def _():
        kall_ref[:n_new, :] = k_ref[...]
        kall_ref[n_new:, :] = kc_ref[...]
        vext_ref[:n_new, :V_DIM] = v_ref[...]
        vext_ref[n_new:, :V_DIM] = vc_ref[...]
        vext_ref[:, V_DIM:] = jnp.ones((vext_ref.shape[0], V_DIM), BF16)

    o = _diff_attn_head(q_ref[...], kall_ref[...], vext_ref[...], _diff_lambda(lam_ref, lam_init), gain_ref[...],
                        lam_init)
    o_ref[...] = o.astype(BF16)


def _attn_prompt(lam_vecs, gain, q, k, v, batch, seq, lam_init):
    d = q.shape[1]
    blk = pl.BlockSpec((seq, d), lambda b: (b, 0))
    return pl.pallas_call(
        functools.partial(_attn_prompt_kernel, lam_init=lam_init),
        out_shape=jax.ShapeDtypeStruct((batch * seq, d), BF16),
        grid=(batch,),
        in_specs=[
            pl.BlockSpec(lam_vecs.shape, lambda b: (0, 0)),
            pl.BlockSpec(gain.shape, lambda b: (0, 0)),
            blk, blk, blk,
        ],
        out_specs=blk,
        compiler_params=_params(("arbitrary",)),
        name="diff_attn_ctx",
    )(lam_vecs, gain, q, k, v)


def _attn_sample(lam_vecs, gain, q, k, v, kc, vc, batch, seq, n_ctx, row_offset, lam_init):
    d = q.shape[1]
    tq = Q_TILE
    qpt = seq // tq
    off_seq = row_offset // seq
    off_tile = row_offset // tq
    qmap = lambda b, h, i: (off_tile + b * qpt + i, h)
    kmap = lambda b, h, i: (off_seq + b, h)
    cmap = lambda b, h, i: (b, h)
    return pl.pallas_call(
        functools.partial(_attn_latent_kernel, lam_init=lam_init),
        out_shape=jax.ShapeDtypeStruct((batch * seq, d), BF16),
        grid=(batch, N_HEADS, qpt),
        in_specs=[
            pl.BlockSpec(lam_vecs.shape, lambda b, h, i: (0, 0)),
            pl.BlockSpec(gain.shape, lambda b, h, i: (0, 0)),
            pl.BlockSpec((tq, V_DIM), qmap),
            pl.BlockSpec((seq, V_DIM), kmap),
            pl.BlockSpec((seq, V_DIM), kmap),
            pl.BlockSpec((n_ctx, V_DIM), cmap),
            pl.BlockSpec((n_ctx, V_DIM), cmap),
        ],
        out_specs=pl.BlockSpec((tq, V_DIM), lambda b, h, i: (b * qpt + i, h)),
        scratch_shapes=[pltpu.VMEM((seq + n_ctx, V_DIM), BF16), pltpu.VMEM((seq + n_ctx, 2 * V_DIM), BF16)],
        compiler_params=_params(("arbitrary", "arbitrary", "arbitrary")),
        name="diff_attn_latent",
    )(lam_vecs, gain, q, k, v, kc, vc)


def _route(lg):
    lane = lax.broadcasted_iota(I32, lg.shape, 1)
    lane_f = lane.astype(F32)
    neg = jnp.float32(-jnp.inf)
    big = jnp.float32(LANES)
    gl = jnp.where(lane < N_EXPERT_GROUPS, lg, neg)
    gmax = jnp.max(gl, axis=-1, keepdims=True)
    g_prob = 1.0 / jnp.sum(jnp.exp(gl - gmax), axis=-1, keepdims=True)
    g_idx = jnp.min(jnp.where(gl == gmax, lane_f, big), axis=-1, keepdims=True)
    lo = N_EXPERT_GROUPS + EXPERTS_PER_GROUP * g_idx
    el = jnp.where((lane_f >= lo) & (lane_f < lo + EXPERTS_PER_GROUP), lg, neg)
    m1 = jnp.max(el, axis=-1, keepdims=True)
    i1 = jnp.min(jnp.where(el == m1, lane_f, big), axis=-1, keepdims=True)
    el2 = jnp.where(lane_f == i1, neg, el)
    m2 = jnp.max(el2, axis=-1, keepdims=True)
    i2 = jnp.min(jnp.where(el2 == m2, lane_f, big), axis=-1, keepdims=True)
    t = jnp.exp(m2 - m1)
    w1 = g_prob / (1.0 + t)
    w2 = g_prob * t / (1.0 + t)
    out = jnp.where(lane == 0, i1 - N_EXPERT_GROUPS, 0.0)
    out = jnp.where(lane == 1, i2 - N_EXPERT_GROUPS, out)
    out = jnp.where(lane == 2, w1, out)
    out = jnp.where(lane == 3, w2, out)
    return out


def _pack_bf16_pairs(h):
    n = h.shape[1] // 2
    bits = lax.bitcast_convert_type(h.astype(BF16).astype(F32), jnp.uint32)
    return (bits[:, :n] >> 16) | bits[:, n:]


def _unpack_bf16_pairs(p):
    lo = lax.bitcast_convert_type(p << 16, F32)
    hi = lax.bitcast_convert_type(p & jnp.uint32(0xFFFF0000), F32)
    return jnp.concatenate([lo, hi], axis=1)


def _mix_out_kernel(*refs, n_prompt_tiles, tiles_per_seq, n_mixed, stream_kind):
    n_x = _STREAM_REFS[stream_kind]
    a_refs, w_ref = refs[:n_mixed], refs[n_mixed]
    x_refs = refs[n_mixed + 1:n_mixed + 1 + n_x]
    mod_ref, gb_ref, wr_ref, x1_ref, hp_ref, route_ref = refs[n_mixed + 1 + n_x:]
    t = pl.program_id(0)
    is_prompt = t < n_prompt_tiles
    first_half = (t - n_prompt_tiles) % tiles_per_seq < tiles_per_seq // 2
    for rows in _row_chains(x1_ref.shape[0]):
        a = a_refs[1][rows, :] if n_mixed == 2 else jnp.where(first_half, a_refs[1][rows, :], a_refs[2][rows, :])
        a = jnp.where(is_prompt, a_refs[0][rows, :], a)
        out = _dot(a, w_ref[...])
        x = _stream_rows(stream_kind, x_refs, is_prompt, rows)
        x1 = _deepnorm(x, mod_ref[2:3, :], out) * gb_ref[0:1, :] + gb_ref[1:2, :]
        x1_ref[rows, :] = x1
        h2 = _layernorm(x1) * (1.0 + mod_ref[4:5, :]) + mod_ref[3:4, :]
        hp_ref[rows, :] = _pack_bf16_pairs(h2)
        route_ref[rows, :] = _route(_dot(h2.astype(BF16), wr_ref[...]))


def _mix_out(mixed, w, stream, mod, gb, wr, n_prompt, dec_seq, t_tok):
    d = w.shape[1]
    tl = _Tiles(n_prompt, dec_seq, TOKEN_TILE)
    tm = tl.tm
    if len(mixed) == 2:
        latent_specs = [pl.BlockSpec((tm, d), tl.latent_part)]
    else:
        half = tl.tps // 2

        def half_map(second):
            def index(t):
                lt = jnp.maximum(t - tl.npt, 0)
                within = lt % tl.tps - second * half
                return (lt // tl.tps * half + jnp.clip(within, 0, half - 1), 0)
            return index
        latent_specs = [pl.BlockSpec((tm, d), half_map(0)), pl.BlockSpec((tm, d), half_map(1))]
    return pl.pallas_call(
        functools.partial(_mix_out_kernel, n_prompt_tiles=tl.npt, tiles_per_seq=tl.tps, n_mixed=len(mixed),
                          stream_kind=stream[0]),
        out_shape=(
            jax.ShapeDtypeStruct((t_tok, d), F32),
            jax.ShapeDtypeStruct((t_tok, d // 2), jnp.uint32),
            jax.ShapeDtypeStruct((t_tok, LANES), F32),
        ),
        grid=(t_tok // tm,),
        in_specs=[pl.BlockSpec((tm, d), tl.prompt_part)] + latent_specs + [
            pl.BlockSpec(w.shape, lambda t: (0, 0)),
        ] + tl.stream_specs(stream, t_tok // tm) + [
            pl.BlockSpec((None, N_MOD, d), tl.cond),
            pl.BlockSpec(gb.shape, lambda t: (0, 0)),
            pl.BlockSpec(wr.shape, lambda t: (0, 0)),
        ],
        out_specs=(
            pl.BlockSpec((tm, d), tl.row),
            pl.BlockSpec((tm, d // 2), tl.row),
            pl.BlockSpec((tm, LANES), tl.row),
        ),
        compiler_params=_params(("arbitrary",)),
        name="mix_out_postnorm_router",
    )(*mixed, w, *_stream_args(stream), mod, gb, wr)


def _lane_prefix_sum(x, lane):
    sh = 1
    while sh < LANES:
        x = x + jnp.where(lane >= sh, pltpu.roll(x, sh, 1), 0.0)
        sh *= 2
    return x


def _lane_suffix_min_exclusive(x, lane):
    big = float(LANES)
    y = jnp.where(lane + 1 < LANES, pltpu.roll(x, LANES - 1, 1), big)
    sh = 1
    while sh < LANES:
        y = jnp.minimum(y, jnp.where(lane + sh < LANES, pltpu.roll(y, LANES - sh, 1), big))
        sh *= 2
    return y


def _work_items(counts, starts, ends, lane8, n_moe_tiles):
    shift = int(math.log2(MOE_TILE))
    first_tile = (starts.astype(I32) >> shift).astype(F32)
    last_tile = ((ends.astype(I32) - 1) >> shift).astype(F32)
    n_it = jnp.where(counts > 0.0, last_tile - first_tile + 1.0, 0.0)
    it_end = _lane_prefix_sum(n_it, lane8)
    it_start = it_end - n_it
    total = it_end[0:1, LANES - 1:LANES]
    sub = lax.broadcasted_iota(I32, (LANES, LANES), 0)
    rows = lambda x: jnp.broadcast_to(x[0:1, :], (LANES, LANES))
    used = counts > 0.0
    weight_slot = ((_lane_prefix_sum(jnp.where(used, 1.0, 0.0), lane8) - 1.0).astype(I32) & 1).astype(F32)
    next_used = _lane_suffix_min_exclusive(jnp.where(used, lane8.astype(F32), float(LANES)), lane8)
    per_expert = (starts, ends, first_tile, it_start, it_end, weight_slot, next_used)
    stacked = jnp.zeros((LANES, LANES), F32)
    for j, vec in enumerate(per_expert):
        stacked = jnp.where(sub == j, rows(vec), stacked)
    cols = stacked.T
    col = lambda j: cols[:, j:j + 1]
    sub_f = sub.astype(F32)
    w = lax.broadcasted_iota(I32, (LANES, LANES), 1).astype(F32)
    ex = jnp.sum(jnp.where((sub < N_EXPERTS) & (col(4) <= w), 1.0, 0.0), axis=0, keepdims=True)
    ex = jnp.minimum(ex, N_EXPERTS - 1.0)
    w1 = w[0:1, :]
    valid = w1 < total
    ex = jnp.where(valid, ex, jnp.max(jnp.where(valid, ex, 0.0), axis=-1, keepdims=True))
    onehot = sub_f == ex
    pick = lambda j: jnp.sum(jnp.where(onehot, col(j), 0.0), axis=0, keepdims=True)
    tile = jnp.where(valid, pick(2) + (w1 - pick(3)), n_moe_tiles - 1.0)
    lo = jnp.where(valid, jnp.maximum(pick(0), tile * MOE_TILE), 0.0)
    hi = jnp.where(valid, jnp.minimum(pick(1), (tile + 1.0) * MOE_TILE), 0.0)
    b8 = lambda x: jnp.broadcast_to(x, (SUBLANES, LANES))
    ex8, tile8 = b8(ex), b8(tile)
    first = jnp.where((lane8 == 0) | (tile8 != pltpu.roll(tile8, 1, 1)), 1.0, 0.0)
    newexp = jnp.where((lane8 == 0) | (ex8 != pltpu.roll(ex8, 1, 1)), 1.0, 0.0)
    sub8 = lax.broadcasted_iota(I32, (SUBLANES, LANES), 0)
    table = jnp.zeros((SUBLANES, LANES), F32)
    for j, vec in enumerate((ex8, tile8, b8(lo), b8(hi), first, newexp, b8(pick(5)), b8(pick(6)))):
        table = jnp.where(sub8 == j, vec, table)
    return table


def _plan_kernel(route_ref, pos_ref, items_ref, tri_ref, carry_ref, tot_ref, *, n_moe_tiles):
    p = pl.program_id(0)
    t = pl.program_id(1)
    tm = route_ref.shape[0]
    r = route_ref[...]
    lane = lax.broadcasted_iota(I32, (tm, LANES), 1)
    lane_f = lane.astype(F32)
    e0 = r[:, 0:1]
    e1 = r[:, 1:2] + N_EXPERTS
    m = jnp.where((lane_f == e0) | (lane_f == e1), 1.0, 0.0)
    colsum = jnp.sum(m, axis=0, keepdims=True)

    @pl.when((p == 0) & (t == 0))
    def _():
        carry_ref[...] = jnp.zeros_like(carry_ref)
        row = lax.broadcasted_iota(I32, (tm, tm), 0)
        col = lax.broadcasted_iota(I32, (tm, tm), 1)
        tri_ref[...] = jnp.where(row > col, 1.0, 0.0).astype(BF16)

    @pl.when(p == 0)
    def _():
        carry_ref[...] += colsum

    @pl.when((p == 0) & (t == pl.num_programs(1) - 1))
    def _():
        tot_ref[...] = carry_ref[...]
        carry_ref[...] = jnp.zeros_like(carry_ref)

    @pl.when(p == 1)
    def _():
        lane8 = lax.broadcasted_iota(I32, (SUBLANES, LANES), 1)
        tot = tot_ref[...]
        is_first = lane8 < N_EXPERTS
        tot0 = jnp.where(is_first, tot, 0.0)
        counts = jnp.where(is_first, tot + pltpu.roll(tot, LANES - N_EXPERTS, 1), 0.0)
        ends = _lane_prefix_sum(counts, lane8)
        starts = ends - counts
        base = jnp.where(is_first, starts, pltpu.roll(starts + tot0, N_EXPERTS, 1))
        before = _dot(tri_ref[...], m.astype(BF16)) + carry_ref[0:1, :]
        carry_ref[...] += colsum
        rows = before + base[0:1, :]
        pos0 = jnp.sum(jnp.where(lane_f == e0, rows, 0.0), axis=-1, keepdims=True)
        pos1 = jnp.sum(jnp.where(lane_f == e1, rows, 0.0), axis=-1, keepdims=True)
        both = jnp.where(lane == 0, pos0, jnp.where(lane == 1, pos1, 0.0))
        pos_ref[...] = both.T[0:SUBLANES, :].astype(I32)

        @pl.when(t == 0)
        def _():
            items_ref[...] = _work_items(counts, starts, ends, lane8, n_moe_tiles).astype(I32)


def _moe_plan(route):
    t_tok = route.shape[0]
    tm = PLAN_TILE
    n_moe_tiles = 2 * t_tok // MOE_TILE
    assert n_moe_tiles + N_EXPERTS - 1 <= LANES and 2 * N_EXPERTS <= LANES and MOE_TILE & (MOE_TILE - 1) == 0
    return pl.pallas_call(
        functools.partial(_plan_kernel, n_moe_tiles=n_moe_tiles),
        out_shape=(jax.ShapeDtypeStruct((SUBLANES, t_tok), I32), jax.ShapeDtypeStruct((SUBLANES, LANES), I32)),
        grid=(2, t_tok // tm),
        in_specs=[pl.BlockSpec((tm, LANES), lambda p, t: (t, 0))],
        out_specs=(pl.BlockSpec((SUBLANES, tm), lambda p, t: (0, t * p)),
                   pl.BlockSpec((SUBLANES, LANES), lambda p, t: (0, 0))),
        scratch_shapes=[pltpu.VMEM((tm, tm), BF16), pltpu.VMEM((SUBLANES, LANES), F32),
                        pltpu.VMEM((SUBLANES, LANES), F32)],
        compiler_params=_params(("arbitrary", "arbitrary")),
        name="moe_positions",
    )(route)


def _sc_workers():
    info = plsc.get_sparse_core_info()
    return info.num_cores, info.num_cores * info.num_subcores


def _sc_pipeline(n_chunks, loads, stores):
    def start(copies):
        for cp in copies:
            cp.start()

    def wait(copies):
        for cp in copies:
            cp.wait()

    start(loads(0, 0))
    for j in range(n_chunks):
        b = j % 2
        if j + 1 < n_chunks:
            if j >= 1:
                wait(stores(j - 1, 1 - b))
            start(loads(j + 1, 1 - b))
        wait(loads(j, b))
        start(stores(j, b))
    if n_chunks >= 2:
        wait(stores(n_chunks - 2, n_chunks % 2))
    wait(stores(n_chunks - 1, (n_chunks - 1) % 2))


def _sc_chunking(n_rows, d):
    n_cores, n_workers = _sc_workers()
    per_worker = n_rows // n_workers
    k = SC_BUFFER_BYTES // (d * 4)
    n_chunks = per_worker // k
    assert n_chunks * k * n_workers == n_rows and k <= SC_MAX_INDEX_CHUNK
    return n_cores, n_workers, per_worker, k, n_chunks


def _sc_gather_rows(table, idx):
    n, d = idx.shape[0], table.shape[1]
    n_cores, n_workers, per_worker, k, n_chunks = _sc_chunking(n, d)

    def body(table_hbm, idx_hbm, out_hbm, idx_v, rows_v, gsem, osem):
        wid = lax.axis_index("s") * n_cores + lax.axis_index("c")
        base = wid * per_worker
        pltpu.sync_copy(idx_hbm.at[wid], idx_v)
        _sc_pipeline(
            n_chunks,
            lambda j, b: [pltpu.make_async_copy(table_hbm.at[idx_v.at[j]], rows_v.at[b], gsem.at[b])],
            lambda j, b: [pltpu.make_async_copy(rows_v.at[b], out_hbm.at[pl.ds(base + j * k, k)], osem.at[b])])

    return pl.kernel(
        body,
        out_type=jax.ShapeDtypeStruct((n, d), table.dtype),
        mesh=plsc.VectorSubcoreMesh(core_axis_name="c", subcore_axis_name="s"),
        scratch_types=[pltpu.VMEM((n_chunks, k), I32), pltpu.VMEM((2, k, d), table.dtype),
                       pltpu.SemaphoreType.DMA((2,)), pltpu.SemaphoreType.DMA((2,))],
        name="sc_row_gather",
    )(table, idx.reshape(n_workers, n_chunks, k))


def _sc_scatter_rows(src, idx):
    n_lists, n_src = idx.shape
    d = src.shape[1]
    n_cores, n_workers, per_worker, k, n_chunks = _sc_chunking(n_src, d)

    def body(src_hbm, idx_hbm, out_hbm, idx_v, rows_v, gsem, osem):
        wid = lax.axis_index("s") * n_cores + lax.axis_index("c")
        base = wid * per_worker
        pltpu.sync_copy(idx_hbm.at[wid], idx_v)
        _sc_pipeline(
            n_chunks,
            lambda j, b: [pltpu.make_async_copy(src_hbm.at[pl.ds(base + j * k, k)], rows_v.at[b], gsem.at[b])],
            lambda j, b: [pltpu.make_async_copy(rows_v.at[b], out_hbm.at[idx_v.at[c, j]], osem.at[b, c])
                          for c in range(n_lists)])

    return pl.kernel(
        body,
        out_type=jax.ShapeDtypeStruct((n_lists * n_src, d), src.dtype),
        mesh=plsc.VectorSubcoreMesh(core_axis_name="c", subcore_axis_name="s"),
        scratch_types=[pltpu.VMEM((n_lists, n_chunks, k), I32), pltpu.VMEM((2, k, d), src.dtype),
                       pltpu.SemaphoreType.DMA((2,)), pltpu.SemaphoreType.DMA((2, n_lists))],
        name="sc_row_scatter",
    )(src, idx.reshape(n_lists, n_workers, n_chunks, k).transpose(1, 0, 2, 3))


IT_EXPERT, IT_TILE, IT_LO, IT_HI, IT_FIRST, IT_NEWEXP, IT_SLOT, IT_NEXT = range(8)


def _moe_kernel(it_ref, xs_ref, wg_hbm, wu_hbm, wd_hbm, o_ref, wg_f, wu_f, wd_f, sem, *, layer):
    w = pl.program_id(0)

    def weight_copies(expert, slot):
        return [pltpu.make_async_copy(src.at[layer, expert], dst.at[slot], sem.at[slot, j])
                for j, (src, dst) in enumerate(((wg_hbm, wg_f), (wu_hbm, wu_f), (wd_hbm, wd_f)))]

    @pl.when(it_ref[IT_NEWEXP, w] == 1)
    def _():
        expert = it_ref[IT_EXPERT, w]
        slot = it_ref[IT_SLOT, w]
        nxt = it_ref[IT_NEXT, w]

        @pl.when(w == 0)
        def _():
            for cp in weight_copies(expert, slot):
                cp.start()

        for cp in weight_copies(expert, slot):
            cp.wait()

        @pl.when(nxt < N_EXPERTS)
        def _():
            for cp in weight_copies(nxt, 1 - slot):
                cp.start()

    slot = it_ref[IT_SLOT, w]
    tm = xs_ref.shape[0]
    x = _unpack_bf16_pairs(xs_ref[...]).astype(BF16)
    a = _dot(x, wg_f[slot].astype(BF16))
    u = _dot(x, wu_f[slot].astype(BF16))
    act = (_silu(a) * u).astype(BF16)
    y = _pack_bf16_pairs(_dot(act, wd_f[slot].astype(BF16)))
    row = it_ref[IT_TILE, w] * tm + lax.broadcasted_iota(I32, (tm, 1), 0)
    mine = (row >= it_ref[IT_LO, w]) & (row < it_ref[IT_HI, w])

    @pl.when(it_ref[IT_FIRST, w] == 1)
    def _():
        o_ref[...] = jnp.where(mine, y, jnp.zeros_like(y))

    @pl.when(it_ref[IT_FIRST, w] == 0)
    def _():
        o_ref[...] = jnp.where(mine, y, o_ref[...])


def _moe_experts(items, xs, w_gate, w_up, w_down, layer):
    n_rows, dp = xs.shape
    _, _, d, f = w_gate.shape
    tm = MOE_TILE
    rmap = lambda w, it: (it[IT_TILE, w], 0)
    hbm = pl.BlockSpec(memory_space=pl.ANY)
    return pl.pallas_call(
        functools.partial(_moe_kernel, layer=layer),
        out_shape=jax.ShapeDtypeStruct((n_rows, d // 2), jnp.uint32),
        grid_spec=pltpu.PrefetchScalarGridSpec(
            num_scalar_prefetch=1,
            grid=(n_rows // tm + N_EXPERTS - 1,),
            in_specs=[pl.BlockSpec((tm, dp), rmap), hbm, hbm, hbm],
            out_specs=pl.BlockSpec((tm, d // 2), rmap),
            scratch_shapes=[
                pltpu.VMEM((2, d, f), F32), pltpu.VMEM((2, d, f), F32), pltpu.VMEM((2, f, d), F32),
                pltpu.SemaphoreType.DMA((2, 3)),
            ],
        ),
        compiler_params=_params(("arbitrary",)),
        name="moe_grouped_mlp",
    )(items, xs, w_gate, w_up, w_down)


def _final_postnorm_kernel(*refs, n_prompt_tiles):
    x2 = _finish_moe(*refs[:6])
    out_prompt, out_latent = refs[6:]

    @pl.when(pl.program_id(0) < n_prompt_tiles)
    def _():
        out_prompt[...] = x2

    @pl.when(pl.program_id(0) >= n_prompt_tiles)
    def _():
        out_latent[...] = x2


def _final_postnorm(stream, n_prompt, dec_seq, t_tok):
    d = stream[1][0].shape[1]
    tl = _Tiles(n_prompt, dec_seq, TOKEN_TILE)
    tm = tl.tm
    return pl.pallas_call(
        functools.partial(_final_postnorm_kernel, n_prompt_tiles=tl.npt),
        out_shape=(jax.ShapeDtypeStruct((n_prompt, d), F32), jax.ShapeDtypeStruct((t_tok - n_prompt, d), F32)),
        grid=(t_tok // tm,),
        in_specs=tl.stream_specs(stream, t_tok // tm),
        out_specs=(pl.BlockSpec((tm, d), tl.prompt_part), pl.BlockSpec((tm, d), tl.latent_part)),
        compiler_params=_params(("arbitrary",)),
        name="moe_combine_postnorm",
    )(*_stream_args(stream))


def _rope_tables(n_lat):
    nf = ROPE_FREQS
    s = np.arange(n_lat)
    lane = np.arange(LANES)
    inv = ROPE_BASE ** (-(lane % nf).astype(np.float64) / nf)
    use_col = (lane % HEAD_DIM) >= HEAD_DIM // 2
    p = np.where(use_col[None, :], (s % GRID_W)[:, None], (s // GRID_W)[:, None]).astype(np.float64)
    ang = p * inv[None, :]
    sign = np.where((lane % (2 * nf)) < nf, -1.0, 1.0)
    return jnp.asarray(np.cos(ang), F32), jnp.asarray(np.sin(ang) * sign[None, :], F32)


def _dft_tables(n):
    k = np.arange(n)
    ang = 2.0 * np.pi * ((k[:, None] * k[None, :]) % n).astype(np.float64) / n
    return np.cos(ang), np.sin(ang)


def kernel(x_prompt, x_sample, cache_k, cache_v, c, c_ctx, w_ada, b_ada, ln_gain, ln_bias, w_qkv, w_attn_out,
           lambda_q1, lambda_k1, lambda_q2, lambda_k2, subln_gain, w_fourier_out, w_router_group,
           w_router_expert, w_expert_gate, w_expert_up, w_expert_down):
    bp, sp, d = x_prompt.shape
    bs, n_lat, _ = x_sample.shape
    n_ctx = cache_k.shape[2]
    n_prompt = bp * sp
    t_tok = n_prompt + bs * n_lat
    assert d == D_MODEL and n_prompt % n_lat == 0 and n_lat % TOKEN_TILE == 0 and sp == QKV_TILE

    cond = jnp.concatenate([c_ctx[None, :], c, jnp.zeros((SUBLANES - 1 - bs, d), F32)], axis=0)
    mods = [_ada_layer(cond, w_ada, b_ada, i).reshape(SUBLANES, N_MOD, d) for i in range(DEPTH)]

    cos, sin = _rope_tables(n_lat)
    cc, sc = _dft_tables(FOURIER_GROUP_DIM)
    dcs = jnp.asarray(np.concatenate([cc, sc], axis=1), BF16)
    seq_tabs = {s: tuple(jnp.asarray(m, BF16) for m in _dft_tables(s)) for s in (sp, n_lat)}
    attn_layers = [i for i in range(DEPTH) if i % 2 == 0]

    stream = ("pair", (x_prompt.reshape(n_prompt, d), x_sample.reshape(bs * n_lat, d)))
    prev_kv = []
    new_k = new_v = None
    for i in range(DEPTH):
        mod = mods[i]
        if i % 2 == 0:
            a = i // 2
            lam_init = 0.8 - 0.6 * math.exp(-0.3 * i)
            last_attn = i == attn_layers[-1]
            q, k, v, kf, vf, *finished = _qkv(stream, mod, w_qkv[a].astype(BF16), cos, sin, prev_kv, last_attn,
                                              n_prompt, n_lat, t_tok)
            if last_attn:
                new_k, new_v = kf, vf
            else:
                prev_kv.append((kf, vf))
            lam_vecs = jnp.stack([lambda_q1[a], lambda_k1[a], lambda_q2[a], lambda_k2[a]], axis=0)
            gain = subln_gain[a][None, :]
            kc = cache_k[:, a].reshape(bs * n_ctx, d).astype(BF16)
            vc = cache_v[:, a].reshape(bs * n_ctx, d).astype(BF16)
            mixed = (_attn_prompt(lam_vecs, gain, q, k, v, bp, sp, lam_init),
                     _attn_sample(lam_vecs, gain, q, k, v, kc, vc, bs, n_lat, n_ctx, n_prompt, lam_init))
            w_mix = w_attn_out[a]
        else:
            xc, xsn, *finished = _chan_dft(stream, mod, dcs, n_prompt, n_lat, t_tok)
            mixed = (_seq_dft(*seq_tabs[sp], xc, xsn, bp, sp, 0),
                     *_seq_dft_mirror(*seq_tabs[n_lat], xc, xsn, bs, n_lat, n_prompt))
            w_mix = w_fourier_out[i // 2]
        if finished:
            stream = ("merged", (finished[0],))
        gb0 = jnp.stack([ln_gain[i, 0], ln_bias[i, 0]], axis=0)
        gb1 = jnp.stack([ln_gain[i, 1], ln_bias[i, 1]], axis=0)
        wr = jnp.concatenate([w_router_group[i], w_router_expert[i],
                              jnp.zeros((d, LANES - N_EXPERT_GROUPS - N_EXPERTS), F32)], axis=1)
        x1, hp, route = _mix_out(mixed, w_mix.astype(BF16), stream, mod, gb0, wr.astype(BF16), n_prompt, n_lat,
                                 t_tok)
        pos8, items = _moe_plan(route)
        pos = pos8[0:2]
        xs_sorted = _sc_scatter_rows(hp, pos)
        ys = _moe_experts(items, xs_sorted, w_expert_gate, w_expert_up, w_expert_down, i)
        yg = _sc_gather_rows(ys, pos.reshape(-1))
        stream = ("pending", (x1, yg, route, mods[i], gb1))

    out_prompt, out_latent = _final_postnorm(stream, n_prompt, n_lat, t_tok)
    y_prompt = out_prompt.reshape(bp, sp, d)
    y_sample = out_latent.reshape(bs, n_lat, d)
    return (y_prompt, y_sample, new_k.reshape(bp, len(attn_layers), sp, N_HEADS, 2 * HEAD_DIM),
            new_v.reshape(bp, len(attn_layers), sp, N_HEADS, V_DIM))
```

```python
import functools
import math

import numpy as np
import jax
import jax.numpy as jnp
from jax import lax
from jax.experimental import pallas as pl
from jax.experimental.pallas import tpu as pltpu
from jax.experimental.pallas import tpu_sc as plsc

F32 = jnp.float32
BF16 = jnp.bfloat16
I32 = jnp.int32

D_MODEL = 1024
DEPTH = 4
GRID_W = 64
N_HEADS = 8
HEAD_DIM = 64
V_DIM = 2 * HEAD_DIM
ROPE_BASE = 10000.0
ROPE_FREQS = HEAD_DIM // 4
N_FOURIER_GROUPS = 8
FOURIER_GROUP_DIM = D_MODEL // N_FOURIER_GROUPS
N_EXPERT_GROUPS = 4
EXPERTS_PER_GROUP = 8
N_EXPERTS = N_EXPERT_GROUPS * EXPERTS_PER_GROUP
N_MOD = 6
LN_EPS = 1e-5
DEEPNORM_ALPHA = (2.0 * DEPTH) ** 0.25
Q_SCALE = math.log2(math.e) * HEAD_DIM ** -0.5

LANES = 128
SUBLANES = 8
BF16_SUBLANES = 16
TOKEN_TILE = 1024
CHAIN_ROWS = 256
QKV_TILE = 256
Q_TILE = 2048
ATTN_ROW_CHUNK = 128
MOE_TILE = 512
PLAN_TILE = 1024
ADA_COLUMNS = 1536
SC_BUFFER_BYTES = 128 * 1024
SC_MAX_INDEX_CHUNK = 128
VMEM_LIMIT = 48 * 1024 * 1024


def _params(semantics):
    return pltpu.CompilerParams(dimension_semantics=semantics, vmem_limit_bytes=VMEM_LIMIT)


def _layernorm(x, eps=LN_EPS):
    mu = jnp.mean(x, axis=-1, keepdims=True)
    xc = x - mu
    var = jnp.mean(xc * xc, axis=-1, keepdims=True)
    return xc * lax.rsqrt(var + eps)


def _deepnorm(x, branch_gate, branch):
    return _layernorm(x + (branch_gate * (1.0 / DEEPNORM_ALPHA)) * branch, LN_EPS / DEEPNORM_ALPHA ** 2)


def _silu(a):
    return a / (1.0 + jnp.exp(-a))


def _dot(a, b):
    return jnp.dot(a, b, preferred_element_type=F32)


class _Tiles:
    def __init__(self, n_prompt, dec_seq, tm):
        self.tm = tm
        self.npt = n_prompt // tm
        self.tps = dec_seq // tm

    def row(self, t):
        return (t, 0)

    def cond(self, t):
        return (jnp.where(t < self.npt, 0, (t - self.npt) // self.tps + 1), 0, 0)

    def latent_pos(self, t):
        return (jnp.maximum(t - self.npt, 0) % self.tps, 0)

    def prompt_part(self, t):
        return (jnp.minimum(t, self.npt - 1), 0)

    def latent_part(self, t):
        return (jnp.maximum(t - self.npt, 0), 0)

    def stream_specs(self, stream, n_tiles):
        kind, arrays = stream
        tm = self.tm
        if kind == "merged":
            return [pl.BlockSpec((tm, arrays[0].shape[1]), self.row)]
        if kind == "pair":
            return [pl.BlockSpec((tm, arrays[0].shape[1]), self.prompt_part),
                    pl.BlockSpec((tm, arrays[1].shape[1]), self.latent_part)]
        x1, yg, route, mod, gb = arrays
        return [pl.BlockSpec((tm, x1.shape[1]), self.row),
                pl.BlockSpec((tm, yg.shape[1]), self.row),
                pl.BlockSpec((tm, yg.shape[1]), lambda t: (t + n_tiles, 0)),
                pl.BlockSpec((tm, route.shape[1]), self.row),
                pl.BlockSpec((None,) + mod.shape[1:], self.cond),
                pl.BlockSpec(gb.shape, lambda t: (0, 0))]


_STREAM_REFS = {"merged": 1, "pair": 2, "pending": 6}


def _stream_args(stream):
    kind, arrays = stream
    if kind == "pending":
        x1, yg, route, mod, gb = arrays
        return [x1, yg, yg, route, mod, gb]
    return list(arrays)


def _finish_moe(x1_ref, y0_ref, y1_ref, route_ref, mod_ref, gb_ref, rows=slice(None)):
    r = route_ref[rows, :]
    moe = r[:, 2:3] * _unpack_bf16_pairs(y0_ref[rows, :]) + r[:, 3:4] * _unpack_bf16_pairs(y1_ref[rows, :])
    return _deepnorm(x1_ref[rows, :], mod_ref[5:6, :], moe) * gb_ref[0:1, :] + gb_ref[1:2, :]


def _stream_rows(kind, refs, is_prompt_tile, rows=slice(None)):
    if kind == "merged":
        return refs[0][rows, :]
    if kind == "pair":
        return jnp.where(is_prompt_tile, refs[0][rows, :], refs[1][rows, :])
    return _finish_moe(*refs, rows)


def _ada_kernel(cond_ref, w_ref, b_ref, o_ref):
    a = _silu(cond_ref[...])
    o_ref[...] = _dot(a.astype(BF16), w_ref[...].astype(BF16)) + b_ref[...]


def _ada_layer(cond, w_ada, b_ada, layer):
    depth, d, n = w_ada.shape
    tn = ADA_COLUMNS
    return pl.pallas_call(
        _ada_kernel,
        out_shape=jax.ShapeDtypeStruct((cond.shape[0], n), F32),
        grid=(n // tn,),
        in_specs=[
            pl.BlockSpec(cond.shape, lambda j: (0, 0)),
            pl.BlockSpec((None, d, tn), lambda j: (layer, 0, j)),
            pl.BlockSpec((None, 1, tn), lambda j: (layer, 0, j)),
        ],
        out_specs=pl.BlockSpec((cond.shape[0], tn), lambda j: (0, j)),
        compiler_params=_params(("arbitrary",)),
        name="ada",
    )(cond, w_ada, b_ada.reshape(depth, 1, n))


def _rope(x, cos, sin_signed, first_half):
    outs = []
    for c in range(x.shape[1] // LANES):
        xc = x[:, c * LANES:(c + 1) * LANES]
        partner = jnp.where(first_half, pltpu.roll(xc, LANES - ROPE_FREQS, 1), pltpu.roll(xc, ROPE_FREQS, 1))
        outs.append(xc * cos + partner * sin_signed)
    return jnp.concatenate(outs, axis=1)


def _store_heads(cache_ref, slot, rows):
    cache_ref[slot] = pltpu.einshape("s(hd)->shd", rows, h=N_HEADS)


def _qkv_kernel(*refs, n_prompt_tiles, stream_kind, n_prev):
    n_x = _STREAM_REFS[stream_kind]
    x_refs, (mod_ref, w_ref, cos_ref, sin_ref) = refs[:n_x], refs[n_x:n_x + 4]
    prev = refs[n_x + 4:n_x + 4 + 2 * max(n_prev, 0)]
    q_ref, k_ref, v_ref, ko_ref, vo_ref = refs[n_x + 4 + 2 * max(n_prev, 0):][:5]
    t = pl.program_id(0)
    d = w_ref.shape[0]

    def projector(is_prompt):
        x = _stream_rows(stream_kind, x_refs, is_prompt)
        if stream_kind == "pending":
            refs[-1][...] = x
        h = (_layernorm(x) * (1.0 + mod_ref[1:2, :]) + mod_ref[0:1, :]).astype(BF16)
        return lambda j: _dot(h, w_ref[:, j * d:(j + 1) * d])

    @pl.when(t < n_prompt_tiles)
    def _():
        project = projector(True)
        q_ref[...] = (project(0) * Q_SCALE).astype(BF16)
        k = project(1)
        k_ref[...] = k.astype(BF16)
        v = project(2)
        v_ref[...] = v.astype(BF16)
        if n_prev < 0:
            ko_ref[...] = k
            vo_ref[...] = v
        else:
            for a in range(n_prev):
                _store_heads(ko_ref, a, prev[2 * a][...])
                _store_heads(vo_ref, a, prev[2 * a + 1][...])
            _store_heads(ko_ref, n_prev, k)
            _store_heads(vo_ref, n_prev, v)

    @pl.when(t >= n_prompt_tiles)
    def _():
        project = projector(False)
        lane = lax.broadcasted_iota(I32, (q_ref.shape[0], LANES), 1)
        first_half = (lane % (2 * ROPE_FREQS)) < ROPE_FREQS
        cos = cos_ref[...]
        sin = sin_ref[...]
        q_ref[...] = _rope(project(0) * Q_SCALE, cos, sin, first_half).astype(BF16)
        k_ref[...] = _rope(project(1), cos, sin, first_half).astype(BF16)
        v_ref[...] = project(2).astype(BF16)


def _qkv(stream, mod, w_qkv, cos, sin, prev_kv, finish_cache, n_prompt, dec_seq, t_tok):
    d = w_qkv.shape[0]
    tl = _Tiles(n_prompt, dec_seq, QKV_TILE)
    tm = tl.tm
    pending = stream[0] == "pending"
    row_out = jax.ShapeDtypeStruct((t_tok, d), F32), pl.BlockSpec((tm, d), tl.row)
    n_prev = len(prev_kv) if finish_cache else -1
    if finish_cache:
        n_slots = n_prev + 1
        kv_shape = jax.ShapeDtypeStruct((n_prompt // tm, n_slots, tm, N_HEADS, V_DIM), F32)
        kv_spec = pl.BlockSpec((None, n_slots, tm, N_HEADS, V_DIM), lambda t: (tl.prompt_part(t)[0], 0, 0, 0, 0))
    else:
        kv_shape = jax.ShapeDtypeStruct((n_prompt, d), F32)
        kv_spec = pl.BlockSpec((tm, d), tl.prompt_part)
    prev_flat = [a for kv in prev_kv for a in kv] if finish_cache else []
    return pl.pallas_call(
        functools.partial(_qkv_kernel, n_prompt_tiles=tl.npt, stream_kind=stream[0], n_prev=n_prev),
        out_shape=(
            jax.ShapeDtypeStruct((t_tok, d), BF16),
            jax.ShapeDtypeStruct((t_tok, d), BF16),
            jax.ShapeDtypeStruct((t_tok, d), BF16),
            kv_shape, kv_shape,
        ) + ((row_out[0],) if pending else ()),
        grid=(t_tok // tm,),
        in_specs=tl.stream_specs(stream, t_tok // tm) + [
            pl.BlockSpec((None, N_MOD, d), tl.cond),
            pl.BlockSpec(w_qkv.shape, lambda t: (0, 0)),
            pl.BlockSpec((tm, LANES), tl.latent_pos),
            pl.BlockSpec((tm, LANES), tl.latent_pos),
        ] + [pl.BlockSpec((tm, d), tl.prompt_part) for _ in prev_flat],
        out_specs=(
            pl.BlockSpec((tm, d), tl.row),
            pl.BlockSpec((tm, d), tl.row),
            pl.BlockSpec((tm, d), tl.row),
            kv_spec, kv_spec,
        ) + ((row_out[1],) if pending else ()),
        compiler_params=_params(("arbitrary",)),
        name="ln_qkv_rope",
    )(*_stream_args(stream), mod, w_qkv, cos, sin, *prev_flat)


def _row_chains(n_rows):
    return [slice(r, r + CHAIN_ROWS) for r in range(0, n_rows, CHAIN_ROWS)]


def _chan_dft_kernel(*refs, n_prompt_tiles, stream_kind):
    n_x = _STREAM_REFS[stream_kind]
    x_refs, (mod_ref, dcs_ref, xc_ref, xs_ref) = refs[:n_x], refs[n_x:n_x + 4]
    g = FOURIER_GROUP_DIM
    is_prompt = pl.program_id(0) < n_prompt_tiles
    for rows in _row_chains(xc_ref.shape[0]):
        x = _stream_rows(stream_kind, x_refs, is_prompt, rows)
        if stream_kind == "pending":
            refs[-1][rows, :] = x
        h = (_layernorm(x) * (1.0 + mod_ref[1:2, :]) + mod_ref[0:1, :]).astype(BF16)
        for i in range(N_FOURIER_GROUPS):
            r = _dot(h[:, i * g:(i + 1) * g], dcs_ref[...])
            xc_ref[rows, i * g:(i + 1) * g] = r[:, :g].astype(BF16)
            xs_ref[rows, i * g:(i + 1) * g] = r[:, g:].astype(BF16)


def _chan_dft(stream, mod, dcs, n_prompt, dec_seq, t_tok):
    d = mod.shape[-1]
    tl = _Tiles(n_prompt, dec_seq, TOKEN_TILE)
    tm = tl.tm
    pending = stream[0] == "pending"
    half = jax.ShapeDtypeStruct((t_tok, d), BF16)
    return pl.pallas_call(
        functools.partial(_chan_dft_kernel, n_prompt_tiles=tl.npt, stream_kind=stream[0]),
        out_shape=(half, half) + ((jax.ShapeDtypeStruct((t_tok, d), F32),) if pending else ()),
        grid=(t_tok // tm,),
        in_specs=tl.stream_specs(stream, t_tok // tm) + [
            pl.BlockSpec((None, N_MOD, d), tl.cond),
            pl.BlockSpec(dcs.shape, lambda t: (0, 0)),
        ],
        out_specs=(pl.BlockSpec((tm, d), tl.row),) * (3 if pending else 2),
        compiler_params=_params(("arbitrary",)),
        name="ln_chan_dft",
    )(*_stream_args(stream), mod, dcs)


def _seq_dft_kernel(cs_ref, ss_ref, xc_ref, xs_ref, o_ref, *, norm):
    f = _dot(cs_ref[...], xc_ref[...]) - _dot(ss_ref[...], xs_ref[...])
    o_ref[...] = (f * norm).astype(BF16)


def _seq_dft(cs, ss, xc, xs, batch, seq, row_offset):
    d = xc.shape[1]
    tm = min(QKV_TILE, seq)
    spt = seq // tm
    off_seq = row_offset // seq
    return pl.pallas_call(
        functools.partial(_seq_dft_kernel, norm=1.0 / math.sqrt(seq * FOURIER_GROUP_DIM)),
        out_shape=jax.ShapeDtypeStruct((batch * seq, d), BF16),
        grid=(batch, spt),
        in_specs=[
            pl.BlockSpec((tm, seq), lambda b, i: (i, 0)),
            pl.BlockSpec((tm, seq), lambda b, i: (i, 0)),
            pl.BlockSpec((seq, d), lambda b, i: (off_seq + b, 0)),
            pl.BlockSpec((seq, d), lambda b, i: (off_seq + b, 0)),
        ],
        out_specs=pl.BlockSpec((tm, d), lambda b, i: (b * spt + i, 0)),
        compiler_params=_params(("arbitrary", "arbitrary")),
        name=f"seq_dft_{seq}",
    )(cs, ss, xc, xs)


def _seq_dft_mirror_kernel(cs_ref, ss_ref, xc_ref, xs_ref, lo_ref, hi_ref, *, norm):
    tm = lo_ref.shape[0]
    a = _dot(cs_ref[...], xc_ref[...])
    b = _dot(ss_ref[...], xs_ref[...])
    lo_ref[...] = ((a[:tm] - b[:tm]) * norm).astype(BF16)
    both = ((a + b) * norm).astype(BF16)
    r = lax.broadcasted_iota(I32, (tm, cs_ref.shape[0]), 0)
    c = lax.broadcasted_iota(I32, (tm, cs_ref.shape[0]), 1)
    flip = jnp.where(c == tm - r, 1.0, 0.0).astype(BF16)
    hi_ref[...] = _dot(flip, both).astype(BF16)


def _seq_dft_mirror(cs, ss, xc, xs, batch, seq, row_offset):
    d = xc.shape[1]
    tm = QKV_TILE
    half_tiles = seq // tm // 2
    off_seq = row_offset // seq
    half = jax.ShapeDtypeStruct((batch * seq // 2, d), BF16)
    return pl.pallas_call(
        functools.partial(_seq_dft_mirror_kernel, norm=1.0 / math.sqrt(seq * FOURIER_GROUP_DIM)),
        out_shape=(half, half),
        grid=(batch, half_tiles),
        in_specs=[
            pl.BlockSpec((pl.Element(tm + BF16_SUBLANES), pl.Element(seq)), lambda b, i: (i * tm, 0)),
            pl.BlockSpec((pl.Element(tm + BF16_SUBLANES), pl.Element(seq)), lambda b, i: (i * tm, 0)),
            pl.BlockSpec((seq, d), lambda b, i: (off_seq + b, 0)),
            pl.BlockSpec((seq, d), lambda b, i: (off_seq + b, 0)),
        ],
        out_specs=(pl.BlockSpec((tm, d), lambda b, i: (b * half_tiles + i, 0)),
                   pl.BlockSpec((tm, d), lambda b, i: (b * half_tiles + half_tiles - 1 - i, 0))),
        compiler_params=_params(("arbitrary", "arbitrary")),
        name=f"seq_dft_mirror_{seq}",
    )(cs, ss, xc, xs)


def _diff_lambda(lam_ref, lam_init):
    lv = lam_ref[...]
    return (jnp.exp(jnp.sum(lv[0:1] * lv[1:2], axis=-1, keepdims=True))
            - jnp.exp(jnp.sum(lv[2:3] * lv[3:4], axis=-1, keepdims=True)) + lam_init)


def _diff_attn_head(q, k, v_ext, lam, gain, lam_init):
    tq = q.shape[0]
    lane = lax.broadcasted_iota(I32, q.shape, 1)
    zero = jnp.zeros_like(q)
    qq = jnp.concatenate([jnp.where(lane < HEAD_DIM, q, zero), jnp.where(lane >= HEAD_DIM, q, zero)], axis=0)
    parts = []
    for r in range(0, 2 * tq, ATTN_ROW_CHUNK):
        s = lax.dot_general(qq[r:r + ATTN_ROW_CHUNK], k, (((1,), (1,)), ((), ())), preferred_element_type=F32)
        e = jnp.exp2(s - jnp.max(s, axis=-1, keepdims=True)).astype(BF16)
        parts.append(_dot(e, v_ext))
    oe = jnp.concatenate(parts, axis=0)
    o = oe[:, :V_DIM] / oe[:, V_DIM:]
    o = o[:tq] - lam * o[tq:]
    o = o * lax.rsqrt(jnp.mean(o * o, axis=-1, keepdims=True) + LN_EPS)
    return o * gain * (1.0 - lam_init)


def _attn_prompt_kernel(lam_ref, gain_ref, q_ref, k_ref, v_ref, o_ref, *, lam_init):
    lam = _diff_lambda(lam_ref, lam_init)
    gain = gain_ref[...]
    ones = jnp.ones((k_ref.shape[0], V_DIM), BF16)
    for hd in range(N_HEADS):
        cols = slice(hd * V_DIM, (hd + 1) * V_DIM)
        v_ext = jnp.concatenate([v_ref[:, cols], ones], axis=1)
        o_ref[:, cols] = _diff_attn_head(q_ref[:, cols], k_ref[:, cols], v_ext, lam, gain, lam_init).astype(BF16)


def _attn_latent_kernel(lam_ref, gain_ref, q_ref, k_ref, v_ref, kc_ref, vc_ref, o_ref, kall_ref, vext_ref, *,
                        lam_init):
    n_new = k_ref.shape[0]

    @pl.when(pl.program_id(2) == 0)
    def _():
        kall_ref[:n_new, :] = k_ref[...]
        kall_ref[n_new:, :] = kc_ref[...]
        vext_ref[:n_new, :V_DIM] = v_ref[...]
        vext_ref[n_new:, :V_DIM] = vc_ref[...]
        vext_ref[:, V_DIM:] = jnp.ones((vext_ref.shape[0], V_DIM), BF16)

    o = _diff_attn_head(q_ref[...], kall_ref[...], vext_ref[...], _diff_lambda(lam_ref, lam_init), gain_ref[...],
                        lam_init)
    o_ref[...] = o.astype(BF16)


def _attn_prompt(lam_vecs, gain, q, k, v, batch, seq, lam_init):
    d = q.shape[1]
    blk = pl.BlockSpec((seq, d), lambda b: (b, 0))
    return pl.pallas_call(
        functools.partial(_attn_prompt_kernel, lam_init=lam_init),
        out_shape=jax.ShapeDtypeStruct((batch * seq, d), BF16),
        grid=(batch,),
        in_specs=[
            pl.BlockSpec(lam_vecs.shape, lambda b: (0, 0)),
            pl.BlockSpec(gain.shape, lambda b: (0, 0)),
            blk, blk, blk,
        ],
        out_specs=blk,
        compiler_params=_params(("arbitrary",)),
        name="diff_attn_ctx",
    )(lam_vecs, gain, q, k, v)


def _attn_sample(lam_vecs, gain, q, k, v, kc, vc, batch, seq, n_ctx, row_offset, lam_init):
    d = q.shape[1]
    tq = Q_TILE
    qpt = seq // tq
    off_seq = row_offset // seq
    off_tile = row_offset // tq
    qmap = lambda b, h, i: (off_tile + b * qpt + i, h)
    kmap = lambda b, h, i: (off_seq + b, h)
    cmap = lambda b, h, i: (b, h)
    return pl.pallas_call(
        functools.partial(_attn_latent_kernel, lam_init=lam_init),
        out_shape=jax.ShapeDtypeStruct((batch * seq, d), BF16),
        grid=(batch, N_HEADS, qpt),
        in_specs=[
            pl.BlockSpec(lam_vecs.shape, lambda b, h, i: (0, 0)),
            pl.BlockSpec(gain.shape, lambda b, h, i: (0, 0)),
            pl.BlockSpec((tq, V_DIM), qmap),
            pl.BlockSpec((seq, V_DIM), kmap),
            pl.BlockSpec((seq, V_DIM), kmap),
            pl.BlockSpec((n_ctx, V_DIM), cmap),
            pl.BlockSpec((n_ctx, V_DIM), cmap),
        ],
        out_specs=pl.BlockSpec((tq, V_DIM), lambda b, h, i: (b * qpt + i, h)),
        scratch_shapes=[pltpu.VMEM((seq + n_ctx, V_DIM), BF16), pltpu.VMEM((seq + n_ctx, 2 * V_DIM), BF16)],
        compiler_params=_params(("arbitrary", "arbitrary", "arbitrary")),
        name="diff_attn_latent",
    )(lam_vecs, gain, q, k, v, kc, vc)


def _route(lg):
    lane = lax.broadcasted_iota(I32, lg.shape, 1)
    lane_f = lane.astype(F32)
    neg = jnp.float32(-jnp.inf)
    big = jnp.float32(LANES)
    gl = jnp.where(lane < N_EXPERT_GROUPS, lg, neg)
    gmax = jnp.max(gl, axis=-1, keepdims=True)
    g_prob = 1.0 / jnp.sum(jnp.exp(gl - gmax), axis=-1, keepdims=True)
    g_idx = jnp.min(jnp.where(gl == gmax, lane_f, big), axis=-1, keepdims=True)
    lo = N_EXPERT_GROUPS + EXPERTS_PER_GROUP * g_idx
    el = jnp.where((lane_f >= lo) & (lane_f < lo + EXPERTS_PER_GROUP), lg, neg)
    m1 = jnp.max(el, axis=-1, keepdims=True)
    i1 = jnp.min(jnp.where(el == m1, lane_f, big), axis=-1, keepdims=True)
    el2 = jnp.where(lane_f == i1, neg, el)
    m2 = jnp.max(el2, axis=-1, keepdims=True)
    i2 = jnp.min(jnp.where(el2 == m2, lane_f, big), axis=-1, keepdims=True)
    t = jnp.exp(m2 - m1)
    w1 = g_prob / (1.0 + t)
    w2 = g_prob * t / (1.0 + t)
    out = jnp.where(lane == 0, i1 - N_EXPERT_GROUPS, 0.0)
    out = jnp.where(lane == 1, i2 - N_EXPERT_GROUPS, out)
    out = jnp.where(lane == 2, w1, out)
    out = jnp.where(lane == 3, w2, out)
    return out


def _pack_bf16_pairs(h):
    n = h.shape[1] // 2
    bits = lax.bitcast_convert_type(h.astype(BF16).astype(F32), jnp.uint32)
    return (bits[:, :n] >> 16) | bits[:, n:]


def _unpack_bf16_pairs(p):
    lo = lax.bitcast_convert_type(p << 16, F32)
    hi = lax.bitcast_convert_type(p & jnp.uint32(0xFFFF0000), F32)
    return jnp.concatenate([lo, hi], axis=1)


def _mix_out_kernel(*refs, n_prompt_tiles, tiles_per_seq, n_mixed, stream_kind):
    n_x = _STREAM_REFS[stream_kind]
    a_refs, w_ref = refs[:n_mixed], refs[n_mixed]
    x_refs = refs[n_mixed + 1:n_mixed + 1 + n_x]
    mod_ref, gb_ref, wr_ref, x1_ref, hp_ref, route_ref = refs[n_mixed + 1 + n_x:]
    t = pl.program_id(0)
    is_prompt = t < n_prompt_tiles
    first_half = (t - n_prompt_tiles) % tiles_per_seq < tiles_per_seq // 2
    for rows in _row_chains(x1_ref.shape[0]):
        a = a_refs[1][rows, :] if n_mixed == 2 else jnp.where(first_half, a_refs[1][rows, :], a_refs[2][rows, :])
        a = jnp.where(is_prompt, a_refs[0][rows, :], a)
        out = _dot(a, w_ref[...])
        x = _stream_rows(stream_kind, x_refs, is_prompt, rows)
        x1 = _deepnorm(x, mod_ref[2:3, :], out) * gb_ref[0:1, :] + gb_ref[1:2, :]
        x1_ref[rows, :] = x1
        h2 = _layernorm(x1) * (1.0 + mod_ref[4:5, :]) + mod_ref[3:4, :]
        hp_ref[rows, :] = _pack_bf16_pairs(h2)
        route_ref[rows, :] = _route(_dot(h2.astype(BF16), wr_ref[...]))


def _mix_out(mixed, w, stream, mod, gb, wr, n_prompt, dec_seq, t_tok):
    d = w.shape[1]
    tl = _Tiles(n_prompt, dec_seq, TOKEN_TILE)
    tm = tl.tm
    if len(mixed) == 2:
        latent_specs = [pl.BlockSpec((tm, d), tl.latent_part)]
    else:
        half = tl.tps // 2

        def half_map(second):
            def index(t):
                lt = jnp.maximum(t - tl.npt, 0)
                within = lt % tl.tps - second * half
                return (lt // tl.tps * half + jnp.clip(within, 0, half - 1), 0)
            return index
        latent_specs = [pl.BlockSpec((tm, d), half_map(0)), pl.BlockSpec((tm, d), half_map(1))]
    return pl.pallas_call(
        functools.partial(_mix_out_kernel, n_prompt_tiles=tl.npt, tiles_per_seq=tl.tps, n_mixed=len(mixed),
                          stream_kind=stream[0]),
        out_shape=(
            jax.ShapeDtypeStruct((t_tok, d), F32),
            jax.ShapeDtypeStruct((t_tok, d // 2), jnp.uint32),
            jax.ShapeDtypeStruct((t_tok, LANES), F32),
        ),
        grid=(t_tok // tm,),
        in_specs=[pl.BlockSpec((tm, d), tl.prompt_part)] + latent_specs + [
            pl.BlockSpec(w.shape, lambda t: (0, 0)),
        ] + tl.stream_specs(stream, t_tok // tm) + [
            pl.BlockSpec((None, N_MOD, d), tl.cond),
            pl.BlockSpec(gb.shape, lambda t: (0, 0)),
            pl.BlockSpec(wr.shape, lambda t: (0, 0)),
        ],
        out_specs=(
            pl.BlockSpec((tm, d), tl.row),
            pl.BlockSpec((tm, d // 2), tl.row),
            pl.BlockSpec((tm, LANES), tl.row),
        ),
        compiler_params=_params(("arbitrary",)),
        name="mix_out_postnorm_router",
    )(*mixed, w, *_stream_args(stream), mod, gb, wr)


def _lane_prefix_sum(x, lane):
    sh = 1
    while sh < LANES:
        x = x + jnp.where(lane >= sh, pltpu.roll(x, sh, 1), 0.0)
        sh *= 2
    return x


def _lane_suffix_min_exclusive(x, lane):
    big = float(LANES)
    y = jnp.where(lane + 1 < LANES, pltpu.roll(x, LANES - 1, 1), big)
    sh = 1
    while sh < LANES:
        y = jnp.minimum(y, jnp.where(lane + sh < LANES, pltpu.roll(y, LANES - sh, 1), big))
        sh *= 2
    return y


def _work_items(counts, starts, ends, lane8, n_moe_tiles):
    shift = int(math.log2(MOE_TILE))
    first_tile = (starts.astype(I32) >> shift).astype(F32)
    last_tile = ((ends.astype(I32) - 1) >> shift).astype(F32)
    n_it = jnp.where(counts > 0.0, last_tile - first_tile + 1.0, 0.0)
    it_end = _lane_prefix_sum(n_it, lane8)
    it_start = it_end - n_it
    total = it_end[0:1, LANES - 1:LANES]
    sub = lax.broadcasted_iota(I32, (LANES, LANES), 0)
    rows = lambda x: jnp.broadcast_to(x[0:1, :], (LANES, LANES))
    used = counts > 0.0
    weight_slot = ((_lane_prefix_sum(jnp.where(used, 1.0, 0.0), lane8) - 1.0).astype(I32) & 1).astype(F32)
    next_used = _lane_suffix_min_exclusive(jnp.where(used, lane8.astype(F32), float(LANES)), lane8)
    per_expert = (starts, ends, first_tile, it_start, it_end, weight_slot, next_used)
    stacked = jnp.zeros((LANES, LANES), F32)
    for j, vec in enumerate(per_expert):
        stacked = jnp.where(sub == j, rows(vec), stacked)
    cols = stacked.T
    col = lambda j: cols[:, j:j + 1]
    sub_f = sub.astype(F32)
    w = lax.broadcasted_iota(I32, (LANES, LANES), 1).astype(F32)
    ex = jnp.sum(jnp.where((sub < N_EXPERTS) & (col(4) <= w), 1.0, 0.0), axis=0, keepdims=True)
    ex = jnp.minimum(ex, N_EXPERTS - 1.0)
    w1 = w[0:1, :]
    valid = w1 < total
    ex = jnp.where(valid, ex, jnp.max(jnp.where(valid, ex, 0.0), axis=-1, keepdims=True))
    onehot = sub_f == ex
    pick = lambda j: jnp.sum(jnp.where(onehot, col(j), 0.0), axis=0, keepdims=True)
    tile = jnp.where(valid, pick(2) + (w1 - pick(3)), n_moe_tiles - 1.0)
    lo = jnp.where(valid, jnp.maximum(pick(0), tile * MOE_TILE), 0.0)
    hi = jnp.where(valid, jnp.minimum(pick(1), (tile + 1.0) * MOE_TILE), 0.0)
    b8 = lambda x: jnp.broadcast_to(x, (SUBLANES, LANES))
    ex8, tile8 = b8(ex), b8(tile)
    first = jnp.where((lane8 == 0) | (tile8 != pltpu.roll(tile8, 1, 1)), 1.0, 0.0)
    newexp = jnp.where((lane8 == 0) | (ex8 != pltpu.roll(ex8, 1, 1)), 1.0, 0.0)
    sub8 = lax.broadcasted_iota(I32, (SUBLANES, LANES), 0)
    table = jnp.zeros((SUBLANES, LANES), F32)
    for j, vec in enumerate((ex8, tile8, b8(lo), b8(hi), first, newexp, b8(pick(5)), b8(pick(6)))):
        table = jnp.where(sub8 == j, vec, table)
    return table


def _plan_kernel(route_ref, pos_ref, items_ref, tri_ref, carry_ref, tot_ref, *, n_moe_tiles):
    p = pl.program_id(0)
    t = pl.program_id(1)
    tm = route_ref.shape[0]
    r = route_ref[...]
    lane = lax.broadcasted_iota(I32, (tm, LANES), 1)
    lane_f = lane.astype(F32)
    e0 = r[:, 0:1]
    e1 = r[:, 1:2] + N_EXPERTS
    m = jnp.where((lane_f == e0) | (lane_f == e1), 1.0, 0.0)
    colsum = jnp.sum(m, axis=0, keepdims=True)

    @pl.when((p == 0) & (t == 0))
    def _():
        carry_ref[...] = jnp.zeros_like(carry_ref)
        row = lax.broadcasted_iota(I32, (tm, tm), 0)
        col = lax.broadcasted_iota(I32, (tm, tm), 1)
        tri_ref[...] = jnp.where(row > col, 1.0, 0.0).astype(BF16)

    @pl.when(p == 0)
    def _():
        carry_ref[...] += colsum

    @pl.when((p == 0) & (t == pl.num_programs(1) - 1))
    def _():
        tot_ref[...] = carry_ref[...]
        carry_ref[...] = jnp.zeros_like(carry_ref)

    @pl.when(p == 1)
    def _():
        lane8 = lax.broadcasted_iota(I32, (SUBLANES, LANES), 1)
        tot = tot_ref[...]
        is_first = lane8 < N_EXPERTS
        tot0 = jnp.where(is_first, tot, 0.0)
        counts = jnp.where(is_first, tot + pltpu.roll(tot, LANES - N_EXPERTS, 1), 0.0)
        ends = _lane_prefix_sum(counts, lane8)
        starts = ends - counts
        base = jnp.where(is_first, starts, pltpu.roll(starts + tot0, N_EXPERTS, 1))
        before = _dot(tri_ref[...], m.astype(BF16)) + carry_ref[0:1, :]
        carry_ref[...] += colsum
        rows = before + base[0:1, :]
        pos0 = jnp.sum(jnp.where(lane_f == e0, rows, 0.0), axis=-1, keepdims=True)
        pos1 = jnp.sum(jnp.where(lane_f == e1, rows, 0.0), axis=-1, keepdims=True)
        both = jnp.where(lane == 0, pos0, jnp.where(lane == 1, pos1, 0.0))
        pos_ref[...] = both.T[0:SUBLANES, :].astype(I32)

        @pl.when(t == 0)
        def _():
            items_ref[...] = _work_items(counts, starts, ends, lane8, n_moe_tiles).astype(I32)


def _moe_plan(route):
    t_tok = route.shape[0]
    tm = PLAN_TILE
    n_moe_tiles = 2 * t_tok // MOE_TILE
    assert n_moe_tiles + N_EXPERTS - 1 <= LANES and 2 * N_EXPERTS <= LANES and MOE_TILE & (MOE_TILE - 1) == 0
    return pl.pallas_call(
        functools.partial(_plan_kernel, n_moe_tiles=n_moe_tiles),
        out_shape=(jax.ShapeDtypeStruct((SUBLANES, t_tok), I32), jax.ShapeDtypeStruct((SUBLANES, LANES), I32)),
        grid=(2, t_tok // tm),
        in_specs=[pl.BlockSpec((tm, LANES), lambda p, t: (t, 0))],
        out_specs=(pl.BlockSpec((SUBLANES, tm), lambda p, t: (0, t * p)),
                   pl.BlockSpec((SUBLANES, LANES), lambda p, t: (0, 0))),
        scratch_shapes=[pltpu.VMEM((tm, tm), BF16), pltpu.VMEM((SUBLANES, LANES), F32),
                        pltpu.VMEM((SUBLANES, LANES), F32)],
        compiler_params=_params(("arbitrary", "arbitrary")),
        name="moe_positions",
    )(route)


def _sc_workers():
    info = plsc.get_sparse_core_info()
    return info.num_cores, info.num_cores * info.num_subcores


def _sc_pipeline(n_chunks, loads, stores):
    def start(copies):
        for cp in copies:
            cp.start()

    def wait(copies):
        for cp in copies:
            cp.wait()

    start(loads(0, 0))
    for j in range(n_chunks):
        b = j % 2
        if j + 1 < n_chunks:
            if j >= 1:
                wait(stores(j - 1, 1 - b))
            start(loads(j + 1, 1 - b))
        wait(loads(j, b))
        start(stores(j, b))
    if n_chunks >= 2:
        wait(stores(n_chunks - 2, n_chunks % 2))
    wait(stores(n_chunks - 1, (n_chunks - 1) % 2))


def _sc_chunking(n_rows, d):
    n_cores, n_workers = _sc_workers()
    per_worker = n_rows // n_workers
    k = SC_BUFFER_BYTES // (d * 4)
    n_chunks = per_worker // k
    assert n_chunks * k * n_workers == n_rows and k <= SC_MAX_INDEX_CHUNK
    return n_cores, n_workers, per_worker, k, n_chunks


def _sc_gather_rows(table, idx):
    n, d = idx.shape[0], table.shape[1]
    n_cores, n_workers, per_worker, k, n_chunks = _sc_chunking(n, d)

    def body(table_hbm, idx_hbm, out_hbm, idx_v, rows_v, gsem, osem):
        wid = lax.axis_index("s") * n_cores + lax.axis_index("c")
        base = wid * per_worker
        pltpu.sync_copy(idx_hbm.at[wid], idx_v)
        _sc_pipeline(
            n_chunks,
            lambda j, b: [pltpu.make_async_copy(table_hbm.at[idx_v.at[j]], rows_v.at[b], gsem.at[b])],
            lambda j, b: [pltpu.make_async_copy(rows_v.at[b], out_hbm.at[pl.ds(base + j * k, k)], osem.at[b])])

    return pl.kernel(
        body,
        out_type=jax.ShapeDtypeStruct((n, d), table.dtype),
        mesh=plsc.VectorSubcoreMesh(core_axis_name="c", subcore_axis_name="s"),
        scratch_types=[pltpu.VMEM((n_chunks, k), I32), pltpu.VMEM((2, k, d), table.dtype),
                       pltpu.SemaphoreType.DMA((2,)), pltpu.SemaphoreType.DMA((2,))],
        name="sc_row_gather",
    )(table, idx.reshape(n_workers, n_chunks, k))


def _sc_scatter_rows(src, idx):
    n_lists, n_src = idx.shape
    d = src.shape[1]
    n_cores, n_workers, per_worker, k, n_chunks = _sc_chunking(n_src, d)

    def body(src_hbm, idx_hbm, out_hbm, idx_v, rows_v, gsem, osem):
        wid = lax.axis_index("s") * n_cores + lax.axis_index("c")
        base = wid * per_worker
        pltpu.sync_copy(idx_hbm.at[wid], idx_v)
        _sc_pipeline(
            n_chunks,
            lambda j, b: [pltpu.make_async_copy(src_hbm.at[pl.ds(base + j * k, k)], rows_v.at[b], gsem.at[b])],
            lambda j, b: [pltpu.make_async_copy(rows_v.at[b], out_hbm.at[idx_v.at[c, j]], osem.at[b, c])
                          for c in range(n_lists)])

    return pl.kernel(
        body,
        out_type=jax.ShapeDtypeStruct((n_lists * n_src, d), src.dtype),
        mesh=plsc.VectorSubcoreMesh(core_axis_name="c", subcore_axis_name="s"),
        scratch_types=[pltpu.VMEM((n_lists, n_chunks, k), I32), pltpu.VMEM((2, k, d), src.dtype),
                       pltpu.SemaphoreType.DMA((2,)), pltpu.SemaphoreType.DMA((2, n_lists))],
        name="sc_row_scatter",
    )(src, idx.reshape(n_lists, n_workers, n_chunks, k).transpose(1, 0, 2, 3))


IT_EXPERT, IT_TILE, IT_LO, IT_HI, IT_FIRST, IT_NEWEXP, IT_SLOT, IT_NEXT = range(8)


def _moe_kernel(it_ref, xs_ref, wg_hbm, wu_hbm, wd_hbm, o_ref, wg_f, wu_f, wd_f, sem, *, layer):
    w = pl.program_id(0)

    def weight_copies(expert, slot):
        return [pltpu.make_async_copy(src.at[layer, expert], dst.at[slot], sem.at[slot, j])
                for j, (src, dst) in enumerate(((wg_hbm, wg_f), (wu_hbm, wu_f), (wd_hbm, wd_f)))]

    @pl.when(it_ref[IT_NEWEXP, w] == 1)
    def _():
        expert = it_ref[IT_EXPERT, w]
        slot = it_ref[IT_SLOT, w]
        nxt = it_ref[IT_NEXT, w]

        @pl.when(w == 0)
        def _():
            for cp in weight_copies(expert, slot):
                cp.start()

        for cp in weight_copies(expert, slot):
            cp.wait()

        @pl.when(nxt < N_EXPERTS)
        def _():
            for cp in weight_copies(nxt, 1 - slot):
                cp.start()

    slot = it_ref[IT_SLOT, w]
    tm = xs_ref.shape[0]
    x = _unpack_bf16_pairs(xs_ref[...]).astype(BF16)
    a = _dot(x, wg_f[slot].astype(BF16))
    u = _dot(x, wu_f[slot].astype(BF16))
    act = (_silu(a) * u).astype(BF16)
    y = _pack_bf16_pairs(_dot(act, wd_f[slot].astype(BF16)))
    row = it_ref[IT_TILE, w] * tm + lax.broadcasted_iota(I32, (tm, 1), 0)
    mine = (row >= it_ref[IT_LO, w]) & (row < it_ref[IT_HI, w])

    @pl.when(it_ref[IT_FIRST, w] == 1)
    def _():
        o_ref[...] = jnp.where(mine, y, jnp.zeros_like(y))

    @pl.when(it_ref[IT_FIRST, w] == 0)
    def _():
        o_ref[...] = jnp.where(mine, y, o_ref[...])


def _moe_experts(items, xs, w_gate, w_up, w_down, layer):
    n_rows, dp = xs.shape
    _, _, d, f = w_gate.shape
    tm = MOE_TILE
    rmap = lambda w, it: (it[IT_TILE, w], 0)
    hbm = pl.BlockSpec(memory_space=pl.ANY)
    return pl.pallas_call(
        functools.partial(_moe_kernel, layer=layer),
        out_shape=jax.ShapeDtypeStruct((n_rows, d // 2), jnp.uint32),
        grid_spec=pltpu.PrefetchScalarGridSpec(
            num_scalar_prefetch=1,
            grid=(n_rows // tm + N_EXPERTS - 1,),
            in_specs=[pl.BlockSpec((tm, dp), rmap), hbm, hbm, hbm],
            out_specs=pl.BlockSpec((tm, d // 2), rmap),
            scratch_shapes=[
                pltpu.VMEM((2, d, f), F32), pltpu.VMEM((2, d, f), F32), pltpu.VMEM((2, f, d), F32),
                pltpu.SemaphoreType.DMA((2, 3)),
            ],
        ),
        compiler_params=_params(("arbitrary",)),
        name="moe_grouped_mlp",
    )(items, xs, w_gate, w_up, w_down)


def _final_postnorm_kernel(*refs, n_prompt_tiles):
    x2 = _finish_moe(*refs[:6])
    out_prompt, out_latent = refs[6:]

    @pl.when(pl.program_id(0) < n_prompt_tiles)
    def _():
        out_prompt[...] = x2

    @pl.when(pl.program_id(0) >= n_prompt_tiles)
    def _():
        out_latent[...] = x2


def _final_postnorm(stream, n_prompt, dec_seq, t_tok):
    d = stream[1][0].shape[1]
    tl = _Tiles(n_prompt, dec_seq, TOKEN_TILE)
    tm = tl.tm
    return pl.pallas_call(
        functools.partial(_final_postnorm_kernel, n_prompt_tiles=tl.npt),
        out_shape=(jax.ShapeDtypeStruct((n_prompt, d), F32), jax.ShapeDtypeStruct((t_tok - n_prompt, d), F32)),
        grid=(t_tok // tm,),
        in_specs=tl.stream_specs(stream, t_tok // tm),
        out_specs=(pl.BlockSpec((tm, d), tl.prompt_part), pl.BlockSpec((tm, d), tl.latent_part)),
        compiler_params=_params(("arbitrary",)),
        name="moe_combine_postnorm",
    )(*_stream_args(stream))


def _rope_tables(n_lat):
    nf = ROPE_FREQS
    s = np.arange(n_lat)
    lane = np.arange(LANES)
    inv = ROPE_BASE ** (-(lane % nf).astype(np.float64) / nf)
    use_col = (lane % HEAD_DIM) >= HEAD_DIM // 2
    p = np.where(use_col[None, :], (s % GRID_W)[:, None], (s // GRID_W)[:, None]).astype(np.float64)
    ang = p * inv[None, :]
    sign = np.where((lane % (2 * nf)) < nf, -1.0, 1.0)
    return jnp.asarray(np.cos(ang), F32), jnp.asarray(np.sin(ang) * sign[None, :], F32)


def _dft_tables(n):
    k = np.arange(n)
    ang = 2.0 * np.pi * ((k[:, None] * k[None, :]) % n).astype(np.float64) / n
    return np.cos(ang), np.sin(ang)


def kernel(x_prompt, x_sample, cache_k, cache_v, c, c_ctx, w_ada, b_ada, ln_gain, ln_bias, w_qkv, w_attn_out,
           lambda_q1, lambda_k1, lambda_q2, lambda_k2, subln_gain, w_fourier_out, w_router_group,
           w_router_expert, w_expert_gate, w_expert_up, w_expert_down):
    bp, sp, d = x_prompt.shape
    bs, n_lat, _ = x_sample.shape
    n_ctx = cache_k.shape[2]
    n_prompt = bp * sp
    t_tok = n_prompt + bs * n_lat
    assert d == D_MODEL and n_prompt % n_lat == 0 and n_lat % TOKEN_TILE == 0 and sp == QKV_TILE

    cond = jnp.concatenate([c_ctx[None, :], c, jnp.zeros((SUBLANES - 1 - bs, d), F32)], axis=0)
    mods = [_ada_layer(cond, w_ada, b_ada, i).reshape(SUBLANES, N_MOD, d) for i in range(DEPTH)]

    cos, sin = _rope_tables(n_lat)
    cc, sc = _dft_tables(FOURIER_GROUP_DIM)
    dcs = jnp.asarray(np.concatenate([cc, sc], axis=1), BF16)
    seq_tabs = {s: tuple(jnp.asarray(m, BF16) for m in _dft_tables(s)) for s in (sp, n_lat)}
    attn_layers = [i for i in range(DEPTH) if i % 2 == 0]

    stream = ("pair", (x_prompt.reshape(n_prompt, d), x_sample.reshape(bs * n_lat, d)))
    prev_kv = []
    new_k = new_v = None
    for i in range(DEPTH):
        mod = mods[i]
        if i % 2 == 0:
            a = i // 2
            lam_init = 0.8 - 0.6 * math.exp(-0.3 * i)
            last_attn = i == attn_layers[-1]
            q, k, v, kf, vf, *finished = _qkv(stream, mod, w_qkv[a].astype(BF16), cos, sin, prev_kv, last_attn,
                                              n_prompt, n_lat, t_tok)
            if last_attn:
                new_k, new_v = kf, vf
            else:
                prev_kv.append((kf, vf))
            lam_vecs = jnp.stack([lambda_q1[a], lambda_k1[a], lambda_q2[a], lambda_k2[a]], axis=0)
            gain = subln_gain[a][None, :]
            kc = cache_k[:, a].reshape(bs * n_ctx, d).astype(BF16)
            vc = cache_v[:, a].reshape(bs * n_ctx, d).astype(BF16)
            mixed = (_attn_prompt(lam_vecs, gain, q, k, v, bp, sp, lam_init),
                     _attn_sample(lam_vecs, gain, q, k, v, kc, vc, bs, n_lat, n_ctx, n_prompt, lam_init))
            w_mix = w_attn_out[a]
        else:
            xc, xsn, *finished = _chan_dft(stream, mod, dcs, n_prompt, n_lat, t_tok)
            mixed = (_seq_dft(*seq_tabs[sp], xc, xsn, bp, sp, 0),
                     *_seq_dft_mirror(*seq_tabs[n_lat], xc, xsn, bs, n_lat, n_prompt))
            w_mix = w_fourier_out[i // 2]
        if finished:
            stream = ("merged", (finished[0],))
        gb0 = jnp.stack([ln_gain[i, 0], ln_bias[i, 0]], axis=0)
        gb1 = jnp.stack([ln_gain[i, 1], ln_bias[i, 1]], axis=0)
        wr = jnp.concatenate([w_router_group[i], w_router_expert[i],
                              jnp.zeros((d, LANES - N_EXPERT_GROUPS - N_EXPERTS), F32)], axis=1)
        x1, hp, route = _mix_out(mixed, w_mix.astype(BF16), stream, mod, gb0, wr.astype(BF16), n_prompt, n_lat,
                                 t_tok)
        pos8, items = _moe_plan(route)
        pos = pos8[0:2]
        xs_sorted = _sc_scatter_rows(hp, pos)
        ys = _moe_experts(items, xs_sorted, w_expert_gate, w_expert_up, w_expert_down, i)
        yg = _sc_gather_rows(ys, pos.reshape(-1))
        stream = ("pending", (x1, yg, route, mods[i], gb1))

    out_prompt, out_latent = _final_postnorm(stream, n_prompt, n_lat, t_tok)
    y_prompt = out_prompt.reshape(bp, sp, d)
    y_sample = out_latent.reshape(bs, n_lat, d)
    return (y_prompt, y_sample, new_k.reshape(bp, len(attn_layers), sp, N_HEADS, 2 * HEAD_DIM),
            new_v.reshape(bp, len(attn_layers), sp, N_HEADS, V_DIM))
```
